```python
import jax, jax.numpy as jnp
from jax import lax
import numpy as np

D_MODEL = 1024
BATCH = 8
SEQ = 8192
DEPTH = 2

N_EVEN = (DEPTH + 1) // 2
N_ODD = DEPTH // 2
BLOCK = 128
D_FF = 2816
EPS = 1e-6
SB_HEADS = 8
SB_HEAD_DIM = 64
SB_WIDTH = SB_HEADS * SB_HEAD_DIM
SG_GROUPS = 8
SG_GROUP_DIM = 64
SG_WIDTH = SG_GROUPS * SG_GROUP_DIM
SG_CHUNK = 128
EVEN_IN = 3 * SB_WIDTH + 2 * SG_WIDTH
EVEN_MIX = SB_WIDTH + SG_WIDTH
MLA_HEADS = 16
MLA_NOPE = 64
MLA_ROPE = 32
MLA_QK = MLA_NOPE + MLA_ROPE
MLA_V = 64
MLA_Q_LORA = 512
MLA_KV_LORA = 256
MLA_IN = MLA_Q_LORA + MLA_KV_LORA + MLA_ROPE
MLA_MIX = MLA_HEADS * MLA_V
ROPE_THETA = 10000.0
MEM_TOKENS = 256
MEM_HEADS = 4
MEM_HEAD_DIM = D_MODEL // MEM_HEADS

kernel_name = 'hybrid_sb_gmlp_mla_macaron_trunk'


def _rmsnorm(x, g):
    xf = x.astype(jnp.float32)
    y = xf * lax.rsqrt(jnp.mean(xf * xf, axis=-1, keepdims=True) + EPS)
    return (y * g.astype(jnp.float32)).astype(x.dtype)


def _layernorm(x, g, b):
    xf = x.astype(jnp.float32)
    mu = jnp.mean(xf, axis=-1, keepdims=True)
    var = jnp.mean(jnp.square(xf - mu), axis=-1, keepdims=True)
    y = (xf - mu) * lax.rsqrt(var + EPS)
    return (y * g.astype(jnp.float32) + b.astype(jnp.float32)).astype(x.dtype)


def _swiglu(h, w_gu, w_down):
    gate, up = jnp.split(h @ w_gu, 2, axis=-1)
    return (jax.nn.silu(gate) * up) @ w_down


def _to_blocks(t):
    b, s, h, d = t.shape
    return t.reshape(b, s // BLOCK, BLOCK, h, d).transpose(1, 0, 2, 3, 4)


def _from_blocks(t):
    nb, b, l, h, d = t.shape
    return t.transpose(1, 0, 2, 3, 4).reshape(b, nb * l, h, d)


def _rope(t, positions):
    half = t.shape[-1] // 2
    inv_freq = ROPE_THETA ** (-jnp.arange(half, dtype=jnp.float32) / half)
    ang = positions.astype(jnp.float32)[:, :, None, None] * inv_freq
    cos, sin = jnp.cos(ang), jnp.sin(ang)
    tf = t.astype(jnp.float32)
    t1, t2 = tf[..., :half], tf[..., half:]
    return jnp.concatenate([t1 * cos - t2 * sin, t1 * sin + t2 * cos], axis=-1).astype(t.dtype)


def _stick_breaking_attention(q, k, v):
    s_len = q.shape[1]
    scale = SB_HEAD_DIM ** -0.5
    k_pos = jnp.arange(s_len)

    def block(args):
        q_blk, i = args
        z = jnp.einsum('bqhd,bkhd->bhqk', q_blk, k).astype(jnp.float32) * scale
        q_pos = i * BLOCK + jnp.arange(BLOCK)
        strict = (k_pos[None, :] < q_pos[:, None])[None, None]
        log_stay = jnp.where(strict, jax.nn.log_sigmoid(-z), 0.0)
        log_rest = lax.cumsum(log_stay, axis=3, reverse=True) - log_stay
        w = jnp.where(strict, jnp.exp(jax.nn.log_sigmoid(z) + log_rest), 0.0)
        return jnp.einsum('bhqk,bkhd->bqhd', w.astype(v.dtype), v)

    out = lax.map(block, (_to_blocks(q), jnp.arange(s_len // BLOCK)))
    return _from_blocks(out)


def _causal_softmax_attention(q, k, v):
    s_len = q.shape[1]
    scale = q.shape[-1] ** -0.5
    k_pos = jnp.arange(s_len)

    def block(args):
        q_blk, i = args
        sc = jnp.einsum('bqhd,bkhd->bhqk', q_blk, k).astype(jnp.float32) * scale
        q_pos = i * BLOCK + jnp.arange(BLOCK)
        causal = (k_pos[None, :] <= q_pos[:, None])[None, None]
        p = jax.nn.softmax(jnp.where(causal, sc, -jnp.inf), axis=-1)
        return jnp.einsum('bhqk,bkhd->bqhd', p.astype(v.dtype), v)

    out = lax.map(block, (_to_blocks(q), jnp.arange(s_len // BLOCK)))
    return _from_blocks(out)


def _even_mixer(h, w_in, ln_g, ln_b, sgu_w, sgu_b, w_out):
    b, s, _ = h.shape
    q, k, v, z = jnp.split(h @ w_in, [SB_WIDTH, 2 * SB_WIDTH, 3 * SB_WIDTH], axis=-1)
    heads = lambda t: t.reshape(b, s, SB_HEADS, SB_HEAD_DIM)
    o_sb = _stick_breaking_attention(heads(q), heads(k), heads(v)).reshape(b, s, SB_WIDTH)
    u, g = jnp.split(jax.nn.gelu(z), 2, axis=-1)
    g = _layernorm(g, ln_g, ln_b).reshape(b, s // SG_CHUNK, SG_CHUNK, SG_GROUPS, SG_GROUP_DIM)
    tri = jnp.tril(jnp.ones((SG_CHUNK, SG_CHUNK), dtype=sgu_w.dtype))
    mixed = jnp.einsum('gts,bcsgd->bctgd', sgu_w * tri, g) + sgu_b.T[None, None, :, :, None]
    o_sg = u * mixed.reshape(b, s, SG_WIDTH)
    return jnp.concatenate([o_sb, o_sg], axis=-1) @ w_out


def _mla_mixer(h, positions, w_in, q_lora_g, kv_lora_g, w_uq, w_ukv, q_g, k_g, w_out):
    b, s, _ = h.shape
    c_q, c_kv, k_r = jnp.split(h @ w_in, [MLA_Q_LORA, MLA_Q_LORA + MLA_KV_LORA], axis=-1)
    q = (_rmsnorm(c_q, q_lora_g) @ w_uq).reshape(b, s, MLA_HEADS, MLA_QK)
    kv = (_rmsnorm(c_kv, kv_lora_g) @ w_ukv).reshape(b, s, MLA_HEADS, MLA_NOPE + MLA_V)
    k_nope, v = kv[..., :MLA_NOPE], kv[..., MLA_NOPE:]
    k_r = jnp.broadcast_to(k_r[:, :, None, :], (b, s, MLA_HEADS, MLA_ROPE))
    k = jnp.concatenate([k_nope, k_r], axis=-1)
    q = _rmsnorm(q, q_g)
    k = _rmsnorm(k, k_g)
    q = jnp.concatenate([q[..., :MLA_NOPE], _rope(q[..., MLA_NOPE:], positions)], axis=-1)
    k = jnp.concatenate([k[..., :MLA_NOPE], _rope(k[..., MLA_NOPE:], positions)], axis=-1)
    o = _causal_softmax_attention(q, k, v).reshape(b, s, MLA_MIX)
    return o @ w_out


def _memory_cross_attention(hq, hm, wq, wkv, q_g, k_g, wo):
    b, s, _ = hq.shape
    m = hm.shape[1]
    q = _rmsnorm((hq @ wq).reshape(b, s, MEM_HEADS, MEM_HEAD_DIM), q_g)
    k, v = jnp.split((hm @ wkv).reshape(b, m, MEM_HEADS, 2 * MEM_HEAD_DIM), 2, axis=-1)
    k = _rmsnorm(k, k_g)
    sc = jnp.einsum('bqhd,bmhd->bhqm', q, k).astype(jnp.float32) * (MEM_HEAD_DIM ** -0.5)
    p = jax.nn.softmax(sc, axis=-1)
    o = jnp.einsum('bhqm,bmhd->bqhd', p.astype(v.dtype), v).reshape(b, s, D_MODEL)
    return o @ wo


def _w(k, shape, fan_in):
    return jax.random.normal(k, shape, jnp.float32) * (fan_in ** -0.5)


def _gain(k, shape):
    return 1.0 + 0.02 * jax.random.normal(k, shape, jnp.float32)


def _fwd_setup_inputs(seed: int = 0) -> dict:
    key = jax.random.key(seed)
    ks = list(jax.random.split(key, 32))
    nk = ks.pop
    inp = {}
    inp['x'] = jax.random.normal(nk(), (BATCH, SEQ, D_MODEL), jnp.float32)
    inp['mem'] = jax.random.normal(nk(), (BATCH, MEM_TOKENS, D_MODEL), jnp.float32)
    inp['positions'] = jnp.broadcast_to(jnp.arange(SEQ, dtype=jnp.int32)[None, :], (BATCH, SEQ))
    inp['ffn_pre_norm'] = _gain(nk(), (DEPTH, D_MODEL))
    inp['ffn_pre_w_gu'] = _w(nk(), (DEPTH, D_MODEL, 2 * D_FF), D_MODEL)
    inp['ffn_pre_w_down'] = _w(nk(), (DEPTH, D_FF, D_MODEL), D_FF)
    inp['mix_norm'] = _gain(nk(), (DEPTH, D_MODEL))
    inp['sbg_w_in'] = _w(nk(), (N_EVEN, D_MODEL, EVEN_IN), D_MODEL)
    inp['sgu_ln_gain'] = _gain(nk(), (N_EVEN, SG_WIDTH))
    inp['sgu_ln_bias'] = 0.02 * jax.random.normal(nk(), (N_EVEN, SG_WIDTH), jnp.float32)
    inp['sgu_w'] = _w(nk(), (N_EVEN, SG_GROUPS, SG_CHUNK, SG_CHUNK), SG_CHUNK)
    inp['sgu_b'] = 1.0 + 0.1 * jax.random.normal(nk(), (N_EVEN, SG_GROUPS, SG_CHUNK), jnp.float32)
    inp['sbg_w_out'] = _w(nk(), (N_EVEN, EVEN_MIX, D_MODEL), EVEN_MIX)
    inp['mla_w_in'] = _w(nk(), (N_ODD, D_MODEL, MLA_IN), D_MODEL)
    inp['mla_q_lora_gain'] = _gain(nk(), (N_ODD, MLA_Q_LORA))
    inp['mla_kv_lora_gain'] = _gain(nk(), (N_ODD, MLA_KV_LORA))
    inp['mla_w_uq'] = _w(nk(), (N_ODD, MLA_Q_LORA, MLA_HEADS * MLA_QK), MLA_Q_LORA)
    inp['mla_w_ukv'] = _w(nk(), (N_ODD, MLA_KV_LORA, MLA_HEADS * (MLA_NOPE + MLA_V)), MLA_KV_LORA)
    inp['mla_q_gain'] = _gain(nk(), (N_ODD, MLA_QK))
    inp['mla_k_gain'] = _gain(nk(), (N_ODD, MLA_QK))
    inp['mla_w_out'] = _w(nk(), (N_ODD, MLA_MIX, D_MODEL), MLA_MIX)
    inp['xmem_norm'] = _gain(nk(), (DEPTH, D_MODEL))
    inp['xmem_mem_norm'] = _gain(nk(), (DEPTH, D_MODEL))
    inp['xmem_wq'] = _w(nk(), (DEPTH, D_MODEL, D_MODEL), D_MODEL)
    inp['xmem_wkv'] = _w(nk(), (DEPTH, D_MODEL, 2 * D_MODEL), D_MODEL)
    inp['xmem_q_gain'] = _gain(nk(), (DEPTH, MEM_HEAD_DIM))
    inp['xmem_k_gain'] = _gain(nk(), (DEPTH, MEM_HEAD_DIM))
    inp['xmem_wo'] = _w(nk(), (DEPTH, D_MODEL, D_MODEL), D_MODEL)
    inp['ffn_post_norm'] = _gain(nk(), (DEPTH, D_MODEL))
    inp['ffn_post_w_gu'] = _w(nk(), (DEPTH, D_MODEL, 2 * D_FF), D_MODEL)
    inp['ffn_post_w_down'] = _w(nk(), (DEPTH, D_FF, D_MODEL), D_FF)
    return inp


def _fwd_reference(x, mem, positions,
              ffn_pre_norm, ffn_pre_w_gu, ffn_pre_w_down,
              mix_norm,
              sbg_w_in, sgu_ln_gain, sgu_ln_bias, sgu_w, sgu_b, sbg_w_out,
              mla_w_in, mla_q_lora_gain, mla_kv_lora_gain, mla_w_uq, mla_w_ukv,
              mla_q_gain, mla_k_gain, mla_w_out,
              xmem_norm, xmem_mem_norm, xmem_wq, xmem_wkv, xmem_q_gain, xmem_k_gain, xmem_wo,
              ffn_post_norm, ffn_post_w_gu, ffn_post_w_down):
    for layer in range(DEPTH):
        x = x + 0.5 * _swiglu(_rmsnorm(x, ffn_pre_norm[layer]),
                              ffn_pre_w_gu[layer], ffn_pre_w_down[layer])
        h = _rmsnorm(x, mix_norm[layer])
        if layer % 2 == 0:
            e = layer // 2
            x = x + _even_mixer(h, sbg_w_in[e], sgu_ln_gain[e], sgu_ln_bias[e],
                                sgu_w[e], sgu_b[e], sbg_w_out[e])
        else:
            o = layer // 2
            x = x + _mla_mixer(h, positions, mla_w_in[o], mla_q_lora_gain[o],
                               mla_kv_lora_gain[o], mla_w_uq[o], mla_w_ukv[o],
                               mla_q_gain[o], mla_k_gain[o], mla_w_out[o])
        x = x + _memory_cross_attention(_rmsnorm(x, xmem_norm[layer]),
                                        _rmsnorm(mem, xmem_mem_norm[layer]),
                                        xmem_wq[layer], xmem_wkv[layer],
                                        xmem_q_gain[layer], xmem_k_gain[layer], xmem_wo[layer])
        x = x + 0.5 * _swiglu(_rmsnorm(x, ffn_post_norm[layer]),
                              ffn_post_w_gu[layer], ffn_post_w_down[layer])
    return x


import jax as _jax
import jax.numpy as _jnp

TWIN_FORMAT = 'train_step'
FWD_PARAMS = ['x', 'mem', 'positions', 'ffn_pre_norm', 'ffn_pre_w_gu', 'ffn_pre_w_down', 'mix_norm', 'sbg_w_in', 'sgu_ln_gain', 'sgu_ln_bias', 'sgu_w', 'sgu_b', 'sbg_w_out', 'mla_w_in', 'mla_q_lora_gain', 'mla_kv_lora_gain', 'mla_w_uq', 'mla_w_ukv', 'mla_q_gain', 'mla_k_gain', 'mla_w_out', 'xmem_norm', 'xmem_mem_norm', 'xmem_wq', 'xmem_wkv', 'xmem_q_gain', 'xmem_k_gain', 'xmem_wo', 'ffn_post_norm', 'ffn_post_w_gu', 'ffn_post_w_down']
TWIN_WEIGHTS = ['ffn_pre_norm', 'ffn_pre_w_gu', 'ffn_pre_w_down', 'mix_norm', 'sbg_w_in', 'sgu_ln_gain', 'sgu_ln_bias', 'sgu_w', 'sgu_b', 'sbg_w_out', 'mla_w_in', 'mla_q_lora_gain', 'mla_kv_lora_gain', 'mla_w_uq', 'mla_w_ukv', 'mla_q_gain', 'mla_k_gain', 'mla_w_out', 'xmem_norm', 'xmem_mem_norm', 'xmem_wq', 'xmem_wkv', 'xmem_q_gain', 'xmem_k_gain', 'xmem_wo', 'ffn_post_norm', 'ffn_post_w_gu', 'ffn_post_w_down']
TWIN_DIFF_INPUT = 'x'
TWIN_INPUTS = ['x', 'mem', 'positions', 'ffn_pre_norm', 'ffn_pre_w_gu', 'ffn_pre_w_down', 'mix_norm', 'sbg_w_in', 'sgu_ln_gain', 'sgu_ln_bias', 'sgu_w', 'sgu_b', 'sbg_w_out', 'mla_w_in', 'mla_q_lora_gain', 'mla_kv_lora_gain', 'mla_w_uq', 'mla_w_ukv', 'mla_q_gain', 'mla_k_gain', 'mla_w_out', 'xmem_norm', 'xmem_mem_norm', 'xmem_wq', 'xmem_wkv', 'xmem_q_gain', 'xmem_k_gain', 'xmem_wo', 'ffn_post_norm', 'ffn_post_w_gu', 'ffn_post_w_down', 'loss_target', 'm_ffn_pre_norm', 'm_ffn_pre_w_gu', 'm_ffn_pre_w_down', 'm_mix_norm', 'm_sbg_w_in', 'm_sgu_ln_gain', 'm_sgu_ln_bias', 'm_sgu_w', 'm_sgu_b', 'm_sbg_w_out', 'm_mla_w_in', 'm_mla_q_lora_gain', 'm_mla_kv_lora_gain', 'm_mla_w_uq', 'm_mla_w_ukv', 'm_mla_q_gain', 'm_mla_k_gain', 'm_mla_w_out', 'm_xmem_norm', 'm_xmem_mem_norm', 'm_xmem_wq', 'm_xmem_wkv', 'm_xmem_q_gain', 'm_xmem_k_gain', 'm_xmem_wo', 'm_ffn_post_norm', 'm_ffn_post_w_gu', 'm_ffn_post_w_down', 'v_ffn_pre_norm', 'v_ffn_pre_w_gu', 'v_ffn_pre_w_down', 'v_mix_norm', 'v_sbg_w_in', 'v_sgu_ln_gain', 'v_sgu_ln_bias', 'v_sgu_w', 'v_sgu_b', 'v_sbg_w_out', 'v_mla_w_in', 'v_mla_q_lora_gain', 'v_mla_kv_lora_gain', 'v_mla_w_uq', 'v_mla_w_ukv', 'v_mla_q_gain', 'v_mla_k_gain', 'v_mla_w_out', 'v_xmem_norm', 'v_xmem_mem_norm', 'v_xmem_wq', 'v_xmem_wkv', 'v_xmem_q_gain', 'v_xmem_k_gain', 'v_xmem_wo', 'v_ffn_post_norm', 'v_ffn_post_w_gu', 'v_ffn_post_w_down']
TWIN_OUTPUTS = ['loss', 'grad_x', 'grad_ffn_pre_norm', 'grad_ffn_pre_w_gu', 'grad_ffn_pre_w_down', 'grad_mix_norm', 'grad_sbg_w_in', 'grad_sgu_ln_gain', 'grad_sgu_ln_bias', 'grad_sgu_w', 'grad_sgu_b', 'grad_sbg_w_out', 'grad_mla_w_in', 'grad_mla_q_lora_gain', 'grad_mla_kv_lora_gain', 'grad_mla_w_uq', 'grad_mla_w_ukv', 'grad_mla_q_gain', 'grad_mla_k_gain', 'grad_mla_w_out', 'grad_xmem_norm', 'grad_xmem_mem_norm', 'grad_xmem_wq', 'grad_xmem_wkv', 'grad_xmem_q_gain', 'grad_xmem_k_gain', 'grad_xmem_wo', 'grad_ffn_post_norm', 'grad_ffn_post_w_gu', 'grad_ffn_post_w_down', 'delta_ffn_pre_norm', 'delta_ffn_pre_w_gu', 'delta_ffn_pre_w_down', 'delta_mix_norm', 'delta_sbg_w_in', 'delta_sgu_ln_gain', 'delta_sgu_ln_bias', 'delta_sgu_w', 'delta_sgu_b', 'delta_sbg_w_out', 'delta_mla_w_in', 'delta_mla_q_lora_gain', 'delta_mla_kv_lora_gain', 'delta_mla_w_uq', 'delta_mla_w_ukv', 'delta_mla_q_gain', 'delta_mla_k_gain', 'delta_mla_w_out', 'delta_xmem_norm', 'delta_xmem_mem_norm', 'delta_xmem_wq', 'delta_xmem_wkv', 'delta_xmem_q_gain', 'delta_xmem_k_gain', 'delta_xmem_wo', 'delta_ffn_post_norm', 'delta_ffn_post_w_gu', 'delta_ffn_post_w_down', 'new_m_ffn_pre_norm', 'new_m_ffn_pre_w_gu', 'new_m_ffn_pre_w_down', 'new_m_mix_norm', 'new_m_sbg_w_in', 'new_m_sgu_ln_gain', 'new_m_sgu_ln_bias', 'new_m_sgu_w', 'new_m_sgu_b', 'new_m_sbg_w_out', 'new_m_mla_w_in', 'new_m_mla_q_lora_gain', 'new_m_mla_kv_lora_gain', 'new_m_mla_w_uq', 'new_m_mla_w_ukv', 'new_m_mla_q_gain', 'new_m_mla_k_gain', 'new_m_mla_w_out', 'new_m_xmem_norm', 'new_m_xmem_mem_norm', 'new_m_xmem_wq', 'new_m_xmem_wkv', 'new_m_xmem_q_gain', 'new_m_xmem_k_gain', 'new_m_xmem_wo', 'new_m_ffn_post_norm', 'new_m_ffn_post_w_gu', 'new_m_ffn_post_w_down', 'new_v_ffn_pre_norm', 'new_v_ffn_pre_w_gu', 'new_v_ffn_pre_w_down', 'new_v_mix_norm', 'new_v_sbg_w_in', 'new_v_sgu_ln_gain', 'new_v_sgu_ln_bias', 'new_v_sgu_w', 'new_v_sgu_b', 'new_v_sbg_w_out', 'new_v_mla_w_in', 'new_v_mla_q_lora_gain', 'new_v_mla_kv_lora_gain', 'new_v_mla_w_uq', 'new_v_mla_w_ukv', 'new_v_mla_q_gain', 'new_v_mla_k_gain', 'new_v_mla_w_out', 'new_v_xmem_norm', 'new_v_xmem_mem_norm', 'new_v_xmem_wq', 'new_v_xmem_wkv', 'new_v_xmem_q_gain', 'new_v_xmem_k_gain', 'new_v_xmem_wo', 'new_v_ffn_post_norm', 'new_v_ffn_post_w_gu', 'new_v_ffn_post_w_down']
TWIN_LEAF_KINDS = {'loss': 'loss', 'grad_x': 'grad_x', 'grad_ffn_pre_norm': 'grad_w', 'grad_ffn_pre_w_gu': 'grad_w', 'grad_ffn_pre_w_down': 'grad_w', 'grad_mix_norm': 'grad_w', 'grad_sbg_w_in': 'grad_w', 'grad_sgu_ln_gain': 'grad_w', 'grad_sgu_ln_bias': 'grad_w', 'grad_sgu_w': 'grad_w', 'grad_sgu_b': 'grad_w', 'grad_sbg_w_out': 'grad_w', 'grad_mla_w_in': 'grad_w', 'grad_mla_q_lora_gain': 'grad_w', 'grad_mla_kv_lora_gain': 'grad_w', 'grad_mla_w_uq': 'grad_w', 'grad_mla_w_ukv': 'grad_w', 'grad_mla_q_gain': 'grad_w', 'grad_mla_k_gain': 'grad_w', 'grad_mla_w_out': 'grad_w', 'grad_xmem_norm': 'grad_w', 'grad_xmem_mem_norm': 'grad_w', 'grad_xmem_wq': 'grad_w', 'grad_xmem_wkv': 'grad_w', 'grad_xmem_q_gain': 'grad_w', 'grad_xmem_k_gain': 'grad_w', 'grad_xmem_wo': 'grad_w', 'grad_ffn_post_norm': 'grad_w', 'grad_ffn_post_w_gu': 'grad_w', 'grad_ffn_post_w_down': 'grad_w', 'delta_ffn_pre_norm': 'delta_w', 'delta_ffn_pre_w_gu': 'delta_w', 'delta_ffn_pre_w_down': 'delta_w', 'delta_mix_norm': 'delta_w', 'delta_sbg_w_in': 'delta_w', 'delta_sgu_ln_gain': 'delta_w', 'delta_sgu_ln_bias': 'delta_w', 'delta_sgu_w': 'delta_w', 'delta_sgu_b': 'delta_w', 'delta_sbg_w_out': 'delta_w', 'delta_mla_w_in': 'delta_w', 'delta_mla_q_lora_gain': 'delta_w', 'delta_mla_kv_lora_gain': 'delta_w', 'delta_mla_w_uq': 'delta_w', 'delta_mla_w_ukv': 'delta_w', 'delta_mla_q_gain': 'delta_w', 'delta_mla_k_gain': 'delta_w', 'delta_mla_w_out': 'delta_w', 'delta_xmem_norm': 'delta_w', 'delta_xmem_mem_norm': 'delta_w', 'delta_xmem_wq': 'delta_w', 'delta_xmem_wkv': 'delta_w', 'delta_xmem_q_gain': 'delta_w', 'delta_xmem_k_gain': 'delta_w', 'delta_xmem_wo': 'delta_w', 'delta_ffn_post_norm': 'delta_w', 'delta_ffn_post_w_gu': 'delta_w', 'delta_ffn_post_w_down': 'delta_w', 'new_m_ffn_pre_norm': 'new_m', 'new_m_ffn_pre_w_gu': 'new_m', 'new_m_ffn_pre_w_down': 'new_m', 'new_m_mix_norm': 'new_m', 'new_m_sbg_w_in': 'new_m', 'new_m_sgu_ln_gain': 'new_m', 'new_m_sgu_ln_bias': 'new_m', 'new_m_sgu_w': 'new_m', 'new_m_sgu_b': 'new_m', 'new_m_sbg_w_out': 'new_m', 'new_m_mla_w_in': 'new_m', 'new_m_mla_q_lora_gain': 'new_m', 'new_m_mla_kv_lora_gain': 'new_m', 'new_m_mla_w_uq': 'new_m', 'new_m_mla_w_ukv': 'new_m', 'new_m_mla_q_gain': 'new_m', 'new_m_mla_k_gain': 'new_m', 'new_m_mla_w_out': 'new_m', 'new_m_xmem_norm': 'new_m', 'new_m_xmem_mem_norm': 'new_m', 'new_m_xmem_wq': 'new_m', 'new_m_xmem_wkv': 'new_m', 'new_m_xmem_q_gain': 'new_m', 'new_m_xmem_k_gain': 'new_m', 'new_m_xmem_wo': 'new_m', 'new_m_ffn_post_norm': 'new_m', 'new_m_ffn_post_w_gu': 'new_m', 'new_m_ffn_post_w_down': 'new_m', 'new_v_ffn_pre_norm': 'new_v', 'new_v_ffn_pre_w_gu': 'new_v', 'new_v_ffn_pre_w_down': 'new_v', 'new_v_mix_norm': 'new_v', 'new_v_sbg_w_in': 'new_v', 'new_v_sgu_ln_gain': 'new_v', 'new_v_sgu_ln_bias': 'new_v', 'new_v_sgu_w': 'new_v', 'new_v_sgu_b': 'new_v', 'new_v_sbg_w_out': 'new_v', 'new_v_mla_w_in': 'new_v', 'new_v_mla_q_lora_gain': 'new_v', 'new_v_mla_kv_lora_gain': 'new_v', 'new_v_mla_w_uq': 'new_v', 'new_v_mla_w_ukv': 'new_v', 'new_v_mla_q_gain': 'new_v', 'new_v_mla_k_gain': 'new_v', 'new_v_mla_w_out': 'new_v', 'new_v_xmem_norm': 'new_v', 'new_v_xmem_mem_norm': 'new_v', 'new_v_xmem_wq': 'new_v', 'new_v_xmem_wkv': 'new_v', 'new_v_xmem_q_gain': 'new_v', 'new_v_xmem_k_gain': 'new_v', 'new_v_xmem_wo': 'new_v', 'new_v_ffn_post_norm': 'new_v', 'new_v_ffn_post_w_gu': 'new_v', 'new_v_ffn_post_w_down': 'new_v'}


def _forward(args):
    return _fwd_reference(*[args[k] for k in FWD_PARAMS])


def _output_shape():
    def fwd():
        inp = _fwd_setup_inputs(0)
        return _fwd_reference(*[inp[k] for k in FWD_PARAMS])
    out = _jax.eval_shape(fwd)
    return out.shape, out.dtype

N_MICROBATCH = 1
ADAM_LR = 0.001
ADAM_B1 = 0.9
ADAM_B2 = 0.999
ADAM_EPS = 1e-08
ADAM_WD = 0.01
ADAM_STEP = 10
PER_EXAMPLE_BATCH_AXIS = {'x': 0, 'mem': 0, 'positions': 0, 'loss_target': 0}
SHARED_INPUTS = []
_WEIGHT_DTYPES = {'ffn_pre_norm': _jnp.float32, 'ffn_pre_w_gu': _jnp.float32, 'ffn_pre_w_down': _jnp.float32, 'mix_norm': _jnp.float32, 'sbg_w_in': _jnp.float32, 'sgu_ln_gain': _jnp.float32, 'sgu_ln_bias': _jnp.float32, 'sgu_w': _jnp.float32, 'sgu_b': _jnp.float32, 'sbg_w_out': _jnp.float32, 'mla_w_in': _jnp.float32, 'mla_q_lora_gain': _jnp.float32, 'mla_kv_lora_gain': _jnp.float32, 'mla_w_uq': _jnp.float32, 'mla_w_ukv': _jnp.float32, 'mla_q_gain': _jnp.float32, 'mla_k_gain': _jnp.float32, 'mla_w_out': _jnp.float32, 'xmem_norm': _jnp.float32, 'xmem_mem_norm': _jnp.float32, 'xmem_wq': _jnp.float32, 'xmem_wkv': _jnp.float32, 'xmem_q_gain': _jnp.float32, 'xmem_k_gain': _jnp.float32, 'xmem_wo': _jnp.float32, 'ffn_post_norm': _jnp.float32, 'ffn_post_w_gu': _jnp.float32, 'ffn_post_w_down': _jnp.float32}
MOMENT_SCALE = {'ffn_pre_norm': 1.221803e+01, 'ffn_pre_w_gu': 2.136957e-01, 'ffn_pre_w_down': 3.752223e-01, 'mix_norm': 2.972993e+01, 'sbg_w_in': 7.129344e-01, 'sgu_ln_gain': 1.340296e+01, 'sgu_ln_bias': 8.103738e-01, 'sgu_w': 6.052612e-01, 'sgu_b': 1.444409e+01, 'sbg_w_out': 5.519516e+00, 'mla_w_in': 3.736894e+00, 'mla_q_lora_gain': 1.522261e-01, 'mla_kv_lora_gain': 6.890709e+00, 'mla_w_uq': 8.274505e-02, 'mla_w_ukv': 1.591969e+00, 'mla_q_gain': 3.176301e+00, 'mla_k_gain': 3.174001e+00, 'mla_w_out': 1.974499e+00, 'xmem_norm': 1.289197e-01, 'xmem_mem_norm': 9.428768e-01, 'xmem_wq': 1.263677e-01, 'xmem_wkv': 3.565373e-01, 'xmem_q_gain': 2.578955e+00, 'xmem_k_gain': 2.571577e+00, 'xmem_wo': 4.936548e-01, 'ffn_post_norm': 1.239407e+01, 'ffn_post_w_gu': 2.483762e-01, 'ffn_post_w_down': 4.104379e-01}


def _to_microbatches(a, axis):
    t = _jnp.moveaxis(a, axis, 0)
    t = t.reshape((N_MICROBATCH, t.shape[0] // N_MICROBATCH) + t.shape[1:])
    return _jnp.moveaxis(t, 1, axis + 1)


def setup_inputs(seed: int = 0) -> dict:
    inp = _fwd_setup_inputs(seed)
    key = _jax.random.fold_in(_jax.random.key(seed), 7919)
    shape, _ = _output_shape()
    out = dict(inp)
    out["loss_target"] = _jax.random.normal(_jax.random.fold_in(key, 0), shape, _jnp.float32)
    for i, name in enumerate(TWIN_WEIGHTS):
        w = inp[name].astype(_jnp.float32)
        if MOMENT_SCALE is None:
            s = _jnp.sqrt(_jnp.mean(_jnp.square(w)) + 1e-30)
        else:
            s = MOMENT_SCALE[name]
        km, kv = _jax.random.split(_jax.random.fold_in(key, i + 1))
        out[name] = w
        out["m_" + name] = s * _jax.random.normal(km, w.shape, _jnp.float32)
        out["v_" + name] = (s * s) * _jax.random.uniform(kv, w.shape, _jnp.float32, 0.5, 1.5)
    if N_MICROBATCH > 1:
        for name, axis in PER_EXAMPLE_BATCH_AXIS.items():
            out[name] = _to_microbatches(out[name], axis)
    return {'x': out['x'], 'mem': out['mem'], 'positions': out['positions'], 'ffn_pre_norm': out['ffn_pre_norm'], 'ffn_pre_w_gu': out['ffn_pre_w_gu'], 'ffn_pre_w_down': out['ffn_pre_w_down'], 'mix_norm': out['mix_norm'], 'sbg_w_in': out['sbg_w_in'], 'sgu_ln_gain': out['sgu_ln_gain'], 'sgu_ln_bias': out['sgu_ln_bias'], 'sgu_w': out['sgu_w'], 'sgu_b': out['sgu_b'], 'sbg_w_out': out['sbg_w_out'], 'mla_w_in': out['mla_w_in'], 'mla_q_lora_gain': out['mla_q_lora_gain'], 'mla_kv_lora_gain': out['mla_kv_lora_gain'], 'mla_w_uq': out['mla_w_uq'], 'mla_w_ukv': out['mla_w_ukv'], 'mla_q_gain': out['mla_q_gain'], 'mla_k_gain': out['mla_k_gain'], 'mla_w_out': out['mla_w_out'], 'xmem_norm': out['xmem_norm'], 'xmem_mem_norm': out['xmem_mem_norm'], 'xmem_wq': out['xmem_wq'], 'xmem_wkv': out['xmem_wkv'], 'xmem_q_gain': out['xmem_q_gain'], 'xmem_k_gain': out['xmem_k_gain'], 'xmem_wo': out['xmem_wo'], 'ffn_post_norm': out['ffn_post_norm'], 'ffn_post_w_gu': out['ffn_post_w_gu'], 'ffn_post_w_down': out['ffn_post_w_down'], 'loss_target': out['loss_target'], 'm_ffn_pre_norm': out['m_ffn_pre_norm'], 'm_ffn_pre_w_gu': out['m_ffn_pre_w_gu'], 'm_ffn_pre_w_down': out['m_ffn_pre_w_down'], 'm_mix_norm': out['m_mix_norm'], 'm_sbg_w_in': out['m_sbg_w_in'], 'm_sgu_ln_gain': out['m_sgu_ln_gain'], 'm_sgu_ln_bias': out['m_sgu_ln_bias'], 'm_sgu_w': out['m_sgu_w'], 'm_sgu_b': out['m_sgu_b'], 'm_sbg_w_out': out['m_sbg_w_out'], 'm_mla_w_in': out['m_mla_w_in'], 'm_mla_q_lora_gain': out['m_mla_q_lora_gain'], 'm_mla_kv_lora_gain': out['m_mla_kv_lora_gain'], 'm_mla_w_uq': out['m_mla_w_uq'], 'm_mla_w_ukv': out['m_mla_w_ukv'], 'm_mla_q_gain': out['m_mla_q_gain'], 'm_mla_k_gain': out['m_mla_k_gain'], 'm_mla_w_out': out['m_mla_w_out'], 'm_xmem_norm': out['m_xmem_norm'], 'm_xmem_mem_norm': out['m_xmem_mem_norm'], 'm_xmem_wq': out['m_xmem_wq'], 'm_xmem_wkv': out['m_xmem_wkv'], 'm_xmem_q_gain': out['m_xmem_q_gain'], 'm_xmem_k_gain': out['m_xmem_k_gain'], 'm_xmem_wo': out['m_xmem_wo'], 'm_ffn_post_norm': out['m_ffn_post_norm'], 'm_ffn_post_w_gu': out['m_ffn_post_w_gu'], 'm_ffn_post_w_down': out['m_ffn_post_w_down'], 'v_ffn_pre_norm': out['v_ffn_pre_norm'], 'v_ffn_pre_w_gu': out['v_ffn_pre_w_gu'], 'v_ffn_pre_w_down': out['v_ffn_pre_w_down'], 'v_mix_norm': out['v_mix_norm'], 'v_sbg_w_in': out['v_sbg_w_in'], 'v_sgu_ln_gain': out['v_sgu_ln_gain'], 'v_sgu_ln_bias': out['v_sgu_ln_bias'], 'v_sgu_w': out['v_sgu_w'], 'v_sgu_b': out['v_sgu_b'], 'v_sbg_w_out': out['v_sbg_w_out'], 'v_mla_w_in': out['v_mla_w_in'], 'v_mla_q_lora_gain': out['v_mla_q_lora_gain'], 'v_mla_kv_lora_gain': out['v_mla_kv_lora_gain'], 'v_mla_w_uq': out['v_mla_w_uq'], 'v_mla_w_ukv': out['v_mla_w_ukv'], 'v_mla_q_gain': out['v_mla_q_gain'], 'v_mla_k_gain': out['v_mla_k_gain'], 'v_mla_w_out': out['v_mla_w_out'], 'v_xmem_norm': out['v_xmem_norm'], 'v_xmem_mem_norm': out['v_xmem_mem_norm'], 'v_xmem_wq': out['v_xmem_wq'], 'v_xmem_wkv': out['v_xmem_wkv'], 'v_xmem_q_gain': out['v_xmem_q_gain'], 'v_xmem_k_gain': out['v_xmem_k_gain'], 'v_xmem_wo': out['v_xmem_wo'], 'v_ffn_post_norm': out['v_ffn_post_norm'], 'v_ffn_post_w_gu': out['v_ffn_post_w_gu'], 'v_ffn_post_w_down': out['v_ffn_post_w_down']}


def _loss(weights, diff, rest, loss_target):
    with _jax.named_scope("forward"):
        args = {**rest, TWIN_DIFF_INPUT: diff, **{k: w.astype(_WEIGHT_DTYPES[k]) for k, w in weights.items()}}
        y = _forward(args)
    with _jax.named_scope("loss_head"):
        err = _jnp.square(y.astype(_jnp.float32) - loss_target)
        return 0.5 * _jnp.sum(_jnp.mean(err, axis=-1)) if err.ndim else 0.5 * err


def _adamw(w, g, m, v):
    m = ADAM_B1 * m + (1.0 - ADAM_B1) * g
    v = ADAM_B2 * v + (1.0 - ADAM_B2) * _jnp.square(g)
    m_hat = m / (1.0 - ADAM_B1 ** ADAM_STEP)
    v_hat = v / (1.0 - ADAM_B2 ** ADAM_STEP)
    delta = -ADAM_LR * (m_hat / (_jnp.sqrt(v_hat) + ADAM_EPS) + ADAM_WD * w)
    return delta, m, v


def reference(x, mem, positions, ffn_pre_norm, ffn_pre_w_gu, ffn_pre_w_down, mix_norm, sbg_w_in, sgu_ln_gain, sgu_ln_bias, sgu_w, sgu_b, sbg_w_out, mla_w_in, mla_q_lora_gain, mla_kv_lora_gain, mla_w_uq, mla_w_ukv, mla_q_gain, mla_k_gain, mla_w_out, xmem_norm, xmem_mem_norm, xmem_wq, xmem_wkv, xmem_q_gain, xmem_k_gain, xmem_wo, ffn_post_norm, ffn_post_w_gu, ffn_post_w_down, loss_target, m_ffn_pre_norm, m_ffn_pre_w_gu, m_ffn_pre_w_down, m_mix_norm, m_sbg_w_in, m_sgu_ln_gain, m_sgu_ln_bias, m_sgu_w, m_sgu_b, m_sbg_w_out, m_mla_w_in, m_mla_q_lora_gain, m_mla_kv_lora_gain, m_mla_w_uq, m_mla_w_ukv, m_mla_q_gain, m_mla_k_gain, m_mla_w_out, m_xmem_norm, m_xmem_mem_norm, m_xmem_wq, m_xmem_wkv, m_xmem_q_gain, m_xmem_k_gain, m_xmem_wo, m_ffn_post_norm, m_ffn_post_w_gu, m_ffn_post_w_down, v_ffn_pre_norm, v_ffn_pre_w_gu, v_ffn_pre_w_down, v_mix_norm, v_sbg_w_in, v_sgu_ln_gain, v_sgu_ln_bias, v_sgu_w, v_sgu_b, v_sbg_w_out, v_mla_w_in, v_mla_q_lora_gain, v_mla_kv_lora_gain, v_mla_w_uq, v_mla_w_ukv, v_mla_q_gain, v_mla_k_gain, v_mla_w_out, v_xmem_norm, v_xmem_mem_norm, v_xmem_wq, v_xmem_wkv, v_xmem_q_gain, v_xmem_k_gain, v_xmem_wo, v_ffn_post_norm, v_ffn_post_w_gu, v_ffn_post_w_down):
    given = dict(x=x, mem=mem, positions=positions, ffn_pre_norm=ffn_pre_norm, ffn_pre_w_gu=ffn_pre_w_gu, ffn_pre_w_down=ffn_pre_w_down, mix_norm=mix_norm, sbg_w_in=sbg_w_in, sgu_ln_gain=sgu_ln_gain, sgu_ln_bias=sgu_ln_bias, sgu_w=sgu_w, sgu_b=sgu_b, sbg_w_out=sbg_w_out, mla_w_in=mla_w_in, mla_q_lora_gain=mla_q_lora_gain, mla_kv_lora_gain=mla_kv_lora_gain, mla_w_uq=mla_w_uq, mla_w_ukv=mla_w_ukv, mla_q_gain=mla_q_gain, mla_k_gain=mla_k_gain, mla_w_out=mla_w_out, xmem_norm=xmem_norm, xmem_mem_norm=xmem_mem_norm, xmem_wq=xmem_wq, xmem_wkv=xmem_wkv, xmem_q_gain=xmem_q_gain, xmem_k_gain=xmem_k_gain, xmem_wo=xmem_wo, ffn_post_norm=ffn_post_norm, ffn_post_w_gu=ffn_post_w_gu, ffn_post_w_down=ffn_post_w_down, loss_target=loss_target, m_ffn_pre_norm=m_ffn_pre_norm, m_ffn_pre_w_gu=m_ffn_pre_w_gu, m_ffn_pre_w_down=m_ffn_pre_w_down, m_mix_norm=m_mix_norm, m_sbg_w_in=m_sbg_w_in, m_sgu_ln_gain=m_sgu_ln_gain, m_sgu_ln_bias=m_sgu_ln_bias, m_sgu_w=m_sgu_w, m_sgu_b=m_sgu_b, m_sbg_w_out=m_sbg_w_out, m_mla_w_in=m_mla_w_in, m_mla_q_lora_gain=m_mla_q_lora_gain, m_mla_kv_lora_gain=m_mla_kv_lora_gain, m_mla_w_uq=m_mla_w_uq, m_mla_w_ukv=m_mla_w_ukv, m_mla_q_gain=m_mla_q_gain, m_mla_k_gain=m_mla_k_gain, m_mla_w_out=m_mla_w_out, m_xmem_norm=m_xmem_norm, m_xmem_mem_norm=m_xmem_mem_norm, m_xmem_wq=m_xmem_wq, m_xmem_wkv=m_xmem_wkv, m_xmem_q_gain=m_xmem_q_gain, m_xmem_k_gain=m_xmem_k_gain, m_xmem_wo=m_xmem_wo, m_ffn_post_norm=m_ffn_post_norm, m_ffn_post_w_gu=m_ffn_post_w_gu, m_ffn_post_w_down=m_ffn_post_w_down, v_ffn_pre_norm=v_ffn_pre_norm, v_ffn_pre_w_gu=v_ffn_pre_w_gu, v_ffn_pre_w_down=v_ffn_pre_w_down, v_mix_norm=v_mix_norm, v_sbg_w_in=v_sbg_w_in, v_sgu_ln_gain=v_sgu_ln_gain, v_sgu_ln_bias=v_sgu_ln_bias, v_sgu_w=v_sgu_w, v_sgu_b=v_sgu_b, v_sbg_w_out=v_sbg_w_out, v_mla_w_in=v_mla_w_in, v_mla_q_lora_gain=v_mla_q_lora_gain, v_mla_kv_lora_gain=v_mla_kv_lora_gain, v_mla_w_uq=v_mla_w_uq, v_mla_w_ukv=v_mla_w_ukv, v_mla_q_gain=v_mla_q_gain, v_mla_k_gain=v_mla_k_gain, v_mla_w_out=v_mla_w_out, v_xmem_norm=v_xmem_norm, v_xmem_mem_norm=v_xmem_mem_norm, v_xmem_wq=v_xmem_wq, v_xmem_wkv=v_xmem_wkv, v_xmem_q_gain=v_xmem_q_gain, v_xmem_k_gain=v_xmem_k_gain, v_xmem_wo=v_xmem_wo, v_ffn_post_norm=v_ffn_post_norm, v_ffn_post_w_gu=v_ffn_post_w_gu, v_ffn_post_w_down=v_ffn_post_w_down)
    weights = {n: given[n] for n in TWIN_WEIGHTS}
    shared = {n: given[n] for n in SHARED_INPUTS}
    per_example = {n: given[n] for n in ['x', 'mem', 'positions']}
    grad_fn = _jax.value_and_grad(_loss, argnums=(0, 1))

    def one_microbatch(ex, loss_target):
        ex = dict(ex)
        diff = ex.pop(TWIN_DIFF_INPUT)
        return grad_fn(weights, diff, {**shared, **ex}, loss_target)

    if N_MICROBATCH == 1:
        loss, (grad_w, grad_x) = one_microbatch(per_example, given["loss_target"])
    else:
        def body(carry, xs):
            loss_sum, grad_sum = carry
            l_k, (gw_k, gx_k) = one_microbatch(xs[0], xs[1])
            with _jax.named_scope("update"):
                return (loss_sum + l_k, _jax.tree.map(_jnp.add, grad_sum, gw_k)), gx_k

        init = (_jnp.zeros((), _jnp.float32), _jax.tree.map(_jnp.zeros_like, weights))
        (loss, grad_w), grad_x = _jax.lax.scan(body, init, (per_example, given["loss_target"]))
    with _jax.named_scope("update"):
        delta_w, new_m, new_v = {}, {}, {}
        for n in TWIN_WEIGHTS:
            delta_w[n], new_m[n], new_v[n] = _adamw(weights[n], grad_w[n], given["m_" + n], given["v_" + n])
    return (loss, grad_x, *[grad_w[n] for n in TWIN_WEIGHTS], *[delta_w[n] for n in TWIN_WEIGHTS],
            *[new_m[n] for n in TWIN_WEIGHTS], *[new_v[n] for n in TWIN_WEIGHTS])
```

```python
import functools

import jax
import jax.numpy as jnp
from jax import lax
from jax.experimental import pallas as pl
from jax.experimental.pallas import tpu as pltpu

F32 = jnp.float32
BF16 = jnp.bfloat16
MESH = pl.DeviceIdType.MESH
N_DEV = 8

VMEM_LIMIT_BYTES = 56 * 1024 * 1024
LANES = 128

D_MODEL = 1024
DEPTH = 2
D_FF = 2816
EPS = 1e-6
SB_HEADS, SB_HEAD_DIM = 8, 64
SB_WIDTH = SB_HEADS * SB_HEAD_DIM
SG_GROUPS, SG_GROUP_DIM, SG_CHUNK = 8, 64, 128
SG_WIDTH = SG_GROUPS * SG_GROUP_DIM
MLA_HEADS, MLA_NOPE, MLA_ROPE, MLA_V = 16, 64, 32, 64
MLA_QK = MLA_NOPE + MLA_ROPE
MLA_Q_LORA, MLA_KV_LORA = 512, 256
ROPE_THETA = 10000.0
MEM_HEADS = 4
MEM_HEAD_DIM = D_MODEL // MEM_HEADS

ADAM_LR, ADAM_B1, ADAM_B2, ADAM_EPS, ADAM_WD, ADAM_STEP = 0.001, 0.9, 0.999, 1e-08, 0.01, 10

WEIGHTS = ['ffn_pre_norm', 'ffn_pre_w_gu', 'ffn_pre_w_down', 'mix_norm', 'sbg_w_in', 'sgu_ln_gain', 'sgu_ln_bias',
           'sgu_w', 'sgu_b', 'sbg_w_out', 'mla_w_in', 'mla_q_lora_gain', 'mla_kv_lora_gain', 'mla_w_uq', 'mla_w_ukv',
           'mla_q_gain', 'mla_k_gain', 'mla_w_out', 'xmem_norm', 'xmem_mem_norm', 'xmem_wq', 'xmem_wkv',
           'xmem_q_gain', 'xmem_k_gain', 'xmem_wo', 'ffn_post_norm', 'ffn_post_w_gu', 'ffn_post_w_down']
BIG = {'ffn_pre_w_gu': 2, 'ffn_pre_w_down': 1, 'sbg_w_in': 2, 'sbg_w_out': 1, 'mla_w_in': 1, 'mla_w_uq': 2,
       'mla_w_ukv': 2, 'mla_w_out': 1, 'xmem_wq': 1, 'xmem_wkv': 2, 'xmem_wo': 1, 'ffn_post_w_gu': 2,
       'ffn_post_w_down': 1}
GAIN_SHARDED = ('mla_q_lora_gain', 'mla_kv_lora_gain')
SMALL = [n for n in WEIGHTS if n not in BIG]
FLAT_W = 1024
BIG_ROW_TILE = 256
SMALL_ROW_TILE = 8


def _cparams(sem=None):
    return pltpu.CompilerParams(dimension_semantics=sem, vmem_limit_bytes=VMEM_LIMIT_BYTES)


def _pick(dim, prefs=(512, 256, 128)):
    for p in prefs:
        if dim % p == 0:
            return p
    return dim


def pmm(a, b, *, ta=False, tb=False, out_dtype=F32, res=None, alpha=1.0, name):
    kdim, m = (a.shape if ta else a.shape[::-1])
    n, kdim2 = (b.shape if tb else b.shape[::-1])
    assert kdim == kdim2, (a.shape, b.shape, ta, tb)
    tm, tn, tk = _pick(m), _pick(n), _pick(kdim)
    nk = kdim // tk
    dims = (((0 if ta else 1,), (1 if tb else 0,)), ((), ()))

    def body(*refs):
        if res is None:
            a_ref, b_ref, o_ref, acc_ref = refs
        else:
            a_ref, b_ref, r_ref, o_ref, acc_ref = refs
        k = pl.program_id(2)

        @pl.when(k == 0)
        def _():
            acc_ref[...] = jnp.zeros_like(acc_ref)

        acc_ref[...] += lax.dot_general(a_ref[...].astype(BF16), b_ref[...].astype(BF16), dims,
                                        preferred_element_type=F32)

        @pl.when(k == nk - 1)
        def _():
            r = acc_ref[...]
            if alpha != 1.0:
                r = r * alpha
            if res is not None:
                r = r_ref[...] + r
            o_ref[...] = r.astype(out_dtype)

    a_spec = pl.BlockSpec((tk, tm), lambda i, j, k: (k, i)) if ta else pl.BlockSpec((tm, tk), lambda i, j, k: (i, k))
    b_spec = pl.BlockSpec((tn, tk), lambda i, j, k: (j, k)) if tb else pl.BlockSpec((tk, tn), lambda i, j, k: (k, j))
    o_spec = pl.BlockSpec((tm, tn), lambda i, j, k: (i, j))
    ins, in_specs = [a, b], [a_spec, b_spec]
    if res is not None:
        ins.append(res)
        in_specs.append(o_spec)
    return pl.pallas_call(
        body, name=name, grid=(m // tm, n // tn, nk), in_specs=in_specs, out_specs=o_spec,
        out_shape=jax.ShapeDtypeStruct((m, n), out_dtype), scratch_shapes=[pltpu.VMEM((tm, tn), F32)],
        compiler_params=_cparams(("parallel", "parallel", "arbitrary")),
    )(*ins)


def _dg(a, b, ca, cb):
    return lax.dot_general(a, b, (((ca,), (cb,)), ((), ())), preferred_element_type=F32)


@jax.custom_vjp
def bdot(a, b):
    return _dg(a.astype(BF16), b.astype(BF16), 1, 0)


def _bdot_fwd(a, b):
    ab, bb = a.astype(BF16), b.astype(BF16)
    return _dg(ab, bb, 1, 0), (ab, bb)


def _bdot_bwd(saved, g):
    ab, bb = saved
    gb = g.astype(BF16)
    return _dg(gb, bb, 1, 1), _dg(ab, gb, 0, 0)


bdot.defvjp(_bdot_fwd, _bdot_bwd)


@jax.custom_vjp
def bdot_nt(a, b):
    return _dg(a.astype(BF16), b.astype(BF16), 1, 1)


def _bdot_nt_fwd(a, b):
    ab, bb = a.astype(BF16), b.astype(BF16)
    return _dg(ab, bb, 1, 1), (ab, bb)


def _bdot_nt_bwd(saved, g):
    ab, bb = saved
    gb = g.astype(BF16)
    return _dg(gb, bb, 1, 0), _dg(gb, ab, 0, 0)


bdot_nt.defvjp(_bdot_nt_fwd, _bdot_nt_bwd)


def _row_spec(arr, tm):
    if arr.ndim == 3:
        return pl.BlockSpec((None, tm, arr.shape[2]), lambda r, g: (g, r, 0))
    return pl.BlockSpec((tm, arr.shape[1]), lambda r, g: (r, 0))


def _gparam_spec(arr):
    return pl.BlockSpec((None,) + arr.shape[1:], lambda r, g: (g, 0, 0))


def _whole_spec(arr):
    nd = arr.ndim
    return pl.BlockSpec(arr.shape, lambda r, g: (0,) * nd)


def _groups(rows, gparams):
    gs = {a.shape[0] for a in rows if a.ndim == 3} | {a.shape[0] for a in gparams}
    assert len(gs) <= 1
    return gs.pop() if gs else 1


def prow(fn, rows, gparams=(), params=(), *, outs, tm, name):
    rows, gparams, params = list(rows), list(gparams), list(params)
    n_groups = _groups(rows, gparams)
    n_rows = rows[0].shape[-2]
    n_in = len(rows) + len(gparams) + len(params)

    def body(*refs):
        vals = [r[...] for r in refs[:n_in]]
        res = fn(*vals)
        for o_ref, r in zip(refs[n_in:], res, strict=True):
            o_ref[...] = r.astype(o_ref.dtype)

    out_shape, out_specs = [], []
    for width, dtype, grouped in outs:
        shp = (n_groups, n_rows, width) if grouped else (n_rows, width)
        out_shape.append(jax.ShapeDtypeStruct(shp, dtype))
        out_specs.append(_row_spec(out_shape[-1], tm))
    return pl.pallas_call(
        body, name=name, grid=(n_rows // tm, n_groups),
        in_specs=[_row_spec(a, tm) for a in rows] + [_gparam_spec(a) for a in gparams] + [_whole_spec(a) for a in params],
        out_specs=out_specs, out_shape=out_shape,
        compiler_params=_cparams(("parallel", "arbitrary")),
    )(*rows, *gparams, *params)


def prow_vjp(fn, rows, gparams=(), params=(), *, cts, row_grad, adds=None, row_dtypes=None, gparam_grad=None,
             param_grad=None, tm, name):
    rows, gparams, params, cts = list(rows), list(gparams), list(params), list(cts)
    gparam_grad = list(gparam_grad) if gparam_grad is not None else [True] * len(gparams)
    param_grad = list(param_grad) if param_grad is not None else [True] * len(params)
    n_groups = _groups(rows + cts, gparams)
    n_rows = rows[0].shape[-2]
    want_rows = [i for i, w in enumerate(row_grad) if w]
    adds = list(adds) if adds is not None else [None] * len(want_rows)
    row_dtypes = list(row_dtypes) if row_dtypes is not None else [F32] * len(want_rows)
    add_arrays = [a for a in adds if a is not None]
    n_r, n_g, n_p, n_c, n_a = len(rows), len(gparams), len(params), len(cts), len(add_arrays)
    mask = list(row_grad) + gparam_grad + param_grad

    def body(*refs):
        r_id, g_id = pl.program_id(0), pl.program_id(1)
        n_in = n_r + n_g + n_p
        vals = [r[...] for r in refs[:n_in]]
        ct_vals = tuple(r[...].astype(F32) for r in refs[n_in:n_in + n_c])
        add_refs = list(refs[n_in + n_c:n_in + n_c + n_a])
        out_refs = list(refs[n_in + n_c + n_a:])
        diff_idx = [i for i, w in enumerate(mask) if w]

        def wrapped(*diff):
            full = list(vals)
            for i, d in zip(diff_idx, diff):
                full[i] = d
            return tuple(fn(*full))

        _, pull = jax.vjp(wrapped, *[vals[i].astype(F32) for i in diff_idx])
        grads = dict(zip(diff_idx, pull(ct_vals)))
        k = 0
        for j, i in enumerate(want_rows):
            o_ref = out_refs[k]
            k += 1
            gval = grads[i]
            if adds[j] is not None:
                gval = gval + add_refs.pop(0)[...].astype(F32)
            if rows[i].ndim == 2 and n_groups > 1:
                @pl.when(g_id == 0)
                def _(o_ref=o_ref, gval=gval):
                    o_ref[...] = gval.astype(o_ref.dtype)

                @pl.when(g_id != 0)
                def _(o_ref=o_ref, gval=gval):
                    o_ref[...] += gval.astype(o_ref.dtype)
            else:
                o_ref[...] = gval.astype(o_ref.dtype)
        for i in range(n_g):
            if not gparam_grad[i]:
                continue
            o_ref = out_refs[k]
            k += 1
            gval = grads[n_r + i]

            @pl.when(r_id == 0)
            def _(o_ref=o_ref, gval=gval):
                o_ref[g_id] = gval

            @pl.when(r_id != 0)
            def _(o_ref=o_ref, gval=gval):
                o_ref[g_id] += gval
        for i in range(n_p):
            if not param_grad[i]:
                continue
            o_ref = out_refs[k]
            k += 1
            gval = grads[n_r + n_g + i]
            first = jnp.logical_and(r_id == 0, g_id == 0)

            @pl.when(first)
            def _(o_ref=o_ref, gval=gval):
                o_ref[...] = gval

            @pl.when(jnp.logical_not(first))
            def _(o_ref=o_ref, gval=gval):
                o_ref[...] += gval

    out_shape, out_specs = [], []
    for j, i in enumerate(want_rows):
        out_shape.append(jax.ShapeDtypeStruct(rows[i].shape, row_dtypes[j]))
        out_specs.append(_row_spec(rows[i], tm))
    for i in range(n_g):
        if gparam_grad[i]:
            out_shape.append(jax.ShapeDtypeStruct(gparams[i].shape, F32))
            out_specs.append(_whole_spec(gparams[i]))
    for i in range(n_p):
        if param_grad[i]:
            out_shape.append(jax.ShapeDtypeStruct(params[i].shape, F32))
            out_specs.append(_whole_spec(params[i]))
    return pl.pallas_call(
        body, name=name, grid=(n_rows // tm, n_groups),
        in_specs=([_row_spec(a, tm) for a in rows] + [_gparam_spec(a) for a in gparams]
                  + [_whole_spec(a) for a in params] + [_row_spec(a, tm) for a in cts]
                  + [_row_spec(a, tm) for a in add_arrays]),
        out_specs=out_specs, out_shape=out_shape,
        compiler_params=_cparams(("arbitrary", "arbitrary")),
    )(*rows, *gparams, *params, *cts, *add_arrays)


def f_rms(x, g):
    xf = x.astype(F32)
    return (xf * lax.rsqrt(jnp.mean(xf * xf, axis=-1, keepdims=True) + EPS) * g,)


def f_swiglu_act(gu):
    return (jax.nn.silu(gu[:, :D_FF]) * gu[:, D_FF:],)


def f_gate_prep(z, ln_g, ln_b):
    act = jax.nn.gelu(z)
    u, gg = act[:, :SG_WIDTH], act[:, SG_WIDTH:]
    mu = jnp.mean(gg, axis=-1, keepdims=True)
    var = jnp.mean(jnp.square(gg - mu), axis=-1, keepdims=True)
    return u, (gg - mu) * lax.rsqrt(var + EPS) * ln_g + ln_b


def f_spatial_gate(gn, u, w, b):
    t = lax.broadcasted_iota(jnp.int32, w.shape, 0)
    s = lax.broadcasted_iota(jnp.int32, w.shape, 1)
    mixed = bdot(jnp.where(s <= t, w, 0.0), gn) + b
    return (u * mixed,)


def _rope_tail(t, cos, sin):
    half = MLA_ROPE // 2
    t1, t2 = t[:, MLA_NOPE:MLA_NOPE + half], t[:, MLA_NOPE + half:]
    return jnp.concatenate([t[:, :MLA_NOPE], t1 * cos - t2 * sin, t1 * sin + t2 * cos], axis=-1)


def f_mla_q(q, cos, sin, g):
    return (_rope_tail(f_rms(q, g)[0], cos, sin),)


def f_mla_k(k_nope, k_r, cos, sin, g):
    return (_rope_tail(f_rms(jnp.concatenate([k_nope, k_r], axis=-1), g)[0], cos, sin),)


def f_xattn(q, k, v, q_g, k_g):
    qn, kn = f_rms(q, q_g)[0], f_rms(k, k_g)[0]
    sc = bdot_nt(qn, kn) * (MEM_HEAD_DIM ** -0.5)
    return (bdot(jax.nn.softmax(sc, axis=-1), v),)


ATT_TQ = 256
ATT_TK = 128


def _split_dot(x, tri):
    hi = x.astype(BF16)
    lo = (x - hi.astype(F32)).astype(BF16)
    return _dg(hi, tri, 1, 0) + _dg(lo, tri, 1, 0)


def _tri(cmp):
    j = lax.broadcasted_iota(jnp.int32, (ATT_TK, ATT_TK), 0)
    s = lax.broadcasted_iota(jnp.int32, (ATT_TK, ATT_TK), 1)
    return cmp(j, s).astype(BF16)


def _att_specs(s_len, dq, dv):
    q_spec = pl.BlockSpec((None, ATT_TQ, dq), lambda h, i: (h, i, 0))
    k_spec = pl.BlockSpec((None, s_len, dq), lambda h, i: (h, 0, 0))
    v_spec = pl.BlockSpec((None, s_len, dv), lambda h, i: (h, 0, 0))
    o_spec = pl.BlockSpec((None, ATT_TQ, dv), lambda h, i: (h, i, 0))
    r_spec = pl.BlockSpec((None, ATT_TQ, 1), lambda h, i: (h, i, 0))
    return q_spec, k_spec, v_spec, o_spec, r_spec


def _positions(qi, j):
    row = qi * ATT_TQ + lax.broadcasted_iota(jnp.int32, (ATT_TQ, ATT_TK), 0)
    col = j * ATT_TK + lax.broadcasted_iota(jnp.int32, (ATT_TQ, ATT_TK), 1)
    return row, col


def _log_sigmoid(z):
    return jnp.minimum(z, 0.0) - jnp.log1p(jnp.exp(-jnp.abs(z)))


def sb_fwd(q, k, v, *, name):
    n_heads, s_len, d = q.shape
    scale = SB_HEAD_DIM ** -0.5
    blocks_per_q = ATT_TQ // ATT_TK

    def body(q_ref, k_ref, v_ref, o_ref, tot_ref):
        qi = pl.program_id(1)
        qv = q_ref[...]
        upper = _tri(lambda j, s: j > s)

        def step(jj, carry):
            acc, rest = carry
            j = (qi + 1) * blocks_per_q - 1 - jj
            sl = pl.ds(pl.multiple_of(j * ATT_TK, ATT_TK), ATT_TK)
            ks, vs = k_ref[sl, :], v_ref[sl, :]
            z = _dg(qv, ks, 1, 1) * scale
            row, col = _positions(qi, j)
            valid = col < row
            log_beta = _log_sigmoid(z)
            log_stay = jnp.where(valid, log_beta - z, 0.0)
            log_rest = _split_dot(log_stay, upper) + rest
            w = jnp.where(valid, jnp.exp(log_beta + log_rest), 0.0)
            acc = acc + _dg(w.astype(BF16), vs, 1, 0)
            return acc, rest + jnp.sum(log_stay, axis=1, keepdims=True)

        acc, rest = lax.fori_loop(0, (qi + 1) * blocks_per_q, step,
                                  (jnp.zeros((ATT_TQ, d), F32), jnp.zeros((ATT_TQ, 1), F32)))
        o_ref[...] = acc
        tot_ref[...] = rest

    q_spec, k_spec, v_spec, o_spec, r_spec = _att_specs(s_len, d, d)
    return pl.pallas_call(
        body, name=name, grid=(n_heads, s_len // ATT_TQ), in_specs=[q_spec, k_spec, v_spec],
        out_specs=[o_spec, r_spec],
        out_shape=[jax.ShapeDtypeStruct((n_heads, s_len, d), F32), jax.ShapeDtypeStruct((n_heads, s_len, 1), F32)],
        compiler_params=_cparams(("parallel", "arbitrary")),
    )(q, k, v)


def sb_bwd(q, k, v, tot, do, *, name):
    n_heads, s_len, d = q.shape
    scale = SB_HEAD_DIM ** -0.5
    blocks_per_q = ATT_TQ // ATT_TK

    def body(q_ref, k_ref, v_ref, tot_ref, do_ref, dq_ref, dk_ref, dv_ref):
        qi = pl.program_id(1)

        @pl.when(qi == 0)
        def _():
            dk_ref[...] = jnp.zeros_like(dk_ref)
            dv_ref[...] = jnp.zeros_like(dv_ref)

        qv = q_ref[...]
        dob = do_ref[...].astype(BF16)
        total = tot_ref[...]
        incl = _tri(lambda j, s: j <= s)
        excl = _tri(lambda j, s: j < s)

        def step(j, carry):
            dq, stay_before, dl_before = carry
            sl = pl.ds(pl.multiple_of(j * ATT_TK, ATT_TK), ATT_TK)
            ks, vs = k_ref[sl, :], v_ref[sl, :]
            z = _dg(qv, ks, 1, 1) * scale
            row, col = _positions(qi, j)
            valid = col < row
            log_beta = _log_sigmoid(z)
            log_stay = jnp.where(valid, log_beta - z, 0.0)
            log_rest = (total - stay_before) - _split_dot(log_stay, incl)
            w = jnp.where(valid, jnp.exp(log_beta + log_rest), 0.0)
            dl = _dg(dob, vs, 1, 1) * w
            dl_prefix = _split_dot(dl, excl) + dl_before
            beta = jnp.exp(log_beta)
            dz = jnp.where(valid, dl * (1.0 - beta) - beta * dl_prefix, 0.0) * scale
            dzb = dz.astype(BF16)
            dq = dq + _dg(dzb, ks, 1, 0)
            dk_ref[sl, :] += _dg(dzb, qv, 0, 0)
            dv_ref[sl, :] += _dg(w.astype(BF16), dob, 0, 0)
            return (dq, stay_before + jnp.sum(log_stay, axis=1, keepdims=True),
                    dl_before + jnp.sum(dl, axis=1, keepdims=True))

        zero = jnp.zeros((ATT_TQ, 1), F32)
        dq, _, _ = lax.fori_loop(0, (qi + 1) * blocks_per_q, step, (jnp.zeros((ATT_TQ, d), F32), zero, zero))
        dq_ref[...] = dq

    q_spec, k_spec, v_spec, o_spec, r_spec = _att_specs(s_len, d, d)
    shp = jax.ShapeDtypeStruct((n_heads, s_len, d), F32)
    return pl.pallas_call(
        body, name=name, grid=(n_heads, s_len // ATT_TQ), in_specs=[q_spec, k_spec, v_spec, r_spec, o_spec],
        out_specs=[q_spec, k_spec, v_spec], out_shape=[shp, shp, shp],
        compiler_params=_cparams(("arbitrary", "arbitrary")),
    )(q, k, v, tot, do)


NEG_BIG = -1e30


def sm_fwd(q, k, v, *, name):
    n_heads, s_len, dq = q.shape
    dv = v.shape[2]
    scale = dq ** -0.5
    blocks_per_q = ATT_TQ // ATT_TK

    def body(q_ref, k_ref, v_ref, o_ref, lse_ref):
        qi = pl.program_id(1)
        qv = q_ref[...]

        def step(j, carry):
            acc, m, l = carry
            sl = pl.ds(pl.multiple_of(j * ATT_TK, ATT_TK), ATT_TK)
            ks, vs = k_ref[sl, :], v_ref[sl, :]
            row, col = _positions(qi, j)
            sc = jnp.where(col <= row, _dg(qv, ks, 1, 1) * scale, NEG_BIG)
            m_new = jnp.maximum(m, jnp.max(sc, axis=1, keepdims=True))
            p = jnp.exp(sc - m_new)
            fade = jnp.exp(m - m_new)
            return (fade * acc + _dg(p.astype(BF16), vs, 1, 0), m_new,
                    fade * l + jnp.sum(p, axis=1, keepdims=True))

        acc, m, l = lax.fori_loop(0, (qi + 1) * blocks_per_q, step,
                                  (jnp.zeros((ATT_TQ, dv), F32), jnp.full((ATT_TQ, 1), NEG_BIG, F32),
                                   jnp.zeros((ATT_TQ, 1), F32)))
        o_ref[...] = acc / l
        lse_ref[...] = m + jnp.log(l)

    q_spec, k_spec, v_spec, o_spec, r_spec = _att_specs(s_len, dq, dv)
    return pl.pallas_call(
        body, name=name, grid=(n_heads, s_len // ATT_TQ), in_specs=[q_spec, k_spec, v_spec],
        out_specs=[o_spec, r_spec],
        out_shape=[jax.ShapeDtypeStruct((n_heads, s_len, dv), F32), jax.ShapeDtypeStruct((n_heads, s_len, 1), F32)],
        compiler_params=_cparams(("parallel", "arbitrary")),
    )(q, k, v)


def sm_bwd(q, k, v, o, lse, do, *, name):
    n_heads, s_len, dq = q.shape
    dv = v.shape[2]
    scale = dq ** -0.5
    blocks_per_q = ATT_TQ // ATT_TK

    def body(q_ref, k_ref, v_ref, o_ref, lse_ref, do_ref, dq_ref, dk_ref, dv_ref):
        qi = pl.program_id(1)

        @pl.when(qi == 0)
        def _():
            dk_ref[...] = jnp.zeros_like(dk_ref)
            dv_ref[...] = jnp.zeros_like(dv_ref)

        qv = q_ref[...]
        do = do_ref[...]
        dob = do.astype(BF16)
        delta = jnp.sum(do * o_ref[...], axis=1, keepdims=True)
        lse_v = lse_ref[...]

        def step(j, dq_acc):
            sl = pl.ds(pl.multiple_of(j * ATT_TK, ATT_TK), ATT_TK)
            ks, vs = k_ref[sl, :], v_ref[sl, :]
            row, col = _positions(qi, j)
            p = jnp.where(col <= row, jnp.exp(_dg(qv, ks, 1, 1) * scale - lse_v), 0.0)
            dv_ref[sl, :] += _dg(p.astype(BF16), dob, 0, 0)
            ds = (p * (_dg(dob, vs, 1, 1) - delta) * scale).astype(BF16)
            dk_ref[sl, :] += _dg(ds, qv, 0, 0)
            return dq_acc + _dg(ds, ks, 1, 0)

        dq_ref[...] = lax.fori_loop(0, (qi + 1) * blocks_per_q, step, jnp.zeros((ATT_TQ, dq), F32))

    q_spec, k_spec, v_spec, o_spec, r_spec = _att_specs(s_len, dq, dv)
    return pl.pallas_call(
        body, name=name, grid=(n_heads, s_len // ATT_TQ),
        in_specs=[q_spec, k_spec, v_spec, o_spec, r_spec, o_spec], out_specs=[q_spec, k_spec, v_spec],
        out_shape=[jax.ShapeDtypeStruct((n_heads, s_len, dq), F32), jax.ShapeDtypeStruct((n_heads, s_len, dq), F32),
                   jax.ShapeDtypeStruct((n_heads, s_len, dv), F32)],
        compiler_params=_cparams(("arbitrary", "arbitrary")),
    )(q, k, v, o, lse, do)


def loss_head(y, target, *, tm, name):
    n_rows, width = y.shape

    def body(y_ref, t_ref, dy_ref, loss_ref):
        diff = y_ref[...] - t_ref[...]
        dy_ref[...] = diff / width
        part = 0.5 * jnp.sum(jnp.mean(diff * diff, axis=-1, keepdims=True), axis=0, keepdims=True)

        @pl.when(pl.program_id(0) == 0)
        def _():
            loss_ref[...] = jnp.zeros_like(loss_ref)

        loss_ref[...] += jnp.broadcast_to(part, loss_ref.shape)

    spec = pl.BlockSpec((tm, width), lambda r: (r, 0))
    dy, loss = pl.pallas_call(
        body, name=name, grid=(n_rows // tm,), in_specs=[spec, spec],
        out_specs=[spec, pl.BlockSpec((8, LANES), lambda r: (0, 0))],
        out_shape=[jax.ShapeDtypeStruct(y.shape, F32), jax.ShapeDtypeStruct((8, LANES), F32)],
        compiler_params=_cparams(("arbitrary",)),
    )(y, target)
    return dy, loss[0, 0]


def adamw(parts, w, m, v, *, tm, name):
    n_rows, width = w.shape

    def body(p_ref, w_ref, m_ref, v_ref, g_ref, d_ref, nm_ref, nv_ref):
        g = p_ref[0]
        for i in range(1, N_DEV):
            g = g + p_ref[i]
        m_new = ADAM_B1 * m_ref[...] + (1.0 - ADAM_B1) * g
        v_new = ADAM_B2 * v_ref[...] + (1.0 - ADAM_B2) * jnp.square(g)
        m_hat = m_new / (1.0 - ADAM_B1 ** ADAM_STEP)
        v_hat = v_new / (1.0 - ADAM_B2 ** ADAM_STEP)
        g_ref[...] = g
        d_ref[...] = -ADAM_LR * (m_hat / (jnp.sqrt(v_hat) + ADAM_EPS) + ADAM_WD * w_ref[...])
        nm_ref[...] = m_new
        nv_ref[...] = v_new

    spec = pl.BlockSpec((tm, width), lambda r: (r, 0))
    shp = jax.ShapeDtypeStruct(w.shape, F32)
    return pl.pallas_call(
        body, name=name, grid=(n_rows // tm,),
        in_specs=[pl.BlockSpec((N_DEV, tm, width), lambda r: (0, r, 0)), spec, spec, spec],
        out_specs=[spec] * 4, out_shape=[shp] * 4, compiler_params=_cparams(("parallel",)),
    )(parts, w, m, v)


def _me():
    return lax.axis_index("x"), lax.axis_index("y"), lax.axis_index("c")


def all_gather(block, *, name):
    def body(x_ref, out_ref, send_sems, recv_sems, local_sem):
        x, y, c = _me()
        me, sibling = (x, y, c), (x, y, 1 - c)
        chips = [(1 - x, y), (x, 1 - y), (1 - x, 1 - y)]

        def slot(px, py, pc):
            return out_ref.at[4 * px + 2 * py + pc]

        def copy(k, blk, to, src=None):
            return pltpu.make_async_remote_copy(
                src_ref=slot(*blk) if src is None else src, dst_ref=slot(*blk), send_sem=send_sems.at[k],
                recv_sem=recv_sems.at[k], device_id=to, device_id_type=MESH)

        mine = pltpu.make_async_copy(x_ref, slot(*me), local_sem)
        mine.start()
        first = [copy(0, me, sibling, src=x_ref)]
        first += [copy(1 + j, me, (*chip, c), src=x_ref) for j, chip in enumerate(chips)]
        for cp in first:
            cp.start()
        passed = [copy(4 + j, (*chip, c), sibling) for j, chip in enumerate(chips)]
        for j, chip in enumerate(chips):
            copy(1 + j, (*chip, c), me).wait_recv()
            passed[j].start()
        copy(0, sibling, me).wait_recv()
        for j, chip in enumerate(chips):
            copy(4 + j, (*chip, 1 - c), me).wait_recv()
        for cp in first + passed:
            cp.wait_send()
        mine.wait()

    return pl.pallas_call(
        body, name=name, out_shape=jax.ShapeDtypeStruct((N_DEV,) + block.shape, block.dtype),
        in_specs=[pl.BlockSpec(memory_space=pl.ANY)], out_specs=pl.BlockSpec(memory_space=pl.ANY),
        scratch_shapes=[pltpu.SemaphoreType.DMA((7,)), pltpu.SemaphoreType.DMA((7,)), pltpu.SemaphoreType.DMA],
    )(block)


def all_to_all(parts, *, name):
    def body(p_ref, out_ref, send_sems, recv_sems, local_sem):
        x, y, c = _me()
        my_slot = 4 * x + 2 * y + c
        mine = pltpu.make_async_copy(p_ref.at[my_slot], out_ref.at[my_slot], local_sem)
        mine.start()
        copies = []
        for k in range(1, N_DEV):
            px, py, pc = x ^ (k >> 2), y ^ ((k >> 1) & 1), c ^ (k & 1)
            copies.append(pltpu.make_async_remote_copy(
                src_ref=p_ref.at[4 * px + 2 * py + pc], dst_ref=out_ref.at[my_slot], send_sem=send_sems.at[k - 1],
                recv_sem=recv_sems.at[k - 1], device_id=(px, py, pc), device_id_type=MESH))
        for cp in copies:
            cp.start()
        for cp in copies:
            cp.wait_recv()
        for cp in copies:
            cp.wait_send()
        mine.wait()

    return pl.pallas_call(
        body, name=name, out_shape=jax.ShapeDtypeStruct(parts.shape, parts.dtype),
        in_specs=[pl.BlockSpec(memory_space=pl.ANY)], out_specs=pl.BlockSpec(memory_space=pl.ANY),
        scratch_shapes=[pltpu.SemaphoreType.DMA((7,)), pltpu.SemaphoreType.DMA((7,)), pltpu.SemaphoreType.DMA],
    )(parts)


def to_heads(t, n_heads):
    s_len = t.shape[0]
    return t.reshape(s_len, n_heads, -1).transpose(1, 0, 2)


def from_heads(t):
    return t.transpose(1, 0, 2).reshape(t.shape[1], -1)


def _rows_of(n_elems):
    assert n_elems % FLAT_W == 0
    return n_elems // FLAT_W


def pack_shards(shards):
    flat = jnp.concatenate([shards[n].reshape(-1, FLAT_W) for n in BIG], axis=0)
    pad = -flat.shape[0] % BIG_ROW_TILE
    return jnp.pad(flat, ((0, pad), (0, 0)))


def unpack_shards(flat, like):
    out, r = {}, 0
    for n in BIG:
        rows = _rows_of(like[n].size)
        out[n] = flat[r:r + rows].reshape(like[n].shape)
        r += rows
    return out


def unpack_gathered(gathered, like):
    out, r = {}, 0
    for n, axis in BIG.items():
        shp = like[n].shape
        rows = _rows_of(like[n].size)
        t = gathered[:, r:r + rows].reshape((N_DEV,) + shp)
        t = jnp.moveaxis(t, 0, axis)
        out[n] = t.reshape(shp[:axis] + (N_DEV * shp[axis],) + shp[axis + 1:])
        r += rows
    return out


def pack_full_grads(grads, like):
    pieces = []
    for n, axis in BIG.items():
        shp = like[n].shape
        t = grads[n].reshape(shp[:axis] + (N_DEV, shp[axis]) + shp[axis + 1:])
        pieces.append(jnp.moveaxis(t, axis, 0).reshape(N_DEV, -1, FLAT_W))
    flat = jnp.concatenate(pieces, axis=1)
    pad = -flat.shape[1] % BIG_ROW_TILE
    return jnp.pad(flat, ((0, 0), (0, pad), (0, 0)))


def _small_rows(shape):
    n = 1
    for s in shape:
        n *= s
    return -(-n // LANES)


def pack_small(arrs, shapes):
    pieces = []
    for n in SMALL:
        flat = arrs[n].reshape(-1)
        flat = jnp.pad(flat, (0, _small_rows(shapes[n]) * LANES - flat.shape[0]))
        pieces.append(flat.reshape(-1, LANES))
    flat = jnp.concatenate(pieces, axis=0)
    return jnp.pad(flat, ((0, -flat.shape[0] % SMALL_ROW_TILE), (0, 0)))


def unpack_small(flat, shapes):
    out, r = {}, 0
    for n in SMALL:
        rows = _small_rows(shapes[n])
        size = 1
        for s in shapes[n]:
            size *= s
        out[n] = flat[r:r + rows].reshape(-1)[:size].reshape(shapes[n])
        r += rows
    return out


ROW_TM = 256
WIDE_TM = 128


def _norm_fwd(x, g, name):
    return prow(f_rms, [x], params=[g.reshape(1, -1)], outs=[(x.shape[1], BF16, False)], tm=ROW_TM, name=name)[0]


def _norm_bwd(x, g, dh, add, name, want_row=True):
    res = prow_vjp(f_rms, [x], params=[g.reshape(1, -1)], cts=[dh], row_grad=[want_row],
                   adds=[add] if want_row else None, tm=ROW_TM, name=name)
    return (res[0], res[1].reshape(-1)) if want_row else (None, res[0].reshape(-1))


def ffn_fwd(x, p, tag):
    h = _norm_fwd(x, p['norm'], f"{tag}_norm")
    gu = pmm(h, p['w_gu'], name=f"{tag}_gu")
    act = prow(f_swiglu_act, [gu], outs=[(D_FF, BF16, False)], tm=WIDE_TM, name=f"{tag}_act")[0]
    y = pmm(act, p['w_down'], res=x, alpha=0.5, name=f"{tag}_down")
    return y, (x, h, gu, act)


def ffn_bwd(dy, p, saved, tag):
    x, h, gu, act = saved
    d_act = pmm(dy, p['w_down'], tb=True, alpha=0.5, name=f"{tag}_dact")
    g_down = pmm(act, dy, ta=True, alpha=0.5, name=f"{tag}_gdown")
    d_gu = prow_vjp(f_swiglu_act, [gu], cts=[d_act], row_grad=[True], row_dtypes=[BF16], tm=WIDE_TM,
                    name=f"{tag}_dgu")[0]
    g_gu = pmm(h, d_gu, ta=True, name=f"{tag}_ggu")
    dh = pmm(d_gu, p['w_gu'], tb=True, name=f"{tag}_dh")
    dx, g_norm = _norm_bwd(x, p['norm'], dh, dy, f"{tag}_dnorm")
    return dx, {'norm': g_norm, 'w_gu': g_gu, 'w_down': g_down}


def even_mixer_fwd(x, p):
    s_len = x.shape[0]
    h = _norm_fwd(x, p['norm'], "sbg_norm")
    proj = pmm(h, p['w_in'], name="sbg_in")
    q, k, v = (to_heads(proj[:, i * SB_WIDTH:(i + 1) * SB_WIDTH], SB_HEADS).astype(BF16) for i in range(3))
    o_sb, tot = sb_fwd(q, k, v, name="sb_fwd")
    z = proj[:, 3 * SB_WIDTH:]
    ln_g, ln_b = p['ln_gain'].reshape(1, -1), p['ln_bias'].reshape(1, -1)
    u, gn = prow(f_gate_prep, [z], params=[ln_g, ln_b], outs=[(SG_WIDTH, F32, False)] * 2, tm=ROW_TM,
                 name="sgu_prep")
    gn_g, u_g = to_heads(gn, SG_GROUPS), to_heads(u, SG_GROUPS)
    b3 = p['sgu_b'].reshape(SG_GROUPS, SG_CHUNK, 1)
    o_sg = prow(f_spatial_gate, [gn_g, u_g], gparams=[p['sgu_w'], b3], outs=[(SG_GROUP_DIM, F32, True)],
                tm=SG_CHUNK, name="sgu_mix")[0]
    cat = jnp.concatenate([from_heads(o_sb), from_heads(o_sg)], axis=-1).astype(BF16)
    y = pmm(cat, p['w_out'], res=x, name="sbg_out")
    del s_len
    return y, (x, h, q, k, v, tot, z, gn_g, u_g, b3, cat)


def even_mixer_bwd(dy, p, saved):
    x, h, q, k, v, tot, z, gn_g, u_g, b3, cat = saved
    d_cat = pmm(dy, p['w_out'], tb=True, name="sbg_dcat")
    g_out = pmm(cat, dy, ta=True, name="sbg_gout")
    d_osb = to_heads(d_cat[:, :SB_WIDTH], SB_HEADS)
    d_osg = to_heads(d_cat[:, SB_WIDTH:], SG_GROUPS)
    d_gn_g, d_u_g, g_w, g_b = prow_vjp(f_spatial_gate, [gn_g, u_g], gparams=[p['sgu_w'], b3], cts=[d_osg],
                                       row_grad=[True, True], tm=SG_CHUNK, name="sgu_dmix")
    ln_g, ln_b = p['ln_gain'].reshape(1, -1), p['ln_bias'].reshape(1, -1)
    d_z, g_lng, g_lnb = prow_vjp(f_gate_prep, [z], params=[ln_g, ln_b], cts=[from_heads(d_u_g), from_heads(d_gn_g)],
                                 row_grad=[True], row_dtypes=[BF16], tm=ROW_TM, name="sgu_dprep")
    dq, dk, dv = sb_bwd(q, k, v, tot, d_osb, name="sb_bwd")
    d_proj = jnp.concatenate([from_heads(dq).astype(BF16), from_heads(dk).astype(BF16), from_heads(dv).astype(BF16),
                              d_z], axis=-1)
    g_in = pmm(h, d_proj, ta=True, name="sbg_gin")
    dh = pmm(d_proj, p['w_in'], tb=True, name="sbg_dh")
    dx, g_norm = _norm_bwd(x, p['norm'], dh, dy, "sbg_dnorm")
    return dx, {'norm': g_norm, 'w_in': g_in, 'ln_gain': g_lng.reshape(-1), 'ln_bias': g_lnb.reshape(-1),
                'sgu_w': g_w, 'sgu_b': g_b.reshape(SG_GROUPS, SG_CHUNK), 'w_out': g_out}


def mla_fwd(x, cos, sin, p):
    h = _norm_fwd(x, p['norm'], "mla_norm")
    proj = pmm(h, p['w_in'], name="mla_in")
    c_q, c_kv, k_r = proj[:, :MLA_Q_LORA], proj[:, MLA_Q_LORA:MLA_Q_LORA + MLA_KV_LORA], proj[:, MLA_Q_LORA + MLA_KV_LORA:]
    cqn = _norm_fwd(c_q, p['q_lora_gain'], "mla_qlora_norm")
    ckvn = _norm_fwd(c_kv, p['kv_lora_gain'], "mla_kvlora_norm")
    q_h = to_heads(pmm(cqn, p['w_uq'], name="mla_uq"), MLA_HEADS)
    kv_h = to_heads(pmm(ckvn, p['w_ukv'], name="mla_ukv"), MLA_HEADS)
    k_nope, v = kv_h[..., :MLA_NOPE], kv_h[..., MLA_NOPE:].astype(BF16)
    q_g, k_g = p['q_gain'].reshape(1, -1), p['k_gain'].reshape(1, -1)
    qp = prow(f_mla_q, [q_h, cos, sin], params=[q_g], outs=[(MLA_QK, BF16, True)], tm=ROW_TM, name="mla_qprep")[0]
    kp = prow(f_mla_k, [k_nope, k_r, cos, sin], params=[k_g], outs=[(MLA_QK, BF16, True)], tm=ROW_TM,
              name="mla_kprep")[0]
    o, lse = sm_fwd(qp, kp, v, name="mla_att_fwd")
    o_flat = from_heads(o).astype(BF16)
    y = pmm(o_flat, p['w_out'], res=x, name="mla_out")
    return y, (x, h, c_q, c_kv, k_r, cqn, ckvn, q_h, k_nope, v, qp, kp, o, lse, o_flat, q_g, k_g)


def mla_bwd(dy, cos, sin, p, saved):
    x, h, c_q, c_kv, k_r, cqn, ckvn, q_h, k_nope, v, qp, kp, o, lse, o_flat, q_g, k_g = saved
    do = to_heads(pmm(dy, p['w_out'], tb=True, name="mla_do"), MLA_HEADS)
    g_out = pmm(o_flat, dy, ta=True, name="mla_gout")
    dqp, dkp, dv = sm_bwd(qp, kp, v, o, lse, do, name="mla_att_bwd")
    dq_h, g_qg = prow_vjp(f_mla_q, [q_h, cos, sin], params=[q_g], cts=[dqp], row_grad=[True, False, False],
                          row_dtypes=[BF16], tm=ROW_TM, name="mla_dqprep")
    dk_nope, dk_r, g_kg = prow_vjp(f_mla_k, [k_nope, k_r, cos, sin], params=[k_g], cts=[dkp],
                                   row_grad=[True, True, False, False], tm=ROW_TM, name="mla_dkprep")
    d_q = from_heads(dq_h)
    d_kv = from_heads(jnp.concatenate([dk_nope, dv], axis=-1)).astype(BF16)
    g_uq = pmm(cqn, d_q, ta=True, name="mla_guq")
    d_cqn = pmm(d_q, p['w_uq'], tb=True, name="mla_dcqn")
    g_ukv = pmm(ckvn, d_kv, ta=True, name="mla_gukv")
    d_ckvn = pmm(d_kv, p['w_ukv'], tb=True, name="mla_dckvn")
    d_cq, g_qlora = _norm_bwd(c_q, p['q_lora_gain'], d_cqn, None, "mla_dqlora_norm")
    d_ckv, g_kvlora = _norm_bwd(c_kv, p['kv_lora_gain'], d_ckvn, None, "mla_dkvlora_norm")
    d_proj = jnp.concatenate([d_cq, d_ckv, dk_r], axis=-1).astype(BF16)
    g_in = pmm(h, d_proj, ta=True, name="mla_gin")
    dh = pmm(d_proj, p['w_in'], tb=True, name="mla_dh")
    dx, g_norm = _norm_bwd(x, p['norm'], dh, dy, "mla_dnorm")
    return dx, {'norm': g_norm, 'w_in': g_in, 'q_lora_gain': g_qlora, 'kv_lora_gain': g_kvlora, 'w_uq': g_uq,
                'w_ukv': g_ukv, 'q_gain': g_qg.reshape(-1), 'k_gain': g_kg.reshape(-1), 'w_out': g_out}


def xattn_fwd(x, mem, p, tag):
    hq = _norm_fwd(x, p['norm'], f"{tag}_norm")
    hm = _norm_fwd(mem, p['mem_norm'], f"{tag}_mem_norm")
    q_h = to_heads(pmm(hq, p['wq'], name=f"{tag}_q"), MEM_HEADS)
    kv = pmm(hm, p['wkv'], name=f"{tag}_kv").reshape(mem.shape[0], MEM_HEADS, 2 * MEM_HEAD_DIM).transpose(1, 0, 2)
    k_h, v_h = kv[..., :MEM_HEAD_DIM], kv[..., MEM_HEAD_DIM:]
    q_g, k_g = p['q_gain'].reshape(1, -1), p['k_gain'].reshape(1, -1)
    o_h = prow(f_xattn, [q_h], gparams=[k_h, v_h], params=[q_g, k_g], outs=[(MEM_HEAD_DIM, BF16, True)], tm=ROW_TM,
               name=f"{tag}_att")[0]
    o_flat = from_heads(o_h)
    y = pmm(o_flat, p['wo'], res=x, name=f"{tag}_out")
    return y, (x, mem, hq, hm, q_h, k_h, v_h, q_g, k_g, o_flat)


def xattn_bwd(dy, p, saved, tag):
    x, mem, hq, hm, q_h, k_h, v_h, q_g, k_g, o_flat = saved
    d_o = to_heads(pmm(dy, p['wo'], tb=True, name=f"{tag}_do"), MEM_HEADS)
    g_wo = pmm(o_flat, dy, ta=True, name=f"{tag}_gwo")
    dq_h, dk_h, dv_h, g_qg, g_kg = prow_vjp(f_xattn, [q_h], gparams=[k_h, v_h], params=[q_g, k_g], cts=[d_o],
                                            row_grad=[True], row_dtypes=[BF16], tm=ROW_TM, name=f"{tag}_datt")
    d_q = from_heads(dq_h)
    d_kv = jnp.concatenate([dk_h, dv_h], axis=-1).transpose(1, 0, 2).reshape(mem.shape[0], -1).astype(BF16)
    g_wq = pmm(hq, d_q, ta=True, name=f"{tag}_gwq")
    dhq = pmm(d_q, p['wq'], tb=True, name=f"{tag}_dhq")
    g_wkv = pmm(hm, d_kv, ta=True, name=f"{tag}_gwkv")
    dhm = pmm(d_kv, p['wkv'], tb=True, name=f"{tag}_dhm")
    _, g_mem_norm = _norm_bwd(mem, p['mem_norm'], dhm, None, f"{tag}_dmem_norm", want_row=False)
    dx, g_norm = _norm_bwd(x, p['norm'], dhq, dy, f"{tag}_dnorm")
    return dx, {'norm': g_norm, 'mem_norm': g_mem_norm, 'wq': g_wq, 'wkv': g_wkv, 'q_gain': g_qg.reshape(-1),
                'k_gain': g_kg.reshape(-1), 'wo': g_wo}


def rope_tables(positions):
    half = MLA_ROPE // 2
    inv_freq = ROPE_THETA ** (-jnp.arange(half, dtype=F32) / half)
    ang = positions.astype(F32)[:, None] * inv_freq
    return jnp.cos(ang), jnp.sin(ang)


def local_step(x, mem, positions, target, w):
    cos, sin = rope_tables(positions)

    def ffn_params(kind, layer):
        return {'norm': w[f'ffn_{kind}_norm'][layer], 'w_gu': w[f'ffn_{kind}_w_gu'][layer],
                'w_down': w[f'ffn_{kind}_w_down'][layer]}

    def xattn_params(layer):
        return {'norm': w['xmem_norm'][layer], 'mem_norm': w['xmem_mem_norm'][layer], 'wq': w['xmem_wq'][layer],
                'wkv': w['xmem_wkv'][layer], 'q_gain': w['xmem_q_gain'][layer], 'k_gain': w['xmem_k_gain'][layer],
                'wo': w['xmem_wo'][layer]}

    even_p = {'norm': w['mix_norm'][0], 'w_in': w['sbg_w_in'][0], 'ln_gain': w['sgu_ln_gain'][0],
              'ln_bias': w['sgu_ln_bias'][0], 'sgu_w': w['sgu_w'][0], 'sgu_b': w['sgu_b'][0],
              'w_out': w['sbg_w_out'][0]}
    mla_p = {'norm': w['mix_norm'][1], 'w_in': w['mla_w_in'][0], 'q_lora_gain': w['mla_q_lora_gain'][0],
             'kv_lora_gain': w['mla_kv_lora_gain'][0], 'w_uq': w['mla_w_uq'][0], 'w_ukv': w['mla_w_ukv'][0],
             'q_gain': w['mla_q_gain'][0], 'k_gain': w['mla_k_gain'][0], 'w_out': w['mla_w_out'][0]}

    saved = []
    for layer in range(DEPTH):
        x, s_pre = ffn_fwd(x, ffn_params('pre', layer), f"ffn_pre{layer}")
        if layer % 2 == 0:
            x, s_mix = even_mixer_fwd(x, even_p)
        else:
            x, s_mix = mla_fwd(x, cos, sin, mla_p)
        x, s_x = xattn_fwd(x, mem, xattn_params(layer), f"xmem{layer}")
        x, s_post = ffn_fwd(x, ffn_params('post', layer), f"ffn_post{layer}")
        saved.append((s_pre, s_mix, s_x, s_post))

    dx, loss = loss_head(x, target, tm=ROW_TM, name="loss_head")

    per_layer = []
    for layer in reversed(range(DEPTH)):
        s_pre, s_mix, s_x, s_post = saved[layer]
        dx, g_post = ffn_bwd(dx, ffn_params('post', layer), s_post, f"ffn_post{layer}")
        dx, g_x = xattn_bwd(dx, xattn_params(layer), s_x, f"xmem{layer}")
        if layer % 2 == 0:
            dx, g_mix = even_mixer_bwd(dx, even_p, s_mix)
        else:
            dx, g_mix = mla_bwd(dx, cos, sin, mla_p, s_mix)
        dx, g_pre = ffn_bwd(dx, ffn_params('pre', layer), s_pre, f"ffn_pre{layer}")
        per_layer.append((layer, g_pre, g_mix, g_x, g_post))
    per_layer.sort(key=lambda t: t[0])

    def stack(pick):
        return jnp.stack([pick(t) for t in per_layer])

    g_even, g_mla = per_layer[0][2], per_layer[1][2]
    grads = {
        'ffn_pre_norm': stack(lambda t: t[1]['norm']), 'ffn_pre_w_gu': stack(lambda t: t[1]['w_gu']),
        'ffn_pre_w_down': stack(lambda t: t[1]['w_down']), 'mix_norm': stack(lambda t: t[2]['norm']),
        'sbg_w_in': g_even['w_in'][None], 'sgu_ln_gain': g_even['ln_gain'][None], 'sgu_ln_bias': g_even['ln_bias'][None],
        'sgu_w': g_even['sgu_w'][None], 'sgu_b': g_even['sgu_b'][None], 'sbg_w_out': g_even['w_out'][None],
        'mla_w_in': g_mla['w_in'][None], 'mla_q_lora_gain': g_mla['q_lora_gain'][None],
        'mla_kv_lora_gain': g_mla['kv_lora_gain'][None], 'mla_w_uq': g_mla['w_uq'][None],
        'mla_w_ukv': g_mla['w_ukv'][None], 'mla_q_gain': g_mla['q_gain'][None], 'mla_k_gain': g_mla['k_gain'][None],
        'mla_w_out': g_mla['w_out'][None],
        'xmem_norm': stack(lambda t: t[3]['norm']), 'xmem_mem_norm': stack(lambda t: t[3]['mem_norm']),
        'xmem_wq': stack(lambda t: t[3]['wq']), 'xmem_wkv': stack(lambda t: t[3]['wkv']),
        'xmem_q_gain': stack(lambda t: t[3]['q_gain']), 'xmem_k_gain': stack(lambda t: t[3]['k_gain']),
        'xmem_wo': stack(lambda t: t[3]['wo']),
        'ffn_post_norm': stack(lambda t: t[4]['norm']), 'ffn_post_w_gu': stack(lambda t: t[4]['w_gu']),
        'ffn_post_w_down': stack(lambda t: t[4]['w_down']),
    }
    return loss, dx, grads


def _device_slot():
    x, y, c = _me()
    return 4 * x + 2 * y + c


def kernel(x, mem, positions, ffn_pre_norm, ffn_pre_w_gu, ffn_pre_w_down, mix_norm, sbg_w_in, sgu_ln_gain, sgu_ln_bias, sgu_w, sgu_b, sbg_w_out, mla_w_in, mla_q_lora_gain, mla_kv_lora_gain, mla_w_uq, mla_w_ukv, mla_q_gain, mla_k_gain, mla_w_out, xmem_norm, xmem_mem_norm, xmem_wq, xmem_wkv, xmem_q_gain, xmem_k_gain, xmem_wo, ffn_post_norm, ffn_post_w_gu, ffn_post_w_down, loss_target, m_ffn_pre_norm, m_ffn_pre_w_gu, m_ffn_pre_w_down, m_mix_norm, m_sbg_w_in, m_sgu_ln_gain, m_sgu_ln_bias, m_sgu_w, m_sgu_b, m_sbg_w_out, m_mla_w_in, m_mla_q_lora_gain, m_mla_kv_lora_gain, m_mla_w_uq, m_mla_w_ukv, m_mla_q_gain, m_mla_k_gain, m_mla_w_out, m_xmem_norm, m_xmem_mem_norm, m_xmem_wq, m_xmem_wkv, m_xmem_q_gain, m_xmem_k_gain, m_xmem_wo, m_ffn_post_norm, m_ffn_post_w_gu, m_ffn_post_w_down, v_ffn_pre_norm, v_ffn_pre_w_gu, v_ffn_pre_w_down, v_mix_norm, v_sbg_w_in, v_sgu_ln_gain, v_sgu_ln_bias, v_sgu_w, v_sgu_b, v_sbg_w_out, v_mla_w_in, v_mla_q_lora_gain, v_mla_kv_lora_gain, v_mla_w_uq, v_mla_w_ukv, v_mla_q_gain, v_mla_k_gain, v_mla_w_out, v_xmem_norm, v_xmem_mem_norm, v_xmem_wq, v_xmem_wkv, v_xmem_q_gain, v_xmem_k_gain, v_xmem_wo, v_ffn_post_norm, v_ffn_post_w_gu, v_ffn_post_w_down):
    args = locals()
    w_in = {n: args[n] for n in WEIGHTS}
    m_in = {n: args["m_" + n] for n in WEIGHTS}
    v_in = {n: args["v_" + n] for n in WEIGHTS}
    slot = _device_slot()

    tiny = jnp.zeros((8, LANES), F32)
    for i, src in enumerate((w_in, m_in, v_in)):
        tiny = tiny.at[i, :64].set(src['mla_q_lora_gain'][0]).at[i + 3, :32].set(src['mla_kv_lora_gain'][0])
    tiny_all = all_gather(tiny, name="gather_lora_gains")
    full_small = []
    for i, src in enumerate((w_in, m_in, v_in)):
        d = {n: src[n] for n in SMALL}
        d['mla_q_lora_gain'] = tiny_all[:, i, :64].reshape(1, MLA_Q_LORA)
        d['mla_kv_lora_gain'] = tiny_all[:, i + 3, :32].reshape(1, MLA_KV_LORA)
        full_small.append(d)
    w_small, m_small, v_small = full_small
    small_shapes = {n: w_small[n].shape for n in SMALL}

    big_shards = {n: w_in[n] for n in BIG}
    gathered = all_gather(pack_shards(big_shards).astype(BF16), name="gather_weights")
    w_full = dict(w_small)
    w_full.update(unpack_gathered(gathered, big_shards))

    loss, dx, grads = local_step(x[0], mem[0], positions[0], loss_target[0], w_full)
    loss = lax.psum(loss, ("x", "y", "c"))

    parts = all_to_all(pack_full_grads(grads, big_shards), name="exchange_grads")
    big_out = adamw(parts, pack_shards(big_shards), pack_shards({n: m_in[n] for n in BIG}),
                    pack_shards({n: v_in[n] for n in BIG}), tm=BIG_ROW_TILE, name="adamw_big")
    big_out = [unpack_shards(t, big_shards) for t in big_out]

    small_parts = all_gather(pack_small({n: grads[n] for n in SMALL}, small_shapes), name="gather_small_grads")
    small_out = adamw(small_parts, pack_small(w_small, small_shapes), pack_small(m_small, small_shapes),
                      pack_small(v_small, small_shapes), tm=SMALL_ROW_TILE, name="adamw_small")
    small_out = [unpack_small(t, small_shapes) for t in small_out]
    for d in small_out:
        for n, width in zip(GAIN_SHARDED, (64, 32)):
            d[n] = lax.dynamic_slice(d[n], (0, slot * width), (1, width))

    outs = [loss, dx[None]]
    for big_d, small_d in zip(big_out, small_out):
        outs += [big_d[n] if n in BIG else small_d[n] for n in WEIGHTS]
    return tuple(outs)
```

```python
import functools

import jax
import jax.numpy as jnp
from jax import lax
from jax.experimental import pallas as pl
from jax.experimental.pallas import tpu as pltpu

F32 = jnp.float32
BF16 = jnp.bfloat16
MESH = pl.DeviceIdType.MESH
N_DEV = 8

VMEM_LIMIT_BYTES = 56 * 1024 * 1024
LANES = 128

D_MODEL = 1024
DEPTH = 2
D_FF = 2816
EPS = 1e-6
SB_HEADS, SB_HEAD_DIM = 8, 64
SB_WIDTH = SB_HEADS * SB_HEAD_DIM
SG_GROUPS, SG_GROUP_DIM, SG_CHUNK = 8, 64, 128
SG_WIDTH = SG_GROUPS * SG_GROUP_DIM
MLA_HEADS, MLA_NOPE, MLA_ROPE, MLA_V = 16, 64, 32, 64
MLA_QK = MLA_NOPE + MLA_ROPE
MLA_Q_LORA, MLA_KV_LORA = 512, 256
ROPE_THETA = 10000.0
MEM_HEADS = 4
MEM_HEAD_DIM = D_MODEL // MEM_HEADS

ADAM_LR, ADAM_B1, ADAM_B2, ADAM_EPS, ADAM_WD, ADAM_STEP = 0.001, 0.9, 0.999, 1e-08, 0.01, 10

WEIGHTS = ['ffn_pre_norm', 'ffn_pre_w_gu', 'ffn_pre_w_down', 'mix_norm', 'sbg_w_in', 'sgu_ln_gain', 'sgu_ln_bias',
           'sgu_w', 'sgu_b', 'sbg_w_out', 'mla_w_in', 'mla_q_lora_gain', 'mla_kv_lora_gain', 'mla_w_uq', 'mla_w_ukv',
           'mla_q_gain', 'mla_k_gain', 'mla_w_out', 'xmem_norm', 'xmem_mem_norm', 'xmem_wq', 'xmem_wkv',
           'xmem_q_gain', 'xmem_k_gain', 'xmem_wo', 'ffn_post_norm', 'ffn_post_w_gu', 'ffn_post_w_down']
BIG = {'ffn_pre_w_gu': 2, 'ffn_pre_w_down': 1, 'sbg_w_in': 2, 'sbg_w_out': 1, 'mla_w_in': 1, 'mla_w_uq': 2,
       'mla_w_ukv': 2, 'mla_w_out': 1, 'xmem_wq': 1, 'xmem_wkv': 2, 'xmem_wo': 1, 'ffn_post_w_gu': 2,
       'ffn_post_w_down': 1}
GAIN_SHARDED = ('mla_q_lora_gain', 'mla_kv_lora_gain')
SMALL = [n for n in WEIGHTS if n not in BIG]
GRAD_WIRE = BF16
FLAT_W = 1024
BIG_ROW_TILE = 256
SMALL_ROW_TILE = 8


def _cparams(sem=None):
    return pltpu.CompilerParams(dimension_semantics=sem, vmem_limit_bytes=VMEM_LIMIT_BYTES)


MM_TILE_CAP = 1408


def _pick(dim, cap=MM_TILE_CAP):
    if dim % LANES:
        return dim
    return max(t for t in range(LANES, min(dim, cap) + 1, LANES) if dim % t == 0)


def pmm(a, b, *, ta=False, tb=False, out_dtype=F32, res=None, alpha=1.0, name):
    kdim, m = (a.shape if ta else a.shape[::-1])
    n, kdim2 = (b.shape if tb else b.shape[::-1])
    assert kdim == kdim2, (a.shape, b.shape, ta, tb)
    tm, tn, tk = _pick(m), _pick(n), _pick(kdim)
    nk = kdim // tk
    dims = (((0 if ta else 1,), (1 if tb else 0,)), ((), ()))

    def body(*refs):
        if res is None:
            a_ref, b_ref, o_ref, acc_ref = refs
        else:
            a_ref, b_ref, r_ref, o_ref, acc_ref = refs
        k = pl.program_id(2)

        @pl.when(k == 0)
        def _():
            acc_ref[...] = jnp.zeros_like(acc_ref)

        acc_ref[...] += lax.dot_general(a_ref[...].astype(BF16), b_ref[...].astype(BF16), dims,
                                        preferred_element_type=F32)

        @pl.when(k == nk - 1)
        def _():
            r = acc_ref[...]
            if alpha != 1.0:
                r = r * alpha
            if res is not None:
                r = r_ref[...] + r
            o_ref[...] = r.astype(out_dtype)

    a_spec = pl.BlockSpec((tk, tm), lambda i, j, k: (k, i)) if ta else pl.BlockSpec((tm, tk), lambda i, j, k: (i, k))
    b_spec = pl.BlockSpec((tn, tk), lambda i, j, k: (j, k)) if tb else pl.BlockSpec((tk, tn), lambda i, j, k: (k, j))
    o_spec = pl.BlockSpec((tm, tn), lambda i, j, k: (i, j))
    ins, in_specs = [a, b], [a_spec, b_spec]
    if res is not None:
        ins.append(res)
        in_specs.append(o_spec)
    return pl.pallas_call(
        body, name=name, grid=(m // tm, n // tn, nk), in_specs=in_specs, out_specs=o_spec,
        out_shape=jax.ShapeDtypeStruct((m, n), out_dtype), scratch_shapes=[pltpu.VMEM((tm, tn), F32)],
        compiler_params=_cparams(("parallel", "parallel", "arbitrary")),
    )(*ins)


def _dg(a, b, ca, cb):
    return lax.dot_general(a, b, (((ca,), (cb,)), ((), ())), preferred_element_type=F32)


@jax.custom_vjp
def bdot(a, b):
    return _dg(a.astype(BF16), b.astype(BF16), 1, 0)


def _bdot_fwd(a, b):
    ab, bb = a.astype(BF16), b.astype(BF16)
    return _dg(ab, bb, 1, 0), (ab, bb)


def _bdot_bwd(saved, g):
    ab, bb = saved
    gb = g.astype(BF16)
    return _dg(gb, bb, 1, 1), _dg(ab, gb, 0, 0)


bdot.defvjp(_bdot_fwd, _bdot_bwd)


@jax.custom_vjp
def bdot_nt(a, b):
    return _dg(a.astype(BF16), b.astype(BF16), 1, 1)


def _bdot_nt_fwd(a, b):
    ab, bb = a.astype(BF16), b.astype(BF16)
    return _dg(ab, bb, 1, 1), (ab, bb)


def _bdot_nt_bwd(saved, g):
    ab, bb = saved
    gb = g.astype(BF16)
    return _dg(gb, bb, 1, 0), _dg(gb, ab, 0, 0)


bdot_nt.defvjp(_bdot_nt_fwd, _bdot_nt_bwd)


def _row_spec(arr, tm):
    if arr.ndim == 3:
        return pl.BlockSpec((None, tm, arr.shape[2]), lambda r, g: (g, r, 0))
    return pl.BlockSpec((tm, arr.shape[1]), lambda r, g: (r, 0))


def _gparam_spec(arr):
    return pl.BlockSpec((None,) + arr.shape[1:], lambda r, g: (g, 0, 0))


def _whole_spec(arr):
    nd = arr.ndim
    return pl.BlockSpec(arr.shape, lambda r, g: (0,) * nd)


def _groups(rows, gparams):
    gs = {a.shape[0] for a in rows if a.ndim == 3} | {a.shape[0] for a in gparams}
    assert len(gs) <= 1
    return gs.pop() if gs else 1


def prow(fn, rows, gparams=(), params=(), *, outs, tm, name):
    rows, gparams, params = list(rows), list(gparams), list(params)
    n_groups = _groups(rows, gparams)
    n_rows = rows[0].shape[-2]
    n_in = len(rows) + len(gparams) + len(params)

    def body(*refs):
        vals = [r[...] for r in refs[:n_in]]
        res = fn(*vals)
        for o_ref, r in zip(refs[n_in:], res, strict=True):
            o_ref[...] = r.astype(o_ref.dtype)

    out_shape, out_specs = [], []
    for width, dtype, grouped in outs:
        shp = (n_groups, n_rows, width) if grouped else (n_rows, width)
        out_shape.append(jax.ShapeDtypeStruct(shp, dtype))
        out_specs.append(_row_spec(out_shape[-1], tm))
    return pl.pallas_call(
        body, name=name, grid=(n_rows // tm, n_groups),
        in_specs=[_row_spec(a, tm) for a in rows] + [_gparam_spec(a) for a in gparams] + [_whole_spec(a) for a in params],
        out_specs=out_specs, out_shape=out_shape,
        compiler_params=_cparams(("parallel", "arbitrary")),
    )(*rows, *gparams, *params)


def prow_vjp(fn, rows, gparams=(), params=(), *, cts, row_grad, adds=None, row_dtypes=None, gparam_grad=None,
             param_grad=None, tm, name):
    rows, gparams, params, cts = list(rows), list(gparams), list(params), list(cts)
    gparam_grad = list(gparam_grad) if gparam_grad is not None else [True] * len(gparams)
    param_grad = list(param_grad) if param_grad is not None else [True] * len(params)
    n_groups = _groups(rows + cts, gparams)
    n_rows = rows[0].shape[-2]
    want_rows = [i for i, w in enumerate(row_grad) if w]
    adds = list(adds) if adds is not None else [None] * len(want_rows)
    row_dtypes = list(row_dtypes) if row_dtypes is not None else [F32] * len(want_rows)
    add_arrays = [a for a in adds if a is not None]
    n_r, n_g, n_p, n_c, n_a = len(rows), len(gparams), len(params), len(cts), len(add_arrays)
    mask = list(row_grad) + gparam_grad + param_grad

    def body(*refs):
        r_id, g_id = pl.program_id(0), pl.program_id(1)
        n_in = n_r + n_g + n_p
        vals = [r[...] for r in refs[:n_in]]
        ct_vals = tuple(r[...].astype(F32) for r in refs[n_in:n_in + n_c])
        add_refs = list(refs[n_in + n_c:n_in + n_c + n_a])
        out_refs = list(refs[n_in + n_c + n_a:])
        diff_idx = [i for i, w in enumerate(mask) if w]

        def wrapped(*diff):
            full = list(vals)
            for i, d in zip(diff_idx, diff):
                full[i] = d
            return tuple(fn(*full))

        _, pull = jax.vjp(wrapped, *[vals[i].astype(F32) for i in diff_idx])
        grads = dict(zip(diff_idx, pull(ct_vals)))
        k = 0
        for j, i in enumerate(want_rows):
            o_ref = out_refs[k]
            k += 1
            gval = grads[i]
            if adds[j] is not None:
                gval = gval + add_refs.pop(0)[...].astype(F32)
            if rows[i].ndim == 2 and n_groups > 1:
                @pl.when(g_id == 0)
                def _(o_ref=o_ref, gval=gval):
                    o_ref[...] = gval.astype(o_ref.dtype)

                @pl.when(g_id != 0)
                def _(o_ref=o_ref, gval=gval):
                    o_ref[...] += gval.astype(o_ref.dtype)
            else:
                o_ref[...] = gval.astype(o_ref.dtype)
        for i in range(n_g):
            if not gparam_grad[i]:
                continue
            o_ref = out_refs[k]
            k += 1
            gval = grads[n_r + i]

            @pl.when(r_id == 0)
            def _(o_ref=o_ref, gval=gval):
                o_ref[g_id] = gval

            @pl.when(r_id != 0)
            def _(o_ref=o_ref, gval=gval):
                o_ref[g_id] += gval
        for i in range(n_p):
            if not param_grad[i]:
                continue
            o_ref = out_refs[k]
            k += 1
            gval = grads[n_r + n_g + i]
            first = jnp.logical_and(r_id == 0, g_id == 0)

            @pl.when(first)
            def _(o_ref=o_ref, gval=gval):
                o_ref[...] = gval

            @pl.when(jnp.logical_not(first))
            def _(o_ref=o_ref, gval=gval):
                o_ref[...] += gval

    out_shape, out_specs = [], []
    for j, i in enumerate(want_rows):
        out_shape.append(jax.ShapeDtypeStruct(rows[i].shape, row_dtypes[j]))
        out_specs.append(_row_spec(rows[i], tm))
    for i in range(n_g):
        if gparam_grad[i]:
            out_shape.append(jax.ShapeDtypeStruct(gparams[i].shape, F32))
            out_specs.append(_whole_spec(gparams[i]))
    for i in range(n_p):
        if param_grad[i]:
            out_shape.append(jax.ShapeDtypeStruct(params[i].shape, F32))
            out_specs.append(_whole_spec(params[i]))
    return pl.pallas_call(
        body, name=name, grid=(n_rows // tm, n_groups),
        in_specs=([_row_spec(a, tm) for a in rows] + [_gparam_spec(a) for a in gparams]
                  + [_whole_spec(a) for a in params] + [_row_spec(a, tm) for a in cts]
                  + [_row_spec(a, tm) for a in add_arrays]),
        out_specs=out_specs, out_shape=out_shape,
        compiler_params=_cparams(("arbitrary", "arbitrary")),
    )(*rows, *gparams, *params, *cts, *add_arrays)


def f_rms(x, g):
    xf = x.astype(F32)
    return (xf * lax.rsqrt(jnp.mean(xf * xf, axis=-1, keepdims=True) + EPS) * g,)


def f_swiglu_act(gu):
    return (jax.nn.silu(gu[:, :D_FF]) * gu[:, D_FF:],)


def f_gate_prep(z, ln_g, ln_b):
    act = jax.nn.gelu(z)
    u, gg = act[:, :SG_WIDTH], act[:, SG_WIDTH:]
    mu = jnp.mean(gg, axis=-1, keepdims=True)
    var = jnp.mean(jnp.square(gg - mu), axis=-1, keepdims=True)
    return u, (gg - mu) * lax.rsqrt(var + EPS) * ln_g + ln_b


def f_spatial_gate(gn, u, w, b):
    t = lax.broadcasted_iota(jnp.int32, w.shape, 0)
    s = lax.broadcasted_iota(jnp.int32, w.shape, 1)
    mixed = bdot(jnp.where(s <= t, w, 0.0), gn) + b
    return (u * mixed,)


def _rope_tail(t, cos, sin):
    half = MLA_ROPE // 2
    t1, t2 = t[:, MLA_NOPE:MLA_NOPE + half], t[:, MLA_NOPE + half:]
    return jnp.concatenate([t[:, :MLA_NOPE], t1 * cos - t2 * sin, t1 * sin + t2 * cos], axis=-1)


def f_mla_q(q, cos, sin, g):
    return (_rope_tail(f_rms(q, g)[0], cos, sin),)


def f_mla_k(k_nope, k_r, cos, sin, g):
    return (_rope_tail(f_rms(jnp.concatenate([k_nope, k_r], axis=-1), g)[0], cos, sin),)


def f_xattn(q, k, v, q_g, k_g):
    qn, kn = f_rms(q, q_g)[0], f_rms(k, k_g)[0]
    sc = bdot_nt(qn, kn) * (MEM_HEAD_DIM ** -0.5)
    return (bdot(jax.nn.softmax(sc, axis=-1), v),)


def _split_dot(x, tri):
    hi = x.astype(BF16)
    lo = (x - hi.astype(F32)).astype(BF16)
    return _dg(hi, tri, 1, 0) + _dg(lo, tri, 1, 0)


def _tri(tk, cmp):
    j = lax.broadcasted_iota(jnp.int32, (tk, tk), 0)
    s = lax.broadcasted_iota(jnp.int32, (tk, tk), 1)
    return cmp(j, s).astype(BF16)


def _att_specs(s_len, tq, dq, dv):
    q_spec = pl.BlockSpec((None, tq, dq), lambda h, i: (h, i, 0))
    k_spec = pl.BlockSpec((None, s_len, dq), lambda h, i: (h, 0, 0))
    v_spec = pl.BlockSpec((None, s_len, dv), lambda h, i: (h, 0, 0))
    o_spec = pl.BlockSpec((None, tq, dv), lambda h, i: (h, i, 0))
    r_spec = pl.BlockSpec((None, tq, 1), lambda h, i: (h, i, 0))
    return q_spec, k_spec, v_spec, o_spec, r_spec


def _key_blocks(qi, tq, tk):
    return (qi * tq) // tk, ((qi + 1) * tq + tk - 1) // tk


def _keep(qi, j, tq, tk, strict):
    row = qi * tq + lax.broadcasted_iota(jnp.int32, (tq, tk), 0)
    col = j * tk + lax.broadcasted_iota(jnp.int32, (tq, tk), 1)
    return col < row if strict else col <= row


def _log_sigmoid(z):
    return jnp.minimum(z, 0.0) - jnp.log1p(jnp.exp(-jnp.abs(z)))


def sb_fwd(q, k, v, *, tq, tk, name):
    n_heads, s_len, d = q.shape
    scale = SB_HEAD_DIM ** -0.5

    def body(q_ref, k_ref, v_ref, o_ref, tot_ref):
        qi = pl.program_id(1)
        qv = q_ref[...]
        upper = _tri(tk, lambda j, s: j > s)
        n_full, n_all = _key_blocks(qi, tq, tk)

        def make_step(masked, last):
            def step(jj, carry):
                acc, rest = carry
                j = last - 1 - jj
                sl = pl.ds(pl.multiple_of(j * tk, tk), tk)
                ks, vs = k_ref[sl, :], v_ref[sl, :]
                z = _dg(qv, ks, 1, 1) * scale
                log_beta = _log_sigmoid(z)
                log_stay = log_beta - z
                if masked:
                    valid = _keep(qi, j, tq, tk, True)
                    log_stay = jnp.where(valid, log_stay, 0.0)
                w = jnp.exp(log_beta + _split_dot(log_stay, upper) + rest)
                if masked:
                    w = jnp.where(valid, w, 0.0)
                acc = acc + _dg(w.astype(BF16), vs, 1, 0)
                return acc, rest + jnp.sum(log_stay, axis=1, keepdims=True)
            return step

        carry = (jnp.zeros((tq, d), F32), jnp.zeros((tq, 1), F32))
        carry = lax.fori_loop(0, n_all - n_full, make_step(True, n_all), carry)
        acc, rest = lax.fori_loop(0, n_full, make_step(False, n_full), carry)
        o_ref[...] = acc
        tot_ref[...] = rest

    q_spec, k_spec, v_spec, o_spec, r_spec = _att_specs(s_len, tq, d, d)
    return pl.pallas_call(
        body, name=name, grid=(n_heads, s_len // tq), in_specs=[q_spec, k_spec, v_spec],
        out_specs=[o_spec, r_spec],
        out_shape=[jax.ShapeDtypeStruct((n_heads, s_len, d), F32), jax.ShapeDtypeStruct((n_heads, s_len, 1), F32)],
        compiler_params=_cparams(("parallel", "arbitrary")),
    )(q, k, v)


def sb_bwd(q, k, v, tot, do, *, tq, tk, name):
    n_heads, s_len, d = q.shape
    scale = SB_HEAD_DIM ** -0.5

    def body(q_ref, k_ref, v_ref, tot_ref, do_ref, dq_ref, dk_ref, dv_ref):
        qi = pl.program_id(1)

        @pl.when(qi == 0)
        def _():
            dk_ref[...] = jnp.zeros_like(dk_ref)
            dv_ref[...] = jnp.zeros_like(dv_ref)

        qv = q_ref[...]
        dob = do_ref[...].astype(BF16)
        total = tot_ref[...]
        incl = _tri(tk, lambda j, s: j <= s)
        excl = _tri(tk, lambda j, s: j < s)
        n_full, n_all = _key_blocks(qi, tq, tk)

        def make_step(masked):
            def step(j, carry):
                dq, stay_before, dl_before = carry
                sl = pl.ds(pl.multiple_of(j * tk, tk), tk)
                ks, vs = k_ref[sl, :], v_ref[sl, :]
                z = _dg(qv, ks, 1, 1) * scale
                log_beta = _log_sigmoid(z)
                log_stay = log_beta - z
                if masked:
                    valid = _keep(qi, j, tq, tk, True)
                    log_stay = jnp.where(valid, log_stay, 0.0)
                w = jnp.exp(log_beta + (total - stay_before) - _split_dot(log_stay, incl))
                if masked:
                    w = jnp.where(valid, w, 0.0)
                dl = _dg(dob, vs, 1, 1) * w
                dl_prefix = _split_dot(dl, excl) + dl_before
                beta = jnp.exp(log_beta)
                dz = (dl * (1.0 - beta) - beta * dl_prefix) * scale
                if masked:
                    dz = jnp.where(valid, dz, 0.0)
                dzb = dz.astype(BF16)
                dq = dq + _dg(dzb, ks, 1, 0)
                dk_ref[sl, :] += _dg(dzb, qv, 0, 0)
                dv_ref[sl, :] += _dg(w.astype(BF16), dob, 0, 0)
                return (dq, stay_before + jnp.sum(log_stay, axis=1, keepdims=True),
                        dl_before + jnp.sum(dl, axis=1, keepdims=True))
            return step

        zero = jnp.zeros((tq, 1), F32)
        carry = lax.fori_loop(0, n_full, make_step(False), (jnp.zeros((tq, d), F32), zero, zero))
        dq, _, _ = lax.fori_loop(n_full, n_all, make_step(True), carry)
        dq_ref[...] = dq

    q_spec, k_spec, v_spec, o_spec, r_spec = _att_specs(s_len, tq, d, d)
    shp = jax.ShapeDtypeStruct((n_heads, s_len, d), F32)
    return pl.pallas_call(
        body, name=name, grid=(n_heads, s_len // tq), in_specs=[q_spec, k_spec, v_spec, r_spec, o_spec],
        out_specs=[q_spec, k_spec, v_spec], out_shape=[shp, shp, shp],
        compiler_params=_cparams(("arbitrary", "arbitrary")),
    )(q, k, v, tot, do)


NEG_BIG = -1e30


def sm_fwd(q, k, v, *, tq, tk, name):
    n_heads, s_len, dq = q.shape
    dv = v.shape[2]
    scale = dq ** -0.5

    def body(q_ref, k_ref, v_ref, o_ref, lse_ref):
        qi = pl.program_id(1)
        qv = q_ref[...]
        n_full, n_all = _key_blocks(qi, tq, tk)

        def make_step(masked):
            def step(j, carry):
                acc, m, l = carry
                sl = pl.ds(pl.multiple_of(j * tk, tk), tk)
                ks, vs = k_ref[sl, :], v_ref[sl, :]
                sc = _dg(qv, ks, 1, 1) * scale
                if masked:
                    sc = jnp.where(_keep(qi, j, tq, tk, False), sc, NEG_BIG)
                m_new = jnp.maximum(m, jnp.max(sc, axis=1, keepdims=True))
                p = jnp.exp(sc - m_new)
                fade = jnp.exp(m - m_new)
                return (fade * acc + _dg(p.astype(BF16), vs, 1, 0), m_new,
                        fade * l + jnp.sum(p, axis=1, keepdims=True))
            return step

        carry = (jnp.zeros((tq, dv), F32), jnp.full((tq, 1), NEG_BIG, F32), jnp.zeros((tq, 1), F32))
        carry = lax.fori_loop(0, n_full, make_step(False), carry)
        acc, m, l = lax.fori_loop(n_full, n_all, make_step(True), carry)
        o_ref[...] = acc / l
        lse_ref[...] = m + jnp.log(l)

    q_spec, k_spec, v_spec, o_spec, r_spec = _att_specs(s_len, tq, dq, dv)
    return pl.pallas_call(
        body, name=name, grid=(n_heads, s_len // tq), in_specs=[q_spec, k_spec, v_spec],
        out_specs=[o_spec, r_spec],
        out_shape=[jax.ShapeDtypeStruct((n_heads, s_len, dv), F32), jax.ShapeDtypeStruct((n_heads, s_len, 1), F32)],
        compiler_params=_cparams(("parallel", "arbitrary")),
    )(q, k, v)


def sm_bwd(q, k, v, o, lse, do, *, tq, tk, name):
    n_heads, s_len, dq = q.shape
    dv = v.shape[2]
    scale = dq ** -0.5

    def body(q_ref, k_ref, v_ref, o_ref, lse_ref, do_ref, dq_ref, dk_ref, dv_ref):
        qi = pl.program_id(1)

        @pl.when(qi == 0)
        def _():
            dk_ref[...] = jnp.zeros_like(dk_ref)
            dv_ref[...] = jnp.zeros_like(dv_ref)

        qv = q_ref[...]
        do = do_ref[...]
        dob = do.astype(BF16)
        delta = jnp.sum(do * o_ref[...], axis=1, keepdims=True)
        lse_v = lse_ref[...]
        n_full, n_all = _key_blocks(qi, tq, tk)

        def make_step(masked):
            def step(j, dq_acc):
                sl = pl.ds(pl.multiple_of(j * tk, tk), tk)
                ks, vs = k_ref[sl, :], v_ref[sl, :]
                p = jnp.exp(_dg(qv, ks, 1, 1) * scale - lse_v)
                if masked:
                    p = jnp.where(_keep(qi, j, tq, tk, False), p, 0.0)
                dv_ref[sl, :] += _dg(p.astype(BF16), dob, 0, 0)
                ds = (p * (_dg(dob, vs, 1, 1) - delta) * scale).astype(BF16)
                dk_ref[sl, :] += _dg(ds, qv, 0, 0)
                return dq_acc + _dg(ds, ks, 1, 0)
            return step

        dq_acc = lax.fori_loop(0, n_full, make_step(False), jnp.zeros((tq, dq), F32))
        dq_ref[...] = lax.fori_loop(n_full, n_all, make_step(True), dq_acc)

    q_spec, k_spec, v_spec, o_spec, r_spec = _att_specs(s_len, tq, dq, dv)
    return pl.pallas_call(
        body, name=name, grid=(n_heads, s_len // tq),
        in_specs=[q_spec, k_spec, v_spec, o_spec, r_spec, o_spec], out_specs=[q_spec, k_spec, v_spec],
        out_shape=[jax.ShapeDtypeStruct((n_heads, s_len, dq), F32), jax.ShapeDtypeStruct((n_heads, s_len, dq), F32),
                   jax.ShapeDtypeStruct((n_heads, s_len, dv), F32)],
        compiler_params=_cparams(("arbitrary", "arbitrary")),
    )(q, k, v, o, lse, do)


def loss_head(y, target, *, tm, name):
    n_rows, width = y.shape

    def body(y_ref, t_ref, dy_ref, loss_ref):
        diff = y_ref[...] - t_ref[...]
        dy_ref[...] = diff / width
        part = 0.5 * jnp.sum(jnp.mean(diff * diff, axis=-1, keepdims=True), axis=0, keepdims=True)

        @pl.when(pl.program_id(0) == 0)
        def _():
            loss_ref[...] = jnp.zeros_like(loss_ref)

        loss_ref[...] += jnp.broadcast_to(part, loss_ref.shape)

    spec = pl.BlockSpec((tm, width), lambda r: (r, 0))
    dy, loss = pl.pallas_call(
        body, name=name, grid=(n_rows // tm,), in_specs=[spec, spec],
        out_specs=[spec, pl.BlockSpec((8, LANES), lambda r: (0, 0))],
        out_shape=[jax.ShapeDtypeStruct(y.shape, F32), jax.ShapeDtypeStruct((8, LANES), F32)],
        compiler_params=_cparams(("arbitrary",)),
    )(y, target)
    return dy, loss[0, 0]


def adamw(parts, w, m, v, *, tm, name):
    n_rows, width = w.shape

    def body(p_ref, w_ref, m_ref, v_ref, g_ref, d_ref, nm_ref, nv_ref):
        g = p_ref[0].astype(F32)
        for i in range(1, N_DEV):
            g = g + p_ref[i].astype(F32)
        m_new = ADAM_B1 * m_ref[...] + (1.0 - ADAM_B1) * g
        v_new = ADAM_B2 * v_ref[...] + (1.0 - ADAM_B2) * jnp.square(g)
        m_hat = m_new / (1.0 - ADAM_B1 ** ADAM_STEP)
        v_hat = v_new / (1.0 - ADAM_B2 ** ADAM_STEP)
        g_ref[...] = g
        d_ref[...] = -ADAM_LR * (m_hat / (jnp.sqrt(v_hat) + ADAM_EPS) + ADAM_WD * w_ref[...])
        nm_ref[...] = m_new
        nv_ref[...] = v_new

    spec = pl.BlockSpec((tm, width), lambda r: (r, 0))
    shp = jax.ShapeDtypeStruct(w.shape, F32)
    return pl.pallas_call(
        body, name=name, grid=(n_rows // tm,),
        in_specs=[pl.BlockSpec((N_DEV, tm, width), lambda r: (0, r, 0)), spec, spec, spec],
        out_specs=[spec] * 4, out_shape=[shp] * 4, compiler_params=_cparams(("parallel",)),
    )(parts, w, m, v)


def _me():
    return lax.axis_index("x"), lax.axis_index("y"), lax.axis_index("c")


def all_gather(block, *, name):
    def body(x_ref, out_ref, send_sems, recv_sems, local_sem):
        x, y, c = _me()
        me, sibling = (x, y, c), (x, y, 1 - c)
        chips = [(1 - x, y), (x, 1 - y), (1 - x, 1 - y)]

        def slot(px, py, pc):
            return out_ref.at[4 * px + 2 * py + pc]

        def copy(k, blk, to, src=None):
            return pltpu.make_async_remote_copy(
                src_ref=slot(*blk) if src is None else src, dst_ref=slot(*blk), send_sem=send_sems.at[k],
                recv_sem=recv_sems.at[k], device_id=to, device_id_type=MESH)

        mine = pltpu.make_async_copy(x_ref, slot(*me), local_sem)
        mine.start()
        first = [copy(0, me, sibling, src=x_ref)]
        first += [copy(1 + j, me, (*chip, c), src=x_ref) for j, chip in enumerate(chips)]
        for cp in first:
            cp.start()
        passed = [copy(4 + j, (*chip, c), sibling) for j, chip in enumerate(chips)]
        for j, chip in enumerate(chips):
            copy(1 + j, (*chip, c), me).wait_recv()
            passed[j].start()
        copy(0, sibling, me).wait_recv()
        for j, chip in enumerate(chips):
            copy(4 + j, (*chip, 1 - c), me).wait_recv()
        for cp in first + passed:
            cp.wait_send()
        mine.wait()

    return pl.pallas_call(
        body, name=name, out_shape=jax.ShapeDtypeStruct((N_DEV,) + block.shape, block.dtype),
        in_specs=[pl.BlockSpec(memory_space=pl.ANY)], out_specs=pl.BlockSpec(memory_space=pl.ANY),
        scratch_shapes=[pltpu.SemaphoreType.DMA((7,)), pltpu.SemaphoreType.DMA((7,)), pltpu.SemaphoreType.DMA],
    )(block)


def all_to_all(parts, *, name):
    def body(p_ref, out_ref, send_sems, recv_sems, local_sem):
        x, y, c = _me()
        my_slot = 4 * x + 2 * y + c
        mine = pltpu.make_async_copy(p_ref.at[my_slot], out_ref.at[my_slot], local_sem)
        mine.start()
        copies = []
        for k in range(1, N_DEV):
            px, py, pc = x ^ (k >> 2), y ^ ((k >> 1) & 1), c ^ (k & 1)
            copies.append(pltpu.make_async_remote_copy(
                src_ref=p_ref.at[4 * px + 2 * py + pc], dst_ref=out_ref.at[my_slot], send_sem=send_sems.at[k - 1],
                recv_sem=recv_sems.at[k - 1], device_id=(px, py, pc), device_id_type=MESH))
        for cp in copies:
            cp.start()
        for cp in copies:
            cp.wait_recv()
        for cp in copies:
            cp.wait_send()
        mine.wait()

    return pl.pallas_call(
        body, name=name, out_shape=jax.ShapeDtypeStruct(parts.shape, parts.dtype),
        in_specs=[pl.BlockSpec(memory_space=pl.ANY)], out_specs=pl.BlockSpec(memory_space=pl.ANY),
        scratch_shapes=[pltpu.SemaphoreType.DMA((7,)), pltpu.SemaphoreType.DMA((7,)), pltpu.SemaphoreType.DMA],
    )(parts)


def to_heads(t, n_heads):
    s_len = t.shape[0]
    return t.reshape(s_len, n_heads, -1).transpose(1, 0, 2)


def from_heads(t):
    return t.transpose(1, 0, 2).reshape(t.shape[1], -1)


def _rows_of(n_elems):
    assert n_elems % FLAT_W == 0
    return n_elems // FLAT_W


def pack_shards(shards):
    flat = jnp.concatenate([shards[n].reshape(-1, FLAT_W) for n in BIG], axis=0)
    pad = -flat.shape[0] % BIG_ROW_TILE
    return jnp.pad(flat, ((0, pad), (0, 0)))


def unpack_shards(flat, like):
    out, r = {}, 0
    for n in BIG:
        rows = _rows_of(like[n].size)
        out[n] = flat[r:r + rows].reshape(like[n].shape)
        r += rows
    return out


def unpack_gathered(gathered, like):
    out, r = {}, 0
    for n, axis in BIG.items():
        shp = like[n].shape
        rows = _rows_of(like[n].size)
        t = gathered[:, r:r + rows].reshape((N_DEV,) + shp)
        t = jnp.moveaxis(t, 0, axis)
        out[n] = t.reshape(shp[:axis] + (N_DEV * shp[axis],) + shp[axis + 1:])
        r += rows
    return out


def pack_full_grads(grads, like):
    pieces = []
    for n, axis in BIG.items():
        shp = like[n].shape
        t = grads[n].reshape(shp[:axis] + (N_DEV, shp[axis]) + shp[axis + 1:])
        pieces.append(jnp.moveaxis(t, axis, 0).reshape(N_DEV, -1, FLAT_W))
    flat = jnp.concatenate(pieces, axis=1)
    pad = -flat.shape[1] % BIG_ROW_TILE
    return jnp.pad(flat, ((0, 0), (0, pad), (0, 0)))


def _small_rows(shape):
    n = 1
    for s in shape:
        n *= s
    return -(-n // LANES)


def pack_small(arrs, shapes):
    pieces = []
    for n in SMALL:
        flat = arrs[n].reshape(-1)
        flat = jnp.pad(flat, (0, _small_rows(shapes[n]) * LANES - flat.shape[0]))
        pieces.append(flat.reshape(-1, LANES))
    flat = jnp.concatenate(pieces, axis=0)
    return jnp.pad(flat, ((0, -flat.shape[0] % SMALL_ROW_TILE), (0, 0)))


def unpack_small(flat, shapes):
    out, r = {}, 0
    for n in SMALL:
        rows = _small_rows(shapes[n])
        size = 1
        for s in shapes[n]:
            size *= s
        out[n] = flat[r:r + rows].reshape(-1)[:size].reshape(shapes[n])
        r += rows
    return out


ROW_TM = 256
WIDE_TM = 128
HEAD_TM = 1024
SB_TILES = (512, 512)
SM_TILES = (1024, 1024)


def _norm_fwd(x, g, name):
    return prow(f_rms, [x], params=[g.reshape(1, -1)], outs=[(x.shape[1], BF16, False)], tm=ROW_TM, name=name)[0]


def _norm_bwd(x, g, dh, add, name, want_row=True):
    res = prow_vjp(f_rms, [x], params=[g.reshape(1, -1)], cts=[dh], row_grad=[want_row],
                   adds=[add] if want_row else None, tm=ROW_TM, name=name)
    return (res[0], res[1].reshape(-1)) if want_row else (None, res[0].reshape(-1))


def ffn_fwd(x, p, tag):
    h = _norm_fwd(x, p['norm'], f"{tag}_norm")
    gu = pmm(h, p['w_gu'], name=f"{tag}_gu")
    act = prow(f_swiglu_act, [gu], outs=[(D_FF, BF16, False)], tm=WIDE_TM, name=f"{tag}_act")[0]
    y = pmm(act, p['w_down'], res=x, alpha=0.5, name=f"{tag}_down")
    return y, (x, h, gu, act)


def ffn_bwd(dy, p, saved, tag):
    x, h, gu, act = saved
    d_act = pmm(dy, p['w_down'], tb=True, alpha=0.5, name=f"{tag}_dact")
    g_down = pmm(act, dy, ta=True, out_dtype=GRAD_WIRE, alpha=0.5, name=f"{tag}_gdown")
    d_gu = prow_vjp(f_swiglu_act, [gu], cts=[d_act], row_grad=[True], row_dtypes=[BF16], tm=WIDE_TM,
                    name=f"{tag}_dgu")[0]
    g_gu = pmm(h, d_gu, ta=True, out_dtype=GRAD_WIRE, name=f"{tag}_ggu")
    dh = pmm(d_gu, p['w_gu'], tb=True, name=f"{tag}_dh")
    dx, g_norm = _norm_bwd(x, p['norm'], dh, dy, f"{tag}_dnorm")
    return dx, {'norm': g_norm, 'w_gu': g_gu, 'w_down': g_down}


def even_mixer_fwd(x, p):
    h = _norm_fwd(x, p['norm'], "sbg_norm")
    proj = pmm(h, p['w_in'], name="sbg_in")
    q, k, v = (to_heads(proj[:, i * SB_WIDTH:(i + 1) * SB_WIDTH], SB_HEADS).astype(BF16) for i in range(3))
    o_sb, tot = sb_fwd(q, k, v, tq=SB_TILES[0], tk=SB_TILES[1], name="sb_fwd")
    z = proj[:, 3 * SB_WIDTH:]
    ln_g, ln_b = p['ln_gain'].reshape(1, -1), p['ln_bias'].reshape(1, -1)
    u, gn = prow(f_gate_prep, [z], params=[ln_g, ln_b], outs=[(SG_WIDTH, F32, False)] * 2, tm=ROW_TM,
                 name="sgu_prep")
    gn_g, u_g = to_heads(gn, SG_GROUPS), to_heads(u, SG_GROUPS)
    b3 = p['sgu_b'].reshape(SG_GROUPS, SG_CHUNK, 1)
    o_sg = prow(f_spatial_gate, [gn_g, u_g], gparams=[p['sgu_w'], b3], outs=[(SG_GROUP_DIM, F32, True)],
                tm=SG_CHUNK, name="sgu_mix")[0]
    cat = jnp.concatenate([from_heads(o_sb), from_heads(o_sg)], axis=-1).astype(BF16)
    y = pmm(cat, p['w_out'], res=x, name="sbg_out")
    return y, (x, h, q, k, v, tot, z, gn_g, u_g, b3, cat)


def even_mixer_bwd(dy, p, saved):
    x, h, q, k, v, tot, z, gn_g, u_g, b3, cat = saved
    d_cat = pmm(dy, p['w_out'], tb=True, name="sbg_dcat")
    g_out = pmm(cat, dy, ta=True, out_dtype=GRAD_WIRE, name="sbg_gout")
    d_osb = to_heads(d_cat[:, :SB_WIDTH], SB_HEADS)
    d_osg = to_heads(d_cat[:, SB_WIDTH:], SG_GROUPS)
    d_gn_g, d_u_g, g_w, g_b = prow_vjp(f_spatial_gate, [gn_g, u_g], gparams=[p['sgu_w'], b3], cts=[d_osg],
                                       row_grad=[True, True], tm=SG_CHUNK, name="sgu_dmix")
    ln_g, ln_b = p['ln_gain'].reshape(1, -1), p['ln_bias'].reshape(1, -1)
    d_z, g_lng, g_lnb = prow_vjp(f_gate_prep, [z], params=[ln_g, ln_b], cts=[from_heads(d_u_g), from_heads(d_gn_g)],
                                 row_grad=[True], row_dtypes=[BF16], tm=ROW_TM, name="sgu_dprep")
    dq, dk, dv = sb_bwd(q, k, v, tot, d_osb, tq=SB_TILES[0], tk=SB_TILES[1], name="sb_bwd")
    d_proj = jnp.concatenate([from_heads(dq).astype(BF16), from_heads(dk).astype(BF16), from_heads(dv).astype(BF16),
                              d_z], axis=-1)
    g_in = pmm(h, d_proj, ta=True, out_dtype=GRAD_WIRE, name="sbg_gin")
    dh = pmm(d_proj, p['w_in'], tb=True, name="sbg_dh")
    dx, g_norm = _norm_bwd(x, p['norm'], dh, dy, "sbg_dnorm")
    return dx, {'norm': g_norm, 'w_in': g_in, 'ln_gain': g_lng.reshape(-1), 'ln_bias': g_lnb.reshape(-1),
                'sgu_w': g_w, 'sgu_b': g_b.reshape(SG_GROUPS, SG_CHUNK), 'w_out': g_out}


def mla_fwd(x, cos, sin, p):
    h = _norm_fwd(x, p['norm'], "mla_norm")
    proj = pmm(h, p['w_in'], name="mla_in")
    c_q, c_kv, k_r = proj[:, :MLA_Q_LORA], proj[:, MLA_Q_LORA:MLA_Q_LORA + MLA_KV_LORA], proj[:, MLA_Q_LORA + MLA_KV_LORA:]
    cqn = _norm_fwd(c_q, p['q_lora_gain'], "mla_qlora_norm")
    ckvn = _norm_fwd(c_kv, p['kv_lora_gain'], "mla_kvlora_norm")
    q_h = to_heads(pmm(cqn, p['w_uq'], name="mla_uq"), MLA_HEADS)
    kv_h = to_heads(pmm(ckvn, p['w_ukv'], name="mla_ukv"), MLA_HEADS)
    k_nope, v = kv_h[..., :MLA_NOPE], kv_h[..., MLA_NOPE:].astype(BF16)
    q_g, k_g = p['q_gain'].reshape(1, -1), p['k_gain'].reshape(1, -1)
    qp = prow(f_mla_q, [q_h, cos, sin], params=[q_g], outs=[(MLA_QK, BF16, True)], tm=HEAD_TM, name="mla_qprep")[0]
    kp = prow(f_mla_k, [k_nope, k_r, cos, sin], params=[k_g], outs=[(MLA_QK, BF16, True)], tm=HEAD_TM,
              name="mla_kprep")[0]
    o, lse = sm_fwd(qp, kp, v, tq=SM_TILES[0], tk=SM_TILES[1], name="mla_att_fwd")
    o_flat = from_heads(o).astype(BF16)
    y = pmm(o_flat, p['w_out'], res=x, name="mla_out")
    return y, (x, h, c_q, c_kv, k_r, cqn, ckvn, q_h, k_nope, v, qp, kp, o, lse, o_flat, q_g, k_g)


def mla_bwd(dy, cos, sin, p, saved):
    x, h, c_q, c_kv, k_r, cqn, ckvn, q_h, k_nope, v, qp, kp, o, lse, o_flat, q_g, k_g = saved
    do = to_heads(pmm(dy, p['w_out'], tb=True, name="mla_do"), MLA_HEADS)
    g_out = pmm(o_flat, dy, ta=True, out_dtype=GRAD_WIRE, name="mla_gout")
    dqp, dkp, dv = sm_bwd(qp, kp, v, o, lse, do, tq=SM_TILES[0], tk=SM_TILES[1], name="mla_att_bwd")
    dq_h, g_qg = prow_vjp(f_mla_q, [q_h, cos, sin], params=[q_g], cts=[dqp], row_grad=[True, False, False],
                          row_dtypes=[BF16], tm=HEAD_TM, name="mla_dqprep")
    dk_nope, dk_r, g_kg = prow_vjp(f_mla_k, [k_nope, k_r, cos, sin], params=[k_g], cts=[dkp],
                                   row_grad=[True, True, False, False], tm=HEAD_TM, name="mla_dkprep")
    d_q = from_heads(dq_h)
    d_kv = from_heads(jnp.concatenate([dk_nope, dv], axis=-1)).astype(BF16)
    g_uq = pmm(cqn, d_q, ta=True, out_dtype=GRAD_WIRE, name="mla_guq")
    d_cqn = pmm(d_q, p['w_uq'], tb=True, name="mla_dcqn")
    g_ukv = pmm(ckvn, d_kv, ta=True, out_dtype=GRAD_WIRE, name="mla_gukv")
    d_ckvn = pmm(d_kv, p['w_ukv'], tb=True, name="mla_dckvn")
    d_cq, g_qlora = _norm_bwd(c_q, p['q_lora_gain'], d_cqn, None, "mla_dqlora_norm")
    d_ckv, g_kvlora = _norm_bwd(c_kv, p['kv_lora_gain'], d_ckvn, None, "mla_dkvlora_norm")
    d_proj = jnp.concatenate([d_cq, d_ckv, dk_r], axis=-1).astype(BF16)
    g_in = pmm(h, d_proj, ta=True, out_dtype=GRAD_WIRE, name="mla_gin")
    dh = pmm(d_proj, p['w_in'], tb=True, name="mla_dh")
    dx, g_norm = _norm_bwd(x, p['norm'], dh, dy, "mla_dnorm")
    return dx, {'norm': g_norm, 'w_in': g_in, 'q_lora_gain': g_qlora, 'kv_lora_gain': g_kvlora, 'w_uq': g_uq,
                'w_ukv': g_ukv, 'q_gain': g_qg.reshape(-1), 'k_gain': g_kg.reshape(-1), 'w_out': g_out}


def xattn_fwd(x, mem, p, tag):
    hq = _norm_fwd(x, p['norm'], f"{tag}_norm")
    hm = _norm_fwd(mem, p['mem_norm'], f"{tag}_mem_norm")
    q_h = to_heads(pmm(hq, p['wq'], name=f"{tag}_q"), MEM_HEADS)
    kv = pmm(hm, p['wkv'], name=f"{tag}_kv").reshape(mem.shape[0], MEM_HEADS, 2 * MEM_HEAD_DIM).transpose(1, 0, 2)
    k_h, v_h = kv[..., :MEM_HEAD_DIM], kv[..., MEM_HEAD_DIM:]
    q_g, k_g = p['q_gain'].reshape(1, -1), p['k_gain'].reshape(1, -1)
    o_h = prow(f_xattn, [q_h], gparams=[k_h, v_h], params=[q_g, k_g], outs=[(MEM_HEAD_DIM, BF16, True)], tm=ROW_TM,
               name=f"{tag}_att")[0]
    o_flat = from_heads(o_h)
    y = pmm(o_flat, p['wo'], res=x, name=f"{tag}_out")
    return y, (x, mem, hq, hm, q_h, k_h, v_h, q_g, k_g, o_flat)


def xattn_bwd(dy, p, saved, tag):
    x, mem, hq, hm, q_h, k_h, v_h, q_g, k_g, o_flat = saved
    d_o = to_heads(pmm(dy, p['wo'], tb=True, name=f"{tag}_do"), MEM_HEADS)
    g_wo = pmm(o_flat, dy, ta=True, out_dtype=GRAD_WIRE, name=f"{tag}_gwo")
    dq_h, dk_h, dv_h, g_qg, g_kg = prow_vjp(f_xattn, [q_h], gparams=[k_h, v_h], params=[q_g, k_g], cts=[d_o],
                                            row_grad=[True], row_dtypes=[BF16], tm=ROW_TM, name=f"{tag}_datt")
    d_q = from_heads(dq_h)
    d_kv = jnp.concatenate([dk_h, dv_h], axis=-1).transpose(1, 0, 2).reshape(mem.shape[0], -1).astype(BF16)
    g_wq = pmm(hq, d_q, ta=True, out_dtype=GRAD_WIRE, name=f"{tag}_gwq")
    dhq = pmm(d_q, p['wq'], tb=True, name=f"{tag}_dhq")
    g_wkv = pmm(hm, d_kv, ta=True, out_dtype=GRAD_WIRE, name=f"{tag}_gwkv")
    dhm = pmm(d_kv, p['wkv'], tb=True, name=f"{tag}_dhm")
    _, g_mem_norm = _norm_bwd(mem, p['mem_norm'], dhm, None, f"{tag}_dmem_norm", want_row=False)
    dx, g_norm = _norm_bwd(x, p['norm'], dhq, dy, f"{tag}_dnorm")
    return dx, {'norm': g_norm, 'mem_norm': g_mem_norm, 'wq': g_wq, 'wkv': g_wkv, 'q_gain': g_qg.reshape(-1),
                'k_gain': g_kg.reshape(-1), 'wo': g_wo}


def rope_tables(positions):
    half = MLA_ROPE // 2
    inv_freq = ROPE_THETA ** (-jnp.arange(half, dtype=F32) / half)
    ang = positions.astype(F32)[:, None] * inv_freq
    return jnp.cos(ang), jnp.sin(ang)


def local_step(x, mem, positions, target, w):
    cos, sin = rope_tables(positions)

    def ffn_params(kind, layer):
        return {'norm': w[f'ffn_{kind}_norm'][layer], 'w_gu': w[f'ffn_{kind}_w_gu'][layer],
                'w_down': w[f'ffn_{kind}_w_down'][layer]}

    def xattn_params(layer):
        return {'norm': w['xmem_norm'][layer], 'mem_norm': w['xmem_mem_norm'][layer], 'wq': w['xmem_wq'][layer],
                'wkv': w['xmem_wkv'][layer], 'q_gain': w['xmem_q_gain'][layer], 'k_gain': w['xmem_k_gain'][layer],
                'wo': w['xmem_wo'][layer]}

    even_p = {'norm': w['mix_norm'][0], 'w_in': w['sbg_w_in'][0], 'ln_gain': w['sgu_ln_gain'][0],
              'ln_bias': w['sgu_ln_bias'][0], 'sgu_w': w['sgu_w'][0], 'sgu_b': w['sgu_b'][0],
              'w_out': w['sbg_w_out'][0]}
    mla_p = {'norm': w['mix_norm'][1], 'w_in': w['mla_w_in'][0], 'q_lora_gain': w['mla_q_lora_gain'][0],
             'kv_lora_gain': w['mla_kv_lora_gain'][0], 'w_uq': w['mla_w_uq'][0], 'w_ukv': w['mla_w_ukv'][0],
             'q_gain': w['mla_q_gain'][0], 'k_gain': w['mla_k_gain'][0], 'w_out': w['mla_w_out'][0]}

    saved = []
    for layer in range(DEPTH):
        x, s_pre = ffn_fwd(x, ffn_params('pre', layer), f"ffn_pre{layer}")
        if layer % 2 == 0:
            x, s_mix = even_mixer_fwd(x, even_p)
        else:
            x, s_mix = mla_fwd(x, cos, sin, mla_p)
        x, s_x = xattn_fwd(x, mem, xattn_params(layer), f"xmem{layer}")
        x, s_post = ffn_fwd(x, ffn_params('post', layer), f"ffn_post{layer}")
        saved.append((s_pre, s_mix, s_x, s_post))

    dx, loss = loss_head(x, target, tm=ROW_TM, name="loss_head")

    per_layer = []
    for layer in reversed(range(DEPTH)):
        s_pre, s_mix, s_x, s_post = saved[layer]
        dx, g_post = ffn_bwd(dx, ffn_params('post', layer), s_post, f"ffn_post{layer}")
        dx, g_x = xattn_bwd(dx, xattn_params(layer), s_x, f"xmem{layer}")
        if layer % 2 == 0:
            dx, g_mix = even_mixer_bwd(dx, even_p, s_mix)
        else:
            dx, g_mix = mla_bwd(dx, cos, sin, mla_p, s_mix)
        dx, g_pre = ffn_bwd(dx, ffn_params('pre', layer), s_pre, f"ffn_pre{layer}")
        per_layer.append((layer, g_pre, g_mix, g_x, g_post))
    per_layer.sort(key=lambda t: t[0])

    def stack(pick):
        return jnp.stack([pick(t) for t in per_layer])

    g_even, g_mla = per_layer[0][2], per_layer[1][2]
    grads = {
        'ffn_pre_norm': stack(lambda t: t[1]['norm']), 'ffn_pre_w_gu': stack(lambda t: t[1]['w_gu']),
        'ffn_pre_w_down': stack(lambda t: t[1]['w_down']), 'mix_norm': stack(lambda t: t[2]['norm']),
        'sbg_w_in': g_even['w_in'][None], 'sgu_ln_gain': g_even['ln_gain'][None], 'sgu_ln_bias': g_even['ln_bias'][None],
        'sgu_w': g_even['sgu_w'][None], 'sgu_b': g_even['sgu_b'][None], 'sbg_w_out': g_even['w_out'][None],
        'mla_w_in': g_mla['w_in'][None], 'mla_q_lora_gain': g_mla['q_lora_gain'][None],
        'mla_kv_lora_gain': g_mla['kv_lora_gain'][None], 'mla_w_uq': g_mla['w_uq'][None],
        'mla_w_ukv': g_mla['w_ukv'][None], 'mla_q_gain': g_mla['q_gain'][None], 'mla_k_gain': g_mla['k_gain'][None],
        'mla_w_out': g_mla['w_out'][None],
        'xmem_norm': stack(lambda t: t[3]['norm']), 'xmem_mem_norm': stack(lambda t: t[3]['mem_norm']),
        'xmem_wq': stack(lambda t: t[3]['wq']), 'xmem_wkv': stack(lambda t: t[3]['wkv']),
        'xmem_q_gain': stack(lambda t: t[3]['q_gain']), 'xmem_k_gain': stack(lambda t: t[3]['k_gain']),
        'xmem_wo': stack(lambda t: t[3]['wo']),
        'ffn_post_norm': stack(lambda t: t[4]['norm']), 'ffn_post_w_gu': stack(lambda t: t[4]['w_gu']),
        'ffn_post_w_down': stack(lambda t: t[4]['w_down']),
    }
    return loss, dx, grads


def _device_slot():
    x, y, c = _me()
    return 4 * x + 2 * y + c


def kernel(x, mem, positions, ffn_pre_norm, ffn_pre_w_gu, ffn_pre_w_down, mix_norm, sbg_w_in, sgu_ln_gain, sgu_ln_bias, sgu_w, sgu_b, sbg_w_out, mla_w_in, mla_q_lora_gain, mla_kv_lora_gain, mla_w_uq, mla_w_ukv, mla_q_gain, mla_k_gain, mla_w_out, xmem_norm, xmem_mem_norm, xmem_wq, xmem_wkv, xmem_q_gain, xmem_k_gain, xmem_wo, ffn_post_norm, ffn_post_w_gu, ffn_post_w_down, loss_target, m_ffn_pre_norm, m_ffn_pre_w_gu, m_ffn_pre_w_down, m_mix_norm, m_sbg_w_in, m_sgu_ln_gain, m_sgu_ln_bias, m_sgu_w, m_sgu_b, m_sbg_w_out, m_mla_w_in, m_mla_q_lora_gain, m_mla_kv_lora_gain, m_mla_w_uq, m_mla_w_ukv, m_mla_q_gain, m_mla_k_gain, m_mla_w_out, m_xmem_norm, m_xmem_mem_norm, m_xmem_wq, m_xmem_wkv, m_xmem_q_gain, m_xmem_k_gain, m_xmem_wo, m_ffn_post_norm, m_ffn_post_w_gu, m_ffn_post_w_down, v_ffn_pre_norm, v_ffn_pre_w_gu, v_ffn_pre_w_down, v_mix_norm, v_sbg_w_in, v_sgu_ln_gain, v_sgu_ln_bias, v_sgu_w, v_sgu_b, v_sbg_w_out, v_mla_w_in, v_mla_q_lora_gain, v_mla_kv_lora_gain, v_mla_w_uq, v_mla_w_ukv, v_mla_q_gain, v_mla_k_gain, v_mla_w_out, v_xmem_norm, v_xmem_mem_norm, v_xmem_wq, v_xmem_wkv, v_xmem_q_gain, v_xmem_k_gain, v_xmem_wo, v_ffn_post_norm, v_ffn_post_w_gu, v_ffn_post_w_down):
    args = locals()
    w_in = {n: args[n] for n in WEIGHTS}
    m_in = {n: args["m_" + n] for n in WEIGHTS}
    v_in = {n: args["v_" + n] for n in WEIGHTS}
    slot = _device_slot()

    tiny = jnp.zeros((8, LANES), F32)
    for i, src in enumerate((w_in, m_in, v_in)):
        tiny = tiny.at[i, :64].set(src['mla_q_lora_gain'][0]).at[i + 3, :32].set(src['mla_kv_lora_gain'][0])
    tiny_all = all_gather(tiny, name="gather_lora_gains")
    full_small = []
    for i, src in enumerate((w_in, m_in, v_in)):
        d = {n: src[n] for n in SMALL}
        d['mla_q_lora_gain'] = tiny_all[:, i, :64].reshape(1, MLA_Q_LORA)
        d['mla_kv_lora_gain'] = tiny_all[:, i + 3, :32].reshape(1, MLA_KV_LORA)
        full_small.append(d)
    w_small, m_small, v_small = full_small
    small_shapes = {n: w_small[n].shape for n in SMALL}

    big_shards = {n: w_in[n] for n in BIG}
    gathered = all_gather(pack_shards(big_shards).astype(BF16), name="gather_weights")
    w_full = dict(w_small)
    w_full.update(unpack_gathered(gathered, big_shards))

    loss, dx, grads = local_step(x[0], mem[0], positions[0], loss_target[0], w_full)
    loss = lax.psum(loss, ("x", "y", "c"))

    parts = all_to_all(pack_full_grads(grads, big_shards), name="exchange_grads")
    big_out = adamw(parts, pack_shards(big_shards), pack_shards({n: m_in[n] for n in BIG}),
                    pack_shards({n: v_in[n] for n in BIG}), tm=BIG_ROW_TILE, name="adamw_big")
    big_out = [unpack_shards(t, big_shards) for t in big_out]

    small_parts = all_gather(pack_small({n: grads[n] for n in SMALL}, small_shapes), name="gather_small_grads")
    small_out = adamw(small_parts, pack_small(w_small, small_shapes), pack_small(m_small, small_shapes),
                      pack_small(v_small, small_shapes), tm=SMALL_ROW_TILE, name="adamw_small")
    small_out = [unpack_small(t, small_shapes) for t in small_out]
    for d in small_out:
        for n, width in zip(GAIN_SHARDED, (64, 32)):
            d[n] = lax.dynamic_slice(d[n], (0, slot * width), (1, width))

    outs = [loss, dx[None]]
    for big_d, small_d in zip(big_out, small_out):
        outs += [big_d[n] if n in BIG else small_d[n] for n in WEIGHTS]
    return tuple(outs)
```

```python
import functools

import jax
import jax.numpy as jnp
from jax import lax
from jax.experimental import pallas as pl
from jax.experimental.pallas import tpu as pltpu

F32 = jnp.float32
BF16 = jnp.bfloat16
MESH = pl.DeviceIdType.MESH
N_DEV = 8

VMEM_LIMIT_BYTES = 56 * 1024 * 1024
LANES = 128

D_MODEL = 1024
DEPTH = 2
D_FF = 2816
EPS = 1e-6
SB_HEADS, SB_HEAD_DIM = 8, 64
SB_WIDTH = SB_HEADS * SB_HEAD_DIM
SG_GROUPS, SG_GROUP_DIM, SG_CHUNK = 8, 64, 128
SG_WIDTH = SG_GROUPS * SG_GROUP_DIM
MLA_HEADS, MLA_NOPE, MLA_ROPE, MLA_V = 16, 64, 32, 64
MLA_QK = MLA_NOPE + MLA_ROPE
MLA_Q_LORA, MLA_KV_LORA = 512, 256
ROPE_THETA = 10000.0
MEM_HEADS = 4
MEM_HEAD_DIM = D_MODEL // MEM_HEADS

ADAM_LR, ADAM_B1, ADAM_B2, ADAM_EPS, ADAM_WD, ADAM_STEP = 0.001, 0.9, 0.999, 1e-08, 0.01, 10

WEIGHTS = ['ffn_pre_norm', 'ffn_pre_w_gu', 'ffn_pre_w_down', 'mix_norm', 'sbg_w_in', 'sgu_ln_gain', 'sgu_ln_bias',
           'sgu_w', 'sgu_b', 'sbg_w_out', 'mla_w_in', 'mla_q_lora_gain', 'mla_kv_lora_gain', 'mla_w_uq', 'mla_w_ukv',
           'mla_q_gain', 'mla_k_gain', 'mla_w_out', 'xmem_norm', 'xmem_mem_norm', 'xmem_wq', 'xmem_wkv',
           'xmem_q_gain', 'xmem_k_gain', 'xmem_wo', 'ffn_post_norm', 'ffn_post_w_gu', 'ffn_post_w_down']
BIG = {'ffn_pre_w_gu': 2, 'ffn_pre_w_down': 1, 'sbg_w_in': 2, 'sbg_w_out': 1, 'mla_w_in': 1, 'mla_w_uq': 2,
       'mla_w_ukv': 2, 'mla_w_out': 1, 'xmem_wq': 1, 'xmem_wkv': 2, 'xmem_wo': 1, 'ffn_post_w_gu': 2,
       'ffn_post_w_down': 1}
GAIN_SHARDED = ('mla_q_lora_gain', 'mla_kv_lora_gain')
SMALL = [n for n in WEIGHTS if n not in BIG]
GRAD_WIRE = BF16
SMALL_ROW_MULTIPLE = 16


def _cparams(sem=None):
    return pltpu.CompilerParams(dimension_semantics=sem, vmem_limit_bytes=VMEM_LIMIT_BYTES)


MM_TILE_CAP = 1408


def _pick(dim, cap=MM_TILE_CAP):
    if dim % LANES:
        return dim
    return max(t for t in range(LANES, min(dim, cap) + 1, LANES) if dim % t == 0)


def pmm(a, b, *, ta=False, tb=False, out_dtype=F32, res=None, alpha=1.0, name):
    kdim, m = (a.shape if ta else a.shape[::-1])
    n, kdim2 = (b.shape if tb else b.shape[::-1])
    assert kdim == kdim2, (a.shape, b.shape, ta, tb)
    tm, tn, tk = _pick(m), _pick(n), _pick(kdim)
    nk = kdim // tk
    dims = (((0 if ta else 1,), (1 if tb else 0,)), ((), ()))

    def body(*refs):
        if res is None:
            a_ref, b_ref, o_ref, acc_ref = refs
        else:
            a_ref, b_ref, r_ref, o_ref, acc_ref = refs
        k = pl.program_id(2)

        @pl.when(k == 0)
        def _():
            acc_ref[...] = jnp.zeros_like(acc_ref)

        acc_ref[...] += lax.dot_general(a_ref[...].astype(BF16), b_ref[...].astype(BF16), dims,
                                        preferred_element_type=F32)

        @pl.when(k == nk - 1)
        def _():
            r = acc_ref[...]
            if alpha != 1.0:
                r = r * alpha
            if res is not None:
                r = r_ref[...] + r
            o_ref[...] = r.astype(out_dtype)

    a_spec = pl.BlockSpec((tk, tm), lambda i, j, k: (k, i)) if ta else pl.BlockSpec((tm, tk), lambda i, j, k: (i, k))
    b_spec = pl.BlockSpec((tn, tk), lambda i, j, k: (j, k)) if tb else pl.BlockSpec((tk, tn), lambda i, j, k: (k, j))
    o_spec = pl.BlockSpec((tm, tn), lambda i, j, k: (i, j))
    ins, in_specs = [a, b], [a_spec, b_spec]
    if res is not None:
        ins.append(res)
        in_specs.append(o_spec)
    return pl.pallas_call(
        body, name=name, grid=(m // tm, n // tn, nk), in_specs=in_specs, out_specs=o_spec,
        out_shape=jax.ShapeDtypeStruct((m, n), out_dtype), scratch_shapes=[pltpu.VMEM((tm, tn), F32)],
        compiler_params=_cparams(("parallel", "parallel", "arbitrary")),
    )(*ins)


def _dg(a, b, ca, cb):
    return lax.dot_general(a, b, (((ca,), (cb,)), ((), ())), preferred_element_type=F32)


@jax.custom_vjp
def bdot(a, b):
    return _dg(a.astype(BF16), b.astype(BF16), 1, 0)


def _bdot_fwd(a, b):
    ab, bb = a.astype(BF16), b.astype(BF16)
    return _dg(ab, bb, 1, 0), (ab, bb)


def _bdot_bwd(saved, g):
    ab, bb = saved
    gb = g.astype(BF16)
    return _dg(gb, bb, 1, 1), _dg(ab, gb, 0, 0)


bdot.defvjp(_bdot_fwd, _bdot_bwd)


@jax.custom_vjp
def bdot_nt(a, b):
    return _dg(a.astype(BF16), b.astype(BF16), 1, 1)


def _bdot_nt_fwd(a, b):
    ab, bb = a.astype(BF16), b.astype(BF16)
    return _dg(ab, bb, 1, 1), (ab, bb)


def _bdot_nt_bwd(saved, g):
    ab, bb = saved
    gb = g.astype(BF16)
    return _dg(gb, bb, 1, 0), _dg(gb, ab, 0, 0)


bdot_nt.defvjp(_bdot_nt_fwd, _bdot_nt_bwd)


def _row_spec(arr, tm):
    if arr.ndim == 3:
        return pl.BlockSpec((None, tm, arr.shape[2]), lambda r, g: (g, r, 0))
    return pl.BlockSpec((tm, arr.shape[1]), lambda r, g: (r, 0))


def _gparam_spec(arr):
    return pl.BlockSpec((None,) + arr.shape[1:], lambda r, g: (g, 0, 0))


def _whole_spec(arr):
    nd = arr.ndim
    return pl.BlockSpec(arr.shape, lambda r, g: (0,) * nd)


def _groups(rows, gparams):
    gs = {a.shape[0] for a in rows if a.ndim == 3} | {a.shape[0] for a in gparams}
    assert len(gs) <= 1
    return gs.pop() if gs else 1


def prow(fn, rows, gparams=(), params=(), *, outs, tm, name):
    rows, gparams, params = list(rows), list(gparams), list(params)
    n_groups = _groups(rows, gparams)
    n_rows = rows[0].shape[-2]
    n_in = len(rows) + len(gparams) + len(params)

    def body(*refs):
        vals = [r[...] for r in refs[:n_in]]
        res = fn(*vals)
        for o_ref, r in zip(refs[n_in:], res, strict=True):
            o_ref[...] = r.astype(o_ref.dtype)

    out_shape, out_specs = [], []
    for width, dtype, grouped in outs:
        shp = (n_groups, n_rows, width) if grouped else (n_rows, width)
        out_shape.append(jax.ShapeDtypeStruct(shp, dtype))
        out_specs.append(_row_spec(out_shape[-1], tm))
    return pl.pallas_call(
        body, name=name, grid=(n_rows // tm, n_groups),
        in_specs=[_row_spec(a, tm) for a in rows] + [_gparam_spec(a) for a in gparams] + [_whole_spec(a) for a in params],
        out_specs=out_specs, out_shape=out_shape,
        compiler_params=_cparams(("parallel", "arbitrary")),
    )(*rows, *gparams, *params)


def prow_vjp(fn, rows, gparams=(), params=(), *, cts, row_grad, adds=None, row_dtypes=None, gparam_grad=None,
             param_grad=None, tm, name):
    rows, gparams, params, cts = list(rows), list(gparams), list(params), list(cts)
    gparam_grad = list(gparam_grad) if gparam_grad is not None else [True] * len(gparams)
    param_grad = list(param_grad) if param_grad is not None else [True] * len(params)
    n_groups = _groups(rows + cts, gparams)
    n_rows = rows[0].shape[-2]
    want_rows = [i for i, w in enumerate(row_grad) if w]
    adds = list(adds) if adds is not None else [None] * len(want_rows)
    row_dtypes = list(row_dtypes) if row_dtypes is not None else [F32] * len(want_rows)
    add_arrays = [a for a in adds if a is not None]
    n_r, n_g, n_p, n_c, n_a = len(rows), len(gparams), len(params), len(cts), len(add_arrays)
    mask = list(row_grad) + gparam_grad + param_grad

    def body(*refs):
        r_id, g_id = pl.program_id(0), pl.program_id(1)
        n_in = n_r + n_g + n_p
        vals = [r[...] for r in refs[:n_in]]
        ct_vals = tuple(r[...].astype(F32) for r in refs[n_in:n_in + n_c])
        add_refs = list(refs[n_in + n_c:n_in + n_c + n_a])
        out_refs = list(refs[n_in + n_c + n_a:])
        diff_idx = [i for i, w in enumerate(mask) if w]

        def wrapped(*diff):
            full = list(vals)
            for i, d in zip(diff_idx, diff):
                full[i] = d
            return tuple(fn(*full))

        _, pull = jax.vjp(wrapped, *[vals[i].astype(F32) for i in diff_idx])
        grads = dict(zip(diff_idx, pull(ct_vals)))
        k = 0
        for j, i in enumerate(want_rows):
            o_ref = out_refs[k]
            k += 1
            gval = grads[i]
            if adds[j] is not None:
                gval = gval + add_refs.pop(0)[...].astype(F32)
            if rows[i].ndim == 2 and n_groups > 1:
                @pl.when(g_id == 0)
                def _(o_ref=o_ref, gval=gval):
                    o_ref[...] = gval.astype(o_ref.dtype)

                @pl.when(g_id != 0)
                def _(o_ref=o_ref, gval=gval):
                    o_ref[...] += gval.astype(o_ref.dtype)
            else:
                o_ref[...] = gval.astype(o_ref.dtype)
        for i in range(n_g):
            if not gparam_grad[i]:
                continue
            o_ref = out_refs[k]
            k += 1
            gval = grads[n_r + i]

            @pl.when(r_id == 0)
            def _(o_ref=o_ref, gval=gval):
                o_ref[g_id] = gval

            @pl.when(r_id != 0)
            def _(o_ref=o_ref, gval=gval):
                o_ref[g_id] += gval
        for i in range(n_p):
            if not param_grad[i]:
                continue
            o_ref = out_refs[k]
            k += 1
            gval = grads[n_r + n_g + i]
            first = jnp.logical_and(r_id == 0, g_id == 0)

            @pl.when(first)
            def _(o_ref=o_ref, gval=gval):
                o_ref[...] = gval

            @pl.when(jnp.logical_not(first))
            def _(o_ref=o_ref, gval=gval):
                o_ref[...] += gval

    out_shape, out_specs = [], []
    for j, i in enumerate(want_rows):
        out_shape.append(jax.ShapeDtypeStruct(rows[i].shape, row_dtypes[j]))
        out_specs.append(_row_spec(rows[i], tm))
    for i in range(n_g):
        if gparam_grad[i]:
            out_shape.append(jax.ShapeDtypeStruct(gparams[i].shape, F32))
            out_specs.append(_whole_spec(gparams[i]))
    for i in range(n_p):
        if param_grad[i]:
            out_shape.append(jax.ShapeDtypeStruct(params[i].shape, F32))
            out_specs.append(_whole_spec(params[i]))
    return pl.pallas_call(
        body, name=name, grid=(n_rows // tm, n_groups),
        in_specs=([_row_spec(a, tm) for a in rows] + [_gparam_spec(a) for a in gparams]
                  + [_whole_spec(a) for a in params] + [_row_spec(a, tm) for a in cts]
                  + [_row_spec(a, tm) for a in add_arrays]),
        out_specs=out_specs, out_shape=out_shape,
        compiler_params=_cparams(("arbitrary", "arbitrary")),
    )(*rows, *gparams, *params, *cts, *add_arrays)


def f_rms(x, g):
    xf = x.astype(F32)
    return (xf * lax.rsqrt(jnp.mean(xf * xf, axis=-1, keepdims=True) + EPS) * g,)


def f_swiglu_act(gu):
    return (jax.nn.silu(gu[:, :D_FF]) * gu[:, D_FF:],)


def f_gate_prep(z, ln_g, ln_b):
    act = jax.nn.gelu(z)
    u, gg = act[:, :SG_WIDTH], act[:, SG_WIDTH:]
    mu = jnp.mean(gg, axis=-1, keepdims=True)
    var = jnp.mean(jnp.square(gg - mu), axis=-1, keepdims=True)
    return u, (gg - mu) * lax.rsqrt(var + EPS) * ln_g + ln_b


def f_spatial_gate(gn, u, w, b):
    t = lax.broadcasted_iota(jnp.int32, w.shape, 0)
    s = lax.broadcasted_iota(jnp.int32, w.shape, 1)
    w_causal = jnp.where(s <= t, w, 0.0)
    mixed = [bdot(w_causal, gn[i:i + SG_CHUNK]) + b for i in range(0, gn.shape[0], SG_CHUNK)]
    return (u * (mixed[0] if len(mixed) == 1 else jnp.concatenate(mixed, axis=0)),)


def _rope_tail(t, cos, sin):
    half = MLA_ROPE // 2
    t1, t2 = t[:, MLA_NOPE:MLA_NOPE + half], t[:, MLA_NOPE + half:]
    return jnp.concatenate([t[:, :MLA_NOPE], t1 * cos - t2 * sin, t1 * sin + t2 * cos], axis=-1)


def f_mla_q(q, cos, sin, g):
    return (_rope_tail(f_rms(q, g)[0], cos, sin),)


def f_mla_k(k_nope, k_r, cos, sin, g):
    return (_rope_tail(f_rms(jnp.concatenate([k_nope, k_r], axis=-1), g)[0], cos, sin),)


def f_xattn(q, k, v, q_g, k_g):
    qn, kn = f_rms(q, q_g)[0], f_rms(k, k_g)[0]
    sc = bdot_nt(qn, kn) * (MEM_HEAD_DIM ** -0.5)
    return (bdot(jax.nn.softmax(sc, axis=-1), v),)


def _split_dot(x, tri):
    hi = x.astype(BF16)
    lo = (x - hi.astype(F32)).astype(BF16)
    return _dg(hi, tri, 1, 0) + _dg(lo, tri, 1, 0)


def _tri(tk, cmp):
    j = lax.broadcasted_iota(jnp.int32, (tk, tk), 0)
    s = lax.broadcasted_iota(jnp.int32, (tk, tk), 1)
    return cmp(j, s).astype(BF16)


SCAN_CHUNK = 256


def _row_scan(x, tri, reverse):
    n = x.shape[1] // SCAN_CHUNK
    chunks = [x[:, i * SCAN_CHUNK:(i + 1) * SCAN_CHUNK] for i in range(n)]
    out, seen = [None] * n, None
    for i in (reversed(range(n)) if reverse else range(n)):
        local = _split_dot(chunks[i], tri)
        out[i] = local if seen is None else local + seen
        total = jnp.sum(chunks[i], axis=1, keepdims=True)
        seen = total if seen is None else seen + total
    return (out[0] if n == 1 else jnp.concatenate(out, axis=1)), seen


def _att_specs(s_len, tq, dq, dv):
    q_spec = pl.BlockSpec((None, tq, dq), lambda h, i: (h, i, 0))
    k_spec = pl.BlockSpec((None, s_len, dq), lambda h, i: (h, 0, 0))
    v_spec = pl.BlockSpec((None, s_len, dv), lambda h, i: (h, 0, 0))
    o_spec = pl.BlockSpec((None, tq, dv), lambda h, i: (h, i, 0))
    r_spec = pl.BlockSpec((None, tq, 1), lambda h, i: (h, i, 0))
    return q_spec, k_spec, v_spec, o_spec, r_spec


def _key_blocks(qi, tq, tk):
    return (qi * tq) // tk, ((qi + 1) * tq + tk - 1) // tk


def _keep(qi, j, tq, tk, strict):
    row = qi * tq + lax.broadcasted_iota(jnp.int32, (tq, tk), 0)
    col = j * tk + lax.broadcasted_iota(jnp.int32, (tq, tk), 1)
    return col < row if strict else col <= row


def _log_sigmoid(z):
    return jnp.minimum(z, 0.0) - jnp.log(1.0 + jnp.exp(-jnp.abs(z)))


def sb_fwd(q, k, v, *, tq, tk, name):
    n_heads, s_len, d = q.shape
    scale = SB_HEAD_DIM ** -0.5

    def body(q_ref, k_ref, v_ref, o_ref, tot_ref):
        qi = pl.program_id(1)
        qv = q_ref[...]
        upper = _tri(SCAN_CHUNK, lambda j, s: j > s)
        n_full, n_all = _key_blocks(qi, tq, tk)

        def make_step(masked, last):
            def step(jj, carry):
                acc, rest = carry
                j = last - 1 - jj
                sl = pl.ds(pl.multiple_of(j * tk, tk), tk)
                ks, vs = k_ref[sl, :], v_ref[sl, :]
                z = _dg(qv, ks, 1, 1) * scale
                log_beta = _log_sigmoid(z)
                log_stay = log_beta - z
                if masked:
                    valid = _keep(qi, j, tq, tk, True)
                    log_stay = jnp.where(valid, log_stay, 0.0)
                after, total = _row_scan(log_stay, upper, True)
                w = jnp.exp(log_beta + after + rest)
                if masked:
                    w = jnp.where(valid, w, 0.0)
                acc = acc + _dg(w.astype(BF16), vs, 1, 0)
                return acc, rest + total
            return step

        carry = (jnp.zeros((tq, d), F32), jnp.zeros((tq, 1), F32))
        carry = lax.fori_loop(0, n_all - n_full, make_step(True, n_all), carry)
        acc, rest = lax.fori_loop(0, n_full, make_step(False, n_full), carry)
        o_ref[...] = acc
        tot_ref[...] = rest

    q_spec, k_spec, v_spec, o_spec, r_spec = _att_specs(s_len, tq, d, d)
    return pl.pallas_call(
        body, name=name, grid=(n_heads, s_len // tq), in_specs=[q_spec, k_spec, v_spec],
        out_specs=[o_spec, r_spec],
        out_shape=[jax.ShapeDtypeStruct((n_heads, s_len, d), F32), jax.ShapeDtypeStruct((n_heads, s_len, 1), F32)],
        compiler_params=_cparams(("parallel", "arbitrary")),
    )(q, k, v)


def sb_bwd(q, k, v, tot, do, *, tq, tk, name):
    n_heads, s_len, d = q.shape
    scale = SB_HEAD_DIM ** -0.5

    def body(q_ref, k_ref, v_ref, tot_ref, do_ref, dq_ref, dk_ref, dv_ref):
        qi = pl.program_id(1)

        @pl.when(qi == 0)
        def _():
            dk_ref[...] = jnp.zeros_like(dk_ref)
            dv_ref[...] = jnp.zeros_like(dv_ref)

        qv = q_ref[...]
        dob = do_ref[...].astype(BF16)
        total = tot_ref[...]
        incl = _tri(SCAN_CHUNK, lambda j, s: j <= s)
        excl = _tri(SCAN_CHUNK, lambda j, s: j < s)
        n_full, n_all = _key_blocks(qi, tq, tk)

        def make_step(masked):
            def step(j, carry):
                dq, stay_before, dl_before = carry
                sl = pl.ds(pl.multiple_of(j * tk, tk), tk)
                ks, vs = k_ref[sl, :], v_ref[sl, :]
                z = _dg(qv, ks, 1, 1) * scale
                log_beta = _log_sigmoid(z)
                log_stay = log_beta - z
                if masked:
                    valid = _keep(qi, j, tq, tk, True)
                    log_stay = jnp.where(valid, log_stay, 0.0)
                stay_upto, stay_sum = _row_scan(log_stay, incl, False)
                w = jnp.exp(log_beta + (total - stay_before) - stay_upto)
                if masked:
                    w = jnp.where(valid, w, 0.0)
                dl = _dg(dob, vs, 1, 1) * w
                dl_upto, dl_sum = _row_scan(dl, excl, False)
                dl_prefix = dl_upto + dl_before
                beta = jnp.exp(log_beta)
                dz = (dl * (1.0 - beta) - beta * dl_prefix) * scale
                if masked:
                    dz = jnp.where(valid, dz, 0.0)
                dzb = dz.astype(BF16)
                dq = dq + _dg(dzb, ks, 1, 0)
                dk_ref[sl, :] += _dg(dzb, qv, 0, 0)
                dv_ref[sl, :] += _dg(w.astype(BF16), dob, 0, 0)
                return dq, stay_before + stay_sum, dl_before + dl_sum
            return step

        zero = jnp.zeros((tq, 1), F32)
        carry = lax.fori_loop(0, n_full, make_step(False), (jnp.zeros((tq, d), F32), zero, zero))
        dq, _, _ = lax.fori_loop(n_full, n_all, make_step(True), carry)
        dq_ref[...] = dq

    q_spec, k_spec, v_spec, o_spec, r_spec = _att_specs(s_len, tq, d, d)
    shp = jax.ShapeDtypeStruct((n_heads, s_len, d), F32)
    return pl.pallas_call(
        body, name=name, grid=(n_heads, s_len // tq), in_specs=[q_spec, k_spec, v_spec, r_spec, o_spec],
        out_specs=[q_spec, k_spec, v_spec], out_shape=[shp, shp, shp],
        compiler_params=_cparams(("arbitrary", "arbitrary")),
    )(q, k, v, tot, do)


NEG_BIG = -1e30


def sm_fwd(q, k, v, *, tq, tk, name):
    n_heads, s_len, dq = q.shape
    dv = v.shape[2]
    scale = dq ** -0.5

    def body(q_ref, k_ref, v_ref, o_ref, lse_ref):
        qi = pl.program_id(1)
        qv = q_ref[...]
        n_full, n_all = _key_blocks(qi, tq, tk)

        def make_step(masked):
            def step(j, carry):
                acc, m, l = carry
                sl = pl.ds(pl.multiple_of(j * tk, tk), tk)
                ks, vs = k_ref[sl, :], v_ref[sl, :]
                sc = _dg(qv, ks, 1, 1) * scale
                if masked:
                    sc = jnp.where(_keep(qi, j, tq, tk, False), sc, NEG_BIG)
                m_new = jnp.maximum(m, jnp.max(sc, axis=1, keepdims=True))
                p = jnp.exp(sc - m_new)
                fade = jnp.exp(m - m_new)
                return (fade * acc + _dg(p.astype(BF16), vs, 1, 0), m_new,
                        fade * l + jnp.sum(p, axis=1, keepdims=True))
            return step

        carry = (jnp.zeros((tq, dv), F32), jnp.full((tq, 1), NEG_BIG, F32), jnp.zeros((tq, 1), F32))
        carry = lax.fori_loop(0, n_full, make_step(False), carry)
        acc, m, l = lax.fori_loop(n_full, n_all, make_step(True), carry)
        o_ref[...] = acc / l
        lse_ref[...] = m + jnp.log(l)

    q_spec, k_spec, v_spec, o_spec, r_spec = _att_specs(s_len, tq, dq, dv)
    return pl.pallas_call(
        body, name=name, grid=(n_heads, s_len // tq), in_specs=[q_spec, k_spec, v_spec],
        out_specs=[o_spec, r_spec],
        out_shape=[jax.ShapeDtypeStruct((n_heads, s_len, dv), F32), jax.ShapeDtypeStruct((n_heads, s_len, 1), F32)],
        compiler_params=_cparams(("parallel", "arbitrary")),
    )(q, k, v)


def sm_bwd(q, k, v, o, lse, do, *, tq, tk, name):
    n_heads, s_len, dq = q.shape
    dv = v.shape[2]
    scale = dq ** -0.5

    def body(q_ref, k_ref, v_ref, o_ref, lse_ref, do_ref, dq_ref, dk_ref, dv_ref):
        qi = pl.program_id(1)

        @pl.when(qi == 0)
        def _():
            dk_ref[...] = jnp.zeros_like(dk_ref)
            dv_ref[...] = jnp.zeros_like(dv_ref)

        qv = q_ref[...]
        do = do_ref[...]
        dob = do.astype(BF16)
        delta = jnp.sum(do * o_ref[...], axis=1, keepdims=True)
        lse_v = lse_ref[...]
        n_full, n_all = _key_blocks(qi, tq, tk)

        def make_step(masked):
            def step(j, dq_acc):
                sl = pl.ds(pl.multiple_of(j * tk, tk), tk)
                ks, vs = k_ref[sl, :], v_ref[sl, :]
                p = jnp.exp(_dg(qv, ks, 1, 1) * scale - lse_v)
                if masked:
                    p = jnp.where(_keep(qi, j, tq, tk, False), p, 0.0)
                dv_ref[sl, :] += _dg(p.astype(BF16), dob, 0, 0)
                ds = (p * (_dg(dob, vs, 1, 1) - delta) * scale).astype(BF16)
                dk_ref[sl, :] += _dg(ds, qv, 0, 0)
                return dq_acc + _dg(ds, ks, 1, 0)
            return step

        dq_acc = lax.fori_loop(0, n_full, make_step(False), jnp.zeros((tq, dq), F32))
        dq_ref[...] = lax.fori_loop(n_full, n_all, make_step(True), dq_acc)

    q_spec, k_spec, v_spec, o_spec, r_spec = _att_specs(s_len, tq, dq, dv)
    return pl.pallas_call(
        body, name=name, grid=(n_heads, s_len // tq),
        in_specs=[q_spec, k_spec, v_spec, o_spec, r_spec, o_spec], out_specs=[q_spec, k_spec, v_spec],
        out_shape=[jax.ShapeDtypeStruct((n_heads, s_len, dq), F32), jax.ShapeDtypeStruct((n_heads, s_len, dq), F32),
                   jax.ShapeDtypeStruct((n_heads, s_len, dv), F32)],
        compiler_params=_cparams(("arbitrary", "arbitrary")),
    )(q, k, v, o, lse, do)


def loss_head(y, target, *, tm, name):
    n_rows, width = y.shape

    def body(y_ref, t_ref, dy_ref, loss_ref):
        diff = y_ref[...] - t_ref[...]
        dy_ref[...] = diff / width
        part = 0.5 * jnp.sum(jnp.mean(diff * diff, axis=-1, keepdims=True), axis=0, keepdims=True)

        @pl.when(pl.program_id(0) == 0)
        def _():
            loss_ref[...] = jnp.zeros_like(loss_ref)

        loss_ref[...] += jnp.broadcast_to(part, loss_ref.shape)

    spec = pl.BlockSpec((tm, width), lambda r: (r, 0))
    dy, loss = pl.pallas_call(
        body, name=name, grid=(n_rows // tm,), in_specs=[spec, spec],
        out_specs=[spec, pl.BlockSpec((8, LANES), lambda r: (0, 0))],
        out_shape=[jax.ShapeDtypeStruct(y.shape, F32), jax.ShapeDtypeStruct((8, LANES), F32)],
        compiler_params=_cparams(("arbitrary",)),
    )(y, target)
    return dy, loss[0, 0]


ADAM_TILE_ELEMS = 256 * 1024


def _adam_rows(n_rows, width):
    fits = [t for t in range(16, n_rows + 1, 16) if n_rows % t == 0 and t * width <= ADAM_TILE_ELEMS]
    return max(fits) if fits else n_rows


def adamw(parts, w, m, v, *, name):
    n_rows, width = w.shape
    tm = _adam_rows(n_rows, width)

    def body(p_ref, w_ref, m_ref, v_ref, g_ref, d_ref, nm_ref, nv_ref):
        g = p_ref[0].astype(F32)
        for i in range(1, N_DEV):
            g = g + p_ref[i].astype(F32)
        m_new = ADAM_B1 * m_ref[...] + (1.0 - ADAM_B1) * g
        v_new = ADAM_B2 * v_ref[...] + (1.0 - ADAM_B2) * jnp.square(g)
        m_hat = m_new / (1.0 - ADAM_B1 ** ADAM_STEP)
        v_hat = v_new / (1.0 - ADAM_B2 ** ADAM_STEP)
        g_ref[...] = g
        d_ref[...] = -ADAM_LR * (m_hat / (jnp.sqrt(v_hat) + ADAM_EPS) + ADAM_WD * w_ref[...])
        nm_ref[...] = m_new
        nv_ref[...] = v_new

    spec = pl.BlockSpec((tm, width), lambda r: (r, 0))
    shp = jax.ShapeDtypeStruct(w.shape, F32)
    return pl.pallas_call(
        body, name=name, grid=(n_rows // tm,),
        in_specs=[pl.BlockSpec((N_DEV, tm, width), lambda r: (0, r, 0)), spec, spec, spec],
        out_specs=[spec] * 4, out_shape=[shp] * 4, compiler_params=_cparams(("parallel",)),
    )(parts, w, m, v)


def _me():
    return lax.axis_index("x"), lax.axis_index("y"), lax.axis_index("c")


N_PEERS = N_DEV - 1


def _comm_call(body, ins, out_shape, name):
    n = len(ins)
    hbm = pl.BlockSpec(memory_space=pl.ANY)
    return pl.pallas_call(
        body, name=name, out_shape=out_shape, in_specs=[hbm] * n, out_specs=[hbm] * n,
        scratch_shapes=[pltpu.SemaphoreType.DMA((N_PEERS * n,)), pltpu.SemaphoreType.DMA((N_PEERS * n,)),
                        pltpu.SemaphoreType.DMA((n,))],
    )(*ins)


def all_gather(blocks, *, name):
    n = len(blocks)

    def body(*refs):
        x_refs, out_refs = refs[:n], refs[n:2 * n]
        send_sems, recv_sems, local_sems = refs[2 * n:]
        x, y, c = _me()
        me, sibling = (x, y, c), (x, y, 1 - c)
        chips = [(1 - x, y), (x, 1 - y), (1 - x, 1 - y)]

        def slot(i, px, py, pc):
            return out_refs[i].at[4 * px + 2 * py + pc]

        def copy(i, k, blk, to, src=None):
            return pltpu.make_async_remote_copy(
                src_ref=slot(i, *blk) if src is None else src, dst_ref=slot(i, *blk),
                send_sem=send_sems.at[N_PEERS * i + k], recv_sem=recv_sems.at[N_PEERS * i + k], device_id=to,
                device_id_type=MESH)

        mine = [pltpu.make_async_copy(x_refs[i], slot(i, *me), local_sems.at[i]) for i in range(n)]
        first = []
        for i in range(n):
            first.append(copy(i, 0, me, sibling, src=x_refs[i]))
            first += [copy(i, 1 + j, me, (*chip, c), src=x_refs[i]) for j, chip in enumerate(chips)]
        for cp in mine + first:
            cp.start()
        passed = []
        for j, chip in enumerate(chips):
            for i in range(n):
                copy(i, 1 + j, (*chip, c), me).wait_recv()
                passed.append(copy(i, 4 + j, (*chip, c), sibling))
                passed[-1].start()
        for i in range(n):
            copy(i, 0, sibling, me).wait_recv()
            for j, chip in enumerate(chips):
                copy(i, 4 + j, (*chip, 1 - c), me).wait_recv()
        for cp in first + passed:
            cp.wait_send()
        for cp in mine:
            cp.wait()

    return _comm_call(body, blocks, [jax.ShapeDtypeStruct((N_DEV,) + b.shape, b.dtype) for b in blocks], name)


def all_to_all(parts, *, name):
    n = len(parts)

    def body(*refs):
        p_refs, out_refs = refs[:n], refs[n:2 * n]
        send_sems, recv_sems, local_sems = refs[2 * n:]
        x, y, c = _me()
        my_slot = 4 * x + 2 * y + c
        mine = [pltpu.make_async_copy(p_refs[i].at[my_slot], out_refs[i].at[my_slot], local_sems.at[i])
                for i in range(n)]
        copies = []
        for k in range(1, N_DEV):
            px, py, pc = x ^ (k >> 2), y ^ ((k >> 1) & 1), c ^ (k & 1)
            for i in range(n):
                copies.append(pltpu.make_async_remote_copy(
                    src_ref=p_refs[i].at[4 * px + 2 * py + pc], dst_ref=out_refs[i].at[my_slot],
                    send_sem=send_sems.at[N_PEERS * i + k - 1], recv_sem=recv_sems.at[N_PEERS * i + k - 1],
                    device_id=(px, py, pc), device_id_type=MESH))
        for cp in mine + copies:
            cp.start()
        for cp in copies:
            cp.wait_recv()
        for cp in copies:
            cp.wait_send()
        for cp in mine:
            cp.wait()

    return _comm_call(body, parts, [jax.ShapeDtypeStruct(p.shape, p.dtype) for p in parts], name)


def to_heads(t, n_heads):
    s_len = t.shape[0]
    return t.reshape(s_len, n_heads, -1).transpose(1, 0, 2)


def from_heads(t):
    return t.transpose(1, 0, 2).reshape(t.shape[1], -1)


def gathered_to_full(t, axis):
    shp = t.shape[1:]
    return jnp.moveaxis(t, 0, axis).reshape(shp[:axis] + (N_DEV * shp[axis],) + shp[axis + 1:])


def full_to_owner_major(g, axis):
    shp = g.shape
    t = jnp.moveaxis(g.reshape(shp[:axis] + (N_DEV, shp[axis] // N_DEV) + shp[axis + 1:]), axis, 0)
    return t.reshape(N_DEV, -1, t.shape[-1])


def _small_rows(shape):
    n = 1
    for s in shape:
        n *= s
    return -(-n // LANES)


def pack_small(arrs, shapes):
    pieces = []
    for n in SMALL:
        flat = arrs[n].reshape(-1)
        flat = jnp.pad(flat, (0, _small_rows(shapes[n]) * LANES - flat.shape[0]))
        pieces.append(flat.reshape(-1, LANES))
    flat = jnp.concatenate(pieces, axis=0)
    return jnp.pad(flat, ((0, -flat.shape[0] % SMALL_ROW_MULTIPLE), (0, 0)))


def unpack_small(flat, shapes):
    out, r = {}, 0
    for n in SMALL:
        rows = _small_rows(shapes[n])
        size = 1
        for s in shapes[n]:
            size *= s
        out[n] = flat[r:r + rows].reshape(-1)[:size].reshape(shapes[n])
        r += rows
    return out


ROW_TM = 256
WIDE_TM = 128
HEAD_TM = 1024
SG_TM = 8 * SG_CHUNK
SB_TILES = (512, 512)
SM_TILES = (1024, 1024)


def _norm_fwd(x, g, name):
    return prow(f_rms, [x], params=[g.reshape(1, -1)], outs=[(x.shape[1], BF16, False)], tm=ROW_TM, name=name)[0]


def _norm_bwd(x, g, dh, add, name, want_row=True):
    res = prow_vjp(f_rms, [x], params=[g.reshape(1, -1)], cts=[dh], row_grad=[want_row],
                   adds=[add] if want_row else None, tm=ROW_TM, name=name)
    return (res[0], res[1].reshape(-1)) if want_row else (None, res[0].reshape(-1))


def ffn_fwd(x, p, tag):
    h = _norm_fwd(x, p['norm'], f"{tag}_norm")
    gu = pmm(h, p['w_gu'], name=f"{tag}_gu")
    act = prow(f_swiglu_act, [gu], outs=[(D_FF, BF16, False)], tm=WIDE_TM, name=f"{tag}_act")[0]
    y = pmm(act, p['w_down'], res=x, alpha=0.5, name=f"{tag}_down")
    return y, (x, h, gu, act)


def ffn_bwd(dy, p, saved, tag):
    x, h, gu, act = saved
    d_act = pmm(dy, p['w_down'], tb=True, alpha=0.5, name=f"{tag}_dact")
    g_down = pmm(act, dy, ta=True, out_dtype=GRAD_WIRE, alpha=0.5, name=f"{tag}_gdown")
    d_gu = prow_vjp(f_swiglu_act, [gu], cts=[d_act], row_grad=[True], row_dtypes=[BF16], tm=WIDE_TM,
                    name=f"{tag}_dgu")[0]
    g_gu = pmm(h, d_gu, ta=True, out_dtype=GRAD_WIRE, name=f"{tag}_ggu")
    dh = pmm(d_gu, p['w_gu'], tb=True, name=f"{tag}_dh")
    dx, g_norm = _norm_bwd(x, p['norm'], dh, dy, f"{tag}_dnorm")
    return dx, {'norm': g_norm, 'w_gu': g_gu, 'w_down': g_down}


def even_mixer_fwd(x, p):
    h = _norm_fwd(x, p['norm'], "sbg_norm")
    proj = pmm(h, p['w_in'], name="sbg_in")
    q, k, v = (to_heads(proj[:, i * SB_WIDTH:(i + 1) * SB_WIDTH], SB_HEADS).astype(BF16) for i in range(3))
    o_sb, tot = sb_fwd(q, k, v, tq=SB_TILES[0], tk=SB_TILES[1], name="sb_fwd")
    z = proj[:, 3 * SB_WIDTH:]
    ln_g, ln_b = p['ln_gain'].reshape(1, -1), p['ln_bias'].reshape(1, -1)
    u, gn = prow(f_gate_prep, [z], params=[ln_g, ln_b], outs=[(SG_WIDTH, F32, False)] * 2, tm=ROW_TM,
                 name="sgu_prep")
    gn_g, u_g = to_heads(gn, SG_GROUPS), to_heads(u, SG_GROUPS)
    b3 = p['sgu_b'].reshape(SG_GROUPS, SG_CHUNK, 1)
    o_sg = prow(f_spatial_gate, [gn_g, u_g], gparams=[p['sgu_w'], b3], outs=[(SG_GROUP_DIM, F32, True)],
                tm=SG_TM, name="sgu_mix")[0]
    cat = jnp.concatenate([from_heads(o_sb), from_heads(o_sg)], axis=-1).astype(BF16)
    y = pmm(cat, p['w_out'], res=x, name="sbg_out")
    return y, (x, h, q, k, v, tot, z, gn_g, u_g, b3, cat)


def even_mixer_bwd(dy, p, saved):
    x, h, q, k, v, tot, z, gn_g, u_g, b3, cat = saved
    d_cat = pmm(dy, p['w_out'], tb=True, name="sbg_dcat")
    g_out = pmm(cat, dy, ta=True, out_dtype=GRAD_WIRE, name="sbg_gout")
    d_osb = to_heads(d_cat[:, :SB_WIDTH], SB_HEADS)
    d_osg = to_heads(d_cat[:, SB_WIDTH:], SG_GROUPS)
    d_gn_g, d_u_g, g_w, g_b = prow_vjp(f_spatial_gate, [gn_g, u_g], gparams=[p['sgu_w'], b3], cts=[d_osg],
                                       row_grad=[True, True], tm=SG_TM, name="sgu_dmix")
    ln_g, ln_b = p['ln_gain'].reshape(1, -1), p['ln_bias'].reshape(1, -1)
    d_z, g_lng, g_lnb = prow_vjp(f_gate_prep, [z], params=[ln_g, ln_b], cts=[from_heads(d_u_g), from_heads(d_gn_g)],
                                 row_grad=[True], row_dtypes=[BF16], tm=ROW_TM, name="sgu_dprep")
    dq, dk, dv = sb_bwd(q, k, v, tot, d_osb, tq=SB_TILES[0], tk=SB_TILES[1], name="sb_bwd")
    d_proj = jnp.concatenate([from_heads(dq).astype(BF16), from_heads(dk).astype(BF16), from_heads(dv).astype(BF16),
                              d_z], axis=-1)
    g_in = pmm(h, d_proj, ta=True, out_dtype=GRAD_WIRE, name="sbg_gin")
    dh = pmm(d_proj, p['w_in'], tb=True, name="sbg_dh")
    dx, g_norm = _norm_bwd(x, p['norm'], dh, dy, "sbg_dnorm")
    return dx, {'norm': g_norm, 'w_in': g_in, 'ln_gain': g_lng.reshape(-1), 'ln_bias': g_lnb.reshape(-1),
                'sgu_w': g_w, 'sgu_b': g_b.reshape(SG_GROUPS, SG_CHUNK), 'w_out': g_out}


def mla_fwd(x, cos, sin, p):
    h = _norm_fwd(x, p['norm'], "mla_norm")
    proj = pmm(h, p['w_in'], name="mla_in")
    c_q, c_kv, k_r = proj[:, :MLA_Q_LORA], proj[:, MLA_Q_LORA:MLA_Q_LORA + MLA_KV_LORA], proj[:, MLA_Q_LORA + MLA_KV_LORA:]
    cqn = _norm_fwd(c_q, p['q_lora_gain'], "mla_qlora_norm")
    ckvn = _norm_fwd(c_kv, p['kv_lora_gain'], "mla_kvlora_norm")
    q_h = to_heads(pmm(cqn, p['w_uq'], name="mla_uq"), MLA_HEADS)
    kv_h = to_heads(pmm(ckvn, p['w_ukv'], name="mla_ukv"), MLA_HEADS)
    k_nope, v = kv_h[..., :MLA_NOPE], kv_h[..., MLA_NOPE:].astype(BF16)
    q_g, k_g = p['q_gain'].reshape(1, -1), p['k_gain'].reshape(1, -1)
    qp = prow(f_mla_q, [q_h, cos, sin], params=[q_g], outs=[(MLA_QK, BF16, True)], tm=HEAD_TM, name="mla_qprep")[0]
    kp = prow(f_mla_k, [k_nope, k_r, cos, sin], params=[k_g], outs=[(MLA_QK, BF16, True)], tm=HEAD_TM,
              name="mla_kprep")[0]
    o, lse = sm_fwd(qp, kp, v, tq=SM_TILES[0], tk=SM_TILES[1], name="mla_att_fwd")
    o_flat = from_heads(o).astype(BF16)
    y = pmm(o_flat, p['w_out'], res=x, name="mla_out")
    return y, (x, h, c_q, c_kv, k_r, cqn, ckvn, q_h, k_nope, v, qp, kp, o, lse, o_flat, q_g, k_g)


def mla_bwd(dy, cos, sin, p, saved):
    x, h, c_q, c_kv, k_r, cqn, ckvn, q_h, k_nope, v, qp, kp, o, lse, o_flat, q_g, k_g = saved
    do = to_heads(pmm(dy, p['w_out'], tb=True, name="mla_do"), MLA_HEADS)
    g_out = pmm(o_flat, dy, ta=True, out_dtype=GRAD_WIRE, name="mla_gout")
    dqp, dkp, dv = sm_bwd(qp, kp, v, o, lse, do, tq=SM_TILES[0], tk=SM_TILES[1], name="mla_att_bwd")
    dq_h, g_qg = prow_vjp(f_mla_q, [q_h, cos, sin], params=[q_g], cts=[dqp], row_grad=[True, False, False],
                          row_dtypes=[BF16], tm=HEAD_TM, name="mla_dqprep")
    dk_nope, dk_r, g_kg = prow_vjp(f_mla_k, [k_nope, k_r, cos, sin], params=[k_g], cts=[dkp],
                                   row_grad=[True, True, False, False], tm=HEAD_TM, name="mla_dkprep")
    d_q = from_heads(dq_h)
    d_kv = from_heads(jnp.concatenate([dk_nope, dv], axis=-1)).astype(BF16)
    g_uq = pmm(cqn, d_q, ta=True, out_dtype=GRAD_WIRE, name="mla_guq")
    d_cqn = pmm(d_q, p['w_uq'], tb=True, name="mla_dcqn")
    g_ukv = pmm(ckvn, d_kv, ta=True, out_dtype=GRAD_WIRE, name="mla_gukv")
    d_ckvn = pmm(d_kv, p['w_ukv'], tb=True, name="mla_dckvn")
    d_cq, g_qlora = _norm_bwd(c_q, p['q_lora_gain'], d_cqn, None, "mla_dqlora_norm")
    d_ckv, g_kvlora = _norm_bwd(c_kv, p['kv_lora_gain'], d_ckvn, None, "mla_dkvlora_norm")
    d_proj = jnp.concatenate([d_cq, d_ckv, dk_r], axis=-1).astype(BF16)
    g_in = pmm(h, d_proj, ta=True, out_dtype=GRAD_WIRE, name="mla_gin")
    dh = pmm(d_proj, p['w_in'], tb=True, name="mla_dh")
    dx, g_norm = _norm_bwd(x, p['norm'], dh, dy, "mla_dnorm")
    return dx, {'norm': g_norm, 'w_in': g_in, 'q_lora_gain': g_qlora, 'kv_lora_gain': g_kvlora, 'w_uq': g_uq,
                'w_ukv': g_ukv, 'q_gain': g_qg.reshape(-1), 'k_gain': g_kg.reshape(-1), 'w_out': g_out}


def xattn_fwd(x, mem, p, tag):
    hq = _norm_fwd(x, p['norm'], f"{tag}_norm")
    hm = _norm_fwd(mem, p['mem_norm'], f"{tag}_mem_norm")
    q_h = to_heads(pmm(hq, p['wq'], name=f"{tag}_q"), MEM_HEADS)
    kv = pmm(hm, p['wkv'], name=f"{tag}_kv").reshape(mem.shape[0], MEM_HEADS, 2 * MEM_HEAD_DIM).transpose(1, 0, 2)
    k_h, v_h = kv[..., :MEM_HEAD_DIM], kv[..., MEM_HEAD_DIM:]
    q_g, k_g = p['q_gain'].reshape(1, -1), p['k_gain'].reshape(1, -1)
    o_h = prow(f_xattn, [q_h], gparams=[k_h, v_h], params=[q_g, k_g], outs=[(MEM_HEAD_DIM, BF16, True)], tm=ROW_TM,
               name=f"{tag}_att")[0]
    o_flat = from_heads(o_h)
    y = pmm(o_flat, p['wo'], res=x, name=f"{tag}_out")
    return y, (x, mem, hq, hm, q_h, k_h, v_h, q_g, k_g, o_flat)


def xattn_bwd(dy, p, saved, tag):
    x, mem, hq, hm, q_h, k_h, v_h, q_g, k_g, o_flat = saved
    d_o = to_heads(pmm(dy, p['wo'], tb=True, name=f"{tag}_do"), MEM_HEADS)
    g_wo = pmm(o_flat, dy, ta=True, out_dtype=GRAD_WIRE, name=f"{tag}_gwo")
    dq_h, dk_h, dv_h, g_qg, g_kg = prow_vjp(f_xattn, [q_h], gparams=[k_h, v_h], params=[q_g, k_g], cts=[d_o],
                                            row_grad=[True], row_dtypes=[BF16], tm=ROW_TM, name=f"{tag}_datt")
    d_q = from_heads(dq_h)
    d_kv = jnp.concatenate([dk_h, dv_h], axis=-1).transpose(1, 0, 2).reshape(mem.shape[0], -1).astype(BF16)
    g_wq = pmm(hq, d_q, ta=True, out_dtype=GRAD_WIRE, name=f"{tag}_gwq")
    dhq = pmm(d_q, p['wq'], tb=True, name=f"{tag}_dhq")
    g_wkv = pmm(hm, d_kv, ta=True, out_dtype=GRAD_WIRE, name=f"{tag}_gwkv")
    dhm = pmm(d_kv, p['wkv'], tb=True, name=f"{tag}_dhm")
    _, g_mem_norm = _norm_bwd(mem, p['mem_norm'], dhm, None, f"{tag}_dmem_norm", want_row=False)
    dx, g_norm = _norm_bwd(x, p['norm'], dhq, dy, f"{tag}_dnorm")
    return dx, {'norm': g_norm, 'mem_norm': g_mem_norm, 'wq': g_wq, 'wkv': g_wkv, 'q_gain': g_qg.reshape(-1),
                'k_gain': g_kg.reshape(-1), 'wo': g_wo}


def rope_tables(positions):
    half = MLA_ROPE // 2
    inv_freq = ROPE_THETA ** (-jnp.arange(half, dtype=F32) / half)
    ang = positions.astype(F32)[:, None] * inv_freq
    return jnp.cos(ang), jnp.sin(ang)


def local_step(x, mem, positions, target, w):
    cos, sin = rope_tables(positions)

    def ffn_params(kind, layer):
        return {'norm': w[f'ffn_{kind}_norm'][layer], 'w_gu': w[f'ffn_{kind}_w_gu'][layer],
                'w_down': w[f'ffn_{kind}_w_down'][layer]}

    def xattn_params(layer):
        return {'norm': w['xmem_norm'][layer], 'mem_norm': w['xmem_mem_norm'][layer], 'wq': w['xmem_wq'][layer],
                'wkv': w['xmem_wkv'][layer], 'q_gain': w['xmem_q_gain'][layer], 'k_gain': w['xmem_k_gain'][layer],
                'wo': w['xmem_wo'][layer]}

    even_p = {'norm': w['mix_norm'][0], 'w_in': w['sbg_w_in'][0], 'ln_gain': w['sgu_ln_gain'][0],
              'ln_bias': w['sgu_ln_bias'][0], 'sgu_w': w['sgu_w'][0], 'sgu_b': w['sgu_b'][0],
              'w_out': w['sbg_w_out'][0]}
    mla_p = {'norm': w['mix_norm'][1], 'w_in': w['mla_w_in'][0], 'q_lora_gain': w['mla_q_lora_gain'][0],
             'kv_lora_gain': w['mla_kv_lora_gain'][0], 'w_uq': w['mla_w_uq'][0], 'w_ukv': w['mla_w_ukv'][0],
             'q_gain': w['mla_q_gain'][0], 'k_gain': w['mla_k_gain'][0], 'w_out': w['mla_w_out'][0]}

    saved = []
    for layer in range(DEPTH):
        x, s_pre = ffn_fwd(x, ffn_params('pre', layer), f"ffn_pre{layer}")
        if layer % 2 == 0:
            x, s_mix = even_mixer_fwd(x, even_p)
        else:
            x, s_mix = mla_fwd(x, cos, sin, mla_p)
        x, s_x = xattn_fwd(x, mem, xattn_params(layer), f"xmem{layer}")
        x, s_post = ffn_fwd(x, ffn_params('post', layer), f"ffn_post{layer}")
        saved.append((s_pre, s_mix, s_x, s_post))

    dx, loss = loss_head(x, target, tm=ROW_TM, name="loss_head")

    per_layer = []
    for layer in reversed(range(DEPTH)):
        s_pre, s_mix, s_x, s_post = saved[layer]
        dx, g_post = ffn_bwd(dx, ffn_params('post', layer), s_post, f"ffn_post{layer}")
        dx, g_x = xattn_bwd(dx, xattn_params(layer), s_x, f"xmem{layer}")
        if layer % 2 == 0:
            dx, g_mix = even_mixer_bwd(dx, even_p, s_mix)
        else:
            dx, g_mix = mla_bwd(dx, cos, sin, mla_p, s_mix)
        dx, g_pre = ffn_bwd(dx, ffn_params('pre', layer), s_pre, f"ffn_pre{layer}")
        per_layer.append((layer, g_pre, g_mix, g_x, g_post))
    per_layer.sort(key=lambda t: t[0])

    def stack(pick):
        return jnp.stack([pick(t) for t in per_layer])

    g_even, g_mla = per_layer[0][2], per_layer[1][2]
    grads = {
        'ffn_pre_norm': stack(lambda t: t[1]['norm']), 'ffn_pre_w_gu': stack(lambda t: t[1]['w_gu']),
        'ffn_pre_w_down': stack(lambda t: t[1]['w_down']), 'mix_norm': stack(lambda t: t[2]['norm']),
        'sbg_w_in': g_even['w_in'][None], 'sgu_ln_gain': g_even['ln_gain'][None], 'sgu_ln_bias': g_even['ln_bias'][None],
        'sgu_w': g_even['sgu_w'][None], 'sgu_b': g_even['sgu_b'][None], 'sbg_w_out': g_even['w_out'][None],
        'mla_w_in': g_mla['w_in'][None], 'mla_q_lora_gain': g_mla['q_lora_gain'][None],
        'mla_kv_lora_gain': g_mla['kv_lora_gain'][None], 'mla_w_uq': g_mla['w_uq'][None],
        'mla_w_ukv': g_mla['w_ukv'][None], 'mla_q_gain': g_mla['q_gain'][None], 'mla_k_gain': g_mla['k_gain'][None],
        'mla_w_out': g_mla['w_out'][None],
        'xmem_norm': stack(lambda t: t[3]['norm']), 'xmem_mem_norm': stack(lambda t: t[3]['mem_norm']),
        'xmem_wq': stack(lambda t: t[3]['wq']), 'xmem_wkv': stack(lambda t: t[3]['wkv']),
        'xmem_q_gain': stack(lambda t: t[3]['q_gain']), 'xmem_k_gain': stack(lambda t: t[3]['k_gain']),
        'xmem_wo': stack(lambda t: t[3]['wo']),
        'ffn_post_norm': stack(lambda t: t[4]['norm']), 'ffn_post_w_gu': stack(lambda t: t[4]['w_gu']),
        'ffn_post_w_down': stack(lambda t: t[4]['w_down']),
    }
    return loss, dx, grads


def _device_slot():
    x, y, c = _me()
    return 4 * x + 2 * y + c


def kernel(x, mem, positions, ffn_pre_norm, ffn_pre_w_gu, ffn_pre_w_down, mix_norm, sbg_w_in, sgu_ln_gain, sgu_ln_bias, sgu_w, sgu_b, sbg_w_out, mla_w_in, mla_q_lora_gain, mla_kv_lora_gain, mla_w_uq, mla_w_ukv, mla_q_gain, mla_k_gain, mla_w_out, xmem_norm, xmem_mem_norm, xmem_wq, xmem_wkv, xmem_q_gain, xmem_k_gain, xmem_wo, ffn_post_norm, ffn_post_w_gu, ffn_post_w_down, loss_target, m_ffn_pre_norm, m_ffn_pre_w_gu, m_ffn_pre_w_down, m_mix_norm, m_sbg_w_in, m_sgu_ln_gain, m_sgu_ln_bias, m_sgu_w, m_sgu_b, m_sbg_w_out, m_mla_w_in, m_mla_q_lora_gain, m_mla_kv_lora_gain, m_mla_w_uq, m_mla_w_ukv, m_mla_q_gain, m_mla_k_gain, m_mla_w_out, m_xmem_norm, m_xmem_mem_norm, m_xmem_wq, m_xmem_wkv, m_xmem_q_gain, m_xmem_k_gain, m_xmem_wo, m_ffn_post_norm, m_ffn_post_w_gu, m_ffn_post_w_down, v_ffn_pre_norm, v_ffn_pre_w_gu, v_ffn_pre_w_down, v_mix_norm, v_sbg_w_in, v_sgu_ln_gain, v_sgu_ln_bias, v_sgu_w, v_sgu_b, v_sbg_w_out, v_mla_w_in, v_mla_q_lora_gain, v_mla_kv_lora_gain, v_mla_w_uq, v_mla_w_ukv, v_mla_q_gain, v_mla_k_gain, v_mla_w_out, v_xmem_norm, v_xmem_mem_norm, v_xmem_wq, v_xmem_wkv, v_xmem_q_gain, v_xmem_k_gain, v_xmem_wo, v_ffn_post_norm, v_ffn_post_w_gu, v_ffn_post_w_down):
    args = locals()
    w_in = {n: args[n] for n in WEIGHTS}
    m_in = {n: args["m_" + n] for n in WEIGHTS}
    v_in = {n: args["v_" + n] for n in WEIGHTS}
    slot = _device_slot()

    tiny = jnp.zeros((8, LANES), F32)
    for i, src in enumerate((w_in, m_in, v_in)):
        tiny = tiny.at[i, :64].set(src['mla_q_lora_gain'][0]).at[i + 3, :32].set(src['mla_kv_lora_gain'][0])
    tiny_all = all_gather([tiny], name="gather_lora_gains")[0]
    full_small = []
    for i, src in enumerate((w_in, m_in, v_in)):
        d = {n: src[n] for n in SMALL}
        d['mla_q_lora_gain'] = tiny_all[:, i, :64].reshape(1, MLA_Q_LORA)
        d['mla_kv_lora_gain'] = tiny_all[:, i + 3, :32].reshape(1, MLA_KV_LORA)
        full_small.append(d)
    w_small, m_small, v_small = full_small
    small_shapes = {n: w_small[n].shape for n in SMALL}

    big = list(BIG)
    gathered = all_gather([w_in[n].astype(BF16) for n in big], name="gather_weights")
    w_full = dict(w_small)
    w_full.update({n: gathered_to_full(t, BIG[n]) for n, t in zip(big, gathered)})

    loss, dx, grads = local_step(x[0], mem[0], positions[0], loss_target[0], w_full)
    loss = lax.psum(loss, ("x", "y", "c"))

    parts = all_to_all([full_to_owner_major(grads[n], BIG[n]) for n in big], name="exchange_grads")
    big_out = {}
    for n, p in zip(big, parts):
        shard, two_d = w_in[n].shape, (-1, w_in[n].shape[-1])
        res = adamw(p, w_in[n].reshape(two_d), m_in[n].reshape(two_d), v_in[n].reshape(two_d), name=f"adamw_{n}")
        big_out[n] = [t.reshape(shard) for t in res]

    small_parts = all_gather([pack_small({n: grads[n] for n in SMALL}, small_shapes)], name="gather_small_grads")[0]
    small_out = adamw(small_parts, pack_small(w_small, small_shapes), pack_small(m_small, small_shapes),
                      pack_small(v_small, small_shapes), name="adamw_small")
    small_out = [unpack_small(t, small_shapes) for t in small_out]
    for d in small_out:
        for n, width in zip(GAIN_SHARDED, (64, 32)):
            d[n] = lax.dynamic_slice(d[n], (0, slot * width), (1, width))

    outs = [loss, dx[None]]
    for kind, small_d in enumerate(small_out):
        outs += [big_out[n][kind] if n in BIG else small_d[n] for n in WEIGHTS]
    return tuple(outs)
```

```python
import functools

import jax
import jax.numpy as jnp
from jax import lax
from jax.experimental import pallas as pl
from jax.experimental.pallas import tpu as pltpu

F32 = jnp.float32
BF16 = jnp.bfloat16
MESH = pl.DeviceIdType.MESH
N_DEV = 8

VMEM_LIMIT_BYTES = 56 * 1024 * 1024
LANES = 128

D_MODEL = 1024
DEPTH = 2
D_FF = 2816
EPS = 1e-6
SB_HEADS, SB_HEAD_DIM = 8, 64
SB_WIDTH = SB_HEADS * SB_HEAD_DIM
SG_GROUPS, SG_GROUP_DIM, SG_CHUNK = 8, 64, 128
SG_WIDTH = SG_GROUPS * SG_GROUP_DIM
MLA_HEADS, MLA_NOPE, MLA_ROPE, MLA_V = 16, 64, 32, 64
MLA_QK = MLA_NOPE + MLA_ROPE
MLA_Q_LORA, MLA_KV_LORA = 512, 256
ROPE_THETA = 10000.0
MEM_HEADS = 4
MEM_HEAD_DIM = D_MODEL // MEM_HEADS

ADAM_LR, ADAM_B1, ADAM_B2, ADAM_EPS, ADAM_WD, ADAM_STEP = 0.001, 0.9, 0.999, 1e-08, 0.01, 10

WEIGHTS = ['ffn_pre_norm', 'ffn_pre_w_gu', 'ffn_pre_w_down', 'mix_norm', 'sbg_w_in', 'sgu_ln_gain', 'sgu_ln_bias',
           'sgu_w', 'sgu_b', 'sbg_w_out', 'mla_w_in', 'mla_q_lora_gain', 'mla_kv_lora_gain', 'mla_w_uq', 'mla_w_ukv',
           'mla_q_gain', 'mla_k_gain', 'mla_w_out', 'xmem_norm', 'xmem_mem_norm', 'xmem_wq', 'xmem_wkv',
           'xmem_q_gain', 'xmem_k_gain', 'xmem_wo', 'ffn_post_norm', 'ffn_post_w_gu', 'ffn_post_w_down']
BIG = {'ffn_pre_w_gu': 2, 'ffn_pre_w_down': 1, 'sbg_w_in': 2, 'sbg_w_out': 1, 'mla_w_in': 1, 'mla_w_uq': 2,
       'mla_w_ukv': 2, 'mla_w_out': 1, 'xmem_wq': 1, 'xmem_wkv': 2, 'xmem_wo': 1, 'ffn_post_w_gu': 2,
       'ffn_post_w_down': 1}
GAIN_SHARDED = ('mla_q_lora_gain', 'mla_kv_lora_gain')
SMALL = [n for n in WEIGHTS if n not in BIG]
GRAD_WIRE = BF16
SMALL_ROW_MULTIPLE = 16


def _cparams(sem=None):
    return pltpu.CompilerParams(dimension_semantics=sem, vmem_limit_bytes=VMEM_LIMIT_BYTES)


MM_TILE_CAP = 1408


def _pick(dim, cap=MM_TILE_CAP):
    if dim % LANES:
        return dim
    return max(t for t in range(LANES, min(dim, cap) + 1, LANES) if dim % t == 0)


def _rms(x, g):
    return x * lax.rsqrt(jnp.mean(x * x, axis=-1, keepdims=True) + EPS) * g


def pmm(a, b, *, ta=False, tb=False, out_dtype=F32, res=None, alpha=1.0, k_off_b=0, norm_out=None, norm_bwd=None,
        name):
    kdim, m = (a.shape if ta else a.shape[::-1])
    n = b.shape[0] if tb else b.shape[1]
    tm, tn, tk = _pick(m), _pick(n), _pick(kdim)
    whole_rows = norm_out is not None or norm_bwd is not None
    if whole_rows:
        tm = min(tm, 512)
        assert tn == n
    nk = kdim // tk
    dims = (((0 if ta else 1,), (1 if tb else 0,)), ((), ()))
    n_extra = (res is not None) + (norm_out is not None) + (0 if norm_bwd is None else 2 + (norm_bwd[2] is not None))

    def body(*refs):
        a_ref, b_ref = refs[:2]
        extra = list(refs[2:2 + n_extra])
        outs, acc_ref = refs[2 + n_extra:-1], refs[-1]
        i, k = pl.program_id(0), pl.program_id(2)

        @pl.when(k == 0)
        def _():
            acc_ref[...] = jnp.zeros_like(acc_ref)

        acc_ref[...] += lax.dot_general(a_ref[...].astype(BF16), b_ref[...].astype(BF16), dims,
                                        preferred_element_type=F32)

        @pl.when(k == nk - 1)
        def _():
            r = acc_ref[...]
            if alpha != 1.0:
                r = r * alpha
            if res is not None:
                r = extra.pop(0)[...] + r
            if norm_bwd is None:
                outs[0][...] = r.astype(out_dtype)
            if norm_out is not None:
                outs[1][...] = _rms(r, extra.pop(0)[...]).astype(BF16)
            if norm_bwd is not None:
                x_ref, g_ref = extra.pop(0), extra.pop(0)
                _, pull = jax.vjp(_rms, x_ref[...], g_ref[...])
                dx, dg = pull(r)
                if norm_bwd[2] is not None:
                    dx = dx + extra.pop(0)[...]
                outs[0][...] = dx

                @pl.when(i == 0)
                def _():
                    outs[1][...] = dg

                @pl.when(i != 0)
                def _():
                    outs[1][...] += dg

    a_spec = pl.BlockSpec((tk, tm), lambda i, j, k: (k, i)) if ta else pl.BlockSpec((tm, tk), lambda i, j, k: (i, k))
    b_spec = (pl.BlockSpec((tn, tk), lambda i, j, k: (j, k + k_off_b)) if tb
              else pl.BlockSpec((tk, tn), lambda i, j, k: (k + k_off_b, j)))
    o_spec = pl.BlockSpec((tm, tn), lambda i, j, k: (i, j))
    g_spec = pl.BlockSpec((1, tn), lambda i, j, k: (0, 0))
    ins, in_specs = [a, b], [a_spec, b_spec]
    if res is not None:
        ins.append(res)
        in_specs.append(o_spec)
    out_shape, out_specs = [jax.ShapeDtypeStruct((m, n), out_dtype)], [o_spec]
    if norm_out is not None:
        ins.append(norm_out)
        in_specs.append(g_spec)
        out_shape.append(jax.ShapeDtypeStruct((m, n), BF16))
        out_specs.append(o_spec)
    if norm_bwd is not None:
        ins += [t for t in norm_bwd if t is not None]
        in_specs += [o_spec, g_spec] + ([o_spec] if norm_bwd[2] is not None else [])
        out_shape = [jax.ShapeDtypeStruct((m, n), F32), jax.ShapeDtypeStruct((1, n), F32)]
        out_specs = [o_spec, g_spec]
    result = pl.pallas_call(
        body, name=name, grid=(m // tm, n // tn, nk), in_specs=in_specs, out_specs=out_specs,
        out_shape=out_shape, scratch_shapes=[pltpu.VMEM((tm, tn), F32)],
        compiler_params=_cparams(("arbitrary" if norm_bwd is not None else "parallel", "parallel", "arbitrary")),
    )(*ins)
    return result if whole_rows else result[0]


def ffn_gate_up(h, w_gu, *, name):
    m, kdim = h.shape
    n = w_gu.shape[1] // 2
    tm, tn = min(_pick(m), 512), _pick(n)
    up_off = n // tn

    def body(a_ref, bg_ref, bu_ref, gate_ref, up_ref, act_ref):
        av = a_ref[...].astype(BF16)
        gate = _dg(av, bg_ref[...].astype(BF16), 1, 0)
        up = _dg(av, bu_ref[...].astype(BF16), 1, 0)
        gate_ref[...] = gate
        up_ref[...] = up
        act_ref[...] = (jax.nn.silu(gate) * up).astype(BF16)

    o_spec = pl.BlockSpec((tm, tn), lambda i, j: (i, j))
    return pl.pallas_call(
        body, name=name, grid=(m // tm, n // tn),
        in_specs=[pl.BlockSpec((tm, kdim), lambda i, j: (i, 0)), pl.BlockSpec((kdim, tn), lambda i, j: (0, j)),
                  pl.BlockSpec((kdim, tn), lambda i, j: (0, j + up_off))],
        out_specs=[o_spec] * 3,
        out_shape=[jax.ShapeDtypeStruct((m, n), F32), jax.ShapeDtypeStruct((m, n), F32),
                   jax.ShapeDtypeStruct((m, n), BF16)],
        compiler_params=_cparams(("parallel", "parallel")),
    )(h, w_gu, w_gu)


def ffn_gate_up_bwd(dy, w_down, gate, up, *, alpha, name):
    m, kdim = dy.shape
    n = w_down.shape[0]
    tm, tn = min(_pick(m), 512), _pick(n)

    def body(a_ref, b_ref, gate_ref, up_ref, dgate_ref, dup_ref):
        d_act = _dg(a_ref[...].astype(BF16), b_ref[...].astype(BF16), 1, 1) * alpha
        _, pull = jax.vjp(lambda g, u: jax.nn.silu(g) * u, gate_ref[...], up_ref[...])
        d_gate, d_up = pull(d_act)
        dgate_ref[...] = d_gate.astype(BF16)
        dup_ref[...] = d_up.astype(BF16)

    o_spec = pl.BlockSpec((tm, tn), lambda i, j: (i, j))
    return pl.pallas_call(
        body, name=name, grid=(m // tm, n // tn),
        in_specs=[pl.BlockSpec((tm, kdim), lambda i, j: (i, 0)), pl.BlockSpec((tn, kdim), lambda i, j: (j, 0)),
                  o_spec, o_spec],
        out_specs=[o_spec] * 2, out_shape=[jax.ShapeDtypeStruct((m, n), BF16)] * 2,
        compiler_params=_cparams(("parallel", "parallel")),
    )(dy, w_down, gate, up)


def _dg(a, b, ca, cb):
    return lax.dot_general(a, b, (((ca,), (cb,)), ((), ())), preferred_element_type=F32)


@jax.custom_vjp
def bdot(a, b):
    return _dg(a.astype(BF16), b.astype(BF16), 1, 0)


def _bdot_fwd(a, b):
    ab, bb = a.astype(BF16), b.astype(BF16)
    return _dg(ab, bb, 1, 0), (ab, bb)


def _bdot_bwd(saved, g):
    ab, bb = saved
    gb = g.astype(BF16)
    return _dg(gb, bb, 1, 1), _dg(ab, gb, 0, 0)


bdot.defvjp(_bdot_fwd, _bdot_bwd)


@jax.custom_vjp
def bdot_nt(a, b):
    return _dg(a.astype(BF16), b.astype(BF16), 1, 1)


def _bdot_nt_fwd(a, b):
    ab, bb = a.astype(BF16), b.astype(BF16)
    return _dg(ab, bb, 1, 1), (ab, bb)


def _bdot_nt_bwd(saved, g):
    ab, bb = saved
    gb = g.astype(BF16)
    return _dg(gb, bb, 1, 0), _dg(gb, ab, 0, 0)


bdot_nt.defvjp(_bdot_nt_fwd, _bdot_nt_bwd)


def _row_spec(arr, tm):
    if arr.ndim == 3:
        return pl.BlockSpec((None, tm, arr.shape[2]), lambda r, g: (g, r, 0))
    return pl.BlockSpec((tm, arr.shape[1]), lambda r, g: (r, 0))


def _gparam_spec(arr):
    return pl.BlockSpec((None,) + arr.shape[1:], lambda r, g: (g, 0, 0))


def _whole_spec(arr):
    nd = arr.ndim
    return pl.BlockSpec(arr.shape, lambda r, g: (0,) * nd)


def _groups(rows, gparams):
    gs = {a.shape[0] for a in rows if a.ndim == 3} | {a.shape[0] for a in gparams}
    assert len(gs) <= 1
    return gs.pop() if gs else 1


def prow(fn, rows, gparams=(), params=(), *, outs, tm, name):
    rows, gparams, params = list(rows), list(gparams), list(params)
    n_groups = _groups(rows, gparams)
    n_rows = rows[0].shape[-2]
    n_in = len(rows) + len(gparams) + len(params)

    def body(*refs):
        vals = [r[...] for r in refs[:n_in]]
        res = fn(*vals)
        for o_ref, r in zip(refs[n_in:], res, strict=True):
            o_ref[...] = r.astype(o_ref.dtype)

    out_shape, out_specs = [], []
    for width, dtype, grouped in outs:
        shp = (n_groups, n_rows, width) if grouped else (n_rows, width)
        out_shape.append(jax.ShapeDtypeStruct(shp, dtype))
        out_specs.append(_row_spec(out_shape[-1], tm))
    return pl.pallas_call(
        body, name=name, grid=(n_rows // tm, n_groups),
        in_specs=[_row_spec(a, tm) for a in rows] + [_gparam_spec(a) for a in gparams] + [_whole_spec(a) for a in params],
        out_specs=out_specs, out_shape=out_shape,
        compiler_params=_cparams(("parallel", "arbitrary")),
    )(*rows, *gparams, *params)


def prow_vjp(fn, rows, gparams=(), params=(), *, cts, row_grad, adds=None, row_dtypes=None, gparam_grad=None,
             param_grad=None, tm, name):
    rows, gparams, params, cts = list(rows), list(gparams), list(params), list(cts)
    gparam_grad = list(gparam_grad) if gparam_grad is not None else [True] * len(gparams)
    param_grad = list(param_grad) if param_grad is not None else [True] * len(params)
    n_groups = _groups(rows + cts, gparams)
    n_rows = rows[0].shape[-2]
    want_rows = [i for i, w in enumerate(row_grad) if w]
    adds = list(adds) if adds is not None else [None] * len(want_rows)
    row_dtypes = list(row_dtypes) if row_dtypes is not None else [F32] * len(want_rows)
    add_arrays = [a for a in adds if a is not None]
    n_r, n_g, n_p, n_c, n_a = len(rows), len(gparams), len(params), len(cts), len(add_arrays)
    mask = list(row_grad) + gparam_grad + param_grad

    def body(*refs):
        r_id, g_id = pl.program_id(0), pl.program_id(1)
        n_in = n_r + n_g + n_p
        vals = [r[...] for r in refs[:n_in]]
        ct_vals = tuple(r[...].astype(F32) for r in refs[n_in:n_in + n_c])
        add_refs = list(refs[n_in + n_c:n_in + n_c + n_a])
        out_refs = list(refs[n_in + n_c + n_a:])
        diff_idx = [i for i, w in enumerate(mask) if w]

        def wrapped(*diff):
            full = list(vals)
            for i, d in zip(diff_idx, diff):
                full[i] = d
            return tuple(fn(*full))

        _, pull = jax.vjp(wrapped, *[vals[i].astype(F32) for i in diff_idx])
        grads = dict(zip(diff_idx, pull(ct_vals)))
        k = 0
        for j, i in enumerate(want_rows):
            o_ref = out_refs[k]
            k += 1
            gval = grads[i]
            if adds[j] is not None:
                gval = gval + add_refs.pop(0)[...].astype(F32)
            if rows[i].ndim == 2 and n_groups > 1:
                @pl.when(g_id == 0)
                def _(o_ref=o_ref, gval=gval):
                    o_ref[...] = gval.astype(o_ref.dtype)

                @pl.when(g_id != 0)
                def _(o_ref=o_ref, gval=gval):
                    o_ref[...] += gval.astype(o_ref.dtype)
            else:
                o_ref[...] = gval.astype(o_ref.dtype)
        for i in range(n_g):
            if not gparam_grad[i]:
                continue
            o_ref = out_refs[k]
            k += 1
            gval = grads[n_r + i]

            @pl.when(r_id == 0)
            def _(o_ref=o_ref, gval=gval):
                o_ref[g_id] = gval

            @pl.when(r_id != 0)
            def _(o_ref=o_ref, gval=gval):
                o_ref[g_id] += gval
        for i in range(n_p):
            if not param_grad[i]:
                continue
            o_ref = out_refs[k]
            k += 1
            gval = grads[n_r + n_g + i]
            first = jnp.logical_and(r_id == 0, g_id == 0)

            @pl.when(first)
            def _(o_ref=o_ref, gval=gval):
                o_ref[...] = gval

            @pl.when(jnp.logical_not(first))
            def _(o_ref=o_ref, gval=gval):
                o_ref[...] += gval

    out_shape, out_specs = [], []
    for j, i in enumerate(want_rows):
        out_shape.append(jax.ShapeDtypeStruct(rows[i].shape, row_dtypes[j]))
        out_specs.append(_row_spec(rows[i], tm))
    for i in range(n_g):
        if gparam_grad[i]:
            out_shape.append(jax.ShapeDtypeStruct(gparams[i].shape, F32))
            out_specs.append(_whole_spec(gparams[i]))
    for i in range(n_p):
        if param_grad[i]:
            out_shape.append(jax.ShapeDtypeStruct(params[i].shape, F32))
            out_specs.append(_whole_spec(params[i]))
    return pl.pallas_call(
        body, name=name, grid=(n_rows // tm, n_groups),
        in_specs=([_row_spec(a, tm) for a in rows] + [_gparam_spec(a) for a in gparams]
                  + [_whole_spec(a) for a in params] + [_row_spec(a, tm) for a in cts]
                  + [_row_spec(a, tm) for a in add_arrays]),
        out_specs=out_specs, out_shape=out_shape,
        compiler_params=_cparams(("arbitrary", "arbitrary")),
    )(*rows, *gparams, *params, *cts, *add_arrays)


def f_rms(x, g):
    return (_rms(x.astype(F32), g),)


def f_gate_prep(z, ln_g, ln_b):
    act = jax.nn.gelu(z)
    u, gg = act[:, :SG_WIDTH], act[:, SG_WIDTH:]
    mu = jnp.mean(gg, axis=-1, keepdims=True)
    var = jnp.mean(jnp.square(gg - mu), axis=-1, keepdims=True)
    return u, (gg - mu) * lax.rsqrt(var + EPS) * ln_g + ln_b


def f_spatial_gate(gn, u, w, b):
    t = lax.broadcasted_iota(jnp.int32, w.shape, 0)
    s = lax.broadcasted_iota(jnp.int32, w.shape, 1)
    w_causal = jnp.where(s <= t, w, 0.0)
    mixed = [bdot(w_causal, gn[i:i + SG_CHUNK]) + b for i in range(0, gn.shape[0], SG_CHUNK)]
    return (u * (mixed[0] if len(mixed) == 1 else jnp.concatenate(mixed, axis=0)),)


def _rope_tail(t, cos, sin):
    half = MLA_ROPE // 2
    t1, t2 = t[:, MLA_NOPE:MLA_NOPE + half], t[:, MLA_NOPE + half:]
    return jnp.concatenate([t[:, :MLA_NOPE], t1 * cos - t2 * sin, t1 * sin + t2 * cos], axis=-1)


def f_mla_q(q, cos, sin, g):
    return (_rope_tail(f_rms(q, g)[0], cos, sin),)


def f_mla_k(k_nope, k_r, cos, sin, g):
    return (_rope_tail(f_rms(jnp.concatenate([k_nope, k_r], axis=-1), g)[0], cos, sin),)


def f_xattn(q, k, v, q_g, k_g):
    qn, kn = f_rms(q, q_g)[0], f_rms(k, k_g)[0]
    sc = bdot_nt(qn, kn) * (MEM_HEAD_DIM ** -0.5)
    return (bdot(jax.nn.softmax(sc, axis=-1), v),)


def _split_dot(x, tri):
    hi = x.astype(BF16)
    lo = (x - hi.astype(F32)).astype(BF16)
    return _dg(hi, tri, 1, 0) + _dg(lo, tri, 1, 0)


def _tri(tk, cmp):
    j = lax.broadcasted_iota(jnp.int32, (tk, tk), 0)
    s = lax.broadcasted_iota(jnp.int32, (tk, tk), 1)
    return cmp(j, s).astype(BF16)


SCAN_CHUNK = 256


def _row_scan(x, tri, reverse):
    n = x.shape[1] // SCAN_CHUNK
    chunks = [x[:, i * SCAN_CHUNK:(i + 1) * SCAN_CHUNK] for i in range(n)]
    out, seen = [None] * n, None
    for i in (reversed(range(n)) if reverse else range(n)):
        local = _split_dot(chunks[i], tri)
        out[i] = local if seen is None else local + seen
        total = jnp.sum(chunks[i], axis=1, keepdims=True)
        seen = total if seen is None else seen + total
    return (out[0] if n == 1 else jnp.concatenate(out, axis=1)), seen


def _att_specs(s_len, tq, dq, dv):
    q_spec = pl.BlockSpec((None, tq, dq), lambda h, i: (h, i, 0))
    k_spec = pl.BlockSpec((None, s_len, dq), lambda h, i: (h, 0, 0))
    v_spec = pl.BlockSpec((None, s_len, dv), lambda h, i: (h, 0, 0))
    o_spec = pl.BlockSpec((None, tq, dv), lambda h, i: (h, i, 0))
    r_spec = pl.BlockSpec((None, tq, 1), lambda h, i: (h, i, 0))
    return q_spec, k_spec, v_spec, o_spec, r_spec


def _key_blocks(qi, tq, tk):
    return (qi * tq) // tk, ((qi + 1) * tq + tk - 1) // tk


def _keep(qi, j, tq, tk, strict):
    row = qi * tq + lax.broadcasted_iota(jnp.int32, (tq, tk), 0)
    col = j * tk + lax.broadcasted_iota(jnp.int32, (tq, tk), 1)
    return col < row if strict else col <= row


def _log_sigmoid(z):
    return jnp.minimum(z, 0.0) - jnp.log(1.0 + jnp.exp(-jnp.abs(z)))


def sb_fwd(q, k, v, *, tq, tk, name):
    n_heads, s_len, d = q.shape
    scale = SB_HEAD_DIM ** -0.5

    def body(q_ref, k_ref, v_ref, o_ref, tot_ref):
        qi = pl.program_id(1)
        qv = q_ref[...]
        upper = _tri(SCAN_CHUNK, lambda j, s: j > s)
        n_full, n_all = _key_blocks(qi, tq, tk)

        def make_step(masked, last):
            def step(jj, carry):
                acc, rest = carry
                j = last - 1 - jj
                sl = pl.ds(pl.multiple_of(j * tk, tk), tk)
                ks, vs = k_ref[sl, :], v_ref[sl, :]
                z = _dg(qv, ks, 1, 1) * scale
                log_beta = _log_sigmoid(z)
                log_stay = log_beta - z
                if masked:
                    valid = _keep(qi, j, tq, tk, True)
                    log_stay = jnp.where(valid, log_stay, 0.0)
                after, total = _row_scan(log_stay, upper, True)
                w = jnp.exp(log_beta + after + rest)
                if masked:
                    w = jnp.where(valid, w, 0.0)
                acc = acc + _dg(w.astype(BF16), vs, 1, 0)
                return acc, rest + total
            return step

        carry = (jnp.zeros((tq, d), F32), jnp.zeros((tq, 1), F32))
        carry = lax.fori_loop(0, n_all - n_full, make_step(True, n_all), carry)
        acc, rest = lax.fori_loop(0, n_full, make_step(False, n_full), carry)
        o_ref[...] = acc
        tot_ref[...] = rest

    q_spec, k_spec, v_spec, o_spec, r_spec = _att_specs(s_len, tq, d, d)
    return pl.pallas_call(
        body, name=name, grid=(n_heads, s_len // tq), in_specs=[q_spec, k_spec, v_spec],
        out_specs=[o_spec, r_spec],
        out_shape=[jax.ShapeDtypeStruct((n_heads, s_len, d), F32), jax.ShapeDtypeStruct((n_heads, s_len, 1), F32)],
        compiler_params=_cparams(("parallel", "arbitrary")),
    )(q, k, v)


def sb_bwd(q, k, v, tot, do, *, tq, tk, name):
    n_heads, s_len, d = q.shape
    scale = SB_HEAD_DIM ** -0.5

    def body(q_ref, k_ref, v_ref, tot_ref, do_ref, dq_ref, dk_ref, dv_ref):
        qi = pl.program_id(1)

        @pl.when(qi == 0)
        def _():
            dk_ref[...] = jnp.zeros_like(dk_ref)
            dv_ref[...] = jnp.zeros_like(dv_ref)

        qv = q_ref[...]
        dob = do_ref[...].astype(BF16)
        total = tot_ref[...]
        incl = _tri(SCAN_CHUNK, lambda j, s: j <= s)
        excl = _tri(SCAN_CHUNK, lambda j, s: j < s)
        n_full, n_all = _key_blocks(qi, tq, tk)

        def make_step(masked):
            def step(j, carry):
                dq, stay_before, dl_before = carry
                sl = pl.ds(pl.multiple_of(j * tk, tk), tk)
                ks, vs = k_ref[sl, :], v_ref[sl, :]
                z = _dg(qv, ks, 1, 1) * scale
                log_beta = _log_sigmoid(z)
                log_stay = log_beta - z
                if masked:
                    valid = _keep(qi, j, tq, tk, True)
                    log_stay = jnp.where(valid, log_stay, 0.0)
                stay_upto, stay_sum = _row_scan(log_stay, incl, False)
                w = jnp.exp(log_beta + (total - stay_before) - stay_upto)
                if masked:
                    w = jnp.where(valid, w, 0.0)
                dl = _dg(dob, vs, 1, 1) * w
                dl_upto, dl_sum = _row_scan(dl, excl, False)
                dl_prefix = dl_upto + dl_before
                beta = jnp.exp(log_beta)
                dz = (dl * (1.0 - beta) - beta * dl_prefix) * scale
                if masked:
                    dz = jnp.where(valid, dz, 0.0)
                dzb = dz.astype(BF16)
                dq = dq + _dg(dzb, ks, 1, 0)
                dk_ref[sl, :] += _dg(dzb, qv, 0, 0)
                dv_ref[sl, :] += _dg(w.astype(BF16), dob, 0, 0)
                return dq, stay_before + stay_sum, dl_before + dl_sum
            return step

        zero = jnp.zeros((tq, 1), F32)
        carry = lax.fori_loop(0, n_full, make_step(False), (jnp.zeros((tq, d), F32), zero, zero))
        dq, _, _ = lax.fori_loop(n_full, n_all, make_step(True), carry)
        dq_ref[...] = dq

    q_spec, k_spec, v_spec, o_spec, r_spec = _att_specs(s_len, tq, d, d)
    shp = jax.ShapeDtypeStruct((n_heads, s_len, d), F32)
    return pl.pallas_call(
        body, name=name, grid=(n_heads, s_len // tq), in_specs=[q_spec, k_spec, v_spec, r_spec, o_spec],
        out_specs=[q_spec, k_spec, v_spec], out_shape=[shp, shp, shp],
        compiler_params=_cparams(("arbitrary", "arbitrary")),
    )(q, k, v, tot, do)


NEG_BIG = -1e30


def sm_fwd(q, k, v, *, tq, tk, name):
    n_heads, s_len, dq = q.shape
    dv = v.shape[2]
    scale = dq ** -0.5

    def body(q_ref, k_ref, v_ref, o_ref, lse_ref):
        qi = pl.program_id(1)
        qv = q_ref[...]
        n_full, n_all = _key_blocks(qi, tq, tk)

        def make_step(masked):
            def step(j, carry):
                acc, m, l = carry
                sl = pl.ds(pl.multiple_of(j * tk, tk), tk)
                ks, vs = k_ref[sl, :], v_ref[sl, :]
                sc = _dg(qv, ks, 1, 1) * scale
                if masked:
                    sc = jnp.where(_keep(qi, j, tq, tk, False), sc, NEG_BIG)
                m_new = jnp.maximum(m, jnp.max(sc, axis=1, keepdims=True))
                p = jnp.exp(sc - m_new)
                fade = jnp.exp(m - m_new)
                return (fade * acc + _dg(p.astype(BF16), vs, 1, 0), m_new,
                        fade * l + jnp.sum(p, axis=1, keepdims=True))
            return step

        carry = (jnp.zeros((tq, dv), F32), jnp.full((tq, 1), NEG_BIG, F32), jnp.zeros((tq, 1), F32))
        carry = lax.fori_loop(0, n_full, make_step(False), carry)
        acc, m, l = lax.fori_loop(n_full, n_all, make_step(True), carry)
        o_ref[...] = acc / l
        lse_ref[...] = m + jnp.log(l)

    q_spec, k_spec, v_spec, o_spec, r_spec = _att_specs(s_len, tq, dq, dv)
    return pl.pallas_call(
        body, name=name, grid=(n_heads, s_len // tq), in_specs=[q_spec, k_spec, v_spec],
        out_specs=[o_spec, r_spec],
        out_shape=[jax.ShapeDtypeStruct((n_heads, s_len, dv), F32), jax.ShapeDtypeStruct((n_heads, s_len, 1), F32)],
        compiler_params=_cparams(("parallel", "arbitrary")),
    )(q, k, v)


def sm_bwd(q, k, v, o, lse, do, *, tq, tk, name):
    n_heads, s_len, dq = q.shape
    dv = v.shape[2]
    scale = dq ** -0.5

    def body(q_ref, k_ref, v_ref, o_ref, lse_ref, do_ref, dq_ref, dk_ref, dv_ref):
        qi = pl.program_id(1)

        @pl.when(qi == 0)
        def _():
            dk_ref[...] = jnp.zeros_like(dk_ref)
            dv_ref[...] = jnp.zeros_like(dv_ref)

        qv = q_ref[...]
        do = do_ref[...]
        dob = do.astype(BF16)
        delta = jnp.sum(do * o_ref[...], axis=1, keepdims=True)
        lse_v = lse_ref[...]
        n_full, n_all = _key_blocks(qi, tq, tk)

        def make_step(masked):
            def step(j, dq_acc):
                sl = pl.ds(pl.multiple_of(j * tk, tk), tk)
                ks, vs = k_ref[sl, :], v_ref[sl, :]
                p = jnp.exp(_dg(qv, ks, 1, 1) * scale - lse_v)
                if masked:
                    p = jnp.where(_keep(qi, j, tq, tk, False), p, 0.0)
                dv_ref[sl, :] += _dg(p.astype(BF16), dob, 0, 0)
                ds = (p * (_dg(dob, vs, 1, 1) - delta) * scale).astype(BF16)
                dk_ref[sl, :] += _dg(ds, qv, 0, 0)
                return dq_acc + _dg(ds, ks, 1, 0)
            return step

        dq_acc = lax.fori_loop(0, n_full, make_step(False), jnp.zeros((tq, dq), F32))
        dq_ref[...] = lax.fori_loop(n_full, n_all, make_step(True), dq_acc)

    q_spec, k_spec, v_spec, o_spec, r_spec = _att_specs(s_len, tq, dq, dv)
    return pl.pallas_call(
        body, name=name, grid=(n_heads, s_len // tq),
        in_specs=[q_spec, k_spec, v_spec, o_spec, r_spec, o_spec], out_specs=[q_spec, k_spec, v_spec],
        out_shape=[jax.ShapeDtypeStruct((n_heads, s_len, dq), F32), jax.ShapeDtypeStruct((n_heads, s_len, dq), F32),
                   jax.ShapeDtypeStruct((n_heads, s_len, dv), F32)],
        compiler_params=_cparams(("arbitrary", "arbitrary")),
    )(q, k, v, o, lse, do)


def loss_head(y, target, *, tm, name):
    n_rows, width = y.shape

    def body(y_ref, t_ref, dy_ref, loss_ref):
        diff = y_ref[...] - t_ref[...]
        dy_ref[...] = diff / width
        part = 0.5 * jnp.sum(jnp.mean(diff * diff, axis=-1, keepdims=True), axis=0, keepdims=True)

        @pl.when(pl.program_id(0) == 0)
        def _():
            loss_ref[...] = jnp.zeros_like(loss_ref)

        loss_ref[...] += jnp.broadcast_to(part, loss_ref.shape)

    spec = pl.BlockSpec((tm, width), lambda r: (r, 0))
    dy, loss = pl.pallas_call(
        body, name=name, grid=(n_rows // tm,), in_specs=[spec, spec],
        out_specs=[spec, pl.BlockSpec((8, LANES), lambda r: (0, 0))],
        out_shape=[jax.ShapeDtypeStruct(y.shape, F32), jax.ShapeDtypeStruct((8, LANES), F32)],
        compiler_params=_cparams(("arbitrary",)),
    )(y, target)
    return dy, loss[0, 0]


ADAM_TILE_ELEMS = 256 * 1024


def _adam_rows(n_rows, width):
    fits = [t for t in range(16, n_rows + 1, 16) if n_rows % t == 0 and t * width <= ADAM_TILE_ELEMS]
    return max(fits) if fits else n_rows


def adamw(parts, w, m, v, *, name):
    n_rows, width = w.shape
    tm = _adam_rows(n_rows, width)

    def body(p_ref, w_ref, m_ref, v_ref, g_ref, d_ref, nm_ref, nv_ref):
        g = p_ref[0].astype(F32)
        for i in range(1, N_DEV):
            g = g + p_ref[i].astype(F32)
        m_new = ADAM_B1 * m_ref[...] + (1.0 - ADAM_B1) * g
        v_new = ADAM_B2 * v_ref[...] + (1.0 - ADAM_B2) * jnp.square(g)
        m_hat = m_new / (1.0 - ADAM_B1 ** ADAM_STEP)
        v_hat = v_new / (1.0 - ADAM_B2 ** ADAM_STEP)
        g_ref[...] = g
        d_ref[...] = -ADAM_LR * (m_hat / (jnp.sqrt(v_hat) + ADAM_EPS) + ADAM_WD * w_ref[...])
        nm_ref[...] = m_new
        nv_ref[...] = v_new

    spec = pl.BlockSpec((tm, width), lambda r: (r, 0))
    shp = jax.ShapeDtypeStruct(w.shape, F32)
    return pl.pallas_call(
        body, name=name, grid=(n_rows // tm,),
        in_specs=[pl.BlockSpec((N_DEV, tm, width), lambda r: (0, r, 0)), spec, spec, spec],
        out_specs=[spec] * 4, out_shape=[shp] * 4, compiler_params=_cparams(("parallel",)),
    )(parts, w, m, v)


def _me():
    return lax.axis_index("x"), lax.axis_index("y"), lax.axis_index("c")


N_PEERS = N_DEV - 1


def _comm_call(body, ins, out_shape, name):
    n = len(ins)
    hbm = pl.BlockSpec(memory_space=pl.ANY)
    return pl.pallas_call(
        body, name=name, out_shape=out_shape, in_specs=[hbm] * n, out_specs=[hbm] * n,
        scratch_shapes=[pltpu.SemaphoreType.DMA((N_PEERS * n,)), pltpu.SemaphoreType.DMA((N_PEERS * n,)),
                        pltpu.SemaphoreType.DMA((n,))],
    )(*ins)


def all_gather(blocks, *, name):
    n = len(blocks)

    def body(*refs):
        x_refs, out_refs = refs[:n], refs[n:2 * n]
        send_sems, recv_sems, local_sems = refs[2 * n:]
        x, y, c = _me()
        me, sibling = (x, y, c), (x, y, 1 - c)
        chips = [(1 - x, y), (x, 1 - y), (1 - x, 1 - y)]

        def slot(i, px, py, pc):
            return out_refs[i].at[4 * px + 2 * py + pc]

        def copy(i, k, blk, to, src=None):
            return pltpu.make_async_remote_copy(
                src_ref=slot(i, *blk) if src is None else src, dst_ref=slot(i, *blk),
                send_sem=send_sems.at[N_PEERS * i + k], recv_sem=recv_sems.at[N_PEERS * i + k], device_id=to,
                device_id_type=MESH)

        mine = [pltpu.make_async_copy(x_refs[i], slot(i, *me), local_sems.at[i]) for i in range(n)]
        first = []
        for i in range(n):
            first.append(copy(i, 0, me, sibling, src=x_refs[i]))
            first += [copy(i, 1 + j, me, (*chip, c), src=x_refs[i]) for j, chip in enumerate(chips)]
        for cp in mine + first:
            cp.start()
        passed = []
        for j, chip in enumerate(chips):
            for i in range(n):
                copy(i, 1 + j, (*chip, c), me).wait_recv()
                passed.append(copy(i, 4 + j, (*chip, c), sibling))
                passed[-1].start()
        for i in range(n):
            copy(i, 0, sibling, me).wait_recv()
            for j, chip in enumerate(chips):
                copy(i, 4 + j, (*chip, 1 - c), me).wait_recv()
        for cp in first + passed:
            cp.wait_send()
        for cp in mine:
            cp.wait()

    return _comm_call(body, blocks, [jax.ShapeDtypeStruct((N_DEV,) + b.shape, b.dtype) for b in blocks], name)


def all_to_all(parts, *, name):
    n = len(parts)

    def body(*refs):
        p_refs, out_refs = refs[:n], refs[n:2 * n]
        send_sems, recv_sems, local_sems = refs[2 * n:]
        x, y, c = _me()
        my_slot = 4 * x + 2 * y + c
        mine = [pltpu.make_async_copy(p_refs[i].at[my_slot], out_refs[i].at[my_slot], local_sems.at[i])
                for i in range(n)]
        copies = []
        for k in range(1, N_DEV):
            px, py, pc = x ^ (k >> 2), y ^ ((k >> 1) & 1), c ^ (k & 1)
            for i in range(n):
                copies.append(pltpu.make_async_remote_copy(
                    src_ref=p_refs[i].at[4 * px + 2 * py + pc], dst_ref=out_refs[i].at[my_slot],
                    send_sem=send_sems.at[N_PEERS * i + k - 1], recv_sem=recv_sems.at[N_PEERS * i + k - 1],
                    device_id=(px, py, pc), device_id_type=MESH))
        for cp in mine + copies:
            cp.start()
        for cp in copies:
            cp.wait_recv()
        for cp in copies:
            cp.wait_send()
        for cp in mine:
            cp.wait()

    return _comm_call(body, parts, [jax.ShapeDtypeStruct(p.shape, p.dtype) for p in parts], name)


def to_heads(t, n_heads):
    s_len = t.shape[0]
    return t.reshape(s_len, n_heads, -1).transpose(1, 0, 2)


def from_heads(t):
    return t.transpose(1, 0, 2).reshape(t.shape[1], -1)


def gathered_to_full(t, axis):
    shp = t.shape[1:]
    return jnp.moveaxis(t, 0, axis).reshape(shp[:axis] + (N_DEV * shp[axis],) + shp[axis + 1:])


def full_to_owner_major(g, axis):
    shp = g.shape
    t = jnp.moveaxis(g.reshape(shp[:axis] + (N_DEV, shp[axis] // N_DEV) + shp[axis + 1:]), axis, 0)
    return t.reshape(N_DEV, -1, t.shape[-1])


def _small_rows(shape):
    n = 1
    for s in shape:
        n *= s
    return -(-n // LANES)


def pack_small(arrs, shapes):
    pieces = []
    for n in SMALL:
        flat = arrs[n].reshape(-1)
        flat = jnp.pad(flat, (0, _small_rows(shapes[n]) * LANES - flat.shape[0]))
        pieces.append(flat.reshape(-1, LANES))
    flat = jnp.concatenate(pieces, axis=0)
    return jnp.pad(flat, ((0, -flat.shape[0] % SMALL_ROW_MULTIPLE), (0, 0)))


def unpack_small(flat, shapes):
    out, r = {}, 0
    for n in SMALL:
        rows = _small_rows(shapes[n])
        size = 1
        for s in shapes[n]:
            size *= s
        out[n] = flat[r:r + rows].reshape(-1)[:size].reshape(shapes[n])
        r += rows
    return out


ROW_TM = 256
XATT_TM = 1024
HEAD_TM = 1024
SG_TM = 8 * SG_CHUNK
SB_TILES = (512, 512)
SM_TILES = (1024, 1024)


def _norm_fwd(x, g, name):
    return prow(f_rms, [x], params=[g.reshape(1, -1)], outs=[(x.shape[1], BF16, False)], tm=ROW_TM, name=name)[0]


def _norm_bwd(x, g, dh, add, name, want_row=True):
    res = prow_vjp(f_rms, [x], params=[g.reshape(1, -1)], cts=[dh], row_grad=[want_row],
                   adds=[add] if want_row else None, tm=ROW_TM, name=name)
    return (res[0], res[1].reshape(-1)) if want_row else (None, res[0].reshape(-1))


def _out_proj(a, w, x, next_gain, alpha, name):
    if next_gain is None:
        return pmm(a, w, res=x, alpha=alpha, name=name), None
    return pmm(a, w, res=x, alpha=alpha, norm_out=next_gain.reshape(1, -1), name=name)


def _in_proj_bwd(d, w, x, gain, dy, name, **kw):
    dx, g_gain = pmm(d, w, tb=True, norm_bwd=(x, gain.reshape(1, -1), dy), name=name, **kw)
    return dx, g_gain.reshape(-1)


def ffn_fwd(x, h, p, tag, next_gain):
    gate, up, act = ffn_gate_up(h, p['w_gu'], name=f"{tag}_gu")
    out = _out_proj(act, p['w_down'], x, next_gain, 0.5, f"{tag}_down")
    return out, (x, h, gate, up, act)


def ffn_bwd(dy, p, saved, tag):
    x, h, gate, up, act = saved
    d_gate, d_up = ffn_gate_up_bwd(dy, p['w_down'], gate, up, alpha=0.5, name=f"{tag}_dact")
    g_down = pmm(act, dy, ta=True, out_dtype=GRAD_WIRE, alpha=0.5, name=f"{tag}_gdown")
    g_gu = jnp.concatenate([pmm(h, d_gate, ta=True, out_dtype=GRAD_WIRE, name=f"{tag}_ggate"),
                            pmm(h, d_up, ta=True, out_dtype=GRAD_WIRE, name=f"{tag}_gup")], axis=1)
    dh_gate = pmm(d_gate, p['w_gu'], tb=True, name=f"{tag}_dh_gate")
    dx, g_norm = _in_proj_bwd(d_up, p['w_gu'], x, p['norm'], dy, f"{tag}_dh", res=dh_gate,
                              k_off_b=D_FF // _pick(D_FF))
    return dx, {'norm': g_norm, 'w_gu': g_gu, 'w_down': g_down}


def even_mixer_fwd(x, h, p, next_gain):
    proj = pmm(h, p['w_in'], name="sbg_in")
    q, k, v = (to_heads(proj[:, i * SB_WIDTH:(i + 1) * SB_WIDTH], SB_HEADS).astype(BF16) for i in range(3))
    o_sb, tot = sb_fwd(q, k, v, tq=SB_TILES[0], tk=SB_TILES[1], name="sb_fwd")
    z = proj[:, 3 * SB_WIDTH:]
    ln_g, ln_b = p['ln_gain'].reshape(1, -1), p['ln_bias'].reshape(1, -1)
    u, gn = prow(f_gate_prep, [z], params=[ln_g, ln_b], outs=[(SG_WIDTH, F32, False)] * 2, tm=ROW_TM,
                 name="sgu_prep")
    gn_g, u_g = to_heads(gn, SG_GROUPS), to_heads(u, SG_GROUPS)
    b3 = p['sgu_b'].reshape(SG_GROUPS, SG_CHUNK, 1)
    o_sg = prow(f_spatial_gate, [gn_g, u_g], gparams=[p['sgu_w'], b3], outs=[(SG_GROUP_DIM, F32, True)],
                tm=SG_TM, name="sgu_mix")[0]
    cat = jnp.concatenate([from_heads(o_sb), from_heads(o_sg)], axis=-1).astype(BF16)
    out = _out_proj(cat, p['w_out'], x, next_gain, 1.0, "sbg_out")
    return out, (x, h, q, k, v, tot, z, gn_g, u_g, b3, cat)


def even_mixer_bwd(dy, p, saved):
    x, h, q, k, v, tot, z, gn_g, u_g, b3, cat = saved
    d_cat = pmm(dy, p['w_out'], tb=True, name="sbg_dcat")
    g_out = pmm(cat, dy, ta=True, out_dtype=GRAD_WIRE, name="sbg_gout")
    d_osb = to_heads(d_cat[:, :SB_WIDTH], SB_HEADS)
    d_osg = to_heads(d_cat[:, SB_WIDTH:], SG_GROUPS)
    d_gn_g, d_u_g, g_w, g_b = prow_vjp(f_spatial_gate, [gn_g, u_g], gparams=[p['sgu_w'], b3], cts=[d_osg],
                                       row_grad=[True, True], tm=SG_TM, name="sgu_dmix")
    ln_g, ln_b = p['ln_gain'].reshape(1, -1), p['ln_bias'].reshape(1, -1)
    d_z, g_lng, g_lnb = prow_vjp(f_gate_prep, [z], params=[ln_g, ln_b], cts=[from_heads(d_u_g), from_heads(d_gn_g)],
                                 row_grad=[True], row_dtypes=[BF16], tm=ROW_TM, name="sgu_dprep")
    dq, dk, dv = sb_bwd(q, k, v, tot, d_osb, tq=SB_TILES[0], tk=SB_TILES[1], name="sb_bwd")
    d_proj = jnp.concatenate([from_heads(dq).astype(BF16), from_heads(dk).astype(BF16), from_heads(dv).astype(BF16),
                              d_z], axis=-1)
    g_in = pmm(h, d_proj, ta=True, out_dtype=GRAD_WIRE, name="sbg_gin")
    dx, g_norm = _in_proj_bwd(d_proj, p['w_in'], x, p['norm'], dy, "sbg_dh")
    return dx, {'norm': g_norm, 'w_in': g_in, 'ln_gain': g_lng.reshape(-1), 'ln_bias': g_lnb.reshape(-1),
                'sgu_w': g_w, 'sgu_b': g_b.reshape(SG_GROUPS, SG_CHUNK), 'w_out': g_out}


def mla_fwd(x, h, cos, sin, p, next_gain):
    proj = pmm(h, p['w_in'], name="mla_in")
    c_q, c_kv, k_r = proj[:, :MLA_Q_LORA], proj[:, MLA_Q_LORA:MLA_Q_LORA + MLA_KV_LORA], proj[:, MLA_Q_LORA + MLA_KV_LORA:]
    cqn = _norm_fwd(c_q, p['q_lora_gain'], "mla_qlora_norm")
    ckvn = _norm_fwd(c_kv, p['kv_lora_gain'], "mla_kvlora_norm")
    q_h = to_heads(pmm(cqn, p['w_uq'], name="mla_uq"), MLA_HEADS)
    kv_h = to_heads(pmm(ckvn, p['w_ukv'], name="mla_ukv"), MLA_HEADS)
    k_nope, v = kv_h[..., :MLA_NOPE], kv_h[..., MLA_NOPE:].astype(BF16)
    q_g, k_g = p['q_gain'].reshape(1, -1), p['k_gain'].reshape(1, -1)
    qp = prow(f_mla_q, [q_h, cos, sin], params=[q_g], outs=[(MLA_QK, BF16, True)], tm=HEAD_TM, name="mla_qprep")[0]
    kp = prow(f_mla_k, [k_nope, k_r, cos, sin], params=[k_g], outs=[(MLA_QK, BF16, True)], tm=HEAD_TM,
              name="mla_kprep")[0]
    o, lse = sm_fwd(qp, kp, v, tq=SM_TILES[0], tk=SM_TILES[1], name="mla_att_fwd")
    o_flat = from_heads(o).astype(BF16)
    out = _out_proj(o_flat, p['w_out'], x, next_gain, 1.0, "mla_out")
    return out, (x, h, c_q, c_kv, k_r, cqn, ckvn, q_h, k_nope, v, qp, kp, o, lse, o_flat, q_g, k_g)


def mla_bwd(dy, cos, sin, p, saved):
    x, h, c_q, c_kv, k_r, cqn, ckvn, q_h, k_nope, v, qp, kp, o, lse, o_flat, q_g, k_g = saved
    do = to_heads(pmm(dy, p['w_out'], tb=True, name="mla_do"), MLA_HEADS)
    g_out = pmm(o_flat, dy, ta=True, out_dtype=GRAD_WIRE, name="mla_gout")
    dqp, dkp, dv = sm_bwd(qp, kp, v, o, lse, do, tq=SM_TILES[0], tk=SM_TILES[1], name="mla_att_bwd")
    dq_h, g_qg = prow_vjp(f_mla_q, [q_h, cos, sin], params=[q_g], cts=[dqp], row_grad=[True, False, False],
                          row_dtypes=[BF16], tm=HEAD_TM, name="mla_dqprep")
    dk_nope, dk_r, g_kg = prow_vjp(f_mla_k, [k_nope, k_r, cos, sin], params=[k_g], cts=[dkp],
                                   row_grad=[True, True, False, False], tm=HEAD_TM, name="mla_dkprep")
    d_q = from_heads(dq_h)
    d_kv = from_heads(jnp.concatenate([dk_nope, dv], axis=-1)).astype(BF16)
    g_uq = pmm(cqn, d_q, ta=True, out_dtype=GRAD_WIRE, name="mla_guq")
    d_cqn = pmm(d_q, p['w_uq'], tb=True, name="mla_dcqn")
    g_ukv = pmm(ckvn, d_kv, ta=True, out_dtype=GRAD_WIRE, name="mla_gukv")
    d_ckvn = pmm(d_kv, p['w_ukv'], tb=True, name="mla_dckvn")
    d_cq, g_qlora = _norm_bwd(c_q, p['q_lora_gain'], d_cqn, None, "mla_dqlora_norm")
    d_ckv, g_kvlora = _norm_bwd(c_kv, p['kv_lora_gain'], d_ckvn, None, "mla_dkvlora_norm")
    d_proj = jnp.concatenate([d_cq, d_ckv, dk_r], axis=-1).astype(BF16)
    g_in = pmm(h, d_proj, ta=True, out_dtype=GRAD_WIRE, name="mla_gin")
    dx, g_norm = _in_proj_bwd(d_proj, p['w_in'], x, p['norm'], dy, "mla_dh")
    return dx, {'norm': g_norm, 'w_in': g_in, 'q_lora_gain': g_qlora, 'kv_lora_gain': g_kvlora, 'w_uq': g_uq,
                'w_ukv': g_ukv, 'q_gain': g_qg.reshape(-1), 'k_gain': g_kg.reshape(-1), 'w_out': g_out}


def xattn_fwd(x, hq, mem, p, tag, next_gain):
    hm = _norm_fwd(mem, p['mem_norm'], f"{tag}_mem_norm")
    q_h = to_heads(pmm(hq, p['wq'], name=f"{tag}_q"), MEM_HEADS)
    kv = pmm(hm, p['wkv'], name=f"{tag}_kv").reshape(mem.shape[0], MEM_HEADS, 2 * MEM_HEAD_DIM).transpose(1, 0, 2)
    k_h, v_h = kv[..., :MEM_HEAD_DIM], kv[..., MEM_HEAD_DIM:]
    q_g, k_g = p['q_gain'].reshape(1, -1), p['k_gain'].reshape(1, -1)
    o_h = prow(f_xattn, [q_h], gparams=[k_h, v_h], params=[q_g, k_g], outs=[(MEM_HEAD_DIM, BF16, True)], tm=XATT_TM,
               name=f"{tag}_att")[0]
    o_flat = from_heads(o_h)
    out = _out_proj(o_flat, p['wo'], x, next_gain, 1.0, f"{tag}_out")
    return out, (x, mem, hq, hm, q_h, k_h, v_h, q_g, k_g, o_flat)


def xattn_bwd(dy, p, saved, tag):
    x, mem, hq, hm, q_h, k_h, v_h, q_g, k_g, o_flat = saved
    d_o = to_heads(pmm(dy, p['wo'], tb=True, name=f"{tag}_do"), MEM_HEADS)
    g_wo = pmm(o_flat, dy, ta=True, out_dtype=GRAD_WIRE, name=f"{tag}_gwo")
    dq_h, dk_h, dv_h, g_qg, g_kg = prow_vjp(f_xattn, [q_h], gparams=[k_h, v_h], params=[q_g, k_g], cts=[d_o],
                                            row_grad=[True], row_dtypes=[BF16], tm=XATT_TM, name=f"{tag}_datt")
    d_q = from_heads(dq_h)
    d_kv = jnp.concatenate([dk_h, dv_h], axis=-1).transpose(1, 0, 2).reshape(mem.shape[0], -1).astype(BF16)
    g_wq = pmm(hq, d_q, ta=True, out_dtype=GRAD_WIRE, name=f"{tag}_gwq")
    dx, g_norm = _in_proj_bwd(d_q, p['wq'], x, p['norm'], dy, f"{tag}_dhq")
    g_wkv = pmm(hm, d_kv, ta=True, out_dtype=GRAD_WIRE, name=f"{tag}_gwkv")
    dhm = pmm(d_kv, p['wkv'], tb=True, name=f"{tag}_dhm")
    _, g_mem_norm = _norm_bwd(mem, p['mem_norm'], dhm, None, f"{tag}_dmem_norm", want_row=False)
    return dx, {'norm': g_norm, 'mem_norm': g_mem_norm, 'wq': g_wq, 'wkv': g_wkv, 'q_gain': g_qg.reshape(-1),
                'k_gain': g_kg.reshape(-1), 'wo': g_wo}


def rope_tables(positions):
    half = MLA_ROPE // 2
    inv_freq = ROPE_THETA ** (-jnp.arange(half, dtype=F32) / half)
    ang = positions.astype(F32)[:, None] * inv_freq
    return jnp.cos(ang), jnp.sin(ang)


def local_step(x, mem, positions, target, w):
    cos, sin = rope_tables(positions)

    def ffn_params(kind, layer):
        return {'norm': w[f'ffn_{kind}_norm'][layer], 'w_gu': w[f'ffn_{kind}_w_gu'][layer],
                'w_down': w[f'ffn_{kind}_w_down'][layer]}

    def xattn_params(layer):
        return {'norm': w['xmem_norm'][layer], 'mem_norm': w['xmem_mem_norm'][layer], 'wq': w['xmem_wq'][layer],
                'wkv': w['xmem_wkv'][layer], 'q_gain': w['xmem_q_gain'][layer], 'k_gain': w['xmem_k_gain'][layer],
                'wo': w['xmem_wo'][layer]}

    even_p = {'norm': w['mix_norm'][0], 'w_in': w['sbg_w_in'][0], 'ln_gain': w['sgu_ln_gain'][0],
              'ln_bias': w['sgu_ln_bias'][0], 'sgu_w': w['sgu_w'][0], 'sgu_b': w['sgu_b'][0],
              'w_out': w['sbg_w_out'][0]}
    mla_p = {'norm': w['mix_norm'][1], 'w_in': w['mla_w_in'][0], 'q_lora_gain': w['mla_q_lora_gain'][0],
             'kv_lora_gain': w['mla_kv_lora_gain'][0], 'w_uq': w['mla_w_uq'][0], 'w_ukv': w['mla_w_ukv'][0],
             'q_gain': w['mla_q_gain'][0], 'k_gain': w['mla_k_gain'][0], 'w_out': w['mla_w_out'][0]}

    saved = []
    h = _norm_fwd(x, w['ffn_pre_norm'][0], "ffn_pre0_norm")
    for layer in range(DEPTH):
        (x, h), s_pre = ffn_fwd(x, h, ffn_params('pre', layer), f"ffn_pre{layer}", w['mix_norm'][layer])
        if layer % 2 == 0:
            (x, h), s_mix = even_mixer_fwd(x, h, even_p, w['xmem_norm'][layer])
        else:
            (x, h), s_mix = mla_fwd(x, h, cos, sin, mla_p, w['xmem_norm'][layer])
        (x, h), s_x = xattn_fwd(x, h, mem, xattn_params(layer), f"xmem{layer}", w['ffn_post_norm'][layer])
        following = w['ffn_pre_norm'][layer + 1] if layer + 1 < DEPTH else None
        (x, h), s_post = ffn_fwd(x, h, ffn_params('post', layer), f"ffn_post{layer}", following)
        saved.append((s_pre, s_mix, s_x, s_post))

    dx, loss = loss_head(x, target, tm=ROW_TM, name="loss_head")

    per_layer = []
    for layer in reversed(range(DEPTH)):
        s_pre, s_mix, s_x, s_post = saved[layer]
        dx, g_post = ffn_bwd(dx, ffn_params('post', layer), s_post, f"ffn_post{layer}")
        dx, g_x = xattn_bwd(dx, xattn_params(layer), s_x, f"xmem{layer}")
        if layer % 2 == 0:
            dx, g_mix = even_mixer_bwd(dx, even_p, s_mix)
        else:
            dx, g_mix = mla_bwd(dx, cos, sin, mla_p, s_mix)
        dx, g_pre = ffn_bwd(dx, ffn_params('pre', layer), s_pre, f"ffn_pre{layer}")
        per_layer.append((layer, g_pre, g_mix, g_x, g_post))
    per_layer.sort(key=lambda t: t[0])

    def stack(pick):
        return jnp.stack([pick(t) for t in per_layer])

    g_even, g_mla = per_layer[0][2], per_layer[1][2]
    grads = {
        'ffn_pre_norm': stack(lambda t: t[1]['norm']), 'ffn_pre_w_gu': stack(lambda t: t[1]['w_gu']),
        'ffn_pre_w_down': stack(lambda t: t[1]['w_down']), 'mix_norm': stack(lambda t: t[2]['norm']),
        'sbg_w_in': g_even['w_in'][None], 'sgu_ln_gain': g_even['ln_gain'][None], 'sgu_ln_bias': g_even['ln_bias'][None],
        'sgu_w': g_even['sgu_w'][None], 'sgu_b': g_even['sgu_b'][None], 'sbg_w_out': g_even['w_out'][None],
        'mla_w_in': g_mla['w_in'][None], 'mla_q_lora_gain': g_mla['q_lora_gain'][None],
        'mla_kv_lora_gain': g_mla['kv_lora_gain'][None], 'mla_w_uq': g_mla['w_uq'][None],
        'mla_w_ukv': g_mla['w_ukv'][None], 'mla_q_gain': g_mla['q_gain'][None], 'mla_k_gain': g_mla['k_gain'][None],
        'mla_w_out': g_mla['w_out'][None],
        'xmem_norm': stack(lambda t: t[3]['norm']), 'xmem_mem_norm': stack(lambda t: t[3]['mem_norm']),
        'xmem_wq': stack(lambda t: t[3]['wq']), 'xmem_wkv': stack(lambda t: t[3]['wkv']),
        'xmem_q_gain': stack(lambda t: t[3]['q_gain']), 'xmem_k_gain': stack(lambda t: t[3]['k_gain']),
        'xmem_wo': stack(lambda t: t[3]['wo']),
        'ffn_post_norm': stack(lambda t: t[4]['norm']), 'ffn_post_w_gu': stack(lambda t: t[4]['w_gu']),
        'ffn_post_w_down': stack(lambda t: t[4]['w_down']),
    }
    return loss, dx, grads


def _device_slot():
    x, y, c = _me()
    return 4 * x + 2 * y + c


def kernel(x, mem, positions, ffn_pre_norm, ffn_pre_w_gu, ffn_pre_w_down, mix_norm, sbg_w_in, sgu_ln_gain, sgu_ln_bias, sgu_w, sgu_b, sbg_w_out, mla_w_in, mla_q_lora_gain, mla_kv_lora_gain, mla_w_uq, mla_w_ukv, mla_q_gain, mla_k_gain, mla_w_out, xmem_norm, xmem_mem_norm, xmem_wq, xmem_wkv, xmem_q_gain, xmem_k_gain, xmem_wo, ffn_post_norm, ffn_post_w_gu, ffn_post_w_down, loss_target, m_ffn_pre_norm, m_ffn_pre_w_gu, m_ffn_pre_w_down, m_mix_norm, m_sbg_w_in, m_sgu_ln_gain, m_sgu_ln_bias, m_sgu_w, m_sgu_b, m_sbg_w_out, m_mla_w_in, m_mla_q_lora_gain, m_mla_kv_lora_gain, m_mla_w_uq, m_mla_w_ukv, m_mla_q_gain, m_mla_k_gain, m_mla_w_out, m_xmem_norm, m_xmem_mem_norm, m_xmem_wq, m_xmem_wkv, m_xmem_q_gain, m_xmem_k_gain, m_xmem_wo, m_ffn_post_norm, m_ffn_post_w_gu, m_ffn_post_w_down, v_ffn_pre_norm, v_ffn_pre_w_gu, v_ffn_pre_w_down, v_mix_norm, v_sbg_w_in, v_sgu_ln_gain, v_sgu_ln_bias, v_sgu_w, v_sgu_b, v_sbg_w_out, v_mla_w_in, v_mla_q_lora_gain, v_mla_kv_lora_gain, v_mla_w_uq, v_mla_w_ukv, v_mla_q_gain, v_mla_k_gain, v_mla_w_out, v_xmem_norm, v_xmem_mem_norm, v_xmem_wq, v_xmem_wkv, v_xmem_q_gain, v_xmem_k_gain, v_xmem_wo, v_ffn_post_norm, v_ffn_post_w_gu, v_ffn_post_w_down):
    args = locals()
    w_in = {n: args[n] for n in WEIGHTS}
    m_in = {n: args["m_" + n] for n in WEIGHTS}
    v_in = {n: args["v_" + n] for n in WEIGHTS}
    slot = _device_slot()

    tiny = jnp.zeros((8, LANES), F32)
    for i, src in enumerate((w_in, m_in, v_in)):
        tiny = tiny.at[i, :64].set(src['mla_q_lora_gain'][0]).at[i + 3, :32].set(src['mla_kv_lora_gain'][0])
    tiny_all = all_gather([tiny], name="gather_lora_gains")[0]
    full_small = []
    for i, src in enumerate((w_in, m_in, v_in)):
        d = {n: src[n] for n in SMALL}
        d['mla_q_lora_gain'] = tiny_all[:, i, :64].reshape(1, MLA_Q_LORA)
        d['mla_kv_lora_gain'] = tiny_all[:, i + 3, :32].reshape(1, MLA_KV_LORA)
        full_small.append(d)
    w_small, m_small, v_small = full_small
    small_shapes = {n: w_small[n].shape for n in SMALL}

    big = list(BIG)
    gathered = all_gather([w_in[n].astype(BF16) for n in big], name="gather_weights")
    w_full = dict(w_small)
    w_full.update({n: gathered_to_full(t, BIG[n]) for n, t in zip(big, gathered)})

    loss, dx, grads = local_step(x[0], mem[0], positions[0], loss_target[0], w_full)
    loss = lax.psum(loss, ("x", "y", "c"))

    parts = all_to_all([full_to_owner_major(grads[n], BIG[n]) for n in big], name="exchange_grads")
    big_out = {}
    for n, p in zip(big, parts):
        shard, two_d = w_in[n].shape, (-1, w_in[n].shape[-1])
        res = adamw(p, w_in[n].reshape(two_d), m_in[n].reshape(two_d), v_in[n].reshape(two_d), name=f"adamw_{n}")
        big_out[n] = [t.reshape(shard) for t in res]

    small_parts = all_gather([pack_small({n: grads[n] for n in SMALL}, small_shapes)], name="gather_small_grads")[0]
    small_out = adamw(small_parts, pack_small(w_small, small_shapes), pack_small(m_small, small_shapes),
                      pack_small(v_small, small_shapes), name="adamw_small")
    small_out = [unpack_small(t, small_shapes) for t in small_out]
    for d in small_out:
        for n, width in zip(GAIN_SHARDED, (64, 32)):
            d[n] = lax.dynamic_slice(d[n], (0, slot * width), (1, width))

    outs = [loss, dx[None]]
    for kind, small_d in enumerate(small_out):
        outs += [big_out[n][kind] if n in BIG else small_d[n] for n in WEIGHTS]
    return tuple(outs)
```

```python
import functools

import jax
import jax.numpy as jnp
from jax import lax
from jax.experimental import pallas as pl
from jax.experimental.pallas import tpu as pltpu

F32 = jnp.float32
BF16 = jnp.bfloat16
MESH = pl.DeviceIdType.MESH
N_DEV = 8

VMEM_LIMIT_BYTES = 56 * 1024 * 1024
LANES = 128

D_MODEL = 1024
DEPTH = 2
D_FF = 2816
EPS = 1e-6
SB_HEADS, SB_HEAD_DIM = 8, 64
SB_WIDTH = SB_HEADS * SB_HEAD_DIM
SG_GROUPS, SG_GROUP_DIM, SG_CHUNK = 8, 64, 128
SG_WIDTH = SG_GROUPS * SG_GROUP_DIM
MLA_HEADS, MLA_NOPE, MLA_ROPE, MLA_V = 16, 64, 32, 64
MLA_QK = MLA_NOPE + MLA_ROPE
MLA_Q_LORA, MLA_KV_LORA = 512, 256
ROPE_THETA = 10000.0
MEM_HEADS = 4
MEM_HEAD_DIM = D_MODEL // MEM_HEADS

ADAM_LR, ADAM_B1, ADAM_B2, ADAM_EPS, ADAM_WD, ADAM_STEP = 0.001, 0.9, 0.999, 1e-08, 0.01, 10

WEIGHTS = ['ffn_pre_norm', 'ffn_pre_w_gu', 'ffn_pre_w_down', 'mix_norm', 'sbg_w_in', 'sgu_ln_gain', 'sgu_ln_bias',
           'sgu_w', 'sgu_b', 'sbg_w_out', 'mla_w_in', 'mla_q_lora_gain', 'mla_kv_lora_gain', 'mla_w_uq', 'mla_w_ukv',
           'mla_q_gain', 'mla_k_gain', 'mla_w_out', 'xmem_norm', 'xmem_mem_norm', 'xmem_wq', 'xmem_wkv',
           'xmem_q_gain', 'xmem_k_gain', 'xmem_wo', 'ffn_post_norm', 'ffn_post_w_gu', 'ffn_post_w_down']
BIG = {'ffn_pre_w_gu': 2, 'ffn_pre_w_down': 1, 'sbg_w_in': 2, 'sbg_w_out': 1, 'mla_w_in': 1, 'mla_w_uq': 2,
       'mla_w_ukv': 2, 'mla_w_out': 1, 'xmem_wq': 1, 'xmem_wkv': 2, 'xmem_wo': 1, 'ffn_post_w_gu': 2,
       'ffn_post_w_down': 1}
GAIN_SHARDED = ('mla_q_lora_gain', 'mla_kv_lora_gain')
SMALL = [n for n in WEIGHTS if n not in BIG]
GRAD_WIRE = BF16
SMALL_ROW_MULTIPLE = 16


def _cparams(sem=None):
    return pltpu.CompilerParams(dimension_semantics=sem, vmem_limit_bytes=VMEM_LIMIT_BYTES)


MM_TILE_CAP = 1408


def _pick(dim, cap=MM_TILE_CAP):
    if dim % LANES:
        return dim
    return max(t for t in range(LANES, min(dim, cap) + 1, LANES) if dim % t == 0)


def _rms(x, g):
    return x * lax.rsqrt(jnp.mean(x * x, axis=-1, keepdims=True) + EPS) * g


def pmm(a, b, *, ta=False, tb=False, out_dtype=F32, res=None, alpha=1.0, k_off_b=0, norm_out=None, norm_bwd=None,
        name):
    kdim, m = (a.shape if ta else a.shape[::-1])
    n = b.shape[0] if tb else b.shape[1]
    tm, tn, tk = _pick(m), _pick(n), _pick(kdim)
    whole_rows = norm_out is not None or norm_bwd is not None
    if whole_rows:
        assert tn == n
    if norm_bwd is not None:
        tm = min(tm, 512)
    nk = kdim // tk
    dims = (((0 if ta else 1,), (1 if tb else 0,)), ((), ()))
    n_extra = (res is not None) + (norm_out is not None) + (0 if norm_bwd is None else 2 + (norm_bwd[2] is not None))

    def body(*refs):
        a_ref, b_ref = refs[:2]
        extra = list(refs[2:2 + n_extra])
        outs, acc_ref = refs[2 + n_extra:-1], refs[-1]
        i, k = pl.program_id(0), pl.program_id(2)

        @pl.when(k == 0)
        def _():
            acc_ref[...] = jnp.zeros_like(acc_ref)

        acc_ref[...] += lax.dot_general(a_ref[...].astype(BF16), b_ref[...].astype(BF16), dims,
                                        preferred_element_type=F32)

        @pl.when(k == nk - 1)
        def _():
            r = acc_ref[...]
            if alpha != 1.0:
                r = r * alpha
            if res is not None:
                r = extra.pop(0)[...] + r
            if norm_bwd is None:
                outs[0][...] = r.astype(out_dtype)
            if norm_out is not None:
                outs[1][...] = _rms(r, extra.pop(0)[...]).astype(BF16)
            if norm_bwd is not None:
                x_ref, g_ref = extra.pop(0), extra.pop(0)
                _, pull = jax.vjp(_rms, x_ref[...], g_ref[...])
                dx, dg = pull(r)
                if norm_bwd[2] is not None:
                    dx = dx + extra.pop(0)[...]
                outs[0][...] = dx

                @pl.when(i == 0)
                def _():
                    outs[1][...] = dg

                @pl.when(i != 0)
                def _():
                    outs[1][...] += dg

    gi, gj = m // tm, n // tn
    a_bytes, b_bytes = a.size * a.dtype.itemsize, (n * kdim) * b.dtype.itemsize
    j_outer = not whole_rows and nk == 1 and gj * a_bytes + b_bytes < a_bytes + gi * b_bytes
    grid = (gj, gi, nk) if j_outer else (gi, gj, nk)

    def spec(block, index):
        return pl.BlockSpec(block, (lambda j, i, k: index(i, j, k)) if j_outer else index)

    a_spec = spec((tk, tm), lambda i, j, k: (k, i)) if ta else spec((tm, tk), lambda i, j, k: (i, k))
    b_spec = (spec((tn, tk), lambda i, j, k: (j, k + k_off_b)) if tb
              else spec((tk, tn), lambda i, j, k: (k + k_off_b, j)))
    o_spec = spec((tm, tn), lambda i, j, k: (i, j))
    g_spec = spec((1, tn), lambda i, j, k: (0, 0))
    ins, in_specs = [a, b], [a_spec, b_spec]
    if res is not None:
        ins.append(res)
        in_specs.append(o_spec)
    out_shape, out_specs = [jax.ShapeDtypeStruct((m, n), out_dtype)], [o_spec]
    if norm_out is not None:
        ins.append(norm_out)
        in_specs.append(g_spec)
        out_shape.append(jax.ShapeDtypeStruct((m, n), BF16))
        out_specs.append(o_spec)
    if norm_bwd is not None:
        ins += [t for t in norm_bwd if t is not None]
        in_specs += [o_spec, g_spec] + ([o_spec] if norm_bwd[2] is not None else [])
        out_shape = [jax.ShapeDtypeStruct((m, n), F32), jax.ShapeDtypeStruct((1, n), F32)]
        out_specs = [o_spec, g_spec]
    result = pl.pallas_call(
        body, name=name, grid=grid, in_specs=in_specs, out_specs=out_specs,
        out_shape=out_shape, scratch_shapes=[pltpu.VMEM((tm, tn), F32)],
        compiler_params=_cparams(("arbitrary" if norm_bwd is not None else "parallel", "parallel", "arbitrary")),
    )(*ins)
    return result if whole_rows else result[0]


def ffn_gate_up(h, w_gu, *, name):
    m, kdim = h.shape
    n = w_gu.shape[1] // 2
    tm, tn = min(_pick(m), 512), _pick(n)
    up_off = n // tn

    def body(a_ref, bg_ref, bu_ref, gate_ref, up_ref, act_ref):
        av = a_ref[...].astype(BF16)
        gate = _dg(av, bg_ref[...].astype(BF16), 1, 0)
        up = _dg(av, bu_ref[...].astype(BF16), 1, 0)
        gate_ref[...] = gate
        up_ref[...] = up
        act_ref[...] = (jax.nn.silu(gate) * up).astype(BF16)

    o_spec = pl.BlockSpec((tm, tn), lambda j, i: (i, j))
    return pl.pallas_call(
        body, name=name, grid=(n // tn, m // tm),
        in_specs=[pl.BlockSpec((tm, kdim), lambda j, i: (i, 0)), pl.BlockSpec((kdim, tn), lambda j, i: (0, j)),
                  pl.BlockSpec((kdim, tn), lambda j, i: (0, j + up_off))],
        out_specs=[o_spec] * 3,
        out_shape=[jax.ShapeDtypeStruct((m, n), F32), jax.ShapeDtypeStruct((m, n), F32),
                   jax.ShapeDtypeStruct((m, n), BF16)],
        compiler_params=_cparams(("parallel", "parallel")),
    )(h, w_gu, w_gu)


def ffn_gate_up_bwd(dy, w_down, gate, up, *, alpha, name):
    m, kdim = dy.shape
    n = w_down.shape[0]
    tm, tn = min(_pick(m), 512), _pick(n)

    def body(a_ref, b_ref, gate_ref, up_ref, dgate_ref, dup_ref):
        d_act = _dg(a_ref[...].astype(BF16), b_ref[...].astype(BF16), 1, 1) * alpha
        _, pull = jax.vjp(lambda g, u: jax.nn.silu(g) * u, gate_ref[...], up_ref[...])
        d_gate, d_up = pull(d_act)
        dgate_ref[...] = d_gate.astype(BF16)
        dup_ref[...] = d_up.astype(BF16)

    o_spec = pl.BlockSpec((tm, tn), lambda j, i: (i, j))
    return pl.pallas_call(
        body, name=name, grid=(n // tn, m // tm),
        in_specs=[pl.BlockSpec((tm, kdim), lambda j, i: (i, 0)), pl.BlockSpec((tn, kdim), lambda j, i: (j, 0)),
                  o_spec, o_spec],
        out_specs=[o_spec] * 2, out_shape=[jax.ShapeDtypeStruct((m, n), BF16)] * 2,
        compiler_params=_cparams(("parallel", "parallel")),
    )(dy, w_down, gate, up)


def _dg(a, b, ca, cb):
    return lax.dot_general(a, b, (((ca,), (cb,)), ((), ())), preferred_element_type=F32)


@jax.custom_vjp
def bdot(a, b):
    return _dg(a.astype(BF16), b.astype(BF16), 1, 0)


def _bdot_fwd(a, b):
    ab, bb = a.astype(BF16), b.astype(BF16)
    return _dg(ab, bb, 1, 0), (ab, bb)


def _bdot_bwd(saved, g):
    ab, bb = saved
    gb = g.astype(BF16)
    return _dg(gb, bb, 1, 1), _dg(ab, gb, 0, 0)


bdot.defvjp(_bdot_fwd, _bdot_bwd)


@jax.custom_vjp
def bdot_nt(a, b):
    return _dg(a.astype(BF16), b.astype(BF16), 1, 1)


def _bdot_nt_fwd(a, b):
    ab, bb = a.astype(BF16), b.astype(BF16)
    return _dg(ab, bb, 1, 1), (ab, bb)


def _bdot_nt_bwd(saved, g):
    ab, bb = saved
    gb = g.astype(BF16)
    return _dg(gb, bb, 1, 0), _dg(gb, ab, 0, 0)


bdot_nt.defvjp(_bdot_nt_fwd, _bdot_nt_bwd)


def _row_spec(arr, tm):
    if arr.ndim == 3:
        return pl.BlockSpec((None, tm, arr.shape[2]), lambda r, g: (g, r, 0))
    return pl.BlockSpec((tm, arr.shape[1]), lambda r, g: (r, 0))


def _gparam_spec(arr):
    return pl.BlockSpec((None,) + arr.shape[1:], lambda r, g: (g, 0, 0))


def _whole_spec(arr):
    nd = arr.ndim
    return pl.BlockSpec(arr.shape, lambda r, g: (0,) * nd)


def _groups(rows, gparams):
    gs = {a.shape[0] for a in rows if a.ndim == 3} | {a.shape[0] for a in gparams}
    assert len(gs) <= 1
    return gs.pop() if gs else 1


def prow(fn, rows, gparams=(), params=(), *, outs, tm, name):
    rows, gparams, params = list(rows), list(gparams), list(params)
    n_groups = _groups(rows, gparams)
    n_rows = rows[0].shape[-2]
    n_in = len(rows) + len(gparams) + len(params)

    def body(*refs):
        vals = [r[...] for r in refs[:n_in]]
        res = fn(*vals)
        for o_ref, r in zip(refs[n_in:], res, strict=True):
            o_ref[...] = r.astype(o_ref.dtype)

    out_shape, out_specs = [], []
    for width, dtype, grouped in outs:
        shp = (n_groups, n_rows, width) if grouped else (n_rows, width)
        out_shape.append(jax.ShapeDtypeStruct(shp, dtype))
        out_specs.append(_row_spec(out_shape[-1], tm))
    return pl.pallas_call(
        body, name=name, grid=(n_rows // tm, n_groups),
        in_specs=[_row_spec(a, tm) for a in rows] + [_gparam_spec(a) for a in gparams] + [_whole_spec(a) for a in params],
        out_specs=out_specs, out_shape=out_shape,
        compiler_params=_cparams(("parallel", "arbitrary")),
    )(*rows, *gparams, *params)


def prow_vjp(fn, rows, gparams=(), params=(), *, cts, row_grad, adds=None, row_dtypes=None, gparam_grad=None,
             param_grad=None, tm, name):
    rows, gparams, params, cts = list(rows), list(gparams), list(params), list(cts)
    gparam_grad = list(gparam_grad) if gparam_grad is not None else [True] * len(gparams)
    param_grad = list(param_grad) if param_grad is not None else [True] * len(params)
    n_groups = _groups(rows + cts, gparams)
    n_rows = rows[0].shape[-2]
    want_rows = [i for i, w in enumerate(row_grad) if w]
    adds = list(adds) if adds is not None else [None] * len(want_rows)
    row_dtypes = list(row_dtypes) if row_dtypes is not None else [F32] * len(want_rows)
    add_arrays = [a for a in adds if a is not None]
    n_r, n_g, n_p, n_c, n_a = len(rows), len(gparams), len(params), len(cts), len(add_arrays)
    mask = list(row_grad) + gparam_grad + param_grad

    def body(*refs):
        r_id, g_id = pl.program_id(0), pl.program_id(1)
        n_in = n_r + n_g + n_p
        vals = [r[...] for r in refs[:n_in]]
        ct_vals = tuple(r[...].astype(F32) for r in refs[n_in:n_in + n_c])
        add_refs = list(refs[n_in + n_c:n_in + n_c + n_a])
        out_refs = list(refs[n_in + n_c + n_a:])
        diff_idx = [i for i, w in enumerate(mask) if w]

        def wrapped(*diff):
            full = list(vals)
            for i, d in zip(diff_idx, diff):
                full[i] = d
            return tuple(fn(*full))

        _, pull = jax.vjp(wrapped, *[vals[i].astype(F32) for i in diff_idx])
        grads = dict(zip(diff_idx, pull(ct_vals)))
        k = 0
        for j, i in enumerate(want_rows):
            o_ref = out_refs[k]
            k += 1
            gval = grads[i]
            if adds[j] is not None:
                gval = gval + add_refs.pop(0)[...].astype(F32)
            if rows[i].ndim == 2 and n_groups > 1:
                @pl.when(g_id == 0)
                def _(o_ref=o_ref, gval=gval):
                    o_ref[...] = gval.astype(o_ref.dtype)

                @pl.when(g_id != 0)
                def _(o_ref=o_ref, gval=gval):
                    o_ref[...] += gval.astype(o_ref.dtype)
            else:
                o_ref[...] = gval.astype(o_ref.dtype)
        for i in range(n_g):
            if not gparam_grad[i]:
                continue
            o_ref = out_refs[k]
            k += 1
            gval = grads[n_r + i]

            @pl.when(r_id == 0)
            def _(o_ref=o_ref, gval=gval):
                o_ref[g_id] = gval

            @pl.when(r_id != 0)
            def _(o_ref=o_ref, gval=gval):
                o_ref[g_id] += gval
        for i in range(n_p):
            if not param_grad[i]:
                continue
            o_ref = out_refs[k]
            k += 1
            gval = grads[n_r + n_g + i]
            first = jnp.logical_and(r_id == 0, g_id == 0)

            @pl.when(first)
            def _(o_ref=o_ref, gval=gval):
                o_ref[...] = gval

            @pl.when(jnp.logical_not(first))
            def _(o_ref=o_ref, gval=gval):
                o_ref[...] += gval

    out_shape, out_specs = [], []
    for j, i in enumerate(want_rows):
        out_shape.append(jax.ShapeDtypeStruct(rows[i].shape, row_dtypes[j]))
        out_specs.append(_row_spec(rows[i], tm))
    for i in range(n_g):
        if gparam_grad[i]:
            out_shape.append(jax.ShapeDtypeStruct(gparams[i].shape, F32))
            out_specs.append(_whole_spec(gparams[i]))
    for i in range(n_p):
        if param_grad[i]:
            out_shape.append(jax.ShapeDtypeStruct(params[i].shape, F32))
            out_specs.append(_whole_spec(params[i]))
    return pl.pallas_call(
        body, name=name, grid=(n_rows // tm, n_groups),
        in_specs=([_row_spec(a, tm) for a in rows] + [_gparam_spec(a) for a in gparams]
                  + [_whole_spec(a) for a in params] + [_row_spec(a, tm) for a in cts]
                  + [_row_spec(a, tm) for a in add_arrays]),
        out_specs=out_specs, out_shape=out_shape,
        compiler_params=_cparams(("arbitrary", "arbitrary")),
    )(*rows, *gparams, *params, *cts, *add_arrays)


def f_rms(x, g):
    return (_rms(x.astype(F32), g),)


def f_gate_prep(z, ln_g, ln_b):
    act = jax.nn.gelu(z)
    u, gg = act[:, :SG_WIDTH], act[:, SG_WIDTH:]
    mu = jnp.mean(gg, axis=-1, keepdims=True)
    var = jnp.mean(jnp.square(gg - mu), axis=-1, keepdims=True)
    return u, (gg - mu) * lax.rsqrt(var + EPS) * ln_g + ln_b


def f_spatial_gate(gn, u, w, b):
    t = lax.broadcasted_iota(jnp.int32, w.shape, 0)
    s = lax.broadcasted_iota(jnp.int32, w.shape, 1)
    w_causal = jnp.where(s <= t, w, 0.0)
    mixed = [bdot(w_causal, gn[i:i + SG_CHUNK]) + b for i in range(0, gn.shape[0], SG_CHUNK)]
    return (u * (mixed[0] if len(mixed) == 1 else jnp.concatenate(mixed, axis=0)),)


def _rope_tail(t, cos, sin):
    half = MLA_ROPE // 2
    t1, t2 = t[:, MLA_NOPE:MLA_NOPE + half], t[:, MLA_NOPE + half:]
    return jnp.concatenate([t[:, :MLA_NOPE], t1 * cos - t2 * sin, t1 * sin + t2 * cos], axis=-1)


def f_mla_q(q, cos, sin, g):
    return (_rope_tail(f_rms(q, g)[0], cos, sin),)


def f_mla_k(k_nope, k_r, cos, sin, g):
    return (_rope_tail(f_rms(jnp.concatenate([k_nope, k_r], axis=-1), g)[0], cos, sin),)


def f_xattn(q, k, v, q_g, k_g):
    qn, kn = f_rms(q, q_g)[0], f_rms(k, k_g)[0]
    sc = bdot_nt(qn, kn) * (MEM_HEAD_DIM ** -0.5)
    return (bdot(jax.nn.softmax(sc, axis=-1), v),)


def _split_dot(x, tri):
    hi = x.astype(BF16)
    lo = (x - hi.astype(F32)).astype(BF16)
    return _dg(hi, tri, 1, 0) + _dg(lo, tri, 1, 0)


def _tri(tk, cmp):
    j = lax.broadcasted_iota(jnp.int32, (tk, tk), 0)
    s = lax.broadcasted_iota(jnp.int32, (tk, tk), 1)
    return cmp(j, s).astype(BF16)


SCAN_CHUNK = 256


def _row_scan(x, tri, reverse):
    n = x.shape[1] // SCAN_CHUNK
    chunks = [x[:, i * SCAN_CHUNK:(i + 1) * SCAN_CHUNK] for i in range(n)]
    out, seen = [None] * n, None
    for i in (reversed(range(n)) if reverse else range(n)):
        local = _split_dot(chunks[i], tri)
        out[i] = local if seen is None else local + seen
        total = jnp.sum(chunks[i], axis=1, keepdims=True)
        seen = total if seen is None else seen + total
    return (out[0] if n == 1 else jnp.concatenate(out, axis=1)), seen


def _att_specs(s_len, tq, dq, dv):
    q_spec = pl.BlockSpec((None, tq, dq), lambda h, i: (h, i, 0))
    k_spec = pl.BlockSpec((None, s_len, dq), lambda h, i: (h, 0, 0))
    v_spec = pl.BlockSpec((None, s_len, dv), lambda h, i: (h, 0, 0))
    o_spec = pl.BlockSpec((None, tq, dv), lambda h, i: (h, i, 0))
    r_spec = pl.BlockSpec((None, tq, 1), lambda h, i: (h, i, 0))
    return q_spec, k_spec, v_spec, o_spec, r_spec


def _key_blocks(qi, tq, tk):
    return (qi * tq) // tk, ((qi + 1) * tq + tk - 1) // tk


def _keep(qi, j, tq, tk, strict):
    row = qi * tq + lax.broadcasted_iota(jnp.int32, (tq, tk), 0)
    col = j * tk + lax.broadcasted_iota(jnp.int32, (tq, tk), 1)
    return col < row if strict else col <= row


def _log_sigmoid(z):
    return jnp.minimum(z, 0.0) - jnp.log(1.0 + jnp.exp(-jnp.abs(z)))


def sb_fwd(q, k, v, *, tq, tk, name, job=None):
    n_heads, s_len, d = q.shape
    scale = SB_HEAD_DIM ** -0.5

    def body(q_ref, k_ref, v_ref, o_ref, tot_ref):
        qi = pl.program_id(1)
        qv = q_ref[...]
        upper = _tri(SCAN_CHUNK, lambda j, s: j > s)
        n_full, n_all = _key_blocks(qi, tq, tk)

        def make_step(masked, last):
            def step(jj, carry):
                acc, rest = carry
                j = last - 1 - jj
                sl = pl.ds(pl.multiple_of(j * tk, tk), tk)
                ks, vs = k_ref[sl, :], v_ref[sl, :]
                z = _dg(qv, ks, 1, 1) * scale
                log_beta = _log_sigmoid(z)
                log_stay = log_beta - z
                if masked:
                    valid = _keep(qi, j, tq, tk, True)
                    log_stay = jnp.where(valid, log_stay, 0.0)
                after, total = _row_scan(log_stay, upper, True)
                w = jnp.exp(log_beta + after + rest)
                if masked:
                    w = jnp.where(valid, w, 0.0)
                acc = acc + _dg(w.astype(BF16), vs, 1, 0)
                return acc, rest + total
            return step

        carry = (jnp.zeros((tq, d), F32), jnp.zeros((tq, 1), F32))
        carry = lax.fori_loop(0, n_all - n_full, make_step(True, n_all), carry)
        acc, rest = lax.fori_loop(0, n_full, make_step(False, n_full), carry)
        o_ref[...] = acc
        tot_ref[...] = rest

    q_spec, k_spec, v_spec, o_spec, r_spec = _att_specs(s_len, tq, d, d)
    return ride_call(
        job, body, name=name, grid=(n_heads, s_len // tq), in_specs=[q_spec, k_spec, v_spec],
        out_specs=[o_spec, r_spec],
        out_shape=[jax.ShapeDtypeStruct((n_heads, s_len, d), F32), jax.ShapeDtypeStruct((n_heads, s_len, 1), F32)],
        ins=[q, k, v], sem=("parallel", "arbitrary"))


def sb_bwd(q, k, v, tot, do, *, tq, tk, name, job=None):
    n_heads, s_len, d = q.shape
    scale = SB_HEAD_DIM ** -0.5

    def body(q_ref, k_ref, v_ref, tot_ref, do_ref, dq_ref, dk_ref, dv_ref):
        qi = pl.program_id(1)

        @pl.when(qi == 0)
        def _():
            dk_ref[...] = jnp.zeros_like(dk_ref)
            dv_ref[...] = jnp.zeros_like(dv_ref)

        qv = q_ref[...]
        dob = do_ref[...].astype(BF16)
        total = tot_ref[...]
        incl = _tri(SCAN_CHUNK, lambda j, s: j <= s)
        excl = _tri(SCAN_CHUNK, lambda j, s: j < s)
        n_full, n_all = _key_blocks(qi, tq, tk)

        def make_step(masked):
            def step(j, carry):
                dq, stay_before, dl_before = carry
                sl = pl.ds(pl.multiple_of(j * tk, tk), tk)
                ks, vs = k_ref[sl, :], v_ref[sl, :]
                z = _dg(qv, ks, 1, 1) * scale
                log_beta = _log_sigmoid(z)
                log_stay = log_beta - z
                if masked:
                    valid = _keep(qi, j, tq, tk, True)
                    log_stay = jnp.where(valid, log_stay, 0.0)
                stay_upto, stay_sum = _row_scan(log_stay, incl, False)
                w = jnp.exp(log_beta + (total - stay_before) - stay_upto)
                if masked:
                    w = jnp.where(valid, w, 0.0)
                dl = _dg(dob, vs, 1, 1) * w
                dl_upto, dl_sum = _row_scan(dl, excl, False)
                dl_prefix = dl_upto + dl_before
                beta = jnp.exp(log_beta)
                dz = (dl * (1.0 - beta) - beta * dl_prefix) * scale
                if masked:
                    dz = jnp.where(valid, dz, 0.0)
                dzb = dz.astype(BF16)
                dq = dq + _dg(dzb, ks, 1, 0)
                dk_ref[sl, :] += _dg(dzb, qv, 0, 0)
                dv_ref[sl, :] += _dg(w.astype(BF16), dob, 0, 0)
                return dq, stay_before + stay_sum, dl_before + dl_sum
            return step

        zero = jnp.zeros((tq, 1), F32)
        carry = lax.fori_loop(0, n_full, make_step(False), (jnp.zeros((tq, d), F32), zero, zero))
        dq, _, _ = lax.fori_loop(n_full, n_all, make_step(True), carry)
        dq_ref[...] = dq

    q_spec, k_spec, v_spec, o_spec, r_spec = _att_specs(s_len, tq, d, d)
    shp = jax.ShapeDtypeStruct((n_heads, s_len, d), F32)
    return ride_call(
        job, body, name=name, grid=(n_heads, s_len // tq), in_specs=[q_spec, k_spec, v_spec, r_spec, o_spec],
        out_specs=[q_spec, k_spec, v_spec], out_shape=[shp, shp, shp], ins=[q, k, v, tot, do],
        sem=("arbitrary", "arbitrary"))


NEG_BIG = -1e30


def sm_fwd(q, k, v, *, tq, tk, name):
    n_heads, s_len, dq = q.shape
    dv = v.shape[2]
    scale = dq ** -0.5

    def body(q_ref, k_ref, v_ref, o_ref, lse_ref):
        qi = pl.program_id(1)
        qv = q_ref[...]
        n_full, n_all = _key_blocks(qi, tq, tk)

        def make_step(masked):
            def step(j, carry):
                acc, m, l = carry
                sl = pl.ds(pl.multiple_of(j * tk, tk), tk)
                ks, vs = k_ref[sl, :], v_ref[sl, :]
                sc = _dg(qv, ks, 1, 1) * scale
                if masked:
                    sc = jnp.where(_keep(qi, j, tq, tk, False), sc, NEG_BIG)
                m_new = jnp.maximum(m, jnp.max(sc, axis=1, keepdims=True))
                p = jnp.exp(sc - m_new)
                fade = jnp.exp(m - m_new)
                return (fade * acc + _dg(p.astype(BF16), vs, 1, 0), m_new,
                        fade * l + jnp.sum(p, axis=1, keepdims=True))
            return step

        carry = (jnp.zeros((tq, dv), F32), jnp.full((tq, 1), NEG_BIG, F32), jnp.zeros((tq, 1), F32))
        carry = lax.fori_loop(0, n_full, make_step(False), carry)
        acc, m, l = lax.fori_loop(n_full, n_all, make_step(True), carry)
        o_ref[...] = acc / l
        lse_ref[...] = m + jnp.log(l)

    q_spec, k_spec, v_spec, o_spec, r_spec = _att_specs(s_len, tq, dq, dv)
    return pl.pallas_call(
        body, name=name, grid=(n_heads, s_len // tq), in_specs=[q_spec, k_spec, v_spec],
        out_specs=[o_spec, r_spec],
        out_shape=[jax.ShapeDtypeStruct((n_heads, s_len, dv), F32), jax.ShapeDtypeStruct((n_heads, s_len, 1), F32)],
        compiler_params=_cparams(("parallel", "arbitrary")),
    )(q, k, v)


def sm_bwd(q, k, v, o, lse, do, *, tq, tk, name, job=None):
    n_heads, s_len, dq = q.shape
    dv = v.shape[2]
    scale = dq ** -0.5

    def body(q_ref, k_ref, v_ref, o_ref, lse_ref, do_ref, dq_ref, dk_ref, dv_ref):
        qi = pl.program_id(1)

        @pl.when(qi == 0)
        def _():
            dk_ref[...] = jnp.zeros_like(dk_ref)
            dv_ref[...] = jnp.zeros_like(dv_ref)

        qv = q_ref[...]
        do = do_ref[...]
        dob = do.astype(BF16)
        delta = jnp.sum(do * o_ref[...], axis=1, keepdims=True)
        lse_v = lse_ref[...]
        n_full, n_all = _key_blocks(qi, tq, tk)

        def make_step(masked):
            def step(j, dq_acc):
                sl = pl.ds(pl.multiple_of(j * tk, tk), tk)
                ks, vs = k_ref[sl, :], v_ref[sl, :]
                p = jnp.exp(_dg(qv, ks, 1, 1) * scale - lse_v)
                if masked:
                    p = jnp.where(_keep(qi, j, tq, tk, False), p, 0.0)
                dv_ref[sl, :] += _dg(p.astype(BF16), dob, 0, 0)
                ds = (p * (_dg(dob, vs, 1, 1) - delta) * scale).astype(BF16)
                dk_ref[sl, :] += _dg(ds, qv, 0, 0)
                return dq_acc + _dg(ds, ks, 1, 0)
            return step

        dq_acc = lax.fori_loop(0, n_full, make_step(False), jnp.zeros((tq, dq), F32))
        dq_ref[...] = lax.fori_loop(n_full, n_all, make_step(True), dq_acc)

    q_spec, k_spec, v_spec, o_spec, r_spec = _att_specs(s_len, tq, dq, dv)
    return ride_call(
        job, body, name=name, grid=(n_heads, s_len // tq),
        in_specs=[q_spec, k_spec, v_spec, o_spec, r_spec, o_spec], out_specs=[q_spec, k_spec, v_spec],
        out_shape=[jax.ShapeDtypeStruct((n_heads, s_len, dq), F32), jax.ShapeDtypeStruct((n_heads, s_len, dq), F32),
                   jax.ShapeDtypeStruct((n_heads, s_len, dv), F32)],
        ins=[q, k, v, o, lse, do], sem=("arbitrary", "arbitrary"))


def loss_head(y, target, *, tm, name):
    n_rows, width = y.shape

    def body(y_ref, t_ref, dy_ref, loss_ref):
        diff = y_ref[...] - t_ref[...]
        dy_ref[...] = diff / width
        part = 0.5 * jnp.sum(jnp.mean(diff * diff, axis=-1, keepdims=True), axis=0, keepdims=True)

        @pl.when(pl.program_id(0) == 0)
        def _():
            loss_ref[...] = jnp.zeros_like(loss_ref)

        loss_ref[...] += jnp.broadcast_to(part, loss_ref.shape)

    spec = pl.BlockSpec((tm, width), lambda r: (r, 0))
    dy, loss = pl.pallas_call(
        body, name=name, grid=(n_rows // tm,), in_specs=[spec, spec],
        out_specs=[spec, pl.BlockSpec((8, LANES), lambda r: (0, 0))],
        out_shape=[jax.ShapeDtypeStruct(y.shape, F32), jax.ShapeDtypeStruct((8, LANES), F32)],
        compiler_params=_cparams(("arbitrary",)),
    )(y, target)
    return dy, loss[0, 0]


ADAM_TILE_ELEMS = 256 * 1024


def _adam_rows(n_rows, width):
    fits = [t for t in range(16, n_rows + 1, 16) if n_rows % t == 0 and t * width <= ADAM_TILE_ELEMS]
    return max(fits) if fits else n_rows


def adamw(parts, w, m, v, *, name):
    n_layers, n_rows, width = w.shape
    assert len(parts) == n_layers
    tm = _adam_rows(n_rows, width)
    n_tiles = n_rows // tm

    def body(*refs):
        p_refs = refs[:n_layers]
        w_ref, m_ref, v_ref, g_ref, d_ref, nm_ref, nv_ref = refs[n_layers:]
        layer = pl.program_id(0)
        for this, p_ref in enumerate(p_refs):
            @pl.when(layer == this)
            def _(p_ref=p_ref):
                g = p_ref[0].astype(F32)
                for i in range(1, N_DEV):
                    g = g + p_ref[i].astype(F32)
                m_new = ADAM_B1 * m_ref[...] + (1.0 - ADAM_B1) * g
                v_new = ADAM_B2 * v_ref[...] + (1.0 - ADAM_B2) * jnp.square(g)
                m_hat = m_new / (1.0 - ADAM_B1 ** ADAM_STEP)
                v_hat = v_new / (1.0 - ADAM_B2 ** ADAM_STEP)
                g_ref[...] = g
                d_ref[...] = -ADAM_LR * (m_hat / (jnp.sqrt(v_hat) + ADAM_EPS) + ADAM_WD * w_ref[...])
                nm_ref[...] = m_new
                nv_ref[...] = v_new

    def part_spec(this):
        def index(layer, r):
            return 0, jnp.where(layer == this, r, jnp.where(layer < this, 0, n_tiles - 1)), 0
        return pl.BlockSpec((N_DEV, tm, width), index)

    spec = pl.BlockSpec((None, tm, width), lambda layer, r: (layer, r, 0))
    shp = jax.ShapeDtypeStruct(w.shape, F32)
    return pl.pallas_call(
        body, name=name, grid=(n_layers, n_tiles),
        in_specs=[part_spec(this) for this in range(n_layers)] + [spec, spec, spec],
        out_specs=[spec] * 4, out_shape=[shp] * 4, compiler_params=_cparams(("arbitrary", "arbitrary")),
    )(*parts, w, m, v)


def _me():
    return lax.axis_index("x"), lax.axis_index("y"), lax.axis_index("c")


N_PEERS = N_DEV - 1


class CommJob:
    def __init__(self, kind, arrays):
        self.kind, self.arrays, self.n = kind, list(arrays), len(arrays)

    def out_shape(self):
        lead = (N_DEV,) if self.kind == 'gather' else ()
        return [jax.ShapeDtypeStruct(lead + a.shape, a.dtype) for a in self.arrays]

    def scratch(self):
        return [pltpu.SemaphoreType.DMA((N_PEERS * self.n,)), pltpu.SemaphoreType.DMA((N_PEERS * self.n,)),
                pltpu.SemaphoreType.DMA((self.n,))]

    def phases(self, in_refs, out_refs, send_sems, recv_sems, local_sems):
        n = self.n
        x, y, c = _me()

        def remote(i, k, src, dst, to):
            return pltpu.make_async_remote_copy(
                src_ref=src, dst_ref=dst, send_sem=send_sems.at[N_PEERS * i + k],
                recv_sem=recv_sems.at[N_PEERS * i + k], device_id=to, device_id_type=MESH)

        if self.kind == 'gather':
            me, sibling = (x, y, c), (x, y, 1 - c)
            chips = [(1 - x, y), (x, 1 - y), (1 - x, 1 - y)]

            def slot(i, px, py, pc):
                return out_refs[i].at[4 * px + 2 * py + pc]

            def copy(i, k, blk, to, src=None):
                return remote(i, k, slot(i, *blk) if src is None else src, slot(i, *blk), to)

            def mine():
                return [pltpu.make_async_copy(in_refs[i], slot(i, *me), local_sems.at[i]) for i in range(n)]

            def first():
                cps = []
                for i in range(n):
                    cps.append(copy(i, 0, me, sibling, src=in_refs[i]))
                    cps += [copy(i, 1 + j, me, (*chip, c), src=in_refs[i]) for j, chip in enumerate(chips)]
                return cps

            def passed():
                return [copy(i, 4 + j, (*chip, c), sibling) for j, chip in enumerate(chips) for i in range(n)]

            def start():
                for cp in mine() + first():
                    cp.start()

            def forward():
                for j, chip in enumerate(chips):
                    for i in range(n):
                        copy(i, 1 + j, (*chip, c), me).wait_recv()
                        copy(i, 4 + j, (*chip, c), sibling).start()

            def finish():
                for i in range(n):
                    copy(i, 0, sibling, me).wait_recv()
                    for j, chip in enumerate(chips):
                        copy(i, 4 + j, (*chip, 1 - c), me).wait_recv()
                for cp in first() + passed():
                    cp.wait_send()
                for cp in mine():
                    cp.wait()

            return start, forward, finish

        my_slot = 4 * x + 2 * y + c

        def mine():
            return [pltpu.make_async_copy(in_refs[i].at[my_slot], out_refs[i].at[my_slot], local_sems.at[i])
                    for i in range(n)]

        def copies():
            cps = []
            for k in range(1, N_DEV):
                px, py, pc = x ^ (k >> 2), y ^ ((k >> 1) & 1), c ^ (k & 1)
                cps += [remote(i, k - 1, in_refs[i].at[4 * px + 2 * py + pc], out_refs[i].at[my_slot], (px, py, pc))
                        for i in range(n)]
            return cps

        def start():
            for cp in mine() + copies():
                cp.start()

        def finish():
            for cp in copies():
                cp.wait_recv()
            for cp in copies():
                cp.wait_send()
            for cp in mine():
                cp.wait()

        return start, (lambda: None), finish


def comm_call(kind, arrays, *, name):
    job = CommJob(kind, arrays)
    n = job.n

    def body(*refs):
        start, forward, finish = job.phases(refs[:n], refs[n:2 * n], *refs[2 * n:])
        start()
        forward()
        finish()

    hbm = pl.BlockSpec(memory_space=pl.ANY)
    return pl.pallas_call(body, name=name, out_shape=job.out_shape(), in_specs=[hbm] * n, out_specs=[hbm] * n,
                          scratch_shapes=job.scratch())(*job.arrays)


def ride_along(job, n_in, n_out, step, n_steps, compute):
    if job is None:
        return compute
    n = job.n

    def body(*refs):
        ins, job_ins = refs[:n_in], refs[n_in:n_in + n]
        outs, job_outs = refs[n_in + n:n_in + n + n_out], refs[n_in + n + n_out:n_in + 2 * n + n_out]
        start, forward, finish = job.phases(job_ins, job_outs, *refs[n_in + 2 * n + n_out:])
        now = step()
        pl.when(now == 0)(start)
        pl.when(now == n_steps // 2)(forward)
        compute(*ins, *outs)
        pl.when(now == n_steps - 1)(finish)

    return body


def ride_call(job, compute, *, name, grid, in_specs, out_specs, out_shape, ins, sem):
    n_steps = grid[0] * grid[1]
    body = ride_along(job, len(ins), len(out_shape), lambda: pl.program_id(0) * grid[1] + pl.program_id(1), n_steps,
                      compute)
    if job is None:
        return pl.pallas_call(body, name=name, grid=grid, in_specs=in_specs, out_specs=out_specs,
                              out_shape=out_shape, compiler_params=_cparams(sem))(*ins), None
    hbm = pl.BlockSpec(memory_space=pl.ANY)
    res = pl.pallas_call(
        body, name=name, grid=grid, in_specs=list(in_specs) + [hbm] * job.n,
        out_specs=list(out_specs) + [hbm] * job.n, out_shape=list(out_shape) + job.out_shape(),
        scratch_shapes=job.scratch(), compiler_params=_cparams(("arbitrary", "arbitrary")),
    )(*ins, *job.arrays)
    return res[:len(out_shape)], res[len(out_shape):]


def to_heads(t, n_heads):
    s_len = t.shape[0]
    return t.reshape(s_len, n_heads, -1).transpose(1, 0, 2)


def from_heads(t):
    return t.transpose(1, 0, 2).reshape(t.shape[1], -1)


def gathered_to_full(t, axis):
    shp = t.shape[1:]
    return jnp.moveaxis(t, 0, axis).reshape(shp[:axis] + (N_DEV * shp[axis],) + shp[axis + 1:])


def full_to_owner_major(g, axis):
    shp = g.shape
    t = jnp.moveaxis(g.reshape(shp[:axis] + (N_DEV, shp[axis] // N_DEV) + shp[axis + 1:]), axis, 0)
    return t.reshape(N_DEV, -1, t.shape[-1])


def _small_rows(shape):
    n = 1
    for s in shape:
        n *= s
    return -(-n // LANES)


def pack_small(arrs, shapes):
    pieces = []
    for n in SMALL:
        flat = arrs[n].reshape(-1)
        flat = jnp.pad(flat, (0, _small_rows(shapes[n]) * LANES - flat.shape[0]))
        pieces.append(flat.reshape(-1, LANES))
    flat = jnp.concatenate(pieces, axis=0)
    return jnp.pad(flat, ((0, -flat.shape[0] % SMALL_ROW_MULTIPLE), (0, 0)))


def unpack_small(flat, shapes):
    out, r = {}, 0
    for n in SMALL:
        rows = _small_rows(shapes[n])
        size = 1
        for s in shapes[n]:
            size *= s
        out[n] = flat[r:r + rows].reshape(-1)[:size].reshape(shapes[n])
        r += rows
    return out


ROW_TM = 256
XATT_TM = 1024
HEAD_TM = 1024
SG_TM = 8 * SG_CHUNK
SB_TILES = (512, 512)
SM_TILES = (1024, 1024)


def _norm_fwd(x, g, name):
    return prow(f_rms, [x], params=[g.reshape(1, -1)], outs=[(x.shape[1], BF16, False)], tm=ROW_TM, name=name)[0]


def _norm_bwd(x, g, dh, add, name, want_row=True):
    res = prow_vjp(f_rms, [x], params=[g.reshape(1, -1)], cts=[dh], row_grad=[want_row],
                   adds=[add] if want_row else None, tm=ROW_TM, name=name)
    return (res[0], res[1].reshape(-1)) if want_row else (None, res[0].reshape(-1))


def _out_proj(a, w, x, next_gain, alpha, name):
    if next_gain is None:
        return pmm(a, w, res=x, alpha=alpha, name=name), None
    return pmm(a, w, res=x, alpha=alpha, norm_out=next_gain.reshape(1, -1), name=name)


def _in_proj_bwd(d, w, x, gain, dy, name, **kw):
    dx, g_gain = pmm(d, w, tb=True, norm_bwd=(x, gain.reshape(1, -1), dy), name=name, **kw)
    return dx, g_gain.reshape(-1)


def ffn_fwd(x, h, p, tag, next_gain):
    gate, up, act = ffn_gate_up(h, p['w_gu'], name=f"{tag}_gu")
    out = _out_proj(act, p['w_down'], x, next_gain, 0.5, f"{tag}_down")
    return out, (x, h, gate, up, act)


def ffn_bwd(dy, p, saved, tag):
    x, h, gate, up, act = saved
    d_gate, d_up = ffn_gate_up_bwd(dy, p['w_down'], gate, up, alpha=0.5, name=f"{tag}_dact")
    g_down = pmm(act, dy, ta=True, out_dtype=GRAD_WIRE, alpha=0.5, name=f"{tag}_gdown")
    g_gu = jnp.concatenate([pmm(h, d_gate, ta=True, out_dtype=GRAD_WIRE, name=f"{tag}_ggate"),
                            pmm(h, d_up, ta=True, out_dtype=GRAD_WIRE, name=f"{tag}_gup")], axis=1)
    dh_gate = pmm(d_gate, p['w_gu'], tb=True, name=f"{tag}_dh_gate")
    dx, g_norm = _in_proj_bwd(d_up, p['w_gu'], x, p['norm'], dy, f"{tag}_dh", res=dh_gate,
                              k_off_b=D_FF // _pick(D_FF))
    return dx, {'norm': g_norm, 'w_gu': g_gu, 'w_down': g_down}


def even_mixer_fwd(x, h, p, next_gain, job=None, after_job=None):
    proj = pmm(h, p['w_in'], name="sbg_in")
    q, k, v = (to_heads(proj[:, i * SB_WIDTH:(i + 1) * SB_WIDTH], SB_HEADS).astype(BF16) for i in range(3))
    (o_sb, tot), landed = sb_fwd(q, k, v, tq=SB_TILES[0], tk=SB_TILES[1], name="sb_fwd", job=job)
    if job is not None:
        after_job(landed)
    z = proj[:, 3 * SB_WIDTH:]
    ln_g, ln_b = p['ln_gain'].reshape(1, -1), p['ln_bias'].reshape(1, -1)
    u, gn = prow(f_gate_prep, [z], params=[ln_g, ln_b], outs=[(SG_WIDTH, F32, False)] * 2, tm=ROW_TM,
                 name="sgu_prep")
    gn_g, u_g = to_heads(gn, SG_GROUPS), to_heads(u, SG_GROUPS)
    b3 = p['sgu_b'].reshape(SG_GROUPS, SG_CHUNK, 1)
    o_sg = prow(f_spatial_gate, [gn_g, u_g], gparams=[p['sgu_w'], b3], outs=[(SG_GROUP_DIM, F32, True)],
                tm=SG_TM, name="sgu_mix")[0]
    cat = jnp.concatenate([from_heads(o_sb), from_heads(o_sg)], axis=-1).astype(BF16)
    out = _out_proj(cat, p['w_out'], x, next_gain, 1.0, "sbg_out")
    return out, (x, h, q, k, v, tot, z, gn_g, u_g, b3, cat)


def even_mixer_bwd(dy, p, saved, job_of=None):
    x, h, q, k, v, tot, z, gn_g, u_g, b3, cat = saved
    d_cat = pmm(dy, p['w_out'], tb=True, name="sbg_dcat")
    g_out = pmm(cat, dy, ta=True, out_dtype=GRAD_WIRE, name="sbg_gout")
    d_osb = to_heads(d_cat[:, :SB_WIDTH], SB_HEADS)
    d_osg = to_heads(d_cat[:, SB_WIDTH:], SG_GROUPS)
    d_gn_g, d_u_g, g_w, g_b = prow_vjp(f_spatial_gate, [gn_g, u_g], gparams=[p['sgu_w'], b3], cts=[d_osg],
                                       row_grad=[True, True], tm=SG_TM, name="sgu_dmix")
    ln_g, ln_b = p['ln_gain'].reshape(1, -1), p['ln_bias'].reshape(1, -1)
    d_z, g_lng, g_lnb = prow_vjp(f_gate_prep, [z], params=[ln_g, ln_b], cts=[from_heads(d_u_g), from_heads(d_gn_g)],
                                 row_grad=[True], row_dtypes=[BF16], tm=ROW_TM, name="sgu_dprep")
    job = None if job_of is None else job_of({'w_out': g_out})
    (dq, dk, dv), landed = sb_bwd(q, k, v, tot, d_osb, tq=SB_TILES[0], tk=SB_TILES[1], name="sb_bwd", job=job)
    d_proj = jnp.concatenate([from_heads(dq).astype(BF16), from_heads(dk).astype(BF16), from_heads(dv).astype(BF16),
                              d_z], axis=-1)
    g_in = pmm(h, d_proj, ta=True, out_dtype=GRAD_WIRE, name="sbg_gin")
    dx, g_norm = _in_proj_bwd(d_proj, p['w_in'], x, p['norm'], dy, "sbg_dh")
    return dx, {'norm': g_norm, 'w_in': g_in, 'ln_gain': g_lng.reshape(-1), 'ln_bias': g_lnb.reshape(-1),
                'sgu_w': g_w, 'sgu_b': g_b.reshape(SG_GROUPS, SG_CHUNK), 'w_out': g_out}, landed


def mla_fwd(x, h, cos, sin, p, next_gain):
    proj = pmm(h, p['w_in'], name="mla_in")
    c_q, c_kv, k_r = proj[:, :MLA_Q_LORA], proj[:, MLA_Q_LORA:MLA_Q_LORA + MLA_KV_LORA], proj[:, MLA_Q_LORA + MLA_KV_LORA:]
    cqn = _norm_fwd(c_q, p['q_lora_gain'], "mla_qlora_norm")
    ckvn = _norm_fwd(c_kv, p['kv_lora_gain'], "mla_kvlora_norm")
    q_h = to_heads(pmm(cqn, p['w_uq'], name="mla_uq"), MLA_HEADS)
    kv_h = to_heads(pmm(ckvn, p['w_ukv'], name="mla_ukv"), MLA_HEADS)
    k_nope, v = kv_h[..., :MLA_NOPE], kv_h[..., MLA_NOPE:].astype(BF16)
    q_g, k_g = p['q_gain'].reshape(1, -1), p['k_gain'].reshape(1, -1)
    qp = prow(f_mla_q, [q_h, cos, sin], params=[q_g], outs=[(MLA_QK, BF16, True)], tm=HEAD_TM, name="mla_qprep")[0]
    kp = prow(f_mla_k, [k_nope, k_r, cos, sin], params=[k_g], outs=[(MLA_QK, BF16, True)], tm=HEAD_TM,
              name="mla_kprep")[0]
    o, lse = sm_fwd(qp, kp, v, tq=SM_TILES[0], tk=SM_TILES[1], name="mla_att_fwd")
    o_flat = from_heads(o).astype(BF16)
    out = _out_proj(o_flat, p['w_out'], x, next_gain, 1.0, "mla_out")
    return out, (x, h, c_q, c_kv, k_r, cqn, ckvn, q_h, k_nope, v, qp, kp, o, lse, o_flat, q_g, k_g)


def mla_bwd(dy, cos, sin, p, saved, job_of=None):
    x, h, c_q, c_kv, k_r, cqn, ckvn, q_h, k_nope, v, qp, kp, o, lse, o_flat, q_g, k_g = saved
    do = to_heads(pmm(dy, p['w_out'], tb=True, name="mla_do"), MLA_HEADS)
    g_out = pmm(o_flat, dy, ta=True, out_dtype=GRAD_WIRE, name="mla_gout")
    job = None if job_of is None else job_of({'w_out': g_out})
    (dqp, dkp, dv), landed = sm_bwd(qp, kp, v, o, lse, do, tq=SM_TILES[0], tk=SM_TILES[1], name="mla_att_bwd", job=job)
    dq_h, g_qg = prow_vjp(f_mla_q, [q_h, cos, sin], params=[q_g], cts=[dqp], row_grad=[True, False, False],
                          row_dtypes=[BF16], tm=HEAD_TM, name="mla_dqprep")
    dk_nope, dk_r, g_kg = prow_vjp(f_mla_k, [k_nope, k_r, cos, sin], params=[k_g], cts=[dkp],
                                   row_grad=[True, True, False, False], tm=HEAD_TM, name="mla_dkprep")
    d_q = from_heads(dq_h)
    d_kv = from_heads(jnp.concatenate([dk_nope, dv], axis=-1)).astype(BF16)
    g_uq = pmm(cqn, d_q, ta=True, out_dtype=GRAD_WIRE, name="mla_guq")
    d_cqn = pmm(d_q, p['w_uq'], tb=True, name="mla_dcqn")
    g_ukv = pmm(ckvn, d_kv, ta=True, out_dtype=GRAD_WIRE, name="mla_gukv")
    d_ckvn = pmm(d_kv, p['w_ukv'], tb=True, name="mla_dckvn")
    d_cq, g_qlora = _norm_bwd(c_q, p['q_lora_gain'], d_cqn, None, "mla_dqlora_norm")
    d_ckv, g_kvlora = _norm_bwd(c_kv, p['kv_lora_gain'], d_ckvn, None, "mla_dkvlora_norm")
    d_proj = jnp.concatenate([d_cq, d_ckv, dk_r], axis=-1).astype(BF16)
    g_in = pmm(h, d_proj, ta=True, out_dtype=GRAD_WIRE, name="mla_gin")
    dx, g_norm = _in_proj_bwd(d_proj, p['w_in'], x, p['norm'], dy, "mla_dh")
    return dx, {'norm': g_norm, 'w_in': g_in, 'q_lora_gain': g_qlora, 'kv_lora_gain': g_kvlora, 'w_uq': g_uq,
                'w_ukv': g_ukv, 'q_gain': g_qg.reshape(-1), 'k_gain': g_kg.reshape(-1), 'w_out': g_out}, landed


def xattn_fwd(x, hq, mem, p, tag, next_gain):
    hm = _norm_fwd(mem, p['mem_norm'], f"{tag}_mem_norm")
    q_h = to_heads(pmm(hq, p['wq'], name=f"{tag}_q"), MEM_HEADS)
    kv = pmm(hm, p['wkv'], name=f"{tag}_kv").reshape(mem.shape[0], MEM_HEADS, 2 * MEM_HEAD_DIM).transpose(1, 0, 2)
    k_h, v_h = kv[..., :MEM_HEAD_DIM], kv[..., MEM_HEAD_DIM:]
    q_g, k_g = p['q_gain'].reshape(1, -1), p['k_gain'].reshape(1, -1)
    o_h = prow(f_xattn, [q_h], gparams=[k_h, v_h], params=[q_g, k_g], outs=[(MEM_HEAD_DIM, BF16, True)], tm=XATT_TM,
               name=f"{tag}_att")[0]
    o_flat = from_heads(o_h)
    out = _out_proj(o_flat, p['wo'], x, next_gain, 1.0, f"{tag}_out")
    return out, (x, mem, hq, hm, q_h, k_h, v_h, q_g, k_g, o_flat)


def xattn_bwd(dy, p, saved, tag):
    x, mem, hq, hm, q_h, k_h, v_h, q_g, k_g, o_flat = saved
    d_o = to_heads(pmm(dy, p['wo'], tb=True, name=f"{tag}_do"), MEM_HEADS)
    g_wo = pmm(o_flat, dy, ta=True, out_dtype=GRAD_WIRE, name=f"{tag}_gwo")
    dq_h, dk_h, dv_h, g_qg, g_kg = prow_vjp(f_xattn, [q_h], gparams=[k_h, v_h], params=[q_g, k_g], cts=[d_o],
                                            row_grad=[True], row_dtypes=[BF16], tm=XATT_TM, name=f"{tag}_datt")
    d_q = from_heads(dq_h)
    d_kv = jnp.concatenate([dk_h, dv_h], axis=-1).transpose(1, 0, 2).reshape(mem.shape[0], -1).astype(BF16)
    g_wq = pmm(hq, d_q, ta=True, out_dtype=GRAD_WIRE, name=f"{tag}_gwq")
    dx, g_norm = _in_proj_bwd(d_q, p['wq'], x, p['norm'], dy, f"{tag}_dhq")
    g_wkv = pmm(hm, d_kv, ta=True, out_dtype=GRAD_WIRE, name=f"{tag}_gwkv")
    dhm = pmm(d_kv, p['wkv'], tb=True, name=f"{tag}_dhm")
    _, g_mem_norm = _norm_bwd(mem, p['mem_norm'], dhm, None, f"{tag}_dmem_norm", want_row=False)
    return dx, {'norm': g_norm, 'mem_norm': g_mem_norm, 'wq': g_wq, 'wkv': g_wkv, 'q_gain': g_qg.reshape(-1),
                'k_gain': g_kg.reshape(-1), 'wo': g_wo}


def rope_tables(positions):
    half = MLA_ROPE // 2
    inv_freq = ROPE_THETA ** (-jnp.arange(half, dtype=F32) / half)
    ang = positions.astype(F32)[:, None] * inv_freq
    return jnp.cos(ang), jnp.sin(ang)


EARLY_UNITS = [('ffn_pre_w_gu', 0), ('ffn_pre_w_down', 0), ('sbg_w_in', 0)]


def local_step(x, mem, positions, target, w, shards):
    cos, sin = rope_tables(positions)
    full = {}

    def absorb(units, gathered):
        for (n, layer), t in zip(units, gathered):
            full[(n, layer)] = gathered_to_full(t, BIG[n] - 1)

    late_units = [u for u in shards if u not in EARLY_UNITS]
    absorb(EARLY_UNITS, comm_call('gather', [shards[u] for u in EARLY_UNITS], name="gather_weights_early"))

    def ffn_params(kind, layer):
        return {'norm': w[f'ffn_{kind}_norm'][layer], 'w_gu': full[(f'ffn_{kind}_w_gu', layer)],
                'w_down': full[(f'ffn_{kind}_w_down', layer)]}

    def xattn_params(layer):
        return {'norm': w['xmem_norm'][layer], 'mem_norm': w['xmem_mem_norm'][layer], 'wq': full[('xmem_wq', layer)],
                'wkv': full[('xmem_wkv', layer)], 'q_gain': w['xmem_q_gain'][layer], 'k_gain': w['xmem_k_gain'][layer],
                'wo': full[('xmem_wo', layer)]}

    even_p = {'norm': w['mix_norm'][0], 'w_in': full[('sbg_w_in', 0)], 'ln_gain': w['sgu_ln_gain'][0],
              'ln_bias': w['sgu_ln_bias'][0], 'sgu_w': w['sgu_w'][0], 'sgu_b': w['sgu_b'][0]}

    def late_weights_landed(gathered):
        absorb(late_units, gathered)
        even_p['w_out'] = full[('sbg_w_out', 0)]

    def mla_params():
        return {'norm': w['mix_norm'][1], 'w_in': full[('mla_w_in', 0)], 'q_lora_gain': w['mla_q_lora_gain'][0],
                'kv_lora_gain': w['mla_kv_lora_gain'][0], 'w_uq': full[('mla_w_uq', 0)],
                'w_ukv': full[('mla_w_ukv', 0)], 'q_gain': w['mla_q_gain'][0], 'k_gain': w['mla_k_gain'][0],
                'w_out': full[('mla_w_out', 0)]}

    saved = []
    h = _norm_fwd(x, w['ffn_pre_norm'][0], "ffn_pre0_norm")
    for layer in range(DEPTH):
        (x, h), s_pre = ffn_fwd(x, h, ffn_params('pre', layer), f"ffn_pre{layer}", w['mix_norm'][layer])
        if layer % 2 == 0:
            (x, h), s_mix = even_mixer_fwd(x, h, even_p, w['xmem_norm'][layer],
                                           job=CommJob('gather', [shards[u] for u in late_units]),
                                           after_job=late_weights_landed)
        else:
            (x, h), s_mix = mla_fwd(x, h, cos, sin, mla_params(), w['xmem_norm'][layer])
        (x, h), s_x = xattn_fwd(x, h, mem, xattn_params(layer), f"xmem{layer}", w['ffn_post_norm'][layer])
        following = w['ffn_pre_norm'][layer + 1] if layer + 1 < DEPTH else None
        (x, h), s_post = ffn_fwd(x, h, ffn_params('post', layer), f"ffn_post{layer}", following)
        saved.append((s_pre, s_mix, s_x, s_post))

    dx, loss = loss_head(x, target, tm=ROW_TM, name="loss_head")

    ready, riding, landed = {}, [], {}

    def offer(name, layer, g):
        ready[(name, layer)] = full_to_owner_major(g, BIG[name] - 1)

    def ride(name):
        def job_of(own):
            offer(name, 0, own['w_out'])
            riding[:] = list(ready)
            return CommJob('exchange', [ready.pop(u) for u in riding])
        return job_of

    per_layer = []
    for layer in reversed(range(DEPTH)):
        s_pre, s_mix, s_x, s_post = saved[layer]
        dx, g_post = ffn_bwd(dx, ffn_params('post', layer), s_post, f"ffn_post{layer}")
        offer('ffn_post_w_gu', layer, g_post['w_gu'])
        offer('ffn_post_w_down', layer, g_post['w_down'])
        dx, g_x = xattn_bwd(dx, xattn_params(layer), s_x, f"xmem{layer}")
        for n in ('wq', 'wkv', 'wo'):
            offer('xmem_' + n, layer, g_x[n])
        if layer % 2 == 0:
            dx, g_mix, arrived = even_mixer_bwd(dx, even_p, s_mix, job_of=ride('sbg_w_out'))
            landed.update(zip(riding, arrived))
            offer('sbg_w_in', 0, g_mix['w_in'])
        else:
            dx, g_mix, arrived = mla_bwd(dx, cos, sin, mla_params(), s_mix, job_of=ride('mla_w_out'))
            landed.update(zip(riding, arrived))
            for n in ('w_in', 'w_uq', 'w_ukv'):
                offer('mla_' + n, 0, g_mix[n])
        dx, g_pre = ffn_bwd(dx, ffn_params('pre', layer), s_pre, f"ffn_pre{layer}")
        offer('ffn_pre_w_gu', layer, g_pre['w_gu'])
        offer('ffn_pre_w_down', layer, g_pre['w_down'])
        per_layer.append((layer, g_pre, g_mix, g_x, g_post))
    per_layer.sort(key=lambda t: t[0])
    last_units = list(ready)
    landed.update(zip(last_units, comm_call('exchange', [ready.pop(u) for u in last_units], name="exchange_grads_last")))

    def stack(pick):
        return jnp.stack([pick(t) for t in per_layer])

    g_even, g_mla = per_layer[0][2], per_layer[1][2]
    small_grads = {
        'ffn_pre_norm': stack(lambda t: t[1]['norm']), 'mix_norm': stack(lambda t: t[2]['norm']),
        'sgu_ln_gain': g_even['ln_gain'][None], 'sgu_ln_bias': g_even['ln_bias'][None],
        'sgu_w': g_even['sgu_w'][None], 'sgu_b': g_even['sgu_b'][None],
        'mla_q_lora_gain': g_mla['q_lora_gain'][None], 'mla_kv_lora_gain': g_mla['kv_lora_gain'][None],
        'mla_q_gain': g_mla['q_gain'][None], 'mla_k_gain': g_mla['k_gain'][None],
        'xmem_norm': stack(lambda t: t[3]['norm']), 'xmem_mem_norm': stack(lambda t: t[3]['mem_norm']),
        'xmem_q_gain': stack(lambda t: t[3]['q_gain']), 'xmem_k_gain': stack(lambda t: t[3]['k_gain']),
        'ffn_post_norm': stack(lambda t: t[4]['norm']),
    }
    return loss, dx, small_grads, landed


def _device_slot():
    x, y, c = _me()
    return 4 * x + 2 * y + c


def kernel(x, mem, positions, ffn_pre_norm, ffn_pre_w_gu, ffn_pre_w_down, mix_norm, sbg_w_in, sgu_ln_gain, sgu_ln_bias, sgu_w, sgu_b, sbg_w_out, mla_w_in, mla_q_lora_gain, mla_kv_lora_gain, mla_w_uq, mla_w_ukv, mla_q_gain, mla_k_gain, mla_w_out, xmem_norm, xmem_mem_norm, xmem_wq, xmem_wkv, xmem_q_gain, xmem_k_gain, xmem_wo, ffn_post_norm, ffn_post_w_gu, ffn_post_w_down, loss_target, m_ffn_pre_norm, m_ffn_pre_w_gu, m_ffn_pre_w_down, m_mix_norm, m_sbg_w_in, m_sgu_ln_gain, m_sgu_ln_bias, m_sgu_w, m_sgu_b, m_sbg_w_out, m_mla_w_in, m_mla_q_lora_gain, m_mla_kv_lora_gain, m_mla_w_uq, m_mla_w_ukv, m_mla_q_gain, m_mla_k_gain, m_mla_w_out, m_xmem_norm, m_xmem_mem_norm, m_xmem_wq, m_xmem_wkv, m_xmem_q_gain, m_xmem_k_gain, m_xmem_wo, m_ffn_post_norm, m_ffn_post_w_gu, m_ffn_post_w_down, v_ffn_pre_norm, v_ffn_pre_w_gu, v_ffn_pre_w_down, v_mix_norm, v_sbg_w_in, v_sgu_ln_gain, v_sgu_ln_bias, v_sgu_w, v_sgu_b, v_sbg_w_out, v_mla_w_in, v_mla_q_lora_gain, v_mla_kv_lora_gain, v_mla_w_uq, v_mla_w_ukv, v_mla_q_gain, v_mla_k_gain, v_mla_w_out, v_xmem_norm, v_xmem_mem_norm, v_xmem_wq, v_xmem_wkv, v_xmem_q_gain, v_xmem_k_gain, v_xmem_wo, v_ffn_post_norm, v_ffn_post_w_gu, v_ffn_post_w_down):
    args = locals()
    w_in = {n: args[n] for n in WEIGHTS}
    m_in = {n: args["m_" + n] for n in WEIGHTS}
    v_in = {n: args["v_" + n] for n in WEIGHTS}
    slot = _device_slot()

    tiny = jnp.zeros((8, LANES), F32)
    for i, src in enumerate((w_in, m_in, v_in)):
        tiny = tiny.at[i, :64].set(src['mla_q_lora_gain'][0]).at[i + 3, :32].set(src['mla_kv_lora_gain'][0])
    tiny_all = comm_call('gather', [tiny], name="gather_lora_gains")[0]
    full_small = []
    for i, src in enumerate((w_in, m_in, v_in)):
        d = {n: src[n] for n in SMALL}
        d['mla_q_lora_gain'] = tiny_all[:, i, :64].reshape(1, MLA_Q_LORA)
        d['mla_kv_lora_gain'] = tiny_all[:, i + 3, :32].reshape(1, MLA_KV_LORA)
        full_small.append(d)
    w_small, m_small, v_small = full_small
    small_shapes = {n: w_small[n].shape for n in SMALL}

    shards = {(n, layer): w_in[n][layer].astype(BF16) for n in BIG for layer in range(w_in[n].shape[0])}
    loss, dx, grads, landed = local_step(x[0], mem[0], positions[0], loss_target[0], w_small, shards)
    loss = lax.psum(loss, ("x", "y", "c"))
    big_out = {n: adamw([landed[(n, layer)] for layer in range(w_in[n].shape[0])], w_in[n], m_in[n], v_in[n],
                        name=f"adamw_{n}") for n in BIG}

    small_parts = comm_call('gather', [pack_small(grads, small_shapes)], name="gather_small_grads")
    small_out = adamw(small_parts, pack_small(w_small, small_shapes)[None], pack_small(m_small, small_shapes)[None],
                      pack_small(v_small, small_shapes)[None], name="adamw_small")
    small_out = [unpack_small(t[0], small_shapes) for t in small_out]
    for d in small_out:
        for n, width in zip(GAIN_SHARDED, (64, 32)):
            d[n] = lax.dynamic_slice(d[n], (0, slot * width), (1, width))

    outs = [loss, dx[None]]
    for kind, small_d in enumerate(small_out):
        outs += [big_out[n][kind] if n in BIG else small_d[n] for n in WEIGHTS]
    return tuple(outs)
```

```python
import functools

import jax
import jax.numpy as jnp
from jax import lax
from jax.experimental import pallas as pl
from jax.experimental.pallas import tpu as pltpu

F32 = jnp.float32
BF16 = jnp.bfloat16
MESH = pl.DeviceIdType.MESH
N_DEV = 8

VMEM_LIMIT_BYTES = 56 * 1024 * 1024
LANES = 128

D_MODEL = 1024
DEPTH = 2
D_FF = 2816
EPS = 1e-6
SB_HEADS, SB_HEAD_DIM = 8, 64
SB_WIDTH = SB_HEADS * SB_HEAD_DIM
SG_GROUPS, SG_GROUP_DIM, SG_CHUNK = 8, 64, 128
SG_WIDTH = SG_GROUPS * SG_GROUP_DIM
MLA_HEADS, MLA_NOPE, MLA_ROPE, MLA_V = 16, 64, 32, 64
MLA_QK = MLA_NOPE + MLA_ROPE
MLA_Q_LORA, MLA_KV_LORA = 512, 256
ROPE_THETA = 10000.0
MEM_HEADS = 4
MEM_HEAD_DIM = D_MODEL // MEM_HEADS

ADAM_LR, ADAM_B1, ADAM_B2, ADAM_EPS, ADAM_WD, ADAM_STEP = 0.001, 0.9, 0.999, 1e-08, 0.01, 10

WEIGHTS = ['ffn_pre_norm', 'ffn_pre_w_gu', 'ffn_pre_w_down', 'mix_norm', 'sbg_w_in', 'sgu_ln_gain', 'sgu_ln_bias',
           'sgu_w', 'sgu_b', 'sbg_w_out', 'mla_w_in', 'mla_q_lora_gain', 'mla_kv_lora_gain', 'mla_w_uq', 'mla_w_ukv',
           'mla_q_gain', 'mla_k_gain', 'mla_w_out', 'xmem_norm', 'xmem_mem_norm', 'xmem_wq', 'xmem_wkv',
           'xmem_q_gain', 'xmem_k_gain', 'xmem_wo', 'ffn_post_norm', 'ffn_post_w_gu', 'ffn_post_w_down']
BIG = {'ffn_pre_w_gu': 2, 'ffn_pre_w_down': 1, 'sbg_w_in': 2, 'sbg_w_out': 1, 'mla_w_in': 1, 'mla_w_uq': 2,
       'mla_w_ukv': 2, 'mla_w_out': 1, 'xmem_wq': 1, 'xmem_wkv': 2, 'xmem_wo': 1, 'ffn_post_w_gu': 2,
       'ffn_post_w_down': 1}
GAIN_SHARDED = ('mla_q_lora_gain', 'mla_kv_lora_gain')
SMALL = [n for n in WEIGHTS if n not in BIG]
GRAD_WIRE = BF16
FFN_SAVE = BF16
SMALL_ROW_MULTIPLE = 16


def _cparams(sem=None):
    return pltpu.CompilerParams(dimension_semantics=sem, vmem_limit_bytes=VMEM_LIMIT_BYTES)


MM_TILE_CAP = 1408


def _pick(dim, cap=MM_TILE_CAP):
    if dim % LANES:
        return dim
    return max(t for t in range(LANES, min(dim, cap) + 1, LANES) if dim % t == 0)


def _rms(x, g):
    return x * lax.rsqrt(jnp.mean(x * x, axis=-1, keepdims=True) + EPS) * g


def pmm(a, b, *, ta=False, tb=False, out_dtype=F32, res=None, alpha=1.0, k_off_b=0, norm_out=None, norm_bwd=None,
        name):
    kdim, m = (a.shape if ta else a.shape[::-1])
    n = b.shape[0] if tb else b.shape[1]
    tm, tn, tk = _pick(m), _pick(n), _pick(kdim)
    whole_rows = norm_out is not None or norm_bwd is not None
    if whole_rows:
        assert tn == n
    if norm_bwd is not None:
        tm = min(tm, 512)
    nk = kdim // tk
    dims = (((0 if ta else 1,), (1 if tb else 0,)), ((), ()))
    n_extra = (res is not None) + (norm_out is not None) + (0 if norm_bwd is None else 2 + (norm_bwd[2] is not None))

    def body(*refs):
        a_ref, b_ref = refs[:2]
        extra = list(refs[2:2 + n_extra])
        outs, acc_ref = refs[2 + n_extra:-1], refs[-1]
        i, k = pl.program_id(0), pl.program_id(2)

        @pl.when(k == 0)
        def _():
            acc_ref[...] = jnp.zeros_like(acc_ref)

        acc_ref[...] += lax.dot_general(a_ref[...].astype(BF16), b_ref[...].astype(BF16), dims,
                                        preferred_element_type=F32)

        @pl.when(k == nk - 1)
        def _():
            r = acc_ref[...]
            if alpha != 1.0:
                r = r * alpha
            if res is not None:
                r = extra.pop(0)[...] + r
            if norm_bwd is None:
                outs[0][...] = r.astype(out_dtype)
            if norm_out is not None:
                outs[1][...] = _rms(r, extra.pop(0)[...]).astype(BF16)
            if norm_bwd is not None:
                x_ref, g_ref = extra.pop(0), extra.pop(0)
                _, pull = jax.vjp(_rms, x_ref[...], g_ref[...])
                dx, dg = pull(r)
                if norm_bwd[2] is not None:
                    dx = dx + extra.pop(0)[...]
                outs[0][...] = dx

                @pl.when(i == 0)
                def _():
                    outs[1][...] = dg

                @pl.when(i != 0)
                def _():
                    outs[1][...] += dg

    gi, gj = m // tm, n // tn
    a_bytes, b_bytes = a.size * a.dtype.itemsize, (n * kdim) * b.dtype.itemsize
    j_outer = not whole_rows and nk == 1 and gj * a_bytes + b_bytes < a_bytes + gi * b_bytes
    grid = (gj, gi, nk) if j_outer else (gi, gj, nk)

    def spec(block, index):
        return pl.BlockSpec(block, (lambda j, i, k: index(i, j, k)) if j_outer else index)

    a_spec = spec((tk, tm), lambda i, j, k: (k, i)) if ta else spec((tm, tk), lambda i, j, k: (i, k))
    b_spec = (spec((tn, tk), lambda i, j, k: (j, k + k_off_b)) if tb
              else spec((tk, tn), lambda i, j, k: (k + k_off_b, j)))
    o_spec = spec((tm, tn), lambda i, j, k: (i, j))
    g_spec = spec((1, tn), lambda i, j, k: (0, 0))
    ins, in_specs = [a, b], [a_spec, b_spec]
    if res is not None:
        ins.append(res)
        in_specs.append(o_spec)
    out_shape, out_specs = [jax.ShapeDtypeStruct((m, n), out_dtype)], [o_spec]
    if norm_out is not None:
        ins.append(norm_out)
        in_specs.append(g_spec)
        out_shape.append(jax.ShapeDtypeStruct((m, n), BF16))
        out_specs.append(o_spec)
    if norm_bwd is not None:
        ins += [t for t in norm_bwd if t is not None]
        in_specs += [o_spec, g_spec] + ([o_spec] if norm_bwd[2] is not None else [])
        out_shape = [jax.ShapeDtypeStruct((m, n), F32), jax.ShapeDtypeStruct((1, n), F32)]
        out_specs = [o_spec, g_spec]
    result = pl.pallas_call(
        body, name=name, grid=grid, in_specs=in_specs, out_specs=out_specs,
        out_shape=out_shape, scratch_shapes=[pltpu.VMEM((tm, tn), F32)],
        compiler_params=_cparams(("arbitrary" if norm_bwd is not None else "parallel", "parallel", "arbitrary")),
    )(*ins)
    return result if whole_rows else result[0]


def ffn_gate_up(h, w_gu, *, name):
    m, kdim = h.shape
    n = w_gu.shape[1] // 2
    tm, tn = min(_pick(m), 512), _pick(n)
    up_off = n // tn

    def body(a_ref, bg_ref, bu_ref, gate_ref, up_ref, act_ref):
        av = a_ref[...].astype(BF16)
        gate = _dg(av, bg_ref[...].astype(BF16), 1, 0)
        up = _dg(av, bu_ref[...].astype(BF16), 1, 0)
        gate_ref[...] = gate.astype(gate_ref.dtype)
        up_ref[...] = up.astype(up_ref.dtype)
        act_ref[...] = (jax.nn.silu(gate) * up).astype(BF16)

    o_spec = pl.BlockSpec((tm, tn), lambda j, i: (i, j))
    return pl.pallas_call(
        body, name=name, grid=(n // tn, m // tm),
        in_specs=[pl.BlockSpec((tm, kdim), lambda j, i: (i, 0)), pl.BlockSpec((kdim, tn), lambda j, i: (0, j)),
                  pl.BlockSpec((kdim, tn), lambda j, i: (0, j + up_off))],
        out_specs=[o_spec] * 3,
        out_shape=[jax.ShapeDtypeStruct((m, n), FFN_SAVE), jax.ShapeDtypeStruct((m, n), FFN_SAVE),
                   jax.ShapeDtypeStruct((m, n), BF16)],
        compiler_params=_cparams(("parallel", "parallel")),
    )(h, w_gu, w_gu)


def ffn_gate_up_bwd(dy, w_down, gate, up, *, alpha, name):
    m, kdim = dy.shape
    n = w_down.shape[0]
    tm, tn = min(_pick(m), 512), _pick(n)

    def body(a_ref, b_ref, gate_ref, up_ref, dgate_ref, dup_ref):
        d_act = _dg(a_ref[...].astype(BF16), b_ref[...].astype(BF16), 1, 1) * alpha
        _, pull = jax.vjp(lambda g, u: jax.nn.silu(g) * u, gate_ref[...].astype(F32), up_ref[...].astype(F32))
        d_gate, d_up = pull(d_act)
        dgate_ref[...] = d_gate.astype(BF16)
        dup_ref[...] = d_up.astype(BF16)

    o_spec = pl.BlockSpec((tm, tn), lambda j, i: (i, j))
    return pl.pallas_call(
        body, name=name, grid=(n // tn, m // tm),
        in_specs=[pl.BlockSpec((tm, kdim), lambda j, i: (i, 0)), pl.BlockSpec((tn, kdim), lambda j, i: (j, 0)),
                  o_spec, o_spec],
        out_specs=[o_spec] * 2, out_shape=[jax.ShapeDtypeStruct((m, n), BF16)] * 2,
        compiler_params=_cparams(("parallel", "parallel")),
    )(dy, w_down, gate, up)


def _dg(a, b, ca, cb):
    return lax.dot_general(a, b, (((ca,), (cb,)), ((), ())), preferred_element_type=F32)


@jax.custom_vjp
def bdot(a, b):
    return _dg(a.astype(BF16), b.astype(BF16), 1, 0)


def _bdot_fwd(a, b):
    ab, bb = a.astype(BF16), b.astype(BF16)
    return _dg(ab, bb, 1, 0), (ab, bb)


def _bdot_bwd(saved, g):
    ab, bb = saved
    gb = g.astype(BF16)
    return _dg(gb, bb, 1, 1), _dg(ab, gb, 0, 0)


bdot.defvjp(_bdot_fwd, _bdot_bwd)


@jax.custom_vjp
def bdot_nt(a, b):
    return _dg(a.astype(BF16), b.astype(BF16), 1, 1)


def _bdot_nt_fwd(a, b):
    ab, bb = a.astype(BF16), b.astype(BF16)
    return _dg(ab, bb, 1, 1), (ab, bb)


def _bdot_nt_bwd(saved, g):
    ab, bb = saved
    gb = g.astype(BF16)
    return _dg(gb, bb, 1, 0), _dg(gb, ab, 0, 0)


bdot_nt.defvjp(_bdot_nt_fwd, _bdot_nt_bwd)


class ColGroups:
    def __init__(self, arr, width):
        self.arr, self.width = arr, width
        self.shape, self.dtype, self.ndim = arr.shape, arr.dtype, 3


def _plain(a):
    return a.arr if isinstance(a, ColGroups) else a


def _row_spec(arr, tm):
    if isinstance(arr, ColGroups):
        return pl.BlockSpec((tm, arr.width), lambda r, g: (r, g))
    if arr.ndim == 3:
        return pl.BlockSpec((None, tm, arr.shape[2]), lambda r, g: (g, r, 0))
    return pl.BlockSpec((tm, arr.shape[1]), lambda r, g: (r, 0))


def _gparam_spec(arr):
    return pl.BlockSpec((None,) + arr.shape[1:], lambda r, g: (g, 0, 0))


def _whole_spec(arr):
    nd = arr.ndim
    return pl.BlockSpec(arr.shape, lambda r, g: (0,) * nd)


def _groups(rows, gparams):
    gs = {a.shape[1] // a.width if isinstance(a, ColGroups) else a.shape[0] for a in rows if a.ndim == 3}
    gs |= {a.shape[0] for a in gparams}
    assert len(gs) <= 1
    return gs.pop() if gs else 1


def prow(fn, rows, gparams=(), params=(), *, outs, tm, name):
    rows, gparams, params = list(rows), list(gparams), list(params)
    n_groups = _groups(rows, gparams)
    n_rows = rows[0].shape[-2]
    n_in = len(rows) + len(gparams) + len(params)

    def body(*refs):
        vals = [r[...] for r in refs[:n_in]]
        res = fn(*vals)
        for o_ref, r in zip(refs[n_in:], res, strict=True):
            o_ref[...] = r.astype(o_ref.dtype)

    out_shape, out_specs = [], []
    for width, dtype, grouped in outs:
        if grouped == 'cols':
            out_shape.append(jax.ShapeDtypeStruct((n_rows, n_groups * width), dtype))
            out_specs.append(_row_spec(ColGroups(out_shape[-1], width), tm))
            continue
        shp = (n_groups, n_rows, width) if grouped else (n_rows, width)
        out_shape.append(jax.ShapeDtypeStruct(shp, dtype))
        out_specs.append(_row_spec(out_shape[-1], tm))
    return pl.pallas_call(
        body, name=name, grid=(n_rows // tm, n_groups),
        in_specs=[_row_spec(a, tm) for a in rows] + [_gparam_spec(a) for a in gparams] + [_whole_spec(a) for a in params],
        out_specs=out_specs, out_shape=out_shape,
        compiler_params=_cparams(("parallel", "arbitrary")),
    )(*[_plain(a) for a in rows], *gparams, *params)


def prow_vjp(fn, rows, gparams=(), params=(), *, cts, row_grad, adds=None, row_dtypes=None, gparam_grad=None,
             param_grad=None, tm, name):
    rows, gparams, params, cts = list(rows), list(gparams), list(params), list(cts)
    gparam_grad = list(gparam_grad) if gparam_grad is not None else [True] * len(gparams)
    param_grad = list(param_grad) if param_grad is not None else [True] * len(params)
    n_groups = _groups(rows + cts, gparams)
    n_rows = rows[0].shape[-2]
    want_rows = [i for i, w in enumerate(row_grad) if w]
    adds = list(adds) if adds is not None else [None] * len(want_rows)
    row_dtypes = list(row_dtypes) if row_dtypes is not None else [F32] * len(want_rows)
    add_arrays = [a for a in adds if a is not None]
    n_r, n_g, n_p, n_c, n_a = len(rows), len(gparams), len(params), len(cts), len(add_arrays)
    mask = list(row_grad) + gparam_grad + param_grad

    def body(*refs):
        r_id, g_id = pl.program_id(0), pl.program_id(1)
        n_in = n_r + n_g + n_p
        vals = [r[...] for r in refs[:n_in]]
        ct_vals = tuple(r[...].astype(F32) for r in refs[n_in:n_in + n_c])
        add_refs = list(refs[n_in + n_c:n_in + n_c + n_a])
        out_refs = list(refs[n_in + n_c + n_a:])
        diff_idx = [i for i, w in enumerate(mask) if w]

        def wrapped(*diff):
            full = list(vals)
            for i, d in zip(diff_idx, diff):
                full[i] = d
            return tuple(fn(*full))

        _, pull = jax.vjp(wrapped, *[vals[i].astype(F32) for i in diff_idx])
        grads = dict(zip(diff_idx, pull(ct_vals)))
        k = 0
        for j, i in enumerate(want_rows):
            o_ref = out_refs[k]
            k += 1
            gval = grads[i]
            if adds[j] is not None:
                gval = gval + add_refs.pop(0)[...].astype(F32)
            if rows[i].ndim == 2 and n_groups > 1:
                @pl.when(g_id == 0)
                def _(o_ref=o_ref, gval=gval):
                    o_ref[...] = gval.astype(o_ref.dtype)

                @pl.when(g_id != 0)
                def _(o_ref=o_ref, gval=gval):
                    o_ref[...] += gval.astype(o_ref.dtype)
            else:
                o_ref[...] = gval.astype(o_ref.dtype)
        for i in range(n_g):
            if not gparam_grad[i]:
                continue
            o_ref = out_refs[k]
            k += 1
            gval = grads[n_r + i]

            @pl.when(r_id == 0)
            def _(o_ref=o_ref, gval=gval):
                o_ref[g_id] = gval

            @pl.when(r_id != 0)
            def _(o_ref=o_ref, gval=gval):
                o_ref[g_id] += gval
        for i in range(n_p):
            if not param_grad[i]:
                continue
            o_ref = out_refs[k]
            k += 1
            gval = grads[n_r + n_g + i]
            first = jnp.logical_and(r_id == 0, g_id == 0)

            @pl.when(first)
            def _(o_ref=o_ref, gval=gval):
                o_ref[...] = gval

            @pl.when(jnp.logical_not(first))
            def _(o_ref=o_ref, gval=gval):
                o_ref[...] += gval

    out_shape, out_specs = [], []
    for j, i in enumerate(want_rows):
        out_shape.append(jax.ShapeDtypeStruct(rows[i].shape, row_dtypes[j]))
        out_specs.append(_row_spec(rows[i], tm))
    for i in range(n_g):
        if gparam_grad[i]:
            out_shape.append(jax.ShapeDtypeStruct(gparams[i].shape, F32))
            out_specs.append(_whole_spec(gparams[i]))
    for i in range(n_p):
        if param_grad[i]:
            out_shape.append(jax.ShapeDtypeStruct(params[i].shape, F32))
            out_specs.append(_whole_spec(params[i]))
    return pl.pallas_call(
        body, name=name, grid=(n_rows // tm, n_groups),
        in_specs=([_row_spec(a, tm) for a in rows] + [_gparam_spec(a) for a in gparams]
                  + [_whole_spec(a) for a in params] + [_row_spec(a, tm) for a in cts]
                  + [_row_spec(a, tm) for a in add_arrays]),
        out_specs=out_specs, out_shape=out_shape,
        compiler_params=_cparams(("arbitrary", "arbitrary")),
    )(*[_plain(a) for a in rows], *gparams, *params, *[_plain(a) for a in cts], *add_arrays)


def f_rms(x, g):
    return (_rms(x.astype(F32), g),)


def f_gate_prep(z, ln_g, ln_b):
    act = jax.nn.gelu(z)
    u, gg = act[:, :SG_WIDTH], act[:, SG_WIDTH:]
    mu = jnp.mean(gg, axis=-1, keepdims=True)
    var = jnp.mean(jnp.square(gg - mu), axis=-1, keepdims=True)
    return u, (gg - mu) * lax.rsqrt(var + EPS) * ln_g + ln_b


def f_spatial_gate(gn, u, w, b):
    t = lax.broadcasted_iota(jnp.int32, w.shape, 0)
    s = lax.broadcasted_iota(jnp.int32, w.shape, 1)
    w_causal = jnp.where(s <= t, w, 0.0)
    mixed = [bdot(w_causal, gn[i:i + SG_CHUNK]) + b for i in range(0, gn.shape[0], SG_CHUNK)]
    return (u * (mixed[0] if len(mixed) == 1 else jnp.concatenate(mixed, axis=0)),)


def _rope_tail(t, cos, sin):
    half = MLA_ROPE // 2
    t1, t2 = t[:, MLA_NOPE:MLA_NOPE + half], t[:, MLA_NOPE + half:]
    return jnp.concatenate([t[:, :MLA_NOPE], t1 * cos - t2 * sin, t1 * sin + t2 * cos], axis=-1)


def f_mla_q(q, cos, sin, g):
    return (_rope_tail(f_rms(q, g)[0], cos, sin),)


def f_mla_k(k_nope, k_r, cos, sin, g):
    return (_rope_tail(f_rms(jnp.concatenate([k_nope, k_r], axis=-1), g)[0], cos, sin),)


def f_xattn(q, k, v, q_g, k_g):
    qn, kn = f_rms(q, q_g)[0], f_rms(k, k_g)[0]
    sc = bdot_nt(qn, kn) * (MEM_HEAD_DIM ** -0.5)
    return (bdot(jax.nn.softmax(sc, axis=-1), v),)


def _split_dot(x, tri):
    hi = x.astype(BF16)
    lo = (x - hi.astype(F32)).astype(BF16)
    return _dg(hi, tri, 1, 0) + _dg(lo, tri, 1, 0)


def _tri(tk, cmp):
    j = lax.broadcasted_iota(jnp.int32, (tk, tk), 0)
    s = lax.broadcasted_iota(jnp.int32, (tk, tk), 1)
    return cmp(j, s).astype(BF16)


SCAN_CHUNK = 256


def _row_scan(x, tri, reverse):
    n = x.shape[1] // SCAN_CHUNK
    chunks = [x[:, i * SCAN_CHUNK:(i + 1) * SCAN_CHUNK] for i in range(n)]
    out, seen = [None] * n, None
    for i in (reversed(range(n)) if reverse else range(n)):
        local = _split_dot(chunks[i], tri)
        out[i] = local if seen is None else local + seen
        total = jnp.sum(chunks[i], axis=1, keepdims=True)
        seen = total if seen is None else seen + total
    return (out[0] if n == 1 else jnp.concatenate(out, axis=1)), seen


def _att_specs(s_len, tq, dq, dv):
    q_spec = pl.BlockSpec((None, tq, dq), lambda h, i: (h, i, 0))
    k_spec = pl.BlockSpec((None, s_len, dq), lambda h, i: (h, 0, 0))
    v_spec = pl.BlockSpec((None, s_len, dv), lambda h, i: (h, 0, 0))
    o_spec = pl.BlockSpec((None, tq, dv), lambda h, i: (h, i, 0))
    r_spec = pl.BlockSpec((None, tq, 1), lambda h, i: (h, i, 0))
    return q_spec, k_spec, v_spec, o_spec, r_spec


def _key_blocks(qi, tq, tk):
    return (qi * tq) // tk, ((qi + 1) * tq + tk - 1) // tk


def _keep(qi, j, tq, tk, strict):
    row = qi * tq + lax.broadcasted_iota(jnp.int32, (tq, tk), 0)
    col = j * tk + lax.broadcasted_iota(jnp.int32, (tq, tk), 1)
    return col < row if strict else col <= row


def _log_sigmoid(z):
    return jnp.minimum(z, 0.0) - jnp.log(1.0 + jnp.exp(-jnp.abs(z)))


def sb_fwd(q, k, v, *, tq, tk, name, job=None):
    n_heads, s_len, d = q.shape
    scale = SB_HEAD_DIM ** -0.5

    def body(q_ref, k_ref, v_ref, o_ref, tot_ref):
        qi = pl.program_id(1)
        qv = q_ref[...]
        upper = _tri(SCAN_CHUNK, lambda j, s: j > s)
        n_full, n_all = _key_blocks(qi, tq, tk)

        def make_step(masked, last):
            def step(jj, carry):
                acc, rest = carry
                j = last - 1 - jj
                sl = pl.ds(pl.multiple_of(j * tk, tk), tk)
                ks, vs = k_ref[sl, :], v_ref[sl, :]
                z = _dg(qv, ks, 1, 1) * scale
                log_beta = _log_sigmoid(z)
                log_stay = log_beta - z
                if masked:
                    valid = _keep(qi, j, tq, tk, True)
                    log_stay = jnp.where(valid, log_stay, 0.0)
                after, total = _row_scan(log_stay, upper, True)
                w = jnp.exp(log_beta + after + rest)
                if masked:
                    w = jnp.where(valid, w, 0.0)
                acc = acc + _dg(w.astype(BF16), vs, 1, 0)
                return acc, rest + total
            return step

        carry = (jnp.zeros((tq, d), F32), jnp.zeros((tq, 1), F32))
        carry = lax.fori_loop(0, n_all - n_full, make_step(True, n_all), carry)
        acc, rest = lax.fori_loop(0, n_full, make_step(False, n_full), carry)
        o_ref[...] = acc
        tot_ref[...] = rest

    q_spec, k_spec, v_spec, o_spec, r_spec = _att_specs(s_len, tq, d, d)
    return ride_call(
        job, body, name=name, grid=(n_heads, s_len // tq), in_specs=[q_spec, k_spec, v_spec],
        out_specs=[o_spec, r_spec],
        out_shape=[jax.ShapeDtypeStruct((n_heads, s_len, d), F32), jax.ShapeDtypeStruct((n_heads, s_len, 1), F32)],
        ins=[q, k, v], sem=("parallel", "arbitrary"))


def sb_bwd(q, k, v, tot, do, *, tq, tk, name, job=None):
    n_heads, s_len, d = q.shape
    scale = SB_HEAD_DIM ** -0.5

    def body(q_ref, k_ref, v_ref, tot_ref, do_ref, dq_ref, dk_ref, dv_ref):
        qi = pl.program_id(1)

        @pl.when(qi == 0)
        def _():
            dk_ref[...] = jnp.zeros_like(dk_ref)
            dv_ref[...] = jnp.zeros_like(dv_ref)

        qv = q_ref[...]
        dob = do_ref[...].astype(BF16)
        total = tot_ref[...]
        incl = _tri(SCAN_CHUNK, lambda j, s: j <= s)
        excl = _tri(SCAN_CHUNK, lambda j, s: j < s)
        n_full, n_all = _key_blocks(qi, tq, tk)

        def make_step(masked):
            def step(j, carry):
                dq, stay_before, dl_before = carry
                sl = pl.ds(pl.multiple_of(j * tk, tk), tk)
                ks, vs = k_ref[sl, :], v_ref[sl, :]
                z = _dg(qv, ks, 1, 1) * scale
                log_beta = _log_sigmoid(z)
                log_stay = log_beta - z
                if masked:
                    valid = _keep(qi, j, tq, tk, True)
                    log_stay = jnp.where(valid, log_stay, 0.0)
                stay_upto, stay_sum = _row_scan(log_stay, incl, False)
                w = jnp.exp(log_beta + (total - stay_before) - stay_upto)
                if masked:
                    w = jnp.where(valid, w, 0.0)
                dl = _dg(dob, vs, 1, 1) * w
                dl_upto, dl_sum = _row_scan(dl, excl, False)
                dl_prefix = dl_upto + dl_before
                beta = jnp.exp(log_beta)
                dz = (dl * (1.0 - beta) - beta * dl_prefix) * scale
                if masked:
                    dz = jnp.where(valid, dz, 0.0)
                dzb = dz.astype(BF16)
                dq = dq + _dg(dzb, ks, 1, 0)
                dk_ref[sl, :] += _dg(dzb, qv, 0, 0)
                dv_ref[sl, :] += _dg(w.astype(BF16), dob, 0, 0)
                return dq, stay_before + stay_sum, dl_before + dl_sum
            return step

        zero = jnp.zeros((tq, 1), F32)
        carry = lax.fori_loop(0, n_full, make_step(False), (jnp.zeros((tq, d), F32), zero, zero))
        dq, _, _ = lax.fori_loop(n_full, n_all, make_step(True), carry)
        dq_ref[...] = dq

    q_spec, k_spec, v_spec, o_spec, r_spec = _att_specs(s_len, tq, d, d)
    shp = jax.ShapeDtypeStruct((n_heads, s_len, d), F32)
    return ride_call(
        job, body, name=name, grid=(n_heads, s_len // tq), in_specs=[q_spec, k_spec, v_spec, r_spec, o_spec],
        out_specs=[q_spec, k_spec, v_spec], out_shape=[shp, shp, shp], ins=[q, k, v, tot, do],
        sem=("arbitrary", "arbitrary"))


NEG_BIG = -1e30


def _lower_left(rows, cols):
    r = lax.broadcasted_iota(jnp.int32, (rows, cols), 0)
    c = lax.broadcasted_iota(jnp.int32, (rows, cols), 1)
    return c <= r


def _prep_specs(tq, q_prep):
    cos, _, gain = q_prep
    rope_spec = pl.BlockSpec((tq, cos.shape[1]), lambda h, i: (i, 0))
    return [rope_spec, rope_spec, pl.BlockSpec(gain.shape, lambda h, i: (0, 0))]


def sm_fwd(q, k, v, *, tq, tk, name, q_prep=None):
    n_heads, s_len, dq = q.shape
    dv = v.shape[2]
    scale = dq ** -0.5
    assert tq == tk
    half = tk // 2
    n_prep = 0 if q_prep is None else 3

    def body(*refs):
        q_ref, prep_refs = refs[0], refs[1:1 + n_prep]
        k_ref, v_ref, o_ref, lse_ref = refs[1 + n_prep:]
        qi = pl.program_id(1)
        qv = q_ref[...]
        if q_prep is not None:
            qv = f_mla_q(qv, *[r[...] for r in prep_refs])[0].astype(BF16)

        def attend(carry, q_rows, keys, keep):
            acc, m, l = carry
            sc = _dg(q_rows, k_ref[keys, :], 1, 1) * scale
            if keep is not None:
                sc = jnp.where(keep, sc, NEG_BIG)
            m_new = jnp.maximum(m, jnp.max(sc, axis=1, keepdims=True))
            p = jnp.exp(sc - m_new)
            fade = jnp.exp(m - m_new)
            return (fade * acc + _dg(p.astype(BF16), v_ref[keys, :], 1, 0), m_new,
                    fade * l + jnp.sum(p, axis=1, keepdims=True))

        carry = (jnp.zeros((tq, dv), F32), jnp.full((tq, 1), NEG_BIG, F32), jnp.zeros((tq, 1), F32))
        carry = lax.fori_loop(
            0, qi, lambda j, c: attend(c, qv, pl.ds(pl.multiple_of(j * tk, tk), tk), None), carry)
        base = pl.multiple_of(qi * tk, tk)
        carry = attend(carry, qv, pl.ds(base, half), _lower_left(tq, half))
        low = attend(tuple(t[half:] for t in carry), qv[half:], pl.ds(pl.multiple_of(base + half, half), half),
                     _lower_left(half, half))
        acc, m, l = (jnp.concatenate([t[:half], u], axis=0) for t, u in zip(carry, low))
        o_ref[...] = acc / l
        lse_ref[...] = m + jnp.log(l)

    q_spec, k_spec, v_spec, o_spec, r_spec = _att_specs(s_len, tq, dq, dv)
    prep = [] if q_prep is None else list(q_prep)
    return pl.pallas_call(
        body, name=name, grid=(n_heads, s_len // tq),
        in_specs=[q_spec] + ([] if q_prep is None else _prep_specs(tq, q_prep)) + [k_spec, v_spec],
        out_specs=[o_spec, r_spec],
        out_shape=[jax.ShapeDtypeStruct((n_heads, s_len, dv), F32), jax.ShapeDtypeStruct((n_heads, s_len, 1), F32)],
        compiler_params=_cparams(("parallel", "arbitrary")),
    )(q, *prep, k, v)


def sm_bwd(q, k, v, o, lse, do, *, tq, tk, name, q_prep=None, job=None):
    n_heads, s_len, dq = q.shape
    dv = v.shape[2]
    scale = dq ** -0.5
    assert tq == tk
    half = tk // 2
    n_prep = 0 if q_prep is None else 3

    def body(*refs):
        q_ref, prep_refs = refs[0], refs[1:1 + n_prep]
        k_ref, v_ref, o_ref, lse_ref, do_ref, dq_ref, dk_ref, dv_ref = refs[1 + n_prep:9 + n_prep]
        head, qi = pl.program_id(0), pl.program_id(1)

        @pl.when(qi == 0)
        def _():
            dk_ref[...] = jnp.zeros_like(dk_ref)
            dv_ref[...] = jnp.zeros_like(dv_ref)

        q_raw = q_ref[...]
        prep_vals = [r[...] for r in prep_refs]
        qv = q_raw if q_prep is None else f_mla_q(q_raw, *prep_vals)[0].astype(BF16)
        do = do_ref[...]
        dob = do.astype(BF16)
        delta = jnp.sum(do * o_ref[...], axis=1, keepdims=True)
        lse_v = lse_ref[...]

        def attend(rows, keys, keep):
            ks, vs = k_ref[keys, :], v_ref[keys, :]
            p = jnp.exp(_dg(qv[rows], ks, 1, 1) * scale - lse_v[rows])
            if keep is not None:
                p = jnp.where(keep, p, 0.0)
            dv_ref[keys, :] += _dg(p.astype(BF16), dob[rows], 0, 0)
            ds = (p * (_dg(dob[rows], vs, 1, 1) - delta[rows]) * scale).astype(BF16)
            dk_ref[keys, :] += _dg(ds, qv[rows], 0, 0)
            return _dg(ds, ks, 1, 0)

        everything = slice(None)
        dq_acc = lax.fori_loop(
            0, qi, lambda j, acc: acc + attend(everything, pl.ds(pl.multiple_of(j * tk, tk), tk), None),
            jnp.zeros((tq, dq), F32))
        base = pl.multiple_of(qi * tk, tk)
        dq_acc = dq_acc + attend(everything, pl.ds(base, half), _lower_left(tq, half))
        low = attend(slice(half, None), pl.ds(pl.multiple_of(base + half, half), half), _lower_left(half, half))
        dq_acc = jnp.concatenate([dq_acc[:half], dq_acc[half:] + low], axis=0)
        if q_prep is None:
            dq_ref[...] = dq_acc
        else:
            cos, sin, gain = prep_vals
            _, pull = jax.vjp(lambda t, g: f_mla_q(t, cos, sin, g)[0], q_raw, gain)
            dq_raw, d_gain = pull(dq_acc)
            dq_ref[...] = dq_raw.astype(dq_ref.dtype)
            dgain_ref = refs[9 + n_prep]
            first = jnp.logical_and(head == 0, qi == 0)

            @pl.when(first)
            def _():
                dgain_ref[...] = d_gain

            @pl.when(jnp.logical_not(first))
            def _():
                dgain_ref[...] += d_gain

    q_spec, k_spec, v_spec, o_spec, r_spec = _att_specs(s_len, tq, dq, dv)
    out_specs = [q_spec, k_spec, v_spec]
    out_shape = [jax.ShapeDtypeStruct((n_heads, s_len, dq), F32 if q_prep is None else BF16),
                 jax.ShapeDtypeStruct((n_heads, s_len, dq), F32), jax.ShapeDtypeStruct((n_heads, s_len, dv), F32)]
    prep, prep_specs = [], []
    if q_prep is not None:
        prep, prep_specs = list(q_prep), _prep_specs(tq, q_prep)
        out_specs.append(prep_specs[2])
        out_shape.append(jax.ShapeDtypeStruct(q_prep[2].shape, F32))
    return ride_call(
        job, body, name=name, grid=(n_heads, s_len // tq),
        in_specs=[q_spec] + prep_specs + [k_spec, v_spec, o_spec, r_spec, o_spec], out_specs=out_specs,
        out_shape=out_shape, ins=[q, *prep, k, v, o, lse, do], sem=("arbitrary", "arbitrary"))


def loss_head(y, target, *, tm, name):
    n_rows, width = y.shape

    def body(y_ref, t_ref, dy_ref, loss_ref):
        diff = y_ref[...] - t_ref[...]
        dy_ref[...] = diff / width
        part = 0.5 * jnp.sum(jnp.mean(diff * diff, axis=-1, keepdims=True), axis=0, keepdims=True)

        @pl.when(pl.program_id(0) == 0)
        def _():
            loss_ref[...] = jnp.zeros_like(loss_ref)

        loss_ref[...] += jnp.broadcast_to(part, loss_ref.shape)

    spec = pl.BlockSpec((tm, width), lambda r: (r, 0))
    dy, loss = pl.pallas_call(
        body, name=name, grid=(n_rows // tm,), in_specs=[spec, spec],
        out_specs=[spec, pl.BlockSpec((8, LANES), lambda r: (0, 0))],
        out_shape=[jax.ShapeDtypeStruct(y.shape, F32), jax.ShapeDtypeStruct((8, LANES), F32)],
        compiler_params=_cparams(("arbitrary",)),
    )(y, target)
    return dy, loss[0, 0]


ADAM_TILE_ELEMS = 256 * 1024


def _adam_rows(n_rows, width):
    fits = [t for t in range(16, n_rows + 1, 16) if n_rows % t == 0 and t * width <= ADAM_TILE_ELEMS]
    return max(fits) if fits else n_rows


def adamw(parts, w, m, v, *, name):
    n_layers, n_rows, width = w.shape
    assert len(parts) == n_layers
    tm = _adam_rows(n_rows, width)
    n_tiles = n_rows // tm

    def body(*refs):
        p_refs = refs[:n_layers]
        w_ref, m_ref, v_ref, g_ref, d_ref, nm_ref, nv_ref = refs[n_layers:]
        layer = pl.program_id(0)
        for this, p_ref in enumerate(p_refs):
            @pl.when(layer == this)
            def _(p_ref=p_ref):
                g = p_ref[0].astype(F32)
                for i in range(1, N_DEV):
                    g = g + p_ref[i].astype(F32)
                m_new = ADAM_B1 * m_ref[...] + (1.0 - ADAM_B1) * g
                v_new = ADAM_B2 * v_ref[...] + (1.0 - ADAM_B2) * jnp.square(g)
                m_hat = m_new / (1.0 - ADAM_B1 ** ADAM_STEP)
                v_hat = v_new / (1.0 - ADAM_B2 ** ADAM_STEP)
                g_ref[...] = g
                d_ref[...] = -ADAM_LR * (m_hat / (jnp.sqrt(v_hat) + ADAM_EPS) + ADAM_WD * w_ref[...])
                nm_ref[...] = m_new
                nv_ref[...] = v_new

    def part_spec(this):
        def index(layer, r):
            return 0, jnp.where(layer == this, r, jnp.where(layer < this, 0, n_tiles - 1)), 0
        return pl.BlockSpec((N_DEV, tm, width), index)

    spec = pl.BlockSpec((None, tm, width), lambda layer, r: (layer, r, 0))
    shp = jax.ShapeDtypeStruct(w.shape, F32)
    return pl.pallas_call(
        body, name=name, grid=(n_layers, n_tiles),
        in_specs=[part_spec(this) for this in range(n_layers)] + [spec, spec, spec],
        out_specs=[spec] * 4, out_shape=[shp] * 4, compiler_params=_cparams(("arbitrary", "arbitrary")),
    )(*parts, w, m, v)


def _me():
    return lax.axis_index("x"), lax.axis_index("y"), lax.axis_index("c")


N_PEERS = N_DEV - 1


class CommJob:
    def __init__(self, kind, arrays):
        self.kind, self.arrays, self.n = kind, list(arrays), len(arrays)

    def out_shape(self):
        lead = (N_DEV,) if self.kind == 'gather' else ()
        return [jax.ShapeDtypeStruct(lead + a.shape, a.dtype) for a in self.arrays]

    def scratch(self):
        return [pltpu.SemaphoreType.DMA((N_PEERS * self.n,)), pltpu.SemaphoreType.DMA((N_PEERS * self.n,)),
                pltpu.SemaphoreType.DMA((self.n,))]

    def phases(self, in_refs, out_refs, send_sems, recv_sems, local_sems):
        n = self.n
        x, y, c = _me()

        def remote(i, k, src, dst, to):
            return pltpu.make_async_remote_copy(
                src_ref=src, dst_ref=dst, send_sem=send_sems.at[N_PEERS * i + k],
                recv_sem=recv_sems.at[N_PEERS * i + k], device_id=to, device_id_type=MESH)

        if self.kind == 'gather':
            me, sibling = (x, y, c), (x, y, 1 - c)
            chips = [(1 - x, y), (x, 1 - y), (1 - x, 1 - y)]

            def slot(i, px, py, pc):
                return out_refs[i].at[4 * px + 2 * py + pc]

            def copy(i, k, blk, to, src=None):
                return remote(i, k, slot(i, *blk) if src is None else src, slot(i, *blk), to)

            def mine():
                return [pltpu.make_async_copy(in_refs[i], slot(i, *me), local_sems.at[i]) for i in range(n)]

            def first():
                cps = []
                for i in range(n):
                    cps.append(copy(i, 0, me, sibling, src=in_refs[i]))
                    cps += [copy(i, 1 + j, me, (*chip, c), src=in_refs[i]) for j, chip in enumerate(chips)]
                return cps

            def passed():
                return [copy(i, 4 + j, (*chip, c), sibling) for j, chip in enumerate(chips) for i in range(n)]

            def start():
                for cp in mine() + first():
                    cp.start()

            def forward():
                for j, chip in enumerate(chips):
                    for i in range(n):
                        copy(i, 1 + j, (*chip, c), me).wait_recv()
                        copy(i, 4 + j, (*chip, c), sibling).start()

            def finish():
                for i in range(n):
                    copy(i, 0, sibling, me).wait_recv()
                    for j, chip in enumerate(chips):
                        copy(i, 4 + j, (*chip, 1 - c), me).wait_recv()
                for cp in first() + passed():
                    cp.wait_send()
                for cp in mine():
                    cp.wait()

            return start, forward, finish

        my_slot = 4 * x + 2 * y + c

        def mine():
            return [pltpu.make_async_copy(in_refs[i].at[my_slot], out_refs[i].at[my_slot], local_sems.at[i])
                    for i in range(n)]

        def copies():
            cps = []
            for k in range(1, N_DEV):
                px, py, pc = x ^ (k >> 2), y ^ ((k >> 1) & 1), c ^ (k & 1)
                cps += [remote(i, k - 1, in_refs[i].at[4 * px + 2 * py + pc], out_refs[i].at[my_slot], (px, py, pc))
                        for i in range(n)]
            return cps

        def start():
            for cp in mine() + copies():
                cp.start()

        def finish():
            for cp in copies():
                cp.wait_recv()
            for cp in copies():
                cp.wait_send()
            for cp in mine():
                cp.wait()

        return start, (lambda: None), finish


def comm_call(kind, arrays, *, name):
    job = CommJob(kind, arrays)
    n = job.n

    def body(*refs):
        start, forward, finish = job.phases(refs[:n], refs[n:2 * n], *refs[2 * n:])
        start()
        forward()
        finish()

    hbm = pl.BlockSpec(memory_space=pl.ANY)
    return pl.pallas_call(body, name=name, out_shape=job.out_shape(), in_specs=[hbm] * n, out_specs=[hbm] * n,
                          scratch_shapes=job.scratch())(*job.arrays)


def ride_along(job, n_in, n_out, step, n_steps, compute):
    if job is None:
        return compute
    n = job.n

    def body(*refs):
        ins, job_ins = refs[:n_in], refs[n_in:n_in + n]
        outs, job_outs = refs[n_in + n:n_in + n + n_out], refs[n_in + n + n_out:n_in + 2 * n + n_out]
        start, forward, finish = job.phases(job_ins, job_outs, *refs[n_in + 2 * n + n_out:])
        now = step()
        pl.when(now == 0)(start)
        pl.when(now == n_steps // 2)(forward)
        compute(*ins, *outs)
        pl.when(now == n_steps - 1)(finish)

    return body


def ride_call(job, compute, *, name, grid, in_specs, out_specs, out_shape, ins, sem):
    n_steps = grid[0] * grid[1]
    body = ride_along(job, len(ins), len(out_shape), lambda: pl.program_id(0) * grid[1] + pl.program_id(1), n_steps,
                      compute)
    if job is None:
        return pl.pallas_call(body, name=name, grid=grid, in_specs=in_specs, out_specs=out_specs,
                              out_shape=out_shape, compiler_params=_cparams(sem))(*ins), None
    hbm = pl.BlockSpec(memory_space=pl.ANY)
    res = pl.pallas_call(
        body, name=name, grid=grid, in_specs=list(in_specs) + [hbm] * job.n,
        out_specs=list(out_specs) + [hbm] * job.n, out_shape=list(out_shape) + job.out_shape(),
        scratch_shapes=job.scratch(), compiler_params=_cparams(("arbitrary", "arbitrary")),
    )(*ins, *job.arrays)
    return res[:len(out_shape)], res[len(out_shape):]


def to_heads(t, n_heads):
    s_len = t.shape[0]
    return t.reshape(s_len, n_heads, -1).transpose(1, 0, 2)


def from_heads(t):
    return t.transpose(1, 0, 2).reshape(t.shape[1], -1)


def gathered_to_full(t, axis):
    shp = t.shape[1:]
    return jnp.moveaxis(t, 0, axis).reshape(shp[:axis] + (N_DEV * shp[axis],) + shp[axis + 1:])


def full_to_owner_major(g, axis):
    shp = g.shape
    t = jnp.moveaxis(g.reshape(shp[:axis] + (N_DEV, shp[axis] // N_DEV) + shp[axis + 1:]), axis, 0)
    return t.reshape(N_DEV, -1, t.shape[-1])


def _small_rows(shape):
    n = 1
    for s in shape:
        n *= s
    return -(-n // LANES)


def pack_small(arrs, shapes):
    pieces = []
    for n in SMALL:
        flat = arrs[n].reshape(-1)
        flat = jnp.pad(flat, (0, _small_rows(shapes[n]) * LANES - flat.shape[0]))
        pieces.append(flat.reshape(-1, LANES))
    flat = jnp.concatenate(pieces, axis=0)
    return jnp.pad(flat, ((0, -flat.shape[0] % SMALL_ROW_MULTIPLE), (0, 0)))


def unpack_small(flat, shapes):
    out, r = {}, 0
    for n in SMALL:
        rows = _small_rows(shapes[n])
        size = 1
        for s in shapes[n]:
            size *= s
        out[n] = flat[r:r + rows].reshape(-1)[:size].reshape(shapes[n])
        r += rows
    return out


ROW_TM = 256
XATT_TM = 1024
HEAD_TM = 1024
SG_TM = 8 * SG_CHUNK
SB_TILES = (512, 512)
SM_TILE = 1024


def _norm_fwd(x, g, name):
    return prow(f_rms, [x], params=[g.reshape(1, -1)], outs=[(x.shape[1], BF16, False)], tm=ROW_TM, name=name)[0]


def _norm_bwd(x, g, dh, add, name, want_row=True):
    res = prow_vjp(f_rms, [x], params=[g.reshape(1, -1)], cts=[dh], row_grad=[want_row],
                   adds=[add] if want_row else None, tm=ROW_TM, name=name)
    return (res[0], res[1].reshape(-1)) if want_row else (None, res[0].reshape(-1))


def _out_proj(a, w, x, next_gain, alpha, name):
    if next_gain is None:
        return pmm(a, w, res=x, alpha=alpha, name=name), None
    return pmm(a, w, res=x, alpha=alpha, norm_out=next_gain.reshape(1, -1), name=name)


def _in_proj_bwd(d, w, x, gain, dy, name, **kw):
    dx, g_gain = pmm(d, w, tb=True, norm_bwd=(x, gain.reshape(1, -1), dy), name=name, **kw)
    return dx, g_gain.reshape(-1)


def ffn_fwd(x, h, p, tag, next_gain):
    gate, up, act = ffn_gate_up(h, p['w_gu'], name=f"{tag}_gu")
    out = _out_proj(act, p['w_down'], x, next_gain, 0.5, f"{tag}_down")
    return out, (x, h, gate, up, act)


def ffn_bwd(dy, p, saved, tag):
    x, h, gate, up, act = saved
    d_gate, d_up = ffn_gate_up_bwd(dy, p['w_down'], gate, up, alpha=0.5, name=f"{tag}_dact")
    g_down = pmm(act, dy, ta=True, out_dtype=GRAD_WIRE, alpha=0.5, name=f"{tag}_gdown")
    g_gu = jnp.concatenate([pmm(h, d_gate, ta=True, out_dtype=GRAD_WIRE, name=f"{tag}_ggate"),
                            pmm(h, d_up, ta=True, out_dtype=GRAD_WIRE, name=f"{tag}_gup")], axis=1)
    dh_gate = pmm(d_gate, p['w_gu'], tb=True, name=f"{tag}_dh_gate")
    dx, g_norm = _in_proj_bwd(d_up, p['w_gu'], x, p['norm'], dy, f"{tag}_dh", res=dh_gate,
                              k_off_b=D_FF // _pick(D_FF))
    return dx, {'norm': g_norm, 'w_gu': g_gu, 'w_down': g_down}


def even_mixer_fwd(x, h, p, next_gain, job=None, after_job=None):
    proj = pmm(h, p['w_in'], name="sbg_in")
    q, k, v = (to_heads(proj[:, i * SB_WIDTH:(i + 1) * SB_WIDTH], SB_HEADS).astype(BF16) for i in range(3))
    (o_sb, tot), landed = sb_fwd(q, k, v, tq=SB_TILES[0], tk=SB_TILES[1], name="sb_fwd", job=job)
    if job is not None:
        after_job(landed)
    z = proj[:, 3 * SB_WIDTH:]
    ln_g, ln_b = p['ln_gain'].reshape(1, -1), p['ln_bias'].reshape(1, -1)
    u, gn = prow(f_gate_prep, [z], params=[ln_g, ln_b], outs=[(SG_WIDTH, F32, False)] * 2, tm=ROW_TM,
                 name="sgu_prep")
    gn_g, u_g = to_heads(gn, SG_GROUPS), to_heads(u, SG_GROUPS)
    b3 = p['sgu_b'].reshape(SG_GROUPS, SG_CHUNK, 1)
    o_sg = prow(f_spatial_gate, [gn_g, u_g], gparams=[p['sgu_w'], b3], outs=[(SG_GROUP_DIM, F32, True)],
                tm=SG_TM, name="sgu_mix")[0]
    cat = jnp.concatenate([from_heads(o_sb), from_heads(o_sg)], axis=-1).astype(BF16)
    out = _out_proj(cat, p['w_out'], x, next_gain, 1.0, "sbg_out")
    return out, (x, h, q, k, v, tot, z, gn_g, u_g, b3, cat)


def even_mixer_bwd(dy, p, saved, job_of=None):
    x, h, q, k, v, tot, z, gn_g, u_g, b3, cat = saved
    d_cat = pmm(dy, p['w_out'], tb=True, name="sbg_dcat")
    g_out = pmm(cat, dy, ta=True, out_dtype=GRAD_WIRE, name="sbg_gout")
    d_osb = to_heads(d_cat[:, :SB_WIDTH], SB_HEADS)
    d_osg = to_heads(d_cat[:, SB_WIDTH:], SG_GROUPS)
    d_gn_g, d_u_g, g_w, g_b = prow_vjp(f_spatial_gate, [gn_g, u_g], gparams=[p['sgu_w'], b3], cts=[d_osg],
                                       row_grad=[True, True], tm=SG_TM, name="sgu_dmix")
    ln_g, ln_b = p['ln_gain'].reshape(1, -1), p['ln_bias'].reshape(1, -1)
    d_z, g_lng, g_lnb = prow_vjp(f_gate_prep, [z], params=[ln_g, ln_b], cts=[from_heads(d_u_g), from_heads(d_gn_g)],
                                 row_grad=[True], row_dtypes=[BF16], tm=ROW_TM, name="sgu_dprep")
    job = None if job_of is None else job_of({'w_out': g_out})
    (dq, dk, dv), landed = sb_bwd(q, k, v, tot, d_osb, tq=SB_TILES[0], tk=SB_TILES[1], name="sb_bwd", job=job)
    d_proj = jnp.concatenate([from_heads(dq).astype(BF16), from_heads(dk).astype(BF16), from_heads(dv).astype(BF16),
                              d_z], axis=-1)
    g_in = pmm(h, d_proj, ta=True, out_dtype=GRAD_WIRE, name="sbg_gin")
    dx, g_norm = _in_proj_bwd(d_proj, p['w_in'], x, p['norm'], dy, "sbg_dh")
    return dx, {'norm': g_norm, 'w_in': g_in, 'ln_gain': g_lng.reshape(-1), 'ln_bias': g_lnb.reshape(-1),
                'sgu_w': g_w, 'sgu_b': g_b.reshape(SG_GROUPS, SG_CHUNK), 'w_out': g_out}, landed


def mla_fwd(x, h, cos, sin, p, next_gain):
    proj = pmm(h, p['w_in'], name="mla_in")
    c_q, c_kv, k_r = proj[:, :MLA_Q_LORA], proj[:, MLA_Q_LORA:MLA_Q_LORA + MLA_KV_LORA], proj[:, MLA_Q_LORA + MLA_KV_LORA:]
    cqn = _norm_fwd(c_q, p['q_lora_gain'], "mla_qlora_norm")
    ckvn = _norm_fwd(c_kv, p['kv_lora_gain'], "mla_kvlora_norm")
    q_h = to_heads(pmm(cqn, p['w_uq'], name="mla_uq"), MLA_HEADS)
    kv_h = to_heads(pmm(ckvn, p['w_ukv'], name="mla_ukv"), MLA_HEADS)
    k_nope, v = kv_h[..., :MLA_NOPE], kv_h[..., MLA_NOPE:].astype(BF16)
    q_g, k_g = p['q_gain'].reshape(1, -1), p['k_gain'].reshape(1, -1)
    kp = prow(f_mla_k, [k_nope, k_r, cos, sin], params=[k_g], outs=[(MLA_QK, BF16, True)], tm=HEAD_TM,
              name="mla_kprep")[0]
    o, lse = sm_fwd(q_h, kp, v, tq=SM_TILE, tk=SM_TILE, name="mla_att_fwd", q_prep=(cos, sin, q_g))
    o_flat = from_heads(o).astype(BF16)
    out = _out_proj(o_flat, p['w_out'], x, next_gain, 1.0, "mla_out")
    return out, (x, h, c_q, c_kv, k_r, cqn, ckvn, q_h, k_nope, v, kp, o, lse, o_flat, q_g, k_g)


def mla_bwd(dy, cos, sin, p, saved, job_of=None):
    x, h, c_q, c_kv, k_r, cqn, ckvn, q_h, k_nope, v, kp, o, lse, o_flat, q_g, k_g = saved
    do = to_heads(pmm(dy, p['w_out'], tb=True, name="mla_do"), MLA_HEADS)
    g_out = pmm(o_flat, dy, ta=True, out_dtype=GRAD_WIRE, name="mla_gout")
    job = None if job_of is None else job_of({'w_out': g_out})
    (dq_h, dkp, dv, g_qg), landed = sm_bwd(q_h, kp, v, o, lse, do, tq=SM_TILE, tk=SM_TILE, name="mla_att_bwd",
                                           q_prep=(cos, sin, q_g), job=job)
    dk_nope, dk_r, g_kg = prow_vjp(f_mla_k, [k_nope, k_r, cos, sin], params=[k_g], cts=[dkp],
                                   row_grad=[True, True, False, False], tm=HEAD_TM, name="mla_dkprep")
    d_q = from_heads(dq_h)
    d_kv = from_heads(jnp.concatenate([dk_nope, dv], axis=-1)).astype(BF16)
    g_uq = pmm(cqn, d_q, ta=True, out_dtype=GRAD_WIRE, name="mla_guq")
    d_cqn = pmm(d_q, p['w_uq'], tb=True, name="mla_dcqn")
    g_ukv = pmm(ckvn, d_kv, ta=True, out_dtype=GRAD_WIRE, name="mla_gukv")
    d_ckvn = pmm(d_kv, p['w_ukv'], tb=True, name="mla_dckvn")
    d_cq, g_qlora = _norm_bwd(c_q, p['q_lora_gain'], d_cqn, None, "mla_dqlora_norm")
    d_ckv, g_kvlora = _norm_bwd(c_kv, p['kv_lora_gain'], d_ckvn, None, "mla_dkvlora_norm")
    d_proj = jnp.concatenate([d_cq, d_ckv, dk_r], axis=-1).astype(BF16)
    g_in = pmm(h, d_proj, ta=True, out_dtype=GRAD_WIRE, name="mla_gin")
    dx, g_norm = _in_proj_bwd(d_proj, p['w_in'], x, p['norm'], dy, "mla_dh")
    return dx, {'norm': g_norm, 'w_in': g_in, 'q_lora_gain': g_qlora, 'kv_lora_gain': g_kvlora, 'w_uq': g_uq,
                'w_ukv': g_ukv, 'q_gain': g_qg.reshape(-1), 'k_gain': g_kg.reshape(-1), 'w_out': g_out}, landed


def xattn_fwd(x, hq, mem, p, tag, next_gain):
    hm = _norm_fwd(mem, p['mem_norm'], f"{tag}_mem_norm")
    q_h = ColGroups(pmm(hq, p['wq'], name=f"{tag}_q"), MEM_HEAD_DIM)
    kv = pmm(hm, p['wkv'], name=f"{tag}_kv").reshape(mem.shape[0], MEM_HEADS, 2 * MEM_HEAD_DIM).transpose(1, 0, 2)
    k_h, v_h = kv[..., :MEM_HEAD_DIM], kv[..., MEM_HEAD_DIM:]
    q_g, k_g = p['q_gain'].reshape(1, -1), p['k_gain'].reshape(1, -1)
    o_flat = prow(f_xattn, [q_h], gparams=[k_h, v_h], params=[q_g, k_g], outs=[(MEM_HEAD_DIM, BF16, 'cols')],
                  tm=XATT_TM, name=f"{tag}_att")[0]
    out = _out_proj(o_flat, p['wo'], x, next_gain, 1.0, f"{tag}_out")
    return out, (x, mem, hq, hm, q_h, k_h, v_h, q_g, k_g, o_flat)


def xattn_bwd(dy, p, saved, tag):
    x, mem, hq, hm, q_h, k_h, v_h, q_g, k_g, o_flat = saved
    d_o = ColGroups(pmm(dy, p['wo'], tb=True, name=f"{tag}_do"), MEM_HEAD_DIM)
    g_wo = pmm(o_flat, dy, ta=True, out_dtype=GRAD_WIRE, name=f"{tag}_gwo")
    d_q, dk_h, dv_h, g_qg, g_kg = prow_vjp(f_xattn, [q_h], gparams=[k_h, v_h], params=[q_g, k_g], cts=[d_o],
                                           row_grad=[True], row_dtypes=[BF16], tm=XATT_TM, name=f"{tag}_datt")
    d_kv = jnp.concatenate([dk_h, dv_h], axis=-1).transpose(1, 0, 2).reshape(mem.shape[0], -1).astype(BF16)
    g_wq = pmm(hq, d_q, ta=True, out_dtype=GRAD_WIRE, name=f"{tag}_gwq")
    dx, g_norm = _in_proj_bwd(d_q, p['wq'], x, p['norm'], dy, f"{tag}_dhq")
    g_wkv = pmm(hm, d_kv, ta=True, out_dtype=GRAD_WIRE, name=f"{tag}_gwkv")
    dhm = pmm(d_kv, p['wkv'], tb=True, name=f"{tag}_dhm")
    _, g_mem_norm = _norm_bwd(mem, p['mem_norm'], dhm, None, f"{tag}_dmem_norm", want_row=False)
    return dx, {'norm': g_norm, 'mem_norm': g_mem_norm, 'wq': g_wq, 'wkv': g_wkv, 'q_gain': g_qg.reshape(-1),
                'k_gain': g_kg.reshape(-1), 'wo': g_wo}


def rope_tables(positions):
    half = MLA_ROPE // 2
    inv_freq = ROPE_THETA ** (-jnp.arange(half, dtype=F32) / half)
    ang = positions.astype(F32)[:, None] * inv_freq
    return jnp.cos(ang), jnp.sin(ang)


EARLY_UNITS = [('ffn_pre_w_gu', 0), ('ffn_pre_w_down', 0), ('sbg_w_in', 0)]


def local_step(x, mem, positions, target, w, shards):
    cos, sin = rope_tables(positions)
    full = {}

    def absorb(units, gathered):
        for (n, layer), t in zip(units, gathered):
            full[(n, layer)] = gathered_to_full(t, BIG[n] - 1)

    late_units = [u for u in shards if u not in EARLY_UNITS]
    absorb(EARLY_UNITS, comm_call('gather', [shards[u] for u in EARLY_UNITS], name="gather_weights_early"))

    def ffn_params(kind, layer):
        return {'norm': w[f'ffn_{kind}_norm'][layer], 'w_gu': full[(f'ffn_{kind}_w_gu', layer)],
                'w_down': full[(f'ffn_{kind}_w_down', layer)]}

    def xattn_params(layer):
        return {'norm': w['xmem_norm'][layer], 'mem_norm': w['xmem_mem_norm'][layer], 'wq': full[('xmem_wq', layer)],
                'wkv': full[('xmem_wkv', layer)], 'q_gain': w['xmem_q_gain'][layer], 'k_gain': w['xmem_k_gain'][layer],
                'wo': full[('xmem_wo', layer)]}

    even_p = {'norm': w['mix_norm'][0], 'w_in': full[('sbg_w_in', 0)], 'ln_gain': w['sgu_ln_gain'][0],
              'ln_bias': w['sgu_ln_bias'][0], 'sgu_w': w['sgu_w'][0], 'sgu_b': w['sgu_b'][0]}

    def late_weights_landed(gathered):
        absorb(late_units, gathered)
        even_p['w_out'] = full[('sbg_w_out', 0)]

    def mla_params():
        return {'norm': w['mix_norm'][1], 'w_in': full[('mla_w_in', 0)], 'q_lora_gain': w['mla_q_lora_gain'][0],
                'kv_lora_gain': w['mla_kv_lora_gain'][0], 'w_uq': full[('mla_w_uq', 0)],
                'w_ukv': full[('mla_w_ukv', 0)], 'q_gain': w['mla_q_gain'][0], 'k_gain': w['mla_k_gain'][0],
                'w_out': full[('mla_w_out', 0)]}

    saved = []
    h = _norm_fwd(x, w['ffn_pre_norm'][0], "ffn_pre0_norm")
    for layer in range(DEPTH):
        (x, h), s_pre = ffn_fwd(x, h, ffn_params('pre', layer), f"ffn_pre{layer}", w['mix_norm'][layer])
        if layer % 2 == 0:
            (x, h), s_mix = even_mixer_fwd(x, h, even_p, w['xmem_norm'][layer],
                                           job=CommJob('gather', [shards[u] for u in late_units]),
                                           after_job=late_weights_landed)
        else:
            (x, h), s_mix = mla_fwd(x, h, cos, sin, mla_params(), w['xmem_norm'][layer])
        (x, h), s_x = xattn_fwd(x, h, mem, xattn_params(layer), f"xmem{layer}", w['ffn_post_norm'][layer])
        following = w['ffn_pre_norm'][layer + 1] if layer + 1 < DEPTH else None
        (x, h), s_post = ffn_fwd(x, h, ffn_params('post', layer), f"ffn_post{layer}", following)
        saved.append((s_pre, s_mix, s_x, s_post))

    dx, loss = loss_head(x, target, tm=ROW_TM, name="loss_head")

    ready, riding, landed = {}, [], {}

    def offer(name, layer, g):
        ready[(name, layer)] = full_to_owner_major(g, BIG[name] - 1)

    def ride(name):
        def job_of(own):
            offer(name, 0, own['w_out'])
            riding[:] = list(ready)
            return CommJob('exchange', [ready.pop(u) for u in riding])
        return job_of

    per_layer = []
    for layer in reversed(range(DEPTH)):
        s_pre, s_mix, s_x, s_post = saved[layer]
        dx, g_post = ffn_bwd(dx, ffn_params('post', layer), s_post, f"ffn_post{layer}")
        offer('ffn_post_w_gu', layer, g_post['w_gu'])
        offer('ffn_post_w_down', layer, g_post['w_down'])
        dx, g_x = xattn_bwd(dx, xattn_params(layer), s_x, f"xmem{layer}")
        for n in ('wq', 'wkv', 'wo'):
            offer('xmem_' + n, layer, g_x[n])
        if layer % 2 == 0:
            dx, g_mix, arrived = even_mixer_bwd(dx, even_p, s_mix, job_of=ride('sbg_w_out'))
            landed.update(zip(riding, arrived))
            offer('sbg_w_in', 0, g_mix['w_in'])
        else:
            dx, g_mix, arrived = mla_bwd(dx, cos, sin, mla_params(), s_mix, job_of=ride('mla_w_out'))
            landed.update(zip(riding, arrived))
            for n in ('w_in', 'w_uq', 'w_ukv'):
                offer('mla_' + n, 0, g_mix[n])
        dx, g_pre = ffn_bwd(dx, ffn_params('pre', layer), s_pre, f"ffn_pre{layer}")
        offer('ffn_pre_w_gu', layer, g_pre['w_gu'])
        offer('ffn_pre_w_down', layer, g_pre['w_down'])
        per_layer.append((layer, g_pre, g_mix, g_x, g_post))
    per_layer.sort(key=lambda t: t[0])
    last_units = list(ready)
    landed.update(zip(last_units, comm_call('exchange', [ready.pop(u) for u in last_units], name="exchange_grads_last")))

    def stack(pick):
        return jnp.stack([pick(t) for t in per_layer])

    g_even, g_mla = per_layer[0][2], per_layer[1][2]
    small_grads = {
        'ffn_pre_norm': stack(lambda t: t[1]['norm']), 'mix_norm': stack(lambda t: t[2]['norm']),
        'sgu_ln_gain': g_even['ln_gain'][None], 'sgu_ln_bias': g_even['ln_bias'][None],
        'sgu_w': g_even['sgu_w'][None], 'sgu_b': g_even['sgu_b'][None],
        'mla_q_lora_gain': g_mla['q_lora_gain'][None], 'mla_kv_lora_gain': g_mla['kv_lora_gain'][None],
        'mla_q_gain': g_mla['q_gain'][None], 'mla_k_gain': g_mla['k_gain'][None],
        'xmem_norm': stack(lambda t: t[3]['norm']), 'xmem_mem_norm': stack(lambda t: t[3]['mem_norm']),
        'xmem_q_gain': stack(lambda t: t[3]['q_gain']), 'xmem_k_gain': stack(lambda t: t[3]['k_gain']),
        'ffn_post_norm': stack(lambda t: t[4]['norm']),
    }
    return loss, dx, small_grads, landed


def _device_slot():
    x, y, c = _me()
    return 4 * x + 2 * y + c


def kernel(x, mem, positions, ffn_pre_norm, ffn_pre_w_gu, ffn_pre_w_down, mix_norm, sbg_w_in, sgu_ln_gain, sgu_ln_bias, sgu_w, sgu_b, sbg_w_out, mla_w_in, mla_q_lora_gain, mla_kv_lora_gain, mla_w_uq, mla_w_ukv, mla_q_gain, mla_k_gain, mla_w_out, xmem_norm, xmem_mem_norm, xmem_wq, xmem_wkv, xmem_q_gain, xmem_k_gain, xmem_wo, ffn_post_norm, ffn_post_w_gu, ffn_post_w_down, loss_target, m_ffn_pre_norm, m_ffn_pre_w_gu, m_ffn_pre_w_down, m_mix_norm, m_sbg_w_in, m_sgu_ln_gain, m_sgu_ln_bias, m_sgu_w, m_sgu_b, m_sbg_w_out, m_mla_w_in, m_mla_q_lora_gain, m_mla_kv_lora_gain, m_mla_w_uq, m_mla_w_ukv, m_mla_q_gain, m_mla_k_gain, m_mla_w_out, m_xmem_norm, m_xmem_mem_norm, m_xmem_wq, m_xmem_wkv, m_xmem_q_gain, m_xmem_k_gain, m_xmem_wo, m_ffn_post_norm, m_ffn_post_w_gu, m_ffn_post_w_down, v_ffn_pre_norm, v_ffn_pre_w_gu, v_ffn_pre_w_down, v_mix_norm, v_sbg_w_in, v_sgu_ln_gain, v_sgu_ln_bias, v_sgu_w, v_sgu_b, v_sbg_w_out, v_mla_w_in, v_mla_q_lora_gain, v_mla_kv_lora_gain, v_mla_w_uq, v_mla_w_ukv, v_mla_q_gain, v_mla_k_gain, v_mla_w_out, v_xmem_norm, v_xmem_mem_norm, v_xmem_wq, v_xmem_wkv, v_xmem_q_gain, v_xmem_k_gain, v_xmem_wo, v_ffn_post_norm, v_ffn_post_w_gu, v_ffn_post_w_down):
    args = locals()
    w_in = {n: args[n] for n in WEIGHTS}
    m_in = {n: args["m_" + n] for n in WEIGHTS}
    v_in = {n: args["v_" + n] for n in WEIGHTS}
    slot = _device_slot()

    tiny = jnp.zeros((8, LANES), F32)
    for i, src in enumerate((w_in, m_in, v_in)):
        tiny = tiny.at[i, :64].set(src['mla_q_lora_gain'][0]).at[i + 3, :32].set(src['mla_kv_lora_gain'][0])
    tiny_all = comm_call('gather', [tiny], name="gather_lora_gains")[0]
    full_small = []
    for i, src in enumerate((w_in, m_in, v_in)):
        d = {n: src[n] for n in SMALL}
        d['mla_q_lora_gain'] = tiny_all[:, i, :64].reshape(1, MLA_Q_LORA)
        d['mla_kv_lora_gain'] = tiny_all[:, i + 3, :32].reshape(1, MLA_KV_LORA)
        full_small.append(d)
    w_small, m_small, v_small = full_small
    small_shapes = {n: w_small[n].shape for n in SMALL}

    shards = {(n, layer): w_in[n][layer].astype(BF16) for n in BIG for layer in range(w_in[n].shape[0])}
    loss, dx, grads, landed = local_step(x[0], mem[0], positions[0], loss_target[0], w_small, shards)
    loss = lax.psum(loss, ("x", "y", "c"))
    big_out = {n: adamw([landed[(n, layer)] for layer in range(w_in[n].shape[0])], w_in[n], m_in[n], v_in[n],
                        name=f"adamw_{n}") for n in BIG}

    small_parts = comm_call('gather', [pack_small(grads, small_shapes)], name="gather_small_grads")
    small_out = adamw(small_parts, pack_small(w_small, small_shapes)[None], pack_small(m_small, small_shapes)[None],
                      pack_small(v_small, small_shapes)[None], name="adamw_small")
    small_out = [unpack_small(t[0], small_shapes) for t in small_out]
    for d in small_out:
        for n, width in zip(GAIN_SHARDED, (64, 32)):
            d[n] = lax.dynamic_slice(d[n], (0, slot * width), (1, width))

    outs = [loss, dx[None]]
    for kind, small_d in enumerate(small_out):
        outs += [big_out[n][kind] if n in BIG else small_d[n] for n in WEIGHTS]
    return tuple(outs)
```

```python
import functools

import jax
import jax.numpy as jnp
from jax import lax
from jax.experimental import pallas as pl
from jax.experimental.pallas import tpu as pltpu

F32 = jnp.float32
BF16 = jnp.bfloat16
MESH = pl.DeviceIdType.MESH
N_DEV = 8

VMEM_LIMIT_BYTES = 56 * 1024 * 1024
LANES = 128

D_MODEL = 1024
DEPTH = 2
D_FF = 2816
EPS = 1e-6
SB_HEADS, SB_HEAD_DIM = 8, 64
SB_WIDTH = SB_HEADS * SB_HEAD_DIM
SG_GROUPS, SG_GROUP_DIM, SG_CHUNK = 8, 64, 128
SG_WIDTH = SG_GROUPS * SG_GROUP_DIM
MLA_HEADS, MLA_NOPE, MLA_ROPE, MLA_V = 16, 64, 32, 64
MLA_QK = MLA_NOPE + MLA_ROPE
MLA_Q_LORA, MLA_KV_LORA = 512, 256
ROPE_THETA = 10000.0
MEM_HEADS = 4
MEM_HEAD_DIM = D_MODEL // MEM_HEADS

ADAM_LR, ADAM_B1, ADAM_B2, ADAM_EPS, ADAM_WD, ADAM_STEP = 0.001, 0.9, 0.999, 1e-08, 0.01, 10

WEIGHTS = ['ffn_pre_norm', 'ffn_pre_w_gu', 'ffn_pre_w_down', 'mix_norm', 'sbg_w_in', 'sgu_ln_gain', 'sgu_ln_bias',
           'sgu_w', 'sgu_b', 'sbg_w_out', 'mla_w_in', 'mla_q_lora_gain', 'mla_kv_lora_gain', 'mla_w_uq', 'mla_w_ukv',
           'mla_q_gain', 'mla_k_gain', 'mla_w_out', 'xmem_norm', 'xmem_mem_norm', 'xmem_wq', 'xmem_wkv',
           'xmem_q_gain', 'xmem_k_gain', 'xmem_wo', 'ffn_post_norm', 'ffn_post_w_gu', 'ffn_post_w_down']
BIG = {'ffn_pre_w_gu': 2, 'ffn_pre_w_down': 1, 'sbg_w_in': 2, 'sbg_w_out': 1, 'mla_w_in': 1, 'mla_w_uq': 2,
       'mla_w_ukv': 2, 'mla_w_out': 1, 'xmem_wq': 1, 'xmem_wkv': 2, 'xmem_wo': 1, 'ffn_post_w_gu': 2,
       'ffn_post_w_down': 1}
GAIN_SHARDED = ('mla_q_lora_gain', 'mla_kv_lora_gain')
SMALL = [n for n in WEIGHTS if n not in BIG]
GRAD_WIRE = BF16
FFN_SAVE = BF16
SMALL_ROW_MULTIPLE = 16


def _cparams(sem=None):
    return pltpu.CompilerParams(dimension_semantics=sem, vmem_limit_bytes=VMEM_LIMIT_BYTES)


MM_TILE_CAP = 1408


def _pick(dim, cap=MM_TILE_CAP):
    if dim % LANES:
        return dim
    return max(t for t in range(LANES, min(dim, cap) + 1, LANES) if dim % t == 0)


def _rms(x, g):
    return x * lax.rsqrt(jnp.mean(x * x, axis=-1, keepdims=True) + EPS) * g


def pmm(a, b, *, ta=False, tb=False, out_dtype=F32, res=None, alpha=1.0, k_off_b=0, norm_out=None, norm_bwd=None,
        job=None, name):
    kdim, m = (a.shape if ta else a.shape[::-1])
    n = b.shape[0] if tb else b.shape[1]
    tm, tn, tk = _pick(m), _pick(n), _pick(kdim)
    whole_rows = norm_out is not None or norm_bwd is not None
    if whole_rows:
        assert tn == n
    if norm_bwd is not None:
        tm = min(tm, 512)
    nk = kdim // tk
    dims = (((0 if ta else 1,), (1 if tb else 0,)), ((), ()))
    n_extra = (res is not None) + (norm_out is not None) + (0 if norm_bwd is None else 2 + (norm_bwd[2] is not None))

    def body(*refs):
        a_ref, b_ref = refs[:2]
        extra = list(refs[2:2 + n_extra])
        outs, acc_ref = refs[2 + n_extra:-1], refs[-1]
        i, k = pl.program_id(0), pl.program_id(2)

        @pl.when(k == 0)
        def _():
            acc_ref[...] = jnp.zeros_like(acc_ref)

        acc_ref[...] += lax.dot_general(a_ref[...].astype(BF16), b_ref[...].astype(BF16), dims,
                                        preferred_element_type=F32)

        @pl.when(k == nk - 1)
        def _():
            r = acc_ref[...]
            if alpha != 1.0:
                r = r * alpha
            if res is not None:
                r = extra.pop(0)[...] + r
            if norm_bwd is None:
                outs[0][...] = r.astype(out_dtype)
            if norm_out is not None:
                outs[1][...] = _rms(r, extra.pop(0)[...]).astype(BF16)
            if norm_bwd is not None:
                x_ref, g_ref = extra.pop(0), extra.pop(0)
                _, pull = jax.vjp(_rms, x_ref[...], g_ref[...])
                dx, dg = pull(r)
                if norm_bwd[2] is not None:
                    dx = dx + extra.pop(0)[...]
                outs[0][...] = dx

                @pl.when(i == 0)
                def _():
                    outs[1][...] = dg

                @pl.when(i != 0)
                def _():
                    outs[1][...] += dg

    gi, gj = m // tm, n // tn
    a_bytes, b_bytes = a.size * a.dtype.itemsize, (n * kdim) * b.dtype.itemsize
    j_outer = not whole_rows and nk == 1 and gj * a_bytes + b_bytes < a_bytes + gi * b_bytes
    grid = (gj, gi, nk) if j_outer else (gi, gj, nk)

    def spec(block, index):
        return pl.BlockSpec(block, (lambda j, i, k: index(i, j, k)) if j_outer else index)

    a_spec = spec((tk, tm), lambda i, j, k: (k, i)) if ta else spec((tm, tk), lambda i, j, k: (i, k))
    b_spec = (spec((tn, tk), lambda i, j, k: (j, k + k_off_b)) if tb
              else spec((tk, tn), lambda i, j, k: (k + k_off_b, j)))
    o_spec = spec((tm, tn), lambda i, j, k: (i, j))
    g_spec = spec((1, tn), lambda i, j, k: (0, 0))
    ins, in_specs = [a, b], [a_spec, b_spec]
    if res is not None:
        ins.append(res)
        in_specs.append(o_spec)
    out_shape, out_specs = [jax.ShapeDtypeStruct((m, n), out_dtype)], [o_spec]
    if norm_out is not None:
        ins.append(norm_out)
        in_specs.append(g_spec)
        out_shape.append(jax.ShapeDtypeStruct((m, n), BF16))
        out_specs.append(o_spec)
    if norm_bwd is not None:
        ins += [t for t in norm_bwd if t is not None]
        in_specs += [o_spec, g_spec] + ([o_spec] if norm_bwd[2] is not None else [])
        out_shape = [jax.ShapeDtypeStruct((m, n), F32), jax.ShapeDtypeStruct((1, n), F32)]
        out_specs = [o_spec, g_spec]
    result, landed = ride_call(
        job, body, name=name, grid=grid, in_specs=in_specs, out_specs=out_specs, out_shape=out_shape, ins=ins,
        scratch=[pltpu.VMEM((tm, tn), F32)],
        sem=("arbitrary" if norm_bwd is not None else "parallel", "parallel", "arbitrary"))
    result = result if whole_rows else result[0]
    return result if job is None else (result, landed)


def ffn_gate_up(h, w_gu, *, name, job=None):
    m, kdim = h.shape
    n = w_gu.shape[1] // 2
    tm, tn = min(_pick(m), 512), _pick(n)
    up_off = n // tn

    def body(a_ref, bg_ref, bu_ref, gate_ref, up_ref, act_ref):
        av = a_ref[...].astype(BF16)
        gate = _dg(av, bg_ref[...].astype(BF16), 1, 0)
        up = _dg(av, bu_ref[...].astype(BF16), 1, 0)
        gate_ref[...] = gate.astype(gate_ref.dtype)
        up_ref[...] = up.astype(up_ref.dtype)
        act_ref[...] = (jax.nn.silu(gate) * up).astype(BF16)

    o_spec = pl.BlockSpec((tm, tn), lambda j, i: (i, j))
    return ride_call(
        job, body, name=name, grid=(n // tn, m // tm),
        in_specs=[pl.BlockSpec((tm, kdim), lambda j, i: (i, 0)), pl.BlockSpec((kdim, tn), lambda j, i: (0, j)),
                  pl.BlockSpec((kdim, tn), lambda j, i: (0, j + up_off))],
        out_specs=[o_spec] * 3,
        out_shape=[jax.ShapeDtypeStruct((m, n), FFN_SAVE), jax.ShapeDtypeStruct((m, n), FFN_SAVE),
                   jax.ShapeDtypeStruct((m, n), BF16)],
        ins=[h, w_gu, w_gu], sem=("parallel", "parallel"))


def ffn_gate_up_bwd(dy, w_down, gate, up, *, alpha, name, job=None):
    m, kdim = dy.shape
    n = w_down.shape[0]
    tm, tn = min(_pick(m), 512), _pick(n)

    def body(a_ref, b_ref, gate_ref, up_ref, dgate_ref, dup_ref):
        d_act = _dg(a_ref[...].astype(BF16), b_ref[...].astype(BF16), 1, 1) * alpha
        _, pull = jax.vjp(lambda g, u: jax.nn.silu(g) * u, gate_ref[...].astype(F32), up_ref[...].astype(F32))
        d_gate, d_up = pull(d_act)
        dgate_ref[...] = d_gate.astype(BF16)
        dup_ref[...] = d_up.astype(BF16)

    o_spec = pl.BlockSpec((tm, tn), lambda j, i: (i, j))
    return ride_call(
        job, body, name=name, grid=(n // tn, m // tm),
        in_specs=[pl.BlockSpec((tm, kdim), lambda j, i: (i, 0)), pl.BlockSpec((tn, kdim), lambda j, i: (j, 0)),
                  o_spec, o_spec],
        out_specs=[o_spec] * 2, out_shape=[jax.ShapeDtypeStruct((m, n), BF16)] * 2,
        ins=[dy, w_down, gate, up], sem=("parallel", "parallel"))


def _dg(a, b, ca, cb):
    return lax.dot_general(a, b, (((ca,), (cb,)), ((), ())), preferred_element_type=F32)


@jax.custom_vjp
def bdot(a, b):
    return _dg(a.astype(BF16), b.astype(BF16), 1, 0)


def _bdot_fwd(a, b):
    ab, bb = a.astype(BF16), b.astype(BF16)
    return _dg(ab, bb, 1, 0), (ab, bb)


def _bdot_bwd(saved, g):
    ab, bb = saved
    gb = g.astype(BF16)
    return _dg(gb, bb, 1, 1), _dg(ab, gb, 0, 0)


bdot.defvjp(_bdot_fwd, _bdot_bwd)


@jax.custom_vjp
def bdot_nt(a, b):
    return _dg(a.astype(BF16), b.astype(BF16), 1, 1)


def _bdot_nt_fwd(a, b):
    ab, bb = a.astype(BF16), b.astype(BF16)
    return _dg(ab, bb, 1, 1), (ab, bb)


def _bdot_nt_bwd(saved, g):
    ab, bb = saved
    gb = g.astype(BF16)
    return _dg(gb, bb, 1, 0), _dg(gb, ab, 0, 0)


bdot_nt.defvjp(_bdot_nt_fwd, _bdot_nt_bwd)


class ColGroups:
    def __init__(self, arr, width):
        self.arr, self.width = arr, width
        self.shape, self.dtype, self.ndim = arr.shape, arr.dtype, 3


def _plain(a):
    return a.arr if isinstance(a, ColGroups) else a


def _row_spec(arr, tm):
    if isinstance(arr, ColGroups):
        return pl.BlockSpec((tm, arr.width), lambda r, g: (r, g))
    if arr.ndim == 3:
        return pl.BlockSpec((None, tm, arr.shape[2]), lambda r, g: (g, r, 0))
    return pl.BlockSpec((tm, arr.shape[1]), lambda r, g: (r, 0))


def _gparam_spec(arr):
    return pl.BlockSpec((None,) + arr.shape[1:], lambda r, g: (g, 0, 0))


def _whole_spec(arr):
    nd = arr.ndim
    return pl.BlockSpec(arr.shape, lambda r, g: (0,) * nd)


def _groups(rows, gparams):
    gs = {a.shape[1] // a.width if isinstance(a, ColGroups) else a.shape[0] for a in rows if a.ndim == 3}
    gs |= {a.shape[0] for a in gparams}
    assert len(gs) <= 1
    return gs.pop() if gs else 1


def prow(fn, rows, gparams=(), params=(), *, outs, tm, name):
    rows, gparams, params = list(rows), list(gparams), list(params)
    n_groups = _groups(rows, gparams)
    n_rows = rows[0].shape[-2]
    n_in = len(rows) + len(gparams) + len(params)

    def body(*refs):
        vals = [r[...] for r in refs[:n_in]]
        res = fn(*vals)
        for o_ref, r in zip(refs[n_in:], res, strict=True):
            o_ref[...] = r.astype(o_ref.dtype)

    out_shape, out_specs = [], []
    for width, dtype, grouped in outs:
        if grouped == 'cols':
            out_shape.append(jax.ShapeDtypeStruct((n_rows, n_groups * width), dtype))
            out_specs.append(_row_spec(ColGroups(out_shape[-1], width), tm))
            continue
        shp = (n_groups, n_rows, width) if grouped else (n_rows, width)
        out_shape.append(jax.ShapeDtypeStruct(shp, dtype))
        out_specs.append(_row_spec(out_shape[-1], tm))
    return pl.pallas_call(
        body, name=name, grid=(n_rows // tm, n_groups),
        in_specs=[_row_spec(a, tm) for a in rows] + [_gparam_spec(a) for a in gparams] + [_whole_spec(a) for a in params],
        out_specs=out_specs, out_shape=out_shape,
        compiler_params=_cparams(("parallel", "arbitrary")),
    )(*[_plain(a) for a in rows], *gparams, *params)


def prow_vjp(fn, rows, gparams=(), params=(), *, cts, row_grad, adds=None, row_dtypes=None, gparam_grad=None,
             param_grad=None, tm, name):
    rows, gparams, params, cts = list(rows), list(gparams), list(params), list(cts)
    gparam_grad = list(gparam_grad) if gparam_grad is not None else [True] * len(gparams)
    param_grad = list(param_grad) if param_grad is not None else [True] * len(params)
    n_groups = _groups(rows + cts, gparams)
    n_rows = rows[0].shape[-2]
    want_rows = [i for i, w in enumerate(row_grad) if w]
    adds = list(adds) if adds is not None else [None] * len(want_rows)
    row_dtypes = list(row_dtypes) if row_dtypes is not None else [F32] * len(want_rows)
    add_arrays = [a for a in adds if a is not None]
    n_r, n_g, n_p, n_c, n_a = len(rows), len(gparams), len(params), len(cts), len(add_arrays)
    mask = list(row_grad) + gparam_grad + param_grad

    def body(*refs):
        r_id, g_id = pl.program_id(0), pl.program_id(1)
        n_in = n_r + n_g + n_p
        vals = [r[...] for r in refs[:n_in]]
        ct_vals = tuple(r[...].astype(F32) for r in refs[n_in:n_in + n_c])
        add_refs = list(refs[n_in + n_c:n_in + n_c + n_a])
        out_refs = list(refs[n_in + n_c + n_a:])
        diff_idx = [i for i, w in enumerate(mask) if w]

        def wrapped(*diff):
            full = list(vals)
            for i, d in zip(diff_idx, diff):
                full[i] = d
            return tuple(fn(*full))

        _, pull = jax.vjp(wrapped, *[vals[i].astype(F32) for i in diff_idx])
        grads = dict(zip(diff_idx, pull(ct_vals)))
        k = 0
        for j, i in enumerate(want_rows):
            o_ref = out_refs[k]
            k += 1
            gval = grads[i]
            if adds[j] is not None:
                gval = gval + add_refs.pop(0)[...].astype(F32)
            if rows[i].ndim == 2 and n_groups > 1:
                @pl.when(g_id == 0)
                def _(o_ref=o_ref, gval=gval):
                    o_ref[...] = gval.astype(o_ref.dtype)

                @pl.when(g_id != 0)
                def _(o_ref=o_ref, gval=gval):
                    o_ref[...] += gval.astype(o_ref.dtype)
            else:
                o_ref[...] = gval.astype(o_ref.dtype)
        for i in range(n_g):
            if not gparam_grad[i]:
                continue
            o_ref = out_refs[k]
            k += 1
            gval = grads[n_r + i]

            @pl.when(r_id == 0)
            def _(o_ref=o_ref, gval=gval):
                o_ref[g_id] = gval

            @pl.when(r_id != 0)
            def _(o_ref=o_ref, gval=gval):
                o_ref[g_id] += gval
        for i in range(n_p):
            if not param_grad[i]:
                continue
            o_ref = out_refs[k]
            k += 1
            gval = grads[n_r + n_g + i]
            first = jnp.logical_and(r_id == 0, g_id == 0)

            @pl.when(first)
            def _(o_ref=o_ref, gval=gval):
                o_ref[...] = gval

            @pl.when(jnp.logical_not(first))
            def _(o_ref=o_ref, gval=gval):
                o_ref[...] += gval

    out_shape, out_specs = [], []
    for j, i in enumerate(want_rows):
        out_shape.append(jax.ShapeDtypeStruct(rows[i].shape, row_dtypes[j]))
        out_specs.append(_row_spec(rows[i], tm))
    for i in range(n_g):
        if gparam_grad[i]:
            out_shape.append(jax.ShapeDtypeStruct(gparams[i].shape, F32))
            out_specs.append(_whole_spec(gparams[i]))
    for i in range(n_p):
        if param_grad[i]:
            out_shape.append(jax.ShapeDtypeStruct(params[i].shape, F32))
            out_specs.append(_whole_spec(params[i]))
    return pl.pallas_call(
        body, name=name, grid=(n_rows // tm, n_groups),
        in_specs=([_row_spec(a, tm) for a in rows] + [_gparam_spec(a) for a in gparams]
                  + [_whole_spec(a) for a in params] + [_row_spec(a, tm) for a in cts]
                  + [_row_spec(a, tm) for a in add_arrays]),
        out_specs=out_specs, out_shape=out_shape,
        compiler_params=_cparams(("arbitrary", "arbitrary")),
    )(*[_plain(a) for a in rows], *gparams, *params, *[_plain(a) for a in cts], *add_arrays)


def f_rms(x, g):
    return (_rms(x.astype(F32), g),)


def f_gate_prep(z, ln_g, ln_b):
    act = jax.nn.gelu(z)
    u, gg = act[:, :SG_WIDTH], act[:, SG_WIDTH:]
    mu = jnp.mean(gg, axis=-1, keepdims=True)
    var = jnp.mean(jnp.square(gg - mu), axis=-1, keepdims=True)
    return u, (gg - mu) * lax.rsqrt(var + EPS) * ln_g + ln_b


def f_spatial_gate(gn, u, w, b):
    t = lax.broadcasted_iota(jnp.int32, w.shape, 0)
    s = lax.broadcasted_iota(jnp.int32, w.shape, 1)
    w_causal = jnp.where(s <= t, w, 0.0)
    mixed = [bdot(w_causal, gn[i:i + SG_CHUNK]) + b for i in range(0, gn.shape[0], SG_CHUNK)]
    return (u * (mixed[0] if len(mixed) == 1 else jnp.concatenate(mixed, axis=0)),)


def _rope_tail(t, cos, sin):
    half = MLA_ROPE // 2
    t1, t2 = t[:, MLA_NOPE:MLA_NOPE + half], t[:, MLA_NOPE + half:]
    return jnp.concatenate([t[:, :MLA_NOPE], t1 * cos - t2 * sin, t1 * sin + t2 * cos], axis=-1)


def f_mla_q(q, cos, sin, g):
    return (_rope_tail(f_rms(q, g)[0], cos, sin),)


def f_mla_k(k_nope, k_r, cos, sin, g):
    return (_rope_tail(f_rms(jnp.concatenate([k_nope, k_r], axis=-1), g)[0], cos, sin),)


def f_xattn(q, k, v, q_g, k_g):
    qn, kn = f_rms(q, q_g)[0], f_rms(k, k_g)[0]
    sc = bdot_nt(qn, kn) * (MEM_HEAD_DIM ** -0.5)
    return (bdot(jax.nn.softmax(sc, axis=-1), v),)


def _split_dot(x, tri, pieces=2):
    hi = x.astype(BF16)
    if pieces == 1:
        return _dg(hi, tri, 1, 0)
    lo = (x - hi.astype(F32)).astype(BF16)
    return _dg(hi, tri, 1, 0) + _dg(lo, tri, 1, 0)


def _tri(tk, cmp):
    j = lax.broadcasted_iota(jnp.int32, (tk, tk), 0)
    s = lax.broadcasted_iota(jnp.int32, (tk, tk), 1)
    return cmp(j, s).astype(BF16)


SCAN_CHUNK = 256


def _row_scan(x, tri, reverse, pieces=2):
    n = x.shape[1] // SCAN_CHUNK
    chunks = [x[:, i * SCAN_CHUNK:(i + 1) * SCAN_CHUNK] for i in range(n)]
    out, seen = [None] * n, None
    for i in (reversed(range(n)) if reverse else range(n)):
        local = _split_dot(chunks[i], tri, pieces)
        out[i] = local if seen is None else local + seen
        total = jnp.sum(chunks[i], axis=1, keepdims=True)
        seen = total if seen is None else seen + total
    return (out[0] if n == 1 else jnp.concatenate(out, axis=1)), seen


def _att_specs(s_len, tq, dq, dv):
    q_spec = pl.BlockSpec((None, tq, dq), lambda h, i: (h, i, 0))
    k_spec = pl.BlockSpec((None, s_len, dq), lambda h, i: (h, 0, 0))
    v_spec = pl.BlockSpec((None, s_len, dv), lambda h, i: (h, 0, 0))
    o_spec = pl.BlockSpec((None, tq, dv), lambda h, i: (h, i, 0))
    r_spec = pl.BlockSpec((None, tq, 1), lambda h, i: (h, i, 0))
    return q_spec, k_spec, v_spec, o_spec, r_spec


def _key_blocks(qi, tq, tk):
    return (qi * tq) // tk, ((qi + 1) * tq + tk - 1) // tk


def _keep(qi, j, tq, tk, strict):
    row = qi * tq + lax.broadcasted_iota(jnp.int32, (tq, tk), 0)
    col = j * tk + lax.broadcasted_iota(jnp.int32, (tq, tk), 1)
    return col < row if strict else col <= row


def _log_sigmoid(z):
    return jnp.minimum(z, 0.0) - jnp.log(1.0 + jnp.exp(-jnp.abs(z)))


def sb_fwd(q, k, v, *, tq, tk, name, job=None):
    n_heads, s_len, d = q.shape
    scale = SB_HEAD_DIM ** -0.5

    def body(q_ref, k_ref, v_ref, o_ref, tot_ref):
        qi = pl.program_id(1)
        qv = q_ref[...]
        upper = _tri(SCAN_CHUNK, lambda j, s: j > s)
        n_full, n_all = _key_blocks(qi, tq, tk)

        def make_step(masked, last):
            def step(jj, carry):
                acc, rest = carry
                j = last - 1 - jj
                sl = pl.ds(pl.multiple_of(j * tk, tk), tk)
                ks, vs = k_ref[sl, :], v_ref[sl, :]
                z = _dg(qv, ks, 1, 1) * scale
                log_beta = _log_sigmoid(z)
                log_stay = log_beta - z
                if masked:
                    valid = _keep(qi, j, tq, tk, True)
                    log_stay = jnp.where(valid, log_stay, 0.0)
                after, total = _row_scan(log_stay, upper, True)
                w = jnp.exp(log_beta + after + rest)
                if masked:
                    w = jnp.where(valid, w, 0.0)
                acc = acc + _dg(w.astype(BF16), vs, 1, 0)
                return acc, rest + total
            return step

        carry = (jnp.zeros((tq, d), F32), jnp.zeros((tq, 1), F32))
        carry = lax.fori_loop(0, n_all - n_full, make_step(True, n_all), carry)
        acc, rest = lax.fori_loop(0, n_full, make_step(False, n_full), carry)
        o_ref[...] = acc
        tot_ref[...] = rest

    q_spec, k_spec, v_spec, o_spec, r_spec = _att_specs(s_len, tq, d, d)
    return ride_call(
        job, body, name=name, grid=(n_heads, s_len // tq), in_specs=[q_spec, k_spec, v_spec],
        out_specs=[o_spec, r_spec],
        out_shape=[jax.ShapeDtypeStruct((n_heads, s_len, d), F32), jax.ShapeDtypeStruct((n_heads, s_len, 1), F32)],
        ins=[q, k, v], sem=("parallel", "arbitrary"))


def sb_bwd(q, k, v, tot, do, *, tq, tk, name, job=None):
    n_heads, s_len, d = q.shape
    scale = SB_HEAD_DIM ** -0.5

    def body(q_ref, k_ref, v_ref, tot_ref, do_ref, dq_ref, dk_ref, dv_ref):
        qi = pl.program_id(1)

        @pl.when(qi == 0)
        def _():
            dk_ref[...] = jnp.zeros_like(dk_ref)
            dv_ref[...] = jnp.zeros_like(dv_ref)

        qv = q_ref[...]
        dob = do_ref[...].astype(BF16)
        total = tot_ref[...]
        incl = _tri(SCAN_CHUNK, lambda j, s: j <= s)
        excl = _tri(SCAN_CHUNK, lambda j, s: j < s)
        n_full, n_all = _key_blocks(qi, tq, tk)

        def make_step(masked):
            def step(j, carry):
                dq, stay_before, dl_before = carry
                sl = pl.ds(pl.multiple_of(j * tk, tk), tk)
                ks, vs = k_ref[sl, :], v_ref[sl, :]
                z = _dg(qv, ks, 1, 1) * scale
                log_beta = _log_sigmoid(z)
                log_stay = log_beta - z
                if masked:
                    valid = _keep(qi, j, tq, tk, True)
                    log_stay = jnp.where(valid, log_stay, 0.0)
                stay_upto, stay_sum = _row_scan(log_stay, incl, False)
                w = jnp.exp(log_beta + (total - stay_before) - stay_upto)
                if masked:
                    w = jnp.where(valid, w, 0.0)
                dl = _dg(dob, vs, 1, 1) * w
                dl_upto, dl_sum = _row_scan(dl, excl, False, pieces=1)
                dl_prefix = dl_upto + dl_before
                beta = jnp.exp(log_beta)
                dz = (dl * (1.0 - beta) - beta * dl_prefix) * scale
                if masked:
                    dz = jnp.where(valid, dz, 0.0)
                dzb = dz.astype(BF16)
                dq = dq + _dg(dzb, ks, 1, 0)
                dk_ref[sl, :] += _dg(dzb, qv, 0, 0)
                dv_ref[sl, :] += _dg(w.astype(BF16), dob, 0, 0)
                return dq, stay_before + stay_sum, dl_before + dl_sum
            return step

        zero = jnp.zeros((tq, 1), F32)
        carry = lax.fori_loop(0, n_full, make_step(False), (jnp.zeros((tq, d), F32), zero, zero))
        dq, _, _ = lax.fori_loop(n_full, n_all, make_step(True), carry)
        dq_ref[...] = dq

    q_spec, k_spec, v_spec, o_spec, r_spec = _att_specs(s_len, tq, d, d)
    shp = jax.ShapeDtypeStruct((n_heads, s_len, d), F32)
    return ride_call(
        job, body, name=name, grid=(n_heads, s_len // tq), in_specs=[q_spec, k_spec, v_spec, r_spec, o_spec],
        out_specs=[q_spec, k_spec, v_spec], out_shape=[shp, shp, shp], ins=[q, k, v, tot, do],
        sem=("arbitrary", "arbitrary"))


NEG_BIG = -1e30


def _lower_left(rows, cols):
    r = lax.broadcasted_iota(jnp.int32, (rows, cols), 0)
    c = lax.broadcasted_iota(jnp.int32, (rows, cols), 1)
    return c <= r


def _prep_specs(tq, q_prep):
    cos, _, gain = q_prep
    rope_spec = pl.BlockSpec((tq, cos.shape[1]), lambda h, i: (i, 0))
    return [rope_spec, rope_spec, pl.BlockSpec(gain.shape, lambda h, i: (0, 0))]


def sm_fwd(q, k, v, *, tq, tk, name, q_prep=None):
    n_heads, s_len, dq = q.shape
    dv = v.shape[2]
    scale = dq ** -0.5
    assert tq == tk
    half = tk // 2
    n_prep = 0 if q_prep is None else 3

    def body(*refs):
        q_ref, prep_refs = refs[0], refs[1:1 + n_prep]
        k_ref, v_ref, o_ref, lse_ref = refs[1 + n_prep:]
        qi = pl.program_id(1)
        qv = q_ref[...]
        if q_prep is not None:
            qv = f_mla_q(qv, *[r[...] for r in prep_refs])[0].astype(BF16)

        def attend(carry, q_rows, keys, keep):
            acc, m, l = carry
            sc = _dg(q_rows, k_ref[keys, :], 1, 1) * scale
            if keep is not None:
                sc = jnp.where(keep, sc, NEG_BIG)
            m_new = jnp.maximum(m, jnp.max(sc, axis=1, keepdims=True))
            p = jnp.exp(sc - m_new)
            fade = jnp.exp(m - m_new)
            return (fade * acc + _dg(p.astype(BF16), v_ref[keys, :], 1, 0), m_new,
                    fade * l + jnp.sum(p, axis=1, keepdims=True))

        carry = (jnp.zeros((tq, dv), F32), jnp.full((tq, 1), NEG_BIG, F32), jnp.zeros((tq, 1), F32))
        carry = lax.fori_loop(
            0, qi, lambda j, c: attend(c, qv, pl.ds(pl.multiple_of(j * tk, tk), tk), None), carry)
        base = pl.multiple_of(qi * tk, tk)
        carry = attend(carry, qv, pl.ds(base, half), _lower_left(tq, half))
        low = attend(tuple(t[half:] for t in carry), qv[half:], pl.ds(pl.multiple_of(base + half, half), half),
                     _lower_left(half, half))
        acc, m, l = (jnp.concatenate([t[:half], u], axis=0) for t, u in zip(carry, low))
        o_ref[...] = acc / l
        lse_ref[...] = m + jnp.log(l)

    q_spec, k_spec, v_spec, o_spec, r_spec = _att_specs(s_len, tq, dq, dv)
    prep = [] if q_prep is None else list(q_prep)
    return pl.pallas_call(
        body, name=name, grid=(n_heads, s_len // tq),
        in_specs=[q_spec] + ([] if q_prep is None else _prep_specs(tq, q_prep)) + [k_spec, v_spec],
        out_specs=[o_spec, r_spec],
        out_shape=[jax.ShapeDtypeStruct((n_heads, s_len, dv), F32), jax.ShapeDtypeStruct((n_heads, s_len, 1), F32)],
        compiler_params=_cparams(("parallel", "arbitrary")),
    )(q, *prep, k, v)


def sm_bwd(q, k, v, o, lse, do, *, tq, tk, name, q_prep=None, job=None):
    n_heads, s_len, dq = q.shape
    dv = v.shape[2]
    scale = dq ** -0.5
    assert tq == tk
    half = tk // 2
    n_prep = 0 if q_prep is None else 3

    def body(*refs):
        q_ref, prep_refs = refs[0], refs[1:1 + n_prep]
        k_ref, v_ref, o_ref, lse_ref, do_ref, dq_ref, dk_ref, dv_ref = refs[1 + n_prep:9 + n_prep]
        head, qi = pl.program_id(0), pl.program_id(1)

        @pl.when(qi == 0)
        def _():
            dk_ref[...] = jnp.zeros_like(dk_ref)
            dv_ref[...] = jnp.zeros_like(dv_ref)

        q_raw = q_ref[...]
        prep_vals = [r[...] for r in prep_refs]
        qv = q_raw if q_prep is None else f_mla_q(q_raw, *prep_vals)[0].astype(BF16)
        do = do_ref[...]
        dob = do.astype(BF16)
        delta = jnp.sum(do * o_ref[...], axis=1, keepdims=True)
        lse_v = lse_ref[...]

        def attend(rows, keys, keep):
            ks, vs = k_ref[keys, :], v_ref[keys, :]
            p = jnp.exp(_dg(qv[rows], ks, 1, 1) * scale - lse_v[rows])
            if keep is not None:
                p = jnp.where(keep, p, 0.0)
            dv_ref[keys, :] += _dg(p.astype(BF16), dob[rows], 0, 0)
            ds = (p * (_dg(dob[rows], vs, 1, 1) - delta[rows]) * scale).astype(BF16)
            dk_ref[keys, :] += _dg(ds, qv[rows], 0, 0)
            return _dg(ds, ks, 1, 0)

        everything = slice(None)
        dq_acc = lax.fori_loop(
            0, qi, lambda j, acc: acc + attend(everything, pl.ds(pl.multiple_of(j * tk, tk), tk), None),
            jnp.zeros((tq, dq), F32))
        base = pl.multiple_of(qi * tk, tk)
        dq_acc = dq_acc + attend(everything, pl.ds(base, half), _lower_left(tq, half))
        low = attend(slice(half, None), pl.ds(pl.multiple_of(base + half, half), half), _lower_left(half, half))
        dq_acc = jnp.concatenate([dq_acc[:half], dq_acc[half:] + low], axis=0)
        if q_prep is None:
            dq_ref[...] = dq_acc
        else:
            cos, sin, gain = prep_vals
            _, pull = jax.vjp(lambda t, g: f_mla_q(t, cos, sin, g)[0], q_raw, gain)
            dq_raw, d_gain = pull(dq_acc)
            dq_ref[...] = dq_raw.astype(dq_ref.dtype)
            dgain_ref = refs[9 + n_prep]
            first = jnp.logical_and(head == 0, qi == 0)

            @pl.when(first)
            def _():
                dgain_ref[...] = d_gain

            @pl.when(jnp.logical_not(first))
            def _():
                dgain_ref[...] += d_gain

    q_spec, k_spec, v_spec, o_spec, r_spec = _att_specs(s_len, tq, dq, dv)
    out_specs = [q_spec, k_spec, v_spec]
    out_shape = [jax.ShapeDtypeStruct((n_heads, s_len, dq), F32 if q_prep is None else BF16),
                 jax.ShapeDtypeStruct((n_heads, s_len, dq), F32), jax.ShapeDtypeStruct((n_heads, s_len, dv), F32)]
    prep, prep_specs = [], []
    if q_prep is not None:
        prep, prep_specs = list(q_prep), _prep_specs(tq, q_prep)
        out_specs.append(prep_specs[2])
        out_shape.append(jax.ShapeDtypeStruct(q_prep[2].shape, F32))
    return ride_call(
        job, body, name=name, grid=(n_heads, s_len // tq),
        in_specs=[q_spec] + prep_specs + [k_spec, v_spec, o_spec, r_spec, o_spec], out_specs=out_specs,
        out_shape=out_shape, ins=[q, *prep, k, v, o, lse, do], sem=("arbitrary", "arbitrary"))


def loss_head(y, target, *, tm, name):
    n_rows, width = y.shape

    def body(y_ref, t_ref, dy_ref, loss_ref):
        diff = y_ref[...] - t_ref[...]
        dy_ref[...] = diff / width
        part = 0.5 * jnp.sum(jnp.mean(diff * diff, axis=-1, keepdims=True), axis=0, keepdims=True)

        @pl.when(pl.program_id(0) == 0)
        def _():
            loss_ref[...] = jnp.zeros_like(loss_ref)

        loss_ref[...] += jnp.broadcast_to(part, loss_ref.shape)

    spec = pl.BlockSpec((tm, width), lambda r: (r, 0))
    dy, loss = pl.pallas_call(
        body, name=name, grid=(n_rows // tm,), in_specs=[spec, spec],
        out_specs=[spec, pl.BlockSpec((8, LANES), lambda r: (0, 0))],
        out_shape=[jax.ShapeDtypeStruct(y.shape, F32), jax.ShapeDtypeStruct((8, LANES), F32)],
        compiler_params=_cparams(("arbitrary",)),
    )(y, target)
    return dy, loss[0, 0]


ADAM_TILE_ELEMS = 256 * 1024


def _adam_rows(n_rows, width):
    fits = [t for t in range(16, n_rows + 1, 16) if n_rows % t == 0 and t * width <= ADAM_TILE_ELEMS]
    return max(fits) if fits else n_rows


def adamw(parts, w, m, v, *, name):
    n_layers, n_rows, width = w.shape
    assert len(parts) == n_layers
    tm = _adam_rows(n_rows, width)
    n_tiles = n_rows // tm

    def body(*refs):
        p_refs = refs[:n_layers]
        w_ref, m_ref, v_ref, g_ref, d_ref, nm_ref, nv_ref = refs[n_layers:]
        layer = pl.program_id(0)
        for this, p_ref in enumerate(p_refs):
            @pl.when(layer == this)
            def _(p_ref=p_ref):
                g = p_ref[0].astype(F32)
                for i in range(1, N_DEV):
                    g = g + p_ref[i].astype(F32)
                m_new = ADAM_B1 * m_ref[...] + (1.0 - ADAM_B1) * g
                v_new = ADAM_B2 * v_ref[...] + (1.0 - ADAM_B2) * jnp.square(g)
                m_hat = m_new / (1.0 - ADAM_B1 ** ADAM_STEP)
                v_hat = v_new / (1.0 - ADAM_B2 ** ADAM_STEP)
                g_ref[...] = g
                d_ref[...] = -ADAM_LR * (m_hat / (jnp.sqrt(v_hat) + ADAM_EPS) + ADAM_WD * w_ref[...])
                nm_ref[...] = m_new
                nv_ref[...] = v_new

    def part_spec(this):
        def index(layer, r):
            return 0, jnp.where(layer == this, r, jnp.where(layer < this, 0, n_tiles - 1)), 0
        return pl.BlockSpec((N_DEV, tm, width), index)

    spec = pl.BlockSpec((None, tm, width), lambda layer, r: (layer, r, 0))
    shp = jax.ShapeDtypeStruct(w.shape, F32)
    return pl.pallas_call(
        body, name=name, grid=(n_layers, n_tiles),
        in_specs=[part_spec(this) for this in range(n_layers)] + [spec, spec, spec],
        out_specs=[spec] * 4, out_shape=[shp] * 4, compiler_params=_cparams(("arbitrary", "arbitrary")),
    )(*parts, w, m, v)


def _me():
    return lax.axis_index("x"), lax.axis_index("y"), lax.axis_index("c")


N_PEERS = N_DEV - 1


class CommJob:
    def __init__(self, kind, arrays):
        self.kind, self.arrays, self.n = kind, list(arrays), len(arrays)

    def out_shape(self):
        lead = (N_DEV,) if self.kind == 'gather' else ()
        return [jax.ShapeDtypeStruct(lead + a.shape, a.dtype) for a in self.arrays]

    def scratch(self):
        return [pltpu.SemaphoreType.DMA((N_PEERS * self.n,)), pltpu.SemaphoreType.DMA((N_PEERS * self.n,)),
                pltpu.SemaphoreType.DMA((self.n,))]

    def phases(self, in_refs, out_refs, send_sems, recv_sems, local_sems):
        n = self.n
        x, y, c = _me()

        def remote(i, k, src, dst, to):
            return pltpu.make_async_remote_copy(
                src_ref=src, dst_ref=dst, send_sem=send_sems.at[N_PEERS * i + k],
                recv_sem=recv_sems.at[N_PEERS * i + k], device_id=to, device_id_type=MESH)

        if self.kind == 'gather':
            me, sibling = (x, y, c), (x, y, 1 - c)
            chips = [(1 - x, y), (x, 1 - y), (1 - x, 1 - y)]

            def slot(i, px, py, pc):
                return out_refs[i].at[4 * px + 2 * py + pc]

            def copy(i, k, blk, to, src=None):
                return remote(i, k, slot(i, *blk) if src is None else src, slot(i, *blk), to)

            def mine():
                return [pltpu.make_async_copy(in_refs[i], slot(i, *me), local_sems.at[i]) for i in range(n)]

            def first():
                cps = []
                for i in range(n):
                    cps.append(copy(i, 0, me, sibling, src=in_refs[i]))
                    cps += [copy(i, 1 + j, me, (*chip, c), src=in_refs[i]) for j, chip in enumerate(chips)]
                return cps

            def passed():
                return [copy(i, 4 + j, (*chip, c), sibling) for j, chip in enumerate(chips) for i in range(n)]

            def start():
                for cp in mine() + first():
                    cp.start()

            def forward():
                for j, chip in enumerate(chips):
                    for i in range(n):
                        copy(i, 1 + j, (*chip, c), me).wait_recv()
                        copy(i, 4 + j, (*chip, c), sibling).start()

            def finish():
                for i in range(n):
                    copy(i, 0, sibling, me).wait_recv()
                    for j, chip in enumerate(chips):
                        copy(i, 4 + j, (*chip, 1 - c), me).wait_recv()
                for cp in first() + passed():
                    cp.wait_send()
                for cp in mine():
                    cp.wait()

            return start, forward, finish

        my_slot = 4 * x + 2 * y + c

        def mine():
            return [pltpu.make_async_copy(in_refs[i].at[my_slot], out_refs[i].at[my_slot], local_sems.at[i])
                    for i in range(n)]

        def copies():
            cps = []
            for k in range(1, N_DEV):
                px, py, pc = x ^ (k >> 2), y ^ ((k >> 1) & 1), c ^ (k & 1)
                cps += [remote(i, k - 1, in_refs[i].at[4 * px + 2 * py + pc], out_refs[i].at[my_slot], (px, py, pc))
                        for i in range(n)]
            return cps

        def start():
            for cp in mine() + copies():
                cp.start()

        def finish():
            for cp in copies():
                cp.wait_recv()
            for cp in copies():
                cp.wait_send()
            for cp in mine():
                cp.wait()

        return start, (lambda: None), finish


def comm_call(kind, arrays, *, name):
    job = CommJob(kind, arrays)
    n = job.n

    def body(*refs):
        start, forward, finish = job.phases(refs[:n], refs[n:2 * n], *refs[2 * n:])
        start()
        forward()
        finish()

    hbm = pl.BlockSpec(memory_space=pl.ANY)
    return pl.pallas_call(body, name=name, out_shape=job.out_shape(), in_specs=[hbm] * n, out_specs=[hbm] * n,
                          scratch_shapes=job.scratch())(*job.arrays)


def ride_call(job, compute, *, name, grid, in_specs, out_specs, out_shape, ins, sem, scratch=()):
    scratch = list(scratch)
    if job is None:
        return pl.pallas_call(compute, name=name, grid=grid, in_specs=in_specs, out_specs=out_specs,
                              out_shape=out_shape, scratch_shapes=scratch, compiler_params=_cparams(sem))(*ins), None
    n, n_in, n_out, n_scr = job.n, len(ins), len(out_shape), len(scratch)
    n_steps = 1
    for g in grid:
        n_steps *= g

    def body(*refs):
        ins_, job_ins = refs[:n_in], refs[n_in:n_in + n]
        outs, job_outs = refs[n_in + n:n_in + n + n_out], refs[n_in + n + n_out:n_in + 2 * n + n_out]
        rest = refs[n_in + 2 * n + n_out:]
        start, forward, finish = job.phases(job_ins, job_outs, *rest[n_scr:])
        now = 0
        for axis, g in enumerate(grid):
            now = now * g + pl.program_id(axis)
        pl.when(now == 0)(start)
        pl.when(now == n_steps // 2)(forward)
        compute(*ins_, *outs, *rest[:n_scr])
        pl.when(now == n_steps - 1)(finish)

    hbm = pl.BlockSpec(memory_space=pl.ANY)
    res = pl.pallas_call(
        body, name=name, grid=grid, in_specs=list(in_specs) + [hbm] * n,
        out_specs=list(out_specs) + [hbm] * n, out_shape=list(out_shape) + job.out_shape(),
        scratch_shapes=scratch + job.scratch(), compiler_params=_cparams(("arbitrary",) * len(grid)),
    )(*ins, *job.arrays)
    return res[:n_out], res[n_out:]


def to_heads(t, n_heads):
    s_len = t.shape[0]
    return t.reshape(s_len, n_heads, -1).transpose(1, 0, 2)


def from_heads(t):
    return t.transpose(1, 0, 2).reshape(t.shape[1], -1)


def gathered_to_full(t, axis):
    shp = t.shape[1:]
    return jnp.moveaxis(t, 0, axis).reshape(shp[:axis] + (N_DEV * shp[axis],) + shp[axis + 1:])


def full_to_owner_major(g, axis):
    shp = g.shape
    t = jnp.moveaxis(g.reshape(shp[:axis] + (N_DEV, shp[axis] // N_DEV) + shp[axis + 1:]), axis, 0)
    return t.reshape(N_DEV, -1, t.shape[-1])


def _small_rows(shape):
    n = 1
    for s in shape:
        n *= s
    return -(-n // LANES)


def pack_small(arrs, shapes):
    pieces = []
    for n in SMALL:
        flat = arrs[n].reshape(-1)
        flat = jnp.pad(flat, (0, _small_rows(shapes[n]) * LANES - flat.shape[0]))
        pieces.append(flat.reshape(-1, LANES))
    flat = jnp.concatenate(pieces, axis=0)
    return jnp.pad(flat, ((0, -flat.shape[0] % SMALL_ROW_MULTIPLE), (0, 0)))


def unpack_small(flat, shapes):
    out, r = {}, 0
    for n in SMALL:
        rows = _small_rows(shapes[n])
        size = 1
        for s in shapes[n]:
            size *= s
        out[n] = flat[r:r + rows].reshape(-1)[:size].reshape(shapes[n])
        r += rows
    return out


ROW_TM = 256
XATT_TM = 1024
HEAD_TM = 1024
SG_TM = 8 * SG_CHUNK
SB_TILES = (512, 512)
SM_TILE = 1024


def _norm_fwd(x, g, name):
    return prow(f_rms, [x], params=[g.reshape(1, -1)], outs=[(x.shape[1], BF16, False)], tm=ROW_TM, name=name)[0]


def _norm_bwd(x, g, dh, add, name, want_row=True):
    res = prow_vjp(f_rms, [x], params=[g.reshape(1, -1)], cts=[dh], row_grad=[want_row],
                   adds=[add] if want_row else None, tm=ROW_TM, name=name)
    return (res[0], res[1].reshape(-1)) if want_row else (None, res[0].reshape(-1))


def _out_proj(a, w, x, next_gain, alpha, name):
    if next_gain is None:
        return pmm(a, w, res=x, alpha=alpha, name=name), None
    return pmm(a, w, res=x, alpha=alpha, norm_out=next_gain.reshape(1, -1), name=name)


def _in_proj_bwd(d, w, x, gain, dy, name, **kw):
    dx, g_gain = pmm(d, w, tb=True, norm_bwd=(x, gain.reshape(1, -1), dy), name=name, **kw)
    return dx, g_gain.reshape(-1)


def ffn_fwd(x, h, p, tag, next_gain, job=None, after_job=None):
    (gate, up, act), landed = ffn_gate_up(h, p['w_gu'], name=f"{tag}_gu", job=job)
    if job is not None:
        after_job(landed)
    out = _out_proj(act, p['w_down'], x, next_gain, 0.5, f"{tag}_down")
    return out, (x, h, gate, up, act)


def _no_rider(run, **own):
    return run(None)[0]


def _pmm_pair(*args, job, **kw):
    out = pmm(*args, job=job, **kw)
    return out if job is not None else (out, None)


def ffn_bwd(dy, p, saved, tag, with_job=_no_rider):
    x, h, gate, up, act = saved
    d_gate, d_up = with_job(lambda job: ffn_gate_up_bwd(dy, p['w_down'], gate, up, alpha=0.5, name=f"{tag}_dact",
                                                        job=job))
    g_down = pmm(act, dy, ta=True, out_dtype=GRAD_WIRE, alpha=0.5, name=f"{tag}_gdown")
    g_gate = with_job(lambda job: _pmm_pair(h, d_gate, ta=True, out_dtype=GRAD_WIRE, name=f"{tag}_ggate", job=job),
                      w_down=g_down)
    g_gu = jnp.concatenate([g_gate, pmm(h, d_up, ta=True, out_dtype=GRAD_WIRE, name=f"{tag}_gup")], axis=1)
    dh_gate = pmm(d_gate, p['w_gu'], tb=True, name=f"{tag}_dh_gate")
    dx, g_norm = with_job(
        lambda job: _pmm_pair(d_up, p['w_gu'], tb=True, norm_bwd=(x, p['norm'].reshape(1, -1), dy), res=dh_gate,
                              k_off_b=D_FF // _pick(D_FF), name=f"{tag}_dh", job=job), w_gu=g_gu)
    return dx, {'norm': g_norm.reshape(-1), 'w_gu': g_gu, 'w_down': g_down}


def even_mixer_fwd(x, h, p, next_gain, job=None, after_job=None):
    proj = pmm(h, p['w_in'], name="sbg_in")
    q, k, v = (to_heads(proj[:, i * SB_WIDTH:(i + 1) * SB_WIDTH], SB_HEADS).astype(BF16) for i in range(3))
    (o_sb, tot), landed = sb_fwd(q, k, v, tq=SB_TILES[0], tk=SB_TILES[1], name="sb_fwd", job=job)
    if job is not None:
        after_job(landed)
    z = proj[:, 3 * SB_WIDTH:]
    ln_g, ln_b = p['ln_gain'].reshape(1, -1), p['ln_bias'].reshape(1, -1)
    u, gn = prow(f_gate_prep, [z], params=[ln_g, ln_b], outs=[(SG_WIDTH, F32, False)] * 2, tm=ROW_TM,
                 name="sgu_prep")
    gn_g, u_g = to_heads(gn, SG_GROUPS), to_heads(u, SG_GROUPS)
    b3 = p['sgu_b'].reshape(SG_GROUPS, SG_CHUNK, 1)
    o_sg = prow(f_spatial_gate, [gn_g, u_g], gparams=[p['sgu_w'], b3], outs=[(SG_GROUP_DIM, F32, True)],
                tm=SG_TM, name="sgu_mix")[0]
    cat = jnp.concatenate([from_heads(o_sb), from_heads(o_sg)], axis=-1).astype(BF16)
    out = _out_proj(cat, p['w_out'], x, next_gain, 1.0, "sbg_out")
    return out, (x, h, q, k, v, tot, z, gn_g, u_g, b3, cat)


def even_mixer_bwd(dy, p, saved, job_of=None):
    x, h, q, k, v, tot, z, gn_g, u_g, b3, cat = saved
    d_cat = pmm(dy, p['w_out'], tb=True, name="sbg_dcat")
    g_out = pmm(cat, dy, ta=True, out_dtype=GRAD_WIRE, name="sbg_gout")
    d_osb = to_heads(d_cat[:, :SB_WIDTH], SB_HEADS)
    d_osg = to_heads(d_cat[:, SB_WIDTH:], SG_GROUPS)
    d_gn_g, d_u_g, g_w, g_b = prow_vjp(f_spatial_gate, [gn_g, u_g], gparams=[p['sgu_w'], b3], cts=[d_osg],
                                       row_grad=[True, True], tm=SG_TM, name="sgu_dmix")
    ln_g, ln_b = p['ln_gain'].reshape(1, -1), p['ln_bias'].reshape(1, -1)
    d_z, g_lng, g_lnb = prow_vjp(f_gate_prep, [z], params=[ln_g, ln_b], cts=[from_heads(d_u_g), from_heads(d_gn_g)],
                                 row_grad=[True], row_dtypes=[BF16], tm=ROW_TM, name="sgu_dprep")
    job = None if job_of is None else job_of({'w_out': g_out})
    (dq, dk, dv), landed = sb_bwd(q, k, v, tot, d_osb, tq=SB_TILES[0], tk=SB_TILES[1], name="sb_bwd", job=job)
    d_proj = jnp.concatenate([from_heads(dq).astype(BF16), from_heads(dk).astype(BF16), from_heads(dv).astype(BF16),
                              d_z], axis=-1)
    g_in = pmm(h, d_proj, ta=True, out_dtype=GRAD_WIRE, name="sbg_gin")
    dx, g_norm = _in_proj_bwd(d_proj, p['w_in'], x, p['norm'], dy, "sbg_dh")
    return dx, {'norm': g_norm, 'w_in': g_in, 'ln_gain': g_lng.reshape(-1), 'ln_bias': g_lnb.reshape(-1),
                'sgu_w': g_w, 'sgu_b': g_b.reshape(SG_GROUPS, SG_CHUNK), 'w_out': g_out}, landed


def mla_fwd(x, h, cos, sin, p, next_gain):
    proj = pmm(h, p['w_in'], name="mla_in")
    c_q, c_kv, k_r = proj[:, :MLA_Q_LORA], proj[:, MLA_Q_LORA:MLA_Q_LORA + MLA_KV_LORA], proj[:, MLA_Q_LORA + MLA_KV_LORA:]
    cqn = _norm_fwd(c_q, p['q_lora_gain'], "mla_qlora_norm")
    ckvn = _norm_fwd(c_kv, p['kv_lora_gain'], "mla_kvlora_norm")
    q_h = to_heads(pmm(cqn, p['w_uq'], name="mla_uq"), MLA_HEADS)
    kv_h = to_heads(pmm(ckvn, p['w_ukv'], name="mla_ukv"), MLA_HEADS)
    k_nope, v = kv_h[..., :MLA_NOPE], kv_h[..., MLA_NOPE:].astype(BF16)
    q_g, k_g = p['q_gain'].reshape(1, -1), p['k_gain'].reshape(1, -1)
    kp = prow(f_mla_k, [k_nope, k_r, cos, sin], params=[k_g], outs=[(MLA_QK, BF16, True)], tm=HEAD_TM,
              name="mla_kprep")[0]
    o, lse = sm_fwd(q_h, kp, v, tq=SM_TILE, tk=SM_TILE, name="mla_att_fwd", q_prep=(cos, sin, q_g))
    o_flat = from_heads(o).astype(BF16)
    out = _out_proj(o_flat, p['w_out'], x, next_gain, 1.0, "mla_out")
    return out, (x, h, c_q, c_kv, k_r, cqn, ckvn, q_h, k_nope, v, kp, o, lse, o_flat, q_g, k_g)


def mla_bwd(dy, cos, sin, p, saved, job_of=None):
    x, h, c_q, c_kv, k_r, cqn, ckvn, q_h, k_nope, v, kp, o, lse, o_flat, q_g, k_g = saved
    do = to_heads(pmm(dy, p['w_out'], tb=True, name="mla_do"), MLA_HEADS)
    g_out = pmm(o_flat, dy, ta=True, out_dtype=GRAD_WIRE, name="mla_gout")
    job = None if job_of is None else job_of({'w_out': g_out})
    (dq_h, dkp, dv, g_qg), landed = sm_bwd(q_h, kp, v, o, lse, do, tq=SM_TILE, tk=SM_TILE, name="mla_att_bwd",
                                           q_prep=(cos, sin, q_g), job=job)
    dk_nope, dk_r, g_kg = prow_vjp(f_mla_k, [k_nope, k_r, cos, sin], params=[k_g], cts=[dkp],
                                   row_grad=[True, True, False, False], tm=HEAD_TM, name="mla_dkprep")
    d_q = from_heads(dq_h)
    d_kv = from_heads(jnp.concatenate([dk_nope, dv], axis=-1)).astype(BF16)
    g_uq = pmm(cqn, d_q, ta=True, out_dtype=GRAD_WIRE, name="mla_guq")
    d_cqn = pmm(d_q, p['w_uq'], tb=True, name="mla_dcqn")
    g_ukv = pmm(ckvn, d_kv, ta=True, out_dtype=GRAD_WIRE, name="mla_gukv")
    d_ckvn = pmm(d_kv, p['w_ukv'], tb=True, name="mla_dckvn")
    d_cq, g_qlora = _norm_bwd(c_q, p['q_lora_gain'], d_cqn, None, "mla_dqlora_norm")
    d_ckv, g_kvlora = _norm_bwd(c_kv, p['kv_lora_gain'], d_ckvn, None, "mla_dkvlora_norm")
    d_proj = jnp.concatenate([d_cq, d_ckv, dk_r], axis=-1).astype(BF16)
    g_in = pmm(h, d_proj, ta=True, out_dtype=GRAD_WIRE, name="mla_gin")
    dx, g_norm = _in_proj_bwd(d_proj, p['w_in'], x, p['norm'], dy, "mla_dh")
    return dx, {'norm': g_norm, 'w_in': g_in, 'q_lora_gain': g_qlora, 'kv_lora_gain': g_kvlora, 'w_uq': g_uq,
                'w_ukv': g_ukv, 'q_gain': g_qg.reshape(-1), 'k_gain': g_kg.reshape(-1), 'w_out': g_out}, landed


def xattn_fwd(x, hq, mem, p, tag, next_gain):
    hm = _norm_fwd(mem, p['mem_norm'], f"{tag}_mem_norm")
    q_h = ColGroups(pmm(hq, p['wq'], name=f"{tag}_q"), MEM_HEAD_DIM)
    kv = pmm(hm, p['wkv'], name=f"{tag}_kv").reshape(mem.shape[0], MEM_HEADS, 2 * MEM_HEAD_DIM).transpose(1, 0, 2)
    k_h, v_h = kv[..., :MEM_HEAD_DIM], kv[..., MEM_HEAD_DIM:]
    q_g, k_g = p['q_gain'].reshape(1, -1), p['k_gain'].reshape(1, -1)
    o_flat = prow(f_xattn, [q_h], gparams=[k_h, v_h], params=[q_g, k_g], outs=[(MEM_HEAD_DIM, BF16, 'cols')],
                  tm=XATT_TM, name=f"{tag}_att")[0]
    out = _out_proj(o_flat, p['wo'], x, next_gain, 1.0, f"{tag}_out")
    return out, (x, mem, hq, hm, q_h, k_h, v_h, q_g, k_g, o_flat)


def xattn_bwd(dy, p, saved, tag):
    x, mem, hq, hm, q_h, k_h, v_h, q_g, k_g, o_flat = saved
    d_o = ColGroups(pmm(dy, p['wo'], tb=True, name=f"{tag}_do"), MEM_HEAD_DIM)
    g_wo = pmm(o_flat, dy, ta=True, out_dtype=GRAD_WIRE, name=f"{tag}_gwo")
    d_q, dk_h, dv_h, g_qg, g_kg = prow_vjp(f_xattn, [q_h], gparams=[k_h, v_h], params=[q_g, k_g], cts=[d_o],
                                           row_grad=[True], row_dtypes=[BF16], tm=XATT_TM, name=f"{tag}_datt")
    d_kv = jnp.concatenate([dk_h, dv_h], axis=-1).transpose(1, 0, 2).reshape(mem.shape[0], -1).astype(BF16)
    g_wq = pmm(hq, d_q, ta=True, out_dtype=GRAD_WIRE, name=f"{tag}_gwq")
    dx, g_norm = _in_proj_bwd(d_q, p['wq'], x, p['norm'], dy, f"{tag}_dhq")
    g_wkv = pmm(hm, d_kv, ta=True, out_dtype=GRAD_WIRE, name=f"{tag}_gwkv")
    dhm = pmm(d_kv, p['wkv'], tb=True, name=f"{tag}_dhm")
    _, g_mem_norm = _norm_bwd(mem, p['mem_norm'], dhm, None, f"{tag}_dmem_norm", want_row=False)
    return dx, {'norm': g_norm, 'mem_norm': g_mem_norm, 'wq': g_wq, 'wkv': g_wkv, 'q_gain': g_qg.reshape(-1),
                'k_gain': g_kg.reshape(-1), 'wo': g_wo}


def rope_tables(positions):
    half = MLA_ROPE // 2
    inv_freq = ROPE_THETA ** (-jnp.arange(half, dtype=F32) / half)
    ang = positions.astype(F32)[:, None] * inv_freq
    return jnp.cos(ang), jnp.sin(ang)


FIRST_UNIT = ('ffn_pre_w_gu', 0)
EARLY_UNITS = [('ffn_pre_w_down', 0), ('sbg_w_in', 0)]


def local_step(x, mem, positions, target, w, shards):
    cos, sin = rope_tables(positions)
    full = {}

    def absorb(units, gathered):
        for (n, layer), t in zip(units, gathered):
            full[(n, layer)] = gathered_to_full(t, BIG[n] - 1)

    late_units = [u for u in shards if u != FIRST_UNIT and u not in EARLY_UNITS]
    absorb([FIRST_UNIT], comm_call('gather', [shards[FIRST_UNIT]], name="gather_weights_first"))
    first_ffn_p = {'norm': w['ffn_pre_norm'][0], 'w_gu': full[FIRST_UNIT]}

    def ffn_params(kind, layer):
        return {'norm': w[f'ffn_{kind}_norm'][layer], 'w_gu': full[(f'ffn_{kind}_w_gu', layer)],
                'w_down': full[(f'ffn_{kind}_w_down', layer)]}

    def xattn_params(layer):
        return {'norm': w['xmem_norm'][layer], 'mem_norm': w['xmem_mem_norm'][layer], 'wq': full[('xmem_wq', layer)],
                'wkv': full[('xmem_wkv', layer)], 'q_gain': w['xmem_q_gain'][layer], 'k_gain': w['xmem_k_gain'][layer],
                'wo': full[('xmem_wo', layer)]}

    even_p = {'norm': w['mix_norm'][0], 'ln_gain': w['sgu_ln_gain'][0], 'ln_bias': w['sgu_ln_bias'][0],
              'sgu_w': w['sgu_w'][0], 'sgu_b': w['sgu_b'][0]}

    def early_weights_landed(gathered):
        absorb(EARLY_UNITS, gathered)
        first_ffn_p['w_down'] = full[('ffn_pre_w_down', 0)]
        even_p['w_in'] = full[('sbg_w_in', 0)]

    def late_weights_landed(gathered):
        absorb(late_units, gathered)
        even_p['w_out'] = full[('sbg_w_out', 0)]

    def mla_params():
        return {'norm': w['mix_norm'][1], 'w_in': full[('mla_w_in', 0)], 'q_lora_gain': w['mla_q_lora_gain'][0],
                'kv_lora_gain': w['mla_kv_lora_gain'][0], 'w_uq': full[('mla_w_uq', 0)],
                'w_ukv': full[('mla_w_ukv', 0)], 'q_gain': w['mla_q_gain'][0], 'k_gain': w['mla_k_gain'][0],
                'w_out': full[('mla_w_out', 0)]}

    saved = []
    h = _norm_fwd(x, w['ffn_pre_norm'][0], "ffn_pre0_norm")
    for layer in range(DEPTH):
        if layer == 0:
            (x, h), s_pre = ffn_fwd(x, h, first_ffn_p, "ffn_pre0", w['mix_norm'][0],
                                    job=CommJob('gather', [shards[u] for u in EARLY_UNITS]),
                                    after_job=early_weights_landed)
        else:
            (x, h), s_pre = ffn_fwd(x, h, ffn_params('pre', layer), f"ffn_pre{layer}", w['mix_norm'][layer])
        if layer % 2 == 0:
            (x, h), s_mix = even_mixer_fwd(x, h, even_p, w['xmem_norm'][layer],
                                           job=CommJob('gather', [shards[u] for u in late_units]),
                                           after_job=late_weights_landed)
        else:
            (x, h), s_mix = mla_fwd(x, h, cos, sin, mla_params(), w['xmem_norm'][layer])
        (x, h), s_x = xattn_fwd(x, h, mem, xattn_params(layer), f"xmem{layer}", w['ffn_post_norm'][layer])
        following = w['ffn_pre_norm'][layer + 1] if layer + 1 < DEPTH else None
        (x, h), s_post = ffn_fwd(x, h, ffn_params('post', layer), f"ffn_post{layer}", following)
        saved.append((s_pre, s_mix, s_x, s_post))

    dx, loss = loss_head(x, target, tm=ROW_TM, name="loss_head")

    ready, riding, landed = {}, [], {}

    def offer(name, layer, g):
        ready[(name, layer)] = full_to_owner_major(g, BIG[name] - 1)

    def ride(name):
        def job_of(own):
            offer(name, 0, own['w_out'])
            riding[:] = list(ready)
            return CommJob('exchange', [ready.pop(u) for u in riding])
        return job_of

    def last_rides(run, **own):
        for kind, g in own.items():
            offer('ffn_pre_' + kind, 0, g)
        units = list(ready)
        if not units:
            return run(None)[0]
        res, arrived = run(CommJob('exchange', [ready.pop(u) for u in units]))
        landed.update(zip(units, arrived))
        return res

    per_layer = []
    for layer in reversed(range(DEPTH)):
        s_pre, s_mix, s_x, s_post = saved[layer]
        dx, g_post = ffn_bwd(dx, ffn_params('post', layer), s_post, f"ffn_post{layer}")
        offer('ffn_post_w_gu', layer, g_post['w_gu'])
        offer('ffn_post_w_down', layer, g_post['w_down'])
        dx, g_x = xattn_bwd(dx, xattn_params(layer), s_x, f"xmem{layer}")
        for n in ('wq', 'wkv', 'wo'):
            offer('xmem_' + n, layer, g_x[n])
        if layer % 2 == 0:
            dx, g_mix, arrived = even_mixer_bwd(dx, even_p, s_mix, job_of=ride('sbg_w_out'))
            landed.update(zip(riding, arrived))
            offer('sbg_w_in', 0, g_mix['w_in'])
        else:
            dx, g_mix, arrived = mla_bwd(dx, cos, sin, mla_params(), s_mix, job_of=ride('mla_w_out'))
            landed.update(zip(riding, arrived))
            for n in ('w_in', 'w_uq', 'w_ukv'):
                offer('mla_' + n, 0, g_mix[n])
        if layer == 0:
            dx, g_pre = ffn_bwd(dx, ffn_params('pre', layer), s_pre, f"ffn_pre{layer}", with_job=last_rides)
        else:
            dx, g_pre = ffn_bwd(dx, ffn_params('pre', layer), s_pre, f"ffn_pre{layer}")
            offer('ffn_pre_w_gu', layer, g_pre['w_gu'])
            offer('ffn_pre_w_down', layer, g_pre['w_down'])
        per_layer.append((layer, g_pre, g_mix, g_x, g_post))
    per_layer.sort(key=lambda t: t[0])
    assert not ready

    def stack(pick):
        return jnp.stack([pick(t) for t in per_layer])

    g_even, g_mla = per_layer[0][2], per_layer[1][2]
    small_grads = {
        'ffn_pre_norm': stack(lambda t: t[1]['norm']), 'mix_norm': stack(lambda t: t[2]['norm']),
        'sgu_ln_gain': g_even['ln_gain'][None], 'sgu_ln_bias': g_even['ln_bias'][None],
        'sgu_w': g_even['sgu_w'][None], 'sgu_b': g_even['sgu_b'][None],
        'mla_q_lora_gain': g_mla['q_lora_gain'][None], 'mla_kv_lora_gain': g_mla['kv_lora_gain'][None],
        'mla_q_gain': g_mla['q_gain'][None], 'mla_k_gain': g_mla['k_gain'][None],
        'xmem_norm': stack(lambda t: t[3]['norm']), 'xmem_mem_norm': stack(lambda t: t[3]['mem_norm']),
        'xmem_q_gain': stack(lambda t: t[3]['q_gain']), 'xmem_k_gain': stack(lambda t: t[3]['k_gain']),
        'ffn_post_norm': stack(lambda t: t[4]['norm']),
    }
    return loss, dx, small_grads, landed


def _device_slot():
    x, y, c = _me()
    return 4 * x + 2 * y + c


def kernel(x, mem, positions, ffn_pre_norm, ffn_pre_w_gu, ffn_pre_w_down, mix_norm, sbg_w_in, sgu_ln_gain, sgu_ln_bias, sgu_w, sgu_b, sbg_w_out, mla_w_in, mla_q_lora_gain, mla_kv_lora_gain, mla_w_uq, mla_w_ukv, mla_q_gain, mla_k_gain, mla_w_out, xmem_norm, xmem_mem_norm, xmem_wq, xmem_wkv, xmem_q_gain, xmem_k_gain, xmem_wo, ffn_post_norm, ffn_post_w_gu, ffn_post_w_down, loss_target, m_ffn_pre_norm, m_ffn_pre_w_gu, m_ffn_pre_w_down, m_mix_norm, m_sbg_w_in, m_sgu_ln_gain, m_sgu_ln_bias, m_sgu_w, m_sgu_b, m_sbg_w_out, m_mla_w_in, m_mla_q_lora_gain, m_mla_kv_lora_gain, m_mla_w_uq, m_mla_w_ukv, m_mla_q_gain, m_mla_k_gain, m_mla_w_out, m_xmem_norm, m_xmem_mem_norm, m_xmem_wq, m_xmem_wkv, m_xmem_q_gain, m_xmem_k_gain, m_xmem_wo, m_ffn_post_norm, m_ffn_post_w_gu, m_ffn_post_w_down, v_ffn_pre_norm, v_ffn_pre_w_gu, v_ffn_pre_w_down, v_mix_norm, v_sbg_w_in, v_sgu_ln_gain, v_sgu_ln_bias, v_sgu_w, v_sgu_b, v_sbg_w_out, v_mla_w_in, v_mla_q_lora_gain, v_mla_kv_lora_gain, v_mla_w_uq, v_mla_w_ukv, v_mla_q_gain, v_mla_k_gain, v_mla_w_out, v_xmem_norm, v_xmem_mem_norm, v_xmem_wq, v_xmem_wkv, v_xmem_q_gain, v_xmem_k_gain, v_xmem_wo, v_ffn_post_norm, v_ffn_post_w_gu, v_ffn_post_w_down):
    args = locals()
    w_in = {n: args[n] for n in WEIGHTS}
    m_in = {n: args["m_" + n] for n in WEIGHTS}
    v_in = {n: args["v_" + n] for n in WEIGHTS}
    slot = _device_slot()

    tiny = jnp.zeros((8, LANES), F32)
    for i, src in enumerate((w_in, m_in, v_in)):
        tiny = tiny.at[i, :64].set(src['mla_q_lora_gain'][0]).at[i + 3, :32].set(src['mla_kv_lora_gain'][0])
    tiny_all = comm_call('gather', [tiny], name="gather_lora_gains")[0]
    full_small = []
    for i, src in enumerate((w_in, m_in, v_in)):
        d = {n: src[n] for n in SMALL}
        d['mla_q_lora_gain'] = tiny_all[:, i, :64].reshape(1, MLA_Q_LORA)
        d['mla_kv_lora_gain'] = tiny_all[:, i + 3, :32].reshape(1, MLA_KV_LORA)
        full_small.append(d)
    w_small, m_small, v_small = full_small
    small_shapes = {n: w_small[n].shape for n in SMALL}

    shards = {(n, layer): w_in[n][layer].astype(BF16) for n in BIG for layer in range(w_in[n].shape[0])}
    loss, dx, grads, landed = local_step(x[0], mem[0], positions[0], loss_target[0], w_small, shards)
    loss = lax.psum(loss, ("x", "y", "c"))
    big_out = {n: adamw([landed[(n, layer)] for layer in range(w_in[n].shape[0])], w_in[n], m_in[n], v_in[n],
                        name=f"adamw_{n}") for n in BIG}

    small_parts = comm_call('gather', [pack_small(grads, small_shapes)], name="gather_small_grads")
    small_out = adamw(small_parts, pack_small(w_small, small_shapes)[None], pack_small(m_small, small_shapes)[None],
                      pack_small(v_small, small_shapes)[None], name="adamw_small")
    small_out = [unpack_small(t[0], small_shapes) for t in small_out]
    for d in small_out:
        for n, width in zip(GAIN_SHARDED, (64, 32)):
            d[n] = lax.dynamic_slice(d[n], (0, slot * width), (1, width))

    outs = [loss, dx[None]]
    for kind, small_d in enumerate(small_out):
        outs += [big_out[n][kind] if n in BIG else small_d[n] for n in WEIGHTS]
    return tuple(outs)
```

```python
import functools

import jax
import jax.numpy as jnp
from jax import lax
from jax.experimental import pallas as pl
from jax.experimental.pallas import tpu as pltpu

F32 = jnp.float32
BF16 = jnp.bfloat16
MESH = pl.DeviceIdType.MESH
N_DEV = 8

VMEM_LIMIT_BYTES = 56 * 1024 * 1024
LANES = 128

D_MODEL = 1024
DEPTH = 2
D_FF = 2816
EPS = 1e-6
SB_HEADS, SB_HEAD_DIM = 8, 64
SB_WIDTH = SB_HEADS * SB_HEAD_DIM
SG_GROUPS, SG_GROUP_DIM, SG_CHUNK = 8, 64, 128
SG_WIDTH = SG_GROUPS * SG_GROUP_DIM
MLA_HEADS, MLA_NOPE, MLA_ROPE, MLA_V = 16, 64, 32, 64
MLA_QK = MLA_NOPE + MLA_ROPE
MLA_Q_LORA, MLA_KV_LORA = 512, 256
ROPE_THETA = 10000.0
MEM_HEADS = 4
MEM_HEAD_DIM = D_MODEL // MEM_HEADS

ADAM_LR, ADAM_B1, ADAM_B2, ADAM_EPS, ADAM_WD, ADAM_STEP = 0.001, 0.9, 0.999, 1e-08, 0.01, 10

WEIGHTS = ['ffn_pre_norm', 'ffn_pre_w_gu', 'ffn_pre_w_down', 'mix_norm', 'sbg_w_in', 'sgu_ln_gain', 'sgu_ln_bias',
           'sgu_w', 'sgu_b', 'sbg_w_out', 'mla_w_in', 'mla_q_lora_gain', 'mla_kv_lora_gain', 'mla_w_uq', 'mla_w_ukv',
           'mla_q_gain', 'mla_k_gain', 'mla_w_out', 'xmem_norm', 'xmem_mem_norm', 'xmem_wq', 'xmem_wkv',
           'xmem_q_gain', 'xmem_k_gain', 'xmem_wo', 'ffn_post_norm', 'ffn_post_w_gu', 'ffn_post_w_down']
BIG = {'ffn_pre_w_gu': 2, 'ffn_pre_w_down': 1, 'sbg_w_in': 2, 'sbg_w_out': 1, 'mla_w_in': 1, 'mla_w_uq': 2,
       'mla_w_ukv': 2, 'mla_w_out': 1, 'xmem_wq': 1, 'xmem_wkv': 2, 'xmem_wo': 1, 'ffn_post_w_gu': 2,
       'ffn_post_w_down': 1}
GAIN_SHARDED = ('mla_q_lora_gain', 'mla_kv_lora_gain')
SMALL = [n for n in WEIGHTS if n not in BIG]
GRAD_WIRE = BF16
FFN_SAVE = BF16
SMALL_ROW_MULTIPLE = 16


def _cparams(sem=None):
    return pltpu.CompilerParams(dimension_semantics=sem, vmem_limit_bytes=VMEM_LIMIT_BYTES)


MM_TILE_CAP = 1408


def _pick(dim, cap=MM_TILE_CAP):
    if dim % LANES:
        return dim
    return max(t for t in range(LANES, min(dim, cap) + 1, LANES) if dim % t == 0)


def _rms(x, g):
    return x * lax.rsqrt(jnp.mean(x * x, axis=-1, keepdims=True) + EPS) * g


def pmm(a, b, *, a2=None, ta=False, tb=False, out_dtype=F32, res=None, alpha=1.0, norm_out=None, norm_bwd=None,
        job=None, name):
    kdim, m = (a.shape if ta else a.shape[::-1])
    n = b.shape[0] if tb else b.shape[1]
    tm, tn, tk = _pick(m), _pick(n), _pick(kdim)
    whole_rows = norm_out is not None or norm_bwd is not None
    if whole_rows:
        assert tn == n
    if norm_bwd is not None:
        tm = min(tm, 512)
    nk1 = kdim // tk
    nk = nk1 if a2 is None else 2 * nk1
    assert a2 is None or (a2.shape == a.shape and not ta)
    dims = (((0 if ta else 1,), (1 if tb else 0,)), ((), ()))
    n_lead = 2 if a2 is None else 3
    n_extra = (res is not None) + (norm_out is not None) + (0 if norm_bwd is None else 2 + (norm_bwd[2] is not None))

    def body(*refs):
        a_ref, b_ref = refs[:2]
        extra = list(refs[n_lead:n_lead + n_extra])
        outs, acc_ref = refs[n_lead + n_extra:-1], refs[-1]
        i, k = pl.program_id(0), pl.program_id(2)

        @pl.when(k == 0)
        def _():
            acc_ref[...] = jnp.zeros_like(acc_ref)

        def accumulate(lhs_ref):
            acc_ref[...] += lax.dot_general(lhs_ref[...].astype(BF16), b_ref[...].astype(BF16), dims,
                                            preferred_element_type=F32)

        if a2 is None:
            accumulate(a_ref)
        else:
            pl.when(k < nk1)(lambda: accumulate(a_ref))
            pl.when(k >= nk1)(lambda: accumulate(refs[2]))

        @pl.when(k == nk - 1)
        def _():
            r = acc_ref[...]
            if alpha != 1.0:
                r = r * alpha
            if res is not None:
                r = extra.pop(0)[...] + r
            if norm_bwd is None:
                outs[0][...] = r.astype(out_dtype)
            if norm_out is not None:
                outs[1][...] = _rms(r, extra.pop(0)[...]).astype(BF16)
            if norm_bwd is not None:
                x_ref, g_ref = extra.pop(0), extra.pop(0)
                _, pull = jax.vjp(_rms, x_ref[...], g_ref[...])
                dx, dg = pull(r)
                if norm_bwd[2] is not None:
                    dx = dx + extra.pop(0)[...]
                outs[0][...] = dx

                @pl.when(i == 0)
                def _():
                    outs[1][...] = dg

                @pl.when(i != 0)
                def _():
                    outs[1][...] += dg

    gi, gj = m // tm, n // tn
    a_bytes, b_bytes = a.size * a.dtype.itemsize, (n * kdim) * b.dtype.itemsize
    j_outer = not whole_rows and nk == 1 and gj * a_bytes + b_bytes < a_bytes + gi * b_bytes
    grid = (gj, gi, nk) if j_outer else (gi, gj, nk)

    def spec(block, index):
        return pl.BlockSpec(block, (lambda j, i, k: index(i, j, k)) if j_outer else index)

    a_spec = spec((tk, tm), lambda i, j, k: (k, i)) if ta else spec((tm, tk), lambda i, j, k: (i, k))
    b_spec = spec((tn, tk), lambda i, j, k: (j, k)) if tb else spec((tk, tn), lambda i, j, k: (k, j))
    o_spec = spec((tm, tn), lambda i, j, k: (i, j))
    g_spec = spec((1, tn), lambda i, j, k: (0, 0))
    ins, in_specs = [a, b], [a_spec, b_spec]
    if a2 is not None:
        in_specs[0] = spec((tm, tk), lambda i, j, k: (i, jnp.minimum(k, nk1 - 1)))
        ins.append(a2)
        in_specs.append(spec((tm, tk), lambda i, j, k: (i, jnp.maximum(k - nk1, 0))))
    if res is not None:
        ins.append(res)
        in_specs.append(o_spec)
    out_shape, out_specs = [jax.ShapeDtypeStruct((m, n), out_dtype)], [o_spec]
    if norm_out is not None:
        ins.append(norm_out)
        in_specs.append(g_spec)
        out_shape.append(jax.ShapeDtypeStruct((m, n), BF16))
        out_specs.append(o_spec)
    if norm_bwd is not None:
        ins += [t for t in norm_bwd if t is not None]
        in_specs += [o_spec, g_spec] + ([o_spec] if norm_bwd[2] is not None else [])
        out_shape = [jax.ShapeDtypeStruct((m, n), F32), jax.ShapeDtypeStruct((1, n), F32)]
        out_specs = [o_spec, g_spec]
    result, landed = ride_call(
        job, body, name=name, grid=grid, in_specs=in_specs, out_specs=out_specs, out_shape=out_shape, ins=ins,
        scratch=[pltpu.VMEM((tm, tn), F32)],
        sem=("arbitrary" if norm_bwd is not None else "parallel", "parallel", "arbitrary"))
    result = result if whole_rows else result[0]
    return result if job is None else (result, landed)


def ffn_gate_up(h, w_gu, *, name, job=None):
    m, kdim = h.shape
    n = w_gu.shape[1] // 2
    tm, tn = min(_pick(m), 512), _pick(n)
    up_off = n // tn

    def body(a_ref, bg_ref, bu_ref, gate_ref, up_ref, act_ref):
        av = a_ref[...].astype(BF16)
        gate = _dg(av, bg_ref[...].astype(BF16), 1, 0)
        up = _dg(av, bu_ref[...].astype(BF16), 1, 0)
        gate_ref[...] = gate.astype(gate_ref.dtype)
        up_ref[...] = up.astype(up_ref.dtype)
        act_ref[...] = (jax.nn.silu(gate) * up).astype(BF16)

    o_spec = pl.BlockSpec((tm, tn), lambda j, i: (i, j))
    return ride_call(
        job, body, name=name, grid=(n // tn, m // tm),
        in_specs=[pl.BlockSpec((tm, kdim), lambda j, i: (i, 0)), pl.BlockSpec((kdim, tn), lambda j, i: (0, j)),
                  pl.BlockSpec((kdim, tn), lambda j, i: (0, j + up_off))],
        out_specs=[o_spec] * 3,
        out_shape=[jax.ShapeDtypeStruct((m, n), FFN_SAVE), jax.ShapeDtypeStruct((m, n), FFN_SAVE),
                   jax.ShapeDtypeStruct((m, n), BF16)],
        ins=[h, w_gu, w_gu], sem=("parallel", "parallel"))


def ffn_gate_up_bwd(dy, w_down, gate, up, *, alpha, name, job=None):
    m, kdim = dy.shape
    n = w_down.shape[0]
    tm, tn = min(_pick(m), 512), _pick(n)

    def body(a_ref, b_ref, gate_ref, up_ref, dgate_ref, dup_ref):
        d_act = _dg(a_ref[...].astype(BF16), b_ref[...].astype(BF16), 1, 1) * alpha
        _, pull = jax.vjp(lambda g, u: jax.nn.silu(g) * u, gate_ref[...].astype(F32), up_ref[...].astype(F32))
        d_gate, d_up = pull(d_act)
        dgate_ref[...] = d_gate.astype(BF16)
        dup_ref[...] = d_up.astype(BF16)

    o_spec = pl.BlockSpec((tm, tn), lambda j, i: (i, j))
    return ride_call(
        job, body, name=name, grid=(n // tn, m // tm),
        in_specs=[pl.BlockSpec((tm, kdim), lambda j, i: (i, 0)), pl.BlockSpec((tn, kdim), lambda j, i: (j, 0)),
                  o_spec, o_spec],
        out_specs=[o_spec] * 2, out_shape=[jax.ShapeDtypeStruct((m, n), BF16)] * 2,
        ins=[dy, w_down, gate, up], sem=("parallel", "parallel"))


def _dg(a, b, ca, cb):
    return lax.dot_general(a, b, (((ca,), (cb,)), ((), ())), preferred_element_type=F32)


@jax.custom_vjp
def bdot(a, b):
    return _dg(a.astype(BF16), b.astype(BF16), 1, 0)


def _bdot_fwd(a, b):
    ab, bb = a.astype(BF16), b.astype(BF16)
    return _dg(ab, bb, 1, 0), (ab, bb)


def _bdot_bwd(saved, g):
    ab, bb = saved
    gb = g.astype(BF16)
    return _dg(gb, bb, 1, 1), _dg(ab, gb, 0, 0)


bdot.defvjp(_bdot_fwd, _bdot_bwd)


@jax.custom_vjp
def bdot_nt(a, b):
    return _dg(a.astype(BF16), b.astype(BF16), 1, 1)


def _bdot_nt_fwd(a, b):
    ab, bb = a.astype(BF16), b.astype(BF16)
    return _dg(ab, bb, 1, 1), (ab, bb)


def _bdot_nt_bwd(saved, g):
    ab, bb = saved
    gb = g.astype(BF16)
    return _dg(gb, bb, 1, 0), _dg(gb, ab, 0, 0)


bdot_nt.defvjp(_bdot_nt_fwd, _bdot_nt_bwd)


class ColGroups:
    def __init__(self, arr, width):
        self.arr, self.width = arr, width
        self.shape, self.dtype, self.ndim = arr.shape, arr.dtype, 3


def _plain(a):
    return a.arr if isinstance(a, ColGroups) else a


def _row_spec(arr, tm):
    if isinstance(arr, ColGroups):
        return pl.BlockSpec((tm, arr.width), lambda r, g: (r, g))
    if arr.ndim == 3:
        return pl.BlockSpec((None, tm, arr.shape[2]), lambda r, g: (g, r, 0))
    return pl.BlockSpec((tm, arr.shape[1]), lambda r, g: (r, 0))


def _gparam_spec(arr):
    return pl.BlockSpec((None,) + arr.shape[1:], lambda r, g: (g, 0, 0))


def _whole_spec(arr):
    nd = arr.ndim
    return pl.BlockSpec(arr.shape, lambda r, g: (0,) * nd)


def _groups(rows, gparams):
    gs = {a.shape[1] // a.width if isinstance(a, ColGroups) else a.shape[0] for a in rows if a.ndim == 3}
    gs |= {a.shape[0] for a in gparams}
    assert len(gs) <= 1
    return gs.pop() if gs else 1


def prow(fn, rows, gparams=(), params=(), *, outs, tm, name):
    rows, gparams, params = list(rows), list(gparams), list(params)
    n_groups = _groups(rows, gparams)
    n_rows = rows[0].shape[-2]
    n_in = len(rows) + len(gparams) + len(params)

    def body(*refs):
        vals = [r[...] for r in refs[:n_in]]
        res = fn(*vals)
        for o_ref, r in zip(refs[n_in:], res, strict=True):
            o_ref[...] = r.astype(o_ref.dtype)

    out_shape, out_specs = [], []
    for width, dtype, grouped in outs:
        if grouped == 'cols':
            out_shape.append(jax.ShapeDtypeStruct((n_rows, n_groups * width), dtype))
            out_specs.append(_row_spec(ColGroups(out_shape[-1], width), tm))
            continue
        shp = (n_groups, n_rows, width) if grouped else (n_rows, width)
        out_shape.append(jax.ShapeDtypeStruct(shp, dtype))
        out_specs.append(_row_spec(out_shape[-1], tm))
    return pl.pallas_call(
        body, name=name, grid=(n_rows // tm, n_groups),
        in_specs=[_row_spec(a, tm) for a in rows] + [_gparam_spec(a) for a in gparams] + [_whole_spec(a) for a in params],
        out_specs=out_specs, out_shape=out_shape,
        compiler_params=_cparams(("parallel", "arbitrary")),
    )(*[_plain(a) for a in rows], *gparams, *params)


def prow_vjp(fn, rows, gparams=(), params=(), *, cts, row_grad, adds=None, row_dtypes=None, gparam_grad=None,
             param_grad=None, tm, name):
    rows, gparams, params, cts = list(rows), list(gparams), list(params), list(cts)
    gparam_grad = list(gparam_grad) if gparam_grad is not None else [True] * len(gparams)
    param_grad = list(param_grad) if param_grad is not None else [True] * len(params)
    n_groups = _groups(rows + cts, gparams)
    n_rows = rows[0].shape[-2]
    want_rows = [i for i, w in enumerate(row_grad) if w]
    adds = list(adds) if adds is not None else [None] * len(want_rows)
    row_dtypes = list(row_dtypes) if row_dtypes is not None else [F32] * len(want_rows)
    add_arrays = [a for a in adds if a is not None]
    n_r, n_g, n_p, n_c, n_a = len(rows), len(gparams), len(params), len(cts), len(add_arrays)
    mask = list(row_grad) + gparam_grad + param_grad

    def body(*refs):
        r_id, g_id = pl.program_id(0), pl.program_id(1)
        n_in = n_r + n_g + n_p
        vals = [r[...] for r in refs[:n_in]]
        ct_vals = tuple(r[...].astype(F32) for r in refs[n_in:n_in + n_c])
        add_refs = list(refs[n_in + n_c:n_in + n_c + n_a])
        out_refs = list(refs[n_in + n_c + n_a:])
        diff_idx = [i for i, w in enumerate(mask) if w]

        def wrapped(*diff):
            full = list(vals)
            for i, d in zip(diff_idx, diff):
                full[i] = d
            return tuple(fn(*full))

        _, pull = jax.vjp(wrapped, *[vals[i].astype(F32) for i in diff_idx])
        grads = dict(zip(diff_idx, pull(ct_vals)))
        k = 0
        for j, i in enumerate(want_rows):
            o_ref = out_refs[k]
            k += 1
            gval = grads[i]
            if adds[j] is not None:
                gval = gval + add_refs.pop(0)[...].astype(F32)
            if rows[i].ndim == 2 and n_groups > 1:
                @pl.when(g_id == 0)
                def _(o_ref=o_ref, gval=gval):
                    o_ref[...] = gval.astype(o_ref.dtype)

                @pl.when(g_id != 0)
                def _(o_ref=o_ref, gval=gval):
                    o_ref[...] += gval.astype(o_ref.dtype)
            else:
                o_ref[...] = gval.astype(o_ref.dtype)
        for i in range(n_g):
            if not gparam_grad[i]:
                continue
            o_ref = out_refs[k]
            k += 1
            gval = grads[n_r + i]

            @pl.when(r_id == 0)
            def _(o_ref=o_ref, gval=gval):
                o_ref[g_id] = gval

            @pl.when(r_id != 0)
            def _(o_ref=o_ref, gval=gval):
                o_ref[g_id] += gval
        for i in range(n_p):
            if not param_grad[i]:
                continue
            o_ref = out_refs[k]
            k += 1
            gval = grads[n_r + n_g + i]
            first = jnp.logical_and(r_id == 0, g_id == 0)

            @pl.when(first)
            def _(o_ref=o_ref, gval=gval):
                o_ref[...] = gval

            @pl.when(jnp.logical_not(first))
            def _(o_ref=o_ref, gval=gval):
                o_ref[...] += gval

    out_shape, out_specs = [], []
    for j, i in enumerate(want_rows):
        out_shape.append(jax.ShapeDtypeStruct(rows[i].shape, row_dtypes[j]))
        out_specs.append(_row_spec(rows[i], tm))
    for i in range(n_g):
        if gparam_grad[i]:
            out_shape.append(jax.ShapeDtypeStruct(gparams[i].shape, F32))
            out_specs.append(_whole_spec(gparams[i]))
    for i in range(n_p):
        if param_grad[i]:
            out_shape.append(jax.ShapeDtypeStruct(params[i].shape, F32))
            out_specs.append(_whole_spec(params[i]))
    return pl.pallas_call(
        body, name=name, grid=(n_rows // tm, n_groups),
        in_specs=([_row_spec(a, tm) for a in rows] + [_gparam_spec(a) for a in gparams]
                  + [_whole_spec(a) for a in params] + [_row_spec(a, tm) for a in cts]
                  + [_row_spec(a, tm) for a in add_arrays]),
        out_specs=out_specs, out_shape=out_shape,
        compiler_params=_cparams(("arbitrary", "arbitrary")),
    )(*[_plain(a) for a in rows], *gparams, *params, *[_plain(a) for a in cts], *add_arrays)


def f_rms(x, g):
    return (_rms(x.astype(F32), g),)


def f_gate_prep(z, ln_g, ln_b):
    act = jax.nn.gelu(z)
    u, gg = act[:, :SG_WIDTH], act[:, SG_WIDTH:]
    mu = jnp.mean(gg, axis=-1, keepdims=True)
    var = jnp.mean(jnp.square(gg - mu), axis=-1, keepdims=True)
    return u, (gg - mu) * lax.rsqrt(var + EPS) * ln_g + ln_b


def f_spatial_gate(gn, u, w, b):
    t = lax.broadcasted_iota(jnp.int32, w.shape, 0)
    s = lax.broadcasted_iota(jnp.int32, w.shape, 1)
    w_causal = jnp.where(s <= t, w, 0.0)
    mixed = [bdot(w_causal, gn[i:i + SG_CHUNK]) + b for i in range(0, gn.shape[0], SG_CHUNK)]
    return (u * (mixed[0] if len(mixed) == 1 else jnp.concatenate(mixed, axis=0)),)


def _two_pieces(x):
    hi = x.astype(BF16)
    return hi, (x - hi.astype(F32)).astype(BF16)


@jax.custom_vjp
def place(x, m):
    hi, lo = _two_pieces(x)
    return _dg(hi, m, 1, 0) + _dg(lo, m, 1, 0)


def _place_fwd(x, m):
    return place(x, m), m


def _place_bwd(m, g):
    hi, lo = _two_pieces(g)
    return _dg(hi, m, 1, 1) + _dg(lo, m, 1, 1), jnp.zeros_like(m)


place.defvjp(_place_fwd, _place_bwd)


def _lane_map(rows, cols, entry):
    src = lax.broadcasted_iota(jnp.int32, (rows, cols), 0)
    dst = lax.broadcasted_iota(jnp.int32, (rows, cols), 1)
    return entry(src, dst).astype(BF16)


def _rope_tail(t, cos_w, sin_w):
    half = MLA_ROPE // 2
    lo_half = lambda d: jnp.logical_and(d >= MLA_NOPE, d < MLA_NOPE + half)
    swap = _lane_map(MLA_QK, MLA_QK, lambda s, d: jnp.where(
        jnp.logical_and(d >= MLA_NOPE + half, s == d - half), 1.0,
        jnp.where(jnp.logical_and(lo_half(d), s == d + half), -1.0, 0.0)))
    return t * cos_w + place(t, swap) * sin_w


def f_mla_q(q, cos_w, sin_w, g):
    return (_rope_tail(f_rms(q, g)[0], cos_w, sin_w),)


def f_mla_k(k_nope, k_r, cos_w, sin_w, g):
    side_by_side = (place(k_nope, _lane_map(MLA_NOPE, MLA_QK, lambda s, d: jnp.where(s == d, 1.0, 0.0)))
                    + place(k_r, _lane_map(MLA_ROPE, MLA_QK, lambda s, d: jnp.where(s + MLA_NOPE == d, 1.0, 0.0))))
    return (_rope_tail(f_rms(side_by_side, g)[0], cos_w, sin_w),)


def f_xattn(q, k, v, q_g, k_g):
    qn, kn = f_rms(q, q_g)[0], f_rms(k, k_g)[0]
    sc = bdot_nt(qn, kn) * (MEM_HEAD_DIM ** -0.5)
    return (bdot(jax.nn.softmax(sc, axis=-1), v),)


def _split_dot(x, tri, pieces=2):
    hi = x.astype(BF16)
    if pieces == 1:
        return _dg(hi, tri, 1, 0)
    lo = (x - hi.astype(F32)).astype(BF16)
    return _dg(hi, tri, 1, 0) + _dg(lo, tri, 1, 0)


def _tri(tk, cmp):
    j = lax.broadcasted_iota(jnp.int32, (tk, tk), 0)
    s = lax.broadcasted_iota(jnp.int32, (tk, tk), 1)
    return cmp(j, s).astype(BF16)


SCAN_CHUNK = 256


def _row_scan(x, tri, reverse, pieces=2):
    n = x.shape[1] // SCAN_CHUNK
    chunks = [x[:, i * SCAN_CHUNK:(i + 1) * SCAN_CHUNK] for i in range(n)]
    out, seen = [None] * n, None
    for i in (reversed(range(n)) if reverse else range(n)):
        local = _split_dot(chunks[i], tri, pieces)
        out[i] = local if seen is None else local + seen
        total = jnp.sum(chunks[i], axis=1, keepdims=True)
        seen = total if seen is None else seen + total
    return (out[0] if n == 1 else jnp.concatenate(out, axis=1)), seen


def _att_specs(s_len, tq, dq, dv):
    q_spec = pl.BlockSpec((None, tq, dq), lambda h, i: (h, i, 0))
    k_spec = pl.BlockSpec((None, s_len, dq), lambda h, i: (h, 0, 0))
    v_spec = pl.BlockSpec((None, s_len, dv), lambda h, i: (h, 0, 0))
    o_spec = pl.BlockSpec((None, tq, dv), lambda h, i: (h, i, 0))
    r_spec = pl.BlockSpec((None, tq, 1), lambda h, i: (h, i, 0))
    return q_spec, k_spec, v_spec, o_spec, r_spec


def _key_blocks(qi, tq, tk):
    return (qi * tq) // tk, ((qi + 1) * tq + tk - 1) // tk


def _keep(qi, j, tq, tk, strict):
    row = qi * tq + lax.broadcasted_iota(jnp.int32, (tq, tk), 0)
    col = j * tk + lax.broadcasted_iota(jnp.int32, (tq, tk), 1)
    return col < row if strict else col <= row


def _log_sigmoid(z):
    return jnp.minimum(z, 0.0) - jnp.log(1.0 + jnp.exp(-jnp.abs(z)))


def sb_fwd(q, k, v, *, tq, tk, name, job=None):
    n_heads, s_len, d = q.shape
    scale = SB_HEAD_DIM ** -0.5

    def body(q_ref, k_ref, v_ref, o_ref, tot_ref):
        qi = pl.program_id(1)
        qv = q_ref[...]
        upper = _tri(SCAN_CHUNK, lambda j, s: j > s)
        n_full, n_all = _key_blocks(qi, tq, tk)

        def make_step(masked, last):
            def step(jj, carry):
                acc, rest = carry
                j = last - 1 - jj
                sl = pl.ds(pl.multiple_of(j * tk, tk), tk)
                ks, vs = k_ref[sl, :], v_ref[sl, :]
                z = _dg(qv, ks, 1, 1) * scale
                log_beta = _log_sigmoid(z)
                log_stay = log_beta - z
                if masked:
                    valid = _keep(qi, j, tq, tk, True)
                    log_stay = jnp.where(valid, log_stay, 0.0)
                after, total = _row_scan(log_stay, upper, True)
                w = jnp.exp(log_beta + after + rest)
                if masked:
                    w = jnp.where(valid, w, 0.0)
                acc = acc + _dg(w.astype(BF16), vs, 1, 0)
                return acc, rest + total
            return step

        carry = (jnp.zeros((tq, d), F32), jnp.zeros((tq, 1), F32))
        carry = lax.fori_loop(0, n_all - n_full, make_step(True, n_all), carry)
        acc, rest = lax.fori_loop(0, n_full, make_step(False, n_full), carry)
        o_ref[...] = acc
        tot_ref[...] = rest

    q_spec, k_spec, v_spec, o_spec, r_spec = _att_specs(s_len, tq, d, d)
    return ride_call(
        job, body, name=name, grid=(n_heads, s_len // tq), in_specs=[q_spec, k_spec, v_spec],
        out_specs=[o_spec, r_spec],
        out_shape=[jax.ShapeDtypeStruct((n_heads, s_len, d), F32), jax.ShapeDtypeStruct((n_heads, s_len, 1), F32)],
        ins=[q, k, v], sem=("parallel", "arbitrary"))


def sb_bwd(q, k, v, tot, do, *, tq, tk, name, job=None):
    n_heads, s_len, d = q.shape
    scale = SB_HEAD_DIM ** -0.5

    def body(q_ref, k_ref, v_ref, tot_ref, do_ref, dq_ref, dk_ref, dv_ref):
        qi = pl.program_id(1)

        @pl.when(qi == 0)
        def _():
            dk_ref[...] = jnp.zeros_like(dk_ref)
            dv_ref[...] = jnp.zeros_like(dv_ref)

        qv = q_ref[...]
        dob = do_ref[...].astype(BF16)
        total = tot_ref[...]
        incl = _tri(SCAN_CHUNK, lambda j, s: j <= s)
        excl = _tri(SCAN_CHUNK, lambda j, s: j < s)
        n_full, n_all = _key_blocks(qi, tq, tk)

        def make_step(masked):
            def step(j, carry):
                dq, stay_before, dl_before = carry
                sl = pl.ds(pl.multiple_of(j * tk, tk), tk)
                ks, vs = k_ref[sl, :], v_ref[sl, :]
                z = _dg(qv, ks, 1, 1) * scale
                log_beta = _log_sigmoid(z)
                log_stay = log_beta - z
                if masked:
                    valid = _keep(qi, j, tq, tk, True)
                    log_stay = jnp.where(valid, log_stay, 0.0)
                stay_upto, stay_sum = _row_scan(log_stay, incl, False)
                w = jnp.exp(log_beta + (total - stay_before) - stay_upto)
                if masked:
                    w = jnp.where(valid, w, 0.0)
                dl = _dg(dob, vs, 1, 1) * w
                dl_upto, dl_sum = _row_scan(dl, excl, False, pieces=1)
                dl_prefix = dl_upto + dl_before
                beta = jnp.exp(log_beta)
                dz = (dl * (1.0 - beta) - beta * dl_prefix) * scale
                if masked:
                    dz = jnp.where(valid, dz, 0.0)
                dzb = dz.astype(BF16)
                dq = dq + _dg(dzb, ks, 1, 0)
                dk_ref[sl, :] += _dg(dzb, qv, 0, 0)
                dv_ref[sl, :] += _dg(w.astype(BF16), dob, 0, 0)
                return dq, stay_before + stay_sum, dl_before + dl_sum
            return step

        zero = jnp.zeros((tq, 1), F32)
        carry = lax.fori_loop(0, n_full, make_step(False), (jnp.zeros((tq, d), F32), zero, zero))
        dq, _, _ = lax.fori_loop(n_full, n_all, make_step(True), carry)
        dq_ref[...] = dq

    q_spec, k_spec, v_spec, o_spec, r_spec = _att_specs(s_len, tq, d, d)
    shp = jax.ShapeDtypeStruct((n_heads, s_len, d), F32)
    return ride_call(
        job, body, name=name, grid=(n_heads, s_len // tq), in_specs=[q_spec, k_spec, v_spec, r_spec, o_spec],
        out_specs=[q_spec, k_spec, v_spec], out_shape=[shp, shp, shp], ins=[q, k, v, tot, do],
        sem=("arbitrary", "arbitrary"))


NEG_BIG = -1e30


def _lower_left(rows, cols):
    r = lax.broadcasted_iota(jnp.int32, (rows, cols), 0)
    c = lax.broadcasted_iota(jnp.int32, (rows, cols), 1)
    return c <= r


def _prep_specs(tq, q_prep):
    cos, _, gain = q_prep
    rope_spec = pl.BlockSpec((tq, cos.shape[1]), lambda h, i: (i, 0))
    return [rope_spec, rope_spec, pl.BlockSpec(gain.shape, lambda h, i: (0, 0))]


def sm_fwd(q, k, v, *, tq, tk, name, q_prep=None):
    n_heads, s_len, dq = q.shape
    dv = v.shape[2]
    scale = dq ** -0.5
    assert tq == tk
    half = tk // 2
    n_prep = 0 if q_prep is None else 3

    def body(*refs):
        q_ref, prep_refs = refs[0], refs[1:1 + n_prep]
        k_ref, v_ref, o_ref, lse_ref = refs[1 + n_prep:]
        qi = pl.program_id(1)
        qv = q_ref[...]
        if q_prep is not None:
            qv = f_mla_q(qv, *[r[...] for r in prep_refs])[0].astype(BF16)

        def attend(carry, q_rows, keys, keep):
            acc, m, l = carry
            sc = _dg(q_rows, k_ref[keys, :], 1, 1) * scale
            if keep is not None:
                sc = jnp.where(keep, sc, NEG_BIG)
            m_new = jnp.maximum(m, jnp.max(sc, axis=1, keepdims=True))
            p = jnp.exp(sc - m_new)
            fade = jnp.exp(m - m_new)
            return (fade * acc + _dg(p.astype(BF16), v_ref[keys, :], 1, 0), m_new,
                    fade * l + jnp.sum(p, axis=1, keepdims=True))

        carry = (jnp.zeros((tq, dv), F32), jnp.full((tq, 1), NEG_BIG, F32), jnp.zeros((tq, 1), F32))
        carry = lax.fori_loop(
            0, qi, lambda j, c: attend(c, qv, pl.ds(pl.multiple_of(j * tk, tk), tk), None), carry)
        base = pl.multiple_of(qi * tk, tk)
        carry = attend(carry, qv, pl.ds(base, half), _lower_left(tq, half))
        low = attend(tuple(t[half:] for t in carry), qv[half:], pl.ds(pl.multiple_of(base + half, half), half),
                     _lower_left(half, half))
        acc, m, l = (jnp.concatenate([t[:half], u], axis=0) for t, u in zip(carry, low))
        o_ref[...] = acc / l
        lse_ref[...] = m + jnp.log(l)

    q_spec, k_spec, v_spec, o_spec, r_spec = _att_specs(s_len, tq, dq, dv)
    prep = [] if q_prep is None else list(q_prep)
    return pl.pallas_call(
        body, name=name, grid=(n_heads, s_len // tq),
        in_specs=[q_spec] + ([] if q_prep is None else _prep_specs(tq, q_prep)) + [k_spec, v_spec],
        out_specs=[o_spec, r_spec],
        out_shape=[jax.ShapeDtypeStruct((n_heads, s_len, dv), F32), jax.ShapeDtypeStruct((n_heads, s_len, 1), F32)],
        compiler_params=_cparams(("parallel", "arbitrary")),
    )(q, *prep, k, v)


def sm_bwd(q, k, v, o, lse, do, *, tq, tk, name, q_prep=None, job=None):
    n_heads, s_len, dq = q.shape
    dv = v.shape[2]
    scale = dq ** -0.5
    assert tq == tk
    half = tk // 2
    n_prep = 0 if q_prep is None else 3

    def body(*refs):
        q_ref, prep_refs = refs[0], refs[1:1 + n_prep]
        k_ref, v_ref, o_ref, lse_ref, do_ref, dq_ref, dk_ref, dv_ref = refs[1 + n_prep:9 + n_prep]
        head, qi = pl.program_id(0), pl.program_id(1)

        @pl.when(qi == 0)
        def _():
            dk_ref[...] = jnp.zeros_like(dk_ref)
            dv_ref[...] = jnp.zeros_like(dv_ref)

        q_raw = q_ref[...]
        prep_vals = [r[...] for r in prep_refs]
        qv = q_raw if q_prep is None else f_mla_q(q_raw, *prep_vals)[0].astype(BF16)
        do = do_ref[...]
        dob = do.astype(BF16)
        delta = jnp.sum(do * o_ref[...], axis=1, keepdims=True)
        lse_v = lse_ref[...]

        def attend(rows, keys, keep):
            ks, vs = k_ref[keys, :], v_ref[keys, :]
            p = jnp.exp(_dg(qv[rows], ks, 1, 1) * scale - lse_v[rows])
            if keep is not None:
                p = jnp.where(keep, p, 0.0)
            dv_ref[keys, :] += _dg(p.astype(BF16), dob[rows], 0, 0)
            ds = (p * (_dg(dob[rows], vs, 1, 1) - delta[rows]) * scale).astype(BF16)
            dk_ref[keys, :] += _dg(ds, qv[rows], 0, 0)
            return _dg(ds, ks, 1, 0)

        everything = slice(None)
        dq_acc = lax.fori_loop(
            0, qi, lambda j, acc: acc + attend(everything, pl.ds(pl.multiple_of(j * tk, tk), tk), None),
            jnp.zeros((tq, dq), F32))
        base = pl.multiple_of(qi * tk, tk)
        dq_acc = dq_acc + attend(everything, pl.ds(base, half), _lower_left(tq, half))
        low = attend(slice(half, None), pl.ds(pl.multiple_of(base + half, half), half), _lower_left(half, half))
        dq_acc = jnp.concatenate([dq_acc[:half], dq_acc[half:] + low], axis=0)
        if q_prep is None:
            dq_ref[...] = dq_acc
        else:
            cos, sin, gain = prep_vals
            _, pull = jax.vjp(lambda t, g: f_mla_q(t, cos, sin, g)[0], q_raw, gain)
            dq_raw, d_gain = pull(dq_acc)
            dq_ref[...] = dq_raw.astype(dq_ref.dtype)
            dgain_ref = refs[9 + n_prep]
            first = jnp.logical_and(head == 0, qi == 0)

            @pl.when(first)
            def _():
                dgain_ref[...] = d_gain

            @pl.when(jnp.logical_not(first))
            def _():
                dgain_ref[...] += d_gain

    q_spec, k_spec, v_spec, o_spec, r_spec = _att_specs(s_len, tq, dq, dv)
    out_specs = [q_spec, k_spec, v_spec]
    out_shape = [jax.ShapeDtypeStruct((n_heads, s_len, dq), F32 if q_prep is None else BF16),
                 jax.ShapeDtypeStruct((n_heads, s_len, dq), F32), jax.ShapeDtypeStruct((n_heads, s_len, dv), F32)]
    prep, prep_specs = [], []
    if q_prep is not None:
        prep, prep_specs = list(q_prep), _prep_specs(tq, q_prep)
        out_specs.append(prep_specs[2])
        out_shape.append(jax.ShapeDtypeStruct(q_prep[2].shape, F32))
    return ride_call(
        job, body, name=name, grid=(n_heads, s_len // tq),
        in_specs=[q_spec] + prep_specs + [k_spec, v_spec, o_spec, r_spec, o_spec], out_specs=out_specs,
        out_shape=out_shape, ins=[q, *prep, k, v, o, lse, do], sem=("arbitrary", "arbitrary"))


def loss_head(y, target, *, tm, name):
    n_rows, width = y.shape

    def body(y_ref, t_ref, dy_ref, loss_ref):
        diff = y_ref[...] - t_ref[...]
        dy_ref[...] = diff / width
        part = 0.5 * jnp.sum(jnp.mean(diff * diff, axis=-1, keepdims=True), axis=0, keepdims=True)

        @pl.when(pl.program_id(0) == 0)
        def _():
            loss_ref[...] = jnp.zeros_like(loss_ref)

        loss_ref[...] += jnp.broadcast_to(part, loss_ref.shape)

    spec = pl.BlockSpec((tm, width), lambda r: (r, 0))
    dy, loss = pl.pallas_call(
        body, name=name, grid=(n_rows // tm,), in_specs=[spec, spec],
        out_specs=[spec, pl.BlockSpec((8, LANES), lambda r: (0, 0))],
        out_shape=[jax.ShapeDtypeStruct(y.shape, F32), jax.ShapeDtypeStruct((8, LANES), F32)],
        compiler_params=_cparams(("arbitrary",)),
    )(y, target)
    return dy, loss[0, 0]


ADAM_TILE_ELEMS = 256 * 1024


def _adam_rows(n_rows, width):
    fits = [t for t in range(16, n_rows + 1, 16) if n_rows % t == 0 and t * width <= ADAM_TILE_ELEMS]
    return max(fits) if fits else n_rows


def adamw(parts, w, m, v, *, name):
    n_layers, n_rows, width = w.shape
    assert len(parts) == n_layers
    tm = _adam_rows(n_rows, width)
    n_tiles = n_rows // tm

    def body(*refs):
        p_refs = refs[:n_layers]
        w_ref, m_ref, v_ref, g_ref, d_ref, nm_ref, nv_ref = refs[n_layers:]
        layer = pl.program_id(0)
        for this, p_ref in enumerate(p_refs):
            @pl.when(layer == this)
            def _(p_ref=p_ref):
                g = p_ref[0].astype(F32)
                for i in range(1, N_DEV):
                    g = g + p_ref[i].astype(F32)
                m_new = ADAM_B1 * m_ref[...] + (1.0 - ADAM_B1) * g
                v_new = ADAM_B2 * v_ref[...] + (1.0 - ADAM_B2) * jnp.square(g)
                m_hat = m_new / (1.0 - ADAM_B1 ** ADAM_STEP)
                v_hat = v_new / (1.0 - ADAM_B2 ** ADAM_STEP)
                g_ref[...] = g
                d_ref[...] = -ADAM_LR * (m_hat / (jnp.sqrt(v_hat) + ADAM_EPS) + ADAM_WD * w_ref[...])
                nm_ref[...] = m_new
                nv_ref[...] = v_new

    def part_spec(this):
        def index(layer, r):
            return 0, jnp.where(layer == this, r, jnp.where(layer < this, 0, n_tiles - 1)), 0
        return pl.BlockSpec((N_DEV, tm, width), index)

    spec = pl.BlockSpec((None, tm, width), lambda layer, r: (layer, r, 0))
    shp = jax.ShapeDtypeStruct(w.shape, F32)
    return pl.pallas_call(
        body, name=name, grid=(n_layers, n_tiles),
        in_specs=[part_spec(this) for this in range(n_layers)] + [spec, spec, spec],
        out_specs=[spec] * 4, out_shape=[shp] * 4, compiler_params=_cparams(("arbitrary", "arbitrary")),
    )(*parts, w, m, v)


def _me():
    return lax.axis_index("x"), lax.axis_index("y"), lax.axis_index("c")


N_PEERS = N_DEV - 1


class CommJob:
    def __init__(self, kind, arrays):
        self.kind, self.arrays, self.n = kind, list(arrays), len(arrays)

    def out_shape(self):
        lead = (N_DEV,) if self.kind == 'gather' else ()
        return [jax.ShapeDtypeStruct(lead + a.shape, a.dtype) for a in self.arrays]

    def scratch(self):
        return [pltpu.SemaphoreType.DMA((N_PEERS * self.n,)), pltpu.SemaphoreType.DMA((N_PEERS * self.n,)),
                pltpu.SemaphoreType.DMA((self.n,))]

    def phases(self, in_refs, out_refs, send_sems, recv_sems, local_sems):
        n = self.n
        x, y, c = _me()

        def remote(i, k, src, dst, to):
            return pltpu.make_async_remote_copy(
                src_ref=src, dst_ref=dst, send_sem=send_sems.at[N_PEERS * i + k],
                recv_sem=recv_sems.at[N_PEERS * i + k], device_id=to, device_id_type=MESH)

        if self.kind == 'gather':
            me, sibling = (x, y, c), (x, y, 1 - c)
            chips = [(1 - x, y), (x, 1 - y), (1 - x, 1 - y)]

            def slot(i, px, py, pc):
                return out_refs[i].at[4 * px + 2 * py + pc]

            def copy(i, k, blk, to, src=None):
                return remote(i, k, slot(i, *blk) if src is None else src, slot(i, *blk), to)

            def mine():
                return [pltpu.make_async_copy(in_refs[i], slot(i, *me), local_sems.at[i]) for i in range(n)]

            def first():
                cps = []
                for i in range(n):
                    cps.append(copy(i, 0, me, sibling, src=in_refs[i]))
                    cps += [copy(i, 1 + j, me, (*chip, c), src=in_refs[i]) for j, chip in enumerate(chips)]
                return cps

            def passed():
                return [copy(i, 4 + j, (*chip, c), sibling) for j, chip in enumerate(chips) for i in range(n)]

            def start():
                for cp in mine() + first():
                    cp.start()

            def forward():
                for j, chip in enumerate(chips):
                    for i in range(n):
                        copy(i, 1 + j, (*chip, c), me).wait_recv()
                        copy(i, 4 + j, (*chip, c), sibling).start()

            def finish():
                for i in range(n):
                    copy(i, 0, sibling, me).wait_recv()
                    for j, chip in enumerate(chips):
                        copy(i, 4 + j, (*chip, 1 - c), me).wait_recv()
                for cp in first() + passed():
                    cp.wait_send()
                for cp in mine():
                    cp.wait()

            return start, forward, finish

        my_slot = 4 * x + 2 * y + c

        def mine():
            return [pltpu.make_async_copy(in_refs[i].at[my_slot], out_refs[i].at[my_slot], local_sems.at[i])
                    for i in range(n)]

        def copies():
            cps = []
            for k in range(1, N_DEV):
                px, py, pc = x ^ (k >> 2), y ^ ((k >> 1) & 1), c ^ (k & 1)
                cps += [remote(i, k - 1, in_refs[i].at[4 * px + 2 * py + pc], out_refs[i].at[my_slot], (px, py, pc))
                        for i in range(n)]
            return cps

        def start():
            for cp in mine() + copies():
                cp.start()

        def finish():
            for cp in copies():
                cp.wait_recv()
            for cp in copies():
                cp.wait_send()
            for cp in mine():
                cp.wait()

        return start, (lambda: None), finish


def comm_call(kind, arrays, *, name):
    job = CommJob(kind, arrays)
    n = job.n

    def body(*refs):
        start, forward, finish = job.phases(refs[:n], refs[n:2 * n], *refs[2 * n:])
        start()
        forward()
        finish()

    hbm = pl.BlockSpec(memory_space=pl.ANY)
    return pl.pallas_call(body, name=name, out_shape=job.out_shape(), in_specs=[hbm] * n, out_specs=[hbm] * n,
                          scratch_shapes=job.scratch())(*job.arrays)


def ride_call(job, compute, *, name, grid, in_specs, out_specs, out_shape, ins, sem, scratch=()):
    scratch = list(scratch)
    if job is None:
        return pl.pallas_call(compute, name=name, grid=grid, in_specs=in_specs, out_specs=out_specs,
                              out_shape=out_shape, scratch_shapes=scratch, compiler_params=_cparams(sem))(*ins), None
    n, n_in, n_out, n_scr = job.n, len(ins), len(out_shape), len(scratch)
    n_steps = 1
    for g in grid:
        n_steps *= g

    def body(*refs):
        ins_, job_ins = refs[:n_in], refs[n_in:n_in + n]
        outs, job_outs = refs[n_in + n:n_in + n + n_out], refs[n_in + n + n_out:n_in + 2 * n + n_out]
        rest = refs[n_in + 2 * n + n_out:]
        start, forward, finish = job.phases(job_ins, job_outs, *rest[n_scr:])
        now = 0
        for axis, g in enumerate(grid):
            now = now * g + pl.program_id(axis)
        pl.when(now == 0)(start)
        pl.when(now == n_steps // 2)(forward)
        compute(*ins_, *outs, *rest[:n_scr])
        pl.when(now == n_steps - 1)(finish)

    hbm = pl.BlockSpec(memory_space=pl.ANY)
    res = pl.pallas_call(
        body, name=name, grid=grid, in_specs=list(in_specs) + [hbm] * n,
        out_specs=list(out_specs) + [hbm] * n, out_shape=list(out_shape) + job.out_shape(),
        scratch_shapes=scratch + job.scratch(), compiler_params=_cparams(("arbitrary",) * len(grid)),
    )(*ins, *job.arrays)
    return res[:n_out], res[n_out:]


def to_heads(t, n_heads):
    s_len = t.shape[0]
    return t.reshape(s_len, n_heads, -1).transpose(1, 0, 2)


def from_heads(t):
    return t.transpose(1, 0, 2).reshape(t.shape[1], -1)


def gathered_to_full(t, axis):
    shp = t.shape[1:]
    return jnp.moveaxis(t, 0, axis).reshape(shp[:axis] + (N_DEV * shp[axis],) + shp[axis + 1:])


def full_to_owner_major(g, axis):
    shp = g.shape
    t = jnp.moveaxis(g.reshape(shp[:axis] + (N_DEV, shp[axis] // N_DEV) + shp[axis + 1:]), axis, 0)
    return t.reshape(N_DEV, -1, t.shape[-1])


def _small_rows(shape):
    n = 1
    for s in shape:
        n *= s
    return -(-n // LANES)


def pack_small(arrs, shapes):
    pieces = []
    for n in SMALL:
        flat = arrs[n].reshape(-1)
        flat = jnp.pad(flat, (0, _small_rows(shapes[n]) * LANES - flat.shape[0]))
        pieces.append(flat.reshape(-1, LANES))
    flat = jnp.concatenate(pieces, axis=0)
    return jnp.pad(flat, ((0, -flat.shape[0] % SMALL_ROW_MULTIPLE), (0, 0)))


def unpack_small(flat, shapes):
    out, r = {}, 0
    for n in SMALL:
        rows = _small_rows(shapes[n])
        size = 1
        for s in shapes[n]:
            size *= s
        out[n] = flat[r:r + rows].reshape(-1)[:size].reshape(shapes[n])
        r += rows
    return out


ROW_TM = 256
XATT_TM = 1024
HEAD_TM = 1024
SG_TM = 8 * SG_CHUNK
SB_TILES = (512, 512)
SM_TILE = 1024


def _norm_fwd(x, g, name):
    return prow(f_rms, [x], params=[g.reshape(1, -1)], outs=[(x.shape[1], BF16, False)], tm=ROW_TM, name=name)[0]


def _norm_bwd(x, g, dh, add, name, want_row=True):
    res = prow_vjp(f_rms, [x], params=[g.reshape(1, -1)], cts=[dh], row_grad=[want_row],
                   adds=[add] if want_row else None, tm=ROW_TM, name=name)
    return (res[0], res[1].reshape(-1)) if want_row else (None, res[0].reshape(-1))


def _out_proj(a, w, x, next_gain, alpha, name):
    if next_gain is None:
        return pmm(a, w, res=x, alpha=alpha, name=name), None
    return pmm(a, w, res=x, alpha=alpha, norm_out=next_gain.reshape(1, -1), name=name)


def _in_proj_bwd(d, w, x, gain, dy, name, **kw):
    dx, g_gain = pmm(d, w, tb=True, norm_bwd=(x, gain.reshape(1, -1), dy), name=name, **kw)
    return dx, g_gain.reshape(-1)


def ffn_fwd(x, h, p, tag, next_gain, job=None, after_job=None):
    (gate, up, act), landed = ffn_gate_up(h, p['w_gu'], name=f"{tag}_gu", job=job)
    if job is not None:
        after_job(landed)
    out = _out_proj(act, p['w_down'], x, next_gain, 0.5, f"{tag}_down")
    return out, (x, h, gate, up, act)


def _no_rider(run, **own):
    return run(None)[0]


def _pmm_pair(*args, job, **kw):
    out = pmm(*args, job=job, **kw)
    return out if job is not None else (out, None)


def ffn_bwd(dy, p, saved, tag, with_job=_no_rider):
    x, h, gate, up, act = saved
    d_gate, d_up = with_job(lambda job: ffn_gate_up_bwd(dy, p['w_down'], gate, up, alpha=0.5, name=f"{tag}_dact",
                                                        job=job))
    g_down = pmm(act, dy, ta=True, out_dtype=GRAD_WIRE, alpha=0.5, name=f"{tag}_gdown")
    g_gate = with_job(lambda job: _pmm_pair(h, d_gate, ta=True, out_dtype=GRAD_WIRE, name=f"{tag}_ggate", job=job),
                      w_down=g_down)
    g_gu = jnp.concatenate([g_gate, pmm(h, d_up, ta=True, out_dtype=GRAD_WIRE, name=f"{tag}_gup")], axis=1)
    dx, g_norm = with_job(
        lambda job: _pmm_pair(d_gate, p['w_gu'], a2=d_up, tb=True, norm_bwd=(x, p['norm'].reshape(1, -1), dy),
                              name=f"{tag}_dh", job=job), w_gu=g_gu)
    return dx, {'norm': g_norm.reshape(-1), 'w_gu': g_gu, 'w_down': g_down}


def even_mixer_fwd(x, h, p, next_gain, job=None, after_job=None):
    proj = pmm(h, p['w_in'], name="sbg_in")
    q, k, v = (to_heads(proj[:, i * SB_WIDTH:(i + 1) * SB_WIDTH], SB_HEADS).astype(BF16) for i in range(3))
    (o_sb, tot), landed = sb_fwd(q, k, v, tq=SB_TILES[0], tk=SB_TILES[1], name="sb_fwd", job=job)
    if job is not None:
        after_job(landed)
    z = proj[:, 3 * SB_WIDTH:]
    ln_g, ln_b = p['ln_gain'].reshape(1, -1), p['ln_bias'].reshape(1, -1)
    u, gn = prow(f_gate_prep, [z], params=[ln_g, ln_b], outs=[(SG_WIDTH, F32, False)] * 2, tm=ROW_TM,
                 name="sgu_prep")
    gn_g, u_g = to_heads(gn, SG_GROUPS), to_heads(u, SG_GROUPS)
    b3 = p['sgu_b'].reshape(SG_GROUPS, SG_CHUNK, 1)
    o_sg = prow(f_spatial_gate, [gn_g, u_g], gparams=[p['sgu_w'], b3], outs=[(SG_GROUP_DIM, F32, True)],
                tm=SG_TM, name="sgu_mix")[0]
    cat = jnp.concatenate([from_heads(o_sb), from_heads(o_sg)], axis=-1).astype(BF16)
    out = _out_proj(cat, p['w_out'], x, next_gain, 1.0, "sbg_out")
    return out, (x, h, q, k, v, tot, z, gn_g, u_g, b3, cat)


def even_mixer_bwd(dy, p, saved, job_of=None):
    x, h, q, k, v, tot, z, gn_g, u_g, b3, cat = saved
    d_cat = pmm(dy, p['w_out'], tb=True, name="sbg_dcat")
    g_out = pmm(cat, dy, ta=True, out_dtype=GRAD_WIRE, name="sbg_gout")
    d_osb = to_heads(d_cat[:, :SB_WIDTH], SB_HEADS)
    d_osg = to_heads(d_cat[:, SB_WIDTH:], SG_GROUPS)
    d_gn_g, d_u_g, g_w, g_b = prow_vjp(f_spatial_gate, [gn_g, u_g], gparams=[p['sgu_w'], b3], cts=[d_osg],
                                       row_grad=[True, True], tm=SG_TM, name="sgu_dmix")
    ln_g, ln_b = p['ln_gain'].reshape(1, -1), p['ln_bias'].reshape(1, -1)
    d_z, g_lng, g_lnb = prow_vjp(f_gate_prep, [z], params=[ln_g, ln_b], cts=[from_heads(d_u_g), from_heads(d_gn_g)],
                                 row_grad=[True], row_dtypes=[BF16], tm=ROW_TM, name="sgu_dprep")
    job = None if job_of is None else job_of({'w_out': g_out})
    (dq, dk, dv), landed = sb_bwd(q, k, v, tot, d_osb, tq=SB_TILES[0], tk=SB_TILES[1], name="sb_bwd", job=job)
    d_proj = jnp.concatenate([from_heads(dq).astype(BF16), from_heads(dk).astype(BF16), from_heads(dv).astype(BF16),
                              d_z], axis=-1)
    g_in = pmm(h, d_proj, ta=True, out_dtype=GRAD_WIRE, name="sbg_gin")
    dx, g_norm = _in_proj_bwd(d_proj, p['w_in'], x, p['norm'], dy, "sbg_dh")
    return dx, {'norm': g_norm, 'w_in': g_in, 'ln_gain': g_lng.reshape(-1), 'ln_bias': g_lnb.reshape(-1),
                'sgu_w': g_w, 'sgu_b': g_b.reshape(SG_GROUPS, SG_CHUNK), 'w_out': g_out}, landed


def mla_fwd(x, h, cos, sin, p, next_gain):
    proj = pmm(h, p['w_in'], name="mla_in")
    c_q, c_kv, k_r = proj[:, :MLA_Q_LORA], proj[:, MLA_Q_LORA:MLA_Q_LORA + MLA_KV_LORA], proj[:, MLA_Q_LORA + MLA_KV_LORA:]
    cqn = _norm_fwd(c_q, p['q_lora_gain'], "mla_qlora_norm")
    ckvn = _norm_fwd(c_kv, p['kv_lora_gain'], "mla_kvlora_norm")
    q_h = to_heads(pmm(cqn, p['w_uq'], name="mla_uq"), MLA_HEADS)
    kv_h = to_heads(pmm(ckvn, p['w_ukv'], name="mla_ukv"), MLA_HEADS)
    k_nope, v = kv_h[..., :MLA_NOPE], kv_h[..., MLA_NOPE:].astype(BF16)
    q_g, k_g = p['q_gain'].reshape(1, -1), p['k_gain'].reshape(1, -1)
    kp = prow(f_mla_k, [k_nope, k_r, cos, sin], params=[k_g], outs=[(MLA_QK, BF16, True)], tm=HEAD_TM,
              name="mla_kprep")[0]
    o, lse = sm_fwd(q_h, kp, v, tq=SM_TILE, tk=SM_TILE, name="mla_att_fwd", q_prep=(cos, sin, q_g))
    o_flat = from_heads(o).astype(BF16)
    out = _out_proj(o_flat, p['w_out'], x, next_gain, 1.0, "mla_out")
    return out, (x, h, c_q, c_kv, k_r, cqn, ckvn, q_h, k_nope, v, kp, o, lse, o_flat, q_g, k_g)


def mla_bwd(dy, cos, sin, p, saved, job_of=None):
    x, h, c_q, c_kv, k_r, cqn, ckvn, q_h, k_nope, v, kp, o, lse, o_flat, q_g, k_g = saved
    do = to_heads(pmm(dy, p['w_out'], tb=True, name="mla_do"), MLA_HEADS)
    g_out = pmm(o_flat, dy, ta=True, out_dtype=GRAD_WIRE, name="mla_gout")
    job = None if job_of is None else job_of({'w_out': g_out})
    (dq_h, dkp, dv, g_qg), landed = sm_bwd(q_h, kp, v, o, lse, do, tq=SM_TILE, tk=SM_TILE, name="mla_att_bwd",
                                           q_prep=(cos, sin, q_g), job=job)
    dk_nope, dk_r, g_kg = prow_vjp(f_mla_k, [k_nope, k_r, cos, sin], params=[k_g], cts=[dkp],
                                   row_grad=[True, True, False, False], tm=HEAD_TM, name="mla_dkprep")
    d_q = from_heads(dq_h)
    d_kv = from_heads(jnp.concatenate([dk_nope, dv], axis=-1)).astype(BF16)
    g_uq = pmm(cqn, d_q, ta=True, out_dtype=GRAD_WIRE, name="mla_guq")
    d_cqn = pmm(d_q, p['w_uq'], tb=True, name="mla_dcqn")
    g_ukv = pmm(ckvn, d_kv, ta=True, out_dtype=GRAD_WIRE, name="mla_gukv")
    d_ckvn = pmm(d_kv, p['w_ukv'], tb=True, name="mla_dckvn")
    d_cq, g_qlora = _norm_bwd(c_q, p['q_lora_gain'], d_cqn, None, "mla_dqlora_norm")
    d_ckv, g_kvlora = _norm_bwd(c_kv, p['kv_lora_gain'], d_ckvn, None, "mla_dkvlora_norm")
    d_proj = jnp.concatenate([d_cq, d_ckv, dk_r], axis=-1).astype(BF16)
    g_in = pmm(h, d_proj, ta=True, out_dtype=GRAD_WIRE, name="mla_gin")
    dx, g_norm = _in_proj_bwd(d_proj, p['w_in'], x, p['norm'], dy, "mla_dh")
    return dx, {'norm': g_norm, 'w_in': g_in, 'q_lora_gain': g_qlora, 'kv_lora_gain': g_kvlora, 'w_uq': g_uq,
                'w_ukv': g_ukv, 'q_gain': g_qg.reshape(-1), 'k_gain': g_kg.reshape(-1), 'w_out': g_out}, landed


def xattn_fwd(x, hq, mem, p, tag, next_gain):
    hm = _norm_fwd(mem, p['mem_norm'], f"{tag}_mem_norm")
    q_h = ColGroups(pmm(hq, p['wq'], name=f"{tag}_q"), MEM_HEAD_DIM)
    kv = pmm(hm, p['wkv'], name=f"{tag}_kv").reshape(mem.shape[0], MEM_HEADS, 2 * MEM_HEAD_DIM).transpose(1, 0, 2)
    k_h, v_h = kv[..., :MEM_HEAD_DIM], kv[..., MEM_HEAD_DIM:]
    q_g, k_g = p['q_gain'].reshape(1, -1), p['k_gain'].reshape(1, -1)
    o_flat = prow(f_xattn, [q_h], gparams=[k_h, v_h], params=[q_g, k_g], outs=[(MEM_HEAD_DIM, BF16, 'cols')],
                  tm=XATT_TM, name=f"{tag}_att")[0]
    out = _out_proj(o_flat, p['wo'], x, next_gain, 1.0, f"{tag}_out")
    return out, (x, mem, hq, hm, q_h, k_h, v_h, q_g, k_g, o_flat)


def xattn_bwd(dy, p, saved, tag):
    x, mem, hq, hm, q_h, k_h, v_h, q_g, k_g, o_flat = saved
    d_o = ColGroups(pmm(dy, p['wo'], tb=True, name=f"{tag}_do"), MEM_HEAD_DIM)
    g_wo = pmm(o_flat, dy, ta=True, out_dtype=GRAD_WIRE, name=f"{tag}_gwo")
    d_q, dk_h, dv_h, g_qg, g_kg = prow_vjp(f_xattn, [q_h], gparams=[k_h, v_h], params=[q_g, k_g], cts=[d_o],
                                           row_grad=[True], row_dtypes=[BF16], tm=XATT_TM, name=f"{tag}_datt")
    d_kv = jnp.concatenate([dk_h, dv_h], axis=-1).transpose(1, 0, 2).reshape(mem.shape[0], -1).astype(BF16)
    g_wq = pmm(hq, d_q, ta=True, out_dtype=GRAD_WIRE, name=f"{tag}_gwq")
    dx, g_norm = _in_proj_bwd(d_q, p['wq'], x, p['norm'], dy, f"{tag}_dhq")
    g_wkv = pmm(hm, d_kv, ta=True, out_dtype=GRAD_WIRE, name=f"{tag}_gwkv")
    dhm = pmm(d_kv, p['wkv'], tb=True, name=f"{tag}_dhm")
    _, g_mem_norm = _norm_bwd(mem, p['mem_norm'], dhm, None, f"{tag}_dmem_norm", want_row=False)
    return dx, {'norm': g_norm, 'mem_norm': g_mem_norm, 'wq': g_wq, 'wkv': g_wkv, 'q_gain': g_qg.reshape(-1),
                'k_gain': g_kg.reshape(-1), 'wo': g_wo}


def rope_tables(positions):
    half = MLA_ROPE // 2
    inv_freq = ROPE_THETA ** (-jnp.arange(half, dtype=F32) / half)
    ang = positions.astype(F32)[:, None] * inv_freq
    cos, sin = jnp.cos(ang), jnp.sin(ang)
    lead = jnp.ones((ang.shape[0], MLA_NOPE), F32)
    return jnp.concatenate([lead, cos, cos], axis=1), jnp.concatenate([0.0 * lead, sin, sin], axis=1)


FIRST_UNIT = ('ffn_pre_w_gu', 0)
EARLY_UNITS = [('ffn_pre_w_down', 0), ('sbg_w_in', 0)]


def local_step(x, mem, positions, target, w, shards):
    cos, sin = rope_tables(positions)
    full = {}

    def absorb(units, gathered):
        for (n, layer), t in zip(units, gathered):
            full[(n, layer)] = gathered_to_full(t, BIG[n] - 1)

    late_units = [u for u in shards if u != FIRST_UNIT and u not in EARLY_UNITS]
    absorb([FIRST_UNIT], comm_call('gather', [shards[FIRST_UNIT]], name="gather_weights_first"))
    first_ffn_p = {'norm': w['ffn_pre_norm'][0], 'w_gu': full[FIRST_UNIT]}

    def ffn_params(kind, layer):
        return {'norm': w[f'ffn_{kind}_norm'][layer], 'w_gu': full[(f'ffn_{kind}_w_gu', layer)],
                'w_down': full[(f'ffn_{kind}_w_down', layer)]}

    def xattn_params(layer):
        return {'norm': w['xmem_norm'][layer], 'mem_norm': w['xmem_mem_norm'][layer], 'wq': full[('xmem_wq', layer)],
                'wkv': full[('xmem_wkv', layer)], 'q_gain': w['xmem_q_gain'][layer], 'k_gain': w['xmem_k_gain'][layer],
                'wo': full[('xmem_wo', layer)]}

    even_p = {'norm': w['mix_norm'][0], 'ln_gain': w['sgu_ln_gain'][0], 'ln_bias': w['sgu_ln_bias'][0],
              'sgu_w': w['sgu_w'][0], 'sgu_b': w['sgu_b'][0]}

    def early_weights_landed(gathered):
        absorb(EARLY_UNITS, gathered)
        first_ffn_p['w_down'] = full[('ffn_pre_w_down', 0)]
        even_p['w_in'] = full[('sbg_w_in', 0)]

    def late_weights_landed(gathered):
        absorb(late_units, gathered)
        even_p['w_out'] = full[('sbg_w_out', 0)]

    def mla_params():
        return {'norm': w['mix_norm'][1], 'w_in': full[('mla_w_in', 0)], 'q_lora_gain': w['mla_q_lora_gain'][0],
                'kv_lora_gain': w['mla_kv_lora_gain'][0], 'w_uq': full[('mla_w_uq', 0)],
                'w_ukv': full[('mla_w_ukv', 0)], 'q_gain': w['mla_q_gain'][0], 'k_gain': w['mla_k_gain'][0],
                'w_out': full[('mla_w_out', 0)]}

    saved = []
    h = _norm_fwd(x, w['ffn_pre_norm'][0], "ffn_pre0_norm")
    for layer in range(DEPTH):
        if layer == 0:
            (x, h), s_pre = ffn_fwd(x, h, first_ffn_p, "ffn_pre0", w['mix_norm'][0],
                                    job=CommJob('gather', [shards[u] for u in EARLY_UNITS]),
                                    after_job=early_weights_landed)
        else:
            (x, h), s_pre = ffn_fwd(x, h, ffn_params('pre', layer), f"ffn_pre{layer}", w['mix_norm'][layer])
        if layer % 2 == 0:
            (x, h), s_mix = even_mixer_fwd(x, h, even_p, w['xmem_norm'][layer],
                                           job=CommJob('gather', [shards[u] for u in late_units]),
                                           after_job=late_weights_landed)
        else:
            (x, h), s_mix = mla_fwd(x, h, cos, sin, mla_params(), w['xmem_norm'][layer])
        (x, h), s_x = xattn_fwd(x, h, mem, xattn_params(layer), f"xmem{layer}", w['ffn_post_norm'][layer])
        following = w['ffn_pre_norm'][layer + 1] if layer + 1 < DEPTH else None
        (x, h), s_post = ffn_fwd(x, h, ffn_params('post', layer), f"ffn_post{layer}", following)
        saved.append((s_pre, s_mix, s_x, s_post))

    dx, loss = loss_head(x, target, tm=ROW_TM, name="loss_head")

    ready, riding, landed = {}, [], {}

    def offer(name, layer, g):
        ready[(name, layer)] = full_to_owner_major(g, BIG[name] - 1)

    def ride(name):
        def job_of(own):
            offer(name, 0, own['w_out'])
            riding[:] = list(ready)
            return CommJob('exchange', [ready.pop(u) for u in riding])
        return job_of

    def last_rides(run, **own):
        for kind, g in own.items():
            offer('ffn_pre_' + kind, 0, g)
        units = list(ready)
        if not units:
            return run(None)[0]
        res, arrived = run(CommJob('exchange', [ready.pop(u) for u in units]))
        landed.update(zip(units, arrived))
        return res

    per_layer = []
    for layer in reversed(range(DEPTH)):
        s_pre, s_mix, s_x, s_post = saved[layer]
        dx, g_post = ffn_bwd(dx, ffn_params('post', layer), s_post, f"ffn_post{layer}")
        offer('ffn_post_w_gu', layer, g_post['w_gu'])
        offer('ffn_post_w_down', layer, g_post['w_down'])
        dx, g_x = xattn_bwd(dx, xattn_params(layer), s_x, f"xmem{layer}")
        for n in ('wq', 'wkv', 'wo'):
            offer('xmem_' + n, layer, g_x[n])
        if layer % 2 == 0:
            dx, g_mix, arrived = even_mixer_bwd(dx, even_p, s_mix, job_of=ride('sbg_w_out'))
            landed.update(zip(riding, arrived))
            offer('sbg_w_in', 0, g_mix['w_in'])
        else:
            dx, g_mix, arrived = mla_bwd(dx, cos, sin, mla_params(), s_mix, job_of=ride('mla_w_out'))
            landed.update(zip(riding, arrived))
            for n in ('w_in', 'w_uq', 'w_ukv'):
                offer('mla_' + n, 0, g_mix[n])
        if layer == 0:
            dx, g_pre = ffn_bwd(dx, ffn_params('pre', layer), s_pre, f"ffn_pre{layer}", with_job=last_rides)
        else:
            dx, g_pre = ffn_bwd(dx, ffn_params('pre', layer), s_pre, f"ffn_pre{layer}")
            offer('ffn_pre_w_gu', layer, g_pre['w_gu'])
            offer('ffn_pre_w_down', layer, g_pre['w_down'])
        per_layer.append((layer, g_pre, g_mix, g_x, g_post))
    per_layer.sort(key=lambda t: t[0])
    assert not ready

    def stack(pick):
        return jnp.stack([pick(t) for t in per_layer])

    g_even, g_mla = per_layer[0][2], per_layer[1][2]
    small_grads = {
        'ffn_pre_norm': stack(lambda t: t[1]['norm']), 'mix_norm': stack(lambda t: t[2]['norm']),
        'sgu_ln_gain': g_even['ln_gain'][None], 'sgu_ln_bias': g_even['ln_bias'][None],
        'sgu_w': g_even['sgu_w'][None], 'sgu_b': g_even['sgu_b'][None],
        'mla_q_lora_gain': g_mla['q_lora_gain'][None], 'mla_kv_lora_gain': g_mla['kv_lora_gain'][None],
        'mla_q_gain': g_mla['q_gain'][None], 'mla_k_gain': g_mla['k_gain'][None],
        'xmem_norm': stack(lambda t: t[3]['norm']), 'xmem_mem_norm': stack(lambda t: t[3]['mem_norm']),
        'xmem_q_gain': stack(lambda t: t[3]['q_gain']), 'xmem_k_gain': stack(lambda t: t[3]['k_gain']),
        'ffn_post_norm': stack(lambda t: t[4]['norm']),
    }
    return loss, dx, small_grads, landed


def _device_slot():
    x, y, c = _me()
    return 4 * x + 2 * y + c


def kernel(x, mem, positions, ffn_pre_norm, ffn_pre_w_gu, ffn_pre_w_down, mix_norm, sbg_w_in, sgu_ln_gain, sgu_ln_bias, sgu_w, sgu_b, sbg_w_out, mla_w_in, mla_q_lora_gain, mla_kv_lora_gain, mla_w_uq, mla_w_ukv, mla_q_gain, mla_k_gain, mla_w_out, xmem_norm, xmem_mem_norm, xmem_wq, xmem_wkv, xmem_q_gain, xmem_k_gain, xmem_wo, ffn_post_norm, ffn_post_w_gu, ffn_post_w_down, loss_target, m_ffn_pre_norm, m_ffn_pre_w_gu, m_ffn_pre_w_down, m_mix_norm, m_sbg_w_in, m_sgu_ln_gain, m_sgu_ln_bias, m_sgu_w, m_sgu_b, m_sbg_w_out, m_mla_w_in, m_mla_q_lora_gain, m_mla_kv_lora_gain, m_mla_w_uq, m_mla_w_ukv, m_mla_q_gain, m_mla_k_gain, m_mla_w_out, m_xmem_norm, m_xmem_mem_norm, m_xmem_wq, m_xmem_wkv, m_xmem_q_gain, m_xmem_k_gain, m_xmem_wo, m_ffn_post_norm, m_ffn_post_w_gu, m_ffn_post_w_down, v_ffn_pre_norm, v_ffn_pre_w_gu, v_ffn_pre_w_down, v_mix_norm, v_sbg_w_in, v_sgu_ln_gain, v_sgu_ln_bias, v_sgu_w, v_sgu_b, v_sbg_w_out, v_mla_w_in, v_mla_q_lora_gain, v_mla_kv_lora_gain, v_mla_w_uq, v_mla_w_ukv, v_mla_q_gain, v_mla_k_gain, v_mla_w_out, v_xmem_norm, v_xmem_mem_norm, v_xmem_wq, v_xmem_wkv, v_xmem_q_gain, v_xmem_k_gain, v_xmem_wo, v_ffn_post_norm, v_ffn_post_w_gu, v_ffn_post_w_down):
    args = locals()
    w_in = {n: args[n] for n in WEIGHTS}
    m_in = {n: args["m_" + n] for n in WEIGHTS}
    v_in = {n: args["v_" + n] for n in WEIGHTS}
    slot = _device_slot()

    tiny = jnp.zeros((8, LANES), F32)
    for i, src in enumerate((w_in, m_in, v_in)):
        tiny = tiny.at[i, :64].set(src['mla_q_lora_gain'][0]).at[i + 3, :32].set(src['mla_kv_lora_gain'][0])
    tiny_all = comm_call('gather', [tiny], name="gather_lora_gains")[0]
    full_small = []
    for i, src in enumerate((w_in, m_in, v_in)):
        d = {n: src[n] for n in SMALL}
        d['mla_q_lora_gain'] = tiny_all[:, i, :64].reshape(1, MLA_Q_LORA)
        d['mla_kv_lora_gain'] = tiny_all[:, i + 3, :32].reshape(1, MLA_KV_LORA)
        full_small.append(d)
    w_small, m_small, v_small = full_small
    small_shapes = {n: w_small[n].shape for n in SMALL}

    shards = {(n, layer): w_in[n][layer].astype(BF16) for n in BIG for layer in range(w_in[n].shape[0])}
    loss, dx, grads, landed = local_step(x[0], mem[0], positions[0], loss_target[0], w_small, shards)
    loss = lax.psum(loss, ("x", "y", "c"))
    big_out = {n: adamw([landed[(n, layer)] for layer in range(w_in[n].shape[0])], w_in[n], m_in[n], v_in[n],
                        name=f"adamw_{n}") for n in BIG}

    small_parts = comm_call('gather', [pack_small(grads, small_shapes)], name="gather_small_grads")
    small_out = adamw(small_parts, pack_small(w_small, small_shapes)[None], pack_small(m_small, small_shapes)[None],
                      pack_small(v_small, small_shapes)[None], name="adamw_small")
    small_out = [unpack_small(t[0], small_shapes) for t in small_out]
    for d in small_out:
        for n, width in zip(GAIN_SHARDED, (64, 32)):
            d[n] = lax.dynamic_slice(d[n], (0, slot * width), (1, width))

    outs = [loss, dx[None]]
    for kind, small_d in enumerate(small_out):
        outs += [big_out[n][kind] if n in BIG else small_d[n] for n in WEIGHTS]
    return tuple(outs)
```

```python
import jax
import jax.numpy as jnp
from jax import lax
from jax.experimental import pallas as pl
from jax.experimental.pallas import tpu as pltpu

F32 = jnp.float32
BF16 = jnp.bfloat16
MESH = pl.DeviceIdType.MESH
N_DEV = 8

VMEM_LIMIT_BYTES = 56 * 1024 * 1024
LANES = 128

D_MODEL = 1024
DEPTH = 2
EPS = 1e-6
SB_HEADS, SB_HEAD_DIM = 8, 64
SB_WIDTH = SB_HEADS * SB_HEAD_DIM
SG_GROUPS, SG_GROUP_DIM, SG_CHUNK = 8, 64, 128
SG_WIDTH = SG_GROUPS * SG_GROUP_DIM
MLA_HEADS, MLA_NOPE, MLA_ROPE, MLA_V = 16, 64, 32, 64
MLA_QK = MLA_NOPE + MLA_ROPE
MLA_Q_LORA, MLA_KV_LORA = 512, 256
ROPE_THETA = 10000.0
MEM_HEADS = 4
MEM_HEAD_DIM = D_MODEL // MEM_HEADS

ADAM_LR, ADAM_B1, ADAM_B2, ADAM_EPS, ADAM_WD, ADAM_STEP = 0.001, 0.9, 0.999, 1e-08, 0.01, 10

WEIGHTS = ['ffn_pre_norm', 'ffn_pre_w_gu', 'ffn_pre_w_down', 'mix_norm', 'sbg_w_in', 'sgu_ln_gain', 'sgu_ln_bias',
           'sgu_w', 'sgu_b', 'sbg_w_out', 'mla_w_in', 'mla_q_lora_gain', 'mla_kv_lora_gain', 'mla_w_uq', 'mla_w_ukv',
           'mla_q_gain', 'mla_k_gain', 'mla_w_out', 'xmem_norm', 'xmem_mem_norm', 'xmem_wq', 'xmem_wkv',
           'xmem_q_gain', 'xmem_k_gain', 'xmem_wo', 'ffn_post_norm', 'ffn_post_w_gu', 'ffn_post_w_down']
BIG = {'ffn_pre_w_gu': 2, 'ffn_pre_w_down': 1, 'sbg_w_in': 2, 'sbg_w_out': 1, 'mla_w_in': 1, 'mla_w_uq': 2,
       'mla_w_ukv': 2, 'mla_w_out': 1, 'xmem_wq': 1, 'xmem_wkv': 2, 'xmem_wo': 1, 'ffn_post_w_gu': 2,
       'ffn_post_w_down': 1}
GAIN_SHARDED = ('mla_q_lora_gain', 'mla_kv_lora_gain')
SMALL = [n for n in WEIGHTS if n not in BIG]
GRAD_WIRE = BF16
FFN_SAVE = BF16
SMALL_ROW_MULTIPLE = 16


def _cparams(sem=None):
    return pltpu.CompilerParams(dimension_semantics=sem, vmem_limit_bytes=VMEM_LIMIT_BYTES)


MM_TILE_CAP = 1408


def _pick(dim, cap=MM_TILE_CAP):
    if dim % LANES:
        return dim
    return max(t for t in range(LANES, min(dim, cap) + 1, LANES) if dim % t == 0)


def _rms(x, g):
    return x * lax.rsqrt(jnp.mean(x * x, axis=-1, keepdims=True) + EPS) * g


def pmm(a, b, *, a2=None, ta=False, tb=False, out_dtype=F32, res=None, alpha=1.0, norm_out=None, norm_bwd=None,
        job=None, name):
    kdim, m = (a.shape if ta else a.shape[::-1])
    n = b.shape[0] if tb else b.shape[1]
    tm, tn, tk = _pick(m), _pick(n), _pick(kdim)
    whole_rows = norm_out is not None or norm_bwd is not None
    if whole_rows:
        assert tn == n
    if norm_bwd is not None:
        tm = min(tm, 512)
    nk1 = kdim // tk
    nk = nk1 if a2 is None else 2 * nk1
    assert a2 is None or (a2.shape == a.shape and not ta)
    dims = (((0 if ta else 1,), (1 if tb else 0,)), ((), ()))
    n_lead = 2 if a2 is None else 3
    n_extra = (res is not None) + (norm_out is not None) + (0 if norm_bwd is None else 2 + (norm_bwd[2] is not None))

    def body(*refs):
        a_ref, b_ref = refs[:2]
        extra = list(refs[n_lead:n_lead + n_extra])
        outs, acc_ref = refs[n_lead + n_extra:-1], refs[-1]
        i, k = pl.program_id(0), pl.program_id(2)

        @pl.when(k == 0)
        def _():
            acc_ref[...] = jnp.zeros_like(acc_ref)

        def accumulate(lhs_ref):
            acc_ref[...] += lax.dot_general(lhs_ref[...].astype(BF16), b_ref[...].astype(BF16), dims,
                                            preferred_element_type=F32)

        if a2 is None:
            accumulate(a_ref)
        else:
            pl.when(k < nk1)(lambda: accumulate(a_ref))
            pl.when(k >= nk1)(lambda: accumulate(refs[2]))

        @pl.when(k == nk - 1)
        def _():
            r = acc_ref[...]
            if alpha != 1.0:
                r = r * alpha
            if res is not None:
                r = extra.pop(0)[...] + r
            if norm_bwd is None:
                outs[0][...] = r.astype(out_dtype)
            if norm_out is not None:
                outs[1][...] = _rms(r, extra.pop(0)[...]).astype(BF16)
            if norm_bwd is not None:
                x_ref, g_ref = extra.pop(0), extra.pop(0)
                _, pull = jax.vjp(_rms, x_ref[...], g_ref[...])
                dx, dg = pull(r)
                if norm_bwd[2] is not None:
                    dx = dx + extra.pop(0)[...]
                outs[0][...] = dx

                @pl.when(i == 0)
                def _():
                    outs[1][...] = dg

                @pl.when(i != 0)
                def _():
                    outs[1][...] += dg

    gi, gj = m // tm, n // tn
    a_bytes, b_bytes = a.size * a.dtype.itemsize, (n * kdim) * b.dtype.itemsize
    j_outer = not whole_rows and nk == 1 and gj * a_bytes + b_bytes < a_bytes + gi * b_bytes
    grid = (gj, gi, nk) if j_outer else (gi, gj, nk)

    def spec(block, index):
        return pl.BlockSpec(block, (lambda j, i, k: index(i, j, k)) if j_outer else index)

    a_spec = spec((tk, tm), lambda i, j, k: (k, i)) if ta else spec((tm, tk), lambda i, j, k: (i, k))
    b_spec = spec((tn, tk), lambda i, j, k: (j, k)) if tb else spec((tk, tn), lambda i, j, k: (k, j))
    o_spec = spec((tm, tn), lambda i, j, k: (i, j))
    g_spec = spec((1, tn), lambda i, j, k: (0, 0))
    ins, in_specs = [a, b], [a_spec, b_spec]
    if a2 is not None:
        in_specs[0] = spec((tm, tk), lambda i, j, k: (i, jnp.minimum(k, nk1 - 1)))
        ins.append(a2)
        in_specs.append(spec((tm, tk), lambda i, j, k: (i, jnp.maximum(k - nk1, 0))))
    if res is not None:
        ins.append(res)
        in_specs.append(o_spec)
    out_shape, out_specs = [jax.ShapeDtypeStruct((m, n), out_dtype)], [o_spec]
    if norm_out is not None:
        ins.append(norm_out)
        in_specs.append(g_spec)
        out_shape.append(jax.ShapeDtypeStruct((m, n), BF16))
        out_specs.append(o_spec)
    if norm_bwd is not None:
        ins += [t for t in norm_bwd if t is not None]
        in_specs += [o_spec, g_spec] + ([o_spec] if norm_bwd[2] is not None else [])
        out_shape = [jax.ShapeDtypeStruct((m, n), F32), jax.ShapeDtypeStruct((1, n), F32)]
        out_specs = [o_spec, g_spec]
    result, landed = ride_call(
        job, body, name=name, grid=grid, in_specs=in_specs, out_specs=out_specs, out_shape=out_shape, ins=ins,
        scratch=[pltpu.VMEM((tm, tn), F32)],
        sem=("arbitrary" if norm_bwd is not None else "parallel", "parallel", "arbitrary"))
    result = result if whole_rows else result[0]
    return result if job is None else (result, landed)


def ffn_gate_up(h, w_gu, *, name, job=None):
    m, kdim = h.shape
    n = w_gu.shape[1] // 2
    tm, tn = min(_pick(m), 512), _pick(n)
    up_off = n // tn

    def body(a_ref, bg_ref, bu_ref, gate_ref, up_ref, act_ref):
        av = a_ref[...].astype(BF16)
        gate = _dg(av, bg_ref[...].astype(BF16), 1, 0)
        up = _dg(av, bu_ref[...].astype(BF16), 1, 0)
        gate_ref[...] = gate.astype(gate_ref.dtype)
        up_ref[...] = up.astype(up_ref.dtype)
        act_ref[...] = (jax.nn.silu(gate) * up).astype(BF16)

    o_spec = pl.BlockSpec((tm, tn), lambda j, i: (i, j))
    return ride_call(
        job, body, name=name, grid=(n // tn, m // tm),
        in_specs=[pl.BlockSpec((tm, kdim), lambda j, i: (i, 0)), pl.BlockSpec((kdim, tn), lambda j, i: (0, j)),
                  pl.BlockSpec((kdim, tn), lambda j, i: (0, j + up_off))],
        out_specs=[o_spec] * 3,
        out_shape=[jax.ShapeDtypeStruct((m, n), FFN_SAVE), jax.ShapeDtypeStruct((m, n), FFN_SAVE),
                   jax.ShapeDtypeStruct((m, n), BF16)],
        ins=[h, w_gu, w_gu], sem=("parallel", "parallel"))


def ffn_gate_up_bwd(dy, w_down, gate, up, *, alpha, name, job=None):
    m, kdim = dy.shape
    n = w_down.shape[0]
    tm, tn = min(_pick(m), 512), _pick(n)

    def body(a_ref, b_ref, gate_ref, up_ref, dgate_ref, dup_ref):
        d_act = _dg(a_ref[...].astype(BF16), b_ref[...].astype(BF16), 1, 1) * alpha
        _, pull = jax.vjp(lambda g, u: jax.nn.silu(g) * u, gate_ref[...].astype(F32), up_ref[...].astype(F32))
        d_gate, d_up = pull(d_act)
        dgate_ref[...] = d_gate.astype(BF16)
        dup_ref[...] = d_up.astype(BF16)

    o_spec = pl.BlockSpec((tm, tn), lambda j, i: (i, j))
    return ride_call(
        job, body, name=name, grid=(n // tn, m // tm),
        in_specs=[pl.BlockSpec((tm, kdim), lambda j, i: (i, 0)), pl.BlockSpec((tn, kdim), lambda j, i: (j, 0)),
                  o_spec, o_spec],
        out_specs=[o_spec] * 2, out_shape=[jax.ShapeDtypeStruct((m, n), BF16)] * 2,
        ins=[dy, w_down, gate, up], sem=("parallel", "parallel"))


def _dg(a, b, ca, cb):
    return lax.dot_general(a, b, (((ca,), (cb,)), ((), ())), preferred_element_type=F32)


@jax.custom_vjp
def bdot(a, b):
    return _dg(a.astype(BF16), b.astype(BF16), 1, 0)


def _bdot_fwd(a, b):
    ab, bb = a.astype(BF16), b.astype(BF16)
    return _dg(ab, bb, 1, 0), (ab, bb)


def _bdot_bwd(saved, g):
    ab, bb = saved
    gb = g.astype(BF16)
    return _dg(gb, bb, 1, 1), _dg(ab, gb, 0, 0)


bdot.defvjp(_bdot_fwd, _bdot_bwd)


@jax.custom_vjp
def bdot_nt(a, b):
    return _dg(a.astype(BF16), b.astype(BF16), 1, 1)


def _bdot_nt_fwd(a, b):
    ab, bb = a.astype(BF16), b.astype(BF16)
    return _dg(ab, bb, 1, 1), (ab, bb)


def _bdot_nt_bwd(saved, g):
    ab, bb = saved
    gb = g.astype(BF16)
    return _dg(gb, bb, 1, 0), _dg(gb, ab, 0, 0)


bdot_nt.defvjp(_bdot_nt_fwd, _bdot_nt_bwd)


class ColGroups:
    def __init__(self, arr, width):
        self.arr, self.width = arr, width
        self.shape, self.dtype, self.ndim = arr.shape, arr.dtype, 3


def _plain(a):
    return a.arr if isinstance(a, ColGroups) else a


def _row_spec(arr, tm):
    if isinstance(arr, ColGroups):
        return pl.BlockSpec((tm, arr.width), lambda r, g: (r, g))
    if arr.ndim == 3:
        return pl.BlockSpec((None, tm, arr.shape[2]), lambda r, g: (g, r, 0))
    return pl.BlockSpec((tm, arr.shape[1]), lambda r, g: (r, 0))


def _gparam_spec(arr):
    return pl.BlockSpec((None,) + arr.shape[1:], lambda r, g: (g, 0, 0))


def _whole_spec(arr):
    nd = arr.ndim
    return pl.BlockSpec(arr.shape, lambda r, g: (0,) * nd)


def _groups(rows, gparams):
    gs = {a.shape[1] // a.width if isinstance(a, ColGroups) else a.shape[0] for a in rows if a.ndim == 3}
    gs |= {a.shape[0] for a in gparams}
    assert len(gs) <= 1
    return gs.pop() if gs else 1


def prow(fn, rows, gparams=(), params=(), *, outs, tm, name):
    rows, gparams, params = list(rows), list(gparams), list(params)
    n_groups = _groups(rows, gparams)
    n_rows = rows[0].shape[-2]
    n_in = len(rows) + len(gparams) + len(params)

    def body(*refs):
        vals = [r[...] for r in refs[:n_in]]
        res = fn(*vals)
        for o_ref, r in zip(refs[n_in:], res, strict=True):
            o_ref[...] = r.astype(o_ref.dtype)

    out_shape, out_specs = [], []
    for width, dtype, grouped in outs:
        if grouped == 'cols':
            out_shape.append(jax.ShapeDtypeStruct((n_rows, n_groups * width), dtype))
            out_specs.append(_row_spec(ColGroups(out_shape[-1], width), tm))
            continue
        shp = (n_groups, n_rows, width) if grouped else (n_rows, width)
        out_shape.append(jax.ShapeDtypeStruct(shp, dtype))
        out_specs.append(_row_spec(out_shape[-1], tm))
    return pl.pallas_call(
        body, name=name, grid=(n_rows // tm, n_groups),
        in_specs=[_row_spec(a, tm) for a in rows] + [_gparam_spec(a) for a in gparams] + [_whole_spec(a) for a in params],
        out_specs=out_specs, out_shape=out_shape,
        compiler_params=_cparams(("parallel", "arbitrary")),
    )(*[_plain(a) for a in rows], *gparams, *params)


def prow_vjp(fn, rows, gparams=(), params=(), *, cts, row_grad, adds=None, row_dtypes=None, gparam_grad=None,
             param_grad=None, tm, name):
    rows, gparams, params, cts = list(rows), list(gparams), list(params), list(cts)
    gparam_grad = list(gparam_grad) if gparam_grad is not None else [True] * len(gparams)
    param_grad = list(param_grad) if param_grad is not None else [True] * len(params)
    n_groups = _groups(rows + cts, gparams)
    n_rows = rows[0].shape[-2]
    want_rows = [i for i, w in enumerate(row_grad) if w]
    adds = list(adds) if adds is not None else [None] * len(want_rows)
    row_dtypes = list(row_dtypes) if row_dtypes is not None else [F32] * len(want_rows)
    add_arrays = [a for a in adds if a is not None]
    n_r, n_g, n_p, n_c, n_a = len(rows), len(gparams), len(params), len(cts), len(add_arrays)
    mask = list(row_grad) + gparam_grad + param_grad

    def body(*refs):
        r_id, g_id = pl.program_id(0), pl.program_id(1)
        n_in = n_r + n_g + n_p
        vals = [r[...] for r in refs[:n_in]]
        ct_vals = tuple(r[...].astype(F32) for r in refs[n_in:n_in + n_c])
        add_refs = list(refs[n_in + n_c:n_in + n_c + n_a])
        out_refs = list(refs[n_in + n_c + n_a:])
        diff_idx = [i for i, w in enumerate(mask) if w]

        def wrapped(*diff):
            full = list(vals)
            for i, d in zip(diff_idx, diff):
                full[i] = d
            return tuple(fn(*full))

        _, pull = jax.vjp(wrapped, *[vals[i].astype(F32) for i in diff_idx])
        grads = dict(zip(diff_idx, pull(ct_vals)))
        k = 0
        for j, i in enumerate(want_rows):
            o_ref = out_refs[k]
            k += 1
            gval = grads[i]
            if adds[j] is not None:
                gval = gval + add_refs.pop(0)[...].astype(F32)
            if rows[i].ndim == 2 and n_groups > 1:
                @pl.when(g_id == 0)
                def _(o_ref=o_ref, gval=gval):
                    o_ref[...] = gval.astype(o_ref.dtype)

                @pl.when(g_id != 0)
                def _(o_ref=o_ref, gval=gval):
                    o_ref[...] += gval.astype(o_ref.dtype)
            else:
                o_ref[...] = gval.astype(o_ref.dtype)
        for i in range(n_g):
            if not gparam_grad[i]:
                continue
            o_ref = out_refs[k]
            k += 1
            gval = grads[n_r + i]

            @pl.when(r_id == 0)
            def _(o_ref=o_ref, gval=gval):
                o_ref[g_id] = gval

            @pl.when(r_id != 0)
            def _(o_ref=o_ref, gval=gval):
                o_ref[g_id] += gval
        for i in range(n_p):
            if not param_grad[i]:
                continue
            o_ref = out_refs[k]
            k += 1
            gval = grads[n_r + n_g + i]
            first = jnp.logical_and(r_id == 0, g_id == 0)

            @pl.when(first)
            def _(o_ref=o_ref, gval=gval):
                o_ref[...] = gval

            @pl.when(jnp.logical_not(first))
            def _(o_ref=o_ref, gval=gval):
                o_ref[...] += gval

    out_shape, out_specs = [], []
    for j, i in enumerate(want_rows):
        out_shape.append(jax.ShapeDtypeStruct(rows[i].shape, row_dtypes[j]))
        out_specs.append(_row_spec(rows[i], tm))
    for i in range(n_g):
        if gparam_grad[i]:
            out_shape.append(jax.ShapeDtypeStruct(gparams[i].shape, F32))
            out_specs.append(_whole_spec(gparams[i]))
    for i in range(n_p):
        if param_grad[i]:
            out_shape.append(jax.ShapeDtypeStruct(params[i].shape, F32))
            out_specs.append(_whole_spec(params[i]))
    return pl.pallas_call(
        body, name=name, grid=(n_rows // tm, n_groups),
        in_specs=([_row_spec(a, tm) for a in rows] + [_gparam_spec(a) for a in gparams]
                  + [_whole_spec(a) for a in params] + [_row_spec(a, tm) for a in cts]
                  + [_row_spec(a, tm) for a in add_arrays]),
        out_specs=out_specs, out_shape=out_shape,
        compiler_params=_cparams(("arbitrary", "arbitrary")),
    )(*[_plain(a) for a in rows], *gparams, *params, *[_plain(a) for a in cts], *add_arrays)


def f_rms(x, g):
    return (_rms(x.astype(F32), g),)


def f_gate_prep(z, ln_g, ln_b):
    act = jax.nn.gelu(z)
    u, gg = act[:, :SG_WIDTH], act[:, SG_WIDTH:]
    mu = jnp.mean(gg, axis=-1, keepdims=True)
    var = jnp.mean(jnp.square(gg - mu), axis=-1, keepdims=True)
    return u, (gg - mu) * lax.rsqrt(var + EPS) * ln_g + ln_b


def f_spatial_gate(gn, u, w, b):
    t = lax.broadcasted_iota(jnp.int32, w.shape, 0)
    s = lax.broadcasted_iota(jnp.int32, w.shape, 1)
    w_causal = jnp.where(s <= t, w, 0.0)
    mixed = [bdot(w_causal, gn[i:i + SG_CHUNK]) + b for i in range(0, gn.shape[0], SG_CHUNK)]
    return (u * (mixed[0] if len(mixed) == 1 else jnp.concatenate(mixed, axis=0)),)


def _two_pieces(x):
    hi = x.astype(BF16)
    return hi, (x - hi.astype(F32)).astype(BF16)


@jax.custom_vjp
def place(x, m):
    hi, lo = _two_pieces(x)
    return _dg(hi, m, 1, 0) + _dg(lo, m, 1, 0)


def _place_fwd(x, m):
    return place(x, m), m


def _place_bwd(m, g):
    hi, lo = _two_pieces(g)
    return _dg(hi, m, 1, 1) + _dg(lo, m, 1, 1), jnp.zeros_like(m)


place.defvjp(_place_fwd, _place_bwd)


def _lane_map(rows, cols, entry):
    src = lax.broadcasted_iota(jnp.int32, (rows, cols), 0)
    dst = lax.broadcasted_iota(jnp.int32, (rows, cols), 1)
    return entry(src, dst).astype(BF16)


def _rope_tail(t, cos_w, sin_w):
    half = MLA_ROPE // 2
    lo_half = lambda d: jnp.logical_and(d >= MLA_NOPE, d < MLA_NOPE + half)
    swap = _lane_map(MLA_QK, MLA_QK, lambda s, d: jnp.where(
        jnp.logical_and(d >= MLA_NOPE + half, s == d - half), 1.0,
        jnp.where(jnp.logical_and(lo_half(d), s == d + half), -1.0, 0.0)))
    return t * cos_w + place(t, swap) * sin_w


def f_mla_q(q, cos_w, sin_w, g):
    return (_rope_tail(f_rms(q, g)[0], cos_w, sin_w),)


def f_mla_k(k_nope, k_r, cos_w, sin_w, g):
    side_by_side = (place(k_nope, _lane_map(MLA_NOPE, MLA_QK, lambda s, d: jnp.where(s == d, 1.0, 0.0)))
                    + place(k_r, _lane_map(MLA_ROPE, MLA_QK, lambda s, d: jnp.where(s + MLA_NOPE == d, 1.0, 0.0))))
    return (_rope_tail(f_rms(side_by_side, g)[0], cos_w, sin_w),)


def f_xattn(q, k, v, q_g, k_g):
    qn, kn = f_rms(q, q_g)[0], f_rms(k, k_g)[0]
    sc = bdot_nt(qn, kn) * (MEM_HEAD_DIM ** -0.5)
    return (bdot(jax.nn.softmax(sc, axis=-1), v),)


def _split_dot(x, tri, pieces=2):
    hi = x.astype(BF16)
    if pieces == 1:
        return _dg(hi, tri, 1, 0)
    lo = (x - hi.astype(F32)).astype(BF16)
    return _dg(hi, tri, 1, 0) + _dg(lo, tri, 1, 0)


def _tri(tk, cmp):
    j = lax.broadcasted_iota(jnp.int32, (tk, tk), 0)
    s = lax.broadcasted_iota(jnp.int32, (tk, tk), 1)
    return cmp(j, s).astype(BF16)


SCAN_CHUNK = 256


def _row_scan(x, tri, reverse, pieces=2):
    n = x.shape[1] // SCAN_CHUNK
    chunks = [x[:, i * SCAN_CHUNK:(i + 1) * SCAN_CHUNK] for i in range(n)]
    out, seen = [None] * n, None
    for i in (reversed(range(n)) if reverse else range(n)):
        local = _split_dot(chunks[i], tri, pieces)
        out[i] = local if seen is None else local + seen
        total = jnp.sum(chunks[i], axis=1, keepdims=True)
        seen = total if seen is None else seen + total
    return (out[0] if n == 1 else jnp.concatenate(out, axis=1)), seen


def _att_specs(s_len, tq, dq, dv):
    q_spec = pl.BlockSpec((None, tq, dq), lambda h, i: (h, i, 0))
    k_spec = pl.BlockSpec((None, s_len, dq), lambda h, i: (h, 0, 0))
    v_spec = pl.BlockSpec((None, s_len, dv), lambda h, i: (h, 0, 0))
    o_spec = pl.BlockSpec((None, tq, dv), lambda h, i: (h, i, 0))
    r_spec = pl.BlockSpec((None, tq, 1), lambda h, i: (h, i, 0))
    return q_spec, k_spec, v_spec, o_spec, r_spec


def _key_blocks(qi, tq, tk):
    return (qi * tq) // tk, ((qi + 1) * tq + tk - 1) // tk


def _earlier(qi, j, tq, tk):
    row = qi * tq + lax.broadcasted_iota(jnp.int32, (tq, tk), 0)
    col = j * tk + lax.broadcasted_iota(jnp.int32, (tq, tk), 1)
    return col < row


LOG2_E = 1.4426950408889634


def _log2_sigmoid(z2):
    return jnp.minimum(z2, 0.0) - jnp.log2(1.0 + jnp.exp2(-jnp.abs(z2)))


def sb_fwd(q, k, v, *, tq, tk, name, job=None):
    n_heads, s_len, d = q.shape
    scale2 = SB_HEAD_DIM ** -0.5 * LOG2_E

    def body(q_ref, k_ref, v_ref, o_ref, tot_ref):
        qi = pl.program_id(1)
        qv = q_ref[...]
        upper = _tri(SCAN_CHUNK, lambda j, s: j > s)
        n_full, n_all = _key_blocks(qi, tq, tk)

        def make_step(masked, last):
            def step(jj, carry):
                acc, rest = carry
                j = last - 1 - jj
                sl = pl.ds(pl.multiple_of(j * tk, tk), tk)
                ks, vs = k_ref[sl, :], v_ref[sl, :]
                z2 = _dg(qv, ks, 1, 1) * scale2
                log_beta = _log2_sigmoid(z2)
                log_stay = log_beta - z2
                if masked:
                    valid = _earlier(qi, j, tq, tk)
                    log_stay = jnp.where(valid, log_stay, 0.0)
                after, total = _row_scan(log_stay, upper, True)
                w = jnp.exp2(log_beta + after + rest)
                if masked:
                    w = jnp.where(valid, w, 0.0)
                acc = acc + _dg(w.astype(BF16), vs, 1, 0)
                return acc, rest + total
            return step

        carry = (jnp.zeros((tq, d), F32), jnp.zeros((tq, 1), F32))
        carry = lax.fori_loop(0, n_all - n_full, make_step(True, n_all), carry)
        acc, rest = lax.fori_loop(0, n_full, make_step(False, n_full), carry)
        o_ref[...] = acc
        tot_ref[...] = rest

    q_spec, k_spec, v_spec, o_spec, r_spec = _att_specs(s_len, tq, d, d)
    return ride_call(
        job, body, name=name, grid=(n_heads, s_len // tq), in_specs=[q_spec, k_spec, v_spec],
        out_specs=[o_spec, r_spec],
        out_shape=[jax.ShapeDtypeStruct((n_heads, s_len, d), F32), jax.ShapeDtypeStruct((n_heads, s_len, 1), F32)],
        ins=[q, k, v], sem=("parallel", "arbitrary"))


def sb_bwd(q, k, v, tot, do, *, tq, tk, name, job=None):
    n_heads, s_len, d = q.shape
    scale = SB_HEAD_DIM ** -0.5
    scale2 = scale * LOG2_E

    def body(q_ref, k_ref, v_ref, tot_ref, do_ref, dq_ref, dk_ref, dv_ref):
        qi = pl.program_id(1)

        @pl.when(qi == 0)
        def _():
            dk_ref[...] = jnp.zeros_like(dk_ref)
            dv_ref[...] = jnp.zeros_like(dv_ref)

        qv = q_ref[...]
        dob = do_ref[...].astype(BF16)
        total = tot_ref[...]
        incl = _tri(SCAN_CHUNK, lambda j, s: j <= s)
        excl = _tri(SCAN_CHUNK, lambda j, s: j < s)
        n_full, n_all = _key_blocks(qi, tq, tk)

        def make_step(masked):
            def step(j, carry):
                dq, stay_before, dl_before = carry
                sl = pl.ds(pl.multiple_of(j * tk, tk), tk)
                ks, vs = k_ref[sl, :], v_ref[sl, :]
                z2 = _dg(qv, ks, 1, 1) * scale2
                log_beta = _log2_sigmoid(z2)
                log_stay = log_beta - z2
                if masked:
                    valid = _earlier(qi, j, tq, tk)
                    log_stay = jnp.where(valid, log_stay, 0.0)
                stay_upto, stay_sum = _row_scan(log_stay, incl, False)
                w = jnp.exp2(log_beta + (total - stay_before) - stay_upto)
                if masked:
                    w = jnp.where(valid, w, 0.0)
                dl = _dg(dob, vs, 1, 1) * w
                dl_upto, dl_sum = _row_scan(dl, excl, False, pieces=1)
                dl_prefix = dl_upto + dl_before
                beta = jnp.exp2(log_beta)
                dz = dl * (1.0 - beta) - beta * dl_prefix
                if masked:
                    dz = jnp.where(valid, dz, 0.0)
                dzb = dz.astype(BF16)
                dq = dq + _dg(dzb, ks, 1, 0)
                dk_ref[sl, :] += _dg(dzb, qv, 0, 0) * scale
                dv_ref[sl, :] += _dg(w.astype(BF16), dob, 0, 0)
                return dq, stay_before + stay_sum, dl_before + dl_sum
            return step

        zero = jnp.zeros((tq, 1), F32)
        carry = lax.fori_loop(0, n_full, make_step(False), (jnp.zeros((tq, d), F32), zero, zero))
        dq, _, _ = lax.fori_loop(n_full, n_all, make_step(True), carry)
        dq_ref[...] = dq * scale

    q_spec, k_spec, v_spec, o_spec, r_spec = _att_specs(s_len, tq, d, d)
    shp = jax.ShapeDtypeStruct((n_heads, s_len, d), F32)
    return ride_call(
        job, body, name=name, grid=(n_heads, s_len // tq), in_specs=[q_spec, k_spec, v_spec, r_spec, o_spec],
        out_specs=[q_spec, k_spec, v_spec], out_shape=[shp, shp, shp], ins=[q, k, v, tot, do],
        sem=("arbitrary", "arbitrary"))


NEG_BIG = -1e30


def _lower_left(rows, cols):
    r = lax.broadcasted_iota(jnp.int32, (rows, cols), 0)
    c = lax.broadcasted_iota(jnp.int32, (rows, cols), 1)
    return c <= r


def _prep_specs(tq, q_prep):
    cos, _, gain = q_prep
    rope_spec = pl.BlockSpec((tq, cos.shape[1]), lambda h, i: (i, 0))
    return [rope_spec, rope_spec, pl.BlockSpec(gain.shape, lambda h, i: (0, 0))]


def sm_fwd(q, k, v, *, tq, tk, name, q_prep=None):
    n_heads, s_len, dq = q.shape
    dv = v.shape[2]
    scale = dq ** -0.5
    assert tq == tk
    half = tk // 2
    n_prep = 0 if q_prep is None else 3

    def body(*refs):
        q_ref, prep_refs = refs[0], refs[1:1 + n_prep]
        k_ref, v_ref, o_ref, lse_ref = refs[1 + n_prep:]
        qi = pl.program_id(1)
        qv = q_ref[...]
        if q_prep is not None:
            qv = f_mla_q(qv, *[r[...] for r in prep_refs])[0].astype(BF16)

        def attend(carry, q_rows, keys, keep):
            acc, m, l = carry
            sc = _dg(q_rows, k_ref[keys, :], 1, 1) * scale
            if keep is not None:
                sc = jnp.where(keep, sc, NEG_BIG)
            m_new = jnp.maximum(m, jnp.max(sc, axis=1, keepdims=True))
            p = jnp.exp(sc - m_new)
            fade = jnp.exp(m - m_new)
            return (fade * acc + _dg(p.astype(BF16), v_ref[keys, :], 1, 0), m_new,
                    fade * l + jnp.sum(p, axis=1, keepdims=True))

        carry = (jnp.zeros((tq, dv), F32), jnp.full((tq, 1), NEG_BIG, F32), jnp.zeros((tq, 1), F32))
        carry = lax.fori_loop(
            0, qi, lambda j, c: attend(c, qv, pl.ds(pl.multiple_of(j * tk, tk), tk), None), carry)
        base = pl.multiple_of(qi * tk, tk)
        carry = attend(carry, qv, pl.ds(base, half), _lower_left(tq, half))
        low = attend(tuple(t[half:] for t in carry), qv[half:], pl.ds(pl.multiple_of(base + half, half), half),
                     _lower_left(half, half))
        acc, m, l = (jnp.concatenate([t[:half], u], axis=0) for t, u in zip(carry, low))
        o_ref[...] = acc / l
        lse_ref[...] = m + jnp.log(l)

    q_spec, k_spec, v_spec, o_spec, r_spec = _att_specs(s_len, tq, dq, dv)
    prep = [] if q_prep is None else list(q_prep)
    return pl.pallas_call(
        body, name=name, grid=(n_heads, s_len // tq),
        in_specs=[q_spec] + ([] if q_prep is None else _prep_specs(tq, q_prep)) + [k_spec, v_spec],
        out_specs=[o_spec, r_spec],
        out_shape=[jax.ShapeDtypeStruct((n_heads, s_len, dv), F32), jax.ShapeDtypeStruct((n_heads, s_len, 1), F32)],
        compiler_params=_cparams(("parallel", "arbitrary")),
    )(q, *prep, k, v)


def sm_bwd(q, k, v, o, lse, do, *, tq, tk, name, q_prep=None, job=None):
    n_heads, s_len, dq = q.shape
    dv = v.shape[2]
    scale = dq ** -0.5
    assert tq == tk
    half = tk // 2
    n_prep = 0 if q_prep is None else 3

    def body(*refs):
        q_ref, prep_refs = refs[0], refs[1:1 + n_prep]
        k_ref, v_ref, o_ref, lse_ref, do_ref, dq_ref, dk_ref, dv_ref = refs[1 + n_prep:9 + n_prep]
        head, qi = pl.program_id(0), pl.program_id(1)

        @pl.when(qi == 0)
        def _():
            dk_ref[...] = jnp.zeros_like(dk_ref)
            dv_ref[...] = jnp.zeros_like(dv_ref)

        q_raw = q_ref[...]
        prep_vals = [r[...] for r in prep_refs]
        qv = q_raw if q_prep is None else f_mla_q(q_raw, *prep_vals)[0].astype(BF16)
        do = do_ref[...]
        dob = do.astype(BF16)
        delta = jnp.sum(do * o_ref[...], axis=1, keepdims=True)
        lse_v = lse_ref[...]

        def attend(rows, keys, keep):
            ks, vs = k_ref[keys, :], v_ref[keys, :]
            p = jnp.exp(_dg(qv[rows], ks, 1, 1) * scale - lse_v[rows])
            if keep is not None:
                p = jnp.where(keep, p, 0.0)
            dv_ref[keys, :] += _dg(p.astype(BF16), dob[rows], 0, 0)
            ds = (p * (_dg(dob[rows], vs, 1, 1) - delta[rows]) * scale).astype(BF16)
            dk_ref[keys, :] += _dg(ds, qv[rows], 0, 0)
            return _dg(ds, ks, 1, 0)

        everything = slice(None)
        dq_acc = lax.fori_loop(
            0, qi, lambda j, acc: acc + attend(everything, pl.ds(pl.multiple_of(j * tk, tk), tk), None),
            jnp.zeros((tq, dq), F32))
        base = pl.multiple_of(qi * tk, tk)
        dq_acc = dq_acc + attend(everything, pl.ds(base, half), _lower_left(tq, half))
        low = attend(slice(half, None), pl.ds(pl.multiple_of(base + half, half), half), _lower_left(half, half))
        dq_acc = jnp.concatenate([dq_acc[:half], dq_acc[half:] + low], axis=0)
        if q_prep is None:
            dq_ref[...] = dq_acc
        else:
            cos, sin, gain = prep_vals
            _, pull = jax.vjp(lambda t, g: f_mla_q(t, cos, sin, g)[0], q_raw, gain)
            dq_raw, d_gain = pull(dq_acc)
            dq_ref[...] = dq_raw.astype(dq_ref.dtype)
            dgain_ref = refs[9 + n_prep]
            first = jnp.logical_and(head == 0, qi == 0)

            @pl.when(first)
            def _():
                dgain_ref[...] = d_gain

            @pl.when(jnp.logical_not(first))
            def _():
                dgain_ref[...] += d_gain

    q_spec, k_spec, v_spec, o_spec, r_spec = _att_specs(s_len, tq, dq, dv)
    out_specs = [q_spec, k_spec, v_spec]
    out_shape = [jax.ShapeDtypeStruct((n_heads, s_len, dq), F32 if q_prep is None else BF16),
                 jax.ShapeDtypeStruct((n_heads, s_len, dq), F32), jax.ShapeDtypeStruct((n_heads, s_len, dv), F32)]
    prep, prep_specs = [], []
    if q_prep is not None:
        prep, prep_specs = list(q_prep), _prep_specs(tq, q_prep)
        out_specs.append(prep_specs[2])
        out_shape.append(jax.ShapeDtypeStruct(q_prep[2].shape, F32))
    return ride_call(
        job, body, name=name, grid=(n_heads, s_len // tq),
        in_specs=[q_spec] + prep_specs + [k_spec, v_spec, o_spec, r_spec, o_spec], out_specs=out_specs,
        out_shape=out_shape, ins=[q, *prep, k, v, o, lse, do], sem=("arbitrary", "arbitrary"))


def loss_head(y, target, *, tm, name):
    n_rows, width = y.shape

    def body(y_ref, t_ref, dy_ref, loss_ref):
        diff = y_ref[...] - t_ref[...]
        dy_ref[...] = diff / width
        part = 0.5 * jnp.sum(jnp.mean(diff * diff, axis=-1, keepdims=True), axis=0, keepdims=True)

        @pl.when(pl.program_id(0) == 0)
        def _():
            loss_ref[...] = jnp.zeros_like(loss_ref)

        loss_ref[...] += jnp.broadcast_to(part, loss_ref.shape)

    spec = pl.BlockSpec((tm, width), lambda r: (r, 0))
    dy, loss = pl.pallas_call(
        body, name=name, grid=(n_rows // tm,), in_specs=[spec, spec],
        out_specs=[spec, pl.BlockSpec((8, LANES), lambda r: (0, 0))],
        out_shape=[jax.ShapeDtypeStruct(y.shape, F32), jax.ShapeDtypeStruct((8, LANES), F32)],
        compiler_params=_cparams(("arbitrary",)),
    )(y, target)
    return dy, loss[0, 0]


ADAM_TILE_ELEMS = 256 * 1024


def _adam_rows(n_rows, width):
    fits = [t for t in range(16, n_rows + 1, 16) if n_rows % t == 0 and t * width <= ADAM_TILE_ELEMS]
    return max(fits) if fits else n_rows


def adamw(parts, w, m, v, *, name):
    n_layers, n_rows, width = w.shape
    assert len(parts) == n_layers
    tm = _adam_rows(n_rows, width)
    n_tiles = n_rows // tm

    def body(*refs):
        p_refs = refs[:n_layers]
        w_ref, m_ref, v_ref, g_ref, d_ref, nm_ref, nv_ref = refs[n_layers:]
        layer = pl.program_id(0)
        for this, p_ref in enumerate(p_refs):
            @pl.when(layer == this)
            def _(p_ref=p_ref):
                g = p_ref[0].astype(F32)
                for i in range(1, N_DEV):
                    g = g + p_ref[i].astype(F32)
                m_new = ADAM_B1 * m_ref[...] + (1.0 - ADAM_B1) * g
                v_new = ADAM_B2 * v_ref[...] + (1.0 - ADAM_B2) * jnp.square(g)
                m_hat = m_new / (1.0 - ADAM_B1 ** ADAM_STEP)
                v_hat = v_new / (1.0 - ADAM_B2 ** ADAM_STEP)
                g_ref[...] = g
                d_ref[...] = -ADAM_LR * (m_hat / (jnp.sqrt(v_hat) + ADAM_EPS) + ADAM_WD * w_ref[...])
                nm_ref[...] = m_new
                nv_ref[...] = v_new

    def part_spec(this):
        def index(layer, r):
            return 0, jnp.where(layer == this, r, jnp.where(layer < this, 0, n_tiles - 1)), 0
        return pl.BlockSpec((N_DEV, tm, width), index)

    spec = pl.BlockSpec((None, tm, width), lambda layer, r: (layer, r, 0))
    shp = jax.ShapeDtypeStruct(w.shape, F32)
    return pl.pallas_call(
        body, name=name, grid=(n_layers, n_tiles),
        in_specs=[part_spec(this) for this in range(n_layers)] + [spec, spec, spec],
        out_specs=[spec] * 4, out_shape=[shp] * 4, compiler_params=_cparams(("arbitrary", "arbitrary")),
    )(*parts, w, m, v)


def _me():
    return lax.axis_index("x"), lax.axis_index("y"), lax.axis_index("c")


N_PEERS = N_DEV - 1


class CommJob:
    def __init__(self, kind, arrays):
        self.kind, self.arrays, self.n = kind, list(arrays), len(arrays)

    def out_shape(self):
        lead = (N_DEV,) if self.kind == 'gather' else ()
        return [jax.ShapeDtypeStruct(lead + a.shape, a.dtype) for a in self.arrays]

    def scratch(self):
        return [pltpu.SemaphoreType.DMA((N_PEERS * self.n,)), pltpu.SemaphoreType.DMA((N_PEERS * self.n,)),
                pltpu.SemaphoreType.DMA((self.n,))]

    def phases(self, in_refs, out_refs, send_sems, recv_sems, local_sems):
        n = self.n
        x, y, c = _me()

        def remote(i, k, src, dst, to):
            return pltpu.make_async_remote_copy(
                src_ref=src, dst_ref=dst, send_sem=send_sems.at[N_PEERS * i + k],
                recv_sem=recv_sems.at[N_PEERS * i + k], device_id=to, device_id_type=MESH)

        if self.kind == 'gather':
            me, sibling = (x, y, c), (x, y, 1 - c)
            chips = [(1 - x, y), (x, 1 - y), (1 - x, 1 - y)]

            def slot(i, px, py, pc):
                return out_refs[i].at[4 * px + 2 * py + pc]

            def copy(i, k, blk, to, src=None):
                return remote(i, k, slot(i, *blk) if src is None else src, slot(i, *blk), to)

            def mine():
                return [pltpu.make_async_copy(in_refs[i], slot(i, *me), local_sems.at[i]) for i in range(n)]

            def first():
                cps = []
                for i in range(n):
                    cps.append(copy(i, 0, me, sibling, src=in_refs[i]))
                    cps += [copy(i, 1 + j, me, (*chip, c), src=in_refs[i]) for j, chip in enumerate(chips)]
                return cps

            def passed():
                return [copy(i, 4 + j, (*chip, c), sibling) for j, chip in enumerate(chips) for i in range(n)]

            def start():
                for cp in mine() + first():
                    cp.start()

            def forward():
                for j, chip in enumerate(chips):
                    for i in range(n):
                        copy(i, 1 + j, (*chip, c), me).wait_recv()
                        copy(i, 4 + j, (*chip, c), sibling).start()

            def finish():
                for i in range(n):
                    copy(i, 0, sibling, me).wait_recv()
                    for j, chip in enumerate(chips):
                        copy(i, 4 + j, (*chip, 1 - c), me).wait_recv()
                for cp in first() + passed():
                    cp.wait_send()
                for cp in mine():
                    cp.wait()

            return start, forward, finish

        my_slot = 4 * x + 2 * y + c

        def mine():
            return [pltpu.make_async_copy(in_refs[i].at[my_slot], out_refs[i].at[my_slot], local_sems.at[i])
                    for i in range(n)]

        def copies():
            cps = []
            for k in range(1, N_DEV):
                px, py, pc = x ^ (k >> 2), y ^ ((k >> 1) & 1), c ^ (k & 1)
                cps += [remote(i, k - 1, in_refs[i].at[4 * px + 2 * py + pc], out_refs[i].at[my_slot], (px, py, pc))
                        for i in range(n)]
            return cps

        def start():
            for cp in mine() + copies():
                cp.start()

        def finish():
            for cp in copies():
                cp.wait_recv()
            for cp in copies():
                cp.wait_send()
            for cp in mine():
                cp.wait()

        return start, (lambda: None), finish


def comm_call(kind, arrays, *, name):
    job = CommJob(kind, arrays)
    n = job.n

    def body(*refs):
        start, forward, finish = job.phases(refs[:n], refs[n:2 * n], *refs[2 * n:])
        start()
        forward()
        finish()

    hbm = pl.BlockSpec(memory_space=pl.ANY)
    return pl.pallas_call(body, name=name, out_shape=job.out_shape(), in_specs=[hbm] * n, out_specs=[hbm] * n,
                          scratch_shapes=job.scratch())(*job.arrays)


def ride_call(job, compute, *, name, grid, in_specs, out_specs, out_shape, ins, sem, scratch=()):
    scratch = list(scratch)
    if job is None:
        return pl.pallas_call(compute, name=name, grid=grid, in_specs=in_specs, out_specs=out_specs,
                              out_shape=out_shape, scratch_shapes=scratch, compiler_params=_cparams(sem))(*ins), None
    n, n_in, n_out, n_scr = job.n, len(ins), len(out_shape), len(scratch)
    n_steps = 1
    for g in grid:
        n_steps *= g

    def body(*refs):
        ins_, job_ins = refs[:n_in], refs[n_in:n_in + n]
        outs, job_outs = refs[n_in + n:n_in + n + n_out], refs[n_in + n + n_out:n_in + 2 * n + n_out]
        rest = refs[n_in + 2 * n + n_out:]
        start, forward, finish = job.phases(job_ins, job_outs, *rest[n_scr:])
        now = 0
        for axis, g in enumerate(grid):
            now = now * g + pl.program_id(axis)
        pl.when(now == 0)(start)
        pl.when(now == n_steps // 2)(forward)
        compute(*ins_, *outs, *rest[:n_scr])
        pl.when(now == n_steps - 1)(finish)

    hbm = pl.BlockSpec(memory_space=pl.ANY)
    res = pl.pallas_call(
        body, name=name, grid=grid, in_specs=list(in_specs) + [hbm] * n,
        out_specs=list(out_specs) + [hbm] * n, out_shape=list(out_shape) + job.out_shape(),
        scratch_shapes=scratch + job.scratch(), compiler_params=_cparams(("arbitrary",) * len(grid)),
    )(*ins, *job.arrays)
    return res[:n_out], res[n_out:]


def to_heads(t, n_heads):
    s_len = t.shape[0]
    return t.reshape(s_len, n_heads, -1).transpose(1, 0, 2)


def from_heads(t):
    return t.transpose(1, 0, 2).reshape(t.shape[1], -1)


def gathered_to_full(t, axis):
    shp = t.shape[1:]
    return jnp.moveaxis(t, 0, axis).reshape(shp[:axis] + (N_DEV * shp[axis],) + shp[axis + 1:])


def full_to_owner_major(g, axis):
    shp = g.shape
    t = jnp.moveaxis(g.reshape(shp[:axis] + (N_DEV, shp[axis] // N_DEV) + shp[axis + 1:]), axis, 0)
    return t.reshape(N_DEV, -1, t.shape[-1])


def _small_rows(shape):
    n = 1
    for s in shape:
        n *= s
    return -(-n // LANES)


def pack_small(arrs, shapes):
    pieces = []
    for n in SMALL:
        flat = arrs[n].reshape(-1)
        flat = jnp.pad(flat, (0, _small_rows(shapes[n]) * LANES - flat.shape[0]))
        pieces.append(flat.reshape(-1, LANES))
    flat = jnp.concatenate(pieces, axis=0)
    return jnp.pad(flat, ((0, -flat.shape[0] % SMALL_ROW_MULTIPLE), (0, 0)))


def unpack_small(flat, shapes):
    out, r = {}, 0
    for n in SMALL:
        rows = _small_rows(shapes[n])
        size = 1
        for s in shapes[n]:
            size *= s
        out[n] = flat[r:r + rows].reshape(-1)[:size].reshape(shapes[n])
        r += rows
    return out


ROW_TM = 256
XATT_TM = 1024
HEAD_TM = 1024
SG_TM = 8 * SG_CHUNK
SB_TILES = (512, 512)
SM_TILE = 1024


def _norm_fwd(x, g, name):
    return prow(f_rms, [x], params=[g.reshape(1, -1)], outs=[(x.shape[1], BF16, False)], tm=ROW_TM, name=name)[0]


def _norm_bwd(x, g, dh, add, name, want_row=True):
    res = prow_vjp(f_rms, [x], params=[g.reshape(1, -1)], cts=[dh], row_grad=[want_row],
                   adds=[add] if want_row else None, tm=ROW_TM, name=name)
    return (res[0], res[1].reshape(-1)) if want_row else (None, res[0].reshape(-1))


def _out_proj(a, w, x, next_gain, alpha, name):
    if next_gain is None:
        return pmm(a, w, res=x, alpha=alpha, name=name), None
    return pmm(a, w, res=x, alpha=alpha, norm_out=next_gain.reshape(1, -1), name=name)


def _in_proj_bwd(d, w, x, gain, dy, name, **kw):
    dx, g_gain = pmm(d, w, tb=True, norm_bwd=(x, gain.reshape(1, -1), dy), name=name, **kw)
    return dx, g_gain.reshape(-1)


def ffn_fwd(x, h, p, tag, next_gain, job=None, after_job=None):
    (gate, up, act), landed = ffn_gate_up(h, p['w_gu'], name=f"{tag}_gu", job=job)
    if job is not None:
        after_job(landed)
    out = _out_proj(act, p['w_down'], x, next_gain, 0.5, f"{tag}_down")
    return out, (x, h, gate, up, act)


def _no_rider(run, **own):
    return run(None)[0]


def _pmm_pair(*args, job, **kw):
    out = pmm(*args, job=job, **kw)
    return out if job is not None else (out, None)


def ffn_bwd(dy, p, saved, tag, with_job=_no_rider):
    x, h, gate, up, act = saved
    d_gate, d_up = with_job(lambda job: ffn_gate_up_bwd(dy, p['w_down'], gate, up, alpha=0.5, name=f"{tag}_dact",
                                                        job=job))
    g_down = pmm(act, dy, ta=True, out_dtype=GRAD_WIRE, alpha=0.5, name=f"{tag}_gdown")
    g_gate = with_job(lambda job: _pmm_pair(h, d_gate, ta=True, out_dtype=GRAD_WIRE, name=f"{tag}_ggate", job=job),
                      w_down=g_down)
    g_gu = jnp.concatenate([g_gate, pmm(h, d_up, ta=True, out_dtype=GRAD_WIRE, name=f"{tag}_gup")], axis=1)
    dx, g_norm = with_job(
        lambda job: _pmm_pair(d_gate, p['w_gu'], a2=d_up, tb=True, norm_bwd=(x, p['norm'].reshape(1, -1), dy),
                              name=f"{tag}_dh", job=job), w_gu=g_gu)
    return dx, {'norm': g_norm.reshape(-1), 'w_gu': g_gu, 'w_down': g_down}


def even_mixer_fwd(x, h, p, next_gain, job=None, after_job=None):
    proj = pmm(h, p['w_in'], name="sbg_in")
    q, k, v = (to_heads(proj[:, i * SB_WIDTH:(i + 1) * SB_WIDTH], SB_HEADS).astype(BF16) for i in range(3))
    (o_sb, tot), landed = sb_fwd(q, k, v, tq=SB_TILES[0], tk=SB_TILES[1], name="sb_fwd", job=job)
    if job is not None:
        after_job(landed)
    z = proj[:, 3 * SB_WIDTH:]
    ln_g, ln_b = p['ln_gain'].reshape(1, -1), p['ln_bias'].reshape(1, -1)
    u, gn = prow(f_gate_prep, [z], params=[ln_g, ln_b], outs=[(SG_WIDTH, F32, False)] * 2, tm=ROW_TM,
                 name="sgu_prep")
    gn_g, u_g = to_heads(gn, SG_GROUPS), to_heads(u, SG_GROUPS)
    b3 = p['sgu_b'].reshape(SG_GROUPS, SG_CHUNK, 1)
    o_sg = prow(f_spatial_gate, [gn_g, u_g], gparams=[p['sgu_w'], b3], outs=[(SG_GROUP_DIM, F32, True)],
                tm=SG_TM, name="sgu_mix")[0]
    cat = jnp.concatenate([from_heads(o_sb), from_heads(o_sg)], axis=-1).astype(BF16)
    out = _out_proj(cat, p['w_out'], x, next_gain, 1.0, "sbg_out")
    return out, (x, h, q, k, v, tot, z, gn_g, u_g, b3, cat)


def even_mixer_bwd(dy, p, saved, job_of=None):
    x, h, q, k, v, tot, z, gn_g, u_g, b3, cat = saved
    d_cat = pmm(dy, p['w_out'], tb=True, name="sbg_dcat")
    g_out = pmm(cat, dy, ta=True, out_dtype=GRAD_WIRE, name="sbg_gout")
    d_osb = to_heads(d_cat[:, :SB_WIDTH], SB_HEADS)
    d_osg = to_heads(d_cat[:, SB_WIDTH:], SG_GROUPS)
    d_gn_g, d_u_g, g_w, g_b = prow_vjp(f_spatial_gate, [gn_g, u_g], gparams=[p['sgu_w'], b3], cts=[d_osg],
                                       row_grad=[True, True], tm=SG_TM, name="sgu_dmix")
    ln_g, ln_b = p['ln_gain'].reshape(1, -1), p['ln_bias'].reshape(1, -1)
    d_z, g_lng, g_lnb = prow_vjp(f_gate_prep, [z], params=[ln_g, ln_b], cts=[from_heads(d_u_g), from_heads(d_gn_g)],
                                 row_grad=[True], row_dtypes=[BF16], tm=ROW_TM, name="sgu_dprep")
    job = None if job_of is None else job_of({'w_out': g_out})
    (dq, dk, dv), landed = sb_bwd(q, k, v, tot, d_osb, tq=SB_TILES[0], tk=SB_TILES[1], name="sb_bwd", job=job)
    d_proj = jnp.concatenate([from_heads(dq).astype(BF16), from_heads(dk).astype(BF16), from_heads(dv).astype(BF16),
                              d_z], axis=-1)
    g_in = pmm(h, d_proj, ta=True, out_dtype=GRAD_WIRE, name="sbg_gin")
    dx, g_norm = _in_proj_bwd(d_proj, p['w_in'], x, p['norm'], dy, "sbg_dh")
    return dx, {'norm': g_norm, 'w_in': g_in, 'ln_gain': g_lng.reshape(-1), 'ln_bias': g_lnb.reshape(-1),
                'sgu_w': g_w, 'sgu_b': g_b.reshape(SG_GROUPS, SG_CHUNK), 'w_out': g_out}, landed


def mla_fwd(x, h, cos, sin, p, next_gain):
    proj = pmm(h, p['w_in'], name="mla_in")
    c_q, c_kv, k_r = proj[:, :MLA_Q_LORA], proj[:, MLA_Q_LORA:MLA_Q_LORA + MLA_KV_LORA], proj[:, MLA_Q_LORA + MLA_KV_LORA:]
    cqn = _norm_fwd(c_q, p['q_lora_gain'], "mla_qlora_norm")
    ckvn = _norm_fwd(c_kv, p['kv_lora_gain'], "mla_kvlora_norm")
    q_h = to_heads(pmm(cqn, p['w_uq'], name="mla_uq"), MLA_HEADS)
    kv_h = to_heads(pmm(ckvn, p['w_ukv'], name="mla_ukv"), MLA_HEADS)
    k_nope, v = kv_h[..., :MLA_NOPE], kv_h[..., MLA_NOPE:].astype(BF16)
    q_g, k_g = p['q_gain'].reshape(1, -1), p['k_gain'].reshape(1, -1)
    kp = prow(f_mla_k, [k_nope, k_r, cos, sin], params=[k_g], outs=[(MLA_QK, BF16, True)], tm=HEAD_TM,
              name="mla_kprep")[0]
    o, lse = sm_fwd(q_h, kp, v, tq=SM_TILE, tk=SM_TILE, name="mla_att_fwd", q_prep=(cos, sin, q_g))
    o_flat = from_heads(o).astype(BF16)
    out = _out_proj(o_flat, p['w_out'], x, next_gain, 1.0, "mla_out")
    return out, (x, h, c_q, c_kv, k_r, cqn, ckvn, q_h, k_nope, v, kp, o, lse, o_flat, q_g, k_g)


def mla_bwd(dy, cos, sin, p, saved, job_of=None):
    x, h, c_q, c_kv, k_r, cqn, ckvn, q_h, k_nope, v, kp, o, lse, o_flat, q_g, k_g = saved
    do = to_heads(pmm(dy, p['w_out'], tb=True, name="mla_do"), MLA_HEADS)
    g_out = pmm(o_flat, dy, ta=True, out_dtype=GRAD_WIRE, name="mla_gout")
    job = None if job_of is None else job_of({'w_out': g_out})
    (dq_h, dkp, dv, g_qg), landed = sm_bwd(q_h, kp, v, o, lse, do, tq=SM_TILE, tk=SM_TILE, name="mla_att_bwd",
                                           q_prep=(cos, sin, q_g), job=job)
    dk_nope, dk_r, g_kg = prow_vjp(f_mla_k, [k_nope, k_r, cos, sin], params=[k_g], cts=[dkp],
                                   row_grad=[True, True, False, False], tm=HEAD_TM, name="mla_dkprep")
    d_q = from_heads(dq_h)
    d_kv = from_heads(jnp.concatenate([dk_nope, dv], axis=-1)).astype(BF16)
    g_uq = pmm(cqn, d_q, ta=True, out_dtype=GRAD_WIRE, name="mla_guq")
    d_cqn = pmm(d_q, p['w_uq'], tb=True, name="mla_dcqn")
    g_ukv = pmm(ckvn, d_kv, ta=True, out_dtype=GRAD_WIRE, name="mla_gukv")
    d_ckvn = pmm(d_kv, p['w_ukv'], tb=True, name="mla_dckvn")
    d_cq, g_qlora = _norm_bwd(c_q, p['q_lora_gain'], d_cqn, None, "mla_dqlora_norm")
    d_ckv, g_kvlora = _norm_bwd(c_kv, p['kv_lora_gain'], d_ckvn, None, "mla_dkvlora_norm")
    d_proj = jnp.concatenate([d_cq, d_ckv, dk_r], axis=-1).astype(BF16)
    g_in = pmm(h, d_proj, ta=True, out_dtype=GRAD_WIRE, name="mla_gin")
    dx, g_norm = _in_proj_bwd(d_proj, p['w_in'], x, p['norm'], dy, "mla_dh")
    return dx, {'norm': g_norm, 'w_in': g_in, 'q_lora_gain': g_qlora, 'kv_lora_gain': g_kvlora, 'w_uq': g_uq,
                'w_ukv': g_ukv, 'q_gain': g_qg.reshape(-1), 'k_gain': g_kg.reshape(-1), 'w_out': g_out}, landed


def xattn_fwd(x, hq, mem, p, tag, next_gain):
    hm = _norm_fwd(mem, p['mem_norm'], f"{tag}_mem_norm")
    q_h = ColGroups(pmm(hq, p['wq'], name=f"{tag}_q"), MEM_HEAD_DIM)
    kv = pmm(hm, p['wkv'], name=f"{tag}_kv").reshape(mem.shape[0], MEM_HEADS, 2 * MEM_HEAD_DIM).transpose(1, 0, 2)
    k_h, v_h = kv[..., :MEM_HEAD_DIM], kv[..., MEM_HEAD_DIM:]
    q_g, k_g = p['q_gain'].reshape(1, -1), p['k_gain'].reshape(1, -1)
    o_flat = prow(f_xattn, [q_h], gparams=[k_h, v_h], params=[q_g, k_g], outs=[(MEM_HEAD_DIM, BF16, 'cols')],
                  tm=XATT_TM, name=f"{tag}_att")[0]
    out = _out_proj(o_flat, p['wo'], x, next_gain, 1.0, f"{tag}_out")
    return out, (x, mem, hq, hm, q_h, k_h, v_h, q_g, k_g, o_flat)


def xattn_bwd(dy, p, saved, tag):
    x, mem, hq, hm, q_h, k_h, v_h, q_g, k_g, o_flat = saved
    d_o = ColGroups(pmm(dy, p['wo'], tb=True, name=f"{tag}_do"), MEM_HEAD_DIM)
    g_wo = pmm(o_flat, dy, ta=True, out_dtype=GRAD_WIRE, name=f"{tag}_gwo")
    d_q, dk_h, dv_h, g_qg, g_kg = prow_vjp(f_xattn, [q_h], gparams=[k_h, v_h], params=[q_g, k_g], cts=[d_o],
                                           row_grad=[True], row_dtypes=[BF16], tm=XATT_TM, name=f"{tag}_datt")
    d_kv = jnp.concatenate([dk_h, dv_h], axis=-1).transpose(1, 0, 2).reshape(mem.shape[0], -1).astype(BF16)
    g_wq = pmm(hq, d_q, ta=True, out_dtype=GRAD_WIRE, name=f"{tag}_gwq")
    dx, g_norm = _in_proj_bwd(d_q, p['wq'], x, p['norm'], dy, f"{tag}_dhq")
    g_wkv = pmm(hm, d_kv, ta=True, out_dtype=GRAD_WIRE, name=f"{tag}_gwkv")
    dhm = pmm(d_kv, p['wkv'], tb=True, name=f"{tag}_dhm")
    _, g_mem_norm = _norm_bwd(mem, p['mem_norm'], dhm, None, f"{tag}_dmem_norm", want_row=False)
    return dx, {'norm': g_norm, 'mem_norm': g_mem_norm, 'wq': g_wq, 'wkv': g_wkv, 'q_gain': g_qg.reshape(-1),
                'k_gain': g_kg.reshape(-1), 'wo': g_wo}


def rope_tables(positions):
    half = MLA_ROPE // 2
    inv_freq = ROPE_THETA ** (-jnp.arange(half, dtype=F32) / half)
    ang = positions.astype(F32)[:, None] * inv_freq
    cos, sin = jnp.cos(ang), jnp.sin(ang)
    lead = jnp.ones((ang.shape[0], MLA_NOPE), F32)
    return jnp.concatenate([lead, cos, cos], axis=1), jnp.concatenate([0.0 * lead, sin, sin], axis=1)


FIRST_UNIT = ('ffn_pre_w_gu', 0)
EARLY_UNITS = [('ffn_pre_w_down', 0), ('sbg_w_in', 0)]


def local_step(x, mem, positions, target, w, shards):
    cos, sin = rope_tables(positions)
    full = {}

    def absorb(units, gathered):
        for (n, layer), t in zip(units, gathered):
            full[(n, layer)] = gathered_to_full(t, BIG[n] - 1)

    late_units = [u for u in shards if u != FIRST_UNIT and u not in EARLY_UNITS]
    absorb([FIRST_UNIT], comm_call('gather', [shards[FIRST_UNIT]], name="gather_weights_first"))
    first_ffn_p = {'norm': w['ffn_pre_norm'][0], 'w_gu': full[FIRST_UNIT]}

    def ffn_params(kind, layer):
        return {'norm': w[f'ffn_{kind}_norm'][layer], 'w_gu': full[(f'ffn_{kind}_w_gu', layer)],
                'w_down': full[(f'ffn_{kind}_w_down', layer)]}

    def xattn_params(layer):
        return {'norm': w['xmem_norm'][layer], 'mem_norm': w['xmem_mem_norm'][layer], 'wq': full[('xmem_wq', layer)],
                'wkv': full[('xmem_wkv', layer)], 'q_gain': w['xmem_q_gain'][layer], 'k_gain': w['xmem_k_gain'][layer],
                'wo': full[('xmem_wo', layer)]}

    even_p = {'norm': w['mix_norm'][0], 'ln_gain': w['sgu_ln_gain'][0], 'ln_bias': w['sgu_ln_bias'][0],
              'sgu_w': w['sgu_w'][0], 'sgu_b': w['sgu_b'][0]}

    def early_weights_landed(gathered):
        absorb(EARLY_UNITS, gathered)
        first_ffn_p['w_down'] = full[('ffn_pre_w_down', 0)]
        even_p['w_in'] = full[('sbg_w_in', 0)]

    def late_weights_landed(gathered):
        absorb(late_units, gathered)
        even_p['w_out'] = full[('sbg_w_out', 0)]

    def mla_params():
        return {'norm': w['mix_norm'][1], 'w_in': full[('mla_w_in', 0)], 'q_lora_gain': w['mla_q_lora_gain'][0],
                'kv_lora_gain': w['mla_kv_lora_gain'][0], 'w_uq': full[('mla_w_uq', 0)],
                'w_ukv': full[('mla_w_ukv', 0)], 'q_gain': w['mla_q_gain'][0], 'k_gain': w['mla_k_gain'][0],
                'w_out': full[('mla_w_out', 0)]}

    saved = []
    h = _norm_fwd(x, w['ffn_pre_norm'][0], "ffn_pre0_norm")
    for layer in range(DEPTH):
        if layer == 0:
            (x, h), s_pre = ffn_fwd(x, h, first_ffn_p, "ffn_pre0", w['mix_norm'][0],
                                    job=CommJob('gather', [shards[u] for u in EARLY_UNITS]),
                                    after_job=early_weights_landed)
        else:
            (x, h), s_pre = ffn_fwd(x, h, ffn_params('pre', layer), f"ffn_pre{layer}", w['mix_norm'][layer])
        if layer % 2 == 0:
            (x, h), s_mix = even_mixer_fwd(x, h, even_p, w['xmem_norm'][layer],
                                           job=CommJob('gather', [shards[u] for u in late_units]),
                                           after_job=late_weights_landed)
        else:
            (x, h), s_mix = mla_fwd(x, h, cos, sin, mla_params(), w['xmem_norm'][layer])
        (x, h), s_x = xattn_fwd(x, h, mem, xattn_params(layer), f"xmem{layer}", w['ffn_post_norm'][layer])
        following = w['ffn_pre_norm'][layer + 1] if layer + 1 < DEPTH else None
        (x, h), s_post = ffn_fwd(x, h, ffn_params('post', layer), f"ffn_post{layer}", following)
        saved.append((s_pre, s_mix, s_x, s_post))

    dx, loss = loss_head(x, target, tm=ROW_TM, name="loss_head")

    ready, riding, landed = {}, [], {}

    def offer(name, layer, g):
        ready[(name, layer)] = full_to_owner_major(g, BIG[name] - 1)

    def ride(name):
        def job_of(own):
            offer(name, 0, own['w_out'])
            riding[:] = list(ready)
            return CommJob('exchange', [ready.pop(u) for u in riding])
        return job_of

    def last_rides(run, **own):
        for kind, g in own.items():
            offer('ffn_pre_' + kind, 0, g)
        units = list(ready)
        if not units:
            return run(None)[0]
        res, arrived = run(CommJob('exchange', [ready.pop(u) for u in units]))
        landed.update(zip(units, arrived))
        return res

    per_layer = []
    for layer in reversed(range(DEPTH)):
        s_pre, s_mix, s_x, s_post = saved[layer]
        dx, g_post = ffn_bwd(dx, ffn_params('post', layer), s_post, f"ffn_post{layer}")
        offer('ffn_post_w_gu', layer, g_post['w_gu'])
        offer('ffn_post_w_down', layer, g_post['w_down'])
        dx, g_x = xattn_bwd(dx, xattn_params(layer), s_x, f"xmem{layer}")
        for n in ('wq', 'wkv', 'wo'):
            offer('xmem_' + n, layer, g_x[n])
        if layer % 2 == 0:
            dx, g_mix, arrived = even_mixer_bwd(dx, even_p, s_mix, job_of=ride('sbg_w_out'))
            landed.update(zip(riding, arrived))
            offer('sbg_w_in', 0, g_mix['w_in'])
        else:
            dx, g_mix, arrived = mla_bwd(dx, cos, sin, mla_params(), s_mix, job_of=ride('mla_w_out'))
            landed.update(zip(riding, arrived))
            for n in ('w_in', 'w_uq', 'w_ukv'):
                offer('mla_' + n, 0, g_mix[n])
        if layer == 0:
            dx, g_pre = ffn_bwd(dx, ffn_params('pre', layer), s_pre, f"ffn_pre{layer}", with_job=last_rides)
        else:
            dx, g_pre = ffn_bwd(dx, ffn_params('pre', layer), s_pre, f"ffn_pre{layer}")
            offer('ffn_pre_w_gu', layer, g_pre['w_gu'])
            offer('ffn_pre_w_down', layer, g_pre['w_down'])
        per_layer.append((layer, g_pre, g_mix, g_x, g_post))
    per_layer.sort(key=lambda t: t[0])
    assert not ready

    def stack(pick):
        return jnp.stack([pick(t) for t in per_layer])

    g_even, g_mla = per_layer[0][2], per_layer[1][2]
    small_grads = {
        'ffn_pre_norm': stack(lambda t: t[1]['norm']), 'mix_norm': stack(lambda t: t[2]['norm']),
        'sgu_ln_gain': g_even['ln_gain'][None], 'sgu_ln_bias': g_even['ln_bias'][None],
        'sgu_w': g_even['sgu_w'][None], 'sgu_b': g_even['sgu_b'][None],
        'mla_q_lora_gain': g_mla['q_lora_gain'][None], 'mla_kv_lora_gain': g_mla['kv_lora_gain'][None],
        'mla_q_gain': g_mla['q_gain'][None], 'mla_k_gain': g_mla['k_gain'][None],
        'xmem_norm': stack(lambda t: t[3]['norm']), 'xmem_mem_norm': stack(lambda t: t[3]['mem_norm']),
        'xmem_q_gain': stack(lambda t: t[3]['q_gain']), 'xmem_k_gain': stack(lambda t: t[3]['k_gain']),
        'ffn_post_norm': stack(lambda t: t[4]['norm']),
    }
    return loss, dx, small_grads, landed


def _device_slot():
    x, y, c = _me()
    return 4 * x + 2 * y + c


def kernel(x, mem, positions, ffn_pre_norm, ffn_pre_w_gu, ffn_pre_w_down, mix_norm, sbg_w_in, sgu_ln_gain, sgu_ln_bias, sgu_w, sgu_b, sbg_w_out, mla_w_in, mla_q_lora_gain, mla_kv_lora_gain, mla_w_uq, mla_w_ukv, mla_q_gain, mla_k_gain, mla_w_out, xmem_norm, xmem_mem_norm, xmem_wq, xmem_wkv, xmem_q_gain, xmem_k_gain, xmem_wo, ffn_post_norm, ffn_post_w_gu, ffn_post_w_down, loss_target, m_ffn_pre_norm, m_ffn_pre_w_gu, m_ffn_pre_w_down, m_mix_norm, m_sbg_w_in, m_sgu_ln_gain, m_sgu_ln_bias, m_sgu_w, m_sgu_b, m_sbg_w_out, m_mla_w_in, m_mla_q_lora_gain, m_mla_kv_lora_gain, m_mla_w_uq, m_mla_w_ukv, m_mla_q_gain, m_mla_k_gain, m_mla_w_out, m_xmem_norm, m_xmem_mem_norm, m_xmem_wq, m_xmem_wkv, m_xmem_q_gain, m_xmem_k_gain, m_xmem_wo, m_ffn_post_norm, m_ffn_post_w_gu, m_ffn_post_w_down, v_ffn_pre_norm, v_ffn_pre_w_gu, v_ffn_pre_w_down, v_mix_norm, v_sbg_w_in, v_sgu_ln_gain, v_sgu_ln_bias, v_sgu_w, v_sgu_b, v_sbg_w_out, v_mla_w_in, v_mla_q_lora_gain, v_mla_kv_lora_gain, v_mla_w_uq, v_mla_w_ukv, v_mla_q_gain, v_mla_k_gain, v_mla_w_out, v_xmem_norm, v_xmem_mem_norm, v_xmem_wq, v_xmem_wkv, v_xmem_q_gain, v_xmem_k_gain, v_xmem_wo, v_ffn_post_norm, v_ffn_post_w_gu, v_ffn_post_w_down):
    args = locals()
    w_in = {n: args[n] for n in WEIGHTS}
    m_in = {n: args["m_" + n] for n in WEIGHTS}
    v_in = {n: args["v_" + n] for n in WEIGHTS}
    slot = _device_slot()

    tiny = jnp.zeros((8, LANES), F32)
    for i, src in enumerate((w_in, m_in, v_in)):
        tiny = tiny.at[i, :64].set(src['mla_q_lora_gain'][0]).at[i + 3, :32].set(src['mla_kv_lora_gain'][0])
    tiny_all = comm_call('gather', [tiny], name="gather_lora_gains")[0]
    full_small = []
    for i, src in enumerate((w_in, m_in, v_in)):
        d = {n: src[n] for n in SMALL}
        d['mla_q_lora_gain'] = tiny_all[:, i, :64].reshape(1, MLA_Q_LORA)
        d['mla_kv_lora_gain'] = tiny_all[:, i + 3, :32].reshape(1, MLA_KV_LORA)
        full_small.append(d)
    w_small, m_small, v_small = full_small
    small_shapes = {n: w_small[n].shape for n in SMALL}

    shards = {(n, layer): w_in[n][layer].astype(BF16) for n in BIG for layer in range(w_in[n].shape[0])}
    loss, dx, grads, landed = local_step(x[0], mem[0], positions[0], loss_target[0], w_small, shards)
    loss = lax.psum(loss, ("x", "y", "c"))
    big_out = {n: adamw([landed[(n, layer)] for layer in range(w_in[n].shape[0])], w_in[n], m_in[n], v_in[n],
                        name=f"adamw_{n}") for n in BIG}

    small_parts = comm_call('gather', [pack_small(grads, small_shapes)], name="gather_small_grads")
    small_out = adamw(small_parts, pack_small(w_small, small_shapes)[None], pack_small(m_small, small_shapes)[None],
                      pack_small(v_small, small_shapes)[None], name="adamw_small")
    small_out = [unpack_small(t[0], small_shapes) for t in small_out]
    for d in small_out:
        for n, width in zip(GAIN_SHARDED, (64, 32)):
            d[n] = lax.dynamic_slice(d[n], (0, slot * width), (1, width))

    outs = [loss, dx[None]]
    for kind, small_d in enumerate(small_out):
        outs += [big_out[n][kind] if n in BIG else small_d[n] for n in WEIGHTS]
    return tuple(outs)
```

```python
import jax
import jax.numpy as jnp
from jax import lax
from jax.experimental import pallas as pl
from jax.experimental.pallas import tpu as pltpu

F32 = jnp.float32
BF16 = jnp.bfloat16
MESH = pl.DeviceIdType.MESH
N_DEV = 8

VMEM_LIMIT_BYTES = 56 * 1024 * 1024
LANES = 128

D_MODEL = 1024
DEPTH = 2
EPS = 1e-6
SB_HEADS, SB_HEAD_DIM = 8, 64
SB_WIDTH = SB_HEADS * SB_HEAD_DIM
SG_GROUPS, SG_GROUP_DIM, SG_CHUNK = 8, 64, 128
SG_WIDTH = SG_GROUPS * SG_GROUP_DIM
MLA_HEADS, MLA_NOPE, MLA_ROPE, MLA_V = 16, 64, 32, 64
MLA_QK = MLA_NOPE + MLA_ROPE
MLA_Q_LORA, MLA_KV_LORA = 512, 256
ROPE_THETA = 10000.0
MEM_HEADS = 4
MEM_HEAD_DIM = D_MODEL // MEM_HEADS

ADAM_LR, ADAM_B1, ADAM_B2, ADAM_EPS, ADAM_WD, ADAM_STEP = 0.001, 0.9, 0.999, 1e-08, 0.01, 10

WEIGHTS = ['ffn_pre_norm', 'ffn_pre_w_gu', 'ffn_pre_w_down', 'mix_norm', 'sbg_w_in', 'sgu_ln_gain', 'sgu_ln_bias',
           'sgu_w', 'sgu_b', 'sbg_w_out', 'mla_w_in', 'mla_q_lora_gain', 'mla_kv_lora_gain', 'mla_w_uq', 'mla_w_ukv',
           'mla_q_gain', 'mla_k_gain', 'mla_w_out', 'xmem_norm', 'xmem_mem_norm', 'xmem_wq', 'xmem_wkv',
           'xmem_q_gain', 'xmem_k_gain', 'xmem_wo', 'ffn_post_norm', 'ffn_post_w_gu', 'ffn_post_w_down']
BIG = {'ffn_pre_w_gu': 2, 'ffn_pre_w_down': 1, 'sbg_w_in': 2, 'sbg_w_out': 1, 'mla_w_in': 1, 'mla_w_uq': 2,
       'mla_w_ukv': 2, 'mla_w_out': 1, 'xmem_wq': 1, 'xmem_wkv': 2, 'xmem_wo': 1, 'ffn_post_w_gu': 2,
       'ffn_post_w_down': 1}
GAIN_SHARDED = ('mla_q_lora_gain', 'mla_kv_lora_gain')
SMALL = [n for n in WEIGHTS if n not in BIG]
GRAD_WIRE = BF16
FFN_SAVE = BF16
SMALL_ROW_MULTIPLE = 16


def _cparams(sem=None):
    return pltpu.CompilerParams(dimension_semantics=sem, vmem_limit_bytes=VMEM_LIMIT_BYTES)


MM_TILE_CAP = 1408


def _pick(dim, cap=MM_TILE_CAP):
    if dim % LANES:
        return dim
    return max(t for t in range(LANES, min(dim, cap) + 1, LANES) if dim % t == 0)


def _rms(x, g):
    return x * lax.rsqrt(jnp.mean(x * x, axis=-1, keepdims=True) + EPS) * g


def pmm(a, b, *, a2=None, ta=False, tb=False, out_dtype=F32, res=None, alpha=1.0, norm_out=None, norm_bwd=None,
        job=None, name):
    kdim, m = (a.shape if ta else a.shape[::-1])
    n = b.shape[0] if tb else b.shape[1]
    tm, tn, tk = _pick(m), _pick(n), _pick(kdim)
    whole_rows = norm_out is not None or norm_bwd is not None
    if whole_rows:
        assert tn == n
    if norm_bwd is not None:
        tm = min(tm, 512)
    nk1 = kdim // tk
    nk = nk1 if a2 is None else 2 * nk1
    assert a2 is None or (a2.shape == a.shape and not ta)
    dims = (((0 if ta else 1,), (1 if tb else 0,)), ((), ()))
    n_lead = 2 if a2 is None else 3
    n_extra = (res is not None) + (norm_out is not None) + (0 if norm_bwd is None else 2 + (norm_bwd[2] is not None))

    def body(*refs):
        a_ref, b_ref = refs[:2]
        extra = list(refs[n_lead:n_lead + n_extra])
        outs, acc_ref = refs[n_lead + n_extra:-1], refs[-1]
        i, k = pl.program_id(0), pl.program_id(2)

        @pl.when(k == 0)
        def _():
            acc_ref[...] = jnp.zeros_like(acc_ref)

        def accumulate(lhs_ref):
            acc_ref[...] += lax.dot_general(lhs_ref[...].astype(BF16), b_ref[...].astype(BF16), dims,
                                            preferred_element_type=F32)

        if a2 is None:
            accumulate(a_ref)
        else:
            pl.when(k < nk1)(lambda: accumulate(a_ref))
            pl.when(k >= nk1)(lambda: accumulate(refs[2]))

        @pl.when(k == nk - 1)
        def _():
            r = acc_ref[...]
            if alpha != 1.0:
                r = r * alpha
            if res is not None:
                r = extra.pop(0)[...] + r
            if norm_bwd is None:
                outs[0][...] = r.astype(out_dtype)
            if norm_out is not None:
                outs[1][...] = _rms(r, extra.pop(0)[...]).astype(BF16)
            if norm_bwd is not None:
                x_ref, g_ref = extra.pop(0), extra.pop(0)
                _, pull = jax.vjp(_rms, x_ref[...], g_ref[...])
                dx, dg = pull(r)
                if norm_bwd[2] is not None:
                    dx = dx + extra.pop(0)[...]
                outs[0][...] = dx

                @pl.when(i == 0)
                def _():
                    outs[1][...] = dg

                @pl.when(i != 0)
                def _():
                    outs[1][...] += dg

    gi, gj = m // tm, n // tn
    a_bytes, b_bytes = a.size * a.dtype.itemsize, (n * kdim) * b.dtype.itemsize
    j_outer = not whole_rows and nk == 1 and gj * a_bytes + b_bytes < a_bytes + gi * b_bytes
    grid = (gj, gi, nk) if j_outer else (gi, gj, nk)

    def spec(block, index):
        return pl.BlockSpec(block, (lambda j, i, k: index(i, j, k)) if j_outer else index)

    a_spec = spec((tk, tm), lambda i, j, k: (k, i)) if ta else spec((tm, tk), lambda i, j, k: (i, k))
    b_spec = spec((tn, tk), lambda i, j, k: (j, k)) if tb else spec((tk, tn), lambda i, j, k: (k, j))
    o_spec = spec((tm, tn), lambda i, j, k: (i, j))
    g_spec = spec((1, tn), lambda i, j, k: (0, 0))
    ins, in_specs = [a, b], [a_spec, b_spec]
    if a2 is not None:
        in_specs[0] = spec((tm, tk), lambda i, j, k: (i, jnp.minimum(k, nk1 - 1)))
        ins.append(a2)
        in_specs.append(spec((tm, tk), lambda i, j, k: (i, jnp.maximum(k - nk1, 0))))
    if res is not None:
        ins.append(res)
        in_specs.append(o_spec)
    out_shape, out_specs = [jax.ShapeDtypeStruct((m, n), out_dtype)], [o_spec]
    if norm_out is not None:
        ins.append(norm_out)
        in_specs.append(g_spec)
        out_shape.append(jax.ShapeDtypeStruct((m, n), BF16))
        out_specs.append(o_spec)
    if norm_bwd is not None:
        ins += [t for t in norm_bwd if t is not None]
        in_specs += [o_spec, g_spec] + ([o_spec] if norm_bwd[2] is not None else [])
        out_shape = [jax.ShapeDtypeStruct((m, n), F32), jax.ShapeDtypeStruct((1, n), F32)]
        out_specs = [o_spec, g_spec]
    result, landed = ride_call(
        job, body, name=name, grid=grid, in_specs=in_specs, out_specs=out_specs, out_shape=out_shape, ins=ins,
        scratch=[pltpu.VMEM((tm, tn), F32)],
        sem=("arbitrary" if norm_bwd is not None else "parallel", "parallel", "arbitrary"))
    result = result if whole_rows else result[0]
    return result if job is None else (result, landed)


def ffn_gate_up(h, w_gu, *, name, job=None):
    m, kdim = h.shape
    n = w_gu.shape[1] // 2
    tm, tn = min(_pick(m), 512), _pick(n)
    up_off = n // tn

    def body(a_ref, bg_ref, bu_ref, gate_ref, up_ref, act_ref):
        av = a_ref[...].astype(BF16)
        gate = _dg(av, bg_ref[...].astype(BF16), 1, 0)
        up = _dg(av, bu_ref[...].astype(BF16), 1, 0)
        gate_ref[...] = gate.astype(gate_ref.dtype)
        up_ref[...] = up.astype(up_ref.dtype)
        act_ref[...] = (jax.nn.silu(gate) * up).astype(BF16)

    o_spec = pl.BlockSpec((tm, tn), lambda j, i: (i, j))
    return ride_call(
        job, body, name=name, grid=(n // tn, m // tm),
        in_specs=[pl.BlockSpec((tm, kdim), lambda j, i: (i, 0)), pl.BlockSpec((kdim, tn), lambda j, i: (0, j)),
                  pl.BlockSpec((kdim, tn), lambda j, i: (0, j + up_off))],
        out_specs=[o_spec] * 3,
        out_shape=[jax.ShapeDtypeStruct((m, n), FFN_SAVE), jax.ShapeDtypeStruct((m, n), FFN_SAVE),
                   jax.ShapeDtypeStruct((m, n), BF16)],
        ins=[h, w_gu, w_gu], sem=("parallel", "parallel"))


def ffn_gate_up_bwd(dy, w_down, gate, up, *, alpha, name, job=None):
    m, kdim = dy.shape
    n = w_down.shape[0]
    tm, tn = min(_pick(m), 512), _pick(n)

    def body(a_ref, b_ref, gate_ref, up_ref, dgate_ref, dup_ref):
        d_act = _dg(a_ref[...].astype(BF16), b_ref[...].astype(BF16), 1, 1) * alpha
        _, pull = jax.vjp(lambda g, u: jax.nn.silu(g) * u, gate_ref[...].astype(F32), up_ref[...].astype(F32))
        d_gate, d_up = pull(d_act)
        dgate_ref[...] = d_gate.astype(BF16)
        dup_ref[...] = d_up.astype(BF16)

    o_spec = pl.BlockSpec((tm, tn), lambda j, i: (i, j))
    return ride_call(
        job, body, name=name, grid=(n // tn, m // tm),
        in_specs=[pl.BlockSpec((tm, kdim), lambda j, i: (i, 0)), pl.BlockSpec((tn, kdim), lambda j, i: (j, 0)),
                  o_spec, o_spec],
        out_specs=[o_spec] * 2, out_shape=[jax.ShapeDtypeStruct((m, n), BF16)] * 2,
        ins=[dy, w_down, gate, up], sem=("parallel", "parallel"))


def _dg(a, b, ca, cb):
    return lax.dot_general(a, b, (((ca,), (cb,)), ((), ())), preferred_element_type=F32)


@jax.custom_vjp
def bdot(a, b):
    return _dg(a.astype(BF16), b.astype(BF16), 1, 0)


def _bdot_fwd(a, b):
    ab, bb = a.astype(BF16), b.astype(BF16)
    return _dg(ab, bb, 1, 0), (ab, bb)


def _bdot_bwd(saved, g):
    ab, bb = saved
    gb = g.astype(BF16)
    return _dg(gb, bb, 1, 1), _dg(ab, gb, 0, 0)


bdot.defvjp(_bdot_fwd, _bdot_bwd)


@jax.custom_vjp
def bdot_nt(a, b):
    return _dg(a.astype(BF16), b.astype(BF16), 1, 1)


def _bdot_nt_fwd(a, b):
    ab, bb = a.astype(BF16), b.astype(BF16)
    return _dg(ab, bb, 1, 1), (ab, bb)


def _bdot_nt_bwd(saved, g):
    ab, bb = saved
    gb = g.astype(BF16)
    return _dg(gb, bb, 1, 0), _dg(gb, ab, 0, 0)


bdot_nt.defvjp(_bdot_nt_fwd, _bdot_nt_bwd)


class ColGroups:
    def __init__(self, arr, width):
        self.arr, self.width = arr, width
        self.shape, self.dtype, self.ndim = arr.shape, arr.dtype, 3


def _plain(a):
    return a.arr if isinstance(a, ColGroups) else a


def _row_spec(arr, tm):
    if isinstance(arr, ColGroups):
        return pl.BlockSpec((tm, arr.width), lambda r, g: (r, g))
    if arr.ndim == 3:
        return pl.BlockSpec((None, tm, arr.shape[2]), lambda r, g: (g, r, 0))
    return pl.BlockSpec((tm, arr.shape[1]), lambda r, g: (r, 0))


def _gparam_spec(arr):
    return pl.BlockSpec((None,) + arr.shape[1:], lambda r, g: (g, 0, 0))


def _whole_spec(arr):
    nd = arr.ndim
    return pl.BlockSpec(arr.shape, lambda r, g: (0,) * nd)


def _groups(rows, gparams):
    gs = {a.shape[1] // a.width if isinstance(a, ColGroups) else a.shape[0] for a in rows if a.ndim == 3}
    gs |= {a.shape[0] for a in gparams}
    assert len(gs) <= 1
    return gs.pop() if gs else 1


def prow(fn, rows, gparams=(), params=(), *, outs, tm, name, job=None):
    rows, gparams, params = list(rows), list(gparams), list(params)
    n_groups = _groups(rows, gparams)
    n_rows = rows[0].shape[-2]
    n_in = len(rows) + len(gparams) + len(params)

    def body(*refs):
        vals = [r[...] for r in refs[:n_in]]
        res = fn(*vals)
        for o_ref, r in zip(refs[n_in:], res, strict=True):
            o_ref[...] = r.astype(o_ref.dtype)

    out_shape, out_specs = [], []
    for width, dtype, grouped in outs:
        if grouped == 'cols':
            out_shape.append(jax.ShapeDtypeStruct((n_rows, n_groups * width), dtype))
            out_specs.append(_row_spec(ColGroups(out_shape[-1], width), tm))
            continue
        shp = (n_groups, n_rows, width) if grouped else (n_rows, width)
        out_shape.append(jax.ShapeDtypeStruct(shp, dtype))
        out_specs.append(_row_spec(out_shape[-1], tm))
    result, landed = ride_call(
        job, body, name=name, grid=(n_rows // tm, n_groups),
        in_specs=[_row_spec(a, tm) for a in rows] + [_gparam_spec(a) for a in gparams] + [_whole_spec(a) for a in params],
        out_specs=out_specs, out_shape=out_shape, ins=[*[_plain(a) for a in rows], *gparams, *params],
        sem=("parallel", "arbitrary"))
    return result if job is None else (result, landed)


def prow_vjp(fn, rows, gparams=(), params=(), *, cts, row_grad, adds=None, row_dtypes=None, gparam_grad=None,
             param_grad=None, tm, name):
    rows, gparams, params, cts = list(rows), list(gparams), list(params), list(cts)
    gparam_grad = list(gparam_grad) if gparam_grad is not None else [True] * len(gparams)
    param_grad = list(param_grad) if param_grad is not None else [True] * len(params)
    n_groups = _groups(rows + cts, gparams)
    n_rows = rows[0].shape[-2]
    want_rows = [i for i, w in enumerate(row_grad) if w]
    adds = list(adds) if adds is not None else [None] * len(want_rows)
    row_dtypes = list(row_dtypes) if row_dtypes is not None else [F32] * len(want_rows)
    add_arrays = [a for a in adds if a is not None]
    n_r, n_g, n_p, n_c, n_a = len(rows), len(gparams), len(params), len(cts), len(add_arrays)
    mask = list(row_grad) + gparam_grad + param_grad

    def body(*refs):
        r_id, g_id = pl.program_id(0), pl.program_id(1)
        n_in = n_r + n_g + n_p
        vals = [r[...] for r in refs[:n_in]]
        ct_vals = tuple(r[...].astype(F32) for r in refs[n_in:n_in + n_c])
        add_refs = list(refs[n_in + n_c:n_in + n_c + n_a])
        out_refs = list(refs[n_in + n_c + n_a:])
        diff_idx = [i for i, w in enumerate(mask) if w]

        def wrapped(*diff):
            full = list(vals)
            for i, d in zip(diff_idx, diff):
                full[i] = d
            return tuple(fn(*full))

        _, pull = jax.vjp(wrapped, *[vals[i].astype(F32) for i in diff_idx])
        grads = dict(zip(diff_idx, pull(ct_vals)))
        k = 0
        for j, i in enumerate(want_rows):
            o_ref = out_refs[k]
            k += 1
            gval = grads[i]
            if adds[j] is not None:
                gval = gval + add_refs.pop(0)[...].astype(F32)
            if rows[i].ndim == 2 and n_groups > 1:
                @pl.when(g_id == 0)
                def _(o_ref=o_ref, gval=gval):
                    o_ref[...] = gval.astype(o_ref.dtype)

                @pl.when(g_id != 0)
                def _(o_ref=o_ref, gval=gval):
                    o_ref[...] += gval.astype(o_ref.dtype)
            else:
                o_ref[...] = gval.astype(o_ref.dtype)
        for i in range(n_g):
            if not gparam_grad[i]:
                continue
            o_ref = out_refs[k]
            k += 1
            gval = grads[n_r + i]

            @pl.when(r_id == 0)
            def _(o_ref=o_ref, gval=gval):
                o_ref[g_id] = gval

            @pl.when(r_id != 0)
            def _(o_ref=o_ref, gval=gval):
                o_ref[g_id] += gval
        for i in range(n_p):
            if not param_grad[i]:
                continue
            o_ref = out_refs[k]
            k += 1
            gval = grads[n_r + n_g + i]
            first = jnp.logical_and(r_id == 0, g_id == 0)

            @pl.when(first)
            def _(o_ref=o_ref, gval=gval):
                o_ref[...] = gval

            @pl.when(jnp.logical_not(first))
            def _(o_ref=o_ref, gval=gval):
                o_ref[...] += gval

    out_shape, out_specs = [], []
    for j, i in enumerate(want_rows):
        out_shape.append(jax.ShapeDtypeStruct(rows[i].shape, row_dtypes[j]))
        out_specs.append(_row_spec(rows[i], tm))
    for i in range(n_g):
        if gparam_grad[i]:
            out_shape.append(jax.ShapeDtypeStruct(gparams[i].shape, F32))
            out_specs.append(_whole_spec(gparams[i]))
    for i in range(n_p):
        if param_grad[i]:
            out_shape.append(jax.ShapeDtypeStruct(params[i].shape, F32))
            out_specs.append(_whole_spec(params[i]))
    return pl.pallas_call(
        body, name=name, grid=(n_rows // tm, n_groups),
        in_specs=([_row_spec(a, tm) for a in rows] + [_gparam_spec(a) for a in gparams]
                  + [_whole_spec(a) for a in params] + [_row_spec(a, tm) for a in cts]
                  + [_row_spec(a, tm) for a in add_arrays]),
        out_specs=out_specs, out_shape=out_shape,
        compiler_params=_cparams(("arbitrary", "arbitrary")),
    )(*[_plain(a) for a in rows], *gparams, *params, *[_plain(a) for a in cts], *add_arrays)


def f_rms(x, g):
    return (_rms(x.astype(F32), g),)


def f_gate_prep(z, ln_g, ln_b):
    act = jax.nn.gelu(z)
    u, gg = act[:, :SG_WIDTH], act[:, SG_WIDTH:]
    mu = jnp.mean(gg, axis=-1, keepdims=True)
    var = jnp.mean(jnp.square(gg - mu), axis=-1, keepdims=True)
    return u, (gg - mu) * lax.rsqrt(var + EPS) * ln_g + ln_b


def f_spatial_gate(gn, u, w, b):
    t = lax.broadcasted_iota(jnp.int32, w.shape, 0)
    s = lax.broadcasted_iota(jnp.int32, w.shape, 1)
    w_causal = jnp.where(s <= t, w, 0.0)
    mixed = [bdot(w_causal, gn[i:i + SG_CHUNK]) + b for i in range(0, gn.shape[0], SG_CHUNK)]
    return (u * (mixed[0] if len(mixed) == 1 else jnp.concatenate(mixed, axis=0)),)


def _two_pieces(x):
    hi = x.astype(BF16)
    return hi, (x - hi.astype(F32)).astype(BF16)


@jax.custom_vjp
def place(x, m):
    hi, lo = _two_pieces(x)
    return _dg(hi, m, 1, 0) + _dg(lo, m, 1, 0)


def _place_fwd(x, m):
    return place(x, m), m


def _place_bwd(m, g):
    hi, lo = _two_pieces(g)
    return _dg(hi, m, 1, 1) + _dg(lo, m, 1, 1), jnp.zeros_like(m)


place.defvjp(_place_fwd, _place_bwd)


def _lane_map(rows, cols, entry):
    src = lax.broadcasted_iota(jnp.int32, (rows, cols), 0)
    dst = lax.broadcasted_iota(jnp.int32, (rows, cols), 1)
    return entry(src, dst).astype(BF16)


def _rope_tail(t, cos_w, sin_w):
    half = MLA_ROPE // 2
    lo_half = lambda d: jnp.logical_and(d >= MLA_NOPE, d < MLA_NOPE + half)
    swap = _lane_map(MLA_QK, MLA_QK, lambda s, d: jnp.where(
        jnp.logical_and(d >= MLA_NOPE + half, s == d - half), 1.0,
        jnp.where(jnp.logical_and(lo_half(d), s == d + half), -1.0, 0.0)))
    return t * cos_w + place(t, swap) * sin_w


def f_mla_q(q, cos_w, sin_w, g):
    return (_rope_tail(f_rms(q, g)[0], cos_w, sin_w),)


def f_mla_k(k_nope, k_r, cos_w, sin_w, g):
    side_by_side = (place(k_nope, _lane_map(MLA_NOPE, MLA_QK, lambda s, d: jnp.where(s == d, 1.0, 0.0)))
                    + place(k_r, _lane_map(MLA_ROPE, MLA_QK, lambda s, d: jnp.where(s + MLA_NOPE == d, 1.0, 0.0))))
    return (_rope_tail(f_rms(side_by_side, g)[0], cos_w, sin_w),)


def f_xattn(q, k, v, q_g, k_g):
    qn, kn = f_rms(q, q_g)[0], f_rms(k, k_g)[0]
    sc = bdot_nt(qn, kn) * (MEM_HEAD_DIM ** -0.5)
    return (bdot(jax.nn.softmax(sc, axis=-1), v),)


def _split_dot(x, tri, pieces=2):
    hi = x.astype(BF16)
    if pieces == 1:
        return _dg(hi, tri, 1, 0)
    lo = (x - hi.astype(F32)).astype(BF16)
    return _dg(hi, tri, 1, 0) + _dg(lo, tri, 1, 0)


def _tri(tk, cmp):
    j = lax.broadcasted_iota(jnp.int32, (tk, tk), 0)
    s = lax.broadcasted_iota(jnp.int32, (tk, tk), 1)
    return cmp(j, s).astype(BF16)


SCAN_CHUNK = 256


def _row_scan(x, tri, reverse, pieces=2):
    n = x.shape[1] // SCAN_CHUNK
    chunks = [x[:, i * SCAN_CHUNK:(i + 1) * SCAN_CHUNK] for i in range(n)]
    out, seen = [None] * n, None
    for i in (reversed(range(n)) if reverse else range(n)):
        local = _split_dot(chunks[i], tri, pieces)
        out[i] = local if seen is None else local + seen
        total = jnp.sum(chunks[i], axis=1, keepdims=True)
        seen = total if seen is None else seen + total
    return (out[0] if n == 1 else jnp.concatenate(out, axis=1)), seen


def _att_specs(s_len, tq, dq, dv):
    q_spec = pl.BlockSpec((None, tq, dq), lambda h, i: (h, i, 0))
    k_spec = pl.BlockSpec((None, s_len, dq), lambda h, i: (h, 0, 0))
    v_spec = pl.BlockSpec((None, s_len, dv), lambda h, i: (h, 0, 0))
    o_spec = pl.BlockSpec((None, tq, dv), lambda h, i: (h, i, 0))
    r_spec = pl.BlockSpec((None, tq, 1), lambda h, i: (h, i, 0))
    return q_spec, k_spec, v_spec, o_spec, r_spec


def _key_blocks(qi, tq, tk):
    return (qi * tq) // tk, ((qi + 1) * tq + tk - 1) // tk


def _earlier(qi, j, tq, tk):
    row = qi * tq + lax.broadcasted_iota(jnp.int32, (tq, tk), 0)
    col = j * tk + lax.broadcasted_iota(jnp.int32, (tq, tk), 1)
    return col < row


LOG2_E = 1.4426950408889634


def _log2_sigmoid(z2):
    return jnp.minimum(z2, 0.0) - jnp.log2(1.0 + jnp.exp2(-jnp.abs(z2)))


def sb_fwd(q, k, v, *, tq, tk, name, job=None):
    n_heads, s_len, d = q.shape
    scale2 = SB_HEAD_DIM ** -0.5 * LOG2_E

    def body(q_ref, k_ref, v_ref, o_ref, tot_ref):
        qi = pl.program_id(1)
        qv = q_ref[...]
        upper = _tri(SCAN_CHUNK, lambda j, s: j > s)
        n_full, n_all = _key_blocks(qi, tq, tk)

        def make_step(masked, last):
            def step(jj, carry):
                acc, rest = carry
                j = last - 1 - jj
                sl = pl.ds(pl.multiple_of(j * tk, tk), tk)
                ks, vs = k_ref[sl, :], v_ref[sl, :]
                z2 = _dg(qv, ks, 1, 1) * scale2
                log_beta = _log2_sigmoid(z2)
                log_stay = log_beta - z2
                if masked:
                    valid = _earlier(qi, j, tq, tk)
                    log_stay = jnp.where(valid, log_stay, 0.0)
                after, total = _row_scan(log_stay, upper, True)
                w = jnp.exp2(log_beta + after + rest)
                if masked:
                    w = jnp.where(valid, w, 0.0)
                acc = acc + _dg(w.astype(BF16), vs, 1, 0)
                return acc, rest + total
            return step

        carry = (jnp.zeros((tq, d), F32), jnp.zeros((tq, 1), F32))
        carry = lax.fori_loop(0, n_all - n_full, make_step(True, n_all), carry)
        acc, rest = lax.fori_loop(0, n_full, make_step(False, n_full), carry)
        o_ref[...] = acc
        tot_ref[...] = rest

    q_spec, k_spec, v_spec, o_spec, r_spec = _att_specs(s_len, tq, d, d)
    return ride_call(
        job, body, name=name, grid=(n_heads, s_len // tq), in_specs=[q_spec, k_spec, v_spec],
        out_specs=[o_spec, r_spec],
        out_shape=[jax.ShapeDtypeStruct((n_heads, s_len, d), F32), jax.ShapeDtypeStruct((n_heads, s_len, 1), F32)],
        ins=[q, k, v], sem=("parallel", "arbitrary"))


def sb_bwd(q, k, v, tot, do, *, tq, tk, name, job=None):
    n_heads, s_len, d = q.shape
    scale = SB_HEAD_DIM ** -0.5
    scale2 = scale * LOG2_E

    def body(q_ref, k_ref, v_ref, tot_ref, do_ref, dq_ref, dk_ref, dv_ref):
        qi = pl.program_id(1)

        @pl.when(qi == 0)
        def _():
            dk_ref[...] = jnp.zeros_like(dk_ref)
            dv_ref[...] = jnp.zeros_like(dv_ref)

        qv = q_ref[...]
        dob = do_ref[...].astype(BF16)
        total = tot_ref[...]
        incl = _tri(SCAN_CHUNK, lambda j, s: j <= s)
        excl = _tri(SCAN_CHUNK, lambda j, s: j < s)
        n_full, n_all = _key_blocks(qi, tq, tk)

        def make_step(masked):
            def step(j, carry):
                dq, stay_before, dl_before = carry
                sl = pl.ds(pl.multiple_of(j * tk, tk), tk)
                ks, vs = k_ref[sl, :], v_ref[sl, :]
                z2 = _dg(qv, ks, 1, 1) * scale2
                log_beta = _log2_sigmoid(z2)
                log_stay = log_beta - z2
                if masked:
                    valid = _earlier(qi, j, tq, tk)
                    log_stay = jnp.where(valid, log_stay, 0.0)
                stay_upto, stay_sum = _row_scan(log_stay, incl, False)
                w = jnp.exp2(log_beta + (total - stay_before) - stay_upto)
                if masked:
                    w = jnp.where(valid, w, 0.0)
                dl = _dg(dob, vs, 1, 1) * w
                dl_upto, dl_sum = _row_scan(dl, excl, False, pieces=1)
                dl_prefix = dl_upto + dl_before
                beta = jnp.exp2(log_beta)
                dz = dl * (1.0 - beta) - beta * dl_prefix
                if masked:
                    dz = jnp.where(valid, dz, 0.0)
                dzb = dz.astype(BF16)
                dq = dq + _dg(dzb, ks, 1, 0)
                dk_ref[sl, :] += _dg(dzb, qv, 0, 0) * scale
                dv_ref[sl, :] += _dg(w.astype(BF16), dob, 0, 0)
                return dq, stay_before + stay_sum, dl_before + dl_sum
            return step

        zero = jnp.zeros((tq, 1), F32)
        carry = lax.fori_loop(0, n_full, make_step(False), (jnp.zeros((tq, d), F32), zero, zero))
        dq, _, _ = lax.fori_loop(n_full, n_all, make_step(True), carry)
        dq_ref[...] = dq * scale

    q_spec, k_spec, v_spec, o_spec, r_spec = _att_specs(s_len, tq, d, d)
    shp = jax.ShapeDtypeStruct((n_heads, s_len, d), F32)
    return ride_call(
        job, body, name=name, grid=(n_heads, s_len // tq), in_specs=[q_spec, k_spec, v_spec, r_spec, o_spec],
        out_specs=[q_spec, k_spec, v_spec], out_shape=[shp, shp, shp], ins=[q, k, v, tot, do],
        sem=("arbitrary", "arbitrary"))


NEG_BIG = -1e30


def _lower_left(rows, cols):
    r = lax.broadcasted_iota(jnp.int32, (rows, cols), 0)
    c = lax.broadcasted_iota(jnp.int32, (rows, cols), 1)
    return c <= r


def _prep_specs(tq, q_prep):
    cos, _, gain = q_prep
    rope_spec = pl.BlockSpec((tq, cos.shape[1]), lambda h, i: (i, 0))
    return [rope_spec, rope_spec, pl.BlockSpec(gain.shape, lambda h, i: (0, 0))]


def sm_fwd(q, k, v, *, tq, tk, name, q_prep=None):
    n_heads, s_len, dq = q.shape
    dv = v.shape[2]
    scale = dq ** -0.5
    assert tq == tk
    half = tk // 2
    n_prep = 0 if q_prep is None else 3

    def body(*refs):
        q_ref, prep_refs = refs[0], refs[1:1 + n_prep]
        k_ref, v_ref, o_ref, lse_ref = refs[1 + n_prep:]
        qi = pl.program_id(1)
        qv = q_ref[...]
        if q_prep is not None:
            qv = f_mla_q(qv, *[r[...] for r in prep_refs])[0].astype(BF16)

        def attend(carry, q_rows, keys, keep):
            acc, m, l = carry
            sc = _dg(q_rows, k_ref[keys, :], 1, 1) * scale
            if keep is not None:
                sc = jnp.where(keep, sc, NEG_BIG)
            m_new = jnp.maximum(m, jnp.max(sc, axis=1, keepdims=True))
            p = jnp.exp(sc - m_new)
            fade = jnp.exp(m - m_new)
            return (fade * acc + _dg(p.astype(BF16), v_ref[keys, :], 1, 0), m_new,
                    fade * l + jnp.sum(p, axis=1, keepdims=True))

        carry = (jnp.zeros((tq, dv), F32), jnp.full((tq, 1), NEG_BIG, F32), jnp.zeros((tq, 1), F32))
        carry = lax.fori_loop(
            0, qi, lambda j, c: attend(c, qv, pl.ds(pl.multiple_of(j * tk, tk), tk), None), carry)
        base = pl.multiple_of(qi * tk, tk)
        carry = attend(carry, qv, pl.ds(base, half), _lower_left(tq, half))
        low = attend(tuple(t[half:] for t in carry), qv[half:], pl.ds(pl.multiple_of(base + half, half), half),
                     _lower_left(half, half))
        acc, m, l = (jnp.concatenate([t[:half], u], axis=0) for t, u in zip(carry, low))
        o_ref[...] = acc / l
        lse_ref[...] = m + jnp.log(l)

    q_spec, k_spec, v_spec, o_spec, r_spec = _att_specs(s_len, tq, dq, dv)
    prep = [] if q_prep is None else list(q_prep)
    return pl.pallas_call(
        body, name=name, grid=(n_heads, s_len // tq),
        in_specs=[q_spec] + ([] if q_prep is None else _prep_specs(tq, q_prep)) + [k_spec, v_spec],
        out_specs=[o_spec, r_spec],
        out_shape=[jax.ShapeDtypeStruct((n_heads, s_len, dv), F32), jax.ShapeDtypeStruct((n_heads, s_len, 1), F32)],
        compiler_params=_cparams(("parallel", "arbitrary")),
    )(q, *prep, k, v)


def sm_bwd(q, k, v, o, lse, do, *, tq, tk, name, q_prep=None, job=None):
    n_heads, s_len, dq = q.shape
    dv = v.shape[2]
    scale = dq ** -0.5
    assert tq == tk
    half = tk // 2
    n_prep = 0 if q_prep is None else 3

    def body(*refs):
        q_ref, prep_refs = refs[0], refs[1:1 + n_prep]
        k_ref, v_ref, o_ref, lse_ref, do_ref, dq_ref, dk_ref, dv_ref = refs[1 + n_prep:9 + n_prep]
        head, qi = pl.program_id(0), pl.program_id(1)

        @pl.when(qi == 0)
        def _():
            dk_ref[...] = jnp.zeros_like(dk_ref)
            dv_ref[...] = jnp.zeros_like(dv_ref)

        q_raw = q_ref[...]
        prep_vals = [r[...] for r in prep_refs]
        qv = q_raw if q_prep is None else f_mla_q(q_raw, *prep_vals)[0].astype(BF16)
        do = do_ref[...]
        dob = do.astype(BF16)
        delta = jnp.sum(do * o_ref[...], axis=1, keepdims=True)
        lse_v = lse_ref[...]

        def attend(rows, keys, keep):
            ks, vs = k_ref[keys, :], v_ref[keys, :]
            p = jnp.exp(_dg(qv[rows], ks, 1, 1) * scale - lse_v[rows])
            if keep is not None:
                p = jnp.where(keep, p, 0.0)
            dv_ref[keys, :] += _dg(p.astype(BF16), dob[rows], 0, 0)
            ds = (p * (_dg(dob[rows], vs, 1, 1) - delta[rows]) * scale).astype(BF16)
            dk_ref[keys, :] += _dg(ds, qv[rows], 0, 0)
            return _dg(ds, ks, 1, 0)

        everything = slice(None)
        dq_acc = lax.fori_loop(
            0, qi, lambda j, acc: acc + attend(everything, pl.ds(pl.multiple_of(j * tk, tk), tk), None),
            jnp.zeros((tq, dq), F32))
        base = pl.multiple_of(qi * tk, tk)
        dq_acc = dq_acc + attend(everything, pl.ds(base, half), _lower_left(tq, half))
        low = attend(slice(half, None), pl.ds(pl.multiple_of(base + half, half), half), _lower_left(half, half))
        dq_acc = jnp.concatenate([dq_acc[:half], dq_acc[half:] + low], axis=0)
        if q_prep is None:
            dq_ref[...] = dq_acc
        else:
            cos, sin, gain = prep_vals
            _, pull = jax.vjp(lambda t, g: f_mla_q(t, cos, sin, g)[0], q_raw, gain)
            dq_raw, d_gain = pull(dq_acc)
            dq_ref[...] = dq_raw.astype(dq_ref.dtype)
            dgain_ref = refs[9 + n_prep]
            first = jnp.logical_and(head == 0, qi == 0)

            @pl.when(first)
            def _():
                dgain_ref[...] = d_gain

            @pl.when(jnp.logical_not(first))
            def _():
                dgain_ref[...] += d_gain

    q_spec, k_spec, v_spec, o_spec, r_spec = _att_specs(s_len, tq, dq, dv)
    out_specs = [q_spec, k_spec, v_spec]
    out_shape = [jax.ShapeDtypeStruct((n_heads, s_len, dq), F32 if q_prep is None else BF16),
                 jax.ShapeDtypeStruct((n_heads, s_len, dq), F32), jax.ShapeDtypeStruct((n_heads, s_len, dv), F32)]
    prep, prep_specs = [], []
    if q_prep is not None:
        prep, prep_specs = list(q_prep), _prep_specs(tq, q_prep)
        out_specs.append(prep_specs[2])
        out_shape.append(jax.ShapeDtypeStruct(q_prep[2].shape, F32))
    return ride_call(
        job, body, name=name, grid=(n_heads, s_len // tq),
        in_specs=[q_spec] + prep_specs + [k_spec, v_spec, o_spec, r_spec, o_spec], out_specs=out_specs,
        out_shape=out_shape, ins=[q, *prep, k, v, o, lse, do], sem=("arbitrary", "arbitrary"))


def loss_head(y, target, *, tm, name):
    n_rows, width = y.shape

    def body(y_ref, t_ref, dy_ref, loss_ref):
        diff = y_ref[...] - t_ref[...]
        dy_ref[...] = diff / width
        part = 0.5 * jnp.sum(jnp.mean(diff * diff, axis=-1, keepdims=True), axis=0, keepdims=True)

        @pl.when(pl.program_id(0) == 0)
        def _():
            loss_ref[...] = jnp.zeros_like(loss_ref)

        loss_ref[...] += jnp.broadcast_to(part, loss_ref.shape)

    spec = pl.BlockSpec((tm, width), lambda r: (r, 0))
    dy, loss = pl.pallas_call(
        body, name=name, grid=(n_rows // tm,), in_specs=[spec, spec],
        out_specs=[spec, pl.BlockSpec((8, LANES), lambda r: (0, 0))],
        out_shape=[jax.ShapeDtypeStruct(y.shape, F32), jax.ShapeDtypeStruct((8, LANES), F32)],
        compiler_params=_cparams(("arbitrary",)),
    )(y, target)
    return dy, loss[0, 0]


ADAM_TILE_ELEMS = 256 * 1024


def _adam_rows(n_rows, width):
    fits = [t for t in range(16, n_rows + 1, 16) if n_rows % t == 0 and t * width <= ADAM_TILE_ELEMS]
    return max(fits) if fits else n_rows


def adamw(parts, w, m, v, *, name):
    n_layers, n_rows, width = w.shape
    assert len(parts) == n_layers
    tm = _adam_rows(n_rows, width)
    n_tiles = n_rows // tm

    def body(*refs):
        p_refs = refs[:n_layers]
        w_ref, m_ref, v_ref, g_ref, d_ref, nm_ref, nv_ref = refs[n_layers:]
        layer = pl.program_id(0)
        for this, p_ref in enumerate(p_refs):
            @pl.when(layer == this)
            def _(p_ref=p_ref):
                g = p_ref[0].astype(F32)
                for i in range(1, N_DEV):
                    g = g + p_ref[i].astype(F32)
                m_new = ADAM_B1 * m_ref[...] + (1.0 - ADAM_B1) * g
                v_new = ADAM_B2 * v_ref[...] + (1.0 - ADAM_B2) * jnp.square(g)
                m_hat = m_new / (1.0 - ADAM_B1 ** ADAM_STEP)
                v_hat = v_new / (1.0 - ADAM_B2 ** ADAM_STEP)
                g_ref[...] = g
                d_ref[...] = -ADAM_LR * (m_hat / (jnp.sqrt(v_hat) + ADAM_EPS) + ADAM_WD * w_ref[...])
                nm_ref[...] = m_new
                nv_ref[...] = v_new

    def part_spec(this):
        def index(layer, r):
            return 0, jnp.where(layer == this, r, jnp.where(layer < this, 0, n_tiles - 1)), 0
        return pl.BlockSpec((N_DEV, tm, width), index)

    spec = pl.BlockSpec((None, tm, width), lambda layer, r: (layer, r, 0))
    shp = jax.ShapeDtypeStruct(w.shape, F32)
    return pl.pallas_call(
        body, name=name, grid=(n_layers, n_tiles),
        in_specs=[part_spec(this) for this in range(n_layers)] + [spec, spec, spec],
        out_specs=[spec] * 4, out_shape=[shp] * 4, compiler_params=_cparams(("arbitrary", "arbitrary")),
    )(*parts, w, m, v)


def _me():
    return lax.axis_index("x"), lax.axis_index("y"), lax.axis_index("c")


N_PEERS = N_DEV - 1


class CommJob:
    def __init__(self, kind, arrays):
        self.kind, self.arrays, self.n = kind, list(arrays), len(arrays)

    def out_shape(self):
        lead = (N_DEV,) if self.kind == 'gather' else ()
        return [jax.ShapeDtypeStruct(lead + a.shape, a.dtype) for a in self.arrays]

    def scratch(self):
        return [pltpu.SemaphoreType.DMA((N_PEERS * self.n,)), pltpu.SemaphoreType.DMA((N_PEERS * self.n,)),
                pltpu.SemaphoreType.DMA((self.n,))]

    def phases(self, in_refs, out_refs, send_sems, recv_sems, local_sems):
        n = self.n
        x, y, c = _me()

        def remote(i, k, src, dst, to):
            return pltpu.make_async_remote_copy(
                src_ref=src, dst_ref=dst, send_sem=send_sems.at[N_PEERS * i + k],
                recv_sem=recv_sems.at[N_PEERS * i + k], device_id=to, device_id_type=MESH)

        if self.kind == 'gather':
            me, sibling = (x, y, c), (x, y, 1 - c)
            chips = [(1 - x, y), (x, 1 - y), (1 - x, 1 - y)]

            def slot(i, px, py, pc):
                return out_refs[i].at[4 * px + 2 * py + pc]

            def copy(i, k, blk, to, src=None):
                return remote(i, k, slot(i, *blk) if src is None else src, slot(i, *blk), to)

            def mine():
                return [pltpu.make_async_copy(in_refs[i], slot(i, *me), local_sems.at[i]) for i in range(n)]

            def first():
                cps = []
                for i in range(n):
                    cps.append(copy(i, 0, me, sibling, src=in_refs[i]))
                    cps += [copy(i, 1 + j, me, (*chip, c), src=in_refs[i]) for j, chip in enumerate(chips)]
                return cps

            def passed():
                return [copy(i, 4 + j, (*chip, c), sibling) for j, chip in enumerate(chips) for i in range(n)]

            def start():
                for cp in mine() + first():
                    cp.start()

            def forward():
                for j, chip in enumerate(chips):
                    for i in range(n):
                        copy(i, 1 + j, (*chip, c), me).wait_recv()
                        copy(i, 4 + j, (*chip, c), sibling).start()

            def finish():
                for i in range(n):
                    copy(i, 0, sibling, me).wait_recv()
                    for j, chip in enumerate(chips):
                        copy(i, 4 + j, (*chip, 1 - c), me).wait_recv()
                for cp in first() + passed():
                    cp.wait_send()
                for cp in mine():
                    cp.wait()

            return start, forward, finish

        my_slot = 4 * x + 2 * y + c

        def mine():
            return [pltpu.make_async_copy(in_refs[i].at[my_slot], out_refs[i].at[my_slot], local_sems.at[i])
                    for i in range(n)]

        def copies():
            cps = []
            for k in range(1, N_DEV):
                px, py, pc = x ^ (k >> 2), y ^ ((k >> 1) & 1), c ^ (k & 1)
                cps += [remote(i, k - 1, in_refs[i].at[4 * px + 2 * py + pc], out_refs[i].at[my_slot], (px, py, pc))
                        for i in range(n)]
            return cps

        def start():
            for cp in mine() + copies():
                cp.start()

        def finish():
            for cp in copies():
                cp.wait_recv()
            for cp in copies():
                cp.wait_send()
            for cp in mine():
                cp.wait()

        return start, (lambda: None), finish


def comm_call(kind, arrays, *, name):
    job = CommJob(kind, arrays)
    n = job.n

    def body(*refs):
        start, forward, finish = job.phases(refs[:n], refs[n:2 * n], *refs[2 * n:])
        start()
        forward()
        finish()

    hbm = pl.BlockSpec(memory_space=pl.ANY)
    return pl.pallas_call(body, name=name, out_shape=job.out_shape(), in_specs=[hbm] * n, out_specs=[hbm] * n,
                          scratch_shapes=job.scratch())(*job.arrays)


def ride_call(job, compute, *, name, grid, in_specs, out_specs, out_shape, ins, sem, scratch=()):
    scratch = list(scratch)
    if job is None:
        return pl.pallas_call(compute, name=name, grid=grid, in_specs=in_specs, out_specs=out_specs,
                              out_shape=out_shape, scratch_shapes=scratch, compiler_params=_cparams(sem))(*ins), None
    n, n_in, n_out, n_scr = job.n, len(ins), len(out_shape), len(scratch)
    n_steps = 1
    for g in grid:
        n_steps *= g

    def body(*refs):
        ins_, job_ins = refs[:n_in], refs[n_in:n_in + n]
        outs, job_outs = refs[n_in + n:n_in + n + n_out], refs[n_in + n + n_out:n_in + 2 * n + n_out]
        rest = refs[n_in + 2 * n + n_out:]
        start, forward, finish = job.phases(job_ins, job_outs, *rest[n_scr:])
        now = 0
        for axis, g in enumerate(grid):
            now = now * g + pl.program_id(axis)
        pl.when(now == 0)(start)
        pl.when(now == n_steps // 2)(forward)
        compute(*ins_, *outs, *rest[:n_scr])
        pl.when(now == n_steps - 1)(finish)

    hbm = pl.BlockSpec(memory_space=pl.ANY)
    res = pl.pallas_call(
        body, name=name, grid=grid, in_specs=list(in_specs) + [hbm] * n,
        out_specs=list(out_specs) + [hbm] * n, out_shape=list(out_shape) + job.out_shape(),
        scratch_shapes=scratch + job.scratch(), compiler_params=_cparams(("arbitrary",) * len(grid)),
    )(*ins, *job.arrays)
    return res[:n_out], res[n_out:]


def to_heads(t, n_heads):
    s_len = t.shape[0]
    return t.reshape(s_len, n_heads, -1).transpose(1, 0, 2)


def from_heads(t):
    return t.transpose(1, 0, 2).reshape(t.shape[1], -1)


def gathered_to_full(t, axis):
    shp = t.shape[1:]
    return jnp.moveaxis(t, 0, axis).reshape(shp[:axis] + (N_DEV * shp[axis],) + shp[axis + 1:])


def full_to_owner_major(g, axis):
    shp = g.shape
    t = jnp.moveaxis(g.reshape(shp[:axis] + (N_DEV, shp[axis] // N_DEV) + shp[axis + 1:]), axis, 0)
    return t.reshape(N_DEV, -1, t.shape[-1])


def _small_rows(shape):
    n = 1
    for s in shape:
        n *= s
    return -(-n // LANES)


def pack_small(arrs, shapes):
    pieces = []
    for n in SMALL:
        flat = arrs[n].reshape(-1)
        flat = jnp.pad(flat, (0, _small_rows(shapes[n]) * LANES - flat.shape[0]))
        pieces.append(flat.reshape(-1, LANES))
    flat = jnp.concatenate(pieces, axis=0)
    return jnp.pad(flat, ((0, -flat.shape[0] % SMALL_ROW_MULTIPLE), (0, 0)))


def unpack_small(flat, shapes):
    out, r = {}, 0
    for n in SMALL:
        rows = _small_rows(shapes[n])
        size = 1
        for s in shapes[n]:
            size *= s
        out[n] = flat[r:r + rows].reshape(-1)[:size].reshape(shapes[n])
        r += rows
    return out


ROW_TM = 256
XATT_TM = 1024
HEAD_TM = 1024
SG_TM = 8 * SG_CHUNK
SB_TILES = (512, 512)
SM_TILE = 1024


def _norm_fwd(x, g, name):
    return prow(f_rms, [x], params=[g.reshape(1, -1)], outs=[(x.shape[1], BF16, False)], tm=ROW_TM, name=name)[0]


def _norm_bwd(x, g, dh, add, name, want_row=True):
    res = prow_vjp(f_rms, [x], params=[g.reshape(1, -1)], cts=[dh], row_grad=[want_row],
                   adds=[add] if want_row else None, tm=ROW_TM, name=name)
    return (res[0], res[1].reshape(-1)) if want_row else (None, res[0].reshape(-1))


def _out_proj(a, w, x, next_gain, alpha, name):
    if next_gain is None:
        return pmm(a, w, res=x, alpha=alpha, name=name), None
    return pmm(a, w, res=x, alpha=alpha, norm_out=next_gain.reshape(1, -1), name=name)


def _in_proj_bwd(d, w, x, gain, dy, name, **kw):
    dx, g_gain = pmm(d, w, tb=True, norm_bwd=(x, gain.reshape(1, -1), dy), name=name, **kw)
    return dx, g_gain.reshape(-1)


def ffn_fwd(x, h, p, tag, next_gain, job=None, after_job=None):
    (gate, up, act), landed = ffn_gate_up(h, p['w_gu'], name=f"{tag}_gu", job=job)
    if job is not None:
        after_job(landed)
    out = _out_proj(act, p['w_down'], x, next_gain, 0.5, f"{tag}_down")
    return out, (x, h, gate, up, act)


def _no_rider(run, **own):
    return run(None)[0]


def _pmm_pair(*args, job, **kw):
    out = pmm(*args, job=job, **kw)
    return out if job is not None else (out, None)


def ffn_bwd(dy, p, saved, tag, with_job=_no_rider):
    x, h, gate, up, act = saved
    d_gate, d_up = with_job(lambda job: ffn_gate_up_bwd(dy, p['w_down'], gate, up, alpha=0.5, name=f"{tag}_dact",
                                                        job=job))
    g_down = pmm(act, dy, ta=True, out_dtype=GRAD_WIRE, alpha=0.5, name=f"{tag}_gdown")
    g_gate = with_job(lambda job: _pmm_pair(h, d_gate, ta=True, out_dtype=GRAD_WIRE, name=f"{tag}_ggate", job=job),
                      w_down=g_down)
    g_gu = jnp.concatenate([g_gate, pmm(h, d_up, ta=True, out_dtype=GRAD_WIRE, name=f"{tag}_gup")], axis=1)
    dx, g_norm = with_job(
        lambda job: _pmm_pair(d_gate, p['w_gu'], a2=d_up, tb=True, norm_bwd=(x, p['norm'].reshape(1, -1), dy),
                              name=f"{tag}_dh", job=job), w_gu=g_gu)
    return dx, {'norm': g_norm.reshape(-1), 'w_gu': g_gu, 'w_down': g_down}


def even_mixer_fwd(x, h, p, next_gain, job=None, after_job=None):
    qkv = pmm(h, p['w_in'][:, :3 * SB_WIDTH], out_dtype=BF16, name="sbg_in_qkv")
    z = pmm(h, p['w_in'][:, 3 * SB_WIDTH:], name="sbg_in_gate")
    q, k, v = (to_heads(qkv[:, i * SB_WIDTH:(i + 1) * SB_WIDTH], SB_HEADS) for i in range(3))
    (o_sb, tot), landed = sb_fwd(q, k, v, tq=SB_TILES[0], tk=SB_TILES[1], name="sb_fwd", job=job)
    if job is not None:
        after_job(landed)
    ln_g, ln_b = p['ln_gain'].reshape(1, -1), p['ln_bias'].reshape(1, -1)
    u, gn = prow(f_gate_prep, [z], params=[ln_g, ln_b], outs=[(SG_WIDTH, F32, False)] * 2, tm=ROW_TM,
                 name="sgu_prep")
    gn_g, u_g = to_heads(gn, SG_GROUPS), to_heads(u, SG_GROUPS)
    b3 = p['sgu_b'].reshape(SG_GROUPS, SG_CHUNK, 1)
    o_sg = prow(f_spatial_gate, [gn_g, u_g], gparams=[p['sgu_w'], b3], outs=[(SG_GROUP_DIM, F32, True)],
                tm=SG_TM, name="sgu_mix")[0]
    cat = jnp.concatenate([from_heads(o_sb), from_heads(o_sg)], axis=-1).astype(BF16)
    out = _out_proj(cat, p['w_out'], x, next_gain, 1.0, "sbg_out")
    return out, (x, h, q, k, v, tot, z, gn_g, u_g, b3, cat)


def even_mixer_bwd(dy, p, saved, job_of=None):
    x, h, q, k, v, tot, z, gn_g, u_g, b3, cat = saved
    d_osb = to_heads(pmm(dy, p['w_out'][:SB_WIDTH], tb=True, name="sbg_dcat_sb"), SB_HEADS)
    d_osg = to_heads(pmm(dy, p['w_out'][SB_WIDTH:], tb=True, name="sbg_dcat_sg"), SG_GROUPS)
    g_out = pmm(cat, dy, ta=True, out_dtype=GRAD_WIRE, name="sbg_gout")
    d_gn_g, d_u_g, g_w, g_b = prow_vjp(f_spatial_gate, [gn_g, u_g], gparams=[p['sgu_w'], b3], cts=[d_osg],
                                       row_grad=[True, True], tm=SG_TM, name="sgu_dmix")
    ln_g, ln_b = p['ln_gain'].reshape(1, -1), p['ln_bias'].reshape(1, -1)
    d_z, g_lng, g_lnb = prow_vjp(f_gate_prep, [z], params=[ln_g, ln_b], cts=[from_heads(d_u_g), from_heads(d_gn_g)],
                                 row_grad=[True], row_dtypes=[BF16], tm=ROW_TM, name="sgu_dprep")
    job = None if job_of is None else job_of({'w_out': g_out})
    (dq, dk, dv), landed = sb_bwd(q, k, v, tot, d_osb, tq=SB_TILES[0], tk=SB_TILES[1], name="sb_bwd", job=job)
    d_proj = jnp.concatenate([from_heads(dq).astype(BF16), from_heads(dk).astype(BF16), from_heads(dv).astype(BF16),
                              d_z], axis=-1)
    g_in = pmm(h, d_proj, ta=True, out_dtype=GRAD_WIRE, name="sbg_gin")
    dx, g_norm = _in_proj_bwd(d_proj, p['w_in'], x, p['norm'], dy, "sbg_dh")
    return dx, {'norm': g_norm, 'w_in': g_in, 'ln_gain': g_lng.reshape(-1), 'ln_bias': g_lnb.reshape(-1),
                'sgu_w': g_w, 'sgu_b': g_b.reshape(SG_GROUPS, SG_CHUNK), 'w_out': g_out}, landed


def mla_fwd(x, h, cos, sin, p, next_gain):
    lora = MLA_Q_LORA + MLA_KV_LORA
    c_q = pmm(h, p['w_in'][:, :MLA_Q_LORA], name="mla_in_q")
    c_kv = pmm(h, p['w_in'][:, MLA_Q_LORA:lora], name="mla_in_kv")
    k_r = pmm(h, p['w_in'][:, lora:], name="mla_in_rope")
    cqn = _norm_fwd(c_q, p['q_lora_gain'], "mla_qlora_norm")
    ckvn = _norm_fwd(c_kv, p['kv_lora_gain'], "mla_kvlora_norm")
    q_h = to_heads(pmm(cqn, p['w_uq'], name="mla_uq"), MLA_HEADS)
    kv_h = to_heads(pmm(ckvn, p['w_ukv'], name="mla_ukv"), MLA_HEADS)
    k_nope, v = kv_h[..., :MLA_NOPE], kv_h[..., MLA_NOPE:].astype(BF16)
    q_g, k_g = p['q_gain'].reshape(1, -1), p['k_gain'].reshape(1, -1)
    kp = prow(f_mla_k, [k_nope, k_r, cos, sin], params=[k_g], outs=[(MLA_QK, BF16, True)], tm=HEAD_TM,
              name="mla_kprep")[0]
    o, lse = sm_fwd(q_h, kp, v, tq=SM_TILE, tk=SM_TILE, name="mla_att_fwd", q_prep=(cos, sin, q_g))
    o_flat = from_heads(o).astype(BF16)
    out = _out_proj(o_flat, p['w_out'], x, next_gain, 1.0, "mla_out")
    return out, (x, h, c_q, c_kv, k_r, cqn, ckvn, q_h, k_nope, v, kp, o, lse, o_flat, q_g, k_g)


def mla_bwd(dy, cos, sin, p, saved, job_of=None):
    x, h, c_q, c_kv, k_r, cqn, ckvn, q_h, k_nope, v, kp, o, lse, o_flat, q_g, k_g = saved
    do = to_heads(pmm(dy, p['w_out'], tb=True, name="mla_do"), MLA_HEADS)
    g_out = pmm(o_flat, dy, ta=True, out_dtype=GRAD_WIRE, name="mla_gout")
    job = None if job_of is None else job_of({'w_out': g_out})
    (dq_h, dkp, dv, g_qg), landed = sm_bwd(q_h, kp, v, o, lse, do, tq=SM_TILE, tk=SM_TILE, name="mla_att_bwd",
                                           q_prep=(cos, sin, q_g), job=job)
    dk_nope, dk_r, g_kg = prow_vjp(f_mla_k, [k_nope, k_r, cos, sin], params=[k_g], cts=[dkp],
                                   row_grad=[True, True, False, False], tm=HEAD_TM, name="mla_dkprep")
    d_q = from_heads(dq_h)
    d_kv = from_heads(jnp.concatenate([dk_nope, dv], axis=-1)).astype(BF16)
    g_uq = pmm(cqn, d_q, ta=True, out_dtype=GRAD_WIRE, name="mla_guq")
    d_cqn = pmm(d_q, p['w_uq'], tb=True, name="mla_dcqn")
    g_ukv = pmm(ckvn, d_kv, ta=True, out_dtype=GRAD_WIRE, name="mla_gukv")
    d_ckvn = pmm(d_kv, p['w_ukv'], tb=True, name="mla_dckvn")
    d_cq, g_qlora = _norm_bwd(c_q, p['q_lora_gain'], d_cqn, None, "mla_dqlora_norm")
    d_ckv, g_kvlora = _norm_bwd(c_kv, p['kv_lora_gain'], d_ckvn, None, "mla_dkvlora_norm")
    d_proj = jnp.concatenate([d_cq, d_ckv, dk_r], axis=-1).astype(BF16)
    g_in = pmm(h, d_proj, ta=True, out_dtype=GRAD_WIRE, name="mla_gin")
    dx, g_norm = _in_proj_bwd(d_proj, p['w_in'], x, p['norm'], dy, "mla_dh")
    return dx, {'norm': g_norm, 'w_in': g_in, 'q_lora_gain': g_qlora, 'kv_lora_gain': g_kvlora, 'w_uq': g_uq,
                'w_ukv': g_ukv, 'q_gain': g_qg.reshape(-1), 'k_gain': g_kg.reshape(-1), 'w_out': g_out}, landed


def xattn_fwd(x, hq, mem, p, tag, next_gain):
    hm = _norm_fwd(mem, p['mem_norm'], f"{tag}_mem_norm")
    q_h = ColGroups(pmm(hq, p['wq'], name=f"{tag}_q"), MEM_HEAD_DIM)
    kv = pmm(hm, p['wkv'], name=f"{tag}_kv").reshape(mem.shape[0], MEM_HEADS, 2 * MEM_HEAD_DIM).transpose(1, 0, 2)
    k_h, v_h = kv[..., :MEM_HEAD_DIM], kv[..., MEM_HEAD_DIM:]
    q_g, k_g = p['q_gain'].reshape(1, -1), p['k_gain'].reshape(1, -1)
    o_flat = prow(f_xattn, [q_h], gparams=[k_h, v_h], params=[q_g, k_g], outs=[(MEM_HEAD_DIM, BF16, 'cols')],
                  tm=XATT_TM, name=f"{tag}_att")[0]
    out = _out_proj(o_flat, p['wo'], x, next_gain, 1.0, f"{tag}_out")
    return out, (x, mem, hq, hm, q_h, k_h, v_h, q_g, k_g, o_flat)


def xattn_bwd(dy, p, saved, tag):
    x, mem, hq, hm, q_h, k_h, v_h, q_g, k_g, o_flat = saved
    d_o = ColGroups(pmm(dy, p['wo'], tb=True, name=f"{tag}_do"), MEM_HEAD_DIM)
    g_wo = pmm(o_flat, dy, ta=True, out_dtype=GRAD_WIRE, name=f"{tag}_gwo")
    d_q, dk_h, dv_h, g_qg, g_kg = prow_vjp(f_xattn, [q_h], gparams=[k_h, v_h], params=[q_g, k_g], cts=[d_o],
                                           row_grad=[True], row_dtypes=[BF16], tm=XATT_TM, name=f"{tag}_datt")
    d_kv = jnp.concatenate([dk_h, dv_h], axis=-1).transpose(1, 0, 2).reshape(mem.shape[0], -1).astype(BF16)
    g_wq = pmm(hq, d_q, ta=True, out_dtype=GRAD_WIRE, name=f"{tag}_gwq")
    dx, g_norm = _in_proj_bwd(d_q, p['wq'], x, p['norm'], dy, f"{tag}_dhq")
    g_wkv = pmm(hm, d_kv, ta=True, out_dtype=GRAD_WIRE, name=f"{tag}_gwkv")
    dhm = pmm(d_kv, p['wkv'], tb=True, name=f"{tag}_dhm")
    _, g_mem_norm = _norm_bwd(mem, p['mem_norm'], dhm, None, f"{tag}_dmem_norm", want_row=False)
    return dx, {'norm': g_norm, 'mem_norm': g_mem_norm, 'wq': g_wq, 'wkv': g_wkv, 'q_gain': g_qg.reshape(-1),
                'k_gain': g_kg.reshape(-1), 'wo': g_wo}


def rope_tables(positions):
    half = MLA_ROPE // 2
    inv_freq = ROPE_THETA ** (-jnp.arange(half, dtype=F32) / half)
    ang = positions.astype(F32)[:, None] * inv_freq
    cos, sin = jnp.cos(ang), jnp.sin(ang)
    lead = jnp.ones((ang.shape[0], MLA_NOPE), F32)
    return jnp.concatenate([lead, cos, cos], axis=1), jnp.concatenate([0.0 * lead, sin, sin], axis=1)


FIRST_UNIT = ('ffn_pre_w_gu', 0)
EARLY_UNITS = [('ffn_pre_w_down', 0), ('sbg_w_in', 0)]


def local_step(x, mem, positions, target, w, shards):
    cos, sin = rope_tables(positions)
    full = {}

    def absorb(units, gathered):
        for (n, layer), t in zip(units, gathered):
            full[(n, layer)] = gathered_to_full(t, BIG[n] - 1)

    late_units = [u for u in shards if u != FIRST_UNIT and u not in EARLY_UNITS]
    (h,), gathered = prow(f_rms, [x], params=[w['ffn_pre_norm'][0].reshape(1, -1)], outs=[(x.shape[1], BF16, False)],
                          tm=ROW_TM, name="ffn_pre0_norm", job=CommJob('gather', [shards[FIRST_UNIT]]))
    absorb([FIRST_UNIT], gathered)
    first_ffn_p = {'norm': w['ffn_pre_norm'][0], 'w_gu': full[FIRST_UNIT]}

    def ffn_params(kind, layer):
        return {'norm': w[f'ffn_{kind}_norm'][layer], 'w_gu': full[(f'ffn_{kind}_w_gu', layer)],
                'w_down': full[(f'ffn_{kind}_w_down', layer)]}

    def xattn_params(layer):
        return {'norm': w['xmem_norm'][layer], 'mem_norm': w['xmem_mem_norm'][layer], 'wq': full[('xmem_wq', layer)],
                'wkv': full[('xmem_wkv', layer)], 'q_gain': w['xmem_q_gain'][layer], 'k_gain': w['xmem_k_gain'][layer],
                'wo': full[('xmem_wo', layer)]}

    even_p = {'norm': w['mix_norm'][0], 'ln_gain': w['sgu_ln_gain'][0], 'ln_bias': w['sgu_ln_bias'][0],
              'sgu_w': w['sgu_w'][0], 'sgu_b': w['sgu_b'][0]}

    def early_weights_landed(gathered):
        absorb(EARLY_UNITS, gathered)
        first_ffn_p['w_down'] = full[('ffn_pre_w_down', 0)]
        even_p['w_in'] = full[('sbg_w_in', 0)]

    def late_weights_landed(gathered):
        absorb(late_units, gathered)
        even_p['w_out'] = full[('sbg_w_out', 0)]

    def mla_params():
        return {'norm': w['mix_norm'][1], 'w_in': full[('mla_w_in', 0)], 'q_lora_gain': w['mla_q_lora_gain'][0],
                'kv_lora_gain': w['mla_kv_lora_gain'][0], 'w_uq': full[('mla_w_uq', 0)],
                'w_ukv': full[('mla_w_ukv', 0)], 'q_gain': w['mla_q_gain'][0], 'k_gain': w['mla_k_gain'][0],
                'w_out': full[('mla_w_out', 0)]}

    saved = []
    for layer in range(DEPTH):
        if layer == 0:
            (x, h), s_pre = ffn_fwd(x, h, first_ffn_p, "ffn_pre0", w['mix_norm'][0],
                                    job=CommJob('gather', [shards[u] for u in EARLY_UNITS]),
                                    after_job=early_weights_landed)
        else:
            (x, h), s_pre = ffn_fwd(x, h, ffn_params('pre', layer), f"ffn_pre{layer}", w['mix_norm'][layer])
        if layer % 2 == 0:
            (x, h), s_mix = even_mixer_fwd(x, h, even_p, w['xmem_norm'][layer],
                                           job=CommJob('gather', [shards[u] for u in late_units]),
                                           after_job=late_weights_landed)
        else:
            (x, h), s_mix = mla_fwd(x, h, cos, sin, mla_params(), w['xmem_norm'][layer])
        (x, h), s_x = xattn_fwd(x, h, mem, xattn_params(layer), f"xmem{layer}", w['ffn_post_norm'][layer])
        following = w['ffn_pre_norm'][layer + 1] if layer + 1 < DEPTH else None
        (x, h), s_post = ffn_fwd(x, h, ffn_params('post', layer), f"ffn_post{layer}", following)
        saved.append((s_pre, s_mix, s_x, s_post))

    dx, loss = loss_head(x, target, tm=ROW_TM, name="loss_head")

    ready, riding, landed = {}, [], {}

    def offer(name, layer, g):
        ready[(name, layer)] = full_to_owner_major(g, BIG[name] - 1)

    def ride(name):
        def job_of(own):
            offer(name, 0, own['w_out'])
            riding[:] = list(ready)
            return CommJob('exchange', [ready.pop(u) for u in riding])
        return job_of

    def last_rides(run, **own):
        for kind, g in own.items():
            offer('ffn_pre_' + kind, 0, g)
        units = list(ready)
        if not units:
            return run(None)[0]
        res, arrived = run(CommJob('exchange', [ready.pop(u) for u in units]))
        landed.update(zip(units, arrived))
        return res

    per_layer = []
    for layer in reversed(range(DEPTH)):
        s_pre, s_mix, s_x, s_post = saved[layer]
        dx, g_post = ffn_bwd(dx, ffn_params('post', layer), s_post, f"ffn_post{layer}")
        offer('ffn_post_w_gu', layer, g_post['w_gu'])
        offer('ffn_post_w_down', layer, g_post['w_down'])
        dx, g_x = xattn_bwd(dx, xattn_params(layer), s_x, f"xmem{layer}")
        for n in ('wq', 'wkv', 'wo'):
            offer('xmem_' + n, layer, g_x[n])
        if layer % 2 == 0:
            dx, g_mix, arrived = even_mixer_bwd(dx, even_p, s_mix, job_of=ride('sbg_w_out'))
            landed.update(zip(riding, arrived))
            offer('sbg_w_in', 0, g_mix['w_in'])
        else:
            dx, g_mix, arrived = mla_bwd(dx, cos, sin, mla_params(), s_mix, job_of=ride('mla_w_out'))
            landed.update(zip(riding, arrived))
            for n in ('w_in', 'w_uq', 'w_ukv'):
                offer('mla_' + n, 0, g_mix[n])
        if layer == 0:
            dx, g_pre = ffn_bwd(dx, ffn_params('pre', layer), s_pre, f"ffn_pre{layer}", with_job=last_rides)
        else:
            dx, g_pre = ffn_bwd(dx, ffn_params('pre', layer), s_pre, f"ffn_pre{layer}")
            offer('ffn_pre_w_gu', layer, g_pre['w_gu'])
            offer('ffn_pre_w_down', layer, g_pre['w_down'])
        per_layer.append((layer, g_pre, g_mix, g_x, g_post))
    per_layer.sort(key=lambda t: t[0])
    assert not ready

    def stack(pick):
        return jnp.stack([pick(t) for t in per_layer])

    g_even, g_mla = per_layer[0][2], per_layer[1][2]
    small_grads = {
        'ffn_pre_norm': stack(lambda t: t[1]['norm']), 'mix_norm': stack(lambda t: t[2]['norm']),
        'sgu_ln_gain': g_even['ln_gain'][None], 'sgu_ln_bias': g_even['ln_bias'][None],
        'sgu_w': g_even['sgu_w'][None], 'sgu_b': g_even['sgu_b'][None],
        'mla_q_lora_gain': g_mla['q_lora_gain'][None], 'mla_kv_lora_gain': g_mla['kv_lora_gain'][None],
        'mla_q_gain': g_mla['q_gain'][None], 'mla_k_gain': g_mla['k_gain'][None],
        'xmem_norm': stack(lambda t: t[3]['norm']), 'xmem_mem_norm': stack(lambda t: t[3]['mem_norm']),
        'xmem_q_gain': stack(lambda t: t[3]['q_gain']), 'xmem_k_gain': stack(lambda t: t[3]['k_gain']),
        'ffn_post_norm': stack(lambda t: t[4]['norm']),
    }
    return loss, dx, small_grads, landed


def _device_slot():
    x, y, c = _me()
    return 4 * x + 2 * y + c


def kernel(x, mem, positions, ffn_pre_norm, ffn_pre_w_gu, ffn_pre_w_down, mix_norm, sbg_w_in, sgu_ln_gain, sgu_ln_bias, sgu_w, sgu_b, sbg_w_out, mla_w_in, mla_q_lora_gain, mla_kv_lora_gain, mla_w_uq, mla_w_ukv, mla_q_gain, mla_k_gain, mla_w_out, xmem_norm, xmem_mem_norm, xmem_wq, xmem_wkv, xmem_q_gain, xmem_k_gain, xmem_wo, ffn_post_norm, ffn_post_w_gu, ffn_post_w_down, loss_target, m_ffn_pre_norm, m_ffn_pre_w_gu, m_ffn_pre_w_down, m_mix_norm, m_sbg_w_in, m_sgu_ln_gain, m_sgu_ln_bias, m_sgu_w, m_sgu_b, m_sbg_w_out, m_mla_w_in, m_mla_q_lora_gain, m_mla_kv_lora_gain, m_mla_w_uq, m_mla_w_ukv, m_mla_q_gain, m_mla_k_gain, m_mla_w_out, m_xmem_norm, m_xmem_mem_norm, m_xmem_wq, m_xmem_wkv, m_xmem_q_gain, m_xmem_k_gain, m_xmem_wo, m_ffn_post_norm, m_ffn_post_w_gu, m_ffn_post_w_down, v_ffn_pre_norm, v_ffn_pre_w_gu, v_ffn_pre_w_down, v_mix_norm, v_sbg_w_in, v_sgu_ln_gain, v_sgu_ln_bias, v_sgu_w, v_sgu_b, v_sbg_w_out, v_mla_w_in, v_mla_q_lora_gain, v_mla_kv_lora_gain, v_mla_w_uq, v_mla_w_ukv, v_mla_q_gain, v_mla_k_gain, v_mla_w_out, v_xmem_norm, v_xmem_mem_norm, v_xmem_wq, v_xmem_wkv, v_xmem_q_gain, v_xmem_k_gain, v_xmem_wo, v_ffn_post_norm, v_ffn_post_w_gu, v_ffn_post_w_down):
    args = locals()
    w_in = {n: args[n] for n in WEIGHTS}
    m_in = {n: args["m_" + n] for n in WEIGHTS}
    v_in = {n: args["v_" + n] for n in WEIGHTS}
    slot = _device_slot()

    tiny = jnp.zeros((8, LANES), F32)
    for i, src in enumerate((w_in, m_in, v_in)):
        tiny = tiny.at[i, :64].set(src['mla_q_lora_gain'][0]).at[i + 3, :32].set(src['mla_kv_lora_gain'][0])
    tiny_all = comm_call('gather', [tiny], name="gather_lora_gains")[0]
    full_small = []
    for i, src in enumerate((w_in, m_in, v_in)):
        d = {n: src[n] for n in SMALL}
        d['mla_q_lora_gain'] = tiny_all[:, i, :64].reshape(1, MLA_Q_LORA)
        d['mla_kv_lora_gain'] = tiny_all[:, i + 3, :32].reshape(1, MLA_KV_LORA)
        full_small.append(d)
    w_small, m_small, v_small = full_small
    small_shapes = {n: w_small[n].shape for n in SMALL}

    shards = {(n, layer): w_in[n][layer].astype(BF16) for n in BIG for layer in range(w_in[n].shape[0])}
    loss, dx, grads, landed = local_step(x[0], mem[0], positions[0], loss_target[0], w_small, shards)
    loss = lax.psum(loss, ("x", "y", "c"))
    big_out = {n: adamw([landed[(n, layer)] for layer in range(w_in[n].shape[0])], w_in[n], m_in[n], v_in[n],
                        name=f"adamw_{n}") for n in BIG}

    small_parts = comm_call('gather', [pack_small(grads, small_shapes)], name="gather_small_grads")
    small_out = adamw(small_parts, pack_small(w_small, small_shapes)[None], pack_small(m_small, small_shapes)[None],
                      pack_small(v_small, small_shapes)[None], name="adamw_small")
    small_out = [unpack_small(t[0], small_shapes) for t in small_out]
    for d in small_out:
        for n, width in zip(GAIN_SHARDED, (64, 32)):
            d[n] = lax.dynamic_slice(d[n], (0, slot * width), (1, width))

    outs = [loss, dx[None]]
    for kind, small_d in enumerate(small_out):
        outs += [big_out[n][kind] if n in BIG else small_d[n] for n in WEIGHTS]
    return tuple(outs)
```

```python
import jax
import jax.numpy as jnp
from jax import lax
from jax.experimental import pallas as pl
from jax.experimental.pallas import tpu as pltpu

F32 = jnp.float32
BF16 = jnp.bfloat16
MESH = pl.DeviceIdType.MESH
N_DEV = 8

VMEM_LIMIT_BYTES = 56 * 1024 * 1024
LANES = 128

D_MODEL = 1024
DEPTH = 2
EPS = 1e-6
SB_HEADS, SB_HEAD_DIM = 8, 64
SB_WIDTH = SB_HEADS * SB_HEAD_DIM
SG_GROUPS, SG_GROUP_DIM, SG_CHUNK = 8, 64, 128
SG_WIDTH = SG_GROUPS * SG_GROUP_DIM
MLA_HEADS, MLA_NOPE, MLA_ROPE, MLA_V = 16, 64, 32, 64
MLA_QK = MLA_NOPE + MLA_ROPE
MLA_Q_LORA, MLA_KV_LORA = 512, 256
ROPE_THETA = 10000.0
MEM_HEADS = 4
MEM_HEAD_DIM = D_MODEL // MEM_HEADS

ADAM_LR, ADAM_B1, ADAM_B2, ADAM_EPS, ADAM_WD, ADAM_STEP = 0.001, 0.9, 0.999, 1e-08, 0.01, 10

WEIGHTS = ['ffn_pre_norm', 'ffn_pre_w_gu', 'ffn_pre_w_down', 'mix_norm', 'sbg_w_in', 'sgu_ln_gain', 'sgu_ln_bias',
           'sgu_w', 'sgu_b', 'sbg_w_out', 'mla_w_in', 'mla_q_lora_gain', 'mla_kv_lora_gain', 'mla_w_uq', 'mla_w_ukv',
           'mla_q_gain', 'mla_k_gain', 'mla_w_out', 'xmem_norm', 'xmem_mem_norm', 'xmem_wq', 'xmem_wkv',
           'xmem_q_gain', 'xmem_k_gain', 'xmem_wo', 'ffn_post_norm', 'ffn_post_w_gu', 'ffn_post_w_down']
BIG = {'ffn_pre_w_gu': 2, 'ffn_pre_w_down': 1, 'sbg_w_in': 2, 'sbg_w_out': 1, 'mla_w_in': 1, 'mla_w_uq': 2,
       'mla_w_ukv': 2, 'mla_w_out': 1, 'xmem_wq': 1, 'xmem_wkv': 2, 'xmem_wo': 1, 'ffn_post_w_gu': 2,
       'ffn_post_w_down': 1}
GAIN_SHARDED = ('mla_q_lora_gain', 'mla_kv_lora_gain')
SMALL = [n for n in WEIGHTS if n not in BIG]
GRAD_WIRE = BF16
FFN_SAVE = BF16
SMALL_ROW_MULTIPLE = 16


def _cparams(sem=None):
    return pltpu.CompilerParams(dimension_semantics=sem, vmem_limit_bytes=VMEM_LIMIT_BYTES)


MM_TILE_CAP = 1408


def _pick(dim, cap=MM_TILE_CAP):
    if dim % LANES:
        return dim
    return max(t for t in range(LANES, min(dim, cap) + 1, LANES) if dim % t == 0)


def _rms(x, g):
    return x * lax.rsqrt(jnp.mean(x * x, axis=-1, keepdims=True) + EPS) * g


def pmm(a, b, *, a2=None, ta=False, tb=False, out_dtype=F32, res=None, alpha=1.0, norm_out=None, norm_bwd=None,
        out_heads=None, job=None, name):
    kdim, m = (a.shape if ta else a.shape[::-1])
    n = b.shape[0] if tb else b.shape[1]
    tm, tn, tk = _pick(m), _pick(n), _pick(kdim)
    if out_heads is not None:
        tn = out_heads
    whole_rows = norm_out is not None or norm_bwd is not None
    if whole_rows:
        assert tn == n
    if norm_bwd is not None:
        tm = min(tm, 512)
    nk1 = kdim // tk
    nk = nk1 if a2 is None else 2 * nk1
    assert a2 is None or (a2.shape == a.shape and not ta)
    dims = (((0 if ta else 1,), (1 if tb else 0,)), ((), ()))
    n_lead = 2 if a2 is None else 3
    n_extra = (res is not None) + (norm_out is not None) + (0 if norm_bwd is None else 2 + (norm_bwd[2] is not None))

    def body(*refs):
        a_ref, b_ref = refs[:2]
        extra = list(refs[n_lead:n_lead + n_extra])
        outs, acc_ref = refs[n_lead + n_extra:-1], refs[-1]
        i, k = pl.program_id(0), pl.program_id(2)

        @pl.when(k == 0)
        def _():
            acc_ref[...] = jnp.zeros_like(acc_ref)

        def accumulate(lhs_ref):
            acc_ref[...] += lax.dot_general(lhs_ref[...].astype(BF16), b_ref[...].astype(BF16), dims,
                                            preferred_element_type=F32)

        if a2 is None:
            accumulate(a_ref)
        else:
            pl.when(k < nk1)(lambda: accumulate(a_ref))
            pl.when(k >= nk1)(lambda: accumulate(refs[2]))

        @pl.when(k == nk - 1)
        def _():
            r = acc_ref[...]
            if alpha != 1.0:
                r = r * alpha
            if res is not None:
                r = extra.pop(0)[...] + r
            if norm_bwd is None:
                outs[0][...] = r.astype(out_dtype)
            if norm_out is not None:
                outs[1][...] = _rms(r, extra.pop(0)[...]).astype(BF16)
            if norm_bwd is not None:
                x_ref, g_ref = extra.pop(0), extra.pop(0)
                _, pull = jax.vjp(_rms, x_ref[...], g_ref[...])
                dx, dg = pull(r)
                if norm_bwd[2] is not None:
                    dx = dx + extra.pop(0)[...]
                outs[0][...] = dx

                @pl.when(i == 0)
                def _():
                    outs[1][...] = dg

                @pl.when(i != 0)
                def _():
                    outs[1][...] += dg

    gi, gj = m // tm, n // tn
    a_bytes, b_bytes = a.size * a.dtype.itemsize, (n * kdim) * b.dtype.itemsize
    j_outer = not whole_rows and nk == 1 and gj * a_bytes + b_bytes < a_bytes + gi * b_bytes
    grid = (gj, gi, nk) if j_outer else (gi, gj, nk)

    def spec(block, index):
        return pl.BlockSpec(block, (lambda j, i, k: index(i, j, k)) if j_outer else index)

    a_spec = spec((tk, tm), lambda i, j, k: (k, i)) if ta else spec((tm, tk), lambda i, j, k: (i, k))
    b_spec = spec((tn, tk), lambda i, j, k: (j, k)) if tb else spec((tk, tn), lambda i, j, k: (k, j))
    o_spec = spec((tm, tn), lambda i, j, k: (i, j))
    g_spec = spec((1, tn), lambda i, j, k: (0, 0))
    ins, in_specs = [a, b], [a_spec, b_spec]
    if a2 is not None:
        in_specs[0] = spec((tm, tk), lambda i, j, k: (i, jnp.minimum(k, nk1 - 1)))
        ins.append(a2)
        in_specs.append(spec((tm, tk), lambda i, j, k: (i, jnp.maximum(k - nk1, 0))))
    if res is not None:
        ins.append(res)
        in_specs.append(o_spec)
    out_shape, out_specs = [jax.ShapeDtypeStruct((m, n), out_dtype)], [o_spec]
    if out_heads is not None:
        out_shape = [jax.ShapeDtypeStruct((n // tn, m, tn), out_dtype)]
        out_specs = [spec((None, tm, tn), lambda i, j, k: (j, i, 0))]
    if norm_out is not None:
        ins.append(norm_out)
        in_specs.append(g_spec)
        out_shape.append(jax.ShapeDtypeStruct((m, n), BF16))
        out_specs.append(o_spec)
    if norm_bwd is not None:
        ins += [t for t in norm_bwd if t is not None]
        in_specs += [o_spec, g_spec] + ([o_spec] if norm_bwd[2] is not None else [])
        out_shape = [jax.ShapeDtypeStruct((m, n), F32), jax.ShapeDtypeStruct((1, n), F32)]
        out_specs = [o_spec, g_spec]
    result, landed = ride_call(
        job, body, name=name, grid=grid, in_specs=in_specs, out_specs=out_specs, out_shape=out_shape, ins=ins,
        scratch=[pltpu.VMEM((tm, tn), F32)],
        sem=("arbitrary" if norm_bwd is not None else "parallel", "parallel", "arbitrary"))
    result = result if whole_rows else result[0]
    return result if job is None else (result, landed)


def ffn_gate_up(h, w_gu, *, name, job=None):
    m, kdim = h.shape
    n = w_gu.shape[1] // 2
    tm, tn = min(_pick(m), 512), _pick(n)
    up_off = n // tn

    def body(a_ref, bg_ref, bu_ref, gate_ref, up_ref, act_ref):
        av = a_ref[...].astype(BF16)
        gate = _dg(av, bg_ref[...].astype(BF16), 1, 0)
        up = _dg(av, bu_ref[...].astype(BF16), 1, 0)
        gate_ref[...] = gate.astype(gate_ref.dtype)
        up_ref[...] = up.astype(up_ref.dtype)
        act_ref[...] = (jax.nn.silu(gate) * up).astype(BF16)

    o_spec = pl.BlockSpec((tm, tn), lambda j, i: (i, j))
    return ride_call(
        job, body, name=name, grid=(n // tn, m // tm),
        in_specs=[pl.BlockSpec((tm, kdim), lambda j, i: (i, 0)), pl.BlockSpec((kdim, tn), lambda j, i: (0, j)),
                  pl.BlockSpec((kdim, tn), lambda j, i: (0, j + up_off))],
        out_specs=[o_spec] * 3,
        out_shape=[jax.ShapeDtypeStruct((m, n), FFN_SAVE), jax.ShapeDtypeStruct((m, n), FFN_SAVE),
                   jax.ShapeDtypeStruct((m, n), BF16)],
        ins=[h, w_gu, w_gu], sem=("parallel", "parallel"))


def ffn_gate_up_bwd(dy, w_down, gate, up, *, alpha, name, job=None):
    m, kdim = dy.shape
    n = w_down.shape[0]
    tm, tn = min(_pick(m), 512), _pick(n)

    def body(a_ref, b_ref, gate_ref, up_ref, dgate_ref, dup_ref):
        d_act = _dg(a_ref[...].astype(BF16), b_ref[...].astype(BF16), 1, 1) * alpha
        _, pull = jax.vjp(lambda g, u: jax.nn.silu(g) * u, gate_ref[...].astype(F32), up_ref[...].astype(F32))
        d_gate, d_up = pull(d_act)
        dgate_ref[...] = d_gate.astype(BF16)
        dup_ref[...] = d_up.astype(BF16)

    o_spec = pl.BlockSpec((tm, tn), lambda j, i: (i, j))
    return ride_call(
        job, body, name=name, grid=(n // tn, m // tm),
        in_specs=[pl.BlockSpec((tm, kdim), lambda j, i: (i, 0)), pl.BlockSpec((tn, kdim), lambda j, i: (j, 0)),
                  o_spec, o_spec],
        out_specs=[o_spec] * 2, out_shape=[jax.ShapeDtypeStruct((m, n), BF16)] * 2,
        ins=[dy, w_down, gate, up], sem=("parallel", "parallel"))


def _dg(a, b, ca, cb):
    return lax.dot_general(a, b, (((ca,), (cb,)), ((), ())), preferred_element_type=F32)


@jax.custom_vjp
def bdot(a, b):
    return _dg(a.astype(BF16), b.astype(BF16), 1, 0)


def _bdot_fwd(a, b):
    ab, bb = a.astype(BF16), b.astype(BF16)
    return _dg(ab, bb, 1, 0), (ab, bb)


def _bdot_bwd(saved, g):
    ab, bb = saved
    gb = g.astype(BF16)
    return _dg(gb, bb, 1, 1), _dg(ab, gb, 0, 0)


bdot.defvjp(_bdot_fwd, _bdot_bwd)


@jax.custom_vjp
def bdot_nt(a, b):
    return _dg(a.astype(BF16), b.astype(BF16), 1, 1)


def _bdot_nt_fwd(a, b):
    ab, bb = a.astype(BF16), b.astype(BF16)
    return _dg(ab, bb, 1, 1), (ab, bb)


def _bdot_nt_bwd(saved, g):
    ab, bb = saved
    gb = g.astype(BF16)
    return _dg(gb, bb, 1, 0), _dg(gb, ab, 0, 0)


bdot_nt.defvjp(_bdot_nt_fwd, _bdot_nt_bwd)


class ColGroups:
    def __init__(self, arr, width):
        self.arr, self.width = arr, width
        self.shape, self.dtype, self.ndim = arr.shape, arr.dtype, 3


def _plain(a):
    return a.arr if isinstance(a, ColGroups) else a


def _row_spec(arr, tm):
    if isinstance(arr, ColGroups):
        return pl.BlockSpec((tm, arr.width), lambda r, g: (r, g))
    if arr.ndim == 3:
        return pl.BlockSpec((None, tm, arr.shape[2]), lambda r, g: (g, r, 0))
    return pl.BlockSpec((tm, arr.shape[1]), lambda r, g: (r, 0))


def _gparam_spec(arr):
    return pl.BlockSpec((None,) + arr.shape[1:], lambda r, g: (g, 0, 0))


def _whole_spec(arr):
    nd = arr.ndim
    return pl.BlockSpec(arr.shape, lambda r, g: (0,) * nd)


def _groups(rows, gparams):
    gs = {a.shape[1] // a.width if isinstance(a, ColGroups) else a.shape[0] for a in rows if a.ndim == 3}
    gs |= {a.shape[0] for a in gparams}
    assert len(gs) <= 1
    return gs.pop() if gs else 1


def prow(fn, rows, gparams=(), params=(), *, outs, tm, name, job=None):
    rows, gparams, params = list(rows), list(gparams), list(params)
    n_groups = _groups(rows, gparams)
    n_rows = rows[0].shape[-2]
    n_in = len(rows) + len(gparams) + len(params)

    def body(*refs):
        vals = [r[...] for r in refs[:n_in]]
        res = fn(*vals)
        for o_ref, r in zip(refs[n_in:], res, strict=True):
            o_ref[...] = r.astype(o_ref.dtype)

    out_shape, out_specs = [], []
    for width, dtype, grouped in outs:
        if grouped == 'cols':
            out_shape.append(jax.ShapeDtypeStruct((n_rows, n_groups * width), dtype))
            out_specs.append(_row_spec(ColGroups(out_shape[-1], width), tm))
            continue
        shp = (n_groups, n_rows, width) if grouped else (n_rows, width)
        out_shape.append(jax.ShapeDtypeStruct(shp, dtype))
        out_specs.append(_row_spec(out_shape[-1], tm))
    result, landed = ride_call(
        job, body, name=name, grid=(n_rows // tm, n_groups),
        in_specs=[_row_spec(a, tm) for a in rows] + [_gparam_spec(a) for a in gparams] + [_whole_spec(a) for a in params],
        out_specs=out_specs, out_shape=out_shape, ins=[*[_plain(a) for a in rows], *gparams, *params],
        sem=("parallel", "arbitrary"))
    return result if job is None else (result, landed)


def prow_vjp(fn, rows, gparams=(), params=(), *, cts, row_grad, adds=None, row_dtypes=None, gparam_grad=None,
             param_grad=None, tm, name):
    rows, gparams, params, cts = list(rows), list(gparams), list(params), list(cts)
    gparam_grad = list(gparam_grad) if gparam_grad is not None else [True] * len(gparams)
    param_grad = list(param_grad) if param_grad is not None else [True] * len(params)
    n_groups = _groups(rows + cts, gparams)
    n_rows = rows[0].shape[-2]
    want_rows = [i for i, w in enumerate(row_grad) if w]
    adds = list(adds) if adds is not None else [None] * len(want_rows)
    row_dtypes = list(row_dtypes) if row_dtypes is not None else [F32] * len(want_rows)
    add_arrays = [a for a in adds if a is not None]
    n_r, n_g, n_p, n_c, n_a = len(rows), len(gparams), len(params), len(cts), len(add_arrays)
    mask = list(row_grad) + gparam_grad + param_grad

    def body(*refs):
        r_id, g_id = pl.program_id(0), pl.program_id(1)
        n_in = n_r + n_g + n_p
        vals = [r[...] for r in refs[:n_in]]
        ct_vals = tuple(r[...].astype(F32) for r in refs[n_in:n_in + n_c])
        add_refs = list(refs[n_in + n_c:n_in + n_c + n_a])
        out_refs = list(refs[n_in + n_c + n_a:])
        diff_idx = [i for i, w in enumerate(mask) if w]

        def wrapped(*diff):
            full = list(vals)
            for i, d in zip(diff_idx, diff):
                full[i] = d
            return tuple(fn(*full))

        _, pull = jax.vjp(wrapped, *[vals[i].astype(F32) for i in diff_idx])
        grads = dict(zip(diff_idx, pull(ct_vals)))
        k = 0
        for j, i in enumerate(want_rows):
            o_ref = out_refs[k]
            k += 1
            gval = grads[i]
            if adds[j] is not None:
                gval = gval + add_refs.pop(0)[...].astype(F32)
            if rows[i].ndim == 2 and n_groups > 1:
                @pl.when(g_id == 0)
                def _(o_ref=o_ref, gval=gval):
                    o_ref[...] = gval.astype(o_ref.dtype)

                @pl.when(g_id != 0)
                def _(o_ref=o_ref, gval=gval):
                    o_ref[...] += gval.astype(o_ref.dtype)
            else:
                o_ref[...] = gval.astype(o_ref.dtype)
        for i in range(n_g):
            if not gparam_grad[i]:
                continue
            o_ref = out_refs[k]
            k += 1
            gval = grads[n_r + i]

            @pl.when(r_id == 0)
            def _(o_ref=o_ref, gval=gval):
                o_ref[g_id] = gval

            @pl.when(r_id != 0)
            def _(o_ref=o_ref, gval=gval):
                o_ref[g_id] += gval
        for i in range(n_p):
            if not param_grad[i]:
                continue
            o_ref = out_refs[k]
            k += 1
            gval = grads[n_r + n_g + i]
            first = jnp.logical_and(r_id == 0, g_id == 0)

            @pl.when(first)
            def _(o_ref=o_ref, gval=gval):
                o_ref[...] = gval

            @pl.when(jnp.logical_not(first))
            def _(o_ref=o_ref, gval=gval):
                o_ref[...] += gval

    out_shape, out_specs = [], []
    for j, i in enumerate(want_rows):
        out_shape.append(jax.ShapeDtypeStruct(rows[i].shape, row_dtypes[j]))
        out_specs.append(_row_spec(rows[i], tm))
    for i in range(n_g):
        if gparam_grad[i]:
            out_shape.append(jax.ShapeDtypeStruct(gparams[i].shape, F32))
            out_specs.append(_whole_spec(gparams[i]))
    for i in range(n_p):
        if param_grad[i]:
            out_shape.append(jax.ShapeDtypeStruct(params[i].shape, F32))
            out_specs.append(_whole_spec(params[i]))
    return pl.pallas_call(
        body, name=name, grid=(n_rows // tm, n_groups),
        in_specs=([_row_spec(a, tm) for a in rows] + [_gparam_spec(a) for a in gparams]
                  + [_whole_spec(a) for a in params] + [_row_spec(a, tm) for a in cts]
                  + [_row_spec(a, tm) for a in add_arrays]),
        out_specs=out_specs, out_shape=out_shape,
        compiler_params=_cparams(("arbitrary", "arbitrary")),
    )(*[_plain(a) for a in rows], *gparams, *params, *[_plain(a) for a in cts], *add_arrays)


def f_rms(x, g):
    return (_rms(x.astype(F32), g),)


def f_gate_prep(z, ln_g, ln_b):
    act = jax.nn.gelu(z)
    u, gg = act[:, :SG_WIDTH], act[:, SG_WIDTH:]
    mu = jnp.mean(gg, axis=-1, keepdims=True)
    var = jnp.mean(jnp.square(gg - mu), axis=-1, keepdims=True)
    return u, (gg - mu) * lax.rsqrt(var + EPS) * ln_g + ln_b


def f_spatial_gate(gn, u, w, b):
    t = lax.broadcasted_iota(jnp.int32, w.shape, 0)
    s = lax.broadcasted_iota(jnp.int32, w.shape, 1)
    w_causal = jnp.where(s <= t, w, 0.0)
    mixed = [bdot(w_causal, gn[i:i + SG_CHUNK]) + b for i in range(0, gn.shape[0], SG_CHUNK)]
    return (u * (mixed[0] if len(mixed) == 1 else jnp.concatenate(mixed, axis=0)),)


def _two_pieces(x):
    hi = x.astype(BF16)
    return hi, (x - hi.astype(F32)).astype(BF16)


@jax.custom_vjp
def place(x, m):
    hi, lo = _two_pieces(x)
    return _dg(hi, m, 1, 0) + _dg(lo, m, 1, 0)


def _place_fwd(x, m):
    return place(x, m), m


def _place_bwd(m, g):
    hi, lo = _two_pieces(g)
    return _dg(hi, m, 1, 1) + _dg(lo, m, 1, 1), jnp.zeros_like(m)


place.defvjp(_place_fwd, _place_bwd)


def _lane_map(rows, cols, entry):
    src = lax.broadcasted_iota(jnp.int32, (rows, cols), 0)
    dst = lax.broadcasted_iota(jnp.int32, (rows, cols), 1)
    return entry(src, dst).astype(BF16)


def _rope_tail(t, cos_w, sin_w):
    half = MLA_ROPE // 2
    lo_half = lambda d: jnp.logical_and(d >= MLA_NOPE, d < MLA_NOPE + half)
    swap = _lane_map(MLA_QK, MLA_QK, lambda s, d: jnp.where(
        jnp.logical_and(d >= MLA_NOPE + half, s == d - half), 1.0,
        jnp.where(jnp.logical_and(lo_half(d), s == d + half), -1.0, 0.0)))
    return t * cos_w + place(t, swap) * sin_w


def f_mla_q(q, cos_w, sin_w, g):
    return (_rope_tail(f_rms(q, g)[0], cos_w, sin_w),)


def f_mla_k(kv, k_r, cos_w, sin_w, g):
    width = MLA_NOPE + MLA_V
    nope = _lane_map(width, MLA_QK, lambda s, d: jnp.where(jnp.logical_and(s == d, d < MLA_NOPE), 1.0, 0.0))
    tail = _lane_map(MLA_ROPE, MLA_QK, lambda s, d: jnp.where(s + MLA_NOPE == d, 1.0, 0.0))
    value = _lane_map(width, MLA_V, lambda s, d: jnp.where(s == d + MLA_NOPE, 1.0, 0.0))
    key = _rope_tail(f_rms(place(kv, nope) + place(k_r, tail), g)[0], cos_w, sin_w)
    return key, place(kv, value)


def f_xattn(q, k, v, q_g, k_g):
    qn, kn = f_rms(q, q_g)[0], f_rms(k, k_g)[0]
    sc = bdot_nt(qn, kn) * (MEM_HEAD_DIM ** -0.5)
    return (bdot(jax.nn.softmax(sc, axis=-1), v),)


def _split_dot(x, tri, pieces=2):
    hi = x.astype(BF16)
    if pieces == 1:
        return _dg(hi, tri, 1, 0)
    lo = (x - hi.astype(F32)).astype(BF16)
    return _dg(hi, tri, 1, 0) + _dg(lo, tri, 1, 0)


def _tri(tk, cmp):
    j = lax.broadcasted_iota(jnp.int32, (tk, tk), 0)
    s = lax.broadcasted_iota(jnp.int32, (tk, tk), 1)
    return cmp(j, s).astype(BF16)


SCAN_CHUNK = 256


def _row_scan(x, tri, reverse, pieces=2):
    n = x.shape[1] // SCAN_CHUNK
    chunks = [x[:, i * SCAN_CHUNK:(i + 1) * SCAN_CHUNK] for i in range(n)]
    out, seen = [None] * n, None
    for i in (reversed(range(n)) if reverse else range(n)):
        local = _split_dot(chunks[i], tri, pieces)
        out[i] = local if seen is None else local + seen
        total = jnp.sum(chunks[i], axis=1, keepdims=True)
        seen = total if seen is None else seen + total
    return (out[0] if n == 1 else jnp.concatenate(out, axis=1)), seen


def _att_specs(s_len, tq, dq, dv):
    q_spec = pl.BlockSpec((None, tq, dq), lambda h, i: (h, i, 0))
    k_spec = pl.BlockSpec((None, s_len, dq), lambda h, i: (h, 0, 0))
    v_spec = pl.BlockSpec((None, s_len, dv), lambda h, i: (h, 0, 0))
    o_spec = pl.BlockSpec((None, tq, dv), lambda h, i: (h, i, 0))
    r_spec = pl.BlockSpec((None, tq, 1), lambda h, i: (h, i, 0))
    return q_spec, k_spec, v_spec, o_spec, r_spec


def _key_blocks(qi, tq, tk):
    return (qi * tq) // tk, ((qi + 1) * tq + tk - 1) // tk


def _earlier(qi, j, tq, tk):
    row = qi * tq + lax.broadcasted_iota(jnp.int32, (tq, tk), 0)
    col = j * tk + lax.broadcasted_iota(jnp.int32, (tq, tk), 1)
    return col < row


LOG2_E = 1.4426950408889634


def _log2_sigmoid(z2):
    return jnp.minimum(z2, 0.0) - jnp.log2(1.0 + jnp.exp2(-jnp.abs(z2)))


def sb_fwd(q, k, v, *, tq, tk, name, job=None):
    n_heads, s_len, d = q.shape
    scale2 = SB_HEAD_DIM ** -0.5 * LOG2_E

    def body(q_ref, k_ref, v_ref, o_ref, tot_ref):
        qi = pl.program_id(1)
        qv = q_ref[...]
        upper = _tri(SCAN_CHUNK, lambda j, s: j > s)
        n_full, n_all = _key_blocks(qi, tq, tk)

        def make_step(masked, last):
            def step(jj, carry):
                acc, rest = carry
                j = last - 1 - jj
                sl = pl.ds(pl.multiple_of(j * tk, tk), tk)
                ks, vs = k_ref[sl, :], v_ref[sl, :]
                z2 = _dg(qv, ks, 1, 1) * scale2
                log_beta = _log2_sigmoid(z2)
                log_stay = log_beta - z2
                if masked:
                    valid = _earlier(qi, j, tq, tk)
                    log_stay = jnp.where(valid, log_stay, 0.0)
                after, total = _row_scan(log_stay, upper, True)
                w = jnp.exp2(log_beta + after + rest)
                if masked:
                    w = jnp.where(valid, w, 0.0)
                acc = acc + _dg(w.astype(BF16), vs, 1, 0)
                return acc, rest + total
            return step

        carry = (jnp.zeros((tq, d), F32), jnp.zeros((tq, 1), F32))
        carry = lax.fori_loop(0, n_all - n_full, make_step(True, n_all), carry)
        acc, rest = lax.fori_loop(0, n_full, make_step(False, n_full), carry)
        o_ref[...] = acc
        tot_ref[...] = rest

    q_spec, k_spec, v_spec, o_spec, r_spec = _att_specs(s_len, tq, d, d)
    return ride_call(
        job, body, name=name, grid=(n_heads, s_len // tq), in_specs=[q_spec, k_spec, v_spec],
        out_specs=[o_spec, r_spec],
        out_shape=[jax.ShapeDtypeStruct((n_heads, s_len, d), F32), jax.ShapeDtypeStruct((n_heads, s_len, 1), F32)],
        ins=[q, k, v], sem=("parallel", "arbitrary"))


def sb_bwd(q, k, v, tot, do, *, tq, tk, name, job=None):
    n_heads, s_len, d = q.shape
    scale = SB_HEAD_DIM ** -0.5
    scale2 = scale * LOG2_E

    def body(q_ref, k_ref, v_ref, tot_ref, do_ref, dq_ref, dk_ref, dv_ref):
        qi = pl.program_id(1)

        @pl.when(qi == 0)
        def _():
            dk_ref[...] = jnp.zeros_like(dk_ref)
            dv_ref[...] = jnp.zeros_like(dv_ref)

        qv = q_ref[...]
        dob = do_ref[...].astype(BF16)
        total = tot_ref[...]
        incl = _tri(SCAN_CHUNK, lambda j, s: j <= s)
        excl = _tri(SCAN_CHUNK, lambda j, s: j < s)
        n_full, n_all = _key_blocks(qi, tq, tk)

        def make_step(masked):
            def step(j, carry):
                dq, stay_before, dl_before = carry
                sl = pl.ds(pl.multiple_of(j * tk, tk), tk)
                ks, vs = k_ref[sl, :], v_ref[sl, :]
                z2 = _dg(qv, ks, 1, 1) * scale2
                log_beta = _log2_sigmoid(z2)
                log_stay = log_beta - z2
                if masked:
                    valid = _earlier(qi, j, tq, tk)
                    log_stay = jnp.where(valid, log_stay, 0.0)
                stay_upto, stay_sum = _row_scan(log_stay, incl, False)
                w = jnp.exp2(log_beta + (total - stay_before) - stay_upto)
                if masked:
                    w = jnp.where(valid, w, 0.0)
                dl = _dg(dob, vs, 1, 1) * w
                dl_upto, dl_sum = _row_scan(dl, excl, False, pieces=1)
                dl_prefix = dl_upto + dl_before
                beta = jnp.exp2(log_beta)
                dz = dl * (1.0 - beta) - beta * dl_prefix
                if masked:
                    dz = jnp.where(valid, dz, 0.0)
                dzb = dz.astype(BF16)
                dq = dq + _dg(dzb, ks, 1, 0)
                dk_ref[sl, :] += _dg(dzb, qv, 0, 0) * scale
                dv_ref[sl, :] += _dg(w.astype(BF16), dob, 0, 0)
                return dq, stay_before + stay_sum, dl_before + dl_sum
            return step

        zero = jnp.zeros((tq, 1), F32)
        carry = lax.fori_loop(0, n_full, make_step(False), (jnp.zeros((tq, d), F32), zero, zero))
        dq, _, _ = lax.fori_loop(n_full, n_all, make_step(True), carry)
        dq_ref[...] = dq * scale

    q_spec, k_spec, v_spec, o_spec, r_spec = _att_specs(s_len, tq, d, d)
    shp = jax.ShapeDtypeStruct((n_heads, s_len, d), F32)
    return ride_call(
        job, body, name=name, grid=(n_heads, s_len // tq), in_specs=[q_spec, k_spec, v_spec, r_spec, o_spec],
        out_specs=[q_spec, k_spec, v_spec], out_shape=[shp, shp, shp], ins=[q, k, v, tot, do],
        sem=("arbitrary", "arbitrary"))


NEG_BIG = -1e30


def _lower_left(rows, cols):
    r = lax.broadcasted_iota(jnp.int32, (rows, cols), 0)
    c = lax.broadcasted_iota(jnp.int32, (rows, cols), 1)
    return c <= r


def _prep_specs(tq, q_prep):
    cos, _, gain = q_prep
    rope_spec = pl.BlockSpec((tq, cos.shape[1]), lambda h, i: (i, 0))
    return [rope_spec, rope_spec, pl.BlockSpec(gain.shape, lambda h, i: (0, 0))]


def sm_fwd(q, k, v, *, tq, tk, name, q_prep=None):
    n_heads, s_len, dq = q.shape
    dv = v.shape[2]
    scale = dq ** -0.5
    assert tq == tk
    half = tk // 2
    n_prep = 0 if q_prep is None else 3

    def body(*refs):
        q_ref, prep_refs = refs[0], refs[1:1 + n_prep]
        k_ref, v_ref, o_ref, lse_ref = refs[1 + n_prep:]
        qi = pl.program_id(1)
        qv = q_ref[...]
        if q_prep is not None:
            qv = f_mla_q(qv, *[r[...] for r in prep_refs])[0].astype(BF16)

        def attend(carry, q_rows, keys, keep):
            acc, m, l = carry
            sc = _dg(q_rows, k_ref[keys, :], 1, 1) * scale
            if keep is not None:
                sc = jnp.where(keep, sc, NEG_BIG)
            m_new = jnp.maximum(m, jnp.max(sc, axis=1, keepdims=True))
            p = jnp.exp(sc - m_new)
            fade = jnp.exp(m - m_new)
            return (fade * acc + _dg(p.astype(BF16), v_ref[keys, :], 1, 0), m_new,
                    fade * l + jnp.sum(p, axis=1, keepdims=True))

        carry = (jnp.zeros((tq, dv), F32), jnp.full((tq, 1), NEG_BIG, F32), jnp.zeros((tq, 1), F32))
        carry = lax.fori_loop(
            0, qi, lambda j, c: attend(c, qv, pl.ds(pl.multiple_of(j * tk, tk), tk), None), carry)
        base = pl.multiple_of(qi * tk, tk)
        carry = attend(carry, qv, pl.ds(base, half), _lower_left(tq, half))
        low = attend(tuple(t[half:] for t in carry), qv[half:], pl.ds(pl.multiple_of(base + half, half), half),
                     _lower_left(half, half))
        acc, m, l = (jnp.concatenate([t[:half], u], axis=0) for t, u in zip(carry, low))
        o_ref[...] = acc / l
        lse_ref[...] = m + jnp.log(l)

    q_spec, k_spec, v_spec, o_spec, r_spec = _att_specs(s_len, tq, dq, dv)
    prep = [] if q_prep is None else list(q_prep)
    return pl.pallas_call(
        body, name=name, grid=(n_heads, s_len // tq),
        in_specs=[q_spec] + ([] if q_prep is None else _prep_specs(tq, q_prep)) + [k_spec, v_spec],
        out_specs=[o_spec, r_spec],
        out_shape=[jax.ShapeDtypeStruct((n_heads, s_len, dv), F32), jax.ShapeDtypeStruct((n_heads, s_len, 1), F32)],
        compiler_params=_cparams(("parallel", "arbitrary")),
    )(q, *prep, k, v)


def sm_bwd(q, k, v, o, lse, do, *, tq, tk, name, q_prep=None, job=None):
    n_heads, s_len, dq = q.shape
    dv = v.shape[2]
    scale = dq ** -0.5
    assert tq == tk
    half = tk // 2
    n_prep = 0 if q_prep is None else 3

    def body(*refs):
        q_ref, prep_refs = refs[0], refs[1:1 + n_prep]
        k_ref, v_ref, o_ref, lse_ref, do_ref, dq_ref, dk_ref, dv_ref = refs[1 + n_prep:9 + n_prep]
        head, qi = pl.program_id(0), pl.program_id(1)

        @pl.when(qi == 0)
        def _():
            dk_ref[...] = jnp.zeros_like(dk_ref)
            dv_ref[...] = jnp.zeros_like(dv_ref)

        q_raw = q_ref[...]
        prep_vals = [r[...] for r in prep_refs]
        qv = q_raw if q_prep is None else f_mla_q(q_raw, *prep_vals)[0].astype(BF16)
        do = do_ref[...]
        dob = do.astype(BF16)
        delta = jnp.sum(do * o_ref[...], axis=1, keepdims=True)
        lse_v = lse_ref[...]

        def attend(rows, keys, keep):
            ks, vs = k_ref[keys, :], v_ref[keys, :]
            p = jnp.exp(_dg(qv[rows], ks, 1, 1) * scale - lse_v[rows])
            if keep is not None:
                p = jnp.where(keep, p, 0.0)
            dv_ref[keys, :] += _dg(p.astype(BF16), dob[rows], 0, 0)
            ds = (p * (_dg(dob[rows], vs, 1, 1) - delta[rows]) * scale).astype(BF16)
            dk_ref[keys, :] += _dg(ds, qv[rows], 0, 0)
            return _dg(ds, ks, 1, 0)

        everything = slice(None)
        dq_acc = lax.fori_loop(
            0, qi, lambda j, acc: acc + attend(everything, pl.ds(pl.multiple_of(j * tk, tk), tk), None),
            jnp.zeros((tq, dq), F32))
        base = pl.multiple_of(qi * tk, tk)
        dq_acc = dq_acc + attend(everything, pl.ds(base, half), _lower_left(tq, half))
        low = attend(slice(half, None), pl.ds(pl.multiple_of(base + half, half), half), _lower_left(half, half))
        dq_acc = jnp.concatenate([dq_acc[:half], dq_acc[half:] + low], axis=0)
        if q_prep is None:
            dq_ref[...] = dq_acc
        else:
            cos, sin, gain = prep_vals
            _, pull = jax.vjp(lambda t, g: f_mla_q(t, cos, sin, g)[0], q_raw, gain)
            dq_raw, d_gain = pull(dq_acc)
            dq_ref[...] = dq_raw.astype(dq_ref.dtype)
            dgain_ref = refs[9 + n_prep]
            first = jnp.logical_and(head == 0, qi == 0)

            @pl.when(first)
            def _():
                dgain_ref[...] = d_gain

            @pl.when(jnp.logical_not(first))
            def _():
                dgain_ref[...] += d_gain

    q_spec, k_spec, v_spec, o_spec, r_spec = _att_specs(s_len, tq, dq, dv)
    out_specs = [q_spec, k_spec, v_spec]
    out_shape = [jax.ShapeDtypeStruct((n_heads, s_len, dq), F32 if q_prep is None else BF16),
                 jax.ShapeDtypeStruct((n_heads, s_len, dq), F32), jax.ShapeDtypeStruct((n_heads, s_len, dv), F32)]
    prep, prep_specs = [], []
    if q_prep is not None:
        prep, prep_specs = list(q_prep), _prep_specs(tq, q_prep)
        out_specs.append(prep_specs[2])
        out_shape.append(jax.ShapeDtypeStruct(q_prep[2].shape, F32))
    return ride_call(
        job, body, name=name, grid=(n_heads, s_len // tq),
        in_specs=[q_spec] + prep_specs + [k_spec, v_spec, o_spec, r_spec, o_spec], out_specs=out_specs,
        out_shape=out_shape, ins=[q, *prep, k, v, o, lse, do], sem=("arbitrary", "arbitrary"))


def loss_head(y, target, *, tm, name):
    n_rows, width = y.shape

    def body(y_ref, t_ref, dy_ref, loss_ref):
        diff = y_ref[...] - t_ref[...]
        dy_ref[...] = diff / width
        part = 0.5 * jnp.sum(jnp.mean(diff * diff, axis=-1, keepdims=True), axis=0, keepdims=True)

        @pl.when(pl.program_id(0) == 0)
        def _():
            loss_ref[...] = jnp.zeros_like(loss_ref)

        loss_ref[...] += jnp.broadcast_to(part, loss_ref.shape)

    spec = pl.BlockSpec((tm, width), lambda r: (r, 0))
    dy, loss = pl.pallas_call(
        body, name=name, grid=(n_rows // tm,), in_specs=[spec, spec],
        out_specs=[spec, pl.BlockSpec((8, LANES), lambda r: (0, 0))],
        out_shape=[jax.ShapeDtypeStruct(y.shape, F32), jax.ShapeDtypeStruct((8, LANES), F32)],
        compiler_params=_cparams(("arbitrary",)),
    )(y, target)
    return dy, loss[0, 0]


ADAM_TILE_ELEMS = 256 * 1024


def _adam_rows(n_rows, width):
    fits = [t for t in range(16, n_rows + 1, 16) if n_rows % t == 0 and t * width <= ADAM_TILE_ELEMS]
    return max(fits) if fits else n_rows


def adamw(parts, w, m, v, *, name):
    n_layers, n_rows, width = w.shape
    assert len(parts) == n_layers
    tm = _adam_rows(n_rows, width)
    n_tiles = n_rows // tm

    def body(*refs):
        p_refs = refs[:n_layers]
        w_ref, m_ref, v_ref, g_ref, d_ref, nm_ref, nv_ref = refs[n_layers:]
        layer = pl.program_id(0)
        for this, p_ref in enumerate(p_refs):
            @pl.when(layer == this)
            def _(p_ref=p_ref):
                g = p_ref[0].astype(F32)
                for i in range(1, N_DEV):
                    g = g + p_ref[i].astype(F32)
                m_new = ADAM_B1 * m_ref[...] + (1.0 - ADAM_B1) * g
                v_new = ADAM_B2 * v_ref[...] + (1.0 - ADAM_B2) * jnp.square(g)
                m_hat = m_new / (1.0 - ADAM_B1 ** ADAM_STEP)
                v_hat = v_new / (1.0 - ADAM_B2 ** ADAM_STEP)
                g_ref[...] = g
                d_ref[...] = -ADAM_LR * (m_hat / (jnp.sqrt(v_hat) + ADAM_EPS) + ADAM_WD * w_ref[...])
                nm_ref[...] = m_new
                nv_ref[...] = v_new

    def part_spec(this):
        def index(layer, r):
            return 0, jnp.where(layer == this, r, jnp.where(layer < this, 0, n_tiles - 1)), 0
        return pl.BlockSpec((N_DEV, tm, width), index)

    spec = pl.BlockSpec((None, tm, width), lambda layer, r: (layer, r, 0))
    shp = jax.ShapeDtypeStruct(w.shape, F32)
    return pl.pallas_call(
        body, name=name, grid=(n_layers, n_tiles),
        in_specs=[part_spec(this) for this in range(n_layers)] + [spec, spec, spec],
        out_specs=[spec] * 4, out_shape=[shp] * 4, compiler_params=_cparams(("arbitrary", "arbitrary")),
    )(*parts, w, m, v)


def _me():
    return lax.axis_index("x"), lax.axis_index("y"), lax.axis_index("c")


N_PEERS = N_DEV - 1


class CommJob:
    def __init__(self, kind, arrays):
        self.kind, self.arrays, self.n = kind, list(arrays), len(arrays)

    def out_shape(self):
        lead = (N_DEV,) if self.kind == 'gather' else ()
        return [jax.ShapeDtypeStruct(lead + a.shape, a.dtype) for a in self.arrays]

    def scratch(self):
        return [pltpu.SemaphoreType.DMA((N_PEERS * self.n,)), pltpu.SemaphoreType.DMA((N_PEERS * self.n,)),
                pltpu.SemaphoreType.DMA((self.n,))]

    def phases(self, in_refs, out_refs, send_sems, recv_sems, local_sems):
        n = self.n
        x, y, c = _me()

        def remote(i, k, src, dst, to):
            return pltpu.make_async_remote_copy(
                src_ref=src, dst_ref=dst, send_sem=send_sems.at[N_PEERS * i + k],
                recv_sem=recv_sems.at[N_PEERS * i + k], device_id=to, device_id_type=MESH)

        if self.kind == 'gather':
            me, sibling = (x, y, c), (x, y, 1 - c)
            chips = [(1 - x, y), (x, 1 - y), (1 - x, 1 - y)]

            def slot(i, px, py, pc):
                return out_refs[i].at[4 * px + 2 * py + pc]

            def copy(i, k, blk, to, src=None):
                return remote(i, k, slot(i, *blk) if src is None else src, slot(i, *blk), to)

            def mine():
                return [pltpu.make_async_copy(in_refs[i], slot(i, *me), local_sems.at[i]) for i in range(n)]

            def first():
                cps = []
                for i in range(n):
                    cps.append(copy(i, 0, me, sibling, src=in_refs[i]))
                    cps += [copy(i, 1 + j, me, (*chip, c), src=in_refs[i]) for j, chip in enumerate(chips)]
                return cps

            def passed():
                return [copy(i, 4 + j, (*chip, c), sibling) for j, chip in enumerate(chips) for i in range(n)]

            def start():
                for cp in mine() + first():
                    cp.start()

            def forward():
                for j, chip in enumerate(chips):
                    for i in range(n):
                        copy(i, 1 + j, (*chip, c), me).wait_recv()
                        copy(i, 4 + j, (*chip, c), sibling).start()

            def finish():
                for i in range(n):
                    copy(i, 0, sibling, me).wait_recv()
                    for j, chip in enumerate(chips):
                        copy(i, 4 + j, (*chip, 1 - c), me).wait_recv()
                for cp in first() + passed():
                    cp.wait_send()
                for cp in mine():
                    cp.wait()

            return start, forward, finish

        my_slot = 4 * x + 2 * y + c

        def mine():
            return [pltpu.make_async_copy(in_refs[i].at[my_slot], out_refs[i].at[my_slot], local_sems.at[i])
                    for i in range(n)]

        def copies():
            cps = []
            for k in range(1, N_DEV):
                px, py, pc = x ^ (k >> 2), y ^ ((k >> 1) & 1), c ^ (k & 1)
                cps += [remote(i, k - 1, in_refs[i].at[4 * px + 2 * py + pc], out_refs[i].at[my_slot], (px, py, pc))
                        for i in range(n)]
            return cps

        def start():
            for cp in mine() + copies():
                cp.start()

        def finish():
            for cp in copies():
                cp.wait_recv()
            for cp in copies():
                cp.wait_send()
            for cp in mine():
                cp.wait()

        return start, (lambda: None), finish


def comm_call(kind, arrays, *, name):
    job = CommJob(kind, arrays)
    n = job.n

    def body(*refs):
        start, forward, finish = job.phases(refs[:n], refs[n:2 * n], *refs[2 * n:])
        start()
        forward()
        finish()

    hbm = pl.BlockSpec(memory_space=pl.ANY)
    return pl.pallas_call(body, name=name, out_shape=job.out_shape(), in_specs=[hbm] * n, out_specs=[hbm] * n,
                          scratch_shapes=job.scratch())(*job.arrays)


def ride_call(job, compute, *, name, grid, in_specs, out_specs, out_shape, ins, sem, scratch=()):
    scratch = list(scratch)
    if job is None:
        return pl.pallas_call(compute, name=name, grid=grid, in_specs=in_specs, out_specs=out_specs,
                              out_shape=out_shape, scratch_shapes=scratch, compiler_params=_cparams(sem))(*ins), None
    n, n_in, n_out, n_scr = job.n, len(ins), len(out_shape), len(scratch)
    n_steps = 1
    for g in grid:
        n_steps *= g

    def body(*refs):
        ins_, job_ins = refs[:n_in], refs[n_in:n_in + n]
        outs, job_outs = refs[n_in + n:n_in + n + n_out], refs[n_in + n + n_out:n_in + 2 * n + n_out]
        rest = refs[n_in + 2 * n + n_out:]
        start, forward, finish = job.phases(job_ins, job_outs, *rest[n_scr:])
        now = 0
        for axis, g in enumerate(grid):
            now = now * g + pl.program_id(axis)
        pl.when(now == 0)(start)
        pl.when(now == n_steps // 2)(forward)
        compute(*ins_, *outs, *rest[:n_scr])
        pl.when(now == n_steps - 1)(finish)

    hbm = pl.BlockSpec(memory_space=pl.ANY)
    res = pl.pallas_call(
        body, name=name, grid=grid, in_specs=list(in_specs) + [hbm] * n,
        out_specs=list(out_specs) + [hbm] * n, out_shape=list(out_shape) + job.out_shape(),
        scratch_shapes=scratch + job.scratch(), compiler_params=_cparams(("arbitrary",) * len(grid)),
    )(*ins, *job.arrays)
    return res[:n_out], res[n_out:]


def to_heads(t, n_heads):
    s_len = t.shape[0]
    return t.reshape(s_len, n_heads, -1).transpose(1, 0, 2)


def from_heads(t):
    return t.transpose(1, 0, 2).reshape(t.shape[1], -1)


def gathered_to_full(t, axis):
    shp = t.shape[1:]
    return jnp.moveaxis(t, 0, axis).reshape(shp[:axis] + (N_DEV * shp[axis],) + shp[axis + 1:])


def full_to_owner_major(g, axis):
    shp = g.shape
    t = jnp.moveaxis(g.reshape(shp[:axis] + (N_DEV, shp[axis] // N_DEV) + shp[axis + 1:]), axis, 0)
    return t.reshape(N_DEV, -1, t.shape[-1])


def _small_rows(shape):
    n = 1
    for s in shape:
        n *= s
    return -(-n // LANES)


def pack_small(arrs, shapes):
    pieces = []
    for n in SMALL:
        flat = arrs[n].reshape(-1)
        flat = jnp.pad(flat, (0, _small_rows(shapes[n]) * LANES - flat.shape[0]))
        pieces.append(flat.reshape(-1, LANES))
    flat = jnp.concatenate(pieces, axis=0)
    return jnp.pad(flat, ((0, -flat.shape[0] % SMALL_ROW_MULTIPLE), (0, 0)))


def unpack_small(flat, shapes):
    out, r = {}, 0
    for n in SMALL:
        rows = _small_rows(shapes[n])
        size = 1
        for s in shapes[n]:
            size *= s
        out[n] = flat[r:r + rows].reshape(-1)[:size].reshape(shapes[n])
        r += rows
    return out


ROW_TM = 256
XATT_TM = 1024
HEAD_TM = 1024
SG_TM = 8 * SG_CHUNK
SB_TILES = (512, 512)
SM_TILE = 1024


def _norm_fwd(x, g, name):
    return prow(f_rms, [x], params=[g.reshape(1, -1)], outs=[(x.shape[1], BF16, False)], tm=ROW_TM, name=name)[0]


def _norm_bwd(x, g, dh, add, name, want_row=True):
    res = prow_vjp(f_rms, [x], params=[g.reshape(1, -1)], cts=[dh], row_grad=[want_row],
                   adds=[add] if want_row else None, tm=ROW_TM, name=name)
    return (res[0], res[1].reshape(-1)) if want_row else (None, res[0].reshape(-1))


def _out_proj(a, w, x, next_gain, alpha, name):
    if next_gain is None:
        return pmm(a, w, res=x, alpha=alpha, name=name), None
    return pmm(a, w, res=x, alpha=alpha, norm_out=next_gain.reshape(1, -1), name=name)


def _in_proj_bwd(d, w, x, gain, dy, name, **kw):
    dx, g_gain = pmm(d, w, tb=True, norm_bwd=(x, gain.reshape(1, -1), dy), name=name, **kw)
    return dx, g_gain.reshape(-1)


def ffn_fwd(x, h, p, tag, next_gain, job=None, after_job=None):
    (gate, up, act), landed = ffn_gate_up(h, p['w_gu'], name=f"{tag}_gu", job=job)
    if job is not None:
        after_job(landed)
    out = _out_proj(act, p['w_down'], x, next_gain, 0.5, f"{tag}_down")
    return out, (x, h, gate, up, act)


def _no_rider(run, **own):
    return run(None)[0]


def _pmm_pair(*args, job, **kw):
    out = pmm(*args, job=job, **kw)
    return out if job is not None else (out, None)


def ffn_bwd(dy, p, saved, tag, with_job=_no_rider):
    x, h, gate, up, act = saved
    d_gate, d_up = with_job(lambda job: ffn_gate_up_bwd(dy, p['w_down'], gate, up, alpha=0.5, name=f"{tag}_dact",
                                                        job=job))
    g_down = pmm(act, dy, ta=True, out_dtype=GRAD_WIRE, alpha=0.5, name=f"{tag}_gdown")
    g_gate = with_job(lambda job: _pmm_pair(h, d_gate, ta=True, out_dtype=GRAD_WIRE, name=f"{tag}_ggate", job=job),
                      w_down=g_down)
    g_gu = jnp.concatenate([g_gate, pmm(h, d_up, ta=True, out_dtype=GRAD_WIRE, name=f"{tag}_gup")], axis=1)
    dx, g_norm = with_job(
        lambda job: _pmm_pair(d_gate, p['w_gu'], a2=d_up, tb=True, norm_bwd=(x, p['norm'].reshape(1, -1), dy),
                              name=f"{tag}_dh", job=job), w_gu=g_gu)
    return dx, {'norm': g_norm.reshape(-1), 'w_gu': g_gu, 'w_down': g_down}


def even_mixer_fwd(x, h, p, next_gain, job=None, after_job=None):
    qkv = pmm(h, p['w_in'][:, :3 * SB_WIDTH], out_dtype=BF16, name="sbg_in_qkv")
    z = pmm(h, p['w_in'][:, 3 * SB_WIDTH:], name="sbg_in_gate")
    q, k, v = (to_heads(qkv[:, i * SB_WIDTH:(i + 1) * SB_WIDTH], SB_HEADS) for i in range(3))
    (o_sb, tot), landed = sb_fwd(q, k, v, tq=SB_TILES[0], tk=SB_TILES[1], name="sb_fwd", job=job)
    if job is not None:
        after_job(landed)
    ln_g, ln_b = p['ln_gain'].reshape(1, -1), p['ln_bias'].reshape(1, -1)
    u, gn = prow(f_gate_prep, [z], params=[ln_g, ln_b], outs=[(SG_WIDTH, F32, False)] * 2, tm=ROW_TM,
                 name="sgu_prep")
    gn_g, u_g = to_heads(gn, SG_GROUPS), to_heads(u, SG_GROUPS)
    b3 = p['sgu_b'].reshape(SG_GROUPS, SG_CHUNK, 1)
    o_sg = prow(f_spatial_gate, [gn_g, u_g], gparams=[p['sgu_w'], b3], outs=[(SG_GROUP_DIM, F32, True)],
                tm=SG_TM, name="sgu_mix")[0]
    cat = jnp.concatenate([from_heads(o_sb), from_heads(o_sg)], axis=-1).astype(BF16)
    out = _out_proj(cat, p['w_out'], x, next_gain, 1.0, "sbg_out")
    return out, (x, h, q, k, v, tot, z, gn_g, u_g, b3, cat)


def even_mixer_bwd(dy, p, saved, job_of=None):
    x, h, q, k, v, tot, z, gn_g, u_g, b3, cat = saved
    d_osb = to_heads(pmm(dy, p['w_out'][:SB_WIDTH], tb=True, name="sbg_dcat_sb"), SB_HEADS)
    d_osg = to_heads(pmm(dy, p['w_out'][SB_WIDTH:], tb=True, name="sbg_dcat_sg"), SG_GROUPS)
    g_out = pmm(cat, dy, ta=True, out_dtype=GRAD_WIRE, name="sbg_gout")
    d_gn_g, d_u_g, g_w, g_b = prow_vjp(f_spatial_gate, [gn_g, u_g], gparams=[p['sgu_w'], b3], cts=[d_osg],
                                       row_grad=[True, True], tm=SG_TM, name="sgu_dmix")
    ln_g, ln_b = p['ln_gain'].reshape(1, -1), p['ln_bias'].reshape(1, -1)
    d_z, g_lng, g_lnb = prow_vjp(f_gate_prep, [z], params=[ln_g, ln_b], cts=[from_heads(d_u_g), from_heads(d_gn_g)],
                                 row_grad=[True], row_dtypes=[BF16], tm=ROW_TM, name="sgu_dprep")
    job = None if job_of is None else job_of({'w_out': g_out})
    (dq, dk, dv), landed = sb_bwd(q, k, v, tot, d_osb, tq=SB_TILES[0], tk=SB_TILES[1], name="sb_bwd", job=job)
    d_proj = jnp.concatenate([from_heads(dq).astype(BF16), from_heads(dk).astype(BF16), from_heads(dv).astype(BF16),
                              d_z], axis=-1)
    g_in = pmm(h, d_proj, ta=True, out_dtype=GRAD_WIRE, name="sbg_gin")
    dx, g_norm = _in_proj_bwd(d_proj, p['w_in'], x, p['norm'], dy, "sbg_dh")
    return dx, {'norm': g_norm, 'w_in': g_in, 'ln_gain': g_lng.reshape(-1), 'ln_bias': g_lnb.reshape(-1),
                'sgu_w': g_w, 'sgu_b': g_b.reshape(SG_GROUPS, SG_CHUNK), 'w_out': g_out}, landed


def mla_fwd(x, h, cos, sin, p, next_gain):
    lora = MLA_Q_LORA + MLA_KV_LORA
    c_q = pmm(h, p['w_in'][:, :MLA_Q_LORA], name="mla_in_q")
    c_kv = pmm(h, p['w_in'][:, MLA_Q_LORA:lora], name="mla_in_kv")
    k_r = pmm(h, p['w_in'][:, lora:], name="mla_in_rope")
    cqn = _norm_fwd(c_q, p['q_lora_gain'], "mla_qlora_norm")
    ckvn = _norm_fwd(c_kv, p['kv_lora_gain'], "mla_kvlora_norm")
    q_h = to_heads(pmm(cqn, p['w_uq'], name="mla_uq"), MLA_HEADS)
    kv_h = pmm(ckvn, p['w_ukv'], out_heads=MLA_NOPE + MLA_V, name="mla_ukv")
    q_g, k_g = p['q_gain'].reshape(1, -1), p['k_gain'].reshape(1, -1)
    kp, v = prow(f_mla_k, [kv_h, k_r, cos, sin], params=[k_g], outs=[(MLA_QK, BF16, True), (MLA_V, BF16, True)],
                 tm=HEAD_TM, name="mla_kprep")
    o, lse = sm_fwd(q_h, kp, v, tq=SM_TILE, tk=SM_TILE, name="mla_att_fwd", q_prep=(cos, sin, q_g))
    o_flat = from_heads(o).astype(BF16)
    out = _out_proj(o_flat, p['w_out'], x, next_gain, 1.0, "mla_out")
    return out, (x, h, c_q, c_kv, k_r, cqn, ckvn, q_h, kv_h, v, kp, o, lse, o_flat, q_g, k_g)


def mla_bwd(dy, cos, sin, p, saved, job_of=None):
    x, h, c_q, c_kv, k_r, cqn, ckvn, q_h, kv_h, v, kp, o, lse, o_flat, q_g, k_g = saved
    do = to_heads(pmm(dy, p['w_out'], tb=True, name="mla_do"), MLA_HEADS)
    g_out = pmm(o_flat, dy, ta=True, out_dtype=GRAD_WIRE, name="mla_gout")
    job = None if job_of is None else job_of({'w_out': g_out})
    (dq_h, dkp, dv, g_qg), landed = sm_bwd(q_h, kp, v, o, lse, do, tq=SM_TILE, tk=SM_TILE, name="mla_att_bwd",
                                           q_prep=(cos, sin, q_g), job=job)
    d_kv_h, dk_r, g_kg = prow_vjp(f_mla_k, [kv_h, k_r, cos, sin], params=[k_g], cts=[dkp, dv],
                                  row_grad=[True, True, False, False], row_dtypes=[BF16, F32], tm=HEAD_TM,
                                  name="mla_dkprep")
    d_q = from_heads(dq_h)
    d_kv = from_heads(d_kv_h)
    g_uq = pmm(cqn, d_q, ta=True, out_dtype=GRAD_WIRE, name="mla_guq")
    d_cqn = pmm(d_q, p['w_uq'], tb=True, name="mla_dcqn")
    g_ukv = pmm(ckvn, d_kv, ta=True, out_dtype=GRAD_WIRE, name="mla_gukv")
    d_ckvn = pmm(d_kv, p['w_ukv'], tb=True, name="mla_dckvn")
    d_cq, g_qlora = _norm_bwd(c_q, p['q_lora_gain'], d_cqn, None, "mla_dqlora_norm")
    d_ckv, g_kvlora = _norm_bwd(c_kv, p['kv_lora_gain'], d_ckvn, None, "mla_dkvlora_norm")
    d_proj = jnp.concatenate([d_cq, d_ckv, dk_r], axis=-1).astype(BF16)
    g_in = pmm(h, d_proj, ta=True, out_dtype=GRAD_WIRE, name="mla_gin")
    dx, g_norm = _in_proj_bwd(d_proj, p['w_in'], x, p['norm'], dy, "mla_dh")
    return dx, {'norm': g_norm, 'w_in': g_in, 'q_lora_gain': g_qlora, 'kv_lora_gain': g_kvlora, 'w_uq': g_uq,
                'w_ukv': g_ukv, 'q_gain': g_qg.reshape(-1), 'k_gain': g_kg.reshape(-1), 'w_out': g_out}, landed


def xattn_fwd(x, hq, mem, p, tag, next_gain):
    hm = _norm_fwd(mem, p['mem_norm'], f"{tag}_mem_norm")
    q_h = ColGroups(pmm(hq, p['wq'], name=f"{tag}_q"), MEM_HEAD_DIM)
    kv = pmm(hm, p['wkv'], name=f"{tag}_kv").reshape(mem.shape[0], MEM_HEADS, 2 * MEM_HEAD_DIM).transpose(1, 0, 2)
    k_h, v_h = kv[..., :MEM_HEAD_DIM], kv[..., MEM_HEAD_DIM:]
    q_g, k_g = p['q_gain'].reshape(1, -1), p['k_gain'].reshape(1, -1)
    o_flat = prow(f_xattn, [q_h], gparams=[k_h, v_h], params=[q_g, k_g], outs=[(MEM_HEAD_DIM, BF16, 'cols')],
                  tm=XATT_TM, name=f"{tag}_att")[0]
    out = _out_proj(o_flat, p['wo'], x, next_gain, 1.0, f"{tag}_out")
    return out, (x, mem, hq, hm, q_h, k_h, v_h, q_g, k_g, o_flat)


def xattn_bwd(dy, p, saved, tag):
    x, mem, hq, hm, q_h, k_h, v_h, q_g, k_g, o_flat = saved
    d_o = ColGroups(pmm(dy, p['wo'], tb=True, name=f"{tag}_do"), MEM_HEAD_DIM)
    g_wo = pmm(o_flat, dy, ta=True, out_dtype=GRAD_WIRE, name=f"{tag}_gwo")
    d_q, dk_h, dv_h, g_qg, g_kg = prow_vjp(f_xattn, [q_h], gparams=[k_h, v_h], params=[q_g, k_g], cts=[d_o],
                                           row_grad=[True], row_dtypes=[BF16], tm=XATT_TM, name=f"{tag}_datt")
    d_kv = jnp.concatenate([dk_h, dv_h], axis=-1).transpose(1, 0, 2).reshape(mem.shape[0], -1).astype(BF16)
    g_wq = pmm(hq, d_q, ta=True, out_dtype=GRAD_WIRE, name=f"{tag}_gwq")
    dx, g_norm = _in_proj_bwd(d_q, p['wq'], x, p['norm'], dy, f"{tag}_dhq")
    g_wkv = pmm(hm, d_kv, ta=True, out_dtype=GRAD_WIRE, name=f"{tag}_gwkv")
    dhm = pmm(d_kv, p['wkv'], tb=True, name=f"{tag}_dhm")
    _, g_mem_norm = _norm_bwd(mem, p['mem_norm'], dhm, None, f"{tag}_dmem_norm", want_row=False)
    return dx, {'norm': g_norm, 'mem_norm': g_mem_norm, 'wq': g_wq, 'wkv': g_wkv, 'q_gain': g_qg.reshape(-1),
                'k_gain': g_kg.reshape(-1), 'wo': g_wo}


def rope_tables(positions):
    half = MLA_ROPE // 2
    inv_freq = ROPE_THETA ** (-jnp.arange(half, dtype=F32) / half)
    ang = positions.astype(F32)[:, None] * inv_freq
    cos, sin = jnp.cos(ang), jnp.sin(ang)
    lead = jnp.ones((ang.shape[0], MLA_NOPE), F32)
    return jnp.concatenate([lead, cos, cos], axis=1), jnp.concatenate([0.0 * lead, sin, sin], axis=1)


FIRST_UNIT = ('ffn_pre_w_gu', 0)
EARLY_UNITS = [('ffn_pre_w_down', 0), ('sbg_w_in', 0)]


def local_step(x, mem, positions, target, w, shards):
    cos, sin = rope_tables(positions)
    full = {}

    def absorb(units, gathered):
        for (n, layer), t in zip(units, gathered):
            full[(n, layer)] = gathered_to_full(t, BIG[n] - 1)

    late_units = [u for u in shards if u != FIRST_UNIT and u not in EARLY_UNITS]
    (h,), gathered = prow(f_rms, [x], params=[w['ffn_pre_norm'][0].reshape(1, -1)], outs=[(x.shape[1], BF16, False)],
                          tm=ROW_TM, name="ffn_pre0_norm", job=CommJob('gather', [shards[FIRST_UNIT]]))
    absorb([FIRST_UNIT], gathered)
    first_ffn_p = {'norm': w['ffn_pre_norm'][0], 'w_gu': full[FIRST_UNIT]}

    def ffn_params(kind, layer):
        return {'norm': w[f'ffn_{kind}_norm'][layer], 'w_gu': full[(f'ffn_{kind}_w_gu', layer)],
                'w_down': full[(f'ffn_{kind}_w_down', layer)]}

    def xattn_params(layer):
        return {'norm': w['xmem_norm'][layer], 'mem_norm': w['xmem_mem_norm'][layer], 'wq': full[('xmem_wq', layer)],
                'wkv': full[('xmem_wkv', layer)], 'q_gain': w['xmem_q_gain'][layer], 'k_gain': w['xmem_k_gain'][layer],
                'wo': full[('xmem_wo', layer)]}

    even_p = {'norm': w['mix_norm'][0], 'ln_gain': w['sgu_ln_gain'][0], 'ln_bias': w['sgu_ln_bias'][0],
              'sgu_w': w['sgu_w'][0], 'sgu_b': w['sgu_b'][0]}

    def early_weights_landed(gathered):
        absorb(EARLY_UNITS, gathered)
        first_ffn_p['w_down'] = full[('ffn_pre_w_down', 0)]
        even_p['w_in'] = full[('sbg_w_in', 0)]

    def late_weights_landed(gathered):
        absorb(late_units, gathered)
        even_p['w_out'] = full[('sbg_w_out', 0)]

    def mla_params():
        return {'norm': w['mix_norm'][1], 'w_in': full[('mla_w_in', 0)], 'q_lora_gain': w['mla_q_lora_gain'][0],
                'kv_lora_gain': w['mla_kv_lora_gain'][0], 'w_uq': full[('mla_w_uq', 0)],
                'w_ukv': full[('mla_w_ukv', 0)], 'q_gain': w['mla_q_gain'][0], 'k_gain': w['mla_k_gain'][0],
                'w_out': full[('mla_w_out', 0)]}

    saved = []
    for layer in range(DEPTH):
        if layer == 0:
            (x, h), s_pre = ffn_fwd(x, h, first_ffn_p, "ffn_pre0", w['mix_norm'][0],
                                    job=CommJob('gather', [shards[u] for u in EARLY_UNITS]),
                                    after_job=early_weights_landed)
        else:
            (x, h), s_pre = ffn_fwd(x, h, ffn_params('pre', layer), f"ffn_pre{layer}", w['mix_norm'][layer])
        if layer % 2 == 0:
            (x, h), s_mix = even_mixer_fwd(x, h, even_p, w['xmem_norm'][layer],
                                           job=CommJob('gather', [shards[u] for u in late_units]),
                                           after_job=late_weights_landed)
        else:
            (x, h), s_mix = mla_fwd(x, h, cos, sin, mla_params(), w['xmem_norm'][layer])
        (x, h), s_x = xattn_fwd(x, h, mem, xattn_params(layer), f"xmem{layer}", w['ffn_post_norm'][layer])
        following = w['ffn_pre_norm'][layer + 1] if layer + 1 < DEPTH else None
        (x, h), s_post = ffn_fwd(x, h, ffn_params('post', layer), f"ffn_post{layer}", following)
        saved.append((s_pre, s_mix, s_x, s_post))

    dx, loss = loss_head(x, target, tm=ROW_TM, name="loss_head")

    ready, riding, landed = {}, [], {}

    def offer(name, layer, g):
        ready[(name, layer)] = full_to_owner_major(g, BIG[name] - 1)

    def ride(name):
        def job_of(own):
            offer(name, 0, own['w_out'])
            riding[:] = list(ready)
            return CommJob('exchange', [ready.pop(u) for u in riding])
        return job_of

    def last_rides(run, **own):
        for kind, g in own.items():
            offer('ffn_pre_' + kind, 0, g)
        units = list(ready)
        if not units:
            return run(None)[0]
        res, arrived = run(CommJob('exchange', [ready.pop(u) for u in units]))
        landed.update(zip(units, arrived))
        return res

    per_layer = []
    for layer in reversed(range(DEPTH)):
        s_pre, s_mix, s_x, s_post = saved[layer]
        dx, g_post = ffn_bwd(dx, ffn_params('post', layer), s_post, f"ffn_post{layer}")
        offer('ffn_post_w_gu', layer, g_post['w_gu'])
        offer('ffn_post_w_down', layer, g_post['w_down'])
        dx, g_x = xattn_bwd(dx, xattn_params(layer), s_x, f"xmem{layer}")
        for n in ('wq', 'wkv', 'wo'):
            offer('xmem_' + n, layer, g_x[n])
        if layer % 2 == 0:
            dx, g_mix, arrived = even_mixer_bwd(dx, even_p, s_mix, job_of=ride('sbg_w_out'))
            landed.update(zip(riding, arrived))
            offer('sbg_w_in', 0, g_mix['w_in'])
        else:
            dx, g_mix, arrived = mla_bwd(dx, cos, sin, mla_params(), s_mix, job_of=ride('mla_w_out'))
            landed.update(zip(riding, arrived))
            for n in ('w_in', 'w_uq', 'w_ukv'):
                offer('mla_' + n, 0, g_mix[n])
        if layer == 0:
            dx, g_pre = ffn_bwd(dx, ffn_params('pre', layer), s_pre, f"ffn_pre{layer}", with_job=last_rides)
        else:
            dx, g_pre = ffn_bwd(dx, ffn_params('pre', layer), s_pre, f"ffn_pre{layer}")
            offer('ffn_pre_w_gu', layer, g_pre['w_gu'])
            offer('ffn_pre_w_down', layer, g_pre['w_down'])
        per_layer.append((layer, g_pre, g_mix, g_x, g_post))
    per_layer.sort(key=lambda t: t[0])
    assert not ready

    def stack(pick):
        return jnp.stack([pick(t) for t in per_layer])

    g_even, g_mla = per_layer[0][2], per_layer[1][2]
    small_grads = {
        'ffn_pre_norm': stack(lambda t: t[1]['norm']), 'mix_norm': stack(lambda t: t[2]['norm']),
        'sgu_ln_gain': g_even['ln_gain'][None], 'sgu_ln_bias': g_even['ln_bias'][None],
        'sgu_w': g_even['sgu_w'][None], 'sgu_b': g_even['sgu_b'][None],
        'mla_q_lora_gain': g_mla['q_lora_gain'][None], 'mla_kv_lora_gain': g_mla['kv_lora_gain'][None],
        'mla_q_gain': g_mla['q_gain'][None], 'mla_k_gain': g_mla['k_gain'][None],
        'xmem_norm': stack(lambda t: t[3]['norm']), 'xmem_mem_norm': stack(lambda t: t[3]['mem_norm']),
        'xmem_q_gain': stack(lambda t: t[3]['q_gain']), 'xmem_k_gain': stack(lambda t: t[3]['k_gain']),
        'ffn_post_norm': stack(lambda t: t[4]['norm']),
    }
    return loss, dx, small_grads, landed


def _device_slot():
    x, y, c = _me()
    return 4 * x + 2 * y + c


def kernel(x, mem, positions, ffn_pre_norm, ffn_pre_w_gu, ffn_pre_w_down, mix_norm, sbg_w_in, sgu_ln_gain, sgu_ln_bias, sgu_w, sgu_b, sbg_w_out, mla_w_in, mla_q_lora_gain, mla_kv_lora_gain, mla_w_uq, mla_w_ukv, mla_q_gain, mla_k_gain, mla_w_out, xmem_norm, xmem_mem_norm, xmem_wq, xmem_wkv, xmem_q_gain, xmem_k_gain, xmem_wo, ffn_post_norm, ffn_post_w_gu, ffn_post_w_down, loss_target, m_ffn_pre_norm, m_ffn_pre_w_gu, m_ffn_pre_w_down, m_mix_norm, m_sbg_w_in, m_sgu_ln_gain, m_sgu_ln_bias, m_sgu_w, m_sgu_b, m_sbg_w_out, m_mla_w_in, m_mla_q_lora_gain, m_mla_kv_lora_gain, m_mla_w_uq, m_mla_w_ukv, m_mla_q_gain, m_mla_k_gain, m_mla_w_out, m_xmem_norm, m_xmem_mem_norm, m_xmem_wq, m_xmem_wkv, m_xmem_q_gain, m_xmem_k_gain, m_xmem_wo, m_ffn_post_norm, m_ffn_post_w_gu, m_ffn_post_w_down, v_ffn_pre_norm, v_ffn_pre_w_gu, v_ffn_pre_w_down, v_mix_norm, v_sbg_w_in, v_sgu_ln_gain, v_sgu_ln_bias, v_sgu_w, v_sgu_b, v_sbg_w_out, v_mla_w_in, v_mla_q_lora_gain, v_mla_kv_lora_gain, v_mla_w_uq, v_mla_w_ukv, v_mla_q_gain, v_mla_k_gain, v_mla_w_out, v_xmem_norm, v_xmem_mem_norm, v_xmem_wq, v_xmem_wkv, v_xmem_q_gain, v_xmem_k_gain, v_xmem_wo, v_ffn_post_norm, v_ffn_post_w_gu, v_ffn_post_w_down):
    args = locals()
    w_in = {n: args[n] for n in WEIGHTS}
    m_in = {n: args["m_" + n] for n in WEIGHTS}
    v_in = {n: args["v_" + n] for n in WEIGHTS}
    slot = _device_slot()

    tiny = jnp.zeros((8, LANES), F32)
    for i, src in enumerate((w_in, m_in, v_in)):
        tiny = tiny.at[i, :64].set(src['mla_q_lora_gain'][0]).at[i + 3, :32].set(src['mla_kv_lora_gain'][0])
    tiny_all = comm_call('gather', [tiny], name="gather_lora_gains")[0]
    full_small = []
    for i, src in enumerate((w_in, m_in, v_in)):
        d = {n: src[n] for n in SMALL}
        d['mla_q_lora_gain'] = tiny_all[:, i, :64].reshape(1, MLA_Q_LORA)
        d['mla_kv_lora_gain'] = tiny_all[:, i + 3, :32].reshape(1, MLA_KV_LORA)
        full_small.append(d)
    w_small, m_small, v_small = full_small
    small_shapes = {n: w_small[n].shape for n in SMALL}

    shards = {(n, layer): w_in[n][layer].astype(BF16) for n in BIG for layer in range(w_in[n].shape[0])}
    loss, dx, grads, landed = local_step(x[0], mem[0], positions[0], loss_target[0], w_small, shards)
    loss = lax.psum(loss, ("x", "y", "c"))
    big_out = {n: adamw([landed[(n, layer)] for layer in range(w_in[n].shape[0])], w_in[n], m_in[n], v_in[n],
                        name=f"adamw_{n}") for n in BIG}

    small_parts = comm_call('gather', [pack_small(grads, small_shapes)], name="gather_small_grads")
    small_out = adamw(small_parts, pack_small(w_small, small_shapes)[None], pack_small(m_small, small_shapes)[None],
                      pack_small(v_small, small_shapes)[None], name="adamw_small")
    small_out = [unpack_small(t[0], small_shapes) for t in small_out]
    for d in small_out:
        for n, width in zip(GAIN_SHARDED, (64, 32)):
            d[n] = lax.dynamic_slice(d[n], (0, slot * width), (1, width))

    outs = [loss, dx[None]]
    for kind, small_d in enumerate(small_out):
        outs += [big_out[n][kind] if n in BIG else small_d[n] for n in WEIGHTS]
    return tuple(outs)
```

```python
import jax
import jax.numpy as jnp
from jax import lax
from jax.experimental import pallas as pl
from jax.experimental.pallas import tpu as pltpu

F32 = jnp.float32
BF16 = jnp.bfloat16
MESH = pl.DeviceIdType.MESH
N_DEV = 8

VMEM_LIMIT_BYTES = 56 * 1024 * 1024
LANES = 128

D_MODEL = 1024
DEPTH = 2
EPS = 1e-6
SB_HEADS, SB_HEAD_DIM = 8, 64
SB_WIDTH = SB_HEADS * SB_HEAD_DIM
SG_GROUPS, SG_GROUP_DIM, SG_CHUNK = 8, 64, 128
SG_WIDTH = SG_GROUPS * SG_GROUP_DIM
MLA_HEADS, MLA_NOPE, MLA_ROPE, MLA_V = 16, 64, 32, 64
MLA_QK = MLA_NOPE + MLA_ROPE
MLA_Q_LORA, MLA_KV_LORA = 512, 256
ROPE_THETA = 10000.0
MEM_HEADS = 4
MEM_HEAD_DIM = D_MODEL // MEM_HEADS

ADAM_LR, ADAM_B1, ADAM_B2, ADAM_EPS, ADAM_WD, ADAM_STEP = 0.001, 0.9, 0.999, 1e-08, 0.01, 10

WEIGHTS = ['ffn_pre_norm', 'ffn_pre_w_gu', 'ffn_pre_w_down', 'mix_norm', 'sbg_w_in', 'sgu_ln_gain', 'sgu_ln_bias',
           'sgu_w', 'sgu_b', 'sbg_w_out', 'mla_w_in', 'mla_q_lora_gain', 'mla_kv_lora_gain', 'mla_w_uq', 'mla_w_ukv',
           'mla_q_gain', 'mla_k_gain', 'mla_w_out', 'xmem_norm', 'xmem_mem_norm', 'xmem_wq', 'xmem_wkv',
           'xmem_q_gain', 'xmem_k_gain', 'xmem_wo', 'ffn_post_norm', 'ffn_post_w_gu', 'ffn_post_w_down']
BIG = {'ffn_pre_w_gu': 2, 'ffn_pre_w_down': 1, 'sbg_w_in': 2, 'sbg_w_out': 1, 'mla_w_in': 1, 'mla_w_uq': 2,
       'mla_w_ukv': 2, 'mla_w_out': 1, 'xmem_wq': 1, 'xmem_wkv': 2, 'xmem_wo': 1, 'ffn_post_w_gu': 2,
       'ffn_post_w_down': 1}
GAIN_SHARDED = ('mla_q_lora_gain', 'mla_kv_lora_gain')
SMALL = [n for n in WEIGHTS if n not in BIG]
GRAD_WIRE = BF16
FFN_SAVE = BF16
SMALL_ROW_MULTIPLE = 16


def _cparams(sem=None):
    return pltpu.CompilerParams(dimension_semantics=sem, vmem_limit_bytes=VMEM_LIMIT_BYTES)


MM_TILE_CAP = 1408
HEAD_MAJOR_ROWS = 4096


def _pick(dim, cap=MM_TILE_CAP):
    if dim % LANES:
        return dim
    return max(t for t in range(LANES, min(dim, cap) + 1, LANES) if dim % t == 0)


def _rms(x, g):
    return x * lax.rsqrt(jnp.mean(x * x, axis=-1, keepdims=True) + EPS) * g


def pmm(a, b, *, a2=None, ta=False, tb=False, out_dtype=F32, res=None, alpha=1.0, norm_out=None, norm_bwd=None,
        out_heads=None, b_tiles=False, job=None, name):
    kdim, m = (a.shape if ta else a.shape[::-1])
    n = b.shape[1] if b_tiles else (b.shape[0] if tb else b.shape[1])
    tm, tn, tk = _pick(m), _pick(n), _pick(kdim)
    if b_tiles:
        assert tb and b.shape[2] == tk
    if out_heads is not None:
        tn, tm = out_heads, _pick(m, HEAD_MAJOR_ROWS)
    whole_rows = norm_out is not None or norm_bwd is not None
    if whole_rows:
        assert tn == n
    if norm_bwd is not None:
        tm = min(tm, 512)
    nk1 = kdim // tk
    nk = nk1 if a2 is None else 2 * nk1
    assert a2 is None or (a2.shape == a.shape and not ta)
    dims = (((0 if ta else 1,), (1 if tb else 0,)), ((), ()))
    n_lead = 2 if a2 is None else 3
    n_extra = (res is not None) + (norm_out is not None) + (0 if norm_bwd is None else 2 + (norm_bwd[2] is not None))

    def body(*refs):
        a_ref, b_ref = refs[:2]
        extra = list(refs[n_lead:n_lead + n_extra])
        outs, acc_ref = refs[n_lead + n_extra:-1], refs[-1]
        i, k = pl.program_id(0), pl.program_id(2)

        @pl.when(k == 0)
        def _():
            acc_ref[...] = jnp.zeros_like(acc_ref)

        def accumulate(lhs_ref):
            rhs = b_ref[k] if b_tiles else b_ref[...]
            acc_ref[...] += lax.dot_general(lhs_ref[...].astype(BF16), rhs.astype(BF16), dims,
                                            preferred_element_type=F32)

        if a2 is None:
            accumulate(a_ref)
        else:
            pl.when(k < nk1)(lambda: accumulate(a_ref))
            pl.when(k >= nk1)(lambda: accumulate(refs[2]))

        @pl.when(k == nk - 1)
        def _():
            r = acc_ref[...]
            if alpha != 1.0:
                r = r * alpha
            if res is not None:
                r = extra.pop(0)[...] + r
            if norm_bwd is None:
                outs[0][...] = r.astype(out_dtype)
            if norm_out is not None:
                outs[1][...] = _rms(r, extra.pop(0)[...]).astype(BF16)
            if norm_bwd is not None:
                x_ref, g_ref = extra.pop(0), extra.pop(0)
                _, pull = jax.vjp(_rms, x_ref[...], g_ref[...])
                dx, dg = pull(r)
                if norm_bwd[2] is not None:
                    dx = dx + extra.pop(0)[...]
                outs[0][...] = dx

                @pl.when(i == 0)
                def _():
                    outs[1][...] = dg

                @pl.when(i != 0)
                def _():
                    outs[1][...] += dg

    gi, gj = m // tm, n // tn
    a_bytes, b_bytes = a.size * a.dtype.itemsize, (n * kdim) * b.dtype.itemsize
    j_outer = not whole_rows and nk == 1 and gj * a_bytes + b_bytes < a_bytes + gi * b_bytes
    grid = (gj, gi, nk) if j_outer else (gi, gj, nk)

    def spec(block, index):
        return pl.BlockSpec(block, (lambda j, i, k: index(i, j, k)) if j_outer else index)

    a_spec = spec((tk, tm), lambda i, j, k: (k, i)) if ta else spec((tm, tk), lambda i, j, k: (i, k))
    b_spec = spec((tn, tk), lambda i, j, k: (j, k)) if tb else spec((tk, tn), lambda i, j, k: (k, j))
    if b_tiles:
        assert tn == n
        b_spec = spec(b.shape, lambda i, j, k: (0, 0, 0))
    o_spec = spec((tm, tn), lambda i, j, k: (i, j))
    g_spec = spec((1, tn), lambda i, j, k: (0, 0))
    ins, in_specs = [a, b], [a_spec, b_spec]
    if a2 is not None:
        in_specs[0] = spec((tm, tk), lambda i, j, k: (i, jnp.minimum(k, nk1 - 1)))
        ins.append(a2)
        in_specs.append(spec((tm, tk), lambda i, j, k: (i, jnp.maximum(k - nk1, 0))))
    if res is not None:
        ins.append(res)
        in_specs.append(o_spec)
    out_shape, out_specs = [jax.ShapeDtypeStruct((m, n), out_dtype)], [o_spec]
    if out_heads is not None:
        out_shape = [jax.ShapeDtypeStruct((n // tn, m, tn), out_dtype)]
        out_specs = [spec((None, tm, tn), lambda i, j, k: (j, i, 0))]
    if norm_out is not None:
        ins.append(norm_out)
        in_specs.append(g_spec)
        out_shape.append(jax.ShapeDtypeStruct((m, n), BF16))
        out_specs.append(o_spec)
    if norm_bwd is not None:
        ins += [t for t in norm_bwd if t is not None]
        in_specs += [o_spec, g_spec] + ([o_spec] if norm_bwd[2] is not None else [])
        out_shape = [jax.ShapeDtypeStruct((m, n), F32), jax.ShapeDtypeStruct((1, n), F32)]
        out_specs = [o_spec, g_spec]
    result, landed = ride_call(
        job, body, name=name, grid=grid, in_specs=in_specs, out_specs=out_specs, out_shape=out_shape, ins=ins,
        scratch=[pltpu.VMEM((tm, tn), F32)],
        sem=("arbitrary" if norm_bwd is not None else "parallel", "parallel", "arbitrary"))
    result = result if whole_rows else result[0]
    return result if job is None else (result, landed)


def ffn_gate_up(h, w_gu, *, name, job=None):
    m, kdim = h.shape
    n = w_gu.shape[1] // 2
    tm, tn = min(_pick(m), 512), _pick(n)
    up_off = n // tn

    def body(a_ref, bg_ref, bu_ref, gate_ref, up_ref, act_ref):
        av = a_ref[...].astype(BF16)
        gate = _dg(av, bg_ref[...].astype(BF16), 1, 0)
        up = _dg(av, bu_ref[...].astype(BF16), 1, 0)
        gate_ref[...] = gate.astype(gate_ref.dtype)
        up_ref[...] = up.astype(up_ref.dtype)
        act_ref[...] = (jax.nn.silu(gate) * up).astype(BF16)

    o_spec = pl.BlockSpec((tm, tn), lambda j, i: (i, j))
    return ride_call(
        job, body, name=name, grid=(n // tn, m // tm),
        in_specs=[pl.BlockSpec((tm, kdim), lambda j, i: (i, 0)), pl.BlockSpec((kdim, tn), lambda j, i: (0, j)),
                  pl.BlockSpec((kdim, tn), lambda j, i: (0, j + up_off))],
        out_specs=[o_spec] * 3,
        out_shape=[jax.ShapeDtypeStruct((m, n), FFN_SAVE), jax.ShapeDtypeStruct((m, n), FFN_SAVE),
                   jax.ShapeDtypeStruct((m, n), BF16)],
        ins=[h, w_gu, w_gu], sem=("parallel", "parallel"))


def ffn_gate_up_bwd(dy, w_down, gate, up, *, alpha, name, job=None):
    m, kdim = dy.shape
    n = w_down.shape[0]
    tm, tn = min(_pick(m), 512), _pick(n)

    def body(a_ref, b_ref, gate_ref, up_ref, dgate_ref, dup_ref):
        d_act = _dg(a_ref[...].astype(BF16), b_ref[...].astype(BF16), 1, 1) * alpha
        _, pull = jax.vjp(lambda g, u: jax.nn.silu(g) * u, gate_ref[...].astype(F32), up_ref[...].astype(F32))
        d_gate, d_up = pull(d_act)
        dgate_ref[...] = d_gate.astype(BF16)
        dup_ref[...] = d_up.astype(BF16)

    o_spec = pl.BlockSpec((tm, tn), lambda j, i: (i, j))
    return ride_call(
        job, body, name=name, grid=(n // tn, m // tm),
        in_specs=[pl.BlockSpec((tm, kdim), lambda j, i: (i, 0)), pl.BlockSpec((tn, kdim), lambda j, i: (j, 0)),
                  o_spec, o_spec],
        out_specs=[o_spec] * 2, out_shape=[jax.ShapeDtypeStruct((m, n), BF16)] * 2,
        ins=[dy, w_down, gate, up], sem=("parallel", "parallel"))


def _dg(a, b, ca, cb):
    return lax.dot_general(a, b, (((ca,), (cb,)), ((), ())), preferred_element_type=F32)


@jax.custom_vjp
def bdot(a, b):
    return _dg(a.astype(BF16), b.astype(BF16), 1, 0)


def _bdot_fwd(a, b):
    ab, bb = a.astype(BF16), b.astype(BF16)
    return _dg(ab, bb, 1, 0), (ab, bb)


def _bdot_bwd(saved, g):
    ab, bb = saved
    gb = g.astype(BF16)
    return _dg(gb, bb, 1, 1), _dg(ab, gb, 0, 0)


bdot.defvjp(_bdot_fwd, _bdot_bwd)


@jax.custom_vjp
def bdot_nt(a, b):
    return _dg(a.astype(BF16), b.astype(BF16), 1, 1)


def _bdot_nt_fwd(a, b):
    ab, bb = a.astype(BF16), b.astype(BF16)
    return _dg(ab, bb, 1, 1), (ab, bb)


def _bdot_nt_bwd(saved, g):
    ab, bb = saved
    gb = g.astype(BF16)
    return _dg(gb, bb, 1, 0), _dg(gb, ab, 0, 0)


bdot_nt.defvjp(_bdot_nt_fwd, _bdot_nt_bwd)


class ColGroups:
    def __init__(self, arr, width):
        self.arr, self.width = arr, width
        self.shape, self.dtype, self.ndim = arr.shape, arr.dtype, 3


def _plain(a):
    return a.arr if isinstance(a, ColGroups) else a


def _row_spec(arr, tm):
    if isinstance(arr, ColGroups):
        return pl.BlockSpec((tm, arr.width), lambda r, g: (r, g))
    if arr.ndim == 3:
        return pl.BlockSpec((None, tm, arr.shape[2]), lambda r, g: (g, r, 0))
    return pl.BlockSpec((tm, arr.shape[1]), lambda r, g: (r, 0))


def _gparam_spec(arr):
    return pl.BlockSpec((None,) + arr.shape[1:], lambda r, g: (g, 0, 0))


def _whole_spec(arr):
    nd = arr.ndim
    return pl.BlockSpec(arr.shape, lambda r, g: (0,) * nd)


def _groups(rows, gparams):
    gs = {a.shape[1] // a.width if isinstance(a, ColGroups) else a.shape[0] for a in rows if a.ndim == 3}
    gs |= {a.shape[0] for a in gparams}
    assert len(gs) <= 1
    return gs.pop() if gs else 1


def prow(fn, rows, gparams=(), params=(), *, outs, tm, name, job=None):
    rows, gparams, params = list(rows), list(gparams), list(params)
    n_groups = _groups(rows, gparams)
    n_rows = rows[0].shape[-2]
    n_in = len(rows) + len(gparams) + len(params)

    def body(*refs):
        vals = [r[...] for r in refs[:n_in]]
        res = fn(*vals)
        for o_ref, r in zip(refs[n_in:], res, strict=True):
            o_ref[...] = r.astype(o_ref.dtype)

    out_shape, out_specs = [], []
    for width, dtype, grouped in outs:
        if grouped == 'cols':
            out_shape.append(jax.ShapeDtypeStruct((n_rows, n_groups * width), dtype))
            out_specs.append(_row_spec(ColGroups(out_shape[-1], width), tm))
            continue
        shp = (n_groups, n_rows, width) if grouped else (n_rows, width)
        out_shape.append(jax.ShapeDtypeStruct(shp, dtype))
        out_specs.append(_row_spec(out_shape[-1], tm))
    result, landed = ride_call(
        job, body, name=name, grid=(n_rows // tm, n_groups),
        in_specs=[_row_spec(a, tm) for a in rows] + [_gparam_spec(a) for a in gparams] + [_whole_spec(a) for a in params],
        out_specs=out_specs, out_shape=out_shape, ins=[*[_plain(a) for a in rows], *gparams, *params],
        sem=("parallel", "arbitrary"))
    return result if job is None else (result, landed)


def prow_vjp(fn, rows, gparams=(), params=(), *, cts, row_grad, adds=None, row_dtypes=None, gparam_grad=None,
             param_grad=None, tm, name):
    rows, gparams, params, cts = list(rows), list(gparams), list(params), list(cts)
    gparam_grad = list(gparam_grad) if gparam_grad is not None else [True] * len(gparams)
    param_grad = list(param_grad) if param_grad is not None else [True] * len(params)
    n_groups = _groups(rows + cts, gparams)
    n_rows = rows[0].shape[-2]
    want_rows = [i for i, w in enumerate(row_grad) if w]
    adds = list(adds) if adds is not None else [None] * len(want_rows)
    row_dtypes = list(row_dtypes) if row_dtypes is not None else [F32] * len(want_rows)
    add_arrays = [a for a in adds if a is not None]
    n_r, n_g, n_p, n_c, n_a = len(rows), len(gparams), len(params), len(cts), len(add_arrays)
    mask = list(row_grad) + gparam_grad + param_grad

    def body(*refs):
        r_id, g_id = pl.program_id(0), pl.program_id(1)
        n_in = n_r + n_g + n_p
        vals = [r[...] for r in refs[:n_in]]
        ct_vals = tuple(r[...].astype(F32) for r in refs[n_in:n_in + n_c])
        add_refs = list(refs[n_in + n_c:n_in + n_c + n_a])
        out_refs = list(refs[n_in + n_c + n_a:])
        diff_idx = [i for i, w in enumerate(mask) if w]

        def wrapped(*diff):
            full = list(vals)
            for i, d in zip(diff_idx, diff):
                full[i] = d
            return tuple(fn(*full))

        _, pull = jax.vjp(wrapped, *[vals[i].astype(F32) for i in diff_idx])
        grads = dict(zip(diff_idx, pull(ct_vals)))
        k = 0
        for j, i in enumerate(want_rows):
            o_ref = out_refs[k]
            k += 1
            gval = grads[i]
            if adds[j] is not None:
                gval = gval + add_refs.pop(0)[...].astype(F32)
            if rows[i].ndim == 2 and n_groups > 1:
                @pl.when(g_id == 0)
                def _(o_ref=o_ref, gval=gval):
                    o_ref[...] = gval.astype(o_ref.dtype)

                @pl.when(g_id != 0)
                def _(o_ref=o_ref, gval=gval):
                    o_ref[...] += gval.astype(o_ref.dtype)
            else:
                o_ref[...] = gval.astype(o_ref.dtype)
        for i in range(n_g):
            if not gparam_grad[i]:
                continue
            o_ref = out_refs[k]
            k += 1
            gval = grads[n_r + i]

            @pl.when(r_id == 0)
            def _(o_ref=o_ref, gval=gval):
                o_ref[g_id] = gval

            @pl.when(r_id != 0)
            def _(o_ref=o_ref, gval=gval):
                o_ref[g_id] += gval
        for i in range(n_p):
            if not param_grad[i]:
                continue
            o_ref = out_refs[k]
            k += 1
            gval = grads[n_r + n_g + i]
            first = jnp.logical_and(r_id == 0, g_id == 0)

            @pl.when(first)
            def _(o_ref=o_ref, gval=gval):
                o_ref[...] = gval

            @pl.when(jnp.logical_not(first))
            def _(o_ref=o_ref, gval=gval):
                o_ref[...] += gval

    out_shape, out_specs = [], []
    for j, i in enumerate(want_rows):
        out_shape.append(jax.ShapeDtypeStruct(rows[i].shape, row_dtypes[j]))
        out_specs.append(_row_spec(rows[i], tm))
    for i in range(n_g):
        if gparam_grad[i]:
            out_shape.append(jax.ShapeDtypeStruct(gparams[i].shape, F32))
            out_specs.append(_whole_spec(gparams[i]))
    for i in range(n_p):
        if param_grad[i]:
            out_shape.append(jax.ShapeDtypeStruct(params[i].shape, F32))
            out_specs.append(_whole_spec(params[i]))
    return pl.pallas_call(
        body, name=name, grid=(n_rows // tm, n_groups),
        in_specs=([_row_spec(a, tm) for a in rows] + [_gparam_spec(a) for a in gparams]
                  + [_whole_spec(a) for a in params] + [_row_spec(a, tm) for a in cts]
                  + [_row_spec(a, tm) for a in add_arrays]),
        out_specs=out_specs, out_shape=out_shape,
        compiler_params=_cparams(("arbitrary", "arbitrary")),
    )(*[_plain(a) for a in rows], *gparams, *params, *[_plain(a) for a in cts], *add_arrays)


def f_rms(x, g):
    return (_rms(x.astype(F32), g),)


def f_gate_prep(z, ln_g, ln_b):
    act = jax.nn.gelu(z)
    u, gg = act[:, :SG_WIDTH], act[:, SG_WIDTH:]
    mu = jnp.mean(gg, axis=-1, keepdims=True)
    var = jnp.mean(jnp.square(gg - mu), axis=-1, keepdims=True)
    return u, (gg - mu) * lax.rsqrt(var + EPS) * ln_g + ln_b


def f_spatial_gate(gn, u, w, b):
    t = lax.broadcasted_iota(jnp.int32, w.shape, 0)
    s = lax.broadcasted_iota(jnp.int32, w.shape, 1)
    w_causal = jnp.where(s <= t, w, 0.0)
    mixed = [bdot(w_causal, gn[i:i + SG_CHUNK]) + b for i in range(0, gn.shape[0], SG_CHUNK)]
    return (u * (mixed[0] if len(mixed) == 1 else jnp.concatenate(mixed, axis=0)),)


def _two_pieces(x):
    hi = x.astype(BF16)
    return hi, (x - hi.astype(F32)).astype(BF16)


@jax.custom_vjp
def place(x, m):
    hi, lo = _two_pieces(x)
    return _dg(hi, m, 1, 0) + _dg(lo, m, 1, 0)


def _place_fwd(x, m):
    return place(x, m), m


def _place_bwd(m, g):
    hi, lo = _two_pieces(g)
    return _dg(hi, m, 1, 1) + _dg(lo, m, 1, 1), jnp.zeros_like(m)


place.defvjp(_place_fwd, _place_bwd)


def _lane_map(rows, cols, entry):
    src = lax.broadcasted_iota(jnp.int32, (rows, cols), 0)
    dst = lax.broadcasted_iota(jnp.int32, (rows, cols), 1)
    return entry(src, dst).astype(BF16)


def _rope_tail(t, cos_w, sin_w):
    half = MLA_ROPE // 2
    lo_half = lambda d: jnp.logical_and(d >= MLA_NOPE, d < MLA_NOPE + half)
    swap = _lane_map(MLA_QK, MLA_QK, lambda s, d: jnp.where(
        jnp.logical_and(d >= MLA_NOPE + half, s == d - half), 1.0,
        jnp.where(jnp.logical_and(lo_half(d), s == d + half), -1.0, 0.0)))
    return t * cos_w + place(t, swap) * sin_w


def f_mla_q(q, cos_w, sin_w, g):
    return (_rope_tail(f_rms(q, g)[0], cos_w, sin_w),)


def f_mla_k(kv, k_r, cos_w, sin_w, g):
    width = MLA_NOPE + MLA_V
    nope = _lane_map(width, MLA_QK, lambda s, d: jnp.where(jnp.logical_and(s == d, d < MLA_NOPE), 1.0, 0.0))
    tail = _lane_map(MLA_ROPE, MLA_QK, lambda s, d: jnp.where(s + MLA_NOPE == d, 1.0, 0.0))
    value = _lane_map(width, MLA_V, lambda s, d: jnp.where(s == d + MLA_NOPE, 1.0, 0.0))
    key = _rope_tail(f_rms(place(kv, nope) + place(k_r, tail), g)[0], cos_w, sin_w)
    return key, place(kv, value)


def f_xattn(q, k, v, q_g, k_g):
    qn, kn = f_rms(q, q_g)[0], f_rms(k, k_g)[0]
    sc = bdot_nt(qn, kn) * (MEM_HEAD_DIM ** -0.5)
    return (bdot(jax.nn.softmax(sc, axis=-1), v),)


def _split_dot(x, tri, pieces=2):
    hi = x.astype(BF16)
    if pieces == 1:
        return _dg(hi, tri, 1, 0)
    lo = (x - hi.astype(F32)).astype(BF16)
    return _dg(hi, tri, 1, 0) + _dg(lo, tri, 1, 0)


def _tri(tk, cmp):
    j = lax.broadcasted_iota(jnp.int32, (tk, tk), 0)
    s = lax.broadcasted_iota(jnp.int32, (tk, tk), 1)
    return cmp(j, s).astype(BF16)


SCAN_CHUNK = 256


def _row_scan(x, tri, reverse, pieces=2):
    n = x.shape[1] // SCAN_CHUNK
    chunks = [x[:, i * SCAN_CHUNK:(i + 1) * SCAN_CHUNK] for i in range(n)]
    out, seen = [None] * n, None
    for i in (reversed(range(n)) if reverse else range(n)):
        local = _split_dot(chunks[i], tri, pieces)
        out[i] = local if seen is None else local + seen
        total = jnp.sum(chunks[i], axis=1, keepdims=True)
        seen = total if seen is None else seen + total
    return (out[0] if n == 1 else jnp.concatenate(out, axis=1)), seen


def _att_specs(s_len, tq, dq, dv):
    q_spec = pl.BlockSpec((None, tq, dq), lambda h, i: (h, i, 0))
    k_spec = pl.BlockSpec((None, s_len, dq), lambda h, i: (h, 0, 0))
    v_spec = pl.BlockSpec((None, s_len, dv), lambda h, i: (h, 0, 0))
    o_spec = pl.BlockSpec((None, tq, dv), lambda h, i: (h, i, 0))
    r_spec = pl.BlockSpec((None, tq, 1), lambda h, i: (h, i, 0))
    return q_spec, k_spec, v_spec, o_spec, r_spec


def _key_blocks(qi, tq, tk):
    return (qi * tq) // tk, ((qi + 1) * tq + tk - 1) // tk


def _earlier(qi, j, tq, tk):
    row = qi * tq + lax.broadcasted_iota(jnp.int32, (tq, tk), 0)
    col = j * tk + lax.broadcasted_iota(jnp.int32, (tq, tk), 1)
    return col < row


LOG2_E = 1.4426950408889634


def _log2_sigmoid(z2):
    return jnp.minimum(z2, 0.0) - jnp.log2(1.0 + jnp.exp2(-jnp.abs(z2)))


def sb_fwd(q, k, v, *, tq, tk, name, job=None):
    n_heads, s_len, d = q.shape
    scale2 = SB_HEAD_DIM ** -0.5 * LOG2_E

    def body(q_ref, k_ref, v_ref, o_ref, tot_ref):
        qi = pl.program_id(1)
        qv = q_ref[...]
        upper = _tri(SCAN_CHUNK, lambda j, s: j > s)
        n_full, n_all = _key_blocks(qi, tq, tk)

        def make_step(masked, last):
            def step(jj, carry):
                acc, rest = carry
                j = last - 1 - jj
                sl = pl.ds(pl.multiple_of(j * tk, tk), tk)
                ks, vs = k_ref[sl, :], v_ref[sl, :]
                z2 = _dg(qv, ks, 1, 1) * scale2
                log_beta = _log2_sigmoid(z2)
                log_stay = log_beta - z2
                if masked:
                    valid = _earlier(qi, j, tq, tk)
                    log_stay = jnp.where(valid, log_stay, 0.0)
                after, total = _row_scan(log_stay, upper, True)
                w = jnp.exp2(log_beta + after + rest)
                if masked:
                    w = jnp.where(valid, w, 0.0)
                acc = acc + _dg(w.astype(BF16), vs, 1, 0)
                return acc, rest + total
            return step

        carry = (jnp.zeros((tq, d), F32), jnp.zeros((tq, 1), F32))
        carry = lax.fori_loop(0, n_all - n_full, make_step(True, n_all), carry)
        acc, rest = lax.fori_loop(0, n_full, make_step(False, n_full), carry)
        o_ref[...] = acc
        tot_ref[...] = rest

    q_spec, k_spec, v_spec, o_spec, r_spec = _att_specs(s_len, tq, d, d)
    return ride_call(
        job, body, name=name, grid=(n_heads, s_len // tq), in_specs=[q_spec, k_spec, v_spec],
        out_specs=[o_spec, r_spec],
        out_shape=[jax.ShapeDtypeStruct((n_heads, s_len, d), F32), jax.ShapeDtypeStruct((n_heads, s_len, 1), F32)],
        ins=[q, k, v], sem=("parallel", "arbitrary"))


def sb_bwd(q, k, v, tot, do, *, tq, tk, name, job=None):
    n_heads, s_len, d = q.shape
    scale = SB_HEAD_DIM ** -0.5
    scale2 = scale * LOG2_E

    def body(q_ref, k_ref, v_ref, tot_ref, do_ref, dq_ref, dk_ref, dv_ref):
        qi = pl.program_id(1)

        @pl.when(qi == 0)
        def _():
            dk_ref[...] = jnp.zeros_like(dk_ref)
            dv_ref[...] = jnp.zeros_like(dv_ref)

        qv = q_ref[...]
        dob = do_ref[...].astype(BF16)
        total = tot_ref[...]
        incl = _tri(SCAN_CHUNK, lambda j, s: j <= s)
        excl = _tri(SCAN_CHUNK, lambda j, s: j < s)
        n_full, n_all = _key_blocks(qi, tq, tk)

        def make_step(masked):
            def step(j, carry):
                dq, stay_before, dl_before = carry
                sl = pl.ds(pl.multiple_of(j * tk, tk), tk)
                ks, vs = k_ref[sl, :], v_ref[sl, :]
                z2 = _dg(qv, ks, 1, 1) * scale2
                log_beta = _log2_sigmoid(z2)
                log_stay = log_beta - z2
                if masked:
                    valid = _earlier(qi, j, tq, tk)
                    log_stay = jnp.where(valid, log_stay, 0.0)
                stay_upto, stay_sum = _row_scan(log_stay, incl, False)
                w = jnp.exp2(log_beta + (total - stay_before) - stay_upto)
                if masked:
                    w = jnp.where(valid, w, 0.0)
                dl = _dg(dob, vs, 1, 1) * w
                dl_upto, dl_sum = _row_scan(dl, excl, False, pieces=1)
                dl_prefix = dl_upto + dl_before
                beta = jnp.exp2(log_beta)
                dz = dl * (1.0 - beta) - beta * dl_prefix
                if masked:
                    dz = jnp.where(valid, dz, 0.0)
                dzb = dz.astype(BF16)
                dq = dq + _dg(dzb, ks, 1, 0)
                dk_ref[sl, :] += _dg(dzb, qv, 0, 0) * scale
                dv_ref[sl, :] += _dg(w.astype(BF16), dob, 0, 0)
                return dq, stay_before + stay_sum, dl_before + dl_sum
            return step

        zero = jnp.zeros((tq, 1), F32)
        carry = lax.fori_loop(0, n_full, make_step(False), (jnp.zeros((tq, d), F32), zero, zero))
        dq, _, _ = lax.fori_loop(n_full, n_all, make_step(True), carry)
        dq_ref[...] = dq * scale

    q_spec, k_spec, v_spec, o_spec, r_spec = _att_specs(s_len, tq, d, d)
    shp = jax.ShapeDtypeStruct((n_heads, s_len, d), F32)
    return ride_call(
        job, body, name=name, grid=(n_heads, s_len // tq), in_specs=[q_spec, k_spec, v_spec, r_spec, o_spec],
        out_specs=[q_spec, k_spec, v_spec], out_shape=[shp, shp, shp], ins=[q, k, v, tot, do],
        sem=("arbitrary", "arbitrary"))


NEG_BIG = -1e30


def _lower_left(rows, cols):
    r = lax.broadcasted_iota(jnp.int32, (rows, cols), 0)
    c = lax.broadcasted_iota(jnp.int32, (rows, cols), 1)
    return c <= r


def _prep_specs(tq, q_prep):
    cos, _, gain = q_prep
    rope_spec = pl.BlockSpec((tq, cos.shape[1]), lambda h, i: (i, 0))
    return [rope_spec, rope_spec, pl.BlockSpec(gain.shape, lambda h, i: (0, 0))]


def sm_fwd(q, k, v, *, tq, tk, name, q_prep=None):
    n_heads, s_len, dq = q.shape
    dv = v.shape[2]
    scale = dq ** -0.5
    assert tq == tk
    half = tk // 2
    n_prep = 0 if q_prep is None else 3

    def body(*refs):
        q_ref, prep_refs = refs[0], refs[1:1 + n_prep]
        k_ref, v_ref, o_ref, lse_ref = refs[1 + n_prep:]
        qi = pl.program_id(1)
        qv = q_ref[...]
        if q_prep is not None:
            qv = f_mla_q(qv, *[r[...] for r in prep_refs])[0].astype(BF16)

        def attend(carry, q_rows, keys, keep):
            acc, m, l = carry
            sc = _dg(q_rows, k_ref[keys, :], 1, 1) * scale
            if keep is not None:
                sc = jnp.where(keep, sc, NEG_BIG)
            m_new = jnp.maximum(m, jnp.max(sc, axis=1, keepdims=True))
            p = jnp.exp(sc - m_new)
            fade = jnp.exp(m - m_new)
            return (fade * acc + _dg(p.astype(BF16), v_ref[keys, :], 1, 0), m_new,
                    fade * l + jnp.sum(p, axis=1, keepdims=True))

        carry = (jnp.zeros((tq, dv), F32), jnp.full((tq, 1), NEG_BIG, F32), jnp.zeros((tq, 1), F32))
        carry = lax.fori_loop(
            0, qi, lambda j, c: attend(c, qv, pl.ds(pl.multiple_of(j * tk, tk), tk), None), carry)
        base = pl.multiple_of(qi * tk, tk)
        carry = attend(carry, qv, pl.ds(base, half), _lower_left(tq, half))
        low = attend(tuple(t[half:] for t in carry), qv[half:], pl.ds(pl.multiple_of(base + half, half), half),
                     _lower_left(half, half))
        acc, m, l = (jnp.concatenate([t[:half], u], axis=0) for t, u in zip(carry, low))
        o_ref[...] = acc / l
        lse_ref[...] = m + jnp.log(l)

    q_spec, k_spec, v_spec, o_spec, r_spec = _att_specs(s_len, tq, dq, dv)
    prep = [] if q_prep is None else list(q_prep)
    return pl.pallas_call(
        body, name=name, grid=(n_heads, s_len // tq),
        in_specs=[q_spec] + ([] if q_prep is None else _prep_specs(tq, q_prep)) + [k_spec, v_spec],
        out_specs=[o_spec, r_spec],
        out_shape=[jax.ShapeDtypeStruct((n_heads, s_len, dv), F32), jax.ShapeDtypeStruct((n_heads, s_len, 1), F32)],
        compiler_params=_cparams(("parallel", "arbitrary")),
    )(q, *prep, k, v)


def sm_bwd(q, k, v, o, lse, do, *, tq, tk, name, q_prep=None, job=None):
    n_heads, s_len, dq = q.shape
    dv = v.shape[2]
    scale = dq ** -0.5
    assert tq == tk
    half = tk // 2
    n_prep = 0 if q_prep is None else 3

    def body(*refs):
        q_ref, prep_refs = refs[0], refs[1:1 + n_prep]
        k_ref, v_ref, o_ref, lse_ref, do_ref, dq_ref, dk_ref, dv_ref = refs[1 + n_prep:9 + n_prep]
        head, qi = pl.program_id(0), pl.program_id(1)

        @pl.when(qi == 0)
        def _():
            dk_ref[...] = jnp.zeros_like(dk_ref)
            dv_ref[...] = jnp.zeros_like(dv_ref)

        q_raw = q_ref[...]
        prep_vals = [r[...] for r in prep_refs]
        qv = q_raw if q_prep is None else f_mla_q(q_raw, *prep_vals)[0].astype(BF16)
        do = do_ref[...]
        dob = do.astype(BF16)
        delta = jnp.sum(do * o_ref[...], axis=1, keepdims=True)
        lse_v = lse_ref[...]

        def attend(rows, keys, keep):
            ks, vs = k_ref[keys, :], v_ref[keys, :]
            p = jnp.exp(_dg(qv[rows], ks, 1, 1) * scale - lse_v[rows])
            if keep is not None:
                p = jnp.where(keep, p, 0.0)
            dv_ref[keys, :] += _dg(p.astype(BF16), dob[rows], 0, 0)
            ds = (p * (_dg(dob[rows], vs, 1, 1) - delta[rows]) * scale).astype(BF16)
            dk_ref[keys, :] += _dg(ds, qv[rows], 0, 0)
            return _dg(ds, ks, 1, 0)

        everything = slice(None)
        dq_acc = lax.fori_loop(
            0, qi, lambda j, acc: acc + attend(everything, pl.ds(pl.multiple_of(j * tk, tk), tk), None),
            jnp.zeros((tq, dq), F32))
        base = pl.multiple_of(qi * tk, tk)
        dq_acc = dq_acc + attend(everything, pl.ds(base, half), _lower_left(tq, half))
        low = attend(slice(half, None), pl.ds(pl.multiple_of(base + half, half), half), _lower_left(half, half))
        dq_acc = jnp.concatenate([dq_acc[:half], dq_acc[half:] + low], axis=0)
        if q_prep is None:
            dq_ref[...] = dq_acc
        else:
            cos, sin, gain = prep_vals
            _, pull = jax.vjp(lambda t, g: f_mla_q(t, cos, sin, g)[0], q_raw, gain)
            dq_raw, d_gain = pull(dq_acc)
            dq_ref[...] = dq_raw.astype(dq_ref.dtype)
            dgain_ref = refs[9 + n_prep]
            first = jnp.logical_and(head == 0, qi == 0)

            @pl.when(first)
            def _():
                dgain_ref[...] = d_gain

            @pl.when(jnp.logical_not(first))
            def _():
                dgain_ref[...] += d_gain

    q_spec, k_spec, v_spec, o_spec, r_spec = _att_specs(s_len, tq, dq, dv)
    out_specs = [q_spec, k_spec, v_spec]
    out_shape = [jax.ShapeDtypeStruct((n_heads, s_len, dq), F32 if q_prep is None else BF16),
                 jax.ShapeDtypeStruct((n_heads, s_len, dq), F32), jax.ShapeDtypeStruct((n_heads, s_len, dv), F32)]
    prep, prep_specs = [], []
    if q_prep is not None:
        prep, prep_specs = list(q_prep), _prep_specs(tq, q_prep)
        out_specs.append(prep_specs[2])
        out_shape.append(jax.ShapeDtypeStruct(q_prep[2].shape, F32))
    return ride_call(
        job, body, name=name, grid=(n_heads, s_len // tq),
        in_specs=[q_spec] + prep_specs + [k_spec, v_spec, o_spec, r_spec, o_spec], out_specs=out_specs,
        out_shape=out_shape, ins=[q, *prep, k, v, o, lse, do], sem=("arbitrary", "arbitrary"))


def loss_head(y, target, *, tm, name):
    n_rows, width = y.shape

    def body(y_ref, t_ref, dy_ref, loss_ref):
        diff = y_ref[...] - t_ref[...]
        dy_ref[...] = diff / width
        part = 0.5 * jnp.sum(jnp.mean(diff * diff, axis=-1, keepdims=True), axis=0, keepdims=True)

        @pl.when(pl.program_id(0) == 0)
        def _():
            loss_ref[...] = jnp.zeros_like(loss_ref)

        loss_ref[...] += jnp.broadcast_to(part, loss_ref.shape)

    spec = pl.BlockSpec((tm, width), lambda r: (r, 0))
    dy, loss = pl.pallas_call(
        body, name=name, grid=(n_rows // tm,), in_specs=[spec, spec],
        out_specs=[spec, pl.BlockSpec((8, LANES), lambda r: (0, 0))],
        out_shape=[jax.ShapeDtypeStruct(y.shape, F32), jax.ShapeDtypeStruct((8, LANES), F32)],
        compiler_params=_cparams(("arbitrary",)),
    )(y, target)
    return dy, loss[0, 0]


ADAM_TILE_ELEMS = 256 * 1024


def _adam_rows(n_rows, width):
    fits = [t for t in range(16, n_rows + 1, 16) if n_rows % t == 0 and t * width <= ADAM_TILE_ELEMS]
    return max(fits) if fits else n_rows


def adamw(parts, w, m, v, *, name):
    n_layers, n_rows, width = w.shape
    assert len(parts) == n_layers
    tm = _adam_rows(n_rows, width)
    n_tiles = n_rows // tm

    def body(*refs):
        p_refs = refs[:n_layers]
        w_ref, m_ref, v_ref, g_ref, d_ref, nm_ref, nv_ref = refs[n_layers:]
        layer = pl.program_id(0)
        for this, p_ref in enumerate(p_refs):
            @pl.when(layer == this)
            def _(p_ref=p_ref):
                g = p_ref[0].astype(F32)
                for i in range(1, N_DEV):
                    g = g + p_ref[i].astype(F32)
                m_new = ADAM_B1 * m_ref[...] + (1.0 - ADAM_B1) * g
                v_new = ADAM_B2 * v_ref[...] + (1.0 - ADAM_B2) * jnp.square(g)
                m_hat = m_new / (1.0 - ADAM_B1 ** ADAM_STEP)
                v_hat = v_new / (1.0 - ADAM_B2 ** ADAM_STEP)
                g_ref[...] = g
                d_ref[...] = -ADAM_LR * (m_hat / (jnp.sqrt(v_hat) + ADAM_EPS) + ADAM_WD * w_ref[...])
                nm_ref[...] = m_new
                nv_ref[...] = v_new

    def part_spec(this):
        def index(layer, r):
            return 0, jnp.where(layer == this, r, jnp.where(layer < this, 0, n_tiles - 1)), 0
        return pl.BlockSpec((N_DEV, tm, width), index)

    spec = pl.BlockSpec((None, tm, width), lambda layer, r: (layer, r, 0))
    shp = jax.ShapeDtypeStruct(w.shape, F32)
    return pl.pallas_call(
        body, name=name, grid=(n_layers, n_tiles),
        in_specs=[part_spec(this) for this in range(n_layers)] + [spec, spec, spec],
        out_specs=[spec] * 4, out_shape=[shp] * 4, compiler_params=_cparams(("arbitrary", "arbitrary")),
    )(*parts, w, m, v)


def _me():
    return lax.axis_index("x"), lax.axis_index("y"), lax.axis_index("c")


N_PEERS = N_DEV - 1


class CommJob:
    def __init__(self, kind, arrays):
        self.kind, self.arrays, self.n = kind, list(arrays), len(arrays)

    def out_shape(self):
        lead = (N_DEV,) if self.kind == 'gather' else ()
        return [jax.ShapeDtypeStruct(lead + a.shape, a.dtype) for a in self.arrays]

    def scratch(self):
        return [pltpu.SemaphoreType.DMA((N_PEERS * self.n,)), pltpu.SemaphoreType.DMA((N_PEERS * self.n,)),
                pltpu.SemaphoreType.DMA((self.n,))]

    def phases(self, in_refs, out_refs, send_sems, recv_sems, local_sems):
        n = self.n
        x, y, c = _me()

        def remote(i, k, src, dst, to):
            return pltpu.make_async_remote_copy(
                src_ref=src, dst_ref=dst, send_sem=send_sems.at[N_PEERS * i + k],
                recv_sem=recv_sems.at[N_PEERS * i + k], device_id=to, device_id_type=MESH)

        if self.kind == 'gather':
            me, sibling = (x, y, c), (x, y, 1 - c)
            chips = [(1 - x, y), (x, 1 - y), (1 - x, 1 - y)]

            def slot(i, px, py, pc):
                return out_refs[i].at[4 * px + 2 * py + pc]

            def copy(i, k, blk, to, src=None):
                return remote(i, k, slot(i, *blk) if src is None else src, slot(i, *blk), to)

            def mine():
                return [pltpu.make_async_copy(in_refs[i], slot(i, *me), local_sems.at[i]) for i in range(n)]

            def first():
                cps = []
                for i in range(n):
                    cps.append(copy(i, 0, me, sibling, src=in_refs[i]))
                    cps += [copy(i, 1 + j, me, (*chip, c), src=in_refs[i]) for j, chip in enumerate(chips)]
                return cps

            def passed():
                return [copy(i, 4 + j, (*chip, c), sibling) for j, chip in enumerate(chips) for i in range(n)]

            def start():
                for cp in mine() + first():
                    cp.start()

            def forward():
                for j, chip in enumerate(chips):
                    for i in range(n):
                        copy(i, 1 + j, (*chip, c), me).wait_recv()
                        copy(i, 4 + j, (*chip, c), sibling).start()

            def finish():
                for i in range(n):
                    copy(i, 0, sibling, me).wait_recv()
                    for j, chip in enumerate(chips):
                        copy(i, 4 + j, (*chip, 1 - c), me).wait_recv()
                for cp in first() + passed():
                    cp.wait_send()
                for cp in mine():
                    cp.wait()

            return start, forward, finish

        my_slot = 4 * x + 2 * y + c

        def mine():
            return [pltpu.make_async_copy(in_refs[i].at[my_slot], out_refs[i].at[my_slot], local_sems.at[i])
                    for i in range(n)]

        def copies():
            cps = []
            for k in range(1, N_DEV):
                px, py, pc = x ^ (k >> 2), y ^ ((k >> 1) & 1), c ^ (k & 1)
                cps += [remote(i, k - 1, in_refs[i].at[4 * px + 2 * py + pc], out_refs[i].at[my_slot], (px, py, pc))
                        for i in range(n)]
            return cps

        def start():
            for cp in mine() + copies():
                cp.start()

        def finish():
            for cp in copies():
                cp.wait_recv()
            for cp in copies():
                cp.wait_send()
            for cp in mine():
                cp.wait()

        return start, (lambda: None), finish


def comm_call(kind, arrays, *, name):
    job = CommJob(kind, arrays)
    n = job.n

    def body(*refs):
        start, forward, finish = job.phases(refs[:n], refs[n:2 * n], *refs[2 * n:])
        start()
        forward()
        finish()

    hbm = pl.BlockSpec(memory_space=pl.ANY)
    return pl.pallas_call(body, name=name, out_shape=job.out_shape(), in_specs=[hbm] * n, out_specs=[hbm] * n,
                          scratch_shapes=job.scratch())(*job.arrays)


def ride_call(job, compute, *, name, grid, in_specs, out_specs, out_shape, ins, sem, scratch=()):
    scratch = list(scratch)
    if job is None:
        return pl.pallas_call(compute, name=name, grid=grid, in_specs=in_specs, out_specs=out_specs,
                              out_shape=out_shape, scratch_shapes=scratch, compiler_params=_cparams(sem))(*ins), None
    n, n_in, n_out, n_scr = job.n, len(ins), len(out_shape), len(scratch)
    n_steps = 1
    for g in grid:
        n_steps *= g

    def body(*refs):
        ins_, job_ins = refs[:n_in], refs[n_in:n_in + n]
        outs, job_outs = refs[n_in + n:n_in + n + n_out], refs[n_in + n + n_out:n_in + 2 * n + n_out]
        rest = refs[n_in + 2 * n + n_out:]
        start, forward, finish = job.phases(job_ins, job_outs, *rest[n_scr:])
        now = 0
        for axis, g in enumerate(grid):
            now = now * g + pl.program_id(axis)
        pl.when(now == 0)(start)
        pl.when(now == n_steps // 2)(forward)
        compute(*ins_, *outs, *rest[:n_scr])
        pl.when(now == n_steps - 1)(finish)

    hbm = pl.BlockSpec(memory_space=pl.ANY)
    res = pl.pallas_call(
        body, name=name, grid=grid, in_specs=list(in_specs) + [hbm] * n,
        out_specs=list(out_specs) + [hbm] * n, out_shape=list(out_shape) + job.out_shape(),
        scratch_shapes=scratch + job.scratch(), compiler_params=_cparams(("arbitrary",) * len(grid)),
    )(*ins, *job.arrays)
    return res[:n_out], res[n_out:]


def to_heads(t, n_heads):
    s_len = t.shape[0]
    return t.reshape(s_len, n_heads, -1).transpose(1, 0, 2)


def from_heads(t):
    return t.transpose(1, 0, 2).reshape(t.shape[1], -1)


def gathered_to_full(t, axis):
    shp = t.shape[1:]
    return jnp.moveaxis(t, 0, axis).reshape(shp[:axis] + (N_DEV * shp[axis],) + shp[axis + 1:])


def full_to_owner_major(g, axis):
    shp = g.shape
    t = jnp.moveaxis(g.reshape(shp[:axis] + (N_DEV, shp[axis] // N_DEV) + shp[axis + 1:]), axis, 0)
    return t.reshape(N_DEV, -1, t.shape[-1])


def _small_rows(shape):
    n = 1
    for s in shape:
        n *= s
    return -(-n // LANES)


def pack_small(arrs, shapes):
    pieces = []
    for n in SMALL:
        flat = arrs[n].reshape(-1)
        flat = jnp.pad(flat, (0, _small_rows(shapes[n]) * LANES - flat.shape[0]))
        pieces.append(flat.reshape(-1, LANES))
    flat = jnp.concatenate(pieces, axis=0)
    return jnp.pad(flat, ((0, -flat.shape[0] % SMALL_ROW_MULTIPLE), (0, 0)))


def unpack_small(flat, shapes):
    out, r = {}, 0
    for n in SMALL:
        rows = _small_rows(shapes[n])
        size = 1
        for s in shapes[n]:
            size *= s
        out[n] = flat[r:r + rows].reshape(-1)[:size].reshape(shapes[n])
        r += rows
    return out


ROW_TM = 256
XATT_TM = 1024
HEAD_TM = 1024
SG_TM = 8 * SG_CHUNK
SB_TILES = (512, 512)
SM_TILE = 1024


def _norm_fwd(x, g, name):
    return prow(f_rms, [x], params=[g.reshape(1, -1)], outs=[(x.shape[1], BF16, False)], tm=ROW_TM, name=name)[0]


def _norm_bwd(x, g, dh, add, name, want_row=True):
    res = prow_vjp(f_rms, [x], params=[g.reshape(1, -1)], cts=[dh], row_grad=[want_row],
                   adds=[add] if want_row else None, tm=ROW_TM, name=name)
    return (res[0], res[1].reshape(-1)) if want_row else (None, res[0].reshape(-1))


def _out_proj(a, w, x, next_gain, alpha, name):
    if next_gain is None:
        return pmm(a, w, res=x, alpha=alpha, name=name), None
    return pmm(a, w, res=x, alpha=alpha, norm_out=next_gain.reshape(1, -1), name=name)


def _in_proj_bwd(d, w, x, gain, dy, name, **kw):
    tk = _pick(w.shape[1])
    if tk < w.shape[1]:
        w, kw = w.reshape(w.shape[0], -1, tk).transpose(1, 0, 2), dict(kw, b_tiles=True)
    dx, g_gain = pmm(d, w, tb=True, norm_bwd=(x, gain.reshape(1, -1), dy), name=name, **kw)
    return dx, g_gain.reshape(-1)


def ffn_fwd(x, h, p, tag, next_gain, job=None, after_job=None):
    (gate, up, act), landed = ffn_gate_up(h, p['w_gu'], name=f"{tag}_gu", job=job)
    if job is not None:
        after_job(landed)
    out = _out_proj(act, p['w_down'], x, next_gain, 0.5, f"{tag}_down")
    return out, (x, h, gate, up, act)


def _no_rider(run, **own):
    return run(None)[0]


def _pmm_pair(*args, job, **kw):
    out = pmm(*args, job=job, **kw)
    return out if job is not None else (out, None)


def ffn_bwd(dy, p, saved, tag, with_job=_no_rider):
    x, h, gate, up, act = saved
    d_gate, d_up = with_job(lambda job: ffn_gate_up_bwd(dy, p['w_down'], gate, up, alpha=0.5, name=f"{tag}_dact",
                                                        job=job))
    g_down = pmm(act, dy, ta=True, out_dtype=GRAD_WIRE, alpha=0.5, name=f"{tag}_gdown")
    g_gate = with_job(lambda job: _pmm_pair(h, d_gate, ta=True, out_dtype=GRAD_WIRE, name=f"{tag}_ggate", job=job),
                      w_down=g_down)
    g_gu = jnp.concatenate([g_gate, pmm(h, d_up, ta=True, out_dtype=GRAD_WIRE, name=f"{tag}_gup")], axis=1)
    tk = _pick(d_gate.shape[1])
    w_gu_tiles = p['w_gu'].reshape(p['w_gu'].shape[0], -1, tk).transpose(1, 0, 2)
    dx, g_norm = with_job(
        lambda job: _pmm_pair(d_gate, w_gu_tiles, a2=d_up, tb=True, b_tiles=True,
                              norm_bwd=(x, p['norm'].reshape(1, -1), dy), name=f"{tag}_dh", job=job), w_gu=g_gu)
    return dx, {'norm': g_norm.reshape(-1), 'w_gu': g_gu, 'w_down': g_down}


def even_mixer_fwd(x, h, p, next_gain, job=None, after_job=None):
    qkv = pmm(h, p['w_in'][:, :3 * SB_WIDTH], out_dtype=BF16, name="sbg_in_qkv")
    z = pmm(h, p['w_in'][:, 3 * SB_WIDTH:], name="sbg_in_gate")
    q, k, v = (to_heads(qkv[:, i * SB_WIDTH:(i + 1) * SB_WIDTH], SB_HEADS) for i in range(3))
    (o_sb, tot), landed = sb_fwd(q, k, v, tq=SB_TILES[0], tk=SB_TILES[1], name="sb_fwd", job=job)
    if job is not None:
        after_job(landed)
    ln_g, ln_b = p['ln_gain'].reshape(1, -1), p['ln_bias'].reshape(1, -1)
    u, gn = prow(f_gate_prep, [z], params=[ln_g, ln_b], outs=[(SG_WIDTH, F32, False)] * 2, tm=ROW_TM,
                 name="sgu_prep")
    gn_g, u_g = to_heads(gn, SG_GROUPS), to_heads(u, SG_GROUPS)
    b3 = p['sgu_b'].reshape(SG_GROUPS, SG_CHUNK, 1)
    o_sg = prow(f_spatial_gate, [gn_g, u_g], gparams=[p['sgu_w'], b3], outs=[(SG_GROUP_DIM, F32, True)],
                tm=SG_TM, name="sgu_mix")[0]
    cat = jnp.concatenate([from_heads(o_sb), from_heads(o_sg)], axis=-1).astype(BF16)
    out = _out_proj(cat, p['w_out'], x, next_gain, 1.0, "sbg_out")
    return out, (x, h, q, k, v, tot, z, gn_g, u_g, b3, cat)


def even_mixer_bwd(dy, p, saved, job_of=None):
    x, h, q, k, v, tot, z, gn_g, u_g, b3, cat = saved
    d_osb = to_heads(pmm(dy, p['w_out'][:SB_WIDTH], tb=True, name="sbg_dcat_sb"), SB_HEADS)
    d_osg = to_heads(pmm(dy, p['w_out'][SB_WIDTH:], tb=True, name="sbg_dcat_sg"), SG_GROUPS)
    g_out = pmm(cat, dy, ta=True, out_dtype=GRAD_WIRE, name="sbg_gout")
    d_gn_g, d_u_g, g_w, g_b = prow_vjp(f_spatial_gate, [gn_g, u_g], gparams=[p['sgu_w'], b3], cts=[d_osg],
                                       row_grad=[True, True], tm=SG_TM, name="sgu_dmix")
    ln_g, ln_b = p['ln_gain'].reshape(1, -1), p['ln_bias'].reshape(1, -1)
    d_z, g_lng, g_lnb = prow_vjp(f_gate_prep, [z], params=[ln_g, ln_b], cts=[from_heads(d_u_g), from_heads(d_gn_g)],
                                 row_grad=[True], row_dtypes=[BF16], tm=ROW_TM, name="sgu_dprep")
    job = None if job_of is None else job_of({'w_out': g_out})
    (dq, dk, dv), landed = sb_bwd(q, k, v, tot, d_osb, tq=SB_TILES[0], tk=SB_TILES[1], name="sb_bwd", job=job)
    d_proj = jnp.concatenate([from_heads(dq).astype(BF16), from_heads(dk).astype(BF16), from_heads(dv).astype(BF16),
                              d_z], axis=-1)
    g_in = pmm(h, d_proj, ta=True, out_dtype=GRAD_WIRE, name="sbg_gin")
    dx, g_norm = _in_proj_bwd(d_proj, p['w_in'], x, p['norm'], dy, "sbg_dh")
    return dx, {'norm': g_norm, 'w_in': g_in, 'ln_gain': g_lng.reshape(-1), 'ln_bias': g_lnb.reshape(-1),
                'sgu_w': g_w, 'sgu_b': g_b.reshape(SG_GROUPS, SG_CHUNK), 'w_out': g_out}, landed


def mla_fwd(x, h, cos, sin, p, next_gain):
    lora = MLA_Q_LORA + MLA_KV_LORA
    c_q = pmm(h, p['w_in'][:, :MLA_Q_LORA], name="mla_in_q")
    c_kv = pmm(h, p['w_in'][:, MLA_Q_LORA:lora], name="mla_in_kv")
    k_r = pmm(h, p['w_in'][:, lora:], name="mla_in_rope")
    cqn = _norm_fwd(c_q, p['q_lora_gain'], "mla_qlora_norm")
    ckvn = _norm_fwd(c_kv, p['kv_lora_gain'], "mla_kvlora_norm")
    q_h = to_heads(pmm(cqn, p['w_uq'], name="mla_uq"), MLA_HEADS)
    kv_h = pmm(ckvn, p['w_ukv'], out_heads=MLA_NOPE + MLA_V, name="mla_ukv")
    q_g, k_g = p['q_gain'].reshape(1, -1), p['k_gain'].reshape(1, -1)
    kp, v = prow(f_mla_k, [kv_h, k_r, cos, sin], params=[k_g], outs=[(MLA_QK, BF16, True), (MLA_V, BF16, True)],
                 tm=HEAD_TM, name="mla_kprep")
    o, lse = sm_fwd(q_h, kp, v, tq=SM_TILE, tk=SM_TILE, name="mla_att_fwd", q_prep=(cos, sin, q_g))
    o_flat = from_heads(o).astype(BF16)
    out = _out_proj(o_flat, p['w_out'], x, next_gain, 1.0, "mla_out")
    return out, (x, h, c_q, c_kv, k_r, cqn, ckvn, q_h, kv_h, v, kp, o, lse, o_flat, q_g, k_g)


def mla_bwd(dy, cos, sin, p, saved, job_of=None):
    x, h, c_q, c_kv, k_r, cqn, ckvn, q_h, kv_h, v, kp, o, lse, o_flat, q_g, k_g = saved
    do = to_heads(pmm(dy, p['w_out'], tb=True, name="mla_do"), MLA_HEADS)
    g_out = pmm(o_flat, dy, ta=True, out_dtype=GRAD_WIRE, name="mla_gout")
    job = None if job_of is None else job_of({'w_out': g_out})
    (dq_h, dkp, dv, g_qg), landed = sm_bwd(q_h, kp, v, o, lse, do, tq=SM_TILE, tk=SM_TILE, name="mla_att_bwd",
                                           q_prep=(cos, sin, q_g), job=job)
    d_kv_h, dk_r, g_kg = prow_vjp(f_mla_k, [kv_h, k_r, cos, sin], params=[k_g], cts=[dkp, dv],
                                  row_grad=[True, True, False, False], row_dtypes=[BF16, F32], tm=HEAD_TM,
                                  name="mla_dkprep")
    d_q = from_heads(dq_h)
    d_kv = from_heads(d_kv_h)
    g_uq = pmm(cqn, d_q, ta=True, out_dtype=GRAD_WIRE, name="mla_guq")
    d_cqn = pmm(d_q, p['w_uq'], tb=True, name="mla_dcqn")
    g_ukv = pmm(ckvn, d_kv, ta=True, out_dtype=GRAD_WIRE, name="mla_gukv")
    d_ckvn = pmm(d_kv, p['w_ukv'], tb=True, name="mla_dckvn")
    d_cq, g_qlora = _norm_bwd(c_q, p['q_lora_gain'], d_cqn, None, "mla_dqlora_norm")
    d_ckv, g_kvlora = _norm_bwd(c_kv, p['kv_lora_gain'], d_ckvn, None, "mla_dkvlora_norm")
    d_proj = jnp.concatenate([d_cq, d_ckv, dk_r], axis=-1).astype(BF16)
    g_in = pmm(h, d_proj, ta=True, out_dtype=GRAD_WIRE, name="mla_gin")
    dx, g_norm = _in_proj_bwd(d_proj, p['w_in'], x, p['norm'], dy, "mla_dh")
    return dx, {'norm': g_norm, 'w_in': g_in, 'q_lora_gain': g_qlora, 'kv_lora_gain': g_kvlora, 'w_uq': g_uq,
                'w_ukv': g_ukv, 'q_gain': g_qg.reshape(-1), 'k_gain': g_kg.reshape(-1), 'w_out': g_out}, landed


def xattn_fwd(x, hq, mem, p, tag, next_gain):
    hm = _norm_fwd(mem, p['mem_norm'], f"{tag}_mem_norm")
    q_h = ColGroups(pmm(hq, p['wq'], name=f"{tag}_q"), MEM_HEAD_DIM)
    kv = pmm(hm, p['wkv'], name=f"{tag}_kv").reshape(mem.shape[0], MEM_HEADS, 2 * MEM_HEAD_DIM).transpose(1, 0, 2)
    k_h, v_h = kv[..., :MEM_HEAD_DIM], kv[..., MEM_HEAD_DIM:]
    q_g, k_g = p['q_gain'].reshape(1, -1), p['k_gain'].reshape(1, -1)
    o_flat = prow(f_xattn, [q_h], gparams=[k_h, v_h], params=[q_g, k_g], outs=[(MEM_HEAD_DIM, BF16, 'cols')],
                  tm=XATT_TM, name=f"{tag}_att")[0]
    out = _out_proj(o_flat, p['wo'], x, next_gain, 1.0, f"{tag}_out")
    return out, (x, mem, hq, hm, q_h, k_h, v_h, q_g, k_g, o_flat)


def xattn_bwd(dy, p, saved, tag):
    x, mem, hq, hm, q_h, k_h, v_h, q_g, k_g, o_flat = saved
    d_o = ColGroups(pmm(dy, p['wo'], tb=True, name=f"{tag}_do"), MEM_HEAD_DIM)
    g_wo = pmm(o_flat, dy, ta=True, out_dtype=GRAD_WIRE, name=f"{tag}_gwo")
    d_q, dk_h, dv_h, g_qg, g_kg = prow_vjp(f_xattn, [q_h], gparams=[k_h, v_h], params=[q_g, k_g], cts=[d_o],
                                           row_grad=[True], row_dtypes=[BF16], tm=XATT_TM, name=f"{tag}_datt")
    d_kv = jnp.concatenate([dk_h, dv_h], axis=-1).transpose(1, 0, 2).reshape(mem.shape[0], -1).astype(BF16)
    g_wq = pmm(hq, d_q, ta=True, out_dtype=GRAD_WIRE, name=f"{tag}_gwq")
    dx, g_norm = _in_proj_bwd(d_q, p['wq'], x, p['norm'], dy, f"{tag}_dhq")
    g_wkv = pmm(hm, d_kv, ta=True, out_dtype=GRAD_WIRE, name=f"{tag}_gwkv")
    dhm = pmm(d_kv, p['wkv'], tb=True, name=f"{tag}_dhm")
    _, g_mem_norm = _norm_bwd(mem, p['mem_norm'], dhm, None, f"{tag}_dmem_norm", want_row=False)
    return dx, {'norm': g_norm, 'mem_norm': g_mem_norm, 'wq': g_wq, 'wkv': g_wkv, 'q_gain': g_qg.reshape(-1),
                'k_gain': g_kg.reshape(-1), 'wo': g_wo}


def rope_tables(positions):
    half = MLA_ROPE // 2
    inv_freq = ROPE_THETA ** (-jnp.arange(half, dtype=F32) / half)
    ang = positions.astype(F32)[:, None] * inv_freq
    cos, sin = jnp.cos(ang), jnp.sin(ang)
    lead = jnp.ones((ang.shape[0], MLA_NOPE), F32)
    return jnp.concatenate([lead, cos, cos], axis=1), jnp.concatenate([0.0 * lead, sin, sin], axis=1)


FIRST_UNIT = ('ffn_pre_w_gu', 0)
EARLY_UNITS = [('ffn_pre_w_down', 0), ('sbg_w_in', 0)]


def local_step(x, mem, positions, target, w, shards):
    cos, sin = rope_tables(positions)
    full = {}

    def absorb(units, gathered):
        for (n, layer), t in zip(units, gathered):
            full[(n, layer)] = gathered_to_full(t, BIG[n] - 1)

    late_units = [u for u in shards if u != FIRST_UNIT and u not in EARLY_UNITS]
    (h,), gathered = prow(f_rms, [x], params=[w['ffn_pre_norm'][0].reshape(1, -1)], outs=[(x.shape[1], BF16, False)],
                          tm=ROW_TM, name="ffn_pre0_norm", job=CommJob('gather', [shards[FIRST_UNIT]]))
    absorb([FIRST_UNIT], gathered)
    first_ffn_p = {'norm': w['ffn_pre_norm'][0], 'w_gu': full[FIRST_UNIT]}

    def ffn_params(kind, layer):
        return {'norm': w[f'ffn_{kind}_norm'][layer], 'w_gu': full[(f'ffn_{kind}_w_gu', layer)],
                'w_down': full[(f'ffn_{kind}_w_down', layer)]}

    def xattn_params(layer):
        return {'norm': w['xmem_norm'][layer], 'mem_norm': w['xmem_mem_norm'][layer], 'wq': full[('xmem_wq', layer)],
                'wkv': full[('xmem_wkv', layer)], 'q_gain': w['xmem_q_gain'][layer], 'k_gain': w['xmem_k_gain'][layer],
                'wo': full[('xmem_wo', layer)]}

    even_p = {'norm': w['mix_norm'][0], 'ln_gain': w['sgu_ln_gain'][0], 'ln_bias': w['sgu_ln_bias'][0],
              'sgu_w': w['sgu_w'][0], 'sgu_b': w['sgu_b'][0]}

    def early_weights_landed(gathered):
        absorb(EARLY_UNITS, gathered)
        first_ffn_p['w_down'] = full[('ffn_pre_w_down', 0)]
        even_p['w_in'] = full[('sbg_w_in', 0)]

    def late_weights_landed(gathered):
        absorb(late_units, gathered)
        even_p['w_out'] = full[('sbg_w_out', 0)]

    def mla_params():
        return {'norm': w['mix_norm'][1], 'w_in': full[('mla_w_in', 0)], 'q_lora_gain': w['mla_q_lora_gain'][0],
                'kv_lora_gain': w['mla_kv_lora_gain'][0], 'w_uq': full[('mla_w_uq', 0)],
                'w_ukv': full[('mla_w_ukv', 0)], 'q_gain': w['mla_q_gain'][0], 'k_gain': w['mla_k_gain'][0],
                'w_out': full[('mla_w_out', 0)]}

    saved = []
    for layer in range(DEPTH):
        if layer == 0:
            (x, h), s_pre = ffn_fwd(x, h, first_ffn_p, "ffn_pre0", w['mix_norm'][0],
                                    job=CommJob('gather', [shards[u] for u in EARLY_UNITS]),
                                    after_job=early_weights_landed)
        else:
            (x, h), s_pre = ffn_fwd(x, h, ffn_params('pre', layer), f"ffn_pre{layer}", w['mix_norm'][layer])
        if layer % 2 == 0:
            (x, h), s_mix = even_mixer_fwd(x, h, even_p, w['xmem_norm'][layer],
                                           job=CommJob('gather', [shards[u] for u in late_units]),
                                           after_job=late_weights_landed)
        else:
            (x, h), s_mix = mla_fwd(x, h, cos, sin, mla_params(), w['xmem_norm'][layer])
        (x, h), s_x = xattn_fwd(x, h, mem, xattn_params(layer), f"xmem{layer}", w['ffn_post_norm'][layer])
        following = w['ffn_pre_norm'][layer + 1] if layer + 1 < DEPTH else None
        (x, h), s_post = ffn_fwd(x, h, ffn_params('post', layer), f"ffn_post{layer}", following)
        saved.append((s_pre, s_mix, s_x, s_post))

    dx, loss = loss_head(x, target, tm=ROW_TM, name="loss_head")

    ready, riding, landed = {}, [], {}

    def offer(name, layer, g):
        ready[(name, layer)] = full_to_owner_major(g, BIG[name] - 1)

    def ride(name):
        def job_of(own):
            offer(name, 0, own['w_out'])
            riding[:] = list(ready)
            return CommJob('exchange', [ready.pop(u) for u in riding])
        return job_of

    def last_rides(run, **own):
        for kind, g in own.items():
            offer('ffn_pre_' + kind, 0, g)
        units = list(ready)
        if not units:
            return run(None)[0]
        res, arrived = run(CommJob('exchange', [ready.pop(u) for u in units]))
        landed.update(zip(units, arrived))
        return res

    per_layer = []
    for layer in reversed(range(DEPTH)):
        s_pre, s_mix, s_x, s_post = saved[layer]
        dx, g_post = ffn_bwd(dx, ffn_params('post', layer), s_post, f"ffn_post{layer}")
        offer('ffn_post_w_gu', layer, g_post['w_gu'])
        offer('ffn_post_w_down', layer, g_post['w_down'])
        dx, g_x = xattn_bwd(dx, xattn_params(layer), s_x, f"xmem{layer}")
        for n in ('wq', 'wkv', 'wo'):
            offer('xmem_' + n, layer, g_x[n])
        if layer % 2 == 0:
            dx, g_mix, arrived = even_mixer_bwd(dx, even_p, s_mix, job_of=ride('sbg_w_out'))
            landed.update(zip(riding, arrived))
            offer('sbg_w_in', 0, g_mix['w_in'])
        else:
            dx, g_mix, arrived = mla_bwd(dx, cos, sin, mla_params(), s_mix, job_of=ride('mla_w_out'))
            landed.update(zip(riding, arrived))
            for n in ('w_in', 'w_uq', 'w_ukv'):
                offer('mla_' + n, 0, g_mix[n])
        if layer == 0:
            dx, g_pre = ffn_bwd(dx, ffn_params('pre', layer), s_pre, f"ffn_pre{layer}", with_job=last_rides)
        else:
            dx, g_pre = ffn_bwd(dx, ffn_params('pre', layer), s_pre, f"ffn_pre{layer}")
            offer('ffn_pre_w_gu', layer, g_pre['w_gu'])
            offer('ffn_pre_w_down', layer, g_pre['w_down'])
        per_layer.append((layer, g_pre, g_mix, g_x, g_post))
    per_layer.sort(key=lambda t: t[0])
    assert not ready

    def stack(pick):
        return jnp.stack([pick(t) for t in per_layer])

    g_even, g_mla = per_layer[0][2], per_layer[1][2]
    small_grads = {
        'ffn_pre_norm': stack(lambda t: t[1]['norm']), 'mix_norm': stack(lambda t: t[2]['norm']),
        'sgu_ln_gain': g_even['ln_gain'][None], 'sgu_ln_bias': g_even['ln_bias'][None],
        'sgu_w': g_even['sgu_w'][None], 'sgu_b': g_even['sgu_b'][None],
        'mla_q_lora_gain': g_mla['q_lora_gain'][None], 'mla_kv_lora_gain': g_mla['kv_lora_gain'][None],
        'mla_q_gain': g_mla['q_gain'][None], 'mla_k_gain': g_mla['k_gain'][None],
        'xmem_norm': stack(lambda t: t[3]['norm']), 'xmem_mem_norm': stack(lambda t: t[3]['mem_norm']),
        'xmem_q_gain': stack(lambda t: t[3]['q_gain']), 'xmem_k_gain': stack(lambda t: t[3]['k_gain']),
        'ffn_post_norm': stack(lambda t: t[4]['norm']),
    }
    return loss, dx, small_grads, landed


def _device_slot():
    x, y, c = _me()
    return 4 * x + 2 * y + c


def kernel(x, mem, positions, ffn_pre_norm, ffn_pre_w_gu, ffn_pre_w_down, mix_norm, sbg_w_in, sgu_ln_gain, sgu_ln_bias, sgu_w, sgu_b, sbg_w_out, mla_w_in, mla_q_lora_gain, mla_kv_lora_gain, mla_w_uq, mla_w_ukv, mla_q_gain, mla_k_gain, mla_w_out, xmem_norm, xmem_mem_norm, xmem_wq, xmem_wkv, xmem_q_gain, xmem_k_gain, xmem_wo, ffn_post_norm, ffn_post_w_gu, ffn_post_w_down, loss_target, m_ffn_pre_norm, m_ffn_pre_w_gu, m_ffn_pre_w_down, m_mix_norm, m_sbg_w_in, m_sgu_ln_gain, m_sgu_ln_bias, m_sgu_w, m_sgu_b, m_sbg_w_out, m_mla_w_in, m_mla_q_lora_gain, m_mla_kv_lora_gain, m_mla_w_uq, m_mla_w_ukv, m_mla_q_gain, m_mla_k_gain, m_mla_w_out, m_xmem_norm, m_xmem_mem_norm, m_xmem_wq, m_xmem_wkv, m_xmem_q_gain, m_xmem_k_gain, m_xmem_wo, m_ffn_post_norm, m_ffn_post_w_gu, m_ffn_post_w_down, v_ffn_pre_norm, v_ffn_pre_w_gu, v_ffn_pre_w_down, v_mix_norm, v_sbg_w_in, v_sgu_ln_gain, v_sgu_ln_bias, v_sgu_w, v_sgu_b, v_sbg_w_out, v_mla_w_in, v_mla_q_lora_gain, v_mla_kv_lora_gain, v_mla_w_uq, v_mla_w_ukv, v_mla_q_gain, v_mla_k_gain, v_mla_w_out, v_xmem_norm, v_xmem_mem_norm, v_xmem_wq, v_xmem_wkv, v_xmem_q_gain, v_xmem_k_gain, v_xmem_wo, v_ffn_post_norm, v_ffn_post_w_gu, v_ffn_post_w_down):
    args = locals()
    w_in = {n: args[n] for n in WEIGHTS}
    m_in = {n: args["m_" + n] for n in WEIGHTS}
    v_in = {n: args["v_" + n] for n in WEIGHTS}
    slot = _device_slot()

    tiny = jnp.zeros((8, LANES), F32)
    for i, src in enumerate((w_in, m_in, v_in)):
        tiny = tiny.at[i, :64].set(src['mla_q_lora_gain'][0]).at[i + 3, :32].set(src['mla_kv_lora_gain'][0])
    tiny_all = comm_call('gather', [tiny], name="gather_lora_gains")[0]
    full_small = []
    for i, src in enumerate((w_in, m_in, v_in)):
        d = {n: src[n] for n in SMALL}
        d['mla_q_lora_gain'] = tiny_all[:, i, :64].reshape(1, MLA_Q_LORA)
        d['mla_kv_lora_gain'] = tiny_all[:, i + 3, :32].reshape(1, MLA_KV_LORA)
        full_small.append(d)
    w_small, m_small, v_small = full_small
    small_shapes = {n: w_small[n].shape for n in SMALL}

    shards = {(n, layer): w_in[n][layer].astype(BF16) for n in BIG for layer in range(w_in[n].shape[0])}
    loss, dx, grads, landed = local_step(x[0], mem[0], positions[0], loss_target[0], w_small, shards)
    loss = lax.psum(loss, ("x", "y", "c"))
    big_out = {n: adamw([landed[(n, layer)] for layer in range(w_in[n].shape[0])], w_in[n], m_in[n], v_in[n],
                        name=f"adamw_{n}") for n in BIG}

    small_parts = comm_call('gather', [pack_small(grads, small_shapes)], name="gather_small_grads")
    small_out = adamw(small_parts, pack_small(w_small, small_shapes)[None], pack_small(m_small, small_shapes)[None],
                      pack_small(v_small, small_shapes)[None], name="adamw_small")
    small_out = [unpack_small(t[0], small_shapes) for t in small_out]
    for d in small_out:
        for n, width in zip(GAIN_SHARDED, (64, 32)):
            d[n] = lax.dynamic_slice(d[n], (0, slot * width), (1, width))

    outs = [loss, dx[None]]
    for kind, small_d in enumerate(small_out):
        outs += [big_out[n][kind] if n in BIG else small_d[n] for n in WEIGHTS]
    return tuple(outs)
```

```python
import jax
import jax.numpy as jnp
from jax import lax
from jax.experimental import pallas as pl
from jax.experimental.pallas import tpu as pltpu

F32 = jnp.float32
BF16 = jnp.bfloat16
MESH = pl.DeviceIdType.MESH
N_DEV = 8

VMEM_LIMIT_BYTES = 56 * 1024 * 1024
LANES = 128

D_MODEL = 1024
DEPTH = 2
EPS = 1e-6
SB_HEADS, SB_HEAD_DIM = 8, 64
SB_WIDTH = SB_HEADS * SB_HEAD_DIM
SG_GROUPS, SG_GROUP_DIM, SG_CHUNK = 8, 64, 128
SG_WIDTH = SG_GROUPS * SG_GROUP_DIM
MLA_HEADS, MLA_NOPE, MLA_ROPE, MLA_V = 16, 64, 32, 64
MLA_QK = MLA_NOPE + MLA_ROPE
MLA_Q_LORA, MLA_KV_LORA = 512, 256
ROPE_THETA = 10000.0
MEM_HEADS = 4
MEM_HEAD_DIM = D_MODEL // MEM_HEADS

ADAM_LR, ADAM_B1, ADAM_B2, ADAM_EPS, ADAM_WD, ADAM_STEP = 0.001, 0.9, 0.999, 1e-08, 0.01, 10

WEIGHTS = ['ffn_pre_norm', 'ffn_pre_w_gu', 'ffn_pre_w_down', 'mix_norm', 'sbg_w_in', 'sgu_ln_gain', 'sgu_ln_bias',
           'sgu_w', 'sgu_b', 'sbg_w_out', 'mla_w_in', 'mla_q_lora_gain', 'mla_kv_lora_gain', 'mla_w_uq', 'mla_w_ukv',
           'mla_q_gain', 'mla_k_gain', 'mla_w_out', 'xmem_norm', 'xmem_mem_norm', 'xmem_wq', 'xmem_wkv',
           'xmem_q_gain', 'xmem_k_gain', 'xmem_wo', 'ffn_post_norm', 'ffn_post_w_gu', 'ffn_post_w_down']
BIG = {'ffn_pre_w_gu': 2, 'ffn_pre_w_down': 1, 'sbg_w_in': 2, 'sbg_w_out': 1, 'mla_w_in': 1, 'mla_w_uq': 2,
       'mla_w_ukv': 2, 'mla_w_out': 1, 'xmem_wq': 1, 'xmem_wkv': 2, 'xmem_wo': 1, 'ffn_post_w_gu': 2,
       'ffn_post_w_down': 1}
GAIN_SHARDED = ('mla_q_lora_gain', 'mla_kv_lora_gain')
SMALL = [n for n in WEIGHTS if n not in BIG]
GRAD_WIRE = BF16
FFN_SAVE = BF16
SMALL_ROW_MULTIPLE = 16


def _cparams(sem=None):
    return pltpu.CompilerParams(dimension_semantics=sem, vmem_limit_bytes=VMEM_LIMIT_BYTES)


MM_TILE_CAP = 1408
HEAD_MAJOR_ROWS = 4096


def _pick(dim, cap=MM_TILE_CAP):
    if dim % LANES:
        return dim
    return max(t for t in range(LANES, min(dim, cap) + 1, LANES) if dim % t == 0)


def _rms(x, g):
    return x * lax.rsqrt(jnp.mean(x * x, axis=-1, keepdims=True) + EPS) * g


def pmm(a, b, *, a2=None, ta=False, tb=False, out_dtype=F32, res=None, alpha=1.0, norm_out=None, norm_bwd=None,
        out_heads=None, b_tiles=False, job=None, name):
    kdim, m = (a.shape if ta else a.shape[::-1])
    n = b.shape[1] if b_tiles else (b.shape[0] if tb else b.shape[1])
    tm, tn, tk = _pick(m), _pick(n), _pick(kdim)
    if b_tiles:
        assert tb and b.shape[2] == tk
    if out_heads is not None:
        tn, tm = out_heads, _pick(m, HEAD_MAJOR_ROWS)
    whole_rows = norm_out is not None or norm_bwd is not None
    if whole_rows:
        assert tn == n
    if norm_bwd is not None:
        tm = min(tm, 512)
    nk1 = kdim // tk
    nk = nk1 if a2 is None else 2 * nk1
    assert a2 is None or (a2.shape == a.shape and not ta)
    dims = (((0 if ta else 1,), (1 if tb else 0,)), ((), ()))
    n_lead = 2 if a2 is None else 3
    n_extra = (res is not None) + (norm_out is not None) + (0 if norm_bwd is None else 2 + (norm_bwd[2] is not None))

    def body(*refs):
        a_ref, b_ref = refs[:2]
        extra = list(refs[n_lead:n_lead + n_extra])
        outs, acc_ref = refs[n_lead + n_extra:-1], refs[-1]
        i, k = pl.program_id(0), pl.program_id(2)

        @pl.when(k == 0)
        def _():
            acc_ref[...] = jnp.zeros_like(acc_ref)

        def accumulate(lhs_ref):
            rhs = b_ref[k] if b_tiles else b_ref[...]
            acc_ref[...] += lax.dot_general(lhs_ref[...].astype(BF16), rhs.astype(BF16), dims,
                                            preferred_element_type=F32)

        if a2 is None:
            accumulate(a_ref)
        else:
            pl.when(k < nk1)(lambda: accumulate(a_ref))
            pl.when(k >= nk1)(lambda: accumulate(refs[2]))

        @pl.when(k == nk - 1)
        def _():
            r = acc_ref[...]
            if alpha != 1.0:
                r = r * alpha
            if res is not None:
                r = extra.pop(0)[...] + r
            if norm_bwd is None:
                outs[0][...] = r.astype(out_dtype)
            if norm_out is not None:
                outs[1][...] = _rms(r, extra.pop(0)[...]).astype(BF16)
            if norm_bwd is not None:
                x_ref, g_ref = extra.pop(0), extra.pop(0)
                _, pull = jax.vjp(_rms, x_ref[...], g_ref[...])
                dx, dg = pull(r)
                if norm_bwd[2] is not None:
                    dx = dx + extra.pop(0)[...]
                outs[0][...] = dx

                @pl.when(i == 0)
                def _():
                    outs[1][...] = dg

                @pl.when(i != 0)
                def _():
                    outs[1][...] += dg

    gi, gj = m // tm, n // tn
    a_bytes, b_bytes = a.size * a.dtype.itemsize, (n * kdim) * b.dtype.itemsize
    j_outer = not whole_rows and nk == 1 and gj * a_bytes + b_bytes < a_bytes + gi * b_bytes
    grid = (gj, gi, nk) if j_outer else (gi, gj, nk)

    def spec(block, index):
        return pl.BlockSpec(block, (lambda j, i, k: index(i, j, k)) if j_outer else index)

    a_spec = spec((tk, tm), lambda i, j, k: (k, i)) if ta else spec((tm, tk), lambda i, j, k: (i, k))
    b_spec = spec((tn, tk), lambda i, j, k: (j, k)) if tb else spec((tk, tn), lambda i, j, k: (k, j))
    if b_tiles:
        assert tn == n
        b_spec = spec(b.shape, lambda i, j, k: (0, 0, 0))
    o_spec = spec((tm, tn), lambda i, j, k: (i, j))
    g_spec = spec((1, tn), lambda i, j, k: (0, 0))
    ins, in_specs = [a, b], [a_spec, b_spec]
    if a2 is not None:
        in_specs[0] = spec((tm, tk), lambda i, j, k: (i, jnp.minimum(k, nk1 - 1)))
        ins.append(a2)
        in_specs.append(spec((tm, tk), lambda i, j, k: (i, jnp.maximum(k - nk1, 0))))
    if res is not None:
        ins.append(res)
        in_specs.append(o_spec)
    out_shape, out_specs = [jax.ShapeDtypeStruct((m, n), out_dtype)], [o_spec]
    if out_heads is not None:
        out_shape = [jax.ShapeDtypeStruct((n // tn, m, tn), out_dtype)]
        out_specs = [spec((None, tm, tn), lambda i, j, k: (j, i, 0))]
    if norm_out is not None:
        ins.append(norm_out)
        in_specs.append(g_spec)
        out_shape.append(jax.ShapeDtypeStruct((m, n), BF16))
        out_specs.append(o_spec)
    if norm_bwd is not None:
        ins += [t for t in norm_bwd if t is not None]
        in_specs += [o_spec, g_spec] + ([o_spec] if norm_bwd[2] is not None else [])
        out_shape = [jax.ShapeDtypeStruct((m, n), F32), jax.ShapeDtypeStruct((1, n), F32)]
        out_specs = [o_spec, g_spec]
    result, landed = ride_call(
        job, body, name=name, grid=grid, in_specs=in_specs, out_specs=out_specs, out_shape=out_shape, ins=ins,
        scratch=[pltpu.VMEM((tm, tn), F32)],
        sem=("arbitrary" if norm_bwd is not None else "parallel", "parallel", "arbitrary"))
    result = result if whole_rows else result[0]
    return result if job is None else (result, landed)


def ffn_gate_up(h, w_gu, *, name, job=None):
    m, kdim = h.shape
    n = w_gu.shape[1] // 2
    tm, tn = min(_pick(m), 512), _pick(n)
    up_off = n // tn

    def body(a_ref, bg_ref, bu_ref, gate_ref, up_ref, act_ref):
        av = a_ref[...].astype(BF16)
        gate = _dg(av, bg_ref[...].astype(BF16), 1, 0)
        up = _dg(av, bu_ref[...].astype(BF16), 1, 0)
        gate_ref[...] = gate.astype(gate_ref.dtype)
        up_ref[...] = up.astype(up_ref.dtype)
        act_ref[...] = (jax.nn.silu(gate) * up).astype(BF16)

    o_spec = pl.BlockSpec((tm, tn), lambda j, i: (i, j))
    return ride_call(
        job, body, name=name, grid=(n // tn, m // tm),
        in_specs=[pl.BlockSpec((tm, kdim), lambda j, i: (i, 0)), pl.BlockSpec((kdim, tn), lambda j, i: (0, j)),
                  pl.BlockSpec((kdim, tn), lambda j, i: (0, j + up_off))],
        out_specs=[o_spec] * 3,
        out_shape=[jax.ShapeDtypeStruct((m, n), FFN_SAVE), jax.ShapeDtypeStruct((m, n), FFN_SAVE),
                   jax.ShapeDtypeStruct((m, n), BF16)],
        ins=[h, w_gu, w_gu], sem=("parallel", "parallel"))


def ffn_gate_up_bwd(dy, w_down, gate, up, *, alpha, name, job=None):
    m, kdim = dy.shape
    n = w_down.shape[0]
    tm, tn = min(_pick(m), 512), _pick(n)

    def body(a_ref, b_ref, gate_ref, up_ref, dgate_ref, dup_ref):
        d_act = _dg(a_ref[...].astype(BF16), b_ref[...].astype(BF16), 1, 1) * alpha
        _, pull = jax.vjp(lambda g, u: jax.nn.silu(g) * u, gate_ref[...].astype(F32), up_ref[...].astype(F32))
        d_gate, d_up = pull(d_act)
        dgate_ref[...] = d_gate.astype(BF16)
        dup_ref[...] = d_up.astype(BF16)

    o_spec = pl.BlockSpec((tm, tn), lambda j, i: (i, j))
    return ride_call(
        job, body, name=name, grid=(n // tn, m // tm),
        in_specs=[pl.BlockSpec((tm, kdim), lambda j, i: (i, 0)), pl.BlockSpec((tn, kdim), lambda j, i: (j, 0)),
                  o_spec, o_spec],
        out_specs=[o_spec] * 2, out_shape=[jax.ShapeDtypeStruct((m, n), BF16)] * 2,
        ins=[dy, w_down, gate, up], sem=("parallel", "parallel"))


def _dg(a, b, ca, cb):
    return lax.dot_general(a, b, (((ca,), (cb,)), ((), ())), preferred_element_type=F32)


@jax.custom_vjp
def bdot(a, b):
    return _dg(a.astype(BF16), b.astype(BF16), 1, 0)


def _bdot_fwd(a, b):
    ab, bb = a.astype(BF16), b.astype(BF16)
    return _dg(ab, bb, 1, 0), (ab, bb)


def _bdot_bwd(saved, g):
    ab, bb = saved
    gb = g.astype(BF16)
    return _dg(gb, bb, 1, 1), _dg(ab, gb, 0, 0)


bdot.defvjp(_bdot_fwd, _bdot_bwd)


@jax.custom_vjp
def bdot_nt(a, b):
    return _dg(a.astype(BF16), b.astype(BF16), 1, 1)


def _bdot_nt_fwd(a, b):
    ab, bb = a.astype(BF16), b.astype(BF16)
    return _dg(ab, bb, 1, 1), (ab, bb)


def _bdot_nt_bwd(saved, g):
    ab, bb = saved
    gb = g.astype(BF16)
    return _dg(gb, bb, 1, 0), _dg(gb, ab, 0, 0)


bdot_nt.defvjp(_bdot_nt_fwd, _bdot_nt_bwd)


class ColGroups:
    def __init__(self, arr, width):
        self.arr, self.width = arr, width
        self.shape, self.dtype, self.ndim = arr.shape, arr.dtype, 3


def _plain(a):
    return a.arr if isinstance(a, ColGroups) else a


def _row_spec(arr, tm):
    if isinstance(arr, ColGroups):
        return pl.BlockSpec((tm, arr.width), lambda r, g: (r, g))
    if arr.ndim == 3:
        return pl.BlockSpec((None, tm, arr.shape[2]), lambda r, g: (g, r, 0))
    return pl.BlockSpec((tm, arr.shape[1]), lambda r, g: (r, 0))


def _gparam_spec(arr):
    return pl.BlockSpec((None,) + arr.shape[1:], lambda r, g: (g, 0, 0))


def _whole_spec(arr):
    nd = arr.ndim
    return pl.BlockSpec(arr.shape, lambda r, g: (0,) * nd)


def _groups(rows, gparams):
    gs = {a.shape[1] // a.width if isinstance(a, ColGroups) else a.shape[0] for a in rows if a.ndim == 3}
    gs |= {a.shape[0] for a in gparams}
    assert len(gs) <= 1
    return gs.pop() if gs else 1


def prow(fn, rows, gparams=(), params=(), *, outs, tm, name, job=None):
    rows, gparams, params = list(rows), list(gparams), list(params)
    n_groups = _groups(rows, gparams)
    n_rows = rows[0].shape[-2]
    n_in = len(rows) + len(gparams) + len(params)

    def body(*refs):
        vals = [r[...] for r in refs[:n_in]]
        res = fn(*vals)
        for o_ref, r in zip(refs[n_in:], res, strict=True):
            o_ref[...] = r.astype(o_ref.dtype)

    out_shape, out_specs = [], []
    for width, dtype, grouped in outs:
        if grouped == 'cols':
            out_shape.append(jax.ShapeDtypeStruct((n_rows, n_groups * width), dtype))
            out_specs.append(_row_spec(ColGroups(out_shape[-1], width), tm))
            continue
        shp = (n_groups, n_rows, width) if grouped else (n_rows, width)
        out_shape.append(jax.ShapeDtypeStruct(shp, dtype))
        out_specs.append(_row_spec(out_shape[-1], tm))
    result, landed = ride_call(
        job, body, name=name, grid=(n_rows // tm, n_groups),
        in_specs=[_row_spec(a, tm) for a in rows] + [_gparam_spec(a) for a in gparams] + [_whole_spec(a) for a in params],
        out_specs=out_specs, out_shape=out_shape, ins=[*[_plain(a) for a in rows], *gparams, *params],
        sem=("parallel", "arbitrary"))
    return result if job is None else (result, landed)


def prow_vjp(fn, rows, gparams=(), params=(), *, cts, row_grad, adds=None, row_dtypes=None, gparam_grad=None,
             param_grad=None, tm, name):
    rows, gparams, params, cts = list(rows), list(gparams), list(params), list(cts)
    gparam_grad = list(gparam_grad) if gparam_grad is not None else [True] * len(gparams)
    param_grad = list(param_grad) if param_grad is not None else [True] * len(params)
    n_groups = _groups(rows + cts, gparams)
    n_rows = rows[0].shape[-2]
    want_rows = [i for i, w in enumerate(row_grad) if w]
    adds = list(adds) if adds is not None else [None] * len(want_rows)
    row_dtypes = list(row_dtypes) if row_dtypes is not None else [F32] * len(want_rows)
    add_arrays = [a for a in adds if a is not None]
    n_r, n_g, n_p, n_c, n_a = len(rows), len(gparams), len(params), len(cts), len(add_arrays)
    mask = list(row_grad) + gparam_grad + param_grad

    def body(*refs):
        r_id, g_id = pl.program_id(0), pl.program_id(1)
        n_in = n_r + n_g + n_p
        vals = [r[...] for r in refs[:n_in]]
        ct_vals = tuple(r[...].astype(F32) for r in refs[n_in:n_in + n_c])
        add_refs = list(refs[n_in + n_c:n_in + n_c + n_a])
        out_refs = list(refs[n_in + n_c + n_a:])
        diff_idx = [i for i, w in enumerate(mask) if w]

        def wrapped(*diff):
            full = list(vals)
            for i, d in zip(diff_idx, diff):
                full[i] = d
            return tuple(fn(*full))

        _, pull = jax.vjp(wrapped, *[vals[i].astype(F32) for i in diff_idx])
        grads = dict(zip(diff_idx, pull(ct_vals)))
        k = 0
        for j, i in enumerate(want_rows):
            o_ref = out_refs[k]
            k += 1
            gval = grads[i]
            if adds[j] is not None:
                gval = gval + add_refs.pop(0)[...].astype(F32)
            if rows[i].ndim == 2 and n_groups > 1:
                @pl.when(g_id == 0)
                def _(o_ref=o_ref, gval=gval):
                    o_ref[...] = gval.astype(o_ref.dtype)

                @pl.when(g_id != 0)
                def _(o_ref=o_ref, gval=gval):
                    o_ref[...] += gval.astype(o_ref.dtype)
            else:
                o_ref[...] = gval.astype(o_ref.dtype)
        for i in range(n_g):
            if not gparam_grad[i]:
                continue
            o_ref = out_refs[k]
            k += 1
            gval = grads[n_r + i]

            @pl.when(r_id == 0)
            def _(o_ref=o_ref, gval=gval):
                o_ref[g_id] = gval

            @pl.when(r_id != 0)
            def _(o_ref=o_ref, gval=gval):
                o_ref[g_id] += gval
        for i in range(n_p):
            if not param_grad[i]:
                continue
            o_ref = out_refs[k]
            k += 1
            gval = grads[n_r + n_g + i]
            first = jnp.logical_and(r_id == 0, g_id == 0)

            @pl.when(first)
            def _(o_ref=o_ref, gval=gval):
                o_ref[...] = gval

            @pl.when(jnp.logical_not(first))
            def _(o_ref=o_ref, gval=gval):
                o_ref[...] += gval

    out_shape, out_specs = [], []
    for j, i in enumerate(want_rows):
        out_shape.append(jax.ShapeDtypeStruct(rows[i].shape, row_dtypes[j]))
        out_specs.append(_row_spec(rows[i], tm))
    for i in range(n_g):
        if gparam_grad[i]:
            out_shape.append(jax.ShapeDtypeStruct(gparams[i].shape, F32))
            out_specs.append(_whole_spec(gparams[i]))
    for i in range(n_p):
        if param_grad[i]:
            out_shape.append(jax.ShapeDtypeStruct(params[i].shape, F32))
            out_specs.append(_whole_spec(params[i]))
    return pl.pallas_call(
        body, name=name, grid=(n_rows // tm, n_groups),
        in_specs=([_row_spec(a, tm) for a in rows] + [_gparam_spec(a) for a in gparams]
                  + [_whole_spec(a) for a in params] + [_row_spec(a, tm) for a in cts]
                  + [_row_spec(a, tm) for a in add_arrays]),
        out_specs=out_specs, out_shape=out_shape,
        compiler_params=_cparams(("arbitrary", "arbitrary")),
    )(*[_plain(a) for a in rows], *gparams, *params, *[_plain(a) for a in cts], *add_arrays)


def f_rms(x, g):
    return (_rms(x.astype(F32), g),)


def f_gate_prep(z, ln_g, ln_b):
    act = jax.nn.gelu(z)
    u, gg = act[:, :SG_WIDTH], act[:, SG_WIDTH:]
    mu = jnp.mean(gg, axis=-1, keepdims=True)
    var = jnp.mean(jnp.square(gg - mu), axis=-1, keepdims=True)
    return u, (gg - mu) * lax.rsqrt(var + EPS) * ln_g + ln_b


SG_PAIR = 2 * SG_GROUP_DIM


def f_spatial_gate(gn, u, w, b):
    t = lax.broadcasted_iota(jnp.int32, (SG_CHUNK, SG_CHUNK), 0)
    s = lax.broadcasted_iota(jnp.int32, (SG_CHUNK, SG_CHUNK), 1)
    mixed = None
    for half in range(2):
        first = half * SG_GROUP_DIM
        take = _lane_map(SG_PAIR, SG_GROUP_DIM, lambda src, dst: jnp.where(src == dst + first, 1.0, 0.0))
        put = _lane_map(SG_GROUP_DIM, SG_PAIR, lambda src, dst: jnp.where(dst == src + first, 1.0, 0.0))
        mine = slice(half * SG_CHUNK, (half + 1) * SG_CHUNK)
        w_causal = jnp.where(s <= t, w[mine], 0.0)
        group = place(gn, take)
        m = [bdot(w_causal, group[i:i + SG_CHUNK]) + b[mine] for i in range(0, group.shape[0], SG_CHUNK)]
        m = place(m[0] if len(m) == 1 else jnp.concatenate(m, axis=0), put)
        mixed = m if mixed is None else mixed + m
    return (u * mixed,)


def _two_pieces(x):
    hi = x.astype(BF16)
    return hi, (x - hi.astype(F32)).astype(BF16)


@jax.custom_vjp
def place(x, m):
    hi, lo = _two_pieces(x)
    return _dg(hi, m, 1, 0) + _dg(lo, m, 1, 0)


def _place_fwd(x, m):
    return place(x, m), m


def _place_bwd(m, g):
    hi, lo = _two_pieces(g)
    return _dg(hi, m, 1, 1) + _dg(lo, m, 1, 1), jnp.zeros_like(m)


place.defvjp(_place_fwd, _place_bwd)


def _lane_map(rows, cols, entry):
    src = lax.broadcasted_iota(jnp.int32, (rows, cols), 0)
    dst = lax.broadcasted_iota(jnp.int32, (rows, cols), 1)
    return entry(src, dst).astype(BF16)


def _rope_tail(t, cos_w, sin_w):
    half = MLA_ROPE // 2
    lo_half = lambda d: jnp.logical_and(d >= MLA_NOPE, d < MLA_NOPE + half)
    swap = _lane_map(MLA_QK, MLA_QK, lambda s, d: jnp.where(
        jnp.logical_and(d >= MLA_NOPE + half, s == d - half), 1.0,
        jnp.where(jnp.logical_and(lo_half(d), s == d + half), -1.0, 0.0)))
    return t * cos_w + place(t, swap) * sin_w


def f_mla_q(q, cos_w, sin_w, g):
    return (_rope_tail(f_rms(q, g)[0], cos_w, sin_w),)


def f_mla_k(kv, k_r, cos_w, sin_w, g):
    width = MLA_NOPE + MLA_V
    nope = _lane_map(width, MLA_QK, lambda s, d: jnp.where(jnp.logical_and(s == d, d < MLA_NOPE), 1.0, 0.0))
    tail = _lane_map(MLA_ROPE, MLA_QK, lambda s, d: jnp.where(s + MLA_NOPE == d, 1.0, 0.0))
    value = _lane_map(width, MLA_V, lambda s, d: jnp.where(s == d + MLA_NOPE, 1.0, 0.0))
    key = _rope_tail(f_rms(place(kv, nope) + place(k_r, tail), g)[0], cos_w, sin_w)
    return key, place(kv, value)


def f_xattn(q, k, v, q_g, k_g):
    qn, kn = f_rms(q, q_g)[0], f_rms(k, k_g)[0]
    sc = bdot_nt(qn, kn) * (MEM_HEAD_DIM ** -0.5)
    return (bdot(jax.nn.softmax(sc, axis=-1), v),)


def _split_dot(x, tri, pieces=2):
    hi = x.astype(BF16)
    if pieces == 1:
        return _dg(hi, tri, 1, 0)
    lo = (x - hi.astype(F32)).astype(BF16)
    return _dg(hi, tri, 1, 0) + _dg(lo, tri, 1, 0)


def _tri(tk, cmp):
    j = lax.broadcasted_iota(jnp.int32, (tk, tk), 0)
    s = lax.broadcasted_iota(jnp.int32, (tk, tk), 1)
    return cmp(j, s).astype(BF16)


SCAN_CHUNK = 256


def _row_scan(x, tri, reverse, pieces=2):
    n = x.shape[1] // SCAN_CHUNK
    chunks = [x[:, i * SCAN_CHUNK:(i + 1) * SCAN_CHUNK] for i in range(n)]
    out, seen = [None] * n, None
    for i in (reversed(range(n)) if reverse else range(n)):
        local = _split_dot(chunks[i], tri, pieces)
        out[i] = local if seen is None else local + seen
        total = jnp.sum(chunks[i], axis=1, keepdims=True)
        seen = total if seen is None else seen + total
    return (out[0] if n == 1 else jnp.concatenate(out, axis=1)), seen


def _att_specs(s_len, tq, dq, dv):
    q_spec = pl.BlockSpec((None, tq, dq), lambda h, i: (h, i, 0))
    k_spec = pl.BlockSpec((None, s_len, dq), lambda h, i: (h, 0, 0))
    v_spec = pl.BlockSpec((None, s_len, dv), lambda h, i: (h, 0, 0))
    o_spec = pl.BlockSpec((None, tq, dv), lambda h, i: (h, i, 0))
    r_spec = pl.BlockSpec((None, tq, 1), lambda h, i: (h, i, 0))
    return q_spec, k_spec, v_spec, o_spec, r_spec


def _key_blocks(qi, tq, tk):
    return (qi * tq) // tk, ((qi + 1) * tq + tk - 1) // tk


def _earlier(qi, j, tq, tk):
    row = qi * tq + lax.broadcasted_iota(jnp.int32, (tq, tk), 0)
    col = j * tk + lax.broadcasted_iota(jnp.int32, (tq, tk), 1)
    return col < row


LOG2_E = 1.4426950408889634


def _log2_sigmoid(z2):
    return jnp.minimum(z2, 0.0) - jnp.log2(1.0 + jnp.exp2(-jnp.abs(z2)))


def sb_fwd(q, k, v, *, tq, tk, name, job=None):
    n_heads, s_len, d = q.shape
    scale2 = SB_HEAD_DIM ** -0.5 * LOG2_E

    def body(q_ref, k_ref, v_ref, o_ref, tot_ref):
        qi = pl.program_id(1)
        qv = q_ref[...]
        upper = _tri(SCAN_CHUNK, lambda j, s: j > s)
        n_full, n_all = _key_blocks(qi, tq, tk)

        def make_step(masked, last):
            def step(jj, carry):
                acc, rest = carry
                j = last - 1 - jj
                sl = pl.ds(pl.multiple_of(j * tk, tk), tk)
                ks, vs = k_ref[sl, :], v_ref[sl, :]
                z2 = _dg(qv, ks, 1, 1) * scale2
                log_beta = _log2_sigmoid(z2)
                log_stay = log_beta - z2
                if masked:
                    valid = _earlier(qi, j, tq, tk)
                    log_stay = jnp.where(valid, log_stay, 0.0)
                after, total = _row_scan(log_stay, upper, True)
                w = jnp.exp2(log_beta + after + rest)
                if masked:
                    w = jnp.where(valid, w, 0.0)
                acc = acc + _dg(w.astype(BF16), vs, 1, 0)
                return acc, rest + total
            return step

        carry = (jnp.zeros((tq, d), F32), jnp.zeros((tq, 1), F32))
        carry = lax.fori_loop(0, n_all - n_full, make_step(True, n_all), carry)
        acc, rest = lax.fori_loop(0, n_full, make_step(False, n_full), carry)
        o_ref[...] = acc
        tot_ref[...] = rest

    q_spec, k_spec, v_spec, o_spec, r_spec = _att_specs(s_len, tq, d, d)
    return ride_call(
        job, body, name=name, grid=(n_heads, s_len // tq), in_specs=[q_spec, k_spec, v_spec],
        out_specs=[o_spec, r_spec],
        out_shape=[jax.ShapeDtypeStruct((n_heads, s_len, d), F32), jax.ShapeDtypeStruct((n_heads, s_len, 1), F32)],
        ins=[q, k, v], sem=("parallel", "arbitrary"))


def sb_bwd(q, k, v, tot, do, *, tq, tk, name, job=None):
    n_heads, s_len, d = q.shape
    scale = SB_HEAD_DIM ** -0.5
    scale2 = scale * LOG2_E

    def body(q_ref, k_ref, v_ref, tot_ref, do_ref, dq_ref, dk_ref, dv_ref):
        qi = pl.program_id(1)

        @pl.when(qi == 0)
        def _():
            dk_ref[...] = jnp.zeros_like(dk_ref)
            dv_ref[...] = jnp.zeros_like(dv_ref)

        qv = q_ref[...]
        dob = do_ref[...].astype(BF16)
        total = tot_ref[...]
        incl = _tri(SCAN_CHUNK, lambda j, s: j <= s)
        excl = _tri(SCAN_CHUNK, lambda j, s: j < s)
        n_full, n_all = _key_blocks(qi, tq, tk)

        def make_step(masked):
            def step(j, carry):
                dq, stay_before, dl_before = carry
                sl = pl.ds(pl.multiple_of(j * tk, tk), tk)
                ks, vs = k_ref[sl, :], v_ref[sl, :]
                z2 = _dg(qv, ks, 1, 1) * scale2
                log_beta = _log2_sigmoid(z2)
                log_stay = log_beta - z2
                if masked:
                    valid = _earlier(qi, j, tq, tk)
                    log_stay = jnp.where(valid, log_stay, 0.0)
                stay_upto, stay_sum = _row_scan(log_stay, incl, False)
                w = jnp.exp2(log_beta + (total - stay_before) - stay_upto)
                if masked:
                    w = jnp.where(valid, w, 0.0)
                dl = _dg(dob, vs, 1, 1) * w
                dl_upto, dl_sum = _row_scan(dl, excl, False, pieces=1)
                dl_prefix = dl_upto + dl_before
                beta = jnp.exp2(log_beta)
                dz = dl * (1.0 - beta) - beta * dl_prefix
                if masked:
                    dz = jnp.where(valid, dz, 0.0)
                dzb = dz.astype(BF16)
                dq = dq + _dg(dzb, ks, 1, 0)
                dk_ref[sl, :] += _dg(dzb, qv, 0, 0) * scale
                dv_ref[sl, :] += _dg(w.astype(BF16), dob, 0, 0)
                return dq, stay_before + stay_sum, dl_before + dl_sum
            return step

        zero = jnp.zeros((tq, 1), F32)
        carry = lax.fori_loop(0, n_full, make_step(False), (jnp.zeros((tq, d), F32), zero, zero))
        dq, _, _ = lax.fori_loop(n_full, n_all, make_step(True), carry)
        dq_ref[...] = dq * scale

    q_spec, k_spec, v_spec, o_spec, r_spec = _att_specs(s_len, tq, d, d)
    shp = jax.ShapeDtypeStruct((n_heads, s_len, d), F32)
    return ride_call(
        job, body, name=name, grid=(n_heads, s_len // tq), in_specs=[q_spec, k_spec, v_spec, r_spec, o_spec],
        out_specs=[q_spec, k_spec, v_spec], out_shape=[shp, shp, shp], ins=[q, k, v, tot, do],
        sem=("arbitrary", "arbitrary"))


NEG_BIG = -1e30


def _lower_left(rows, cols):
    r = lax.broadcasted_iota(jnp.int32, (rows, cols), 0)
    c = lax.broadcasted_iota(jnp.int32, (rows, cols), 1)
    return c <= r


def _prep_specs(tq, q_prep):
    cos, _, gain = q_prep
    rope_spec = pl.BlockSpec((tq, cos.shape[1]), lambda h, i: (i, 0))
    return [rope_spec, rope_spec, pl.BlockSpec(gain.shape, lambda h, i: (0, 0))]


def sm_fwd(q, k, v, *, tq, tk, name, q_prep=None):
    n_heads, s_len, dq = q.shape
    dv = v.shape[2]
    scale = dq ** -0.5
    assert tq == tk
    half = tk // 2
    n_prep = 0 if q_prep is None else 3

    def body(*refs):
        q_ref, prep_refs = refs[0], refs[1:1 + n_prep]
        k_ref, v_ref, o_ref, lse_ref = refs[1 + n_prep:]
        qi = pl.program_id(1)
        qv = q_ref[...]
        if q_prep is not None:
            qv = f_mla_q(qv, *[r[...] for r in prep_refs])[0].astype(BF16)

        def attend(carry, q_rows, keys, keep):
            acc, m, l = carry
            sc = _dg(q_rows, k_ref[keys, :], 1, 1) * scale
            if keep is not None:
                sc = jnp.where(keep, sc, NEG_BIG)
            m_new = jnp.maximum(m, jnp.max(sc, axis=1, keepdims=True))
            p = jnp.exp(sc - m_new)
            fade = jnp.exp(m - m_new)
            return (fade * acc + _dg(p.astype(BF16), v_ref[keys, :], 1, 0), m_new,
                    fade * l + jnp.sum(p, axis=1, keepdims=True))

        carry = (jnp.zeros((tq, dv), F32), jnp.full((tq, 1), NEG_BIG, F32), jnp.zeros((tq, 1), F32))
        carry = lax.fori_loop(
            0, qi, lambda j, c: attend(c, qv, pl.ds(pl.multiple_of(j * tk, tk), tk), None), carry)
        base = pl.multiple_of(qi * tk, tk)
        carry = attend(carry, qv, pl.ds(base, half), _lower_left(tq, half))
        low = attend(tuple(t[half:] for t in carry), qv[half:], pl.ds(pl.multiple_of(base + half, half), half),
                     _lower_left(half, half))
        acc, m, l = (jnp.concatenate([t[:half], u], axis=0) for t, u in zip(carry, low))
        o_ref[...] = acc / l
        lse_ref[...] = m + jnp.log(l)

    q_spec, k_spec, v_spec, o_spec, r_spec = _att_specs(s_len, tq, dq, dv)
    prep = [] if q_prep is None else list(q_prep)
    return pl.pallas_call(
        body, name=name, grid=(n_heads, s_len // tq),
        in_specs=[q_spec] + ([] if q_prep is None else _prep_specs(tq, q_prep)) + [k_spec, v_spec],
        out_specs=[o_spec, r_spec],
        out_shape=[jax.ShapeDtypeStruct((n_heads, s_len, dv), F32), jax.ShapeDtypeStruct((n_heads, s_len, 1), F32)],
        compiler_params=_cparams(("parallel", "arbitrary")),
    )(q, *prep, k, v)


def sm_bwd(q, k, v, o, lse, do, *, tq, tk, name, q_prep=None, job=None):
    n_heads, s_len, dq = q.shape
    dv = v.shape[2]
    scale = dq ** -0.5
    assert tq == tk
    half = tk // 2
    n_prep = 0 if q_prep is None else 3

    def body(*refs):
        q_ref, prep_refs = refs[0], refs[1:1 + n_prep]
        k_ref, v_ref, o_ref, lse_ref, do_ref, dq_ref, dk_ref, dv_ref = refs[1 + n_prep:9 + n_prep]
        head, qi = pl.program_id(0), pl.program_id(1)

        @pl.when(qi == 0)
        def _():
            dk_ref[...] = jnp.zeros_like(dk_ref)
            dv_ref[...] = jnp.zeros_like(dv_ref)

        q_raw = q_ref[...]
        prep_vals = [r[...] for r in prep_refs]
        qv = q_raw if q_prep is None else f_mla_q(q_raw, *prep_vals)[0].astype(BF16)
        do = do_ref[...]
        dob = do.astype(BF16)
        delta = jnp.sum(do * o_ref[...], axis=1, keepdims=True)
        lse_v = lse_ref[...]

        def attend(rows, keys, keep):
            ks, vs = k_ref[keys, :], v_ref[keys, :]
            p = jnp.exp(_dg(qv[rows], ks, 1, 1) * scale - lse_v[rows])
            if keep is not None:
                p = jnp.where(keep, p, 0.0)
            dv_ref[keys, :] += _dg(p.astype(BF16), dob[rows], 0, 0)
            ds = (p * (_dg(dob[rows], vs, 1, 1) - delta[rows]) * scale).astype(BF16)
            dk_ref[keys, :] += _dg(ds, qv[rows], 0, 0)
            return _dg(ds, ks, 1, 0)

        everything = slice(None)
        dq_acc = lax.fori_loop(
            0, qi, lambda j, acc: acc + attend(everything, pl.ds(pl.multiple_of(j * tk, tk), tk), None),
            jnp.zeros((tq, dq), F32))
        base = pl.multiple_of(qi * tk, tk)
        dq_acc = dq_acc + attend(everything, pl.ds(base, half), _lower_left(tq, half))
        low = attend(slice(half, None), pl.ds(pl.multiple_of(base + half, half), half), _lower_left(half, half))
        dq_acc = jnp.concatenate([dq_acc[:half], dq_acc[half:] + low], axis=0)
        if q_prep is None:
            dq_ref[...] = dq_acc
        else:
            cos, sin, gain = prep_vals
            _, pull = jax.vjp(lambda t, g: f_mla_q(t, cos, sin, g)[0], q_raw, gain)
            dq_raw, d_gain = pull(dq_acc)
            dq_ref[...] = dq_raw.astype(dq_ref.dtype)
            dgain_ref = refs[9 + n_prep]
            first = jnp.logical_and(head == 0, qi == 0)

            @pl.when(first)
            def _():
                dgain_ref[...] = d_gain

            @pl.when(jnp.logical_not(first))
            def _():
                dgain_ref[...] += d_gain

    q_spec, k_spec, v_spec, o_spec, r_spec = _att_specs(s_len, tq, dq, dv)
    out_specs = [q_spec, k_spec, v_spec]
    out_shape = [jax.ShapeDtypeStruct((n_heads, s_len, dq), F32 if q_prep is None else BF16),
                 jax.ShapeDtypeStruct((n_heads, s_len, dq), F32), jax.ShapeDtypeStruct((n_heads, s_len, dv), F32)]
    prep, prep_specs = [], []
    if q_prep is not None:
        prep, prep_specs = list(q_prep), _prep_specs(tq, q_prep)
        out_specs.append(prep_specs[2])
        out_shape.append(jax.ShapeDtypeStruct(q_prep[2].shape, F32))
    return ride_call(
        job, body, name=name, grid=(n_heads, s_len // tq),
        in_specs=[q_spec] + prep_specs + [k_spec, v_spec, o_spec, r_spec, o_spec], out_specs=out_specs,
        out_shape=out_shape, ins=[q, *prep, k, v, o, lse, do], sem=("arbitrary", "arbitrary"))


def loss_head(y, target, *, tm, name):
    n_rows, width = y.shape

    def body(y_ref, t_ref, dy_ref, loss_ref):
        diff = y_ref[...] - t_ref[...]
        dy_ref[...] = diff / width
        part = 0.5 * jnp.sum(jnp.mean(diff * diff, axis=-1, keepdims=True), axis=0, keepdims=True)

        @pl.when(pl.program_id(0) == 0)
        def _():
            loss_ref[...] = jnp.zeros_like(loss_ref)

        loss_ref[...] += jnp.broadcast_to(part, loss_ref.shape)

    spec = pl.BlockSpec((tm, width), lambda r: (r, 0))
    dy, loss = pl.pallas_call(
        body, name=name, grid=(n_rows // tm,), in_specs=[spec, spec],
        out_specs=[spec, pl.BlockSpec((8, LANES), lambda r: (0, 0))],
        out_shape=[jax.ShapeDtypeStruct(y.shape, F32), jax.ShapeDtypeStruct((8, LANES), F32)],
        compiler_params=_cparams(("arbitrary",)),
    )(y, target)
    return dy, loss[0, 0]


ADAM_TILE_ELEMS = 256 * 1024


def _adam_rows(n_rows, width):
    fits = [t for t in range(16, n_rows + 1, 16) if n_rows % t == 0 and t * width <= ADAM_TILE_ELEMS]
    return max(fits) if fits else n_rows


def adamw(parts, w, m, v, *, name):
    n_layers, n_rows, width = w.shape
    assert len(parts) == n_layers
    tm = _adam_rows(n_rows, width)
    n_tiles = n_rows // tm

    def body(*refs):
        p_refs = refs[:n_layers]
        w_ref, m_ref, v_ref, g_ref, d_ref, nm_ref, nv_ref = refs[n_layers:]
        layer = pl.program_id(0)
        for this, p_ref in enumerate(p_refs):
            @pl.when(layer == this)
            def _(p_ref=p_ref):
                g = p_ref[0].astype(F32)
                for i in range(1, N_DEV):
                    g = g + p_ref[i].astype(F32)
                m_new = ADAM_B1 * m_ref[...] + (1.0 - ADAM_B1) * g
                v_new = ADAM_B2 * v_ref[...] + (1.0 - ADAM_B2) * jnp.square(g)
                m_hat = m_new / (1.0 - ADAM_B1 ** ADAM_STEP)
                v_hat = v_new / (1.0 - ADAM_B2 ** ADAM_STEP)
                g_ref[...] = g
                d_ref[...] = -ADAM_LR * (m_hat / (jnp.sqrt(v_hat) + ADAM_EPS) + ADAM_WD * w_ref[...])
                nm_ref[...] = m_new
                nv_ref[...] = v_new

    def part_spec(this):
        def index(layer, r):
            return 0, jnp.where(layer == this, r, jnp.where(layer < this, 0, n_tiles - 1)), 0
        return pl.BlockSpec((N_DEV, tm, width), index)

    spec = pl.BlockSpec((None, tm, width), lambda layer, r: (layer, r, 0))
    shp = jax.ShapeDtypeStruct(w.shape, F32)
    return pl.pallas_call(
        body, name=name, grid=(n_layers, n_tiles),
        in_specs=[part_spec(this) for this in range(n_layers)] + [spec, spec, spec],
        out_specs=[spec] * 4, out_shape=[shp] * 4, compiler_params=_cparams(("arbitrary", "arbitrary")),
    )(*parts, w, m, v)


def _me():
    return lax.axis_index("x"), lax.axis_index("y"), lax.axis_index("c")


N_PEERS = N_DEV - 1


class CommJob:
    def __init__(self, kind, arrays):
        self.kind, self.arrays, self.n = kind, list(arrays), len(arrays)

    def out_shape(self):
        lead = (N_DEV,) if self.kind == 'gather' else ()
        return [jax.ShapeDtypeStruct(lead + a.shape, a.dtype) for a in self.arrays]

    def scratch(self):
        return [pltpu.SemaphoreType.DMA((N_PEERS * self.n,)), pltpu.SemaphoreType.DMA((N_PEERS * self.n,)),
                pltpu.SemaphoreType.DMA((self.n,))]

    def phases(self, in_refs, out_refs, send_sems, recv_sems, local_sems):
        n = self.n
        x, y, c = _me()

        def remote(i, k, src, dst, to):
            return pltpu.make_async_remote_copy(
                src_ref=src, dst_ref=dst, send_sem=send_sems.at[N_PEERS * i + k],
                recv_sem=recv_sems.at[N_PEERS * i + k], device_id=to, device_id_type=MESH)

        if self.kind == 'gather':
            me, sibling = (x, y, c), (x, y, 1 - c)
            chips = [(1 - x, y), (x, 1 - y), (1 - x, 1 - y)]

            def slot(i, px, py, pc):
                return out_refs[i].at[4 * px + 2 * py + pc]

            def copy(i, k, blk, to, src=None):
                return remote(i, k, slot(i, *blk) if src is None else src, slot(i, *blk), to)

            def mine():
                return [pltpu.make_async_copy(in_refs[i], slot(i, *me), local_sems.at[i]) for i in range(n)]

            def first():
                cps = []
                for i in range(n):
                    cps.append(copy(i, 0, me, sibling, src=in_refs[i]))
                    cps += [copy(i, 1 + j, me, (*chip, c), src=in_refs[i]) for j, chip in enumerate(chips)]
                return cps

            def passed():
                return [copy(i, 4 + j, (*chip, c), sibling) for j, chip in enumerate(chips) for i in range(n)]

            def start():
                for cp in mine() + first():
                    cp.start()

            def forward():
                for j, chip in enumerate(chips):
                    for i in range(n):
                        copy(i, 1 + j, (*chip, c), me).wait_recv()
                        copy(i, 4 + j, (*chip, c), sibling).start()

            def finish():
                for i in range(n):
                    copy(i, 0, sibling, me).wait_recv()
                    for j, chip in enumerate(chips):
                        copy(i, 4 + j, (*chip, 1 - c), me).wait_recv()
                for cp in first() + passed():
                    cp.wait_send()
                for cp in mine():
                    cp.wait()

            return start, forward, finish

        my_slot = 4 * x + 2 * y + c

        def mine():
            return [pltpu.make_async_copy(in_refs[i].at[my_slot], out_refs[i].at[my_slot], local_sems.at[i])
                    for i in range(n)]

        def copies():
            cps = []
            for k in range(1, N_DEV):
                px, py, pc = x ^ (k >> 2), y ^ ((k >> 1) & 1), c ^ (k & 1)
                cps += [remote(i, k - 1, in_refs[i].at[4 * px + 2 * py + pc], out_refs[i].at[my_slot], (px, py, pc))
                        for i in range(n)]
            return cps

        def start():
            for cp in mine() + copies():
                cp.start()

        def finish():
            for cp in copies():
                cp.wait_recv()
            for cp in copies():
                cp.wait_send()
            for cp in mine():
                cp.wait()

        return start, (lambda: None), finish


def comm_call(kind, arrays, *, name):
    job = CommJob(kind, arrays)
    n = job.n

    def body(*refs):
        start, forward, finish = job.phases(refs[:n], refs[n:2 * n], *refs[2 * n:])
        start()
        forward()
        finish()

    hbm = pl.BlockSpec(memory_space=pl.ANY)
    return pl.pallas_call(body, name=name, out_shape=job.out_shape(), in_specs=[hbm] * n, out_specs=[hbm] * n,
                          scratch_shapes=job.scratch())(*job.arrays)


def ride_call(job, compute, *, name, grid, in_specs, out_specs, out_shape, ins, sem, scratch=()):
    scratch = list(scratch)
    if job is None:
        return pl.pallas_call(compute, name=name, grid=grid, in_specs=in_specs, out_specs=out_specs,
                              out_shape=out_shape, scratch_shapes=scratch, compiler_params=_cparams(sem))(*ins), None
    n, n_in, n_out, n_scr = job.n, len(ins), len(out_shape), len(scratch)
    n_steps = 1
    for g in grid:
        n_steps *= g

    def body(*refs):
        ins_, job_ins = refs[:n_in], refs[n_in:n_in + n]
        outs, job_outs = refs[n_in + n:n_in + n + n_out], refs[n_in + n + n_out:n_in + 2 * n + n_out]
        rest = refs[n_in + 2 * n + n_out:]
        start, forward, finish = job.phases(job_ins, job_outs, *rest[n_scr:])
        now = 0
        for axis, g in enumerate(grid):
            now = now * g + pl.program_id(axis)
        pl.when(now == 0)(start)
        pl.when(now == n_steps // 2)(forward)
        compute(*ins_, *outs, *rest[:n_scr])
        pl.when(now == n_steps - 1)(finish)

    hbm = pl.BlockSpec(memory_space=pl.ANY)
    res = pl.pallas_call(
        body, name=name, grid=grid, in_specs=list(in_specs) + [hbm] * n,
        out_specs=list(out_specs) + [hbm] * n, out_shape=list(out_shape) + job.out_shape(),
        scratch_shapes=scratch + job.scratch(), compiler_params=_cparams(("arbitrary",) * len(grid)),
    )(*ins, *job.arrays)
    return res[:n_out], res[n_out:]


def to_heads(t, n_heads):
    s_len = t.shape[0]
    return t.reshape(s_len, n_heads, -1).transpose(1, 0, 2)


def from_heads(t):
    return t.transpose(1, 0, 2).reshape(t.shape[1], -1)


def gathered_to_full(t, axis):
    shp = t.shape[1:]
    return jnp.moveaxis(t, 0, axis).reshape(shp[:axis] + (N_DEV * shp[axis],) + shp[axis + 1:])


def full_to_owner_major(g, axis):
    shp = g.shape
    t = jnp.moveaxis(g.reshape(shp[:axis] + (N_DEV, shp[axis] // N_DEV) + shp[axis + 1:]), axis, 0)
    return t.reshape(N_DEV, -1, t.shape[-1])


def _small_rows(shape):
    n = 1
    for s in shape:
        n *= s
    return -(-n // LANES)


def pack_small(arrs, shapes):
    pieces = []
    for n in SMALL:
        flat = arrs[n].reshape(-1)
        flat = jnp.pad(flat, (0, _small_rows(shapes[n]) * LANES - flat.shape[0]))
        pieces.append(flat.reshape(-1, LANES))
    flat = jnp.concatenate(pieces, axis=0)
    return jnp.pad(flat, ((0, -flat.shape[0] % SMALL_ROW_MULTIPLE), (0, 0)))


def unpack_small(flat, shapes):
    out, r = {}, 0
    for n in SMALL:
        rows = _small_rows(shapes[n])
        size = 1
        for s in shapes[n]:
            size *= s
        out[n] = flat[r:r + rows].reshape(-1)[:size].reshape(shapes[n])
        r += rows
    return out


ROW_TM = 256
XATT_TM = 1024
HEAD_TM = 1024
SG_TM = 8 * SG_CHUNK
SB_TILES = (512, 512)
SM_TILE = 1024


def _norm_fwd(x, g, name):
    return prow(f_rms, [x], params=[g.reshape(1, -1)], outs=[(x.shape[1], BF16, False)], tm=ROW_TM, name=name)[0]


def _norm_bwd(x, g, dh, add, name, want_row=True):
    res = prow_vjp(f_rms, [x], params=[g.reshape(1, -1)], cts=[dh], row_grad=[want_row],
                   adds=[add] if want_row else None, tm=ROW_TM, name=name)
    return (res[0], res[1].reshape(-1)) if want_row else (None, res[0].reshape(-1))


def _out_proj(a, w, x, next_gain, alpha, name):
    if next_gain is None:
        return pmm(a, w, res=x, alpha=alpha, name=name), None
    return pmm(a, w, res=x, alpha=alpha, norm_out=next_gain.reshape(1, -1), name=name)


def _in_proj_bwd(d, w, x, gain, dy, name, **kw):
    tk = _pick(w.shape[1])
    if tk < w.shape[1]:
        w, kw = w.reshape(w.shape[0], -1, tk).transpose(1, 0, 2), dict(kw, b_tiles=True)
    dx, g_gain = pmm(d, w, tb=True, norm_bwd=(x, gain.reshape(1, -1), dy), name=name, **kw)
    return dx, g_gain.reshape(-1)


def ffn_fwd(x, h, p, tag, next_gain, job=None, after_job=None):
    (gate, up, act), landed = ffn_gate_up(h, p['w_gu'], name=f"{tag}_gu", job=job)
    if job is not None:
        after_job(landed)
    out = _out_proj(act, p['w_down'], x, next_gain, 0.5, f"{tag}_down")
    return out, (x, h, gate, up, act)


def _no_rider(run, **own):
    return run(None)[0]


def _pmm_pair(*args, job, **kw):
    out = pmm(*args, job=job, **kw)
    return out if job is not None else (out, None)


def ffn_bwd(dy, p, saved, tag, with_job=_no_rider):
    x, h, gate, up, act = saved
    d_gate, d_up = with_job(lambda job: ffn_gate_up_bwd(dy, p['w_down'], gate, up, alpha=0.5, name=f"{tag}_dact",
                                                        job=job))
    g_down = pmm(act, dy, ta=True, out_dtype=GRAD_WIRE, alpha=0.5, name=f"{tag}_gdown")
    g_gate = with_job(lambda job: _pmm_pair(h, d_gate, ta=True, out_dtype=GRAD_WIRE, name=f"{tag}_ggate", job=job),
                      w_down=g_down)
    g_gu = jnp.concatenate([g_gate, pmm(h, d_up, ta=True, out_dtype=GRAD_WIRE, name=f"{tag}_gup")], axis=1)
    tk = _pick(d_gate.shape[1])
    w_gu_tiles = p['w_gu'].reshape(p['w_gu'].shape[0], -1, tk).transpose(1, 0, 2)
    dx, g_norm = with_job(
        lambda job: _pmm_pair(d_gate, w_gu_tiles, a2=d_up, tb=True, b_tiles=True,
                              norm_bwd=(x, p['norm'].reshape(1, -1), dy), name=f"{tag}_dh", job=job), w_gu=g_gu)
    return dx, {'norm': g_norm.reshape(-1), 'w_gu': g_gu, 'w_down': g_down}


def even_mixer_fwd(x, h, p, next_gain, job=None, after_job=None):
    qkv = pmm(h, p['w_in'][:, :3 * SB_WIDTH], out_dtype=BF16, name="sbg_in_qkv")
    z = pmm(h, p['w_in'][:, 3 * SB_WIDTH:], name="sbg_in_gate")
    q, k, v = (to_heads(qkv[:, i * SB_WIDTH:(i + 1) * SB_WIDTH], SB_HEADS) for i in range(3))
    (o_sb, tot), landed = sb_fwd(q, k, v, tq=SB_TILES[0], tk=SB_TILES[1], name="sb_fwd", job=job)
    if job is not None:
        after_job(landed)
    ln_g, ln_b = p['ln_gain'].reshape(1, -1), p['ln_bias'].reshape(1, -1)
    u, gn = prow(f_gate_prep, [z], params=[ln_g, ln_b], outs=[(SG_WIDTH, F32, False)] * 2, tm=ROW_TM,
                 name="sgu_prep")
    gn_p, u_p = ColGroups(gn, SG_PAIR), ColGroups(u, SG_PAIR)
    w_p = p['sgu_w'].reshape(SG_GROUPS // 2, 2 * SG_CHUNK, SG_CHUNK)
    b_p = p['sgu_b'].reshape(SG_GROUPS // 2, 2 * SG_CHUNK, 1)
    o_sg = prow(f_spatial_gate, [gn_p, u_p], gparams=[w_p, b_p], outs=[(SG_PAIR, F32, 'cols')], tm=SG_TM,
                name="sgu_mix")[0]
    cat = jnp.concatenate([from_heads(o_sb), o_sg], axis=-1).astype(BF16)
    out = _out_proj(cat, p['w_out'], x, next_gain, 1.0, "sbg_out")
    return out, (x, h, q, k, v, tot, z, gn_p, u_p, w_p, b_p, cat)


def even_mixer_bwd(dy, p, saved, job_of=None):
    x, h, q, k, v, tot, z, gn_p, u_p, w_p, b_p, cat = saved
    d_osb = to_heads(pmm(dy, p['w_out'][:SB_WIDTH], tb=True, name="sbg_dcat_sb"), SB_HEADS)
    d_osg = ColGroups(pmm(dy, p['w_out'][SB_WIDTH:], tb=True, name="sbg_dcat_sg"), SG_PAIR)
    g_out = pmm(cat, dy, ta=True, out_dtype=GRAD_WIRE, name="sbg_gout")
    d_gn, d_u, g_w, g_b = prow_vjp(f_spatial_gate, [gn_p, u_p], gparams=[w_p, b_p], cts=[d_osg],
                                   row_grad=[True, True], tm=SG_TM, name="sgu_dmix")
    ln_g, ln_b = p['ln_gain'].reshape(1, -1), p['ln_bias'].reshape(1, -1)
    d_z, g_lng, g_lnb = prow_vjp(f_gate_prep, [z], params=[ln_g, ln_b], cts=[d_u, d_gn], row_grad=[True],
                                 row_dtypes=[BF16], tm=ROW_TM, name="sgu_dprep")
    job = None if job_of is None else job_of({'w_out': g_out})
    (dq, dk, dv), landed = sb_bwd(q, k, v, tot, d_osb, tq=SB_TILES[0], tk=SB_TILES[1], name="sb_bwd", job=job)
    d_proj = jnp.concatenate([from_heads(dq).astype(BF16), from_heads(dk).astype(BF16), from_heads(dv).astype(BF16),
                              d_z], axis=-1)
    g_in = pmm(h, d_proj, ta=True, out_dtype=GRAD_WIRE, name="sbg_gin")
    dx, g_norm = _in_proj_bwd(d_proj, p['w_in'], x, p['norm'], dy, "sbg_dh")
    return dx, {'norm': g_norm, 'w_in': g_in, 'ln_gain': g_lng.reshape(-1), 'ln_bias': g_lnb.reshape(-1),
                'sgu_w': g_w.reshape(SG_GROUPS, SG_CHUNK, SG_CHUNK), 'sgu_b': g_b.reshape(SG_GROUPS, SG_CHUNK),
                'w_out': g_out}, landed


def mla_fwd(x, h, cos, sin, p, next_gain):
    lora = MLA_Q_LORA + MLA_KV_LORA
    c_q = pmm(h, p['w_in'][:, :MLA_Q_LORA], name="mla_in_q")
    c_kv = pmm(h, p['w_in'][:, MLA_Q_LORA:lora], name="mla_in_kv")
    k_r = pmm(h, p['w_in'][:, lora:], name="mla_in_rope")
    cqn = _norm_fwd(c_q, p['q_lora_gain'], "mla_qlora_norm")
    ckvn = _norm_fwd(c_kv, p['kv_lora_gain'], "mla_kvlora_norm")
    q_h = to_heads(pmm(cqn, p['w_uq'], name="mla_uq"), MLA_HEADS)
    kv_h = pmm(ckvn, p['w_ukv'], out_heads=MLA_NOPE + MLA_V, name="mla_ukv")
    q_g, k_g = p['q_gain'].reshape(1, -1), p['k_gain'].reshape(1, -1)
    kp, v = prow(f_mla_k, [kv_h, k_r, cos, sin], params=[k_g], outs=[(MLA_QK, BF16, True), (MLA_V, BF16, True)],
                 tm=HEAD_TM, name="mla_kprep")
    o, lse = sm_fwd(q_h, kp, v, tq=SM_TILE, tk=SM_TILE, name="mla_att_fwd", q_prep=(cos, sin, q_g))
    o_flat = from_heads(o).astype(BF16)
    out = _out_proj(o_flat, p['w_out'], x, next_gain, 1.0, "mla_out")
    return out, (x, h, c_q, c_kv, k_r, cqn, ckvn, q_h, kv_h, v, kp, o, lse, o_flat, q_g, k_g)


def mla_bwd(dy, cos, sin, p, saved, job_of=None):
    x, h, c_q, c_kv, k_r, cqn, ckvn, q_h, kv_h, v, kp, o, lse, o_flat, q_g, k_g = saved
    do = to_heads(pmm(dy, p['w_out'], tb=True, name="mla_do"), MLA_HEADS)
    g_out = pmm(o_flat, dy, ta=True, out_dtype=GRAD_WIRE, name="mla_gout")
    job = None if job_of is None else job_of({'w_out': g_out})
    (dq_h, dkp, dv, g_qg), landed = sm_bwd(q_h, kp, v, o, lse, do, tq=SM_TILE, tk=SM_TILE, name="mla_att_bwd",
                                           q_prep=(cos, sin, q_g), job=job)
    d_kv_h, dk_r, g_kg = prow_vjp(f_mla_k, [kv_h, k_r, cos, sin], params=[k_g], cts=[dkp, dv],
                                  row_grad=[True, True, False, False], row_dtypes=[BF16, F32], tm=HEAD_TM,
                                  name="mla_dkprep")
    d_q = from_heads(dq_h)
    d_kv = from_heads(d_kv_h)
    g_uq = pmm(cqn, d_q, ta=True, out_dtype=GRAD_WIRE, name="mla_guq")
    d_cqn = pmm(d_q, p['w_uq'], tb=True, name="mla_dcqn")
    g_ukv = pmm(ckvn, d_kv, ta=True, out_dtype=GRAD_WIRE, name="mla_gukv")
    d_ckvn = pmm(d_kv, p['w_ukv'], tb=True, name="mla_dckvn")
    d_cq, g_qlora = _norm_bwd(c_q, p['q_lora_gain'], d_cqn, None, "mla_dqlora_norm")
    d_ckv, g_kvlora = _norm_bwd(c_kv, p['kv_lora_gain'], d_ckvn, None, "mla_dkvlora_norm")
    d_proj = jnp.concatenate([d_cq, d_ckv, dk_r], axis=-1).astype(BF16)
    g_in = pmm(h, d_proj, ta=True, out_dtype=GRAD_WIRE, name="mla_gin")
    dx, g_norm = _in_proj_bwd(d_proj, p['w_in'], x, p['norm'], dy, "mla_dh")
    return dx, {'norm': g_norm, 'w_in': g_in, 'q_lora_gain': g_qlora, 'kv_lora_gain': g_kvlora, 'w_uq': g_uq,
                'w_ukv': g_ukv, 'q_gain': g_qg.reshape(-1), 'k_gain': g_kg.reshape(-1), 'w_out': g_out}, landed


def xattn_fwd(x, hq, mem, p, tag, next_gain):
    hm = _norm_fwd(mem, p['mem_norm'], f"{tag}_mem_norm")
    q_h = ColGroups(pmm(hq, p['wq'], name=f"{tag}_q"), MEM_HEAD_DIM)
    kv = pmm(hm, p['wkv'], name=f"{tag}_kv").reshape(mem.shape[0], MEM_HEADS, 2 * MEM_HEAD_DIM).transpose(1, 0, 2)
    k_h, v_h = kv[..., :MEM_HEAD_DIM], kv[..., MEM_HEAD_DIM:]
    q_g, k_g = p['q_gain'].reshape(1, -1), p['k_gain'].reshape(1, -1)
    o_flat = prow(f_xattn, [q_h], gparams=[k_h, v_h], params=[q_g, k_g], outs=[(MEM_HEAD_DIM, BF16, 'cols')],
                  tm=XATT_TM, name=f"{tag}_att")[0]
    out = _out_proj(o_flat, p['wo'], x, next_gain, 1.0, f"{tag}_out")
    return out, (x, mem, hq, hm, q_h, k_h, v_h, q_g, k_g, o_flat)


def xattn_bwd(dy, p, saved, tag):
    x, mem, hq, hm, q_h, k_h, v_h, q_g, k_g, o_flat = saved
    d_o = ColGroups(pmm(dy, p['wo'], tb=True, name=f"{tag}_do"), MEM_HEAD_DIM)
    g_wo = pmm(o_flat, dy, ta=True, out_dtype=GRAD_WIRE, name=f"{tag}_gwo")
    d_q, dk_h, dv_h, g_qg, g_kg = prow_vjp(f_xattn, [q_h], gparams=[k_h, v_h], params=[q_g, k_g], cts=[d_o],
                                           row_grad=[True], row_dtypes=[BF16], tm=XATT_TM, name=f"{tag}_datt")
    d_kv = jnp.concatenate([dk_h, dv_h], axis=-1).transpose(1, 0, 2).reshape(mem.shape[0], -1).astype(BF16)
    g_wq = pmm(hq, d_q, ta=True, out_dtype=GRAD_WIRE, name=f"{tag}_gwq")
    dx, g_norm = _in_proj_bwd(d_q, p['wq'], x, p['norm'], dy, f"{tag}_dhq")
    g_wkv = pmm(hm, d_kv, ta=True, out_dtype=GRAD_WIRE, name=f"{tag}_gwkv")
    dhm = pmm(d_kv, p['wkv'], tb=True, name=f"{tag}_dhm")
    _, g_mem_norm = _norm_bwd(mem, p['mem_norm'], dhm, None, f"{tag}_dmem_norm", want_row=False)
    return dx, {'norm': g_norm, 'mem_norm': g_mem_norm, 'wq': g_wq, 'wkv': g_wkv, 'q_gain': g_qg.reshape(-1),
                'k_gain': g_kg.reshape(-1), 'wo': g_wo}


def rope_tables(positions):
    half = MLA_ROPE // 2
    inv_freq = ROPE_THETA ** (-jnp.arange(half, dtype=F32) / half)
    ang = positions.astype(F32)[:, None] * inv_freq
    cos, sin = jnp.cos(ang), jnp.sin(ang)
    lead = jnp.ones((ang.shape[0], MLA_NOPE), F32)
    return jnp.concatenate([lead, cos, cos], axis=1), jnp.concatenate([0.0 * lead, sin, sin], axis=1)


FIRST_UNIT = ('ffn_pre_w_gu', 0)
EARLY_UNITS = [('ffn_pre_w_down', 0), ('sbg_w_in', 0)]


def local_step(x, mem, positions, target, w, shards):
    cos, sin = rope_tables(positions)
    full = {}

    def absorb(units, gathered):
        for (n, layer), t in zip(units, gathered):
            full[(n, layer)] = gathered_to_full(t, BIG[n] - 1)

    late_units = [u for u in shards if u != FIRST_UNIT and u not in EARLY_UNITS]
    (h,), gathered = prow(f_rms, [x], params=[w['ffn_pre_norm'][0].reshape(1, -1)], outs=[(x.shape[1], BF16, False)],
                          tm=ROW_TM, name="ffn_pre0_norm", job=CommJob('gather', [shards[FIRST_UNIT]]))
    absorb([FIRST_UNIT], gathered)
    first_ffn_p = {'norm': w['ffn_pre_norm'][0], 'w_gu': full[FIRST_UNIT]}

    def ffn_params(kind, layer):
        return {'norm': w[f'ffn_{kind}_norm'][layer], 'w_gu': full[(f'ffn_{kind}_w_gu', layer)],
                'w_down': full[(f'ffn_{kind}_w_down', layer)]}

    def xattn_params(layer):
        return {'norm': w['xmem_norm'][layer], 'mem_norm': w['xmem_mem_norm'][layer], 'wq': full[('xmem_wq', layer)],
                'wkv': full[('xmem_wkv', layer)], 'q_gain': w['xmem_q_gain'][layer], 'k_gain': w['xmem_k_gain'][layer],
                'wo': full[('xmem_wo', layer)]}

    even_p = {'norm': w['mix_norm'][0], 'ln_gain': w['sgu_ln_gain'][0], 'ln_bias': w['sgu_ln_bias'][0],
              'sgu_w': w['sgu_w'][0], 'sgu_b': w['sgu_b'][0]}

    def early_weights_landed(gathered):
        absorb(EARLY_UNITS, gathered)
        first_ffn_p['w_down'] = full[('ffn_pre_w_down', 0)]
        even_p['w_in'] = full[('sbg_w_in', 0)]

    def late_weights_landed(gathered):
        absorb(late_units, gathered)
        even_p['w_out'] = full[('sbg_w_out', 0)]

    def mla_params():
        return {'norm': w['mix_norm'][1], 'w_in': full[('mla_w_in', 0)], 'q_lora_gain': w['mla_q_lora_gain'][0],
                'kv_lora_gain': w['mla_kv_lora_gain'][0], 'w_uq': full[('mla_w_uq', 0)],
                'w_ukv': full[('mla_w_ukv', 0)], 'q_gain': w['mla_q_gain'][0], 'k_gain': w['mla_k_gain'][0],
                'w_out': full[('mla_w_out', 0)]}

    saved = []
    for layer in range(DEPTH):
        if layer == 0:
            (x, h), s_pre = ffn_fwd(x, h, first_ffn_p, "ffn_pre0", w['mix_norm'][0],
                                    job=CommJob('gather', [shards[u] for u in EARLY_UNITS]),
                                    after_job=early_weights_landed)
        else:
            (x, h), s_pre = ffn_fwd(x, h, ffn_params('pre', layer), f"ffn_pre{layer}", w['mix_norm'][layer])
        if layer % 2 == 0:
            (x, h), s_mix = even_mixer_fwd(x, h, even_p, w['xmem_norm'][layer],
                                           job=CommJob('gather', [shards[u] for u in late_units]),
                                           after_job=late_weights_landed)
        else:
            (x, h), s_mix = mla_fwd(x, h, cos, sin, mla_params(), w['xmem_norm'][layer])
        (x, h), s_x = xattn_fwd(x, h, mem, xattn_params(layer), f"xmem{layer}", w['ffn_post_norm'][layer])
        following = w['ffn_pre_norm'][layer + 1] if layer + 1 < DEPTH else None
        (x, h), s_post = ffn_fwd(x, h, ffn_params('post', layer), f"ffn_post{layer}", following)
        saved.append((s_pre, s_mix, s_x, s_post))

    dx, loss = loss_head(x, target, tm=ROW_TM, name="loss_head")

    ready, riding, landed = {}, [], {}

    def offer(name, layer, g):
        ready[(name, layer)] = full_to_owner_major(g, BIG[name] - 1)

    def ride(name):
        def job_of(own):
            offer(name, 0, own['w_out'])
            riding[:] = list(ready)
            return CommJob('exchange', [ready.pop(u) for u in riding])
        return job_of

    def last_rides(run, **own):
        for kind, g in own.items():
            offer('ffn_pre_' + kind, 0, g)
        units = list(ready)
        if not units:
            return run(None)[0]
        res, arrived = run(CommJob('exchange', [ready.pop(u) for u in units]))
        landed.update(zip(units, arrived))
        return res

    per_layer = []
    for layer in reversed(range(DEPTH)):
        s_pre, s_mix, s_x, s_post = saved[layer]
        dx, g_post = ffn_bwd(dx, ffn_params('post', layer), s_post, f"ffn_post{layer}")
        offer('ffn_post_w_gu', layer, g_post['w_gu'])
        offer('ffn_post_w_down', layer, g_post['w_down'])
        dx, g_x = xattn_bwd(dx, xattn_params(layer), s_x, f"xmem{layer}")
        for n in ('wq', 'wkv', 'wo'):
            offer('xmem_' + n, layer, g_x[n])
        if layer % 2 == 0:
            dx, g_mix, arrived = even_mixer_bwd(dx, even_p, s_mix, job_of=ride('sbg_w_out'))
            landed.update(zip(riding, arrived))
            offer('sbg_w_in', 0, g_mix['w_in'])
        else:
            dx, g_mix, arrived = mla_bwd(dx, cos, sin, mla_params(), s_mix, job_of=ride('mla_w_out'))
            landed.update(zip(riding, arrived))
            for n in ('w_in', 'w_uq', 'w_ukv'):
                offer('mla_' + n, 0, g_mix[n])
        if layer == 0:
            dx, g_pre = ffn_bwd(dx, ffn_params('pre', layer), s_pre, f"ffn_pre{layer}", with_job=last_rides)
        else:
            dx, g_pre = ffn_bwd(dx, ffn_params('pre', layer), s_pre, f"ffn_pre{layer}")
            offer('ffn_pre_w_gu', layer, g_pre['w_gu'])
            offer('ffn_pre_w_down', layer, g_pre['w_down'])
        per_layer.append((layer, g_pre, g_mix, g_x, g_post))
    per_layer.sort(key=lambda t: t[0])
    assert not ready

    def stack(pick):
        return jnp.stack([pick(t) for t in per_layer])

    g_even, g_mla = per_layer[0][2], per_layer[1][2]
    small_grads = {
        'ffn_pre_norm': stack(lambda t: t[1]['norm']), 'mix_norm': stack(lambda t: t[2]['norm']),
        'sgu_ln_gain': g_even['ln_gain'][None], 'sgu_ln_bias': g_even['ln_bias'][None],
        'sgu_w': g_even['sgu_w'][None], 'sgu_b': g_even['sgu_b'][None],
        'mla_q_lora_gain': g_mla['q_lora_gain'][None], 'mla_kv_lora_gain': g_mla['kv_lora_gain'][None],
        'mla_q_gain': g_mla['q_gain'][None], 'mla_k_gain': g_mla['k_gain'][None],
        'xmem_norm': stack(lambda t: t[3]['norm']), 'xmem_mem_norm': stack(lambda t: t[3]['mem_norm']),
        'xmem_q_gain': stack(lambda t: t[3]['q_gain']), 'xmem_k_gain': stack(lambda t: t[3]['k_gain']),
        'ffn_post_norm': stack(lambda t: t[4]['norm']),
    }
    return loss, dx, small_grads, landed


def _device_slot():
    x, y, c = _me()
    return 4 * x + 2 * y + c


def kernel(x, mem, positions, ffn_pre_norm, ffn_pre_w_gu, ffn_pre_w_down, mix_norm, sbg_w_in, sgu_ln_gain, sgu_ln_bias, sgu_w, sgu_b, sbg_w_out, mla_w_in, mla_q_lora_gain, mla_kv_lora_gain, mla_w_uq, mla_w_ukv, mla_q_gain, mla_k_gain, mla_w_out, xmem_norm, xmem_mem_norm, xmem_wq, xmem_wkv, xmem_q_gain, xmem_k_gain, xmem_wo, ffn_post_norm, ffn_post_w_gu, ffn_post_w_down, loss_target, m_ffn_pre_norm, m_ffn_pre_w_gu, m_ffn_pre_w_down, m_mix_norm, m_sbg_w_in, m_sgu_ln_gain, m_sgu_ln_bias, m_sgu_w, m_sgu_b, m_sbg_w_out, m_mla_w_in, m_mla_q_lora_gain, m_mla_kv_lora_gain, m_mla_w_uq, m_mla_w_ukv, m_mla_q_gain, m_mla_k_gain, m_mla_w_out, m_xmem_norm, m_xmem_mem_norm, m_xmem_wq, m_xmem_wkv, m_xmem_q_gain, m_xmem_k_gain, m_xmem_wo, m_ffn_post_norm, m_ffn_post_w_gu, m_ffn_post_w_down, v_ffn_pre_norm, v_ffn_pre_w_gu, v_ffn_pre_w_down, v_mix_norm, v_sbg_w_in, v_sgu_ln_gain, v_sgu_ln_bias, v_sgu_w, v_sgu_b, v_sbg_w_out, v_mla_w_in, v_mla_q_lora_gain, v_mla_kv_lora_gain, v_mla_w_uq, v_mla_w_ukv, v_mla_q_gain, v_mla_k_gain, v_mla_w_out, v_xmem_norm, v_xmem_mem_norm, v_xmem_wq, v_xmem_wkv, v_xmem_q_gain, v_xmem_k_gain, v_xmem_wo, v_ffn_post_norm, v_ffn_post_w_gu, v_ffn_post_w_down):
    args = locals()
    w_in = {n: args[n] for n in WEIGHTS}
    m_in = {n: args["m_" + n] for n in WEIGHTS}
    v_in = {n: args["v_" + n] for n in WEIGHTS}
    slot = _device_slot()

    tiny = jnp.zeros((8, LANES), F32)
    for i, src in enumerate((w_in, m_in, v_in)):
        tiny = tiny.at[i, :64].set(src['mla_q_lora_gain'][0]).at[i + 3, :32].set(src['mla_kv_lora_gain'][0])
    tiny_all = comm_call('gather', [tiny], name="gather_lora_gains")[0]
    full_small = []
    for i, src in enumerate((w_in, m_in, v_in)):
        d = {n: src[n] for n in SMALL}
        d['mla_q_lora_gain'] = tiny_all[:, i, :64].reshape(1, MLA_Q_LORA)
        d['mla_kv_lora_gain'] = tiny_all[:, i + 3, :32].reshape(1, MLA_KV_LORA)
        full_small.append(d)
    w_small, m_small, v_small = full_small
    small_shapes = {n: w_small[n].shape for n in SMALL}

    shards = {(n, layer): w_in[n][layer].astype(BF16) for n in BIG for layer in range(w_in[n].shape[0])}
    loss, dx, grads, landed = local_step(x[0], mem[0], positions[0], loss_target[0], w_small, shards)
    loss = lax.psum(loss, ("x", "y", "c"))
    big_out = {n: adamw([landed[(n, layer)] for layer in range(w_in[n].shape[0])], w_in[n], m_in[n], v_in[n],
                        name=f"adamw_{n}") for n in BIG}

    small_parts = comm_call('gather', [pack_small(grads, small_shapes)], name="gather_small_grads")
    small_out = adamw(small_parts, pack_small(w_small, small_shapes)[None], pack_small(m_small, small_shapes)[None],
                      pack_small(v_small, small_shapes)[None], name="adamw_small")
    small_out = [unpack_small(t[0], small_shapes) for t in small_out]
    for d in small_out:
        for n, width in zip(GAIN_SHARDED, (64, 32)):
            d[n] = lax.dynamic_slice(d[n], (0, slot * width), (1, width))

    outs = [loss, dx[None]]
    for kind, small_d in enumerate(small_out):
        outs += [big_out[n][kind] if n in BIG else small_d[n] for n in WEIGHTS]
    return tuple(outs)
```

```python
import jax
import jax.numpy as jnp
from jax import lax
from jax.experimental import pallas as pl
from jax.experimental.pallas import tpu as pltpu

F32 = jnp.float32
BF16 = jnp.bfloat16
MESH = pl.DeviceIdType.MESH
N_DEV = 8

VMEM_LIMIT_BYTES = 56 * 1024 * 1024
LANES = 128

D_MODEL = 1024
DEPTH = 2
EPS = 1e-6
SB_HEADS, SB_HEAD_DIM = 8, 64
SB_WIDTH = SB_HEADS * SB_HEAD_DIM
SG_GROUPS, SG_GROUP_DIM, SG_CHUNK = 8, 64, 128
SG_WIDTH = SG_GROUPS * SG_GROUP_DIM
MLA_HEADS, MLA_NOPE, MLA_ROPE, MLA_V = 16, 64, 32, 64
MLA_QK = MLA_NOPE + MLA_ROPE
MLA_Q_LORA, MLA_KV_LORA = 512, 256
ROPE_THETA = 10000.0
MEM_HEADS = 4
MEM_HEAD_DIM = D_MODEL // MEM_HEADS

ADAM_LR, ADAM_B1, ADAM_B2, ADAM_EPS, ADAM_WD, ADAM_STEP = 0.001, 0.9, 0.999, 1e-08, 0.01, 10

WEIGHTS = ['ffn_pre_norm', 'ffn_pre_w_gu', 'ffn_pre_w_down', 'mix_norm', 'sbg_w_in', 'sgu_ln_gain', 'sgu_ln_bias',
           'sgu_w', 'sgu_b', 'sbg_w_out', 'mla_w_in', 'mla_q_lora_gain', 'mla_kv_lora_gain', 'mla_w_uq', 'mla_w_ukv',
           'mla_q_gain', 'mla_k_gain', 'mla_w_out', 'xmem_norm', 'xmem_mem_norm', 'xmem_wq', 'xmem_wkv',
           'xmem_q_gain', 'xmem_k_gain', 'xmem_wo', 'ffn_post_norm', 'ffn_post_w_gu', 'ffn_post_w_down']
BIG = {'ffn_pre_w_gu': 2, 'ffn_pre_w_down': 1, 'sbg_w_in': 2, 'sbg_w_out': 1, 'mla_w_in': 1, 'mla_w_uq': 2,
       'mla_w_ukv': 2, 'mla_w_out': 1, 'xmem_wq': 1, 'xmem_wkv': 2, 'xmem_wo': 1, 'ffn_post_w_gu': 2,
       'ffn_post_w_down': 1}
GAIN_SHARDED = ('mla_q_lora_gain', 'mla_kv_lora_gain')
SMALL = [n for n in WEIGHTS if n not in BIG]
GRAD_WIRE = BF16
FFN_SAVE = BF16
SMALL_ROW_MULTIPLE = 16


def _cparams(sem=None):
    return pltpu.CompilerParams(dimension_semantics=sem, vmem_limit_bytes=VMEM_LIMIT_BYTES)


MM_TILE_CAP = 1408
HEAD_MAJOR_ROWS = 4096


def _pick(dim, cap=MM_TILE_CAP):
    if dim % LANES:
        return dim
    return max(t for t in range(LANES, min(dim, cap) + 1, LANES) if dim % t == 0)


def _rms(x, g):
    return x * lax.rsqrt(jnp.mean(x * x, axis=-1, keepdims=True) + EPS) * g


def pmm(a, b, *, a2=None, ta=False, tb=False, out_dtype=F32, res=None, alpha=1.0, norm_out=None, norm_bwd=None,
        out_heads=None, b_tiles=False, job=None, name):
    kdim, m = (a.shape if ta else a.shape[::-1])
    n = b.shape[1] if b_tiles else (b.shape[0] if tb else b.shape[1])
    tm, tn, tk = _pick(m), _pick(n), _pick(kdim)
    if b_tiles:
        assert tb and b.shape[2] == tk
    if out_heads is not None:
        tn, tm = out_heads, _pick(m, HEAD_MAJOR_ROWS)
    whole_rows = norm_out is not None or norm_bwd is not None
    if whole_rows:
        assert tn == n
    if norm_bwd is not None:
        tm = min(tm, 512)
    nk1 = kdim // tk
    nk = nk1 if a2 is None else 2 * nk1
    assert a2 is None or (a2.shape == a.shape and not ta)
    dims = (((0 if ta else 1,), (1 if tb else 0,)), ((), ()))
    n_lead = 2 if a2 is None else 3
    n_extra = (res is not None) + (norm_out is not None) + (0 if norm_bwd is None else 2 + (norm_bwd[2] is not None))

    def body(*refs):
        a_ref, b_ref = refs[:2]
        extra = list(refs[n_lead:n_lead + n_extra])
        outs, acc_ref = refs[n_lead + n_extra:-1], refs[-1]
        i, k = pl.program_id(0), pl.program_id(2)

        @pl.when(k == 0)
        def _():
            acc_ref[...] = jnp.zeros_like(acc_ref)

        def accumulate(lhs_ref):
            rhs = b_ref[k] if b_tiles else b_ref[...]
            acc_ref[...] += lax.dot_general(lhs_ref[...].astype(BF16), rhs.astype(BF16), dims,
                                            preferred_element_type=F32)

        if a2 is None:
            accumulate(a_ref)
        else:
            pl.when(k < nk1)(lambda: accumulate(a_ref))
            pl.when(k >= nk1)(lambda: accumulate(refs[2]))

        @pl.when(k == nk - 1)
        def _():
            r = acc_ref[...]
            if alpha != 1.0:
                r = r * alpha
            if res is not None:
                r = extra.pop(0)[...] + r
            if norm_bwd is None:
                outs[0][...] = r.astype(out_dtype)
            if norm_out is not None:
                outs[1][...] = _rms(r, extra.pop(0)[...]).astype(BF16)
            if norm_bwd is not None:
                x_ref, g_ref = extra.pop(0), extra.pop(0)
                _, pull = jax.vjp(_rms, x_ref[...], g_ref[...])
                dx, dg = pull(r)
                if norm_bwd[2] is not None:
                    dx = dx + extra.pop(0)[...]
                outs[0][...] = dx
                outs[2][...] = dx.astype(BF16)

                @pl.when(i == 0)
                def _():
                    outs[1][...] = dg

                @pl.when(i != 0)
                def _():
                    outs[1][...] += dg

    gi, gj = m // tm, n // tn
    a_bytes, b_bytes = a.size * a.dtype.itemsize, (n * kdim) * b.dtype.itemsize
    j_outer = not whole_rows and nk == 1 and gj * a_bytes + b_bytes < a_bytes + gi * b_bytes
    grid = (gj, gi, nk) if j_outer else (gi, gj, nk)

    def spec(block, index):
        return pl.BlockSpec(block, (lambda j, i, k: index(i, j, k)) if j_outer else index)

    a_spec = spec((tk, tm), lambda i, j, k: (k, i)) if ta else spec((tm, tk), lambda i, j, k: (i, k))
    b_spec = spec((tn, tk), lambda i, j, k: (j, k)) if tb else spec((tk, tn), lambda i, j, k: (k, j))
    if b_tiles:
        assert tn == n
        b_spec = spec(b.shape, lambda i, j, k: (0, 0, 0))
    o_spec = spec((tm, tn), lambda i, j, k: (i, j))
    g_spec = spec((1, tn), lambda i, j, k: (0, 0))
    ins, in_specs = [a, b], [a_spec, b_spec]
    if a2 is not None:
        in_specs[0] = spec((tm, tk), lambda i, j, k: (i, jnp.minimum(k, nk1 - 1)))
        ins.append(a2)
        in_specs.append(spec((tm, tk), lambda i, j, k: (i, jnp.maximum(k - nk1, 0))))
    if res is not None:
        ins.append(res)
        in_specs.append(o_spec)
    out_shape, out_specs = [jax.ShapeDtypeStruct((m, n), out_dtype)], [o_spec]
    if out_heads is not None:
        out_shape = [jax.ShapeDtypeStruct((n // tn, m, tn), out_dtype)]
        out_specs = [spec((None, tm, tn), lambda i, j, k: (j, i, 0))]
    if norm_out is not None:
        ins.append(norm_out)
        in_specs.append(g_spec)
        out_shape.append(jax.ShapeDtypeStruct((m, n), BF16))
        out_specs.append(o_spec)
    if norm_bwd is not None:
        ins += [t for t in norm_bwd if t is not None]
        in_specs += [o_spec, g_spec] + ([o_spec] if norm_bwd[2] is not None else [])
        out_shape = [jax.ShapeDtypeStruct((m, n), F32), jax.ShapeDtypeStruct((1, n), F32),
                     jax.ShapeDtypeStruct((m, n), BF16)]
        out_specs = [o_spec, g_spec, o_spec]
    result, landed = ride_call(
        job, body, name=name, grid=grid, in_specs=in_specs, out_specs=out_specs, out_shape=out_shape, ins=ins,
        scratch=[pltpu.VMEM((tm, tn), F32)],
        sem=("arbitrary" if norm_bwd is not None else "parallel", "parallel", "arbitrary"))
    result = result if whole_rows else result[0]
    return result if job is None else (result, landed)


def ffn_gate_up(h, w_gu, *, name, job=None):
    m, kdim = h.shape
    n = w_gu.shape[1] // 2
    tm, tn = min(_pick(m), 512), _pick(n)
    up_off = n // tn

    def body(a_ref, bg_ref, bu_ref, gate_ref, up_ref, act_ref):
        av = a_ref[...].astype(BF16)
        gate = _dg(av, bg_ref[...].astype(BF16), 1, 0)
        up = _dg(av, bu_ref[...].astype(BF16), 1, 0)
        gate_ref[...] = gate.astype(gate_ref.dtype)
        up_ref[...] = up.astype(up_ref.dtype)
        act_ref[...] = (jax.nn.silu(gate) * up).astype(BF16)

    o_spec = pl.BlockSpec((tm, tn), lambda j, i: (i, j))
    return ride_call(
        job, body, name=name, grid=(n // tn, m // tm),
        in_specs=[pl.BlockSpec((tm, kdim), lambda j, i: (i, 0)), pl.BlockSpec((kdim, tn), lambda j, i: (0, j)),
                  pl.BlockSpec((kdim, tn), lambda j, i: (0, j + up_off))],
        out_specs=[o_spec] * 3,
        out_shape=[jax.ShapeDtypeStruct((m, n), FFN_SAVE), jax.ShapeDtypeStruct((m, n), FFN_SAVE),
                   jax.ShapeDtypeStruct((m, n), BF16)],
        ins=[h, w_gu, w_gu], sem=("parallel", "parallel"))


def ffn_gate_up_bwd(dy, w_down, gate, up, *, alpha, name, job=None):
    m, kdim = dy.shape
    n = w_down.shape[0]
    tm, tn = min(_pick(m), 512), _pick(n)

    def body(a_ref, b_ref, gate_ref, up_ref, dgate_ref, dup_ref):
        d_act = _dg(a_ref[...].astype(BF16), b_ref[...].astype(BF16), 1, 1) * alpha
        _, pull = jax.vjp(lambda g, u: jax.nn.silu(g) * u, gate_ref[...].astype(F32), up_ref[...].astype(F32))
        d_gate, d_up = pull(d_act)
        dgate_ref[...] = d_gate.astype(BF16)
        dup_ref[...] = d_up.astype(BF16)

    o_spec = pl.BlockSpec((tm, tn), lambda j, i: (i, j))
    return ride_call(
        job, body, name=name, grid=(n // tn, m // tm),
        in_specs=[pl.BlockSpec((tm, kdim), lambda j, i: (i, 0)), pl.BlockSpec((tn, kdim), lambda j, i: (j, 0)),
                  o_spec, o_spec],
        out_specs=[o_spec] * 2, out_shape=[jax.ShapeDtypeStruct((m, n), BF16)] * 2,
        ins=[dy, w_down, gate, up], sem=("parallel", "parallel"))


def _dg(a, b, ca, cb):
    return lax.dot_general(a, b, (((ca,), (cb,)), ((), ())), preferred_element_type=F32)


@jax.custom_vjp
def bdot(a, b):
    return _dg(a.astype(BF16), b.astype(BF16), 1, 0)


def _bdot_fwd(a, b):
    ab, bb = a.astype(BF16), b.astype(BF16)
    return _dg(ab, bb, 1, 0), (ab, bb)


def _bdot_bwd(saved, g):
    ab, bb = saved
    gb = g.astype(BF16)
    return _dg(gb, bb, 1, 1), _dg(ab, gb, 0, 0)


bdot.defvjp(_bdot_fwd, _bdot_bwd)


@jax.custom_vjp
def bdot_nt(a, b):
    return _dg(a.astype(BF16), b.astype(BF16), 1, 1)


def _bdot_nt_fwd(a, b):
    ab, bb = a.astype(BF16), b.astype(BF16)
    return _dg(ab, bb, 1, 1), (ab, bb)


def _bdot_nt_bwd(saved, g):
    ab, bb = saved
    gb = g.astype(BF16)
    return _dg(gb, bb, 1, 0), _dg(gb, ab, 0, 0)


bdot_nt.defvjp(_bdot_nt_fwd, _bdot_nt_bwd)


class ColGroups:
    def __init__(self, arr, width):
        self.arr, self.width = arr, width
        self.shape, self.dtype, self.ndim = arr.shape, arr.dtype, 3


def _plain(a):
    return a.arr if isinstance(a, ColGroups) else a


def _row_spec(arr, tm):
    if isinstance(arr, ColGroups):
        return pl.BlockSpec((tm, arr.width), lambda r, g: (r, g))
    if arr.ndim == 3:
        return pl.BlockSpec((None, tm, arr.shape[2]), lambda r, g: (g, r, 0))
    return pl.BlockSpec((tm, arr.shape[1]), lambda r, g: (r, 0))


def _gparam_spec(arr):
    return pl.BlockSpec((None,) + arr.shape[1:], lambda r, g: (g, 0, 0))


def _whole_spec(arr):
    nd = arr.ndim
    return pl.BlockSpec(arr.shape, lambda r, g: (0,) * nd)


def _groups(rows, gparams):
    gs = {a.shape[1] // a.width if isinstance(a, ColGroups) else a.shape[0] for a in rows if a.ndim == 3}
    gs |= {a.shape[0] for a in gparams}
    assert len(gs) <= 1
    return gs.pop() if gs else 1


def prow(fn, rows, gparams=(), params=(), *, outs, tm, name, job=None):
    rows, gparams, params = list(rows), list(gparams), list(params)
    n_groups = _groups(rows, gparams)
    n_rows = rows[0].shape[-2]
    n_in = len(rows) + len(gparams) + len(params)

    def body(*refs):
        vals = [r[...] for r in refs[:n_in]]
        res = fn(*vals)
        for o_ref, r in zip(refs[n_in:], res, strict=True):
            o_ref[...] = r.astype(o_ref.dtype)

    out_shape, out_specs = [], []
    for width, dtype, grouped in outs:
        if grouped == 'cols':
            out_shape.append(jax.ShapeDtypeStruct((n_rows, n_groups * width), dtype))
            out_specs.append(_row_spec(ColGroups(out_shape[-1], width), tm))
            continue
        shp = (n_groups, n_rows, width) if grouped else (n_rows, width)
        out_shape.append(jax.ShapeDtypeStruct(shp, dtype))
        out_specs.append(_row_spec(out_shape[-1], tm))
    result, landed = ride_call(
        job, body, name=name, grid=(n_rows // tm, n_groups),
        in_specs=[_row_spec(a, tm) for a in rows] + [_gparam_spec(a) for a in gparams] + [_whole_spec(a) for a in params],
        out_specs=out_specs, out_shape=out_shape, ins=[*[_plain(a) for a in rows], *gparams, *params],
        sem=("parallel", "arbitrary"))
    return result if job is None else (result, landed)


def prow_vjp(fn, rows, gparams=(), params=(), *, cts, row_grad, adds=None, row_dtypes=None, gparam_grad=None,
             param_grad=None, tm, name):
    rows, gparams, params, cts = list(rows), list(gparams), list(params), list(cts)
    gparam_grad = list(gparam_grad) if gparam_grad is not None else [True] * len(gparams)
    param_grad = list(param_grad) if param_grad is not None else [True] * len(params)
    n_groups = _groups(rows + cts, gparams)
    n_rows = rows[0].shape[-2]
    want_rows = [i for i, w in enumerate(row_grad) if w]
    adds = list(adds) if adds is not None else [None] * len(want_rows)
    row_dtypes = list(row_dtypes) if row_dtypes is not None else [F32] * len(want_rows)
    add_arrays = [a for a in adds if a is not None]
    n_r, n_g, n_p, n_c, n_a = len(rows), len(gparams), len(params), len(cts), len(add_arrays)
    mask = list(row_grad) + gparam_grad + param_grad

    def body(*refs):
        r_id, g_id = pl.program_id(0), pl.program_id(1)
        n_in = n_r + n_g + n_p
        vals = [r[...] for r in refs[:n_in]]
        ct_vals = tuple(r[...].astype(F32) for r in refs[n_in:n_in + n_c])
        add_refs = list(refs[n_in + n_c:n_in + n_c + n_a])
        out_refs = list(refs[n_in + n_c + n_a:])
        diff_idx = [i for i, w in enumerate(mask) if w]

        def wrapped(*diff):
            full = list(vals)
            for i, d in zip(diff_idx, diff):
                full[i] = d
            return tuple(fn(*full))

        _, pull = jax.vjp(wrapped, *[vals[i].astype(F32) for i in diff_idx])
        grads = dict(zip(diff_idx, pull(ct_vals)))
        k = 0
        for j, i in enumerate(want_rows):
            o_ref = out_refs[k]
            k += 1
            gval = grads[i]
            if adds[j] is not None:
                gval = gval + add_refs.pop(0)[...].astype(F32)
            if rows[i].ndim == 2 and n_groups > 1:
                @pl.when(g_id == 0)
                def _(o_ref=o_ref, gval=gval):
                    o_ref[...] = gval.astype(o_ref.dtype)

                @pl.when(g_id != 0)
                def _(o_ref=o_ref, gval=gval):
                    o_ref[...] += gval.astype(o_ref.dtype)
            else:
                o_ref[...] = gval.astype(o_ref.dtype)
        for i in range(n_g):
            if not gparam_grad[i]:
                continue
            o_ref = out_refs[k]
            k += 1
            gval = grads[n_r + i]

            @pl.when(r_id == 0)
            def _(o_ref=o_ref, gval=gval):
                o_ref[g_id] = gval

            @pl.when(r_id != 0)
            def _(o_ref=o_ref, gval=gval):
                o_ref[g_id] += gval
        for i in range(n_p):
            if not param_grad[i]:
                continue
            o_ref = out_refs[k]
            k += 1
            gval = grads[n_r + n_g + i]
            first = jnp.logical_and(r_id == 0, g_id == 0)

            @pl.when(first)
            def _(o_ref=o_ref, gval=gval):
                o_ref[...] = gval

            @pl.when(jnp.logical_not(first))
            def _(o_ref=o_ref, gval=gval):
                o_ref[...] += gval

    out_shape, out_specs = [], []
    for j, i in enumerate(want_rows):
        out_shape.append(jax.ShapeDtypeStruct(rows[i].shape, row_dtypes[j]))
        out_specs.append(_row_spec(rows[i], tm))
    for i in range(n_g):
        if gparam_grad[i]:
            out_shape.append(jax.ShapeDtypeStruct(gparams[i].shape, F32))
            out_specs.append(_whole_spec(gparams[i]))
    for i in range(n_p):
        if param_grad[i]:
            out_shape.append(jax.ShapeDtypeStruct(params[i].shape, F32))
            out_specs.append(_whole_spec(params[i]))
    return pl.pallas_call(
        body, name=name, grid=(n_rows // tm, n_groups),
        in_specs=([_row_spec(a, tm) for a in rows] + [_gparam_spec(a) for a in gparams]
                  + [_whole_spec(a) for a in params] + [_row_spec(a, tm) for a in cts]
                  + [_row_spec(a, tm) for a in add_arrays]),
        out_specs=out_specs, out_shape=out_shape,
        compiler_params=_cparams(("arbitrary", "arbitrary")),
    )(*[_plain(a) for a in rows], *gparams, *params, *[_plain(a) for a in cts], *add_arrays)


def f_rms(x, g):
    return (_rms(x.astype(F32), g),)


def f_gate_prep(z, ln_g, ln_b):
    act = jax.nn.gelu(z)
    u, gg = act[:, :SG_WIDTH], act[:, SG_WIDTH:]
    mu = jnp.mean(gg, axis=-1, keepdims=True)
    var = jnp.mean(jnp.square(gg - mu), axis=-1, keepdims=True)
    return u, (gg - mu) * lax.rsqrt(var + EPS) * ln_g + ln_b


SG_PAIR = 2 * SG_GROUP_DIM


def f_spatial_gate(gn, u, w, b):
    t = lax.broadcasted_iota(jnp.int32, (SG_CHUNK, SG_CHUNK), 0)
    s = lax.broadcasted_iota(jnp.int32, (SG_CHUNK, SG_CHUNK), 1)
    mixed = None
    for half in range(2):
        first = half * SG_GROUP_DIM
        take = _lane_map(SG_PAIR, SG_GROUP_DIM, lambda src, dst: jnp.where(src == dst + first, 1.0, 0.0))
        put = _lane_map(SG_GROUP_DIM, SG_PAIR, lambda src, dst: jnp.where(dst == src + first, 1.0, 0.0))
        mine = slice(half * SG_CHUNK, (half + 1) * SG_CHUNK)
        w_causal = jnp.where(s <= t, w[mine], 0.0)
        group = place(gn, take)
        m = [bdot(w_causal, group[i:i + SG_CHUNK]) + b[mine] for i in range(0, group.shape[0], SG_CHUNK)]
        m = place(m[0] if len(m) == 1 else jnp.concatenate(m, axis=0), put)
        mixed = m if mixed is None else mixed + m
    return (u * mixed,)


def _two_pieces(x):
    hi = x.astype(BF16)
    return hi, (x - hi.astype(F32)).astype(BF16)


@jax.custom_vjp
def place(x, m):
    hi, lo = _two_pieces(x)
    return _dg(hi, m, 1, 0) + _dg(lo, m, 1, 0)


def _place_fwd(x, m):
    return place(x, m), m


def _place_bwd(m, g):
    hi, lo = _two_pieces(g)
    return _dg(hi, m, 1, 1) + _dg(lo, m, 1, 1), jnp.zeros_like(m)


place.defvjp(_place_fwd, _place_bwd)


def _lane_map(rows, cols, entry):
    src = lax.broadcasted_iota(jnp.int32, (rows, cols), 0)
    dst = lax.broadcasted_iota(jnp.int32, (rows, cols), 1)
    return entry(src, dst).astype(BF16)


def _rope_tail(t, cos_w, sin_w):
    half = MLA_ROPE // 2
    lo_half = lambda d: jnp.logical_and(d >= MLA_NOPE, d < MLA_NOPE + half)
    swap = _lane_map(MLA_QK, MLA_QK, lambda s, d: jnp.where(
        jnp.logical_and(d >= MLA_NOPE + half, s == d - half), 1.0,
        jnp.where(jnp.logical_and(lo_half(d), s == d + half), -1.0, 0.0)))
    return t * cos_w + place(t, swap) * sin_w


def f_mla_q(q, cos_w, sin_w, g):
    return (_rope_tail(f_rms(q, g)[0], cos_w, sin_w),)


def f_mla_k(kv, k_r, cos_w, sin_w, g):
    width = MLA_NOPE + MLA_V
    nope = _lane_map(width, MLA_QK, lambda s, d: jnp.where(jnp.logical_and(s == d, d < MLA_NOPE), 1.0, 0.0))
    tail = _lane_map(MLA_ROPE, MLA_QK, lambda s, d: jnp.where(s + MLA_NOPE == d, 1.0, 0.0))
    value = _lane_map(width, MLA_V, lambda s, d: jnp.where(s == d + MLA_NOPE, 1.0, 0.0))
    key = _rope_tail(f_rms(place(kv, nope) + place(k_r, tail), g)[0], cos_w, sin_w)
    return key, place(kv, value)


def f_xattn(q, k, v, q_g, k_g):
    qn, kn = f_rms(q, q_g)[0], f_rms(k, k_g)[0]
    sc = bdot_nt(qn, kn) * (MEM_HEAD_DIM ** -0.5)
    return (bdot(jax.nn.softmax(sc, axis=-1), v),)


def _split_dot(x, tri, pieces=2):
    hi = x.astype(BF16)
    if pieces == 1:
        return _dg(hi, tri, 1, 0)
    lo = (x - hi.astype(F32)).astype(BF16)
    return _dg(hi, tri, 1, 0) + _dg(lo, tri, 1, 0)


def _tri(tk, cmp):
    j = lax.broadcasted_iota(jnp.int32, (tk, tk), 0)
    s = lax.broadcasted_iota(jnp.int32, (tk, tk), 1)
    return cmp(j, s).astype(BF16)


SCAN_CHUNK = 256


def _row_scan(x, tri, reverse, pieces=2):
    n = x.shape[1] // SCAN_CHUNK
    chunks = [x[:, i * SCAN_CHUNK:(i + 1) * SCAN_CHUNK] for i in range(n)]
    out, seen = [None] * n, None
    for i in (reversed(range(n)) if reverse else range(n)):
        local = _split_dot(chunks[i], tri, pieces)
        out[i] = local if seen is None else local + seen
        total = jnp.sum(chunks[i], axis=1, keepdims=True)
        seen = total if seen is None else seen + total
    return (out[0] if n == 1 else jnp.concatenate(out, axis=1)), seen


def _att_specs(s_len, tq, dq, dv):
    q_spec = pl.BlockSpec((None, tq, dq), lambda h, i: (h, i, 0))
    k_spec = pl.BlockSpec((None, s_len, dq), lambda h, i: (h, 0, 0))
    v_spec = pl.BlockSpec((None, s_len, dv), lambda h, i: (h, 0, 0))
    o_spec = pl.BlockSpec((None, tq, dv), lambda h, i: (h, i, 0))
    r_spec = pl.BlockSpec((None, tq, 1), lambda h, i: (h, i, 0))
    return q_spec, k_spec, v_spec, o_spec, r_spec


def _key_blocks(qi, tq, tk):
    return (qi * tq) // tk, ((qi + 1) * tq + tk - 1) // tk


def _earlier(qi, j, tq, tk):
    row = qi * tq + lax.broadcasted_iota(jnp.int32, (tq, tk), 0)
    col = j * tk + lax.broadcasted_iota(jnp.int32, (tq, tk), 1)
    return col < row


LOG2_E = 1.4426950408889634


def _log2_sigmoid(z2):
    return jnp.minimum(z2, 0.0) - jnp.log2(1.0 + jnp.exp2(-jnp.abs(z2)))


def sb_fwd(q, k, v, *, tq, tk, name, job=None):
    n_heads, s_len, d = q.shape
    scale2 = SB_HEAD_DIM ** -0.5 * LOG2_E

    def body(q_ref, k_ref, v_ref, o_ref, tot_ref):
        qi = pl.program_id(1)
        qv = q_ref[...]
        upper = _tri(SCAN_CHUNK, lambda j, s: j > s)
        n_full, n_all = _key_blocks(qi, tq, tk)

        def make_step(masked, last):
            def step(jj, carry):
                acc, rest = carry
                j = last - 1 - jj
                sl = pl.ds(pl.multiple_of(j * tk, tk), tk)
                ks, vs = k_ref[sl, :], v_ref[sl, :]
                z2 = _dg(qv, ks, 1, 1) * scale2
                log_beta = _log2_sigmoid(z2)
                log_stay = log_beta - z2
                if masked:
                    valid = _earlier(qi, j, tq, tk)
                    log_stay = jnp.where(valid, log_stay, 0.0)
                after, total = _row_scan(log_stay, upper, True)
                w = jnp.exp2(log_beta + after + rest)
                if masked:
                    w = jnp.where(valid, w, 0.0)
                acc = acc + _dg(w.astype(BF16), vs, 1, 0)
                return acc, rest + total
            return step

        carry = (jnp.zeros((tq, d), F32), jnp.zeros((tq, 1), F32))
        carry = lax.fori_loop(0, n_all - n_full, make_step(True, n_all), carry)
        acc, rest = lax.fori_loop(0, n_full, make_step(False, n_full), carry)
        o_ref[...] = acc
        tot_ref[...] = rest

    q_spec, k_spec, v_spec, o_spec, r_spec = _att_specs(s_len, tq, d, d)
    return ride_call(
        job, body, name=name, grid=(n_heads, s_len // tq), in_specs=[q_spec, k_spec, v_spec],
        out_specs=[o_spec, r_spec],
        out_shape=[jax.ShapeDtypeStruct((n_heads, s_len, d), F32), jax.ShapeDtypeStruct((n_heads, s_len, 1), F32)],
        ins=[q, k, v], sem=("parallel", "arbitrary"))


def sb_bwd(q, k, v, tot, do, *, tq, tk, name, job=None):
    n_heads, s_len, d = q.shape
    scale = SB_HEAD_DIM ** -0.5
    scale2 = scale * LOG2_E

    def body(q_ref, k_ref, v_ref, tot_ref, do_ref, dq_ref, dk_ref, dv_ref):
        qi = pl.program_id(1)

        @pl.when(qi == 0)
        def _():
            dk_ref[...] = jnp.zeros_like(dk_ref)
            dv_ref[...] = jnp.zeros_like(dv_ref)

        qv = q_ref[...]
        dob = do_ref[...].astype(BF16)
        total = tot_ref[...]
        incl = _tri(SCAN_CHUNK, lambda j, s: j <= s)
        excl = _tri(SCAN_CHUNK, lambda j, s: j < s)
        n_full, n_all = _key_blocks(qi, tq, tk)

        def make_step(masked):
            def step(j, carry):
                dq, stay_before, dl_before = carry
                sl = pl.ds(pl.multiple_of(j * tk, tk), tk)
                ks, vs = k_ref[sl, :], v_ref[sl, :]
                z2 = _dg(qv, ks, 1, 1) * scale2
                log_beta = _log2_sigmoid(z2)
                log_stay = log_beta - z2
                if masked:
                    valid = _earlier(qi, j, tq, tk)
                    log_stay = jnp.where(valid, log_stay, 0.0)
                stay_upto, stay_sum = _row_scan(log_stay, incl, False)
                w = jnp.exp2(log_beta + (total - stay_before) - stay_upto)
                if masked:
                    w = jnp.where(valid, w, 0.0)
                dl = _dg(dob, vs, 1, 1) * w
                dl_upto, dl_sum = _row_scan(dl, excl, False, pieces=1)
                dl_prefix = dl_upto + dl_before
                beta = jnp.exp2(log_beta)
                dz = dl * (1.0 - beta) - beta * dl_prefix
                if masked:
                    dz = jnp.where(valid, dz, 0.0)
                dzb = dz.astype(BF16)
                dq = dq + _dg(dzb, ks, 1, 0)
                dk_ref[sl, :] += _dg(dzb, qv, 0, 0) * scale
                dv_ref[sl, :] += _dg(w.astype(BF16), dob, 0, 0)
                return dq, stay_before + stay_sum, dl_before + dl_sum
            return step

        zero = jnp.zeros((tq, 1), F32)
        carry = lax.fori_loop(0, n_full, make_step(False), (jnp.zeros((tq, d), F32), zero, zero))
        dq, _, _ = lax.fori_loop(n_full, n_all, make_step(True), carry)
        dq_ref[...] = dq * scale

    q_spec, k_spec, v_spec, o_spec, r_spec = _att_specs(s_len, tq, d, d)
    shp = jax.ShapeDtypeStruct((n_heads, s_len, d), F32)
    return ride_call(
        job, body, name=name, grid=(n_heads, s_len // tq), in_specs=[q_spec, k_spec, v_spec, r_spec, o_spec],
        out_specs=[q_spec, k_spec, v_spec], out_shape=[shp, shp, shp], ins=[q, k, v, tot, do],
        sem=("arbitrary", "arbitrary"))


NEG_BIG = -1e30


def _lower_left(rows, cols):
    r = lax.broadcasted_iota(jnp.int32, (rows, cols), 0)
    c = lax.broadcasted_iota(jnp.int32, (rows, cols), 1)
    return c <= r


def _prep_specs(tq, q_prep):
    cos, _, gain = q_prep
    rope_spec = pl.BlockSpec((tq, cos.shape[1]), lambda h, i: (i, 0))
    return [rope_spec, rope_spec, pl.BlockSpec(gain.shape, lambda h, i: (0, 0))]


def sm_fwd(q, k, v, *, tq, tk, name, q_prep=None):
    n_heads, s_len, dq = q.shape
    dv = v.shape[2]
    scale = dq ** -0.5
    assert tq == tk
    half = tk // 2
    n_prep = 0 if q_prep is None else 3

    def body(*refs):
        q_ref, prep_refs = refs[0], refs[1:1 + n_prep]
        k_ref, v_ref, o_ref, lse_ref = refs[1 + n_prep:]
        qi = pl.program_id(1)
        qv = q_ref[...]
        if q_prep is not None:
            qv = f_mla_q(qv, *[r[...] for r in prep_refs])[0].astype(BF16)

        def attend(carry, q_rows, keys, keep):
            acc, m, l = carry
            sc = _dg(q_rows, k_ref[keys, :], 1, 1) * scale
            if keep is not None:
                sc = jnp.where(keep, sc, NEG_BIG)
            m_new = jnp.maximum(m, jnp.max(sc, axis=1, keepdims=True))
            p = jnp.exp(sc - m_new)
            fade = jnp.exp(m - m_new)
            return (fade * acc + _dg(p.astype(BF16), v_ref[keys, :], 1, 0), m_new,
                    fade * l + jnp.sum(p, axis=1, keepdims=True))

        carry = (jnp.zeros((tq, dv), F32), jnp.full((tq, 1), NEG_BIG, F32), jnp.zeros((tq, 1), F32))
        carry = lax.fori_loop(
            0, qi, lambda j, c: attend(c, qv, pl.ds(pl.multiple_of(j * tk, tk), tk), None), carry)
        base = pl.multiple_of(qi * tk, tk)
        carry = attend(carry, qv, pl.ds(base, half), _lower_left(tq, half))
        low = attend(tuple(t[half:] for t in carry), qv[half:], pl.ds(pl.multiple_of(base + half, half), half),
                     _lower_left(half, half))
        acc, m, l = (jnp.concatenate([t[:half], u], axis=0) for t, u in zip(carry, low))
        o_ref[...] = acc / l
        lse_ref[...] = m + jnp.log(l)

    q_spec, k_spec, v_spec, o_spec, r_spec = _att_specs(s_len, tq, dq, dv)
    prep = [] if q_prep is None else list(q_prep)
    return pl.pallas_call(
        body, name=name, grid=(n_heads, s_len // tq),
        in_specs=[q_spec] + ([] if q_prep is None else _prep_specs(tq, q_prep)) + [k_spec, v_spec],
        out_specs=[o_spec, r_spec],
        out_shape=[jax.ShapeDtypeStruct((n_heads, s_len, dv), F32), jax.ShapeDtypeStruct((n_heads, s_len, 1), F32)],
        compiler_params=_cparams(("parallel", "arbitrary")),
    )(q, *prep, k, v)


def sm_bwd(q, k, v, o, lse, do, *, tq, tk, name, q_prep=None, job=None):
    n_heads, s_len, dq = q.shape
    dv = v.shape[2]
    scale = dq ** -0.5
    assert tq == tk
    half = tk // 2
    n_prep = 0 if q_prep is None else 3

    def body(*refs):
        q_ref, prep_refs = refs[0], refs[1:1 + n_prep]
        k_ref, v_ref, o_ref, lse_ref, do_ref, dq_ref, dk_ref, dv_ref = refs[1 + n_prep:9 + n_prep]
        head, qi = pl.program_id(0), pl.program_id(1)

        @pl.when(qi == 0)
        def _():
            dk_ref[...] = jnp.zeros_like(dk_ref)
            dv_ref[...] = jnp.zeros_like(dv_ref)

        q_raw = q_ref[...]
        prep_vals = [r[...] for r in prep_refs]
        qv = q_raw if q_prep is None else f_mla_q(q_raw, *prep_vals)[0].astype(BF16)
        do = do_ref[...]
        dob = do.astype(BF16)
        delta = jnp.sum(do * o_ref[...], axis=1, keepdims=True)
        lse_v = lse_ref[...]

        def attend(rows, keys, keep):
            ks, vs = k_ref[keys, :], v_ref[keys, :]
            p = jnp.exp(_dg(qv[rows], ks, 1, 1) * scale - lse_v[rows])
            if keep is not None:
                p = jnp.where(keep, p, 0.0)
            dv_ref[keys, :] += _dg(p.astype(BF16), dob[rows], 0, 0)
            ds = (p * (_dg(dob[rows], vs, 1, 1) - delta[rows]) * scale).astype(BF16)
            dk_ref[keys, :] += _dg(ds, qv[rows], 0, 0)
            return _dg(ds, ks, 1, 0)

        everything = slice(None)
        dq_acc = lax.fori_loop(
            0, qi, lambda j, acc: acc + attend(everything, pl.ds(pl.multiple_of(j * tk, tk), tk), None),
            jnp.zeros((tq, dq), F32))
        base = pl.multiple_of(qi * tk, tk)
        dq_acc = dq_acc + attend(everything, pl.ds(base, half), _lower_left(tq, half))
        low = attend(slice(half, None), pl.ds(pl.multiple_of(base + half, half), half), _lower_left(half, half))
        dq_acc = jnp.concatenate([dq_acc[:half], dq_acc[half:] + low], axis=0)
        if q_prep is None:
            dq_ref[...] = dq_acc
        else:
            cos, sin, gain = prep_vals
            _, pull = jax.vjp(lambda t, g: f_mla_q(t, cos, sin, g)[0], q_raw, gain)
            dq_raw, d_gain = pull(dq_acc)
            dq_ref[...] = dq_raw.astype(dq_ref.dtype)
            dgain_ref = refs[9 + n_prep]
            first = jnp.logical_and(head == 0, qi == 0)

            @pl.when(first)
            def _():
                dgain_ref[...] = d_gain

            @pl.when(jnp.logical_not(first))
            def _():
                dgain_ref[...] += d_gain

    q_spec, k_spec, v_spec, o_spec, r_spec = _att_specs(s_len, tq, dq, dv)
    out_specs = [q_spec, k_spec, v_spec]
    out_shape = [jax.ShapeDtypeStruct((n_heads, s_len, dq), F32 if q_prep is None else BF16),
                 jax.ShapeDtypeStruct((n_heads, s_len, dq), F32), jax.ShapeDtypeStruct((n_heads, s_len, dv), F32)]
    prep, prep_specs = [], []
    if q_prep is not None:
        prep, prep_specs = list(q_prep), _prep_specs(tq, q_prep)
        out_specs.append(prep_specs[2])
        out_shape.append(jax.ShapeDtypeStruct(q_prep[2].shape, F32))
    return ride_call(
        job, body, name=name, grid=(n_heads, s_len // tq),
        in_specs=[q_spec] + prep_specs + [k_spec, v_spec, o_spec, r_spec, o_spec], out_specs=out_specs,
        out_shape=out_shape, ins=[q, *prep, k, v, o, lse, do], sem=("arbitrary", "arbitrary"))


def loss_head(y, target, *, tm, name):
    n_rows, width = y.shape

    def body(y_ref, t_ref, dy_ref, dy_lo_ref, loss_ref):
        diff = y_ref[...] - t_ref[...]
        dy_ref[...] = diff / width
        dy_lo_ref[...] = (diff / width).astype(BF16)
        part = 0.5 * jnp.sum(jnp.mean(diff * diff, axis=-1, keepdims=True), axis=0, keepdims=True)

        @pl.when(pl.program_id(0) == 0)
        def _():
            loss_ref[...] = jnp.zeros_like(loss_ref)

        loss_ref[...] += jnp.broadcast_to(part, loss_ref.shape)

    spec = pl.BlockSpec((tm, width), lambda r: (r, 0))
    dy, dy_lo, loss = pl.pallas_call(
        body, name=name, grid=(n_rows // tm,), in_specs=[spec, spec],
        out_specs=[spec, spec, pl.BlockSpec((8, LANES), lambda r: (0, 0))],
        out_shape=[jax.ShapeDtypeStruct(y.shape, F32), jax.ShapeDtypeStruct(y.shape, BF16),
                   jax.ShapeDtypeStruct((8, LANES), F32)],
        compiler_params=_cparams(("arbitrary",)),
    )(y, target)
    return (dy, dy_lo), loss[0, 0]


ADAM_TILE_ELEMS = 256 * 1024


def _adam_rows(n_rows, width):
    fits = [t for t in range(16, n_rows + 1, 16) if n_rows % t == 0 and t * width <= ADAM_TILE_ELEMS]
    return max(fits) if fits else n_rows


def adamw(parts, w, m, v, *, name):
    n_layers, n_rows, width = w.shape
    assert len(parts) == n_layers
    tm = _adam_rows(n_rows, width)
    n_tiles = n_rows // tm

    def body(*refs):
        p_refs = refs[:n_layers]
        w_ref, m_ref, v_ref, g_ref, d_ref, nm_ref, nv_ref = refs[n_layers:]
        layer = pl.program_id(0)
        for this, p_ref in enumerate(p_refs):
            @pl.when(layer == this)
            def _(p_ref=p_ref):
                g = p_ref[0].astype(F32)
                for i in range(1, N_DEV):
                    g = g + p_ref[i].astype(F32)
                m_new = ADAM_B1 * m_ref[...] + (1.0 - ADAM_B1) * g
                v_new = ADAM_B2 * v_ref[...] + (1.0 - ADAM_B2) * jnp.square(g)
                m_hat = m_new / (1.0 - ADAM_B1 ** ADAM_STEP)
                v_hat = v_new / (1.0 - ADAM_B2 ** ADAM_STEP)
                g_ref[...] = g
                d_ref[...] = -ADAM_LR * (m_hat / (jnp.sqrt(v_hat) + ADAM_EPS) + ADAM_WD * w_ref[...])
                nm_ref[...] = m_new
                nv_ref[...] = v_new

    def part_spec(this):
        def index(layer, r):
            return 0, jnp.where(layer == this, r, jnp.where(layer < this, 0, n_tiles - 1)), 0
        return pl.BlockSpec((N_DEV, tm, width), index)

    spec = pl.BlockSpec((None, tm, width), lambda layer, r: (layer, r, 0))
    shp = jax.ShapeDtypeStruct(w.shape, F32)
    return pl.pallas_call(
        body, name=name, grid=(n_layers, n_tiles),
        in_specs=[part_spec(this) for this in range(n_layers)] + [spec, spec, spec],
        out_specs=[spec] * 4, out_shape=[shp] * 4, compiler_params=_cparams(("arbitrary", "arbitrary")),
    )(*parts, w, m, v)


def _me():
    return lax.axis_index("x"), lax.axis_index("y"), lax.axis_index("c")


N_PEERS = N_DEV - 1


class CommJob:
    def __init__(self, kind, arrays):
        self.kind, self.arrays, self.n = kind, list(arrays), len(arrays)

    def out_shape(self):
        lead = (N_DEV,) if self.kind == 'gather' else ()
        return [jax.ShapeDtypeStruct(lead + a.shape, a.dtype) for a in self.arrays]

    def scratch(self):
        return [pltpu.SemaphoreType.DMA((N_PEERS * self.n,)), pltpu.SemaphoreType.DMA((N_PEERS * self.n,)),
                pltpu.SemaphoreType.DMA((self.n,))]

    def phases(self, in_refs, out_refs, send_sems, recv_sems, local_sems):
        n = self.n
        x, y, c = _me()

        def remote(i, k, src, dst, to):
            return pltpu.make_async_remote_copy(
                src_ref=src, dst_ref=dst, send_sem=send_sems.at[N_PEERS * i + k],
                recv_sem=recv_sems.at[N_PEERS * i + k], device_id=to, device_id_type=MESH)

        if self.kind == 'gather':
            me, sibling = (x, y, c), (x, y, 1 - c)
            chips = [(1 - x, y), (x, 1 - y), (1 - x, 1 - y)]

            def slot(i, px, py, pc):
                return out_refs[i].at[4 * px + 2 * py + pc]

            def copy(i, k, blk, to, src=None):
                return remote(i, k, slot(i, *blk) if src is None else src, slot(i, *blk), to)

            def mine():
                return [pltpu.make_async_copy(in_refs[i], slot(i, *me), local_sems.at[i]) for i in range(n)]

            def first():
                cps = []
                for i in range(n):
                    cps.append(copy(i, 0, me, sibling, src=in_refs[i]))
                    cps += [copy(i, 1 + j, me, (*chip, c), src=in_refs[i]) for j, chip in enumerate(chips)]
                return cps

            def passed():
                return [copy(i, 4 + j, (*chip, c), sibling) for j, chip in enumerate(chips) for i in range(n)]

            def start():
                for cp in mine() + first():
                    cp.start()

            def forward():
                for j, chip in enumerate(chips):
                    for i in range(n):
                        copy(i, 1 + j, (*chip, c), me).wait_recv()
                        copy(i, 4 + j, (*chip, c), sibling).start()

            def finish():
                for i in range(n):
                    copy(i, 0, sibling, me).wait_recv()
                    for j, chip in enumerate(chips):
                        copy(i, 4 + j, (*chip, 1 - c), me).wait_recv()
                for cp in first() + passed():
                    cp.wait_send()
                for cp in mine():
                    cp.wait()

            return start, forward, finish

        my_slot = 4 * x + 2 * y + c

        def mine():
            return [pltpu.make_async_copy(in_refs[i].at[my_slot], out_refs[i].at[my_slot], local_sems.at[i])
                    for i in range(n)]

        def copies():
            cps = []
            for k in range(1, N_DEV):
                px, py, pc = x ^ (k >> 2), y ^ ((k >> 1) & 1), c ^ (k & 1)
                cps += [remote(i, k - 1, in_refs[i].at[4 * px + 2 * py + pc], out_refs[i].at[my_slot], (px, py, pc))
                        for i in range(n)]
            return cps

        def start():
            for cp in mine() + copies():
                cp.start()

        def finish():
            for cp in copies():
                cp.wait_recv()
            for cp in copies():
                cp.wait_send()
            for cp in mine():
                cp.wait()

        return start, (lambda: None), finish


def comm_call(kind, arrays, *, name):
    job = CommJob(kind, arrays)
    n = job.n

    def body(*refs):
        start, forward, finish = job.phases(refs[:n], refs[n:2 * n], *refs[2 * n:])
        start()
        forward()
        finish()

    hbm = pl.BlockSpec(memory_space=pl.ANY)
    return pl.pallas_call(body, name=name, out_shape=job.out_shape(), in_specs=[hbm] * n, out_specs=[hbm] * n,
                          scratch_shapes=job.scratch())(*job.arrays)


def ride_call(job, compute, *, name, grid, in_specs, out_specs, out_shape, ins, sem, scratch=()):
    scratch = list(scratch)
    if job is None:
        return pl.pallas_call(compute, name=name, grid=grid, in_specs=in_specs, out_specs=out_specs,
                              out_shape=out_shape, scratch_shapes=scratch, compiler_params=_cparams(sem))(*ins), None
    n, n_in, n_out, n_scr = job.n, len(ins), len(out_shape), len(scratch)
    n_steps = 1
    for g in grid:
        n_steps *= g

    def body(*refs):
        ins_, job_ins = refs[:n_in], refs[n_in:n_in + n]
        outs, job_outs = refs[n_in + n:n_in + n + n_out], refs[n_in + n + n_out:n_in + 2 * n + n_out]
        rest = refs[n_in + 2 * n + n_out:]
        start, forward, finish = job.phases(job_ins, job_outs, *rest[n_scr:])
        now = 0
        for axis, g in enumerate(grid):
            now = now * g + pl.program_id(axis)
        pl.when(now == 0)(start)
        pl.when(now == n_steps // 2)(forward)
        compute(*ins_, *outs, *rest[:n_scr])
        pl.when(now == n_steps - 1)(finish)

    hbm = pl.BlockSpec(memory_space=pl.ANY)
    res = pl.pallas_call(
        body, name=name, grid=grid, in_specs=list(in_specs) + [hbm] * n,
        out_specs=list(out_specs) + [hbm] * n, out_shape=list(out_shape) + job.out_shape(),
        scratch_shapes=scratch + job.scratch(), compiler_params=_cparams(("arbitrary",) * len(grid)),
    )(*ins, *job.arrays)
    return res[:n_out], res[n_out:]


def to_heads(t, n_heads):
    s_len = t.shape[0]
    return t.reshape(s_len, n_heads, -1).transpose(1, 0, 2)


def from_heads(t):
    return t.transpose(1, 0, 2).reshape(t.shape[1], -1)


def gathered_to_full(t, axis):
    shp = t.shape[1:]
    return jnp.moveaxis(t, 0, axis).reshape(shp[:axis] + (N_DEV * shp[axis],) + shp[axis + 1:])


def full_to_owner_major(g, axis):
    shp = g.shape
    t = jnp.moveaxis(g.reshape(shp[:axis] + (N_DEV, shp[axis] // N_DEV) + shp[axis + 1:]), axis, 0)
    return t.reshape(N_DEV, -1, t.shape[-1])


def _small_rows(shape):
    n = 1
    for s in shape:
        n *= s
    return -(-n // LANES)


def pack_small(arrs, shapes):
    pieces = []
    for n in SMALL:
        flat = arrs[n].reshape(-1)
        flat = jnp.pad(flat, (0, _small_rows(shapes[n]) * LANES - flat.shape[0]))
        pieces.append(flat.reshape(-1, LANES))
    flat = jnp.concatenate(pieces, axis=0)
    return jnp.pad(flat, ((0, -flat.shape[0] % SMALL_ROW_MULTIPLE), (0, 0)))


def unpack_small(flat, shapes):
    out, r = {}, 0
    for n in SMALL:
        rows = _small_rows(shapes[n])
        size = 1
        for s in shapes[n]:
            size *= s
        out[n] = flat[r:r + rows].reshape(-1)[:size].reshape(shapes[n])
        r += rows
    return out


ROW_TM = 256
XATT_TM = 1024
HEAD_TM = 1024
SG_TM = 8 * SG_CHUNK
SB_TILES = (512, 512)
SM_TILE = 1024


def _norm_fwd(x, g, name):
    return prow(f_rms, [x], params=[g.reshape(1, -1)], outs=[(x.shape[1], BF16, False)], tm=ROW_TM, name=name)[0]


def _norm_bwd(x, g, dh, add, name, want_row=True):
    res = prow_vjp(f_rms, [x], params=[g.reshape(1, -1)], cts=[dh], row_grad=[want_row],
                   adds=[add] if want_row else None, tm=ROW_TM, name=name)
    return (res[0], res[1].reshape(-1)) if want_row else (None, res[0].reshape(-1))


def _out_proj(a, w, x, next_gain, alpha, name):
    if next_gain is None:
        return pmm(a, w, res=x, alpha=alpha, name=name), None
    return pmm(a, w, res=x, alpha=alpha, norm_out=next_gain.reshape(1, -1), name=name)


def _in_proj_bwd(d, w, x, gain, dy, name, **kw):
    tk = _pick(w.shape[1])
    if tk < w.shape[1]:
        w, kw = w.reshape(w.shape[0], -1, tk).transpose(1, 0, 2), dict(kw, b_tiles=True)
    dx, g_gain, dx_bf16 = pmm(d, w, tb=True, norm_bwd=(x, gain.reshape(1, -1), dy), name=name, **kw)
    return (dx, dx_bf16), g_gain.reshape(-1)


def ffn_fwd(x, h, p, tag, next_gain, job=None, after_job=None):
    (gate, up, act), landed = ffn_gate_up(h, p['w_gu'], name=f"{tag}_gu", job=job)
    if job is not None:
        after_job(landed)
    out = _out_proj(act, p['w_down'], x, next_gain, 0.5, f"{tag}_down")
    return out, (x, h, gate, up, act)


def _no_rider(run, **own):
    return run(None)[0]


def _pmm_pair(*args, job, **kw):
    out = pmm(*args, job=job, **kw)
    return out if job is not None else (out, None)


def ffn_bwd(dy, p, saved, tag, with_job=_no_rider):
    x, h, gate, up, act = saved
    dy, dy_lo = dy
    d_gate, d_up = with_job(lambda job: ffn_gate_up_bwd(dy_lo, p['w_down'], gate, up, alpha=0.5, name=f"{tag}_dact",
                                                        job=job))
    g_down = pmm(act, dy_lo, ta=True, out_dtype=GRAD_WIRE, alpha=0.5, name=f"{tag}_gdown")
    g_gate = with_job(lambda job: _pmm_pair(h, d_gate, ta=True, out_dtype=GRAD_WIRE, name=f"{tag}_ggate", job=job),
                      w_down=g_down)
    g_gu = jnp.concatenate([g_gate, pmm(h, d_up, ta=True, out_dtype=GRAD_WIRE, name=f"{tag}_gup")], axis=1)
    tk = _pick(d_gate.shape[1])
    w_gu_tiles = p['w_gu'].reshape(p['w_gu'].shape[0], -1, tk).transpose(1, 0, 2)
    dx, g_norm, dx_lo = with_job(
        lambda job: _pmm_pair(d_gate, w_gu_tiles, a2=d_up, tb=True, b_tiles=True,
                              norm_bwd=(x, p['norm'].reshape(1, -1), dy), name=f"{tag}_dh", job=job), w_gu=g_gu)
    return (dx, dx_lo), {'norm': g_norm.reshape(-1), 'w_gu': g_gu, 'w_down': g_down}


def even_mixer_fwd(x, h, p, next_gain, job=None, after_job=None):
    qkv = pmm(h, p['w_in'][:, :3 * SB_WIDTH], out_dtype=BF16, name="sbg_in_qkv")
    z = pmm(h, p['w_in'][:, 3 * SB_WIDTH:], name="sbg_in_gate")
    q, k, v = (to_heads(qkv[:, i * SB_WIDTH:(i + 1) * SB_WIDTH], SB_HEADS) for i in range(3))
    (o_sb, tot), landed = sb_fwd(q, k, v, tq=SB_TILES[0], tk=SB_TILES[1], name="sb_fwd", job=job)
    if job is not None:
        after_job(landed)
    ln_g, ln_b = p['ln_gain'].reshape(1, -1), p['ln_bias'].reshape(1, -1)
    u, gn = prow(f_gate_prep, [z], params=[ln_g, ln_b], outs=[(SG_WIDTH, F32, False)] * 2, tm=ROW_TM,
                 name="sgu_prep")
    gn_p, u_p = ColGroups(gn, SG_PAIR), ColGroups(u, SG_PAIR)
    w_p = p['sgu_w'].reshape(SG_GROUPS // 2, 2 * SG_CHUNK, SG_CHUNK)
    b_p = p['sgu_b'].reshape(SG_GROUPS // 2, 2 * SG_CHUNK, 1)
    o_sg = prow(f_spatial_gate, [gn_p, u_p], gparams=[w_p, b_p], outs=[(SG_PAIR, F32, 'cols')], tm=SG_TM,
                name="sgu_mix")[0]
    cat = jnp.concatenate([from_heads(o_sb), o_sg], axis=-1).astype(BF16)
    out = _out_proj(cat, p['w_out'], x, next_gain, 1.0, "sbg_out")
    return out, (x, h, q, k, v, tot, z, gn_p, u_p, w_p, b_p, cat)


def even_mixer_bwd(dy, p, saved, job_of=None):
    x, h, q, k, v, tot, z, gn_p, u_p, w_p, b_p, cat = saved
    dy, dy_lo = dy
    d_osb = to_heads(pmm(dy_lo, p['w_out'][:SB_WIDTH], tb=True, name="sbg_dcat_sb"), SB_HEADS)
    d_osg = ColGroups(pmm(dy_lo, p['w_out'][SB_WIDTH:], tb=True, name="sbg_dcat_sg"), SG_PAIR)
    g_out = pmm(cat, dy_lo, ta=True, out_dtype=GRAD_WIRE, name="sbg_gout")
    d_gn, d_u, g_w, g_b = prow_vjp(f_spatial_gate, [gn_p, u_p], gparams=[w_p, b_p], cts=[d_osg],
                                   row_grad=[True, True], tm=SG_TM, name="sgu_dmix")
    ln_g, ln_b = p['ln_gain'].reshape(1, -1), p['ln_bias'].reshape(1, -1)
    d_z, g_lng, g_lnb = prow_vjp(f_gate_prep, [z], params=[ln_g, ln_b], cts=[d_u, d_gn], row_grad=[True],
                                 row_dtypes=[BF16], tm=ROW_TM, name="sgu_dprep")
    job = None if job_of is None else job_of({'w_out': g_out})
    (dq, dk, dv), landed = sb_bwd(q, k, v, tot, d_osb, tq=SB_TILES[0], tk=SB_TILES[1], name="sb_bwd", job=job)
    d_proj = jnp.concatenate([from_heads(dq).astype(BF16), from_heads(dk).astype(BF16), from_heads(dv).astype(BF16),
                              d_z], axis=-1)
    g_in = pmm(h, d_proj, ta=True, out_dtype=GRAD_WIRE, name="sbg_gin")
    dx, g_norm = _in_proj_bwd(d_proj, p['w_in'], x, p['norm'], dy, "sbg_dh")
    return dx, {'norm': g_norm, 'w_in': g_in, 'ln_gain': g_lng.reshape(-1), 'ln_bias': g_lnb.reshape(-1),
                'sgu_w': g_w.reshape(SG_GROUPS, SG_CHUNK, SG_CHUNK), 'sgu_b': g_b.reshape(SG_GROUPS, SG_CHUNK),
                'w_out': g_out}, landed


def mla_fwd(x, h, cos, sin, p, next_gain):
    lora = MLA_Q_LORA + MLA_KV_LORA
    c_q = pmm(h, p['w_in'][:, :MLA_Q_LORA], name="mla_in_q")
    c_kv = pmm(h, p['w_in'][:, MLA_Q_LORA:lora], name="mla_in_kv")
    k_r = pmm(h, p['w_in'][:, lora:], name="mla_in_rope")
    cqn = _norm_fwd(c_q, p['q_lora_gain'], "mla_qlora_norm")
    ckvn = _norm_fwd(c_kv, p['kv_lora_gain'], "mla_kvlora_norm")
    q_h = to_heads(pmm(cqn, p['w_uq'], name="mla_uq"), MLA_HEADS)
    kv_h = pmm(ckvn, p['w_ukv'], out_heads=MLA_NOPE + MLA_V, name="mla_ukv")
    q_g, k_g = p['q_gain'].reshape(1, -1), p['k_gain'].reshape(1, -1)
    kp, v = prow(f_mla_k, [kv_h, k_r, cos, sin], params=[k_g], outs=[(MLA_QK, BF16, True), (MLA_V, BF16, True)],
                 tm=HEAD_TM, name="mla_kprep")
    o, lse = sm_fwd(q_h, kp, v, tq=SM_TILE, tk=SM_TILE, name="mla_att_fwd", q_prep=(cos, sin, q_g))
    o_flat = from_heads(o).astype(BF16)
    out = _out_proj(o_flat, p['w_out'], x, next_gain, 1.0, "mla_out")
    return out, (x, h, c_q, c_kv, k_r, cqn, ckvn, q_h, kv_h, v, kp, o, lse, o_flat, q_g, k_g)


def mla_bwd(dy, cos, sin, p, saved, job_of=None):
    x, h, c_q, c_kv, k_r, cqn, ckvn, q_h, kv_h, v, kp, o, lse, o_flat, q_g, k_g = saved
    dy, dy_lo = dy
    do = to_heads(pmm(dy_lo, p['w_out'], tb=True, name="mla_do"), MLA_HEADS)
    g_out = pmm(o_flat, dy_lo, ta=True, out_dtype=GRAD_WIRE, name="mla_gout")
    job = None if job_of is None else job_of({'w_out': g_out})
    (dq_h, dkp, dv, g_qg), landed = sm_bwd(q_h, kp, v, o, lse, do, tq=SM_TILE, tk=SM_TILE, name="mla_att_bwd",
                                           q_prep=(cos, sin, q_g), job=job)
    d_kv_h, dk_r, g_kg = prow_vjp(f_mla_k, [kv_h, k_r, cos, sin], params=[k_g], cts=[dkp, dv],
                                  row_grad=[True, True, False, False], row_dtypes=[BF16, F32], tm=HEAD_TM,
                                  name="mla_dkprep")
    d_q = from_heads(dq_h)
    d_kv = from_heads(d_kv_h)
    g_uq = pmm(cqn, d_q, ta=True, out_dtype=GRAD_WIRE, name="mla_guq")
    d_cqn = pmm(d_q, p['w_uq'], tb=True, name="mla_dcqn")
    g_ukv = pmm(ckvn, d_kv, ta=True, out_dtype=GRAD_WIRE, name="mla_gukv")
    d_ckvn = pmm(d_kv, p['w_ukv'], tb=True, name="mla_dckvn")
    d_cq, g_qlora = _norm_bwd(c_q, p['q_lora_gain'], d_cqn, None, "mla_dqlora_norm")
    d_ckv, g_kvlora = _norm_bwd(c_kv, p['kv_lora_gain'], d_ckvn, None, "mla_dkvlora_norm")
    d_proj = jnp.concatenate([d_cq, d_ckv, dk_r], axis=-1).astype(BF16)
    g_in = pmm(h, d_proj, ta=True, out_dtype=GRAD_WIRE, name="mla_gin")
    dx, g_norm = _in_proj_bwd(d_proj, p['w_in'], x, p['norm'], dy, "mla_dh")
    return dx, {'norm': g_norm, 'w_in': g_in, 'q_lora_gain': g_qlora, 'kv_lora_gain': g_kvlora, 'w_uq': g_uq,
                'w_ukv': g_ukv, 'q_gain': g_qg.reshape(-1), 'k_gain': g_kg.reshape(-1), 'w_out': g_out}, landed


def xattn_fwd(x, hq, mem, p, tag, next_gain):
    hm = _norm_fwd(mem, p['mem_norm'], f"{tag}_mem_norm")
    q_h = ColGroups(pmm(hq, p['wq'], name=f"{tag}_q"), MEM_HEAD_DIM)
    kv = pmm(hm, p['wkv'], name=f"{tag}_kv").reshape(mem.shape[0], MEM_HEADS, 2 * MEM_HEAD_DIM).transpose(1, 0, 2)
    k_h, v_h = kv[..., :MEM_HEAD_DIM], kv[..., MEM_HEAD_DIM:]
    q_g, k_g = p['q_gain'].reshape(1, -1), p['k_gain'].reshape(1, -1)
    o_flat = prow(f_xattn, [q_h], gparams=[k_h, v_h], params=[q_g, k_g], outs=[(MEM_HEAD_DIM, BF16, 'cols')],
                  tm=XATT_TM, name=f"{tag}_att")[0]
    out = _out_proj(o_flat, p['wo'], x, next_gain, 1.0, f"{tag}_out")
    return out, (x, mem, hq, hm, q_h, k_h, v_h, q_g, k_g, o_flat)


def xattn_bwd(dy, p, saved, tag):
    x, mem, hq, hm, q_h, k_h, v_h, q_g, k_g, o_flat = saved
    dy, dy_lo = dy
    d_o = ColGroups(pmm(dy_lo, p['wo'], tb=True, name=f"{tag}_do"), MEM_HEAD_DIM)
    g_wo = pmm(o_flat, dy_lo, ta=True, out_dtype=GRAD_WIRE, name=f"{tag}_gwo")
    d_q, dk_h, dv_h, g_qg, g_kg = prow_vjp(f_xattn, [q_h], gparams=[k_h, v_h], params=[q_g, k_g], cts=[d_o],
                                           row_grad=[True], row_dtypes=[BF16], tm=XATT_TM, name=f"{tag}_datt")
    d_kv = jnp.concatenate([dk_h, dv_h], axis=-1).transpose(1, 0, 2).reshape(mem.shape[0], -1).astype(BF16)
    g_wq = pmm(hq, d_q, ta=True, out_dtype=GRAD_WIRE, name=f"{tag}_gwq")
    dx, g_norm = _in_proj_bwd(d_q, p['wq'], x, p['norm'], dy, f"{tag}_dhq")
    g_wkv = pmm(hm, d_kv, ta=True, out_dtype=GRAD_WIRE, name=f"{tag}_gwkv")
    dhm = pmm(d_kv, p['wkv'], tb=True, name=f"{tag}_dhm")
    _, g_mem_norm = _norm_bwd(mem, p['mem_norm'], dhm, None, f"{tag}_dmem_norm", want_row=False)
    return dx, {'norm': g_norm, 'mem_norm': g_mem_norm, 'wq': g_wq, 'wkv': g_wkv, 'q_gain': g_qg.reshape(-1),
                'k_gain': g_kg.reshape(-1), 'wo': g_wo}


def rope_tables(positions):
    half = MLA_ROPE // 2
    inv_freq = ROPE_THETA ** (-jnp.arange(half, dtype=F32) / half)
    ang = positions.astype(F32)[:, None] * inv_freq
    cos, sin = jnp.cos(ang), jnp.sin(ang)
    lead = jnp.ones((ang.shape[0], MLA_NOPE), F32)
    return jnp.concatenate([lead, cos, cos], axis=1), jnp.concatenate([0.0 * lead, sin, sin], axis=1)


FIRST_UNIT = ('ffn_pre_w_gu', 0)
EARLY_UNITS = [('ffn_pre_w_down', 0), ('sbg_w_in', 0)]


def local_step(x, mem, positions, target, w, shards):
    cos, sin = rope_tables(positions)
    full = {}

    def absorb(units, gathered):
        for (n, layer), t in zip(units, gathered):
            full[(n, layer)] = gathered_to_full(t, BIG[n] - 1)

    late_units = [u for u in shards if u != FIRST_UNIT and u not in EARLY_UNITS]
    (h,), gathered = prow(f_rms, [x], params=[w['ffn_pre_norm'][0].reshape(1, -1)], outs=[(x.shape[1], BF16, False)],
                          tm=ROW_TM, name="ffn_pre0_norm", job=CommJob('gather', [shards[FIRST_UNIT]]))
    absorb([FIRST_UNIT], gathered)
    first_ffn_p = {'norm': w['ffn_pre_norm'][0], 'w_gu': full[FIRST_UNIT]}

    def ffn_params(kind, layer):
        return {'norm': w[f'ffn_{kind}_norm'][layer], 'w_gu': full[(f'ffn_{kind}_w_gu', layer)],
                'w_down': full[(f'ffn_{kind}_w_down', layer)]}

    def xattn_params(layer):
        return {'norm': w['xmem_norm'][layer], 'mem_norm': w['xmem_mem_norm'][layer], 'wq': full[('xmem_wq', layer)],
                'wkv': full[('xmem_wkv', layer)], 'q_gain': w['xmem_q_gain'][layer], 'k_gain': w['xmem_k_gain'][layer],
                'wo': full[('xmem_wo', layer)]}

    even_p = {'norm': w['mix_norm'][0], 'ln_gain': w['sgu_ln_gain'][0], 'ln_bias': w['sgu_ln_bias'][0],
              'sgu_w': w['sgu_w'][0], 'sgu_b': w['sgu_b'][0]}

    def early_weights_landed(gathered):
        absorb(EARLY_UNITS, gathered)
        first_ffn_p['w_down'] = full[('ffn_pre_w_down', 0)]
        even_p['w_in'] = full[('sbg_w_in', 0)]

    def late_weights_landed(gathered):
        absorb(late_units, gathered)
        even_p['w_out'] = full[('sbg_w_out', 0)]

    def mla_params():
        return {'norm': w['mix_norm'][1], 'w_in': full[('mla_w_in', 0)], 'q_lora_gain': w['mla_q_lora_gain'][0],
                'kv_lora_gain': w['mla_kv_lora_gain'][0], 'w_uq': full[('mla_w_uq', 0)],
                'w_ukv': full[('mla_w_ukv', 0)], 'q_gain': w['mla_q_gain'][0], 'k_gain': w['mla_k_gain'][0],
                'w_out': full[('mla_w_out', 0)]}

    saved = []
    for layer in range(DEPTH):
        if layer == 0:
            (x, h), s_pre = ffn_fwd(x, h, first_ffn_p, "ffn_pre0", w['mix_norm'][0],
                                    job=CommJob('gather', [shards[u] for u in EARLY_UNITS]),
                                    after_job=early_weights_landed)
        else:
            (x, h), s_pre = ffn_fwd(x, h, ffn_params('pre', layer), f"ffn_pre{layer}", w['mix_norm'][layer])
        if layer % 2 == 0:
            (x, h), s_mix = even_mixer_fwd(x, h, even_p, w['xmem_norm'][layer],
                                           job=CommJob('gather', [shards[u] for u in late_units]),
                                           after_job=late_weights_landed)
        else:
            (x, h), s_mix = mla_fwd(x, h, cos, sin, mla_params(), w['xmem_norm'][layer])
        (x, h), s_x = xattn_fwd(x, h, mem, xattn_params(layer), f"xmem{layer}", w['ffn_post_norm'][layer])
        following = w['ffn_pre_norm'][layer + 1] if layer + 1 < DEPTH else None
        (x, h), s_post = ffn_fwd(x, h, ffn_params('post', layer), f"ffn_post{layer}", following)
        saved.append((s_pre, s_mix, s_x, s_post))

    dx, loss = loss_head(x, target, tm=ROW_TM, name="loss_head")

    ready, riding, landed = {}, [], {}

    def offer(name, layer, g):
        ready[(name, layer)] = full_to_owner_major(g, BIG[name] - 1)

    def ride(name):
        def job_of(own):
            offer(name, 0, own['w_out'])
            riding[:] = list(ready)
            return CommJob('exchange', [ready.pop(u) for u in riding])
        return job_of

    def last_rides(run, **own):
        for kind, g in own.items():
            offer('ffn_pre_' + kind, 0, g)
        units = list(ready)
        if not units:
            return run(None)[0]
        res, arrived = run(CommJob('exchange', [ready.pop(u) for u in units]))
        landed.update(zip(units, arrived))
        return res

    per_layer = []
    for layer in reversed(range(DEPTH)):
        s_pre, s_mix, s_x, s_post = saved[layer]
        dx, g_post = ffn_bwd(dx, ffn_params('post', layer), s_post, f"ffn_post{layer}")
        offer('ffn_post_w_gu', layer, g_post['w_gu'])
        offer('ffn_post_w_down', layer, g_post['w_down'])
        dx, g_x = xattn_bwd(dx, xattn_params(layer), s_x, f"xmem{layer}")
        for n in ('wq', 'wkv', 'wo'):
            offer('xmem_' + n, layer, g_x[n])
        if layer % 2 == 0:
            dx, g_mix, arrived = even_mixer_bwd(dx, even_p, s_mix, job_of=ride('sbg_w_out'))
            landed.update(zip(riding, arrived))
            offer('sbg_w_in', 0, g_mix['w_in'])
        else:
            dx, g_mix, arrived = mla_bwd(dx, cos, sin, mla_params(), s_mix, job_of=ride('mla_w_out'))
            landed.update(zip(riding, arrived))
            for n in ('w_in', 'w_uq', 'w_ukv'):
                offer('mla_' + n, 0, g_mix[n])
        if layer == 0:
            dx, g_pre = ffn_bwd(dx, ffn_params('pre', layer), s_pre, f"ffn_pre{layer}", with_job=last_rides)
        else:
            dx, g_pre = ffn_bwd(dx, ffn_params('pre', layer), s_pre, f"ffn_pre{layer}")
            offer('ffn_pre_w_gu', layer, g_pre['w_gu'])
            offer('ffn_pre_w_down', layer, g_pre['w_down'])
        per_layer.append((layer, g_pre, g_mix, g_x, g_post))
    per_layer.sort(key=lambda t: t[0])
    assert not ready

    def stack(pick):
        return jnp.stack([pick(t) for t in per_layer])

    g_even, g_mla = per_layer[0][2], per_layer[1][2]
    small_grads = {
        'ffn_pre_norm': stack(lambda t: t[1]['norm']), 'mix_norm': stack(lambda t: t[2]['norm']),
        'sgu_ln_gain': g_even['ln_gain'][None], 'sgu_ln_bias': g_even['ln_bias'][None],
        'sgu_w': g_even['sgu_w'][None], 'sgu_b': g_even['sgu_b'][None],
        'mla_q_lora_gain': g_mla['q_lora_gain'][None], 'mla_kv_lora_gain': g_mla['kv_lora_gain'][None],
        'mla_q_gain': g_mla['q_gain'][None], 'mla_k_gain': g_mla['k_gain'][None],
        'xmem_norm': stack(lambda t: t[3]['norm']), 'xmem_mem_norm': stack(lambda t: t[3]['mem_norm']),
        'xmem_q_gain': stack(lambda t: t[3]['q_gain']), 'xmem_k_gain': stack(lambda t: t[3]['k_gain']),
        'ffn_post_norm': stack(lambda t: t[4]['norm']),
    }
    return loss, dx[0], small_grads, landed


def _device_slot():
    x, y, c = _me()
    return 4 * x + 2 * y + c


def kernel(x, mem, positions, ffn_pre_norm, ffn_pre_w_gu, ffn_pre_w_down, mix_norm, sbg_w_in, sgu_ln_gain, sgu_ln_bias, sgu_w, sgu_b, sbg_w_out, mla_w_in, mla_q_lora_gain, mla_kv_lora_gain, mla_w_uq, mla_w_ukv, mla_q_gain, mla_k_gain, mla_w_out, xmem_norm, xmem_mem_norm, xmem_wq, xmem_wkv, xmem_q_gain, xmem_k_gain, xmem_wo, ffn_post_norm, ffn_post_w_gu, ffn_post_w_down, loss_target, m_ffn_pre_norm, m_ffn_pre_w_gu, m_ffn_pre_w_down, m_mix_norm, m_sbg_w_in, m_sgu_ln_gain, m_sgu_ln_bias, m_sgu_w, m_sgu_b, m_sbg_w_out, m_mla_w_in, m_mla_q_lora_gain, m_mla_kv_lora_gain, m_mla_w_uq, m_mla_w_ukv, m_mla_q_gain, m_mla_k_gain, m_mla_w_out, m_xmem_norm, m_xmem_mem_norm, m_xmem_wq, m_xmem_wkv, m_xmem_q_gain, m_xmem_k_gain, m_xmem_wo, m_ffn_post_norm, m_ffn_post_w_gu, m_ffn_post_w_down, v_ffn_pre_norm, v_ffn_pre_w_gu, v_ffn_pre_w_down, v_mix_norm, v_sbg_w_in, v_sgu_ln_gain, v_sgu_ln_bias, v_sgu_w, v_sgu_b, v_sbg_w_out, v_mla_w_in, v_mla_q_lora_gain, v_mla_kv_lora_gain, v_mla_w_uq, v_mla_w_ukv, v_mla_q_gain, v_mla_k_gain, v_mla_w_out, v_xmem_norm, v_xmem_mem_norm, v_xmem_wq, v_xmem_wkv, v_xmem_q_gain, v_xmem_k_gain, v_xmem_wo, v_ffn_post_norm, v_ffn_post_w_gu, v_ffn_post_w_down):
    args = locals()
    w_in = {n: args[n] for n in WEIGHTS}
    m_in = {n: args["m_" + n] for n in WEIGHTS}
    v_in = {n: args["v_" + n] for n in WEIGHTS}
    slot = _device_slot()

    tiny = jnp.zeros((8, LANES), F32)
    for i, src in enumerate((w_in, m_in, v_in)):
        tiny = tiny.at[i, :64].set(src['mla_q_lora_gain'][0]).at[i + 3, :32].set(src['mla_kv_lora_gain'][0])
    tiny_all = comm_call('gather', [tiny], name="gather_lora_gains")[0]
    full_small = []
    for i, src in enumerate((w_in, m_in, v_in)):
        d = {n: src[n] for n in SMALL}
        d['mla_q_lora_gain'] = tiny_all[:, i, :64].reshape(1, MLA_Q_LORA)
        d['mla_kv_lora_gain'] = tiny_all[:, i + 3, :32].reshape(1, MLA_KV_LORA)
        full_small.append(d)
    w_small, m_small, v_small = full_small
    small_shapes = {n: w_small[n].shape for n in SMALL}

    shards = {(n, layer): w_in[n][layer].astype(BF16) for n in BIG for layer in range(w_in[n].shape[0])}
    loss, dx, grads, landed = local_step(x[0], mem[0], positions[0], loss_target[0], w_small, shards)
    loss = lax.psum(loss, ("x", "y", "c"))
    big_out = {n: adamw([landed[(n, layer)] for layer in range(w_in[n].shape[0])], w_in[n], m_in[n], v_in[n],
                        name=f"adamw_{n}") for n in BIG}

    small_parts = comm_call('gather', [pack_small(grads, small_shapes)], name="gather_small_grads")
    small_out = adamw(small_parts, pack_small(w_small, small_shapes)[None], pack_small(m_small, small_shapes)[None],
                      pack_small(v_small, small_shapes)[None], name="adamw_small")
    small_out = [unpack_small(t[0], small_shapes) for t in small_out]
    for d in small_out:
        for n, width in zip(GAIN_SHARDED, (64, 32)):
            d[n] = lax.dynamic_slice(d[n], (0, slot * width), (1, width))

    outs = [loss, dx[None]]
    for kind, small_d in enumerate(small_out):
        outs += [big_out[n][kind] if n in BIG else small_d[n] for n in WEIGHTS]
    return tuple(outs)
```

```python
import jax
import jax.numpy as jnp
from jax import lax
from jax.experimental import pallas as pl
from jax.experimental.pallas import tpu as pltpu

F32 = jnp.float32
BF16 = jnp.bfloat16
MESH = pl.DeviceIdType.MESH
N_DEV = 8

VMEM_LIMIT_BYTES = 56 * 1024 * 1024
LANES = 128

D_MODEL = 1024
DEPTH = 2
EPS = 1e-6
SB_HEADS, SB_HEAD_DIM = 8, 64
SB_WIDTH = SB_HEADS * SB_HEAD_DIM
SG_GROUPS, SG_GROUP_DIM, SG_CHUNK = 8, 64, 128
SG_WIDTH = SG_GROUPS * SG_GROUP_DIM
MLA_HEADS, MLA_NOPE, MLA_ROPE, MLA_V = 16, 64, 32, 64
MLA_QK = MLA_NOPE + MLA_ROPE
MLA_Q_LORA, MLA_KV_LORA = 512, 256
ROPE_THETA = 10000.0
MEM_HEADS = 4
MEM_HEAD_DIM = D_MODEL // MEM_HEADS

ADAM_LR, ADAM_B1, ADAM_B2, ADAM_EPS, ADAM_WD, ADAM_STEP = 0.001, 0.9, 0.999, 1e-08, 0.01, 10

WEIGHTS = ['ffn_pre_norm', 'ffn_pre_w_gu', 'ffn_pre_w_down', 'mix_norm', 'sbg_w_in', 'sgu_ln_gain', 'sgu_ln_bias',
           'sgu_w', 'sgu_b', 'sbg_w_out', 'mla_w_in', 'mla_q_lora_gain', 'mla_kv_lora_gain', 'mla_w_uq', 'mla_w_ukv',
           'mla_q_gain', 'mla_k_gain', 'mla_w_out', 'xmem_norm', 'xmem_mem_norm', 'xmem_wq', 'xmem_wkv',
           'xmem_q_gain', 'xmem_k_gain', 'xmem_wo', 'ffn_post_norm', 'ffn_post_w_gu', 'ffn_post_w_down']
BIG = {'ffn_pre_w_gu': 2, 'ffn_pre_w_down': 1, 'sbg_w_in': 2, 'sbg_w_out': 1, 'mla_w_in': 1, 'mla_w_uq': 2,
       'mla_w_ukv': 2, 'mla_w_out': 1, 'xmem_wq': 1, 'xmem_wkv': 2, 'xmem_wo': 1, 'ffn_post_w_gu': 2,
       'ffn_post_w_down': 1}
GAIN_SHARDED = ('mla_q_lora_gain', 'mla_kv_lora_gain')
SMALL = [n for n in WEIGHTS if n not in BIG]
GRAD_WIRE = BF16
FFN_SAVE = BF16
SMALL_ROW_MULTIPLE = 16


def _cparams(sem=None):
    return pltpu.CompilerParams(dimension_semantics=sem, vmem_limit_bytes=VMEM_LIMIT_BYTES)


MM_TILE_CAP = 1408
HEAD_MAJOR_ROWS = 4096


def _pick(dim, cap=MM_TILE_CAP):
    if dim % LANES:
        return dim
    return max(t for t in range(LANES, min(dim, cap) + 1, LANES) if dim % t == 0)


def _rms(x, g):
    return x * lax.rsqrt(jnp.mean(x * x, axis=-1, keepdims=True) + EPS) * g


def pmm(a, b, *, a2=None, ta=False, tb=False, out_dtype=F32, res=None, alpha=1.0, norm_out=None, norm_bwd=None,
        out_heads=None, b_tiles=False, job=None, name):
    kdim, m = (a.shape if ta else a.shape[::-1])
    n = b.shape[1] if b_tiles else (b.shape[0] if tb else b.shape[1])
    tm, tn, tk = _pick(m), _pick(n), _pick(kdim)
    if b_tiles:
        assert tb and b.shape[2] == tk
    if out_heads is not None:
        tn, tm = out_heads, _pick(m, HEAD_MAJOR_ROWS)
    whole_rows = norm_out is not None or norm_bwd is not None
    if whole_rows:
        assert tn == n
    if norm_bwd is not None:
        tm = min(tm, 512)
    nk1 = kdim // tk
    nk = nk1 if a2 is None else 2 * nk1
    assert a2 is None or (a2.shape == a.shape and not ta)
    dims = (((0 if ta else 1,), (1 if tb else 0,)), ((), ()))
    n_lead = 2 if a2 is None else 3
    n_extra = (res is not None) + (norm_out is not None) + (0 if norm_bwd is None else 2 + (norm_bwd[2] is not None))

    def body(*refs):
        a_ref, b_ref = refs[:2]
        extra = list(refs[n_lead:n_lead + n_extra])
        outs, acc_ref = refs[n_lead + n_extra:-1], refs[-1]
        i, k = pl.program_id(0), pl.program_id(2)

        @pl.when(k == 0)
        def _():
            acc_ref[...] = jnp.zeros_like(acc_ref)

        def accumulate(lhs_ref):
            rhs = b_ref[k] if b_tiles else b_ref[...]
            acc_ref[...] += lax.dot_general(lhs_ref[...].astype(BF16), rhs.astype(BF16), dims,
                                            preferred_element_type=F32)

        if a2 is None:
            accumulate(a_ref)
        else:
            pl.when(k < nk1)(lambda: accumulate(a_ref))
            pl.when(k >= nk1)(lambda: accumulate(refs[2]))

        @pl.when(k == nk - 1)
        def _():
            r = acc_ref[...]
            if alpha != 1.0:
                r = r * alpha
            if res is not None:
                r = extra.pop(0)[...] + r
            if norm_bwd is None:
                outs[0][...] = r.astype(out_dtype)
            if norm_out is not None:
                outs[1][...] = _rms(r, extra.pop(0)[...]).astype(BF16)
            if norm_bwd is not None:
                x_ref, g_ref = extra.pop(0), extra.pop(0)
                _, pull = jax.vjp(_rms, x_ref[...], g_ref[...])
                dx, dg = pull(r)
                if norm_bwd[2] is not None:
                    dx = dx + extra.pop(0)[...]
                outs[0][...] = dx

                @pl.when(i == 0)
                def _():
                    outs[1][...] = dg

                @pl.when(i != 0)
                def _():
                    outs[1][...] += dg

    gi, gj = m // tm, n // tn
    a_bytes, b_bytes = a.size * a.dtype.itemsize, (n * kdim) * b.dtype.itemsize
    j_outer = not whole_rows and nk == 1 and gj * a_bytes + b_bytes < a_bytes + gi * b_bytes
    grid = (gj, gi, nk) if j_outer else (gi, gj, nk)

    def spec(block, index):
        return pl.BlockSpec(block, (lambda j, i, k: index(i, j, k)) if j_outer else index)

    a_spec = spec((tk, tm), lambda i, j, k: (k, i)) if ta else spec((tm, tk), lambda i, j, k: (i, k))
    b_spec = spec((tn, tk), lambda i, j, k: (j, k)) if tb else spec((tk, tn), lambda i, j, k: (k, j))
    if b_tiles:
        assert tn == n
        b_spec = spec(b.shape, lambda i, j, k: (0, 0, 0))
    o_spec = spec((tm, tn), lambda i, j, k: (i, j))
    g_spec = spec((1, tn), lambda i, j, k: (0, 0))
    ins, in_specs = [a, b], [a_spec, b_spec]
    if a2 is not None:
        in_specs[0] = spec((tm, tk), lambda i, j, k: (i, jnp.minimum(k, nk1 - 1)))
        ins.append(a2)
        in_specs.append(spec((tm, tk), lambda i, j, k: (i, jnp.maximum(k - nk1, 0))))
    if res is not None:
        ins.append(res)
        in_specs.append(o_spec)
    out_shape, out_specs = [jax.ShapeDtypeStruct((m, n), out_dtype)], [o_spec]
    if out_heads is not None:
        out_shape = [jax.ShapeDtypeStruct((n // tn, m, tn), out_dtype)]
        out_specs = [spec((None, tm, tn), lambda i, j, k: (j, i, 0))]
    if norm_out is not None:
        ins.append(norm_out)
        in_specs.append(g_spec)
        out_shape.append(jax.ShapeDtypeStruct((m, n), BF16))
        out_specs.append(o_spec)
    if norm_bwd is not None:
        ins += [t for t in norm_bwd if t is not None]
        in_specs += [o_spec, g_spec] + ([o_spec] if norm_bwd[2] is not None else [])
        out_shape = [jax.ShapeDtypeStruct((m, n), F32), jax.ShapeDtypeStruct((1, n), F32)]
        out_specs = [o_spec, g_spec]
    result, landed = ride_call(
        job, body, name=name, grid=grid, in_specs=in_specs, out_specs=out_specs, out_shape=out_shape, ins=ins,
        scratch=[pltpu.VMEM((tm, tn), F32)],
        sem=("arbitrary" if norm_bwd is not None else "parallel", "parallel", "arbitrary"))
    result = result if whole_rows else result[0]
    return result if job is None else (result, landed)


def ffn_gate_up(h, w_gu, *, name, job=None):
    m, kdim = h.shape
    n = w_gu.shape[1] // 2
    tm, tn = min(_pick(m), 512), _pick(n)
    up_off = n // tn

    def body(a_ref, bg_ref, bu_ref, gate_ref, up_ref, act_ref):
        av = a_ref[...].astype(BF16)
        gate = _dg(av, bg_ref[...].astype(BF16), 1, 0)
        up = _dg(av, bu_ref[...].astype(BF16), 1, 0)
        gate_ref[...] = gate.astype(gate_ref.dtype)
        up_ref[...] = up.astype(up_ref.dtype)
        act_ref[...] = (jax.nn.silu(gate) * up).astype(BF16)

    o_spec = pl.BlockSpec((tm, tn), lambda j, i: (i, j))
    return ride_call(
        job, body, name=name, grid=(n // tn, m // tm),
        in_specs=[pl.BlockSpec((tm, kdim), lambda j, i: (i, 0)), pl.BlockSpec((kdim, tn), lambda j, i: (0, j)),
                  pl.BlockSpec((kdim, tn), lambda j, i: (0, j + up_off))],
        out_specs=[o_spec] * 3,
        out_shape=[jax.ShapeDtypeStruct((m, n), FFN_SAVE), jax.ShapeDtypeStruct((m, n), FFN_SAVE),
                   jax.ShapeDtypeStruct((m, n), BF16)],
        ins=[h, w_gu, w_gu], sem=("parallel", "parallel"))


def ffn_gate_up_bwd(dy, w_down, gate, up, *, alpha, name, job=None):
    m, kdim = dy.shape
    n = w_down.shape[0]
    tm, tn = min(_pick(m), 512), _pick(n)

    def body(a_ref, b_ref, gate_ref, up_ref, dgate_ref, dup_ref):
        d_act = _dg(a_ref[...].astype(BF16), b_ref[...].astype(BF16), 1, 1) * alpha
        _, pull = jax.vjp(lambda g, u: jax.nn.silu(g) * u, gate_ref[...].astype(F32), up_ref[...].astype(F32))
        d_gate, d_up = pull(d_act)
        dgate_ref[...] = d_gate.astype(BF16)
        dup_ref[...] = d_up.astype(BF16)

    o_spec = pl.BlockSpec((tm, tn), lambda j, i: (i, j))
    return ride_call(
        job, body, name=name, grid=(n // tn, m // tm),
        in_specs=[pl.BlockSpec((tm, kdim), lambda j, i: (i, 0)), pl.BlockSpec((tn, kdim), lambda j, i: (j, 0)),
                  o_spec, o_spec],
        out_specs=[o_spec] * 2, out_shape=[jax.ShapeDtypeStruct((m, n), BF16)] * 2,
        ins=[dy, w_down, gate, up], sem=("parallel", "parallel"))


def _dg(a, b, ca, cb):
    return lax.dot_general(a, b, (((ca,), (cb,)), ((), ())), preferred_element_type=F32)


@jax.custom_vjp
def bdot(a, b):
    return _dg(a.astype(BF16), b.astype(BF16), 1, 0)


def _bdot_fwd(a, b):
    ab, bb = a.astype(BF16), b.astype(BF16)
    return _dg(ab, bb, 1, 0), (ab, bb)


def _bdot_bwd(saved, g):
    ab, bb = saved
    gb = g.astype(BF16)
    return _dg(gb, bb, 1, 1), _dg(ab, gb, 0, 0)


bdot.defvjp(_bdot_fwd, _bdot_bwd)


@jax.custom_vjp
def bdot_nt(a, b):
    return _dg(a.astype(BF16), b.astype(BF16), 1, 1)


def _bdot_nt_fwd(a, b):
    ab, bb = a.astype(BF16), b.astype(BF16)
    return _dg(ab, bb, 1, 1), (ab, bb)


def _bdot_nt_bwd(saved, g):
    ab, bb = saved
    gb = g.astype(BF16)
    return _dg(gb, bb, 1, 0), _dg(gb, ab, 0, 0)


bdot_nt.defvjp(_bdot_nt_fwd, _bdot_nt_bwd)


class ColGroups:
    def __init__(self, arr, width):
        self.arr, self.width = arr, width
        self.shape, self.dtype, self.ndim = arr.shape, arr.dtype, 3


def _plain(a):
    return a.arr if isinstance(a, ColGroups) else a


def _row_spec(arr, tm):
    if isinstance(arr, ColGroups):
        return pl.BlockSpec((tm, arr.width), lambda r, g: (r, g))
    if arr.ndim == 3:
        return pl.BlockSpec((None, tm, arr.shape[2]), lambda r, g: (g, r, 0))
    return pl.BlockSpec((tm, arr.shape[1]), lambda r, g: (r, 0))


def _gparam_spec(arr):
    return pl.BlockSpec((None,) + arr.shape[1:], lambda r, g: (g, 0, 0))


def _whole_spec(arr):
    nd = arr.ndim
    return pl.BlockSpec(arr.shape, lambda r, g: (0,) * nd)


def _groups(rows, gparams):
    gs = {a.shape[1] // a.width if isinstance(a, ColGroups) else a.shape[0] for a in rows if a.ndim == 3}
    gs |= {a.shape[0] for a in gparams}
    assert len(gs) <= 1
    return gs.pop() if gs else 1


def prow(fn, rows, gparams=(), params=(), *, outs, tm, name, job=None):
    rows, gparams, params = list(rows), list(gparams), list(params)
    n_groups = _groups(rows, gparams)
    n_rows = rows[0].shape[-2]
    n_in = len(rows) + len(gparams) + len(params)

    def body(*refs):
        vals = [r[...] for r in refs[:n_in]]
        res = fn(*vals)
        for o_ref, r in zip(refs[n_in:], res, strict=True):
            o_ref[...] = r.astype(o_ref.dtype)

    out_shape, out_specs = [], []
    for width, dtype, grouped in outs:
        if grouped == 'cols':
            out_shape.append(jax.ShapeDtypeStruct((n_rows, n_groups * width), dtype))
            out_specs.append(_row_spec(ColGroups(out_shape[-1], width), tm))
            continue
        shp = (n_groups, n_rows, width) if grouped else (n_rows, width)
        out_shape.append(jax.ShapeDtypeStruct(shp, dtype))
        out_specs.append(_row_spec(out_shape[-1], tm))
    result, landed = ride_call(
        job, body, name=name, grid=(n_rows // tm, n_groups),
        in_specs=[_row_spec(a, tm) for a in rows] + [_gparam_spec(a) for a in gparams] + [_whole_spec(a) for a in params],
        out_specs=out_specs, out_shape=out_shape, ins=[*[_plain(a) for a in rows], *gparams, *params],
        sem=("parallel", "arbitrary"))
    return result if job is None else (result, landed)


def prow_vjp(fn, rows, gparams=(), params=(), *, cts, row_grad, adds=None, row_dtypes=None, gparam_grad=None,
             param_grad=None, tm, name):
    rows, gparams, params, cts = list(rows), list(gparams), list(params), list(cts)
    gparam_grad = list(gparam_grad) if gparam_grad is not None else [True] * len(gparams)
    param_grad = list(param_grad) if param_grad is not None else [True] * len(params)
    n_groups = _groups(rows + cts, gparams)
    n_rows = rows[0].shape[-2]
    want_rows = [i for i, w in enumerate(row_grad) if w]
    adds = list(adds) if adds is not None else [None] * len(want_rows)
    row_dtypes = list(row_dtypes) if row_dtypes is not None else [F32] * len(want_rows)
    add_arrays = [a for a in adds if a is not None]
    n_r, n_g, n_p, n_c, n_a = len(rows), len(gparams), len(params), len(cts), len(add_arrays)
    mask = list(row_grad) + gparam_grad + param_grad

    def body(*refs):
        r_id, g_id = pl.program_id(0), pl.program_id(1)
        n_in = n_r + n_g + n_p
        vals = [r[...] for r in refs[:n_in]]
        ct_vals = tuple(r[...].astype(F32) for r in refs[n_in:n_in + n_c])
        add_refs = list(refs[n_in + n_c:n_in + n_c + n_a])
        out_refs = list(refs[n_in + n_c + n_a:])
        diff_idx = [i for i, w in enumerate(mask) if w]

        def wrapped(*diff):
            full = list(vals)
            for i, d in zip(diff_idx, diff):
                full[i] = d
            return tuple(fn(*full))

        _, pull = jax.vjp(wrapped, *[vals[i].astype(F32) for i in diff_idx])
        grads = dict(zip(diff_idx, pull(ct_vals)))
        k = 0
        for j, i in enumerate(want_rows):
            o_ref = out_refs[k]
            k += 1
            gval = grads[i]
            if adds[j] is not None:
                gval = gval + add_refs.pop(0)[...].astype(F32)
            if rows[i].ndim == 2 and n_groups > 1:
                @pl.when(g_id == 0)
                def _(o_ref=o_ref, gval=gval):
                    o_ref[...] = gval.astype(o_ref.dtype)

                @pl.when(g_id != 0)
                def _(o_ref=o_ref, gval=gval):
                    o_ref[...] += gval.astype(o_ref.dtype)
            else:
                o_ref[...] = gval.astype(o_ref.dtype)
        for i in range(n_g):
            if not gparam_grad[i]:
                continue
            o_ref = out_refs[k]
            k += 1
            gval = grads[n_r + i]

            @pl.when(r_id == 0)
            def _(o_ref=o_ref, gval=gval):
                o_ref[g_id] = gval

            @pl.when(r_id != 0)
            def _(o_ref=o_ref, gval=gval):
                o_ref[g_id] += gval
        for i in range(n_p):
            if not param_grad[i]:
                continue
            o_ref = out_refs[k]
            k += 1
            gval = grads[n_r + n_g + i]
            first = jnp.logical_and(r_id == 0, g_id == 0)

            @pl.when(first)
            def _(o_ref=o_ref, gval=gval):
                o_ref[...] = gval

            @pl.when(jnp.logical_not(first))
            def _(o_ref=o_ref, gval=gval):
                o_ref[...] += gval

    out_shape, out_specs = [], []
    for j, i in enumerate(want_rows):
        out_shape.append(jax.ShapeDtypeStruct(rows[i].shape, row_dtypes[j]))
        out_specs.append(_row_spec(rows[i], tm))
    for i in range(n_g):
        if gparam_grad[i]:
            out_shape.append(jax.ShapeDtypeStruct(gparams[i].shape, F32))
            out_specs.append(_whole_spec(gparams[i]))
    for i in range(n_p):
        if param_grad[i]:
            out_shape.append(jax.ShapeDtypeStruct(params[i].shape, F32))
            out_specs.append(_whole_spec(params[i]))
    return pl.pallas_call(
        body, name=name, grid=(n_rows // tm, n_groups),
        in_specs=([_row_spec(a, tm) for a in rows] + [_gparam_spec(a) for a in gparams]
                  + [_whole_spec(a) for a in params] + [_row_spec(a, tm) for a in cts]
                  + [_row_spec(a, tm) for a in add_arrays]),
        out_specs=out_specs, out_shape=out_shape,
        compiler_params=_cparams(("arbitrary", "arbitrary")),
    )(*[_plain(a) for a in rows], *gparams, *params, *[_plain(a) for a in cts], *add_arrays)


def f_rms(x, g):
    return (_rms(x.astype(F32), g),)


def f_gate_prep(z, ln_g, ln_b):
    act = jax.nn.gelu(z)
    u, gg = act[:, :SG_WIDTH], act[:, SG_WIDTH:]
    mu = jnp.mean(gg, axis=-1, keepdims=True)
    var = jnp.mean(jnp.square(gg - mu), axis=-1, keepdims=True)
    return u, (gg - mu) * lax.rsqrt(var + EPS) * ln_g + ln_b


SG_PAIR = 2 * SG_GROUP_DIM


def f_spatial_gate(gn, u, w, b):
    t = lax.broadcasted_iota(jnp.int32, (SG_CHUNK, SG_CHUNK), 0)
    s = lax.broadcasted_iota(jnp.int32, (SG_CHUNK, SG_CHUNK), 1)
    mixed = None
    for half in range(2):
        first = half * SG_GROUP_DIM
        take = _lane_map(SG_PAIR, SG_GROUP_DIM, lambda src, dst: jnp.where(src == dst + first, 1.0, 0.0))
        put = _lane_map(SG_GROUP_DIM, SG_PAIR, lambda src, dst: jnp.where(dst == src + first, 1.0, 0.0))
        mine = slice(half * SG_CHUNK, (half + 1) * SG_CHUNK)
        w_causal = jnp.where(s <= t, w[mine], 0.0)
        group = place(gn, take)
        m = [bdot(w_causal, group[i:i + SG_CHUNK]) + b[mine] for i in range(0, group.shape[0], SG_CHUNK)]
        m = place(m[0] if len(m) == 1 else jnp.concatenate(m, axis=0), put)
        mixed = m if mixed is None else mixed + m
    return (u * mixed,)


def _two_pieces(x):
    hi = x.astype(BF16)
    return hi, (x - hi.astype(F32)).astype(BF16)


@jax.custom_vjp
def place(x, m):
    hi, lo = _two_pieces(x)
    return _dg(hi, m, 1, 0) + _dg(lo, m, 1, 0)


def _place_fwd(x, m):
    return place(x, m), m


def _place_bwd(m, g):
    hi, lo = _two_pieces(g)
    return _dg(hi, m, 1, 1) + _dg(lo, m, 1, 1), jnp.zeros_like(m)


place.defvjp(_place_fwd, _place_bwd)


def _lane_map(rows, cols, entry):
    src = lax.broadcasted_iota(jnp.int32, (rows, cols), 0)
    dst = lax.broadcasted_iota(jnp.int32, (rows, cols), 1)
    return entry(src, dst).astype(BF16)


def _rope_tail(t, cos_w, sin_w):
    half = MLA_ROPE // 2
    lo_half = lambda d: jnp.logical_and(d >= MLA_NOPE, d < MLA_NOPE + half)
    swap = _lane_map(MLA_QK, MLA_QK, lambda s, d: jnp.where(
        jnp.logical_and(d >= MLA_NOPE + half, s == d - half), 1.0,
        jnp.where(jnp.logical_and(lo_half(d), s == d + half), -1.0, 0.0)))
    return t * cos_w + place(t, swap) * sin_w


def f_mla_q(q, cos_w, sin_w, g):
    return (_rope_tail(f_rms(q, g)[0], cos_w, sin_w),)


def f_mla_k(kv, k_r, cos_w, sin_w, g):
    width = MLA_NOPE + MLA_V
    nope = _lane_map(width, MLA_QK, lambda s, d: jnp.where(jnp.logical_and(s == d, d < MLA_NOPE), 1.0, 0.0))
    tail = _lane_map(MLA_ROPE, MLA_QK, lambda s, d: jnp.where(s + MLA_NOPE == d, 1.0, 0.0))
    value = _lane_map(width, MLA_V, lambda s, d: jnp.where(s == d + MLA_NOPE, 1.0, 0.0))
    key = _rope_tail(f_rms(place(kv, nope) + place(k_r, tail), g)[0], cos_w, sin_w)
    return key, place(kv, value)


def f_xattn(q, k, v, q_g, k_g):
    qn, kn = f_rms(q, q_g)[0], f_rms(k, k_g)[0]
    sc = bdot_nt(qn, kn) * (MEM_HEAD_DIM ** -0.5)
    return (bdot(jax.nn.softmax(sc, axis=-1), v),)


def _split_dot(x, tri, pieces=2):
    hi = x.astype(BF16)
    if pieces == 1:
        return _dg(hi, tri, 1, 0)
    lo = (x - hi.astype(F32)).astype(BF16)
    return _dg(hi, tri, 1, 0) + _dg(lo, tri, 1, 0)


def _tri(tk, cmp):
    j = lax.broadcasted_iota(jnp.int32, (tk, tk), 0)
    s = lax.broadcasted_iota(jnp.int32, (tk, tk), 1)
    return cmp(j, s).astype(BF16)


SCAN_CHUNK = 256


def _tri_spec():
    return pl.BlockSpec((SCAN_CHUNK, SCAN_CHUNK), lambda h, i: (0, 0))


def _row_scan(x, tri, reverse, pieces=2):
    n = x.shape[1] // SCAN_CHUNK
    chunks = [x[:, i * SCAN_CHUNK:(i + 1) * SCAN_CHUNK] for i in range(n)]
    out, seen = [None] * n, None
    for i in (reversed(range(n)) if reverse else range(n)):
        local = _split_dot(chunks[i], tri, pieces)
        out[i] = local if seen is None else local + seen
        total = jnp.sum(chunks[i], axis=1, keepdims=True)
        seen = total if seen is None else seen + total
    return (out[0] if n == 1 else jnp.concatenate(out, axis=1)), seen


def _att_specs(s_len, tq, dq, dv):
    q_spec = pl.BlockSpec((None, tq, dq), lambda h, i: (h, i, 0))
    k_spec = pl.BlockSpec((None, s_len, dq), lambda h, i: (h, 0, 0))
    v_spec = pl.BlockSpec((None, s_len, dv), lambda h, i: (h, 0, 0))
    o_spec = pl.BlockSpec((None, tq, dv), lambda h, i: (h, i, 0))
    r_spec = pl.BlockSpec((None, tq, 1), lambda h, i: (h, i, 0))
    return q_spec, k_spec, v_spec, o_spec, r_spec


def _key_blocks(qi, tq, tk):
    return (qi * tq) // tk, ((qi + 1) * tq + tk - 1) // tk


def _earlier(qi, j, tq, tk):
    row = qi * tq + lax.broadcasted_iota(jnp.int32, (tq, tk), 0)
    col = j * tk + lax.broadcasted_iota(jnp.int32, (tq, tk), 1)
    return col < row


LOG2_E = 1.4426950408889634


def _log2_sigmoid(z2):
    return jnp.minimum(z2, 0.0) - jnp.log2(1.0 + jnp.exp2(-jnp.abs(z2)))


def sb_fwd(q, k, v, *, tq, tk, name, job=None):
    n_heads, s_len, d = q.shape
    scale2 = SB_HEAD_DIM ** -0.5 * LOG2_E

    def body(q_ref, k_ref, v_ref, upper_ref, o_ref, tot_ref):
        qi = pl.program_id(1)
        qv = q_ref[...]
        upper = upper_ref[...]
        n_full, n_all = _key_blocks(qi, tq, tk)

        def make_step(masked, last):
            def step(jj, carry):
                acc, rest = carry
                j = last - 1 - jj
                sl = pl.ds(pl.multiple_of(j * tk, tk), tk)
                ks, vs = k_ref[sl, :], v_ref[sl, :]
                z2 = _dg(qv, ks, 1, 1) * scale2
                log_beta = _log2_sigmoid(z2)
                log_stay = log_beta - z2
                if masked:
                    valid = _earlier(qi, j, tq, tk)
                    log_stay = jnp.where(valid, log_stay, 0.0)
                after, total = _row_scan(log_stay, upper, True)
                w = jnp.exp2(log_beta + after + rest)
                if masked:
                    w = jnp.where(valid, w, 0.0)
                acc = acc + _dg(w.astype(BF16), vs, 1, 0)
                return acc, rest + total
            return step

        carry = (jnp.zeros((tq, d), F32), jnp.zeros((tq, 1), F32))
        carry = lax.fori_loop(0, n_all - n_full, make_step(True, n_all), carry)
        acc, rest = lax.fori_loop(0, n_full, make_step(False, n_full), carry)
        o_ref[...] = acc
        tot_ref[...] = rest

    q_spec, k_spec, v_spec, o_spec, r_spec = _att_specs(s_len, tq, d, d)
    return ride_call(
        job, body, name=name, grid=(n_heads, s_len // tq), in_specs=[q_spec, k_spec, v_spec, _tri_spec()],
        out_specs=[o_spec, r_spec],
        out_shape=[jax.ShapeDtypeStruct((n_heads, s_len, d), F32), jax.ShapeDtypeStruct((n_heads, s_len, 1), F32)],
        ins=[q, k, v, _tri(SCAN_CHUNK, lambda j, s: j > s)], sem=("parallel", "arbitrary"))


def sb_bwd(q, k, v, tot, do, *, tq, tk, name, job=None):
    n_heads, s_len, d = q.shape
    scale = SB_HEAD_DIM ** -0.5
    scale2 = scale * LOG2_E

    def body(q_ref, k_ref, v_ref, tot_ref, do_ref, incl_ref, excl_ref, dq_ref, dk_ref, dv_ref):
        qi = pl.program_id(1)

        @pl.when(qi == 0)
        def _():
            dk_ref[...] = jnp.zeros_like(dk_ref)
            dv_ref[...] = jnp.zeros_like(dv_ref)

        qv = q_ref[...]
        dob = do_ref[...].astype(BF16)
        total = tot_ref[...]
        incl, excl = incl_ref[...], excl_ref[...]
        n_full, n_all = _key_blocks(qi, tq, tk)

        def make_step(masked):
            def step(j, carry):
                dq, stay_before, dl_before = carry
                sl = pl.ds(pl.multiple_of(j * tk, tk), tk)
                ks, vs = k_ref[sl, :], v_ref[sl, :]
                z2 = _dg(qv, ks, 1, 1) * scale2
                log_beta = _log2_sigmoid(z2)
                log_stay = log_beta - z2
                if masked:
                    valid = _earlier(qi, j, tq, tk)
                    log_stay = jnp.where(valid, log_stay, 0.0)
                stay_upto, stay_sum = _row_scan(log_stay, incl, False)
                w = jnp.exp2(log_beta + (total - stay_before) - stay_upto)
                if masked:
                    w = jnp.where(valid, w, 0.0)
                dl = _dg(dob, vs, 1, 1) * w
                dl_upto, dl_sum = _row_scan(dl, excl, False, pieces=1)
                dl_prefix = dl_upto + dl_before
                beta = jnp.exp2(log_beta)
                dz = dl * (1.0 - beta) - beta * dl_prefix
                if masked:
                    dz = jnp.where(valid, dz, 0.0)
                dzb = dz.astype(BF16)
                dq = dq + _dg(dzb, ks, 1, 0)
                dk_ref[sl, :] += _dg(dzb, qv, 0, 0) * scale
                dv_ref[sl, :] += _dg(w.astype(BF16), dob, 0, 0)
                return dq, stay_before + stay_sum, dl_before + dl_sum
            return step

        zero = jnp.zeros((tq, 1), F32)
        carry = lax.fori_loop(0, n_full, make_step(False), (jnp.zeros((tq, d), F32), zero, zero))
        dq, _, _ = lax.fori_loop(n_full, n_all, make_step(True), carry)
        dq_ref[...] = dq * scale

    q_spec, k_spec, v_spec, o_spec, r_spec = _att_specs(s_len, tq, d, d)
    shp = jax.ShapeDtypeStruct((n_heads, s_len, d), F32)
    return ride_call(
        job, body, name=name, grid=(n_heads, s_len // tq),
        in_specs=[q_spec, k_spec, v_spec, r_spec, o_spec, _tri_spec(), _tri_spec()],
        out_specs=[q_spec, k_spec, v_spec], out_shape=[shp, shp, shp],
        ins=[q, k, v, tot, do, _tri(SCAN_CHUNK, lambda j, s: j <= s), _tri(SCAN_CHUNK, lambda j, s: j < s)],
        sem=("arbitrary", "arbitrary"))


NEG_BIG = -1e30


def _lower_left(rows, cols):
    r = lax.broadcasted_iota(jnp.int32, (rows, cols), 0)
    c = lax.broadcasted_iota(jnp.int32, (rows, cols), 1)
    return c <= r


def _prep_specs(tq, q_prep):
    cos, _, gain = q_prep
    rope_spec = pl.BlockSpec((tq, cos.shape[1]), lambda h, i: (i, 0))
    return [rope_spec, rope_spec, pl.BlockSpec(gain.shape, lambda h, i: (0, 0))]


def sm_fwd(q, k, v, *, tq, tk, name, q_prep=None):
    n_heads, s_len, dq = q.shape
    dv = v.shape[2]
    scale = dq ** -0.5
    assert tq == tk
    half = tk // 2
    n_prep = 0 if q_prep is None else 3

    def body(*refs):
        q_ref, prep_refs = refs[0], refs[1:1 + n_prep]
        k_ref, v_ref, o_ref, lse_ref = refs[1 + n_prep:]
        qi = pl.program_id(1)
        qv = q_ref[...]
        if q_prep is not None:
            qv = f_mla_q(qv, *[r[...] for r in prep_refs])[0].astype(BF16)

        def attend(carry, q_rows, keys, keep):
            acc, m, l = carry
            sc = _dg(q_rows, k_ref[keys, :], 1, 1) * scale
            if keep is not None:
                sc = jnp.where(keep, sc, NEG_BIG)
            m_new = jnp.maximum(m, jnp.max(sc, axis=1, keepdims=True))
            p = jnp.exp(sc - m_new)
            fade = jnp.exp(m - m_new)
            return (fade * acc + _dg(p.astype(BF16), v_ref[keys, :], 1, 0), m_new,
                    fade * l + jnp.sum(p, axis=1, keepdims=True))

        carry = (jnp.zeros((tq, dv), F32), jnp.full((tq, 1), NEG_BIG, F32), jnp.zeros((tq, 1), F32))
        carry = lax.fori_loop(
            0, qi, lambda j, c: attend(c, qv, pl.ds(pl.multiple_of(j * tk, tk), tk), None), carry)
        base = pl.multiple_of(qi * tk, tk)
        carry = attend(carry, qv, pl.ds(base, half), _lower_left(tq, half))
        low = attend(tuple(t[half:] for t in carry), qv[half:], pl.ds(pl.multiple_of(base + half, half), half),
                     _lower_left(half, half))
        acc, m, l = (jnp.concatenate([t[:half], u], axis=0) for t, u in zip(carry, low))
        o_ref[...] = acc / l
        lse_ref[...] = m + jnp.log(l)

    q_spec, k_spec, v_spec, o_spec, r_spec = _att_specs(s_len, tq, dq, dv)
    prep = [] if q_prep is None else list(q_prep)
    return pl.pallas_call(
        body, name=name, grid=(n_heads, s_len // tq),
        in_specs=[q_spec] + ([] if q_prep is None else _prep_specs(tq, q_prep)) + [k_spec, v_spec],
        out_specs=[o_spec, r_spec],
        out_shape=[jax.ShapeDtypeStruct((n_heads, s_len, dv), F32), jax.ShapeDtypeStruct((n_heads, s_len, 1), F32)],
        compiler_params=_cparams(("parallel", "arbitrary")),
    )(q, *prep, k, v)


def sm_bwd(q, k, v, o, lse, do, *, tq, tk, name, q_prep=None, job=None):
    n_heads, s_len, dq = q.shape
    dv = v.shape[2]
    scale = dq ** -0.5
    assert tq == tk
    half = tk // 2
    n_prep = 0 if q_prep is None else 3

    def body(*refs):
        q_ref, prep_refs = refs[0], refs[1:1 + n_prep]
        k_ref, v_ref, o_ref, lse_ref, do_ref, dq_ref, dk_ref, dv_ref = refs[1 + n_prep:9 + n_prep]
        head, qi = pl.program_id(0), pl.program_id(1)

        @pl.when(qi == 0)
        def _():
            dk_ref[...] = jnp.zeros_like(dk_ref)
            dv_ref[...] = jnp.zeros_like(dv_ref)

        q_raw = q_ref[...]
        prep_vals = [r[...] for r in prep_refs]
        qv = q_raw if q_prep is None else f_mla_q(q_raw, *prep_vals)[0].astype(BF16)
        do = do_ref[...]
        dob = do.astype(BF16)
        delta = jnp.sum(do * o_ref[...], axis=1, keepdims=True)
        lse_v = lse_ref[...]

        def attend(rows, keys, keep):
            ks, vs = k_ref[keys, :], v_ref[keys, :]
            p = jnp.exp(_dg(qv[rows], ks, 1, 1) * scale - lse_v[rows])
            if keep is not None:
                p = jnp.where(keep, p, 0.0)
            dv_ref[keys, :] += _dg(p.astype(BF16), dob[rows], 0, 0)
            ds = (p * (_dg(dob[rows], vs, 1, 1) - delta[rows]) * scale).astype(BF16)
            dk_ref[keys, :] += _dg(ds, qv[rows], 0, 0)
            return _dg(ds, ks, 1, 0)

        everything = slice(None)
        dq_acc = lax.fori_loop(
            0, qi, lambda j, acc: acc + attend(everything, pl.ds(pl.multiple_of(j * tk, tk), tk), None),
            jnp.zeros((tq, dq), F32))
        base = pl.multiple_of(qi * tk, tk)
        dq_acc = dq_acc + attend(everything, pl.ds(base, half), _lower_left(tq, half))
        low = attend(slice(half, None), pl.ds(pl.multiple_of(base + half, half), half), _lower_left(half, half))
        dq_acc = jnp.concatenate([dq_acc[:half], dq_acc[half:] + low], axis=0)
        if q_prep is None:
            dq_ref[...] = dq_acc
        else:
            cos, sin, gain = prep_vals
            _, pull = jax.vjp(lambda t, g: f_mla_q(t, cos, sin, g)[0], q_raw, gain)
            dq_raw, d_gain = pull(dq_acc)
            dq_ref[...] = dq_raw.astype(dq_ref.dtype)
            dgain_ref = refs[9 + n_prep]
            first = jnp.logical_and(head == 0, qi == 0)

            @pl.when(first)
            def _():
                dgain_ref[...] = d_gain

            @pl.when(jnp.logical_not(first))
            def _():
                dgain_ref[...] += d_gain

    q_spec, k_spec, v_spec, o_spec, r_spec = _att_specs(s_len, tq, dq, dv)
    out_specs = [q_spec, k_spec, v_spec]
    out_shape = [jax.ShapeDtypeStruct((n_heads, s_len, dq), F32 if q_prep is None else BF16),
                 jax.ShapeDtypeStruct((n_heads, s_len, dq), F32), jax.ShapeDtypeStruct((n_heads, s_len, dv), F32)]
    prep, prep_specs = [], []
    if q_prep is not None:
        prep, prep_specs = list(q_prep), _prep_specs(tq, q_prep)
        out_specs.append(prep_specs[2])
        out_shape.append(jax.ShapeDtypeStruct(q_prep[2].shape, F32))
    return ride_call(
        job, body, name=name, grid=(n_heads, s_len // tq),
        in_specs=[q_spec] + prep_specs + [k_spec, v_spec, o_spec, r_spec, o_spec], out_specs=out_specs,
        out_shape=out_shape, ins=[q, *prep, k, v, o, lse, do], sem=("arbitrary", "arbitrary"))


def loss_head(y, target, *, tm, name):
    n_rows, width = y.shape

    def body(y_ref, t_ref, dy_ref, loss_ref):
        diff = y_ref[...] - t_ref[...]
        dy_ref[...] = diff / width
        part = 0.5 * jnp.sum(jnp.mean(diff * diff, axis=-1, keepdims=True), axis=0, keepdims=True)

        @pl.when(pl.program_id(0) == 0)
        def _():
            loss_ref[...] = jnp.zeros_like(loss_ref)

        loss_ref[...] += jnp.broadcast_to(part, loss_ref.shape)

    spec = pl.BlockSpec((tm, width), lambda r: (r, 0))
    dy, loss = pl.pallas_call(
        body, name=name, grid=(n_rows // tm,), in_specs=[spec, spec],
        out_specs=[spec, pl.BlockSpec((8, LANES), lambda r: (0, 0))],
        out_shape=[jax.ShapeDtypeStruct(y.shape, F32), jax.ShapeDtypeStruct((8, LANES), F32)],
        compiler_params=_cparams(("arbitrary",)),
    )(y, target)
    return dy, loss[0, 0]


ADAM_TILE_ELEMS = 256 * 1024


def _adam_rows(n_rows, width):
    fits = [t for t in range(16, n_rows + 1, 16) if n_rows % t == 0 and t * width <= ADAM_TILE_ELEMS]
    return max(fits) if fits else n_rows


def adamw(parts, w, m, v, *, name):
    n_layers, n_rows, width = w.shape
    assert len(parts) == n_layers
    tm = _adam_rows(n_rows, width)
    n_tiles = n_rows // tm

    def body(*refs):
        p_refs = refs[:n_layers]
        w_ref, m_ref, v_ref, g_ref, d_ref, nm_ref, nv_ref = refs[n_layers:]
        layer = pl.program_id(0)
        for this, p_ref in enumerate(p_refs):
            @pl.when(layer == this)
            def _(p_ref=p_ref):
                g = p_ref[0].astype(F32)
                for i in range(1, N_DEV):
                    g = g + p_ref[i].astype(F32)
                m_new = ADAM_B1 * m_ref[...] + (1.0 - ADAM_B1) * g
                v_new = ADAM_B2 * v_ref[...] + (1.0 - ADAM_B2) * jnp.square(g)
                m_hat = m_new / (1.0 - ADAM_B1 ** ADAM_STEP)
                v_hat = v_new / (1.0 - ADAM_B2 ** ADAM_STEP)
                g_ref[...] = g
                d_ref[...] = -ADAM_LR * (m_hat / (jnp.sqrt(v_hat) + ADAM_EPS) + ADAM_WD * w_ref[...])
                nm_ref[...] = m_new
                nv_ref[...] = v_new

    def part_spec(this):
        def index(layer, r):
            return 0, jnp.where(layer == this, r, jnp.where(layer < this, 0, n_tiles - 1)), 0
        return pl.BlockSpec((N_DEV, tm, width), index)

    spec = pl.BlockSpec((None, tm, width), lambda layer, r: (layer, r, 0))
    shp = jax.ShapeDtypeStruct(w.shape, F32)
    return pl.pallas_call(
        body, name=name, grid=(n_layers, n_tiles),
        in_specs=[part_spec(this) for this in range(n_layers)] + [spec, spec, spec],
        out_specs=[spec] * 4, out_shape=[shp] * 4, compiler_params=_cparams(("arbitrary", "arbitrary")),
    )(*parts, w, m, v)


def _me():
    return lax.axis_index("x"), lax.axis_index("y"), lax.axis_index("c")


N_PEERS = N_DEV - 1


class CommJob:
    def __init__(self, kind, arrays):
        self.kind, self.arrays, self.n = kind, list(arrays), len(arrays)

    def out_shape(self):
        lead = (N_DEV,) if self.kind == 'gather' else ()
        return [jax.ShapeDtypeStruct(lead + a.shape, a.dtype) for a in self.arrays]

    def scratch(self):
        return [pltpu.SemaphoreType.DMA((N_PEERS * self.n,)), pltpu.SemaphoreType.DMA((N_PEERS * self.n,)),
                pltpu.SemaphoreType.DMA((self.n,))]

    def phases(self, in_refs, out_refs, send_sems, recv_sems, local_sems):
        n = self.n
        x, y, c = _me()

        def remote(i, k, src, dst, to):
            return pltpu.make_async_remote_copy(
                src_ref=src, dst_ref=dst, send_sem=send_sems.at[N_PEERS * i + k],
                recv_sem=recv_sems.at[N_PEERS * i + k], device_id=to, device_id_type=MESH)

        if self.kind == 'gather':
            me, sibling = (x, y, c), (x, y, 1 - c)
            chips = [(1 - x, y), (x, 1 - y), (1 - x, 1 - y)]

            def slot(i, px, py, pc):
                return out_refs[i].at[4 * px + 2 * py + pc]

            def copy(i, k, blk, to, src=None):
                return remote(i, k, slot(i, *blk) if src is None else src, slot(i, *blk), to)

            def mine():
                return [pltpu.make_async_copy(in_refs[i], slot(i, *me), local_sems.at[i]) for i in range(n)]

            def first():
                cps = []
                for i in range(n):
                    cps.append(copy(i, 0, me, sibling, src=in_refs[i]))
                    cps += [copy(i, 1 + j, me, (*chip, c), src=in_refs[i]) for j, chip in enumerate(chips)]
                return cps

            def passed():
                return [copy(i, 4 + j, (*chip, c), sibling) for j, chip in enumerate(chips) for i in range(n)]

            def start():
                for cp in mine() + first():
                    cp.start()

            def forward():
                for j, chip in enumerate(chips):
                    for i in range(n):
                        copy(i, 1 + j, (*chip, c), me).wait_recv()
                        copy(i, 4 + j, (*chip, c), sibling).start()

            def finish():
                for i in range(n):
                    copy(i, 0, sibling, me).wait_recv()
                    for j, chip in enumerate(chips):
                        copy(i, 4 + j, (*chip, 1 - c), me).wait_recv()
                for cp in first() + passed():
                    cp.wait_send()
                for cp in mine():
                    cp.wait()

            return start, forward, finish

        my_slot = 4 * x + 2 * y + c

        def mine():
            return [pltpu.make_async_copy(in_refs[i].at[my_slot], out_refs[i].at[my_slot], local_sems.at[i])
                    for i in range(n)]

        def copies():
            cps = []
            for k in range(1, N_DEV):
                px, py, pc = x ^ (k >> 2), y ^ ((k >> 1) & 1), c ^ (k & 1)
                cps += [remote(i, k - 1, in_refs[i].at[4 * px + 2 * py + pc], out_refs[i].at[my_slot], (px, py, pc))
                        for i in range(n)]
            return cps

        def start():
            for cp in mine() + copies():
                cp.start()

        def finish():
            for cp in copies():
                cp.wait_recv()
            for cp in copies():
                cp.wait_send()
            for cp in mine():
                cp.wait()

        return start, (lambda: None), finish


def comm_call(kind, arrays, *, name):
    job = CommJob(kind, arrays)
    n = job.n

    def body(*refs):
        start, forward, finish = job.phases(refs[:n], refs[n:2 * n], *refs[2 * n:])
        start()
        forward()
        finish()

    hbm = pl.BlockSpec(memory_space=pl.ANY)
    return pl.pallas_call(body, name=name, out_shape=job.out_shape(), in_specs=[hbm] * n, out_specs=[hbm] * n,
                          scratch_shapes=job.scratch())(*job.arrays)


def ride_call(job, compute, *, name, grid, in_specs, out_specs, out_shape, ins, sem, scratch=()):
    scratch = list(scratch)
    if job is None:
        return pl.pallas_call(compute, name=name, grid=grid, in_specs=in_specs, out_specs=out_specs,
                              out_shape=out_shape, scratch_shapes=scratch, compiler_params=_cparams(sem))(*ins), None
    n, n_in, n_out, n_scr = job.n, len(ins), len(out_shape), len(scratch)
    n_steps = 1
    for g in grid:
        n_steps *= g

    def body(*refs):
        ins_, job_ins = refs[:n_in], refs[n_in:n_in + n]
        outs, job_outs = refs[n_in + n:n_in + n + n_out], refs[n_in + n + n_out:n_in + 2 * n + n_out]
        rest = refs[n_in + 2 * n + n_out:]
        start, forward, finish = job.phases(job_ins, job_outs, *rest[n_scr:])
        now = 0
        for axis, g in enumerate(grid):
            now = now * g + pl.program_id(axis)
        pl.when(now == 0)(start)
        pl.when(now == n_steps // 2)(forward)
        compute(*ins_, *outs, *rest[:n_scr])
        pl.when(now == n_steps - 1)(finish)

    hbm = pl.BlockSpec(memory_space=pl.ANY)
    res = pl.pallas_call(
        body, name=name, grid=grid, in_specs=list(in_specs) + [hbm] * n,
        out_specs=list(out_specs) + [hbm] * n, out_shape=list(out_shape) + job.out_shape(),
        scratch_shapes=scratch + job.scratch(), compiler_params=_cparams(("arbitrary",) * len(grid)),
    )(*ins, *job.arrays)
    return res[:n_out], res[n_out:]


def to_heads(t, n_heads):
    s_len = t.shape[0]
    return t.reshape(s_len, n_heads, -1).transpose(1, 0, 2)


def from_heads(t):
    return t.transpose(1, 0, 2).reshape(t.shape[1], -1)


def gathered_to_full(t, axis):
    shp = t.shape[1:]
    return jnp.moveaxis(t, 0, axis).reshape(shp[:axis] + (N_DEV * shp[axis],) + shp[axis + 1:])


def full_to_owner_major(g, axis):
    shp = g.shape
    t = jnp.moveaxis(g.reshape(shp[:axis] + (N_DEV, shp[axis] // N_DEV) + shp[axis + 1:]), axis, 0)
    return t.reshape(N_DEV, -1, t.shape[-1])


def _small_rows(shape):
    n = 1
    for s in shape:
        n *= s
    return -(-n // LANES)


def pack_small(arrs, shapes):
    pieces = []
    for n in SMALL:
        flat = arrs[n].reshape(-1)
        flat = jnp.pad(flat, (0, _small_rows(shapes[n]) * LANES - flat.shape[0]))
        pieces.append(flat.reshape(-1, LANES))
    flat = jnp.concatenate(pieces, axis=0)
    return jnp.pad(flat, ((0, -flat.shape[0] % SMALL_ROW_MULTIPLE), (0, 0)))


def unpack_small(flat, shapes):
    out, r = {}, 0
    for n in SMALL:
        rows = _small_rows(shapes[n])
        size = 1
        for s in shapes[n]:
            size *= s
        out[n] = flat[r:r + rows].reshape(-1)[:size].reshape(shapes[n])
        r += rows
    return out


ROW_TM = 256
XATT_TM = 1024
HEAD_TM = 1024
SG_TM = 8 * SG_CHUNK
SB_TILES = (512, 512)
SM_TILE = 1024


def _norm_fwd(x, g, name):
    return prow(f_rms, [x], params=[g.reshape(1, -1)], outs=[(x.shape[1], BF16, False)], tm=ROW_TM, name=name)[0]


def _norm_bwd(x, g, dh, add, name, want_row=True):
    res = prow_vjp(f_rms, [x], params=[g.reshape(1, -1)], cts=[dh], row_grad=[want_row],
                   adds=[add] if want_row else None, tm=ROW_TM, name=name)
    return (res[0], res[1].reshape(-1)) if want_row else (None, res[0].reshape(-1))


def _out_proj(a, w, x, next_gain, alpha, name):
    if next_gain is None:
        return pmm(a, w, res=x, alpha=alpha, name=name), None
    return pmm(a, w, res=x, alpha=alpha, norm_out=next_gain.reshape(1, -1), name=name)


def _in_proj_bwd(d, w, x, gain, dy, name, **kw):
    tk = _pick(w.shape[1])
    if tk < w.shape[1]:
        w, kw = w.reshape(w.shape[0], -1, tk).transpose(1, 0, 2), dict(kw, b_tiles=True)
    dx, g_gain = pmm(d, w, tb=True, norm_bwd=(x, gain.reshape(1, -1), dy), name=name, **kw)
    return dx, g_gain.reshape(-1)


def ffn_fwd(x, h, p, tag, next_gain, job=None, after_job=None):
    (gate, up, act), landed = ffn_gate_up(h, p['w_gu'], name=f"{tag}_gu", job=job)
    if job is not None:
        after_job(landed)
    out = _out_proj(act, p['w_down'], x, next_gain, 0.5, f"{tag}_down")
    return out, (x, h, gate, up, act)


def _no_rider(run, **own):
    return run(None)[0]


def _pmm_pair(*args, job, **kw):
    out = pmm(*args, job=job, **kw)
    return out if job is not None else (out, None)


def ffn_bwd(dy, p, saved, tag, with_job=_no_rider):
    x, h, gate, up, act = saved
    d_gate, d_up = with_job(lambda job: ffn_gate_up_bwd(dy, p['w_down'], gate, up, alpha=0.5, name=f"{tag}_dact",
                                                        job=job))
    g_down = pmm(act, dy, ta=True, out_dtype=GRAD_WIRE, alpha=0.5, name=f"{tag}_gdown")
    g_gate = with_job(lambda job: _pmm_pair(h, d_gate, ta=True, out_dtype=GRAD_WIRE, name=f"{tag}_ggate", job=job),
                      w_down=g_down)
    g_gu = jnp.concatenate([g_gate, pmm(h, d_up, ta=True, out_dtype=GRAD_WIRE, name=f"{tag}_gup")], axis=1)
    tk = _pick(d_gate.shape[1])
    w_gu_tiles = p['w_gu'].reshape(p['w_gu'].shape[0], -1, tk).transpose(1, 0, 2)
    dx, g_norm = with_job(
        lambda job: _pmm_pair(d_gate, w_gu_tiles, a2=d_up, tb=True, b_tiles=True,
                              norm_bwd=(x, p['norm'].reshape(1, -1), dy), name=f"{tag}_dh", job=job), w_gu=g_gu)
    return dx, {'norm': g_norm.reshape(-1), 'w_gu': g_gu, 'w_down': g_down}


def even_mixer_fwd(x, h, p, next_gain, job=None, after_job=None):
    qkv = pmm(h, p['w_in'][:, :3 * SB_WIDTH], out_dtype=BF16, name="sbg_in_qkv")
    z = pmm(h, p['w_in'][:, 3 * SB_WIDTH:], name="sbg_in_gate")
    q, k, v = (to_heads(qkv[:, i * SB_WIDTH:(i + 1) * SB_WIDTH], SB_HEADS) for i in range(3))
    (o_sb, tot), landed = sb_fwd(q, k, v, tq=SB_TILES[0], tk=SB_TILES[1], name="sb_fwd", job=job)
    if job is not None:
        after_job(landed)
    ln_g, ln_b = p['ln_gain'].reshape(1, -1), p['ln_bias'].reshape(1, -1)
    u, gn = prow(f_gate_prep, [z], params=[ln_g, ln_b], outs=[(SG_WIDTH, F32, False)] * 2, tm=ROW_TM,
                 name="sgu_prep")
    gn_p, u_p = ColGroups(gn, SG_PAIR), ColGroups(u, SG_PAIR)
    w_p = p['sgu_w'].reshape(SG_GROUPS // 2, 2 * SG_CHUNK, SG_CHUNK)
    b_p = p['sgu_b'].reshape(SG_GROUPS // 2, 2 * SG_CHUNK, 1)
    o_sg = prow(f_spatial_gate, [gn_p, u_p], gparams=[w_p, b_p], outs=[(SG_PAIR, F32, 'cols')], tm=SG_TM,
                name="sgu_mix")[0]
    cat = jnp.concatenate([from_heads(o_sb), o_sg], axis=-1).astype(BF16)
    out = _out_proj(cat, p['w_out'], x, next_gain, 1.0, "sbg_out")
    return out, (x, h, q, k, v, tot, z, gn_p, u_p, w_p, b_p, cat)


def even_mixer_bwd(dy, p, saved, job_of=None):
    x, h, q, k, v, tot, z, gn_p, u_p, w_p, b_p, cat = saved
    d_osb = to_heads(pmm(dy, p['w_out'][:SB_WIDTH], tb=True, name="sbg_dcat_sb"), SB_HEADS)
    d_osg = ColGroups(pmm(dy, p['w_out'][SB_WIDTH:], tb=True, name="sbg_dcat_sg"), SG_PAIR)
    g_out = pmm(cat, dy, ta=True, out_dtype=GRAD_WIRE, name="sbg_gout")
    d_gn, d_u, g_w, g_b = prow_vjp(f_spatial_gate, [gn_p, u_p], gparams=[w_p, b_p], cts=[d_osg],
                                   row_grad=[True, True], tm=SG_TM, name="sgu_dmix")
    ln_g, ln_b = p['ln_gain'].reshape(1, -1), p['ln_bias'].reshape(1, -1)
    d_z, g_lng, g_lnb = prow_vjp(f_gate_prep, [z], params=[ln_g, ln_b], cts=[d_u, d_gn], row_grad=[True],
                                 row_dtypes=[BF16], tm=ROW_TM, name="sgu_dprep")
    job = None if job_of is None else job_of({'w_out': g_out})
    (dq, dk, dv), landed = sb_bwd(q, k, v, tot, d_osb, tq=SB_TILES[0], tk=SB_TILES[1], name="sb_bwd", job=job)
    d_proj = jnp.concatenate([from_heads(dq).astype(BF16), from_heads(dk).astype(BF16), from_heads(dv).astype(BF16),
                              d_z], axis=-1)
    g_in = pmm(h, d_proj, ta=True, out_dtype=GRAD_WIRE, name="sbg_gin")
    dx, g_norm = _in_proj_bwd(d_proj, p['w_in'], x, p['norm'], dy, "sbg_dh")
    return dx, {'norm': g_norm, 'w_in': g_in, 'ln_gain': g_lng.reshape(-1), 'ln_bias': g_lnb.reshape(-1),
                'sgu_w': g_w.reshape(SG_GROUPS, SG_CHUNK, SG_CHUNK), 'sgu_b': g_b.reshape(SG_GROUPS, SG_CHUNK),
                'w_out': g_out}, landed


def mla_fwd(x, h, cos, sin, p, next_gain):
    lora = MLA_Q_LORA + MLA_KV_LORA
    c_q = pmm(h, p['w_in'][:, :MLA_Q_LORA], name="mla_in_q")
    c_kv = pmm(h, p['w_in'][:, MLA_Q_LORA:lora], name="mla_in_kv")
    k_r = pmm(h, p['w_in'][:, lora:], name="mla_in_rope")
    cqn = _norm_fwd(c_q, p['q_lora_gain'], "mla_qlora_norm")
    ckvn = _norm_fwd(c_kv, p['kv_lora_gain'], "mla_kvlora_norm")
    q_h = to_heads(pmm(cqn, p['w_uq'], name="mla_uq"), MLA_HEADS)
    kv_h = pmm(ckvn, p['w_ukv'], out_heads=MLA_NOPE + MLA_V, name="mla_ukv")
    q_g, k_g = p['q_gain'].reshape(1, -1), p['k_gain'].reshape(1, -1)
    kp, v = prow(f_mla_k, [kv_h, k_r, cos, sin], params=[k_g], outs=[(MLA_QK, BF16, True), (MLA_V, BF16, True)],
                 tm=HEAD_TM, name="mla_kprep")
    o, lse = sm_fwd(q_h, kp, v, tq=SM_TILE, tk=SM_TILE, name="mla_att_fwd", q_prep=(cos, sin, q_g))
    o_flat = from_heads(o).astype(BF16)
    out = _out_proj(o_flat, p['w_out'], x, next_gain, 1.0, "mla_out")
    return out, (x, h, c_q, c_kv, k_r, cqn, ckvn, q_h, kv_h, v, kp, o, lse, o_flat, q_g, k_g)


def mla_bwd(dy, cos, sin, p, saved, job_of=None):
    x, h, c_q, c_kv, k_r, cqn, ckvn, q_h, kv_h, v, kp, o, lse, o_flat, q_g, k_g = saved
    do = to_heads(pmm(dy, p['w_out'], tb=True, name="mla_do"), MLA_HEADS)
    g_out = pmm(o_flat, dy, ta=True, out_dtype=GRAD_WIRE, name="mla_gout")
    job = None if job_of is None else job_of({'w_out': g_out})
    (dq_h, dkp, dv, g_qg), landed = sm_bwd(q_h, kp, v, o, lse, do, tq=SM_TILE, tk=SM_TILE, name="mla_att_bwd",
                                           q_prep=(cos, sin, q_g), job=job)
    d_kv_h, dk_r, g_kg = prow_vjp(f_mla_k, [kv_h, k_r, cos, sin], params=[k_g], cts=[dkp, dv],
                                  row_grad=[True, True, False, False], row_dtypes=[BF16, F32], tm=HEAD_TM,
                                  name="mla_dkprep")
    d_q = from_heads(dq_h)
    d_kv = from_heads(d_kv_h)
    g_uq = pmm(cqn, d_q, ta=True, out_dtype=GRAD_WIRE, name="mla_guq")
    d_cqn = pmm(d_q, p['w_uq'], tb=True, name="mla_dcqn")
    g_ukv = pmm(ckvn, d_kv, ta=True, out_dtype=GRAD_WIRE, name="mla_gukv")
    d_ckvn = pmm(d_kv, p['w_ukv'], tb=True, name="mla_dckvn")
    d_cq, g_qlora = _norm_bwd(c_q, p['q_lora_gain'], d_cqn, None, "mla_dqlora_norm")
    d_ckv, g_kvlora = _norm_bwd(c_kv, p['kv_lora_gain'], d_ckvn, None, "mla_dkvlora_norm")
    d_proj = jnp.concatenate([d_cq, d_ckv, dk_r], axis=-1).astype(BF16)
    g_in = pmm(h, d_proj, ta=True, out_dtype=GRAD_WIRE, name="mla_gin")
    dx, g_norm = _in_proj_bwd(d_proj, p['w_in'], x, p['norm'], dy, "mla_dh")
    return dx, {'norm': g_norm, 'w_in': g_in, 'q_lora_gain': g_qlora, 'kv_lora_gain': g_kvlora, 'w_uq': g_uq,
                'w_ukv': g_ukv, 'q_gain': g_qg.reshape(-1), 'k_gain': g_kg.reshape(-1), 'w_out': g_out}, landed


def xattn_fwd(x, hq, mem, p, tag, next_gain):
    hm = _norm_fwd(mem, p['mem_norm'], f"{tag}_mem_norm")
    q_h = ColGroups(pmm(hq, p['wq'], name=f"{tag}_q"), MEM_HEAD_DIM)
    kv = pmm(hm, p['wkv'], name=f"{tag}_kv").reshape(mem.shape[0], MEM_HEADS, 2 * MEM_HEAD_DIM).transpose(1, 0, 2)
    k_h, v_h = kv[..., :MEM_HEAD_DIM], kv[..., MEM_HEAD_DIM:]
    q_g, k_g = p['q_gain'].reshape(1, -1), p['k_gain'].reshape(1, -1)
    o_flat = prow(f_xattn, [q_h], gparams=[k_h, v_h], params=[q_g, k_g], outs=[(MEM_HEAD_DIM, BF16, 'cols')],
                  tm=XATT_TM, name=f"{tag}_att")[0]
    out = _out_proj(o_flat, p['wo'], x, next_gain, 1.0, f"{tag}_out")
    return out, (x, mem, hq, hm, q_h, k_h, v_h, q_g, k_g, o_flat)


def xattn_bwd(dy, p, saved, tag):
    x, mem, hq, hm, q_h, k_h, v_h, q_g, k_g, o_flat = saved
    d_o = ColGroups(pmm(dy, p['wo'], tb=True, name=f"{tag}_do"), MEM_HEAD_DIM)
    g_wo = pmm(o_flat, dy, ta=True, out_dtype=GRAD_WIRE, name=f"{tag}_gwo")
    d_q, dk_h, dv_h, g_qg, g_kg = prow_vjp(f_xattn, [q_h], gparams=[k_h, v_h], params=[q_g, k_g], cts=[d_o],
                                           row_grad=[True], row_dtypes=[BF16], tm=XATT_TM, name=f"{tag}_datt")
    d_kv = jnp.concatenate([dk_h, dv_h], axis=-1).transpose(1, 0, 2).reshape(mem.shape[0], -1).astype(BF16)
    g_wq = pmm(hq, d_q, ta=True, out_dtype=GRAD_WIRE, name=f"{tag}_gwq")
    dx, g_norm = _in_proj_bwd(d_q, p['wq'], x, p['norm'], dy, f"{tag}_dhq")
    g_wkv = pmm(hm, d_kv, ta=True, out_dtype=GRAD_WIRE, name=f"{tag}_gwkv")
    dhm = pmm(d_kv, p['wkv'], tb=True, name=f"{tag}_dhm")
    _, g_mem_norm = _norm_bwd(mem, p['mem_norm'], dhm, None, f"{tag}_dmem_norm", want_row=False)
    return dx, {'norm': g_norm, 'mem_norm': g_mem_norm, 'wq': g_wq, 'wkv': g_wkv, 'q_gain': g_qg.reshape(-1),
                'k_gain': g_kg.reshape(-1), 'wo': g_wo}


def rope_tables(positions):
    half = MLA_ROPE // 2
    inv_freq = ROPE_THETA ** (-jnp.arange(half, dtype=F32) / half)
    ang = positions.astype(F32)[:, None] * inv_freq
    cos, sin = jnp.cos(ang), jnp.sin(ang)
    lead = jnp.ones((ang.shape[0], MLA_NOPE), F32)
    return jnp.concatenate([lead, cos, cos], axis=1), jnp.concatenate([0.0 * lead, sin, sin], axis=1)


FIRST_UNIT = ('ffn_pre_w_gu', 0)
EARLY_UNITS = [('ffn_pre_w_down', 0), ('sbg_w_in', 0)]


def local_step(x, mem, positions, target, w, shards):
    cos, sin = rope_tables(positions)
    full = {}

    def absorb(units, gathered):
        for (n, layer), t in zip(units, gathered):
            full[(n, layer)] = gathered_to_full(t, BIG[n] - 1)

    late_units = [u for u in shards if u != FIRST_UNIT and u not in EARLY_UNITS]
    (h,), gathered = prow(f_rms, [x], params=[w['ffn_pre_norm'][0].reshape(1, -1)], outs=[(x.shape[1], BF16, False)],
                          tm=ROW_TM, name="ffn_pre0_norm", job=CommJob('gather', [shards[FIRST_UNIT]]))
    absorb([FIRST_UNIT], gathered)
    first_ffn_p = {'norm': w['ffn_pre_norm'][0], 'w_gu': full[FIRST_UNIT]}

    def ffn_params(kind, layer):
        return {'norm': w[f'ffn_{kind}_norm'][layer], 'w_gu': full[(f'ffn_{kind}_w_gu', layer)],
                'w_down': full[(f'ffn_{kind}_w_down', layer)]}

    def xattn_params(layer):
        return {'norm': w['xmem_norm'][layer], 'mem_norm': w['xmem_mem_norm'][layer], 'wq': full[('xmem_wq', layer)],
                'wkv': full[('xmem_wkv', layer)], 'q_gain': w['xmem_q_gain'][layer], 'k_gain': w['xmem_k_gain'][layer],
                'wo': full[('xmem_wo', layer)]}

    even_p = {'norm': w['mix_norm'][0], 'ln_gain': w['sgu_ln_gain'][0], 'ln_bias': w['sgu_ln_bias'][0],
              'sgu_w': w['sgu_w'][0], 'sgu_b': w['sgu_b'][0]}

    def early_weights_landed(gathered):
        absorb(EARLY_UNITS, gathered)
        first_ffn_p['w_down'] = full[('ffn_pre_w_down', 0)]
        even_p['w_in'] = full[('sbg_w_in', 0)]

    def late_weights_landed(gathered):
        absorb(late_units, gathered)
        even_p['w_out'] = full[('sbg_w_out', 0)]

    def mla_params():
        return {'norm': w['mix_norm'][1], 'w_in': full[('mla_w_in', 0)], 'q_lora_gain': w['mla_q_lora_gain'][0],
                'kv_lora_gain': w['mla_kv_lora_gain'][0], 'w_uq': full[('mla_w_uq', 0)],
                'w_ukv': full[('mla_w_ukv', 0)], 'q_gain': w['mla_q_gain'][0], 'k_gain': w['mla_k_gain'][0],
                'w_out': full[('mla_w_out', 0)]}

    saved = []
    for layer in range(DEPTH):
        if layer == 0:
            (x, h), s_pre = ffn_fwd(x, h, first_ffn_p, "ffn_pre0", w['mix_norm'][0],
                                    job=CommJob('gather', [shards[u] for u in EARLY_UNITS]),
                                    after_job=early_weights_landed)
        else:
            (x, h), s_pre = ffn_fwd(x, h, ffn_params('pre', layer), f"ffn_pre{layer}", w['mix_norm'][layer])
        if layer % 2 == 0:
            (x, h), s_mix = even_mixer_fwd(x, h, even_p, w['xmem_norm'][layer],
                                           job=CommJob('gather', [shards[u] for u in late_units]),
                                           after_job=late_weights_landed)
        else:
            (x, h), s_mix = mla_fwd(x, h, cos, sin, mla_params(), w['xmem_norm'][layer])
        (x, h), s_x = xattn_fwd(x, h, mem, xattn_params(layer), f"xmem{layer}", w['ffn_post_norm'][layer])
        following = w['ffn_pre_norm'][layer + 1] if layer + 1 < DEPTH else None
        (x, h), s_post = ffn_fwd(x, h, ffn_params('post', layer), f"ffn_post{layer}", following)
        saved.append((s_pre, s_mix, s_x, s_post))

    dx, loss = loss_head(x, target, tm=ROW_TM, name="loss_head")

    ready, riding, landed = {}, [], {}

    def offer(name, layer, g):
        ready[(name, layer)] = full_to_owner_major(g, BIG[name] - 1)

    def ride(name):
        def job_of(own):
            offer(name, 0, own['w_out'])
            riding[:] = list(ready)
            return CommJob('exchange', [ready.pop(u) for u in riding])
        return job_of

    def last_rides(run, **own):
        for kind, g in own.items():
            offer('ffn_pre_' + kind, 0, g)
        units = list(ready)
        if not units:
            return run(None)[0]
        res, arrived = run(CommJob('exchange', [ready.pop(u) for u in units]))
        landed.update(zip(units, arrived))
        return res

    per_layer = []
    for layer in reversed(range(DEPTH)):
        s_pre, s_mix, s_x, s_post = saved[layer]
        dx, g_post = ffn_bwd(dx, ffn_params('post', layer), s_post, f"ffn_post{layer}")
        offer('ffn_post_w_gu', layer, g_post['w_gu'])
        offer('ffn_post_w_down', layer, g_post['w_down'])
        dx, g_x = xattn_bwd(dx, xattn_params(layer), s_x, f"xmem{layer}")
        for n in ('wq', 'wkv', 'wo'):
            offer('xmem_' + n, layer, g_x[n])
        if layer % 2 == 0:
            dx, g_mix, arrived = even_mixer_bwd(dx, even_p, s_mix, job_of=ride('sbg_w_out'))
            landed.update(zip(riding, arrived))
            offer('sbg_w_in', 0, g_mix['w_in'])
        else:
            dx, g_mix, arrived = mla_bwd(dx, cos, sin, mla_params(), s_mix, job_of=ride('mla_w_out'))
            landed.update(zip(riding, arrived))
            for n in ('w_in', 'w_uq', 'w_ukv'):
                offer('mla_' + n, 0, g_mix[n])
        if layer == 0:
            dx, g_pre = ffn_bwd(dx, ffn_params('pre', layer), s_pre, f"ffn_pre{layer}", with_job=last_rides)
        else:
            dx, g_pre = ffn_bwd(dx, ffn_params('pre', layer), s_pre, f"ffn_pre{layer}")
            offer('ffn_pre_w_gu', layer, g_pre['w_gu'])
            offer('ffn_pre_w_down', layer, g_pre['w_down'])
        per_layer.append((layer, g_pre, g_mix, g_x, g_post))
    per_layer.sort(key=lambda t: t[0])
    assert not ready

    def stack(pick):
        return jnp.stack([pick(t) for t in per_layer])

    g_even, g_mla = per_layer[0][2], per_layer[1][2]
    small_grads = {
        'ffn_pre_norm': stack(lambda t: t[1]['norm']), 'mix_norm': stack(lambda t: t[2]['norm']),
        'sgu_ln_gain': g_even['ln_gain'][None], 'sgu_ln_bias': g_even['ln_bias'][None],
        'sgu_w': g_even['sgu_w'][None], 'sgu_b': g_even['sgu_b'][None],
        'mla_q_lora_gain': g_mla['q_lora_gain'][None], 'mla_kv_lora_gain': g_mla['kv_lora_gain'][None],
        'mla_q_gain': g_mla['q_gain'][None], 'mla_k_gain': g_mla['k_gain'][None],
        'xmem_norm': stack(lambda t: t[3]['norm']), 'xmem_mem_norm': stack(lambda t: t[3]['mem_norm']),
        'xmem_q_gain': stack(lambda t: t[3]['q_gain']), 'xmem_k_gain': stack(lambda t: t[3]['k_gain']),
        'ffn_post_norm': stack(lambda t: t[4]['norm']),
    }
    return loss, dx, small_grads, landed


def _device_slot():
    x, y, c = _me()
    return 4 * x + 2 * y + c


def kernel(x, mem, positions, ffn_pre_norm, ffn_pre_w_gu, ffn_pre_w_down, mix_norm, sbg_w_in, sgu_ln_gain, sgu_ln_bias, sgu_w, sgu_b, sbg_w_out, mla_w_in, mla_q_lora_gain, mla_kv_lora_gain, mla_w_uq, mla_w_ukv, mla_q_gain, mla_k_gain, mla_w_out, xmem_norm, xmem_mem_norm, xmem_wq, xmem_wkv, xmem_q_gain, xmem_k_gain, xmem_wo, ffn_post_norm, ffn_post_w_gu, ffn_post_w_down, loss_target, m_ffn_pre_norm, m_ffn_pre_w_gu, m_ffn_pre_w_down, m_mix_norm, m_sbg_w_in, m_sgu_ln_gain, m_sgu_ln_bias, m_sgu_w, m_sgu_b, m_sbg_w_out, m_mla_w_in, m_mla_q_lora_gain, m_mla_kv_lora_gain, m_mla_w_uq, m_mla_w_ukv, m_mla_q_gain, m_mla_k_gain, m_mla_w_out, m_xmem_norm, m_xmem_mem_norm, m_xmem_wq, m_xmem_wkv, m_xmem_q_gain, m_xmem_k_gain, m_xmem_wo, m_ffn_post_norm, m_ffn_post_w_gu, m_ffn_post_w_down, v_ffn_pre_norm, v_ffn_pre_w_gu, v_ffn_pre_w_down, v_mix_norm, v_sbg_w_in, v_sgu_ln_gain, v_sgu_ln_bias, v_sgu_w, v_sgu_b, v_sbg_w_out, v_mla_w_in, v_mla_q_lora_gain, v_mla_kv_lora_gain, v_mla_w_uq, v_mla_w_ukv, v_mla_q_gain, v_mla_k_gain, v_mla_w_out, v_xmem_norm, v_xmem_mem_norm, v_xmem_wq, v_xmem_wkv, v_xmem_q_gain, v_xmem_k_gain, v_xmem_wo, v_ffn_post_norm, v_ffn_post_w_gu, v_ffn_post_w_down):
    args = locals()
    w_in = {n: args[n] for n in WEIGHTS}
    m_in = {n: args["m_" + n] for n in WEIGHTS}
    v_in = {n: args["v_" + n] for n in WEIGHTS}
    slot = _device_slot()

    tiny = jnp.zeros((8, LANES), F32)
    for i, src in enumerate((w_in, m_in, v_in)):
        tiny = tiny.at[i, :64].set(src['mla_q_lora_gain'][0]).at[i + 3, :32].set(src['mla_kv_lora_gain'][0])
    tiny_all = comm_call('gather', [tiny], name="gather_lora_gains")[0]
    full_small = []
    for i, src in enumerate((w_in, m_in, v_in)):
        d = {n: src[n] for n in SMALL}
        d['mla_q_lora_gain'] = tiny_all[:, i, :64].reshape(1, MLA_Q_LORA)
        d['mla_kv_lora_gain'] = tiny_all[:, i + 3, :32].reshape(1, MLA_KV_LORA)
        full_small.append(d)
    w_small, m_small, v_small = full_small
    small_shapes = {n: w_small[n].shape for n in SMALL}

    shards = {(n, layer): w_in[n][layer].astype(BF16) for n in BIG for layer in range(w_in[n].shape[0])}
    loss, dx, grads, landed = local_step(x[0], mem[0], positions[0], loss_target[0], w_small, shards)
    loss = lax.psum(loss, ("x", "y", "c"))
    big_out = {n: adamw([landed[(n, layer)] for layer in range(w_in[n].shape[0])], w_in[n], m_in[n], v_in[n],
                        name=f"adamw_{n}") for n in BIG}

    small_parts = comm_call('gather', [pack_small(grads, small_shapes)], name="gather_small_grads")
    small_out = adamw(small_parts, pack_small(w_small, small_shapes)[None], pack_small(m_small, small_shapes)[None],
                      pack_small(v_small, small_shapes)[None], name="adamw_small")
    small_out = [unpack_small(t[0], small_shapes) for t in small_out]
    for d in small_out:
        for n, width in zip(GAIN_SHARDED, (64, 32)):
            d[n] = lax.dynamic_slice(d[n], (0, slot * width), (1, width))

    outs = [loss, dx[None]]
    for kind, small_d in enumerate(small_out):
        outs += [big_out[n][kind] if n in BIG else small_d[n] for n in WEIGHTS]
    return tuple(outs)
```
